```python
import math
import jax
import jax.numpy as jnp
from jax import lax
import numpy as np

D_MODEL = 1024
BATCH = 8
SEQ = 2048
DEPTH = 2

D_MIX = D_MODEL
GROUP_W = D_MIX // 4
HEAD_DIM = 64
EPS = 1e-6
NEG_INF = -1e30

MLA_HEADS = GROUP_W // HEAD_DIM
MLA_NOPE = 64
MLA_ROPE = 32
MLA_V = GROUP_W // MLA_HEADS
MLA_Q_RANK = GROUP_W
MLA_KV_RANK = GROUP_W // 2
ROPE_THETA = 10000.0
Q_BLOCK = 128

S5_GROUP_CH = 16
S5_GROUPS = GROUP_W // S5_GROUP_CH
S5_STATE = 64
S5_DT_MIN = 1e-3
S5_DT_MAX = 1e-1

DIL_HEADS = GROUP_W // HEAD_DIM
DIL_PAIRS = ((128, 1), (512, 4), (2048, 16))
T5_BUCKETS = 32
T5_MAX_DIST = 2048

DN_HEADS = GROUP_W // HEAD_DIM
DN_DK = HEAD_DIM
DN_DV = HEAD_DIM
DN_CONV = 4
DN_CHUNK = 64
DN_DT_MIN = 1e-3
DN_DT_MAX = 1e-1

FFN_HIDDEN = -(-8 * D_MODEL // (3 * 256)) * 256

IN_SPLITS = (MLA_Q_RANK, MLA_KV_RANK, MLA_ROPE, GROUP_W, 3 * GROUP_W, 3 * GROUP_W, DN_HEADS, DN_HEADS, GROUP_W)
IN_COLS = MLA_Q_RANK + MLA_KV_RANK + MLA_ROPE + 8 * GROUP_W + 2 * DN_HEADS

kernel_name = 'hybrid_parallel_mixer_trunk'


def rms_norm(x, g):
    xf = x.astype(jnp.float32)
    y = xf * lax.rsqrt(jnp.mean(xf * xf, axis=-1, keepdims=True) + EPS)
    return (y * g.astype(jnp.float32)).astype(x.dtype)


def l2_norm(x):
    return x * lax.rsqrt(jnp.sum(x * x, axis=-1, keepdims=True) + EPS)


def split_cols(t, sizes):
    out, start = [], 0
    for s in sizes:
        out.append(t[..., start:start + s])
        start += s
    return out


def apply_rope(x, pos):
    half = x.shape[-1] // 2
    freqs = ROPE_THETA ** (-jnp.arange(half, dtype=jnp.float32) / half)
    ang = pos[:, None] * freqs[None, :]
    cos = jnp.cos(ang)[None, :, None, :]
    sin = jnp.sin(ang)[None, :, None, :]
    xf = x.astype(jnp.float32)
    x1, x2 = xf[..., :half], xf[..., half:]
    return jnp.concatenate([x1 * cos - x2 * sin, x1 * sin + x2 * cos], axis=-1).astype(x.dtype)


def causal_block_attention(q, k, v, scale):
    B, S, H, Dq = q.shape
    nb = S // Q_BLOCK
    qb = q.reshape(B, nb, Q_BLOCK, H, Dq).transpose(1, 0, 2, 3, 4)
    starts = jnp.arange(nb, dtype=jnp.int32) * Q_BLOCK
    kpos = jnp.arange(S, dtype=jnp.int32)

    def block(args):
        q_blk, s0 = args
        logits = jnp.einsum('bqhd,bkhd->bhqk', q_blk, k).astype(jnp.float32) * scale
        qpos = s0 + jnp.arange(Q_BLOCK, dtype=jnp.int32)
        mask = kpos[None, :] <= qpos[:, None]
        logits = jnp.where(mask[None, None], logits, NEG_INF)
        p = jax.nn.softmax(logits, axis=-1).astype(v.dtype)
        return jnp.einsum('bhqk,bkhd->bqhd', p, v)

    out = lax.map(block, (qb, starts))
    return out.transpose(1, 0, 2, 3, 4).reshape(B, S, H, v.shape[-1])


def mla_mixer(c_q, c_kv, k_rope, q_norm, kv_norm, w_uq, w_ukv, qk_q, qk_k):
    B, S, _ = c_q.shape
    H = MLA_HEADS
    dqk = MLA_NOPE + MLA_ROPE
    q = (rms_norm(c_q, q_norm) @ w_uq).reshape(B, S, H, dqk)
    kv = (rms_norm(c_kv, kv_norm) @ w_ukv).reshape(B, S, H, MLA_NOPE + MLA_V)
    k_nope, v = kv[..., :MLA_NOPE], kv[..., MLA_NOPE:]
    k = jnp.concatenate([k_nope, jnp.broadcast_to(k_rope[:, :, None, :], (B, S, H, MLA_ROPE))], axis=-1)
    q = rms_norm(q, qk_q)
    k = rms_norm(k, qk_k)
    pos = jnp.arange(S, dtype=jnp.float32)
    q = jnp.concatenate([q[..., :MLA_NOPE], apply_rope(q[..., MLA_NOPE:], pos)], axis=-1)
    k = jnp.concatenate([k[..., :MLA_NOPE], apply_rope(k[..., MLA_NOPE:], pos)], axis=-1)
    out = causal_block_attention(q, k, v, dqk ** -0.5)
    return out.reshape(B, S, H * MLA_V)


def complex_affine_combine(e1, e2):
    a1r, a1i, b1r, b1i = e1
    a2r, a2i, b2r, b2i = e2
    ar = a1r * a2r - a1i * a2i
    ai = a1r * a2i + a1i * a2r
    br = a2r * b1r - a2i * b1i + b2r
    bi = a2r * b1i + a2i * b1r + b2i
    return ar, ai, br, bi


def s5_mixer(u, lam_re, lam_im, log_dt, b_re, b_im, c_re, c_im, d_skip, w_glu):
    B, S, W = u.shape
    G, CG = S5_GROUPS, S5_GROUP_CH
    f32 = jnp.float32
    uf = u.astype(f32).reshape(B, S, G, CG)
    lr, li = lam_re.astype(f32), lam_im.astype(f32)
    dt = jnp.exp(log_dt.astype(f32))[:, None]
    mag = jnp.exp(lr * dt)
    ar, ai = mag * jnp.cos(li * dt), mag * jnp.sin(li * dt)
    den = lr * lr + li * li
    nr, ni = ar - 1.0, ai
    zr = (nr * lr + ni * li) / den
    zi = (ni * lr - nr * li) / den
    br, bi = b_re.astype(f32), b_im.astype(f32)
    bbr = zr[..., None] * br - zi[..., None] * bi
    bbi = zr[..., None] * bi + zi[..., None] * br
    xr = jnp.einsum('gpc,bsgc->bsgp', bbr, uf)
    xi = jnp.einsum('gpc,bsgc->bsgp', bbi, uf)
    ar_f = jnp.broadcast_to(ar, xr.shape)
    ai_f = jnp.broadcast_to(ai, xr.shape)
    _, _, hr, hi = lax.associative_scan(complex_affine_combine, (ar_f, ai_f, xr, xi), axis=1)
    y = jnp.einsum('gcp,bsgp->bsgc', c_re.astype(f32), hr) - jnp.einsum('gcp,bsgp->bsgc', c_im.astype(f32), hi)
    y = y.reshape(B, S, W) + d_skip.astype(f32) * u.astype(f32)
    y = y.astype(u.dtype)
    val, gate = jnp.split(y @ w_glu, 2, axis=-1)
    return val * jax.nn.sigmoid(gate)


def t5_bucket(dist):
    exact = T5_BUCKETS // 2
    df = jnp.maximum(dist, 1).astype(jnp.float32)
    large = exact + (jnp.log(df / exact) / math.log(T5_MAX_DIST / exact) * (T5_BUCKETS - exact)).astype(jnp.int32)
    large = jnp.minimum(large, T5_BUCKETS - 1)
    return jnp.where(dist < exact, dist, large)


def dilated_branch(q, k, v, bias_table, window, dilation, scale):
    B, S, H, D = q.shape
    span = window // dilation
    L = S // dilation
    nb = -(-L // span)
    Lp = nb * span

    def to_blocks(t):
        t = t.reshape(B, L, dilation, H, D).transpose(0, 2, 1, 3, 4)
        t = jnp.pad(t, ((0, 0), (0, 0), (0, Lp - L), (0, 0), (0, 0)))
        return t.reshape(B, dilation, nb, span, H, D)

    def with_prev(t):
        prev = jnp.pad(t, ((0, 0), (0, 0), (1, 0), (0, 0), (0, 0), (0, 0)))[:, :, :-1]
        return jnp.concatenate([prev, t], axis=3)

    qb = to_blocks(q)
    kk = with_prev(to_blocks(k))
    vv = with_prev(to_blocks(v))
    qi = jnp.arange(span, dtype=jnp.int32)[:, None] + span
    kj = jnp.arange(2 * span, dtype=jnp.int32)[None, :]
    delta = qi - kj
    band = (delta >= 0) & (delta <= span)
    before_start = (jnp.arange(nb)[:, None, None] == 0) & (kj < span)[None]
    valid = band[None] & (~before_start)
    bias = bias_table[t5_bucket(jnp.clip(delta, 0, span) * dilation)]
    bias = bias.transpose(2, 0, 1).astype(jnp.float32)
    logits = jnp.einsum('bgnqhd,bgnkhd->bgnhqk', qb, kk).astype(jnp.float32) * scale + bias
    logits = jnp.where(valid[None, None, :, None], logits, NEG_INF)
    m = jnp.max(logits, axis=-1)
    p = jnp.exp(logits - m[..., None])
    l = jnp.sum(p, axis=-1)
    o = jnp.einsum('bgnhqk,bgnkhd->bgnqhd', p.astype(v.dtype), vv).astype(jnp.float32)

    def from_blocks(t):
        t = t.reshape(B, dilation, Lp, *t.shape[4:])[:, :, :L]
        t = jnp.moveaxis(t, 1, 2)
        return t.reshape(B, S, *t.shape[3:])

    return from_blocks(o), from_blocks(jnp.swapaxes(m, -1, -2)), from_blocks(jnp.swapaxes(l, -1, -2))


def dilated_mixer(qkv, q_norm, k_norm, bias_table):
    B, S, _ = qkv.shape
    q, k, v = [t.reshape(B, S, DIL_HEADS, HEAD_DIM) for t in jnp.split(qkv, 3, axis=-1)]
    q = rms_norm(q, q_norm)
    k = rms_norm(k, k_norm)
    branches = [dilated_branch(q, k, v, bias_table, w, d, HEAD_DIM ** -0.5) for (w, d) in DIL_PAIRS]
    m_all = jnp.stack([br[1] for br in branches])
    l_all = jnp.stack([br[2] for br in branches])
    o_all = jnp.stack([br[0] for br in branches])
    wts = jnp.exp(m_all - jnp.max(m_all, axis=0, keepdims=True))
    num = jnp.sum(wts[..., None] * o_all, axis=0)
    den = jnp.sum(wts * l_all, axis=0)
    return (num / den[..., None]).astype(qkv.dtype).reshape(B, S, DIL_HEADS * HEAD_DIM)


def causal_depthwise_conv(x, w):
    K, C = w.shape
    return lax.conv_general_dilated(x, w[:, None, :].astype(x.dtype), window_strides=(1,),
                                    padding=[(K - 1, 0)], dimension_numbers=('NWC', 'WIO', 'NWC'),
                                    feature_group_count=C)


def chunked_gated_delta(q, k, v, g, beta):
    B, S, H, DK = q.shape
    DV = v.shape[-1]
    C = DN_CHUNK
    N = S // C

    def chunks(t):
        return jnp.moveaxis(t.reshape(B, N, C, H, *t.shape[3:]), 3, 1)

    q, k, v, g, beta = chunks(q), chunks(k), chunks(v), chunks(g), chunks(beta)
    gc = jnp.cumsum(g, axis=-1)
    causal = jnp.tril(jnp.ones((C, C), dtype=bool))
    strict = jnp.tril(jnp.ones((C, C), dtype=bool), -1)
    decay = jnp.exp(jnp.where(causal, gc[..., :, None] - gc[..., None, :], NEG_INF))
    kb = k * beta[..., None]
    lmat = jnp.where(strict, jnp.einsum('bhnid,bhnjd->bhnij', kb, k) * decay, 0.0)
    eye = jnp.eye(C, dtype=jnp.float32)
    rhs = jnp.concatenate([kb * jnp.exp(gc)[..., None], v * beta[..., None]], axis=-1)
    wu = lax.linalg.triangular_solve(eye + lmat, rhs, left_side=True, lower=True, unit_diagonal=True)
    w_c, u_c = wu[..., :DK], wu[..., DK:]
    a_qk = jnp.where(causal, jnp.einsum('bhnid,bhnjd->bhnij', q, k) * decay, 0.0)
    q_dec = q * jnp.exp(gc)[..., None]
    g_last = gc[..., -1]
    k_dec = k * jnp.exp(g_last[..., None] - gc)[..., None]
    xs = tuple(jnp.moveaxis(t, 2, 0) for t in (w_c, u_c, q_dec, a_qk, k_dec, jnp.exp(g_last)))

    def step(state, inp):
        w_i, u_i, q_i, a_i, k_i, d_i = inp
        v_new = u_i - jnp.einsum('bhck,bhkv->bhcv', w_i, state)
        o_i = jnp.einsum('bhck,bhkv->bhcv', q_i, state) + jnp.einsum('bhij,bhjv->bhiv', a_i, v_new)
        state = state * d_i[..., None, None] + jnp.einsum('bhck,bhcv->bhkv', k_i, v_new)
        return state, o_i

    s0 = jnp.zeros((B, H, DK, DV), jnp.float32)
    _, o = lax.scan(step, s0, xs)
    o = jnp.moveaxis(o, 0, 2)
    return jnp.moveaxis(o, 1, 3).reshape(B, S, H, DV)


def gated_delta_mixer(qkv, a, b, gate, conv_w, a_log, dt_bias, o_norm):
    B, S, _ = qkv.shape
    H = DN_HEADS
    f32 = jnp.float32
    qkv_c = jax.nn.silu(causal_depthwise_conv(qkv, conv_w))
    q, k, v = jnp.split(qkv_c, 3, axis=-1)
    q = l2_norm(q.reshape(B, S, H, DN_DK).astype(f32)) * (DN_DK ** -0.5)
    k = l2_norm(k.reshape(B, S, H, DN_DK).astype(f32))
    v = v.reshape(B, S, H, DN_DV).astype(f32)
    beta = jax.nn.sigmoid(b.astype(f32))
    g = -jnp.exp(a_log.astype(f32)) * jax.nn.softplus(a.astype(f32) + dt_bias.astype(f32))
    o = chunked_gated_delta(q, k, v, g, beta)
    o = rms_norm(o, o_norm) * jax.nn.silu(gate.astype(f32).reshape(B, S, H, DN_DV))
    return o.reshape(B, S, H * DN_DV).astype(qkv.dtype)


def _fwd_setup_inputs(seed: int = 0) -> dict:
    key = jax.random.key(seed)
    ks = iter(jax.random.split(key, 32))
    f32 = jnp.float32
    L = DEPTH

    def nrm(shape, scale):
        return jax.random.normal(next(ks), shape, f32) * scale

    def gain(shape):
        return 1.0 + nrm(shape, 0.02)

    def unif(shape, lo, hi):
        return jax.random.uniform(next(ks), shape, f32, lo, hi)

    G, P, CG = S5_GROUPS, S5_STATE, S5_GROUP_CH
    x = nrm((BATCH, SEQ, D_MODEL), 1.0)
    attn_norm = gain((L, D_MODEL))
    w_in = nrm((L, D_MODEL, IN_COLS), D_MODEL ** -0.5)
    w_out = nrm((L, D_MIX, D_MODEL), D_MIX ** -0.5)
    mla_q_norm = gain((L, MLA_Q_RANK))
    mla_kv_norm = gain((L, MLA_KV_RANK))
    mla_w_uq = nrm((L, MLA_Q_RANK, MLA_HEADS * (MLA_NOPE + MLA_ROPE)), MLA_Q_RANK ** -0.5)
    mla_w_ukv = nrm((L, MLA_KV_RANK, MLA_HEADS * (MLA_NOPE + MLA_V)), MLA_KV_RANK ** -0.5)
    mla_qk_q = gain((L, MLA_NOPE + MLA_ROPE))
    mla_qk_k = gain((L, MLA_NOPE + MLA_ROPE))
    s5_lambda_re = -0.5 * (1.0 + nrm((L, G, P), 0.02))
    s5_lambda_im = jnp.pi * jnp.arange(P, dtype=f32)[None, None, :] + nrm((L, G, P), 0.01)
    s5_log_dt = unif((L, G), math.log(S5_DT_MIN), math.log(S5_DT_MAX))
    s5_b_re = nrm((L, G, P, CG), (2 * CG) ** -0.5)
    s5_b_im = nrm((L, G, P, CG), (2 * CG) ** -0.5)
    s5_c_re = nrm((L, G, CG, P), P ** -0.5)
    s5_c_im = nrm((L, G, CG, P), P ** -0.5)
    s5_d = nrm((L, GROUP_W), 1.0)
    s5_w_glu = nrm((L, GROUP_W, 2 * GROUP_W), GROUP_W ** -0.5)
    dil_q_norm = gain((L, HEAD_DIM))
    dil_k_norm = gain((L, HEAD_DIM))
    t5_bias = nrm((T5_BUCKETS, DIL_HEADS), 0.2)
    dn_conv = nrm((L, DN_CONV, 3 * GROUP_W), DN_CONV ** -0.5)
    dn_a_log = jnp.log(unif((L, DN_HEADS), 1.0, 16.0))
    dt = jnp.exp(unif((L, DN_HEADS), math.log(DN_DT_MIN), math.log(DN_DT_MAX)))
    dn_dt_bias = dt + jnp.log(-jnp.expm1(-dt))
    dn_o_norm = gain((L, DN_DV))
    ffn_norm = gain((L, D_MODEL))
    ffn_w1 = nrm((L, D_MODEL, FFN_HIDDEN), D_MODEL ** -0.5)
    ffn_w3 = nrm((L, D_MODEL, FFN_HIDDEN), D_MODEL ** -0.5)
    ffn_w2 = nrm((L, FFN_HIDDEN, D_MODEL), FFN_HIDDEN ** -0.5)
    return {'x': x, 'attn_norm': attn_norm, 'w_in': w_in, 'w_out': w_out,
            'mla_q_norm': mla_q_norm, 'mla_kv_norm': mla_kv_norm, 'mla_w_uq': mla_w_uq,
            'mla_w_ukv': mla_w_ukv, 'mla_qk_q': mla_qk_q, 'mla_qk_k': mla_qk_k,
            's5_lambda_re': s5_lambda_re, 's5_lambda_im': s5_lambda_im, 's5_log_dt': s5_log_dt,
            's5_b_re': s5_b_re, 's5_b_im': s5_b_im, 's5_c_re': s5_c_re, 's5_c_im': s5_c_im,
            's5_d': s5_d, 's5_w_glu': s5_w_glu, 'dil_q_norm': dil_q_norm, 'dil_k_norm': dil_k_norm,
            't5_bias': t5_bias, 'dn_conv': dn_conv, 'dn_a_log': dn_a_log, 'dn_dt_bias': dn_dt_bias,
            'dn_o_norm': dn_o_norm, 'ffn_norm': ffn_norm, 'ffn_w1': ffn_w1, 'ffn_w3': ffn_w3,
            'ffn_w2': ffn_w2}


def _fwd_reference(x, attn_norm, w_in, w_out, mla_q_norm, mla_kv_norm, mla_w_uq, mla_w_ukv, mla_qk_q, mla_qk_k,
              s5_lambda_re, s5_lambda_im, s5_log_dt, s5_b_re, s5_b_im, s5_c_re, s5_c_im, s5_d, s5_w_glu,
              dil_q_norm, dil_k_norm, t5_bias, dn_conv, dn_a_log, dn_dt_bias, dn_o_norm,
              ffn_norm, ffn_w1, ffn_w3, ffn_w2):
    h = x
    for l in range(DEPTH):
        n = rms_norm(h, attn_norm[l])
        proj = n @ w_in[l]
        c_q, c_kv, k_rope, u_s5, qkv_dil, qkv_dn, a_dn, b_dn, gate_dn = split_cols(proj, IN_SPLITS)
        y_mla = mla_mixer(c_q, c_kv, k_rope, mla_q_norm[l], mla_kv_norm[l], mla_w_uq[l], mla_w_ukv[l],
                          mla_qk_q[l], mla_qk_k[l])
        y_s5 = s5_mixer(u_s5, s5_lambda_re[l], s5_lambda_im[l], s5_log_dt[l], s5_b_re[l], s5_b_im[l],
                        s5_c_re[l], s5_c_im[l], s5_d[l], s5_w_glu[l])
        y_dil = dilated_mixer(qkv_dil, dil_q_norm[l], dil_k_norm[l], t5_bias)
        y_dn = gated_delta_mixer(qkv_dn, a_dn, b_dn, gate_dn, dn_conv[l], dn_a_log[l], dn_dt_bias[l],
                                 dn_o_norm[l])
        mixed = jnp.concatenate([y_mla, y_s5, y_dil, y_dn], axis=-1)
        h = h + mixed @ w_out[l]
        n = rms_norm(h, ffn_norm[l])
        h = h + (jax.nn.silu(n @ ffn_w1[l]) * (n @ ffn_w3[l])) @ ffn_w2[l]
    return h


import jax as _jax
import jax.numpy as _jnp

TWIN_FORMAT = 'train_step'
FWD_PARAMS = ['x', 'attn_norm', 'w_in', 'w_out', 'mla_q_norm', 'mla_kv_norm', 'mla_w_uq', 'mla_w_ukv', 'mla_qk_q', 'mla_qk_k', 's5_lambda_re', 's5_lambda_im', 's5_log_dt', 's5_b_re', 's5_b_im', 's5_c_re', 's5_c_im', 's5_d', 's5_w_glu', 'dil_q_norm', 'dil_k_norm', 't5_bias', 'dn_conv', 'dn_a_log', 'dn_dt_bias', 'dn_o_norm', 'ffn_norm', 'ffn_w1', 'ffn_w3', 'ffn_w2']
TWIN_WEIGHTS = ['attn_norm', 'w_in', 'w_out', 'mla_q_norm', 'mla_kv_norm', 'mla_w_uq', 'mla_w_ukv', 'mla_qk_q', 'mla_qk_k', 's5_lambda_re', 's5_lambda_im', 's5_log_dt', 's5_b_re', 's5_b_im', 's5_c_re', 's5_c_im', 's5_d', 's5_w_glu', 'dil_q_norm', 'dil_k_norm', 't5_bias', 'dn_conv', 'dn_a_log', 'dn_dt_bias', 'dn_o_norm', 'ffn_norm', 'ffn_w1', 'ffn_w3', 'ffn_w2']
TWIN_DIFF_INPUT = 'x'
TWIN_INPUTS = ['x', 'attn_norm', 'w_in', 'w_out', 'mla_q_norm', 'mla_kv_norm', 'mla_w_uq', 'mla_w_ukv', 'mla_qk_q', 'mla_qk_k', 's5_lambda_re', 's5_lambda_im', 's5_log_dt', 's5_b_re', 's5_b_im', 's5_c_re', 's5_c_im', 's5_d', 's5_w_glu', 'dil_q_norm', 'dil_k_norm', 't5_bias', 'dn_conv', 'dn_a_log', 'dn_dt_bias', 'dn_o_norm', 'ffn_norm', 'ffn_w1', 'ffn_w3', 'ffn_w2', 'loss_target', 'm_attn_norm', 'm_w_in', 'm_w_out', 'm_mla_q_norm', 'm_mla_kv_norm', 'm_mla_w_uq', 'm_mla_w_ukv', 'm_mla_qk_q', 'm_mla_qk_k', 'm_s5_lambda_re', 'm_s5_lambda_im', 'm_s5_log_dt', 'm_s5_b_re', 'm_s5_b_im', 'm_s5_c_re', 'm_s5_c_im', 'm_s5_d', 'm_s5_w_glu', 'm_dil_q_norm', 'm_dil_k_norm', 'm_t5_bias', 'm_dn_conv', 'm_dn_a_log', 'm_dn_dt_bias', 'm_dn_o_norm', 'm_ffn_norm', 'm_ffn_w1', 'm_ffn_w3', 'm_ffn_w2', 'v_attn_norm', 'v_w_in', 'v_w_out', 'v_mla_q_norm', 'v_mla_kv_norm', 'v_mla_w_uq', 'v_mla_w_ukv', 'v_mla_qk_q', 'v_mla_qk_k', 'v_s5_lambda_re', 'v_s5_lambda_im', 'v_s5_log_dt', 'v_s5_b_re', 'v_s5_b_im', 'v_s5_c_re', 'v_s5_c_im', 'v_s5_d', 'v_s5_w_glu', 'v_dil_q_norm', 'v_dil_k_norm', 'v_t5_bias', 'v_dn_conv', 'v_dn_a_log', 'v_dn_dt_bias', 'v_dn_o_norm', 'v_ffn_norm', 'v_ffn_w1', 'v_ffn_w3', 'v_ffn_w2']
TWIN_OUTPUTS = ['loss', 'grad_x', 'grad_attn_norm', 'grad_w_in', 'grad_w_out', 'grad_mla_q_norm', 'grad_mla_kv_norm', 'grad_mla_w_uq', 'grad_mla_w_ukv', 'grad_mla_qk_q', 'grad_mla_qk_k', 'grad_s5_lambda_re', 'grad_s5_lambda_im', 'grad_s5_log_dt', 'grad_s5_b_re', 'grad_s5_b_im', 'grad_s5_c_re', 'grad_s5_c_im', 'grad_s5_d', 'grad_s5_w_glu', 'grad_dil_q_norm', 'grad_dil_k_norm', 'grad_t5_bias', 'grad_dn_conv', 'grad_dn_a_log', 'grad_dn_dt_bias', 'grad_dn_o_norm', 'grad_ffn_norm', 'grad_ffn_w1', 'grad_ffn_w3', 'grad_ffn_w2', 'delta_attn_norm', 'delta_w_in', 'delta_w_out', 'delta_mla_q_norm', 'delta_mla_kv_norm', 'delta_mla_w_uq', 'delta_mla_w_ukv', 'delta_mla_qk_q', 'delta_mla_qk_k', 'delta_s5_lambda_re', 'delta_s5_lambda_im', 'delta_s5_log_dt', 'delta_s5_b_re', 'delta_s5_b_im', 'delta_s5_c_re', 'delta_s5_c_im', 'delta_s5_d', 'delta_s5_w_glu', 'delta_dil_q_norm', 'delta_dil_k_norm', 'delta_t5_bias', 'delta_dn_conv', 'delta_dn_a_log', 'delta_dn_dt_bias', 'delta_dn_o_norm', 'delta_ffn_norm', 'delta_ffn_w1', 'delta_ffn_w3', 'delta_ffn_w2', 'new_m_attn_norm', 'new_m_w_in', 'new_m_w_out', 'new_m_mla_q_norm', 'new_m_mla_kv_norm', 'new_m_mla_w_uq', 'new_m_mla_w_ukv', 'new_m_mla_qk_q', 'new_m_mla_qk_k', 'new_m_s5_lambda_re', 'new_m_s5_lambda_im', 'new_m_s5_log_dt', 'new_m_s5_b_re', 'new_m_s5_b_im', 'new_m_s5_c_re', 'new_m_s5_c_im', 'new_m_s5_d', 'new_m_s5_w_glu', 'new_m_dil_q_norm', 'new_m_dil_k_norm', 'new_m_t5_bias', 'new_m_dn_conv', 'new_m_dn_a_log', 'new_m_dn_dt_bias', 'new_m_dn_o_norm', 'new_m_ffn_norm', 'new_m_ffn_w1', 'new_m_ffn_w3', 'new_m_ffn_w2', 'new_v_attn_norm', 'new_v_w_in', 'new_v_w_out', 'new_v_mla_q_norm', 'new_v_mla_kv_norm', 'new_v_mla_w_uq', 'new_v_mla_w_ukv', 'new_v_mla_qk_q', 'new_v_mla_qk_k', 'new_v_s5_lambda_re', 'new_v_s5_lambda_im', 'new_v_s5_log_dt', 'new_v_s5_b_re', 'new_v_s5_b_im', 'new_v_s5_c_re', 'new_v_s5_c_im', 'new_v_s5_d', 'new_v_s5_w_glu', 'new_v_dil_q_norm', 'new_v_dil_k_norm', 'new_v_t5_bias', 'new_v_dn_conv', 'new_v_dn_a_log', 'new_v_dn_dt_bias', 'new_v_dn_o_norm', 'new_v_ffn_norm', 'new_v_ffn_w1', 'new_v_ffn_w3', 'new_v_ffn_w2']
TWIN_LEAF_KINDS = {'loss': 'loss', 'grad_x': 'grad_x', 'grad_attn_norm': 'grad_w', 'grad_w_in': 'grad_w', 'grad_w_out': 'grad_w', 'grad_mla_q_norm': 'grad_w', 'grad_mla_kv_norm': 'grad_w', 'grad_mla_w_uq': 'grad_w', 'grad_mla_w_ukv': 'grad_w', 'grad_mla_qk_q': 'grad_w', 'grad_mla_qk_k': 'grad_w', 'grad_s5_lambda_re': 'grad_w', 'grad_s5_lambda_im': 'grad_w', 'grad_s5_log_dt': 'grad_w', 'grad_s5_b_re': 'grad_w', 'grad_s5_b_im': 'grad_w', 'grad_s5_c_re': 'grad_w', 'grad_s5_c_im': 'grad_w', 'grad_s5_d': 'grad_w', 'grad_s5_w_glu': 'grad_w', 'grad_dil_q_norm': 'grad_w', 'grad_dil_k_norm': 'grad_w', 'grad_t5_bias': 'grad_w', 'grad_dn_conv': 'grad_w', 'grad_dn_a_log': 'grad_w', 'grad_dn_dt_bias': 'grad_w', 'grad_dn_o_norm': 'grad_w', 'grad_ffn_norm': 'grad_w', 'grad_ffn_w1': 'grad_w', 'grad_ffn_w3': 'grad_w', 'grad_ffn_w2': 'grad_w', 'delta_attn_norm': 'delta_w', 'delta_w_in': 'delta_w', 'delta_w_out': 'delta_w', 'delta_mla_q_norm': 'delta_w', 'delta_mla_kv_norm': 'delta_w', 'delta_mla_w_uq': 'delta_w', 'delta_mla_w_ukv': 'delta_w', 'delta_mla_qk_q': 'delta_w', 'delta_mla_qk_k': 'delta_w', 'delta_s5_lambda_re': 'delta_w', 'delta_s5_lambda_im': 'delta_w', 'delta_s5_log_dt': 'delta_w', 'delta_s5_b_re': 'delta_w', 'delta_s5_b_im': 'delta_w', 'delta_s5_c_re': 'delta_w', 'delta_s5_c_im': 'delta_w', 'delta_s5_d': 'delta_w', 'delta_s5_w_glu': 'delta_w', 'delta_dil_q_norm': 'delta_w', 'delta_dil_k_norm': 'delta_w', 'delta_t5_bias': 'delta_w', 'delta_dn_conv': 'delta_w', 'delta_dn_a_log': 'delta_w', 'delta_dn_dt_bias': 'delta_w', 'delta_dn_o_norm': 'delta_w', 'delta_ffn_norm': 'delta_w', 'delta_ffn_w1': 'delta_w', 'delta_ffn_w3': 'delta_w', 'delta_ffn_w2': 'delta_w', 'new_m_attn_norm': 'new_m', 'new_m_w_in': 'new_m', 'new_m_w_out': 'new_m', 'new_m_mla_q_norm': 'new_m', 'new_m_mla_kv_norm': 'new_m', 'new_m_mla_w_uq': 'new_m', 'new_m_mla_w_ukv': 'new_m', 'new_m_mla_qk_q': 'new_m', 'new_m_mla_qk_k': 'new_m', 'new_m_s5_lambda_re': 'new_m', 'new_m_s5_lambda_im': 'new_m', 'new_m_s5_log_dt': 'new_m', 'new_m_s5_b_re': 'new_m', 'new_m_s5_b_im': 'new_m', 'new_m_s5_c_re': 'new_m', 'new_m_s5_c_im': 'new_m', 'new_m_s5_d': 'new_m', 'new_m_s5_w_glu': 'new_m', 'new_m_dil_q_norm': 'new_m', 'new_m_dil_k_norm': 'new_m', 'new_m_t5_bias': 'new_m', 'new_m_dn_conv': 'new_m', 'new_m_dn_a_log': 'new_m', 'new_m_dn_dt_bias': 'new_m', 'new_m_dn_o_norm': 'new_m', 'new_m_ffn_norm': 'new_m', 'new_m_ffn_w1': 'new_m', 'new_m_ffn_w3': 'new_m', 'new_m_ffn_w2': 'new_m', 'new_v_attn_norm': 'new_v', 'new_v_w_in': 'new_v', 'new_v_w_out': 'new_v', 'new_v_mla_q_norm': 'new_v', 'new_v_mla_kv_norm': 'new_v', 'new_v_mla_w_uq': 'new_v', 'new_v_mla_w_ukv': 'new_v', 'new_v_mla_qk_q': 'new_v', 'new_v_mla_qk_k': 'new_v', 'new_v_s5_lambda_re': 'new_v', 'new_v_s5_lambda_im': 'new_v', 'new_v_s5_log_dt': 'new_v', 'new_v_s5_b_re': 'new_v', 'new_v_s5_b_im': 'new_v', 'new_v_s5_c_re': 'new_v', 'new_v_s5_c_im': 'new_v', 'new_v_s5_d': 'new_v', 'new_v_s5_w_glu': 'new_v', 'new_v_dil_q_norm': 'new_v', 'new_v_dil_k_norm': 'new_v', 'new_v_t5_bias': 'new_v', 'new_v_dn_conv': 'new_v', 'new_v_dn_a_log': 'new_v', 'new_v_dn_dt_bias': 'new_v', 'new_v_dn_o_norm': 'new_v', 'new_v_ffn_norm': 'new_v', 'new_v_ffn_w1': 'new_v', 'new_v_ffn_w3': 'new_v', 'new_v_ffn_w2': 'new_v'}


def _forward(args):
    return _fwd_reference(*[args[k] for k in FWD_PARAMS])


def _output_shape():
    out = _jax.eval_shape(lambda: _forward(_fwd_setup_inputs(0)))
    return out.shape, out.dtype

N_MICROBATCH = 1
ADAM_LR = 0.001
ADAM_B1 = 0.9
ADAM_B2 = 0.999
ADAM_EPS = 1e-08
ADAM_WD = 0.01
ADAM_STEP = 10
PER_EXAMPLE_BATCH_AXIS = {'x': 0, 'loss_target': 0}
SHARED_INPUTS = []
_WEIGHT_DTYPES = {'attn_norm': _jnp.float32, 'w_in': _jnp.float32, 'w_out': _jnp.float32, 'mla_q_norm': _jnp.float32, 'mla_kv_norm': _jnp.float32, 'mla_w_uq': _jnp.float32, 'mla_w_ukv': _jnp.float32, 'mla_qk_q': _jnp.float32, 'mla_qk_k': _jnp.float32, 's5_lambda_re': _jnp.float32, 's5_lambda_im': _jnp.float32, 's5_log_dt': _jnp.float32, 's5_b_re': _jnp.float32, 's5_b_im': _jnp.float32, 's5_c_re': _jnp.float32, 's5_c_im': _jnp.float32, 's5_d': _jnp.float32, 's5_w_glu': _jnp.float32, 'dil_q_norm': _jnp.float32, 'dil_k_norm': _jnp.float32, 't5_bias': _jnp.float32, 'dn_conv': _jnp.float32, 'dn_a_log': _jnp.float32, 'dn_dt_bias': _jnp.float32, 'dn_o_norm': _jnp.float32, 'ffn_norm': _jnp.float32, 'ffn_w1': _jnp.float32, 'ffn_w3': _jnp.float32, 'ffn_w2': _jnp.float32}
MOMENT_SCALE = {'attn_norm': 2.673701e+00, 'w_in': 2.195433e-01, 'w_out': 2.744452e-01, 'mla_q_norm': 7.594965e-02, 'mla_kv_norm': 5.064311e-01, 'mla_w_uq': 6.099623e-02, 'mla_w_ukv': 9.814622e-02, 'mla_qk_q': 3.279316e-01, 'mla_qk_k': 3.293681e-01, 's5_lambda_re': 2.769949e-02, 's5_lambda_im': 1.560275e-02, 's5_log_dt': 1.025289e+01, 's5_b_re': 1.425265e-02, 's5_b_im': 1.437693e-02, 's5_c_re': 2.047119e-02, 's5_c_im': 2.032572e-02, 's5_d': 4.679761e+00, 's5_w_glu': 3.512128e-01, 'dil_q_norm': 6.579641e-01, 'dil_k_norm': 6.591003e-01, 't5_bias': 2.401446e-01, 'dn_conv': 2.985374e-01, 'dn_a_log': 6.133490e+00, 'dn_dt_bias': 5.874840e+00, 'dn_o_norm': 2.405082e+01, 'ffn_norm': 1.233276e+01, 'ffn_w1': 1.455715e-01, 'ffn_w3': 1.378156e-01, 'ffn_w2': 2.151223e-01}


def _to_microbatches(a, axis):
    t = _jnp.moveaxis(a, axis, 0)
    t = t.reshape((N_MICROBATCH, t.shape[0] // N_MICROBATCH) + t.shape[1:])
    return _jnp.moveaxis(t, 1, axis + 1)


def setup_inputs(seed: int = 0) -> dict:
    inp = _fwd_setup_inputs(seed)
    key = _jax.random.fold_in(_jax.random.key(seed), 7919)
    shape, _ = _output_shape()
    out = dict(inp)
    out["loss_target"] = _jax.random.normal(_jax.random.fold_in(key, 0), shape, _jnp.float32)
    for i, name in enumerate(TWIN_WEIGHTS):
        w = inp[name].astype(_jnp.float32)
        if MOMENT_SCALE is None:
            s = _jnp.sqrt(_jnp.mean(_jnp.square(w)) + 1e-30)
        else:
            s = MOMENT_SCALE[name]
        km, kv = _jax.random.split(_jax.random.fold_in(key, i + 1))
        out[name] = w
        out["m_" + name] = s * _jax.random.normal(km, w.shape, _jnp.float32)
        out["v_" + name] = (s * s) * _jax.random.uniform(kv, w.shape, _jnp.float32, 0.5, 1.5)
    if N_MICROBATCH > 1:
        for name, axis in PER_EXAMPLE_BATCH_AXIS.items():
            out[name] = _to_microbatches(out[name], axis)
    return {'x': out['x'], 'attn_norm': out['attn_norm'], 'w_in': out['w_in'], 'w_out': out['w_out'], 'mla_q_norm': out['mla_q_norm'], 'mla_kv_norm': out['mla_kv_norm'], 'mla_w_uq': out['mla_w_uq'], 'mla_w_ukv': out['mla_w_ukv'], 'mla_qk_q': out['mla_qk_q'], 'mla_qk_k': out['mla_qk_k'], 's5_lambda_re': out['s5_lambda_re'], 's5_lambda_im': out['s5_lambda_im'], 's5_log_dt': out['s5_log_dt'], 's5_b_re': out['s5_b_re'], 's5_b_im': out['s5_b_im'], 's5_c_re': out['s5_c_re'], 's5_c_im': out['s5_c_im'], 's5_d': out['s5_d'], 's5_w_glu': out['s5_w_glu'], 'dil_q_norm': out['dil_q_norm'], 'dil_k_norm': out['dil_k_norm'], 't5_bias': out['t5_bias'], 'dn_conv': out['dn_conv'], 'dn_a_log': out['dn_a_log'], 'dn_dt_bias': out['dn_dt_bias'], 'dn_o_norm': out['dn_o_norm'], 'ffn_norm': out['ffn_norm'], 'ffn_w1': out['ffn_w1'], 'ffn_w3': out['ffn_w3'], 'ffn_w2': out['ffn_w2'], 'loss_target': out['loss_target'], 'm_attn_norm': out['m_attn_norm'], 'm_w_in': out['m_w_in'], 'm_w_out': out['m_w_out'], 'm_mla_q_norm': out['m_mla_q_norm'], 'm_mla_kv_norm': out['m_mla_kv_norm'], 'm_mla_w_uq': out['m_mla_w_uq'], 'm_mla_w_ukv': out['m_mla_w_ukv'], 'm_mla_qk_q': out['m_mla_qk_q'], 'm_mla_qk_k': out['m_mla_qk_k'], 'm_s5_lambda_re': out['m_s5_lambda_re'], 'm_s5_lambda_im': out['m_s5_lambda_im'], 'm_s5_log_dt': out['m_s5_log_dt'], 'm_s5_b_re': out['m_s5_b_re'], 'm_s5_b_im': out['m_s5_b_im'], 'm_s5_c_re': out['m_s5_c_re'], 'm_s5_c_im': out['m_s5_c_im'], 'm_s5_d': out['m_s5_d'], 'm_s5_w_glu': out['m_s5_w_glu'], 'm_dil_q_norm': out['m_dil_q_norm'], 'm_dil_k_norm': out['m_dil_k_norm'], 'm_t5_bias': out['m_t5_bias'], 'm_dn_conv': out['m_dn_conv'], 'm_dn_a_log': out['m_dn_a_log'], 'm_dn_dt_bias': out['m_dn_dt_bias'], 'm_dn_o_norm': out['m_dn_o_norm'], 'm_ffn_norm': out['m_ffn_norm'], 'm_ffn_w1': out['m_ffn_w1'], 'm_ffn_w3': out['m_ffn_w3'], 'm_ffn_w2': out['m_ffn_w2'], 'v_attn_norm': out['v_attn_norm'], 'v_w_in': out['v_w_in'], 'v_w_out': out['v_w_out'], 'v_mla_q_norm': out['v_mla_q_norm'], 'v_mla_kv_norm': out['v_mla_kv_norm'], 'v_mla_w_uq': out['v_mla_w_uq'], 'v_mla_w_ukv': out['v_mla_w_ukv'], 'v_mla_qk_q': out['v_mla_qk_q'], 'v_mla_qk_k': out['v_mla_qk_k'], 'v_s5_lambda_re': out['v_s5_lambda_re'], 'v_s5_lambda_im': out['v_s5_lambda_im'], 'v_s5_log_dt': out['v_s5_log_dt'], 'v_s5_b_re': out['v_s5_b_re'], 'v_s5_b_im': out['v_s5_b_im'], 'v_s5_c_re': out['v_s5_c_re'], 'v_s5_c_im': out['v_s5_c_im'], 'v_s5_d': out['v_s5_d'], 'v_s5_w_glu': out['v_s5_w_glu'], 'v_dil_q_norm': out['v_dil_q_norm'], 'v_dil_k_norm': out['v_dil_k_norm'], 'v_t5_bias': out['v_t5_bias'], 'v_dn_conv': out['v_dn_conv'], 'v_dn_a_log': out['v_dn_a_log'], 'v_dn_dt_bias': out['v_dn_dt_bias'], 'v_dn_o_norm': out['v_dn_o_norm'], 'v_ffn_norm': out['v_ffn_norm'], 'v_ffn_w1': out['v_ffn_w1'], 'v_ffn_w3': out['v_ffn_w3'], 'v_ffn_w2': out['v_ffn_w2']}


def _loss(weights, diff, rest, loss_target):
    with _jax.named_scope("forward"):
        args = {**rest, TWIN_DIFF_INPUT: diff, **{k: w.astype(_WEIGHT_DTYPES[k]) for k, w in weights.items()}}
        y = _forward(args)
    with _jax.named_scope("loss_head"):
        err = _jnp.square(y.astype(_jnp.float32) - loss_target)
        return 0.5 * _jnp.sum(_jnp.mean(err, axis=-1)) if err.ndim else 0.5 * err


def _adamw(w, g, m, v):
    m = ADAM_B1 * m + (1.0 - ADAM_B1) * g
    v = ADAM_B2 * v + (1.0 - ADAM_B2) * _jnp.square(g)
    m_hat = m / (1.0 - ADAM_B1 ** ADAM_STEP)
    v_hat = v / (1.0 - ADAM_B2 ** ADAM_STEP)
    delta = -ADAM_LR * (m_hat / (_jnp.sqrt(v_hat) + ADAM_EPS) + ADAM_WD * w)
    return delta, m, v


def reference(x, attn_norm, w_in, w_out, mla_q_norm, mla_kv_norm, mla_w_uq, mla_w_ukv, mla_qk_q, mla_qk_k, s5_lambda_re, s5_lambda_im, s5_log_dt, s5_b_re, s5_b_im, s5_c_re, s5_c_im, s5_d, s5_w_glu, dil_q_norm, dil_k_norm, t5_bias, dn_conv, dn_a_log, dn_dt_bias, dn_o_norm, ffn_norm, ffn_w1, ffn_w3, ffn_w2, loss_target, m_attn_norm, m_w_in, m_w_out, m_mla_q_norm, m_mla_kv_norm, m_mla_w_uq, m_mla_w_ukv, m_mla_qk_q, m_mla_qk_k, m_s5_lambda_re, m_s5_lambda_im, m_s5_log_dt, m_s5_b_re, m_s5_b_im, m_s5_c_re, m_s5_c_im, m_s5_d, m_s5_w_glu, m_dil_q_norm, m_dil_k_norm, m_t5_bias, m_dn_conv, m_dn_a_log, m_dn_dt_bias, m_dn_o_norm, m_ffn_norm, m_ffn_w1, m_ffn_w3, m_ffn_w2, v_attn_norm, v_w_in, v_w_out, v_mla_q_norm, v_mla_kv_norm, v_mla_w_uq, v_mla_w_ukv, v_mla_qk_q, v_mla_qk_k, v_s5_lambda_re, v_s5_lambda_im, v_s5_log_dt, v_s5_b_re, v_s5_b_im, v_s5_c_re, v_s5_c_im, v_s5_d, v_s5_w_glu, v_dil_q_norm, v_dil_k_norm, v_t5_bias, v_dn_conv, v_dn_a_log, v_dn_dt_bias, v_dn_o_norm, v_ffn_norm, v_ffn_w1, v_ffn_w3, v_ffn_w2):
    given = dict(x=x, attn_norm=attn_norm, w_in=w_in, w_out=w_out, mla_q_norm=mla_q_norm, mla_kv_norm=mla_kv_norm, mla_w_uq=mla_w_uq, mla_w_ukv=mla_w_ukv, mla_qk_q=mla_qk_q, mla_qk_k=mla_qk_k, s5_lambda_re=s5_lambda_re, s5_lambda_im=s5_lambda_im, s5_log_dt=s5_log_dt, s5_b_re=s5_b_re, s5_b_im=s5_b_im, s5_c_re=s5_c_re, s5_c_im=s5_c_im, s5_d=s5_d, s5_w_glu=s5_w_glu, dil_q_norm=dil_q_norm, dil_k_norm=dil_k_norm, t5_bias=t5_bias, dn_conv=dn_conv, dn_a_log=dn_a_log, dn_dt_bias=dn_dt_bias, dn_o_norm=dn_o_norm, ffn_norm=ffn_norm, ffn_w1=ffn_w1, ffn_w3=ffn_w3, ffn_w2=ffn_w2, loss_target=loss_target, m_attn_norm=m_attn_norm, m_w_in=m_w_in, m_w_out=m_w_out, m_mla_q_norm=m_mla_q_norm, m_mla_kv_norm=m_mla_kv_norm, m_mla_w_uq=m_mla_w_uq, m_mla_w_ukv=m_mla_w_ukv, m_mla_qk_q=m_mla_qk_q, m_mla_qk_k=m_mla_qk_k, m_s5_lambda_re=m_s5_lambda_re, m_s5_lambda_im=m_s5_lambda_im, m_s5_log_dt=m_s5_log_dt, m_s5_b_re=m_s5_b_re, m_s5_b_im=m_s5_b_im, m_s5_c_re=m_s5_c_re, m_s5_c_im=m_s5_c_im, m_s5_d=m_s5_d, m_s5_w_glu=m_s5_w_glu, m_dil_q_norm=m_dil_q_norm, m_dil_k_norm=m_dil_k_norm, m_t5_bias=m_t5_bias, m_dn_conv=m_dn_conv, m_dn_a_log=m_dn_a_log, m_dn_dt_bias=m_dn_dt_bias, m_dn_o_norm=m_dn_o_norm, m_ffn_norm=m_ffn_norm, m_ffn_w1=m_ffn_w1, m_ffn_w3=m_ffn_w3, m_ffn_w2=m_ffn_w2, v_attn_norm=v_attn_norm, v_w_in=v_w_in, v_w_out=v_w_out, v_mla_q_norm=v_mla_q_norm, v_mla_kv_norm=v_mla_kv_norm, v_mla_w_uq=v_mla_w_uq, v_mla_w_ukv=v_mla_w_ukv, v_mla_qk_q=v_mla_qk_q, v_mla_qk_k=v_mla_qk_k, v_s5_lambda_re=v_s5_lambda_re, v_s5_lambda_im=v_s5_lambda_im, v_s5_log_dt=v_s5_log_dt, v_s5_b_re=v_s5_b_re, v_s5_b_im=v_s5_b_im, v_s5_c_re=v_s5_c_re, v_s5_c_im=v_s5_c_im, v_s5_d=v_s5_d, v_s5_w_glu=v_s5_w_glu, v_dil_q_norm=v_dil_q_norm, v_dil_k_norm=v_dil_k_norm, v_t5_bias=v_t5_bias, v_dn_conv=v_dn_conv, v_dn_a_log=v_dn_a_log, v_dn_dt_bias=v_dn_dt_bias, v_dn_o_norm=v_dn_o_norm, v_ffn_norm=v_ffn_norm, v_ffn_w1=v_ffn_w1, v_ffn_w3=v_ffn_w3, v_ffn_w2=v_ffn_w2)
    weights = {n: given[n] for n in TWIN_WEIGHTS}
    shared = {n: given[n] for n in SHARED_INPUTS}
    per_example = {n: given[n] for n in ['x']}
    grad_fn = _jax.value_and_grad(_loss, argnums=(0, 1))

    def one_microbatch(ex, loss_target):
        ex = dict(ex)
        diff = ex.pop(TWIN_DIFF_INPUT)
        return grad_fn(weights, diff, {**shared, **ex}, loss_target)

    if N_MICROBATCH == 1:
        loss, (grad_w, grad_x) = one_microbatch(per_example, given["loss_target"])
    else:
        def body(carry, xs):
            loss_sum, grad_sum = carry
            l_k, (gw_k, gx_k) = one_microbatch(xs[0], xs[1])
            with _jax.named_scope("update"):
                return (loss_sum + l_k, _jax.tree.map(_jnp.add, grad_sum, gw_k)), gx_k

        init = (_jnp.zeros((), _jnp.float32), _jax.tree.map(_jnp.zeros_like, weights))
        (loss, grad_w), grad_x = _jax.lax.scan(body, init, (per_example, given["loss_target"]))
    with _jax.named_scope("update"):
        delta_w, new_m, new_v = {}, {}, {}
        for n in TWIN_WEIGHTS:
            delta_w[n], new_m[n], new_v[n] = _adamw(weights[n], grad_w[n], given["m_" + n], given["v_" + n])
    return (loss, grad_x, *[grad_w[n] for n in TWIN_WEIGHTS], *[delta_w[n] for n in TWIN_WEIGHTS],
            *[new_m[n] for n in TWIN_WEIGHTS], *[new_v[n] for n in TWIN_WEIGHTS])
```

```python
import functools
import math

import numpy as np
import jax
import jax.numpy as jnp
from jax import lax
from jax.experimental import pallas as pl
from jax.experimental.pallas import tpu as pltpu

f32 = jnp.float32
bf16 = jnp.bfloat16
HI = lax.Precision.HIGHEST
MESH = pl.DeviceIdType.MESH

VMEM_LIMIT_BYTES = 48 * 1024 * 1024
LANES = 128

D_MODEL = 1024
DEPTH = 2
GROUP_W = 256
HEAD_DIM = 64
EPS = 1e-6
NEG_INF = -1e30
N_HEADS = 4
MLA_NOPE, MLA_ROPE = 64, 32
MLA_DQK = MLA_NOPE + MLA_ROPE
ROPE_THETA = 10000.0
Q_BLOCK = 128
S5_G, S5_CG, S5_P = 16, 16, 64
DIL_PAIRS = ((128, 1), (512, 4), (2048, 16))
T5_BUCKETS, T5_MAX_DIST = 32, 2048
DN_CHUNK = 64
FFN_HIDDEN = 2816
IN_SPLITS = (256, 128, 32, 256, 768, 768, 4, 4, 256)
IN_COLS = sum(IN_SPLITS)

ADAM_LR, ADAM_B1, ADAM_B2, ADAM_EPS, ADAM_WD, ADAM_STEP = 0.001, 0.9, 0.999, 1e-08, 0.01, 10

WEIGHTS = ['attn_norm', 'w_in', 'w_out', 'mla_q_norm', 'mla_kv_norm', 'mla_w_uq', 'mla_w_ukv', 'mla_qk_q', 'mla_qk_k',
           's5_lambda_re', 's5_lambda_im', 's5_log_dt', 's5_b_re', 's5_b_im', 's5_c_re', 's5_c_im', 's5_d', 's5_w_glu',
           'dil_q_norm', 'dil_k_norm', 't5_bias', 'dn_conv', 'dn_a_log', 'dn_dt_bias', 'dn_o_norm', 'ffn_norm',
           'ffn_w1', 'ffn_w3', 'ffn_w2']
BIG = {'w_in': 2, 'w_out': 1, 'mla_w_uq': 2, 'mla_w_ukv': 2, 's5_w_glu': 2, 'dn_conv': 2, 'ffn_w1': 2, 'ffn_w3': 2,
       'ffn_w2': 1}
SMALL = [n for n in WEIGHTS if n not in BIG]
N_SHARDS = 4
PACK_COLS = 1024


def _cparams(sem=None, big=False):
    kw = {}
    if sem is not None:
        kw['dimension_semantics'] = sem
    if big:
        kw['vmem_limit_bytes'] = VMEM_LIMIT_BYTES
    return pltpu.CompilerParams(**kw)


def _pick(n, prefs):
    for p in prefs:
        if p <= n and n % p == 0:
            return p
    return n


def _mm(a, b, mode, name, add=None, out_dtype=f32):
    if mode == 'nn':
        (M, K), (K2, N) = a.shape, b.shape
    elif mode == 'nt':
        (M, K), (N, K2) = a.shape, b.shape
    else:
        (K, M), (K2, N) = a.shape, b.shape
    assert K == K2, (name, a.shape, b.shape)
    tm = _pick(M, (512, 256, 128))
    tn = _pick(N, (512, 384, 256, 128))
    tk = K if K <= 2816 else _pick(K, (2816, 2048, 1408, 1024, 512))
    nk = K // tk
    dims = {'nn': (((1,), (0,)), ((), ())), 'nt': (((1,), (1,)), ((), ())), 'tn': (((0,), (0,)), ((), ()))}[mode]
    has_add = add is not None

    def body(*refs):
        a_ref, b_ref = refs[0], refs[1]
        add_ref = refs[2] if has_add else None
        o_ref = refs[3] if has_add else refs[2]
        part = lax.dot_general(a_ref[...].astype(bf16), b_ref[...].astype(bf16), dims, preferred_element_type=f32)
        if nk == 1:
            if has_add:
                part = part + add_ref[...].astype(f32)
            o_ref[...] = part.astype(out_dtype)
        else:
            acc_ref = refs[-1]
            k = pl.program_id(2)

            @pl.when(k == 0)
            def _():
                acc_ref[...] = part

            @pl.when(k > 0)
            def _():
                acc_ref[...] += part

            @pl.when(k == nk - 1)
            def _():
                r = acc_ref[...]
                if has_add:
                    r = r + add_ref[...].astype(f32)
                o_ref[...] = r.astype(out_dtype)

    if mode == 'nn':
        a_spec = pl.BlockSpec((tm, tk), lambda i, j, k: (i, k))
        b_spec = pl.BlockSpec((tk, tn), lambda i, j, k: (k, j))
    elif mode == 'nt':
        a_spec = pl.BlockSpec((tm, tk), lambda i, j, k: (i, k))
        b_spec = pl.BlockSpec((tn, tk), lambda i, j, k: (j, k))
    else:
        a_spec = pl.BlockSpec((tk, tm), lambda i, j, k: (k, i))
        b_spec = pl.BlockSpec((tk, tn), lambda i, j, k: (k, j))
    in_specs = [a_spec, b_spec]
    args = [a, b]
    if has_add:
        in_specs.append(pl.BlockSpec((tm, tn), lambda i, j, k: (i, j)))
        args.append(add)
    return pl.pallas_call(
        body, name=name, grid=(M // tm, N // tn, nk), in_specs=in_specs,
        out_specs=pl.BlockSpec((tm, tn), lambda i, j, k: (i, j)),
        out_shape=jax.ShapeDtypeStruct((M, N), out_dtype),
        scratch_shapes=[pltpu.VMEM((tm, tn), f32)] if nk > 1 else [],
        compiler_params=_cparams(('parallel', 'parallel', 'arbitrary'), big=True),
    )(*args)


def _full_spec(p):
    nd = p.ndim
    return pl.BlockSpec(p.shape, lambda i, _nd=nd: (0,) * _nd)


def _tile_fwd(f, tiled, params, outs, tm, name):
    S = tiled[0].shape[0]
    nt, npar = len(tiled), len(params)

    def body(*refs):
        vals = [r[...].astype(f32) for r in refs[:nt + npar]]
        res = f(*vals)
        for r, o in zip(res, refs[nt + npar:]):
            o[...] = r.astype(o.dtype)

    return pl.pallas_call(
        body, name=name, grid=(S // tm,),
        in_specs=[pl.BlockSpec((tm, t.shape[1]), lambda i: (i, 0)) for t in tiled] + [_full_spec(p) for p in params],
        out_specs=[pl.BlockSpec((tm, c), lambda i: (i, 0)) for c, _ in outs],
        out_shape=[jax.ShapeDtypeStruct((S, c), dt) for c, dt in outs],
        compiler_params=_cparams(('parallel',), big=True),
    )(*tiled, *params)


def _tile_bwd(f, tiled, params, cts, diff_t, diff_p, tm, name, dt_dtypes=None):
    S = tiled[0].shape[0]
    nt, npar, nc = len(tiled), len(params), len(cts)
    it = [i for i in range(nt) if diff_t[i]]
    ip = [i for i in range(npar) if diff_p[i]]
    if dt_dtypes is None:
        dt_dtypes = [f32] * len(it)

    def body(*refs):
        vals = [r[...].astype(f32) for r in refs[:nt + npar]]
        ct_vals = tuple(r[...].astype(f32) for r in refs[nt + npar:nt + npar + nc])
        out_refs = refs[nt + npar + nc:]

        def g(*dv):
            full = list(vals)
            for k, i in enumerate(it):
                full[i] = dv[k]
            for k, i in enumerate(ip):
                full[nt + i] = dv[len(it) + k]
            return tuple(f(*full))

        _, vjp = jax.vjp(g, *[vals[i] for i in it], *[vals[nt + i] for i in ip])
        grads = vjp(ct_vals)
        for k in range(len(it)):
            out_refs[k][...] = grads[k].astype(out_refs[k].dtype)
        step = pl.program_id(0)
        for k in range(len(ip)):
            o = out_refs[len(it) + k]
            gk = grads[len(it) + k]

            @pl.when(step == 0)
            def _(o=o, gk=gk):
                o[...] = gk

            @pl.when(step > 0)
            def _(o=o, gk=gk):
                o[...] += gk

    out_specs = [pl.BlockSpec((tm, tiled[i].shape[1]), lambda i_: (i_, 0)) for i in it] + [_full_spec(params[i]) for i in ip]
    out_shape = [jax.ShapeDtypeStruct(tiled[i].shape, dt_dtypes[k]) for k, i in enumerate(it)] + \
                [jax.ShapeDtypeStruct(params[i].shape, f32) for i in ip]
    res = pl.pallas_call(
        body, name=name, grid=(S // tm,),
        in_specs=[pl.BlockSpec((tm, t.shape[1]), lambda i: (i, 0)) for t in tiled] + [_full_spec(p) for p in params] +
                 [pl.BlockSpec((tm, c.shape[1]), lambda i: (i, 0)) for c in cts],
        out_specs=out_specs, out_shape=out_shape,
        compiler_params=_cparams(('arbitrary',), big=True),
    )(*tiled, *params, *cts)
    return list(res[:len(it)]), list(res[len(it):])


def _rms(x, g):
    return x * lax.rsqrt(jnp.mean(x * x, axis=-1, keepdims=True) + EPS) * g


def _f_rms(x, g):
    return (_rms(x, g),)


def _f_swiglu(uv):
    h = uv.shape[1] // 2
    u, v = uv[:, :h], uv[:, h:]
    return (u * jax.nn.sigmoid(u) * v,)


def _loss_head(y, target):
    S, D = y.shape
    tm = _pick(S, (256, 128))

    def body(y_ref, t_ref, part_ref, dy_ref):
        e = y_ref[...] - t_ref[...]
        dy_ref[...] = e * (1.0 / D)
        s = 0.5 * jnp.sum(jnp.sum(e * e, axis=1, keepdims=True), axis=0, keepdims=True) * (1.0 / D)
        r = lax.broadcasted_iota(jnp.int32, (8, LANES), 0)
        c = lax.broadcasted_iota(jnp.int32, (8, LANES), 1)
        part_ref[0] = jnp.where((r == 0) & (c == 0), s, 0.0)

    return pl.pallas_call(
        body, name='loss_head', grid=(S // tm,),
        in_specs=[pl.BlockSpec((tm, D), lambda i: (i, 0))] * 2,
        out_specs=[pl.BlockSpec((1, 8, LANES), lambda i: (i, 0, 0)), pl.BlockSpec((tm, D), lambda i: (i, 0))],
        out_shape=[jax.ShapeDtypeStruct((S // tm, 8, LANES), f32), jax.ShapeDtypeStruct((S, D), f32)],
        compiler_params=_cparams(('parallel',)),
    )(y, target)


def _big_block_shape(name, full_shape):
    s = list(full_shape)
    s[BIG[name]] //= N_SHARDS
    return tuple(s)


def _pack_rows(n_elems, mult):
    rows = -(-n_elems // PACK_COLS)
    return -(-rows // mult) * mult


def _pack(arrs, rows, dtype):
    flat = jnp.concatenate([a.astype(dtype).reshape(-1) for a in arrs])
    flat = jnp.pad(flat, (0, rows * PACK_COLS - flat.shape[0]))
    return flat.reshape(rows, PACK_COLS)


def _unpack(pack, shapes):
    flat = pack.reshape(-1)
    out, off = [], 0
    for s in shapes:
        n = math.prod(s)
        out.append(flat[off:off + n].reshape(s))
        off += n
    return out


def _shard_slice(name, full, s):
    ax = BIG[name]
    n = full.shape[ax] // N_SHARDS
    return lax.slice_in_dim(full, s * n, (s + 1) * n, axis=ax)


ANY = pl.BlockSpec(memory_space=pl.ANY)


def _place():
    return lax.axis_index('x'), lax.axis_index('y'), lax.axis_index('c')


def _where():
    return jnp.stack([lax.axis_index('c'), 2 * lax.axis_index('x') + lax.axis_index('y')]).astype(jnp.int32)


def _gather_weights(pack):
    R, C = pack.shape
    H = R // 2

    def body(p_ref, g_ref, send_sems, recv_sems, local_sem):
        x, y, c = _place()
        me, sib = (x, y, c), (x, y, 1 - c)
        chips = [(1 - x, y), (x, 1 - y), (1 - x, 1 - y)]

        def half(px, py, hc):
            return g_ref.at[2 * px + py, pl.ds(hc * H, H), :]

        def copy(k, src, dst, to):
            return pltpu.make_async_remote_copy(src_ref=src, dst_ref=dst, send_sem=send_sems.at[k],
                                                recv_sem=recv_sems.at[k], device_id=to, device_id_type=MESH)

        mine = pltpu.make_async_copy(p_ref, g_ref.at[2 * x + y], local_sem)
        mine.start()
        first = [copy(j, p_ref.at[pl.ds(c * H, H), :], half(x, y, c), (*chip, c)) for j, chip in enumerate(chips)]
        for cp in first:
            cp.start()
        passed = [copy(3 + j, half(*chip, c), half(*chip, c), sib) for j, chip in enumerate(chips)]
        for j, chip in enumerate(chips):
            copy(j, half(*chip, c), half(*chip, c), me).wait_recv()
            passed[j].start()
        for j, chip in enumerate(chips):
            copy(3 + j, half(*chip, 1 - c), half(*chip, 1 - c), me).wait_recv()
        for cp in first + passed:
            cp.wait_send()
        mine.wait()

    return pl.pallas_call(
        body, name='gather_weights', in_specs=[ANY], out_specs=ANY,
        out_shape=jax.ShapeDtypeStruct((N_SHARDS, R, C), pack.dtype),
        scratch_shapes=[pltpu.SemaphoreType.DMA((6,)), pltpu.SemaphoreType.DMA((6,)), pltpu.SemaphoreType.DMA],
    )(pack)


def _sibling_swap(big, small):
    _, R, C = big.shape
    H = R // 2

    def body(b_ref, s_ref, rb_ref, rs_ref, send_sems, recv_sems):
        x, y, c = _place()
        sib = (x, y, 1 - c)
        cb = pltpu.make_async_remote_copy(src_ref=b_ref.at[:, pl.ds((1 - c) * H, H), :], dst_ref=rb_ref,
                                          send_sem=send_sems.at[0], recv_sem=recv_sems.at[0], device_id=sib,
                                          device_id_type=MESH)
        cs = pltpu.make_async_remote_copy(src_ref=s_ref, dst_ref=rs_ref, send_sem=send_sems.at[1],
                                          recv_sem=recv_sems.at[1], device_id=sib, device_id_type=MESH)
        cb.start()
        cs.start()
        cb.wait()
        cs.wait()

    return pl.pallas_call(
        body, name='sibling_swap', in_specs=[ANY, ANY], out_specs=[ANY, ANY],
        out_shape=[jax.ShapeDtypeStruct((N_SHARDS, H, C), big.dtype), jax.ShapeDtypeStruct(small.shape, small.dtype)],
        scratch_shapes=[pltpu.SemaphoreType.DMA((2,)), pltpu.SemaphoreType.DMA((2,))],
    )(big, small)


def _chip_exchange(big, small):
    _, H, C = big.shape

    def body(b_ref, s_ref, rb_ref, rs_ref, send_sems, recv_sems):
        x, y, c = _place()
        chips = [(1 - x, y), (x, 1 - y), (1 - x, 1 - y)]
        cps = []
        for j, (px, py) in enumerate(chips):
            cps.append(pltpu.make_async_remote_copy(src_ref=b_ref.at[2 * px + py], dst_ref=rb_ref.at[j],
                                                    send_sem=send_sems.at[j], recv_sem=recv_sems.at[j],
                                                    device_id=(px, py, c), device_id_type=MESH))
            cps.append(pltpu.make_async_remote_copy(src_ref=s_ref, dst_ref=rs_ref.at[j],
                                                    send_sem=send_sems.at[3 + j], recv_sem=recv_sems.at[3 + j],
                                                    device_id=(px, py, c), device_id_type=MESH))
        for cp in cps:
            cp.start()
        for cp in cps:
            cp.wait()

    return pl.pallas_call(
        body, name='chip_exchange', in_specs=[ANY, ANY], out_specs=[ANY, ANY],
        out_shape=[jax.ShapeDtypeStruct((3, H, C), big.dtype), jax.ShapeDtypeStruct((3,) + small.shape, small.dtype)],
        scratch_shapes=[pltpu.SemaphoreType.DMA((6,)), pltpu.SemaphoreType.DMA((6,))],
    )(big, small)


def _sibling_join(half):
    H, C = half.shape

    def body(h_ref, o_ref, send_sem, recv_sem, local_sem):
        x, y, c = _place()
        mine = pltpu.make_async_copy(h_ref, o_ref.at[pl.ds(c * H, H), :], local_sem)
        mine.start()
        cp = pltpu.make_async_remote_copy(src_ref=h_ref, dst_ref=o_ref.at[pl.ds(c * H, H), :], send_sem=send_sem,
                                          recv_sem=recv_sem, device_id=(x, y, 1 - c), device_id_type=MESH)
        cp.start()
        cp.wait_send()
        pltpu.make_async_remote_copy(src_ref=h_ref, dst_ref=o_ref.at[pl.ds((1 - c) * H, H), :], send_sem=send_sem,
                                     recv_sem=recv_sem, device_id=(x, y, 1 - c), device_id_type=MESH).wait_recv()
        mine.wait()

    return pl.pallas_call(
        body, name='sibling_join', in_specs=[ANY], out_specs=ANY,
        out_shape=jax.ShapeDtypeStruct((2 * H, C), half.dtype),
        scratch_shapes=[pltpu.SemaphoreType.DMA, pltpu.SemaphoreType.DMA, pltpu.SemaphoreType.DMA],
    )(half)


def _chip_sum(big, recv_big):
    _, R, C = big.shape
    H = R // 2
    tr = _pick(H, (512, 256, 128, 64, 32, 16))
    nb = H // tr

    def body(w_ref, a_ref, b_ref, o_ref):
        o_ref[...] = (a_ref[...].astype(f32) + b_ref[...].astype(f32)).astype(o_ref.dtype)

    return pl.pallas_call(
        body, name='chip_sum',
        grid_spec=pltpu.PrefetchScalarGridSpec(
            num_scalar_prefetch=1, grid=(N_SHARDS, nb),
            in_specs=[pl.BlockSpec((1, tr, C), lambda s, i, w: (s, w[0] * nb + i, 0)),
                      pl.BlockSpec((1, tr, C), lambda s, i, w: (s, i, 0))],
            out_specs=pl.BlockSpec((1, tr, C), lambda s, i, w: (s, i, 0))),
        out_shape=jax.ShapeDtypeStruct((N_SHARDS, H, C), bf16),
        compiler_params=_cparams(('parallel', 'parallel')),
    )(_where(), big, recv_big)


def _shard_total(big, recv_sib, recv_chips):
    _, R, C = big.shape
    H = R // 2
    tr = _pick(H, (512, 256, 128, 64, 32, 16))
    nb = H // tr

    def body(w_ref, a_ref, b_ref, r_ref, o_ref):
        t = a_ref[0].astype(f32) + b_ref[0].astype(f32)
        t = t + r_ref[0].astype(f32)
        t = t + r_ref[1].astype(f32)
        t = t + r_ref[2].astype(f32)
        o_ref[...] = t

    return pl.pallas_call(
        body, name='shard_total',
        grid_spec=pltpu.PrefetchScalarGridSpec(
            num_scalar_prefetch=1, grid=(nb,),
            in_specs=[pl.BlockSpec((1, tr, C), lambda i, w: (w[1], w[0] * nb + i, 0)),
                      pl.BlockSpec((1, tr, C), lambda i, w: (w[1], i, 0)),
                      pl.BlockSpec((3, tr, C), lambda i, w: (0, i, 0))],
            out_specs=pl.BlockSpec((tr, C), lambda i, w: (i, 0))),
        out_shape=jax.ShapeDtypeStruct((H, C), f32),
        compiler_params=_cparams(('parallel',)),
    )(_where(), big, recv_sib, recv_chips)


def _adam_math(w, g, m, v):
    m = ADAM_B1 * m + (1.0 - ADAM_B1) * g
    v = ADAM_B2 * v + (1.0 - ADAM_B2) * (g * g)
    m_hat = m / (1.0 - ADAM_B1 ** ADAM_STEP)
    v_hat = v / (1.0 - ADAM_B2 ** ADAM_STEP)
    delta = -ADAM_LR * (m_hat / (jnp.sqrt(v_hat) + ADAM_EPS) + ADAM_WD * w)
    return delta, m, v


def _small_update(own, sib, chips, w, m, v):
    def body(o_ref, s_ref, c_ref, w_ref, m_ref, v_ref, g_out, d_out, m_out, v_out):
        chip = o_ref[...] + s_ref[...]
        g = (chip + c_ref[0]) + (c_ref[1] + c_ref[2])
        d, mn, vn = _adam_math(w_ref[...], g, m_ref[...], v_ref[...])
        g_out[...] = g
        d_out[...] = d
        m_out[...] = mn
        v_out[...] = vn

    return pl.pallas_call(body, name='small_update', out_shape=[jax.ShapeDtypeStruct(own.shape, f32)] * 4)(
        own, sib, chips, w, m, v)


def _small_chip_sum(own, sib):
    def body(o_ref, s_ref, out):
        out[...] = o_ref[...] + s_ref[...]
    return pl.pallas_call(body, name='small_chip_sum', out_shape=jax.ShapeDtypeStruct(own.shape, f32))(own, sib)


def _adamw(w, g, m, v, name):
    shape = w.shape
    w2, g2, m2, v2 = [t.reshape(-1, shape[-1]) for t in (w, g, m, v)]
    rows, cols = w2.shape
    tr = _pick(rows, (256, 128, 64, 32, 16, 8))

    def body(w_ref, g_ref, m_ref, v_ref, d_out, m_out, v_out):
        d, mn, vn = _adam_math(w_ref[...], g_ref[...], m_ref[...], v_ref[...])
        d_out[...] = d
        m_out[...] = mn
        v_out[...] = vn

    spec = pl.BlockSpec((tr, cols), lambda i: (i, 0))
    res = pl.pallas_call(body, name=name, grid=(rows // tr,), in_specs=[spec] * 4, out_specs=[spec] * 3,
                         out_shape=[jax.ShapeDtypeStruct((rows, cols), f32)] * 3,
                         compiler_params=_cparams(('parallel',)))(w2, g2, m2, v2)
    return [r.reshape(shape) for r in res]


def _dg(a, b, ca, cb):
    return lax.dot_general(a.astype(bf16), b.astype(bf16), (((ca,), (cb,)), ((), ())), preferred_element_type=f32)


@jax.custom_vjp
def _bmm(a, b):
    return _dg(a, b, 1, 0)


_bmm.defvjp(lambda a, b: (_dg(a, b, 1, 0), (a, b)), lambda r, g: (_dg(g, r[1], 1, 1), _dg(r[0], g, 0, 0)))


@jax.custom_vjp
def _bmm_nt(a, b):
    return _dg(a, b, 1, 1)


_bmm_nt.defvjp(lambda a, b: (_dg(a, b, 1, 1), (a, b)), lambda r, g: (_dg(g, r[1], 1, 0), _dg(g, r[0], 0, 0)))


@jax.custom_vjp
def _bmm_tn(a, b):
    return _dg(a, b, 0, 0)


_bmm_tn.defvjp(lambda a, b: (_dg(a, b, 0, 0), (a, b)), lambda r, g: (_dg(r[1], g, 1, 1), _dg(r[0], g, 1, 0)))


def _hdot(a, b):
    return jnp.dot(a, b, precision=HI, preferred_element_type=f32)


def _hdot_nt(a, b):
    return lax.dot_general(a, b, (((1,), (1,)), ((), ())), precision=HI, preferred_element_type=f32)


def _hdot_tn(a, b):
    return lax.dot_general(a, b, (((0,), (0,)), ((), ())), precision=HI, preferred_element_type=f32)


def _head_mask(h, width=GROUP_W):
    lane = lax.broadcasted_iota(jnp.int32, (1, width), 1)
    return ((lane >= h * HEAD_DIM) & (lane < (h + 1) * HEAD_DIM)).astype(f32)


def _rope_perm():
    p = np.zeros((LANES, LANES), np.float32)
    half = MLA_ROPE // 2
    for i in range(half):
        p[MLA_NOPE + half + i, MLA_NOPE + i] = -1.0
        p[MLA_NOPE + i, MLA_NOPE + half + i] = 1.0
    return jnp.asarray(p)


def _rope_tables(S):
    half = MLA_ROPE // 2
    freqs = ROPE_THETA ** (-jnp.arange(half, dtype=f32) / half)
    ang = jnp.arange(S, dtype=f32)[:, None] * freqs[None, :]
    cos, sin = jnp.cos(ang), jnp.sin(ang)
    ones, zeros = jnp.ones((S, MLA_NOPE), f32), jnp.zeros((S, LANES - MLA_DQK), f32)
    c_tab = jnp.concatenate([ones, cos, cos, zeros], axis=1)
    s_tab = jnp.concatenate([jnp.zeros((S, MLA_NOPE), f32), sin, sin, zeros], axis=1)
    return c_tab, s_tab


def _f_mla_pre(c_q, c_kv, krope, c_tab, s_tab, q_norm, kv_norm, wq0, wq1, wq2, wq3, wk0, wk1, wk2, wk3, wv, gq, gk, perm):
    wq, wk = (wq0, wq1, wq2, wq3), (wk0, wk1, wk2, wk3)
    nq = _rms(c_q, q_norm)
    nkv = _rms(c_kv, kv_norm)

    def norm_rope(t, g):
        t = t * lax.rsqrt(jnp.sum(t * t, axis=-1, keepdims=True) * (1.0 / MLA_DQK) + EPS) * g
        return t * c_tab + _hdot(t, perm) * s_tab

    qs = [norm_rope(_bmm(nq, wq[h]), gq) * (MLA_DQK ** -0.5) for h in range(N_HEADS)]
    ks = [norm_rope(_bmm(nkv, wk[h]) + krope, gk) for h in range(N_HEADS)]
    return (*qs, *ks, _bmm(nkv, wv))


def _f_attn(qs, ks, v, q0):
    tq, S = qs[0].shape[0], ks[0].shape[0]
    qpos = q0 + lax.broadcasted_iota(jnp.int32, (tq, S), 0)
    kpos = lax.broadcasted_iota(jnp.int32, (tq, S), 1)
    keep = kpos <= qpos
    out = jnp.zeros((tq, GROUP_W), f32)
    for h in range(N_HEADS):
        logits = jnp.where(keep, _bmm_nt(qs[h], ks[h]), NEG_INF)
        p = jnp.exp(logits - jnp.max(logits, axis=-1, keepdims=True))
        p = p / jnp.sum(p, axis=-1, keepdims=True)
        out = out + _bmm(p, v) * _head_mask(h)
    return out


def _mla_attn_fwd(qs, ks, v, name):
    S = v.shape[0]
    tq = Q_BLOCK

    def body(*refs):
        q_vals = [r[...] for r in refs[:4]]
        k_vals = [r[...] for r in refs[4:8]]
        refs[9][...] = _f_attn(q_vals, k_vals, refs[8][...], pl.program_id(0) * tq)

    qspec = pl.BlockSpec((tq, LANES), lambda i: (i, 0))
    return pl.pallas_call(
        body, name=name, grid=(S // tq,),
        in_specs=[qspec] * 4 + [_full_spec(k) for k in ks] + [_full_spec(v)],
        out_specs=pl.BlockSpec((tq, GROUP_W), lambda i: (i, 0)),
        out_shape=jax.ShapeDtypeStruct((S, GROUP_W), f32),
        compiler_params=_cparams(('parallel',), big=True),
    )(*qs, *ks, v)


def _mla_attn_bwd(qs, ks, v, do, name):
    S = v.shape[0]
    tq = Q_BLOCK

    def body(*refs):
        q_vals = [r[...].astype(f32) for r in refs[:4]]
        k_vals = [r[...].astype(f32) for r in refs[4:8]]
        v_val = refs[8][...].astype(f32)
        q0 = pl.program_id(0) * tq
        _, vjp = jax.vjp(lambda a, b, c: _f_attn(a, b, c, q0), q_vals, k_vals, v_val)
        dqs, dks, dv = vjp(refs[9][...])
        outs = refs[10:]
        for h in range(N_HEADS):
            outs[h][...] = dqs[h]
        first = pl.program_id(0) == 0
        for o, g in zip(outs[4:], (*dks, dv)):
            @pl.when(first)
            def _(o=o, g=g):
                o[...] = g

            @pl.when(jnp.logical_not(first))
            def _(o=o, g=g):
                o[...] += g

    qspec = pl.BlockSpec((tq, LANES), lambda i: (i, 0))
    res = pl.pallas_call(
        body, name=name, grid=(S // tq,),
        in_specs=[qspec] * 4 + [_full_spec(k) for k in ks] + [_full_spec(v), pl.BlockSpec((tq, GROUP_W), lambda i: (i, 0))],
        out_specs=[qspec] * 4 + [_full_spec(k) for k in ks] + [_full_spec(v)],
        out_shape=[jax.ShapeDtypeStruct((S, LANES), f32)] * 8 + [jax.ShapeDtypeStruct((S, GROUP_W), f32)],
        compiler_params=_cparams(('arbitrary',), big=True),
    )(*qs, *ks, v, do)
    return res[:4], res[4:8], res[8]


def _mla_params(mp):
    pad = LANES - MLA_DQK
    wq = jnp.pad(mp['mla_w_uq'].reshape(GROUP_W, N_HEADS, MLA_DQK).transpose(1, 0, 2), ((0, 0), (0, 0), (0, pad)))
    wkv = mp['mla_w_ukv'].reshape(LANES, N_HEADS, MLA_NOPE + HEAD_DIM)
    wk = jnp.pad(wkv[:, :, :MLA_NOPE].transpose(1, 0, 2), ((0, 0), (0, 0), (0, LANES - MLA_NOPE)))
    wv = wkv[:, :, MLA_NOPE:].reshape(LANES, GROUP_W)
    gq = jnp.pad(mp['mla_qk_q'], (0, pad))[None]
    gk = jnp.pad(mp['mla_qk_k'], (0, pad))[None]
    return [mp['mla_q_norm'][None], mp['mla_kv_norm'][None], *[wq[h] for h in range(N_HEADS)],
            *[wk[h] for h in range(N_HEADS)], wv, gq, gk, _rope_perm()]


def _mla_fwd(c_q, c_kv, k_rope, mp, l):
    S = c_q.shape[0]
    tm = _pick(S, (256, 128))
    krope = jnp.pad(k_rope, ((0, 0), (MLA_NOPE, LANES - MLA_DQK)))
    c_tab, s_tab = _rope_tables(S)
    tiled = [c_q, c_kv, krope, c_tab, s_tab]
    params = _mla_params(mp)
    res = _tile_fwd(_f_mla_pre, tiled, params, [(LANES, bf16)] * 8 + [(GROUP_W, bf16)], tm, f'mla_pre_fwd_{l}')
    qs, ks, v = res[:4], res[4:8], res[8]
    y = _mla_attn_fwd(qs, ks, v, f'mla_attn_fwd_{l}')
    return y, (tiled, params, qs, ks, v)


def _mla_bwd(dy, saved, l):
    tiled, params, qs, ks, v = saved
    S = dy.shape[0]
    tm = _pick(S, (256, 128))
    dqs, dks, dv = _mla_attn_bwd(qs, ks, v, dy, f'mla_attn_bwd_{l}')
    (dc_q, dc_kv, dkrope), dpar = _tile_bwd(_f_mla_pre, tiled, params, [*dqs, *dks, dv], [True, True, True, False, False],
                                            [True] * 13 + [False], tm, f'mla_pre_bwd_{l}')
    dqn, dkvn = dpar[0], dpar[1]
    dwq, dwk = jnp.stack(dpar[2:6]), jnp.stack(dpar[6:10])
    dwv, dgq, dgk = dpar[10:13]
    dw_uq = dwq[:, :, :MLA_DQK].transpose(1, 0, 2).reshape(GROUP_W, N_HEADS * MLA_DQK)
    dw_ukv = jnp.concatenate([dwk[:, :, :MLA_NOPE].transpose(1, 0, 2), dwv.reshape(LANES, N_HEADS, HEAD_DIM)],
                             axis=2).reshape(LANES, N_HEADS * (MLA_NOPE + HEAD_DIM))
    grads = {'mla_q_norm': dqn[0], 'mla_kv_norm': dkvn[0], 'mla_w_uq': dw_uq, 'mla_w_ukv': dw_ukv,
             'mla_qk_q': dgq[0, :MLA_DQK], 'mla_qk_k': dgk[0, :MLA_DQK]}
    return dc_q, dc_kv, dkrope[:, MLA_NOPE:MLA_DQK], grads


SPAN = 128


def _head_mean_matrix():
    h = np.arange(GROUP_W) // HEAD_DIM
    return jnp.asarray((h[:, None] == h[None, :]).astype(np.float32) / HEAD_DIM)


def _f_dil_pre(q, k, gq, gk, hm):
    qn = q * lax.rsqrt(_hdot(q * q, hm) + EPS) * gq * (HEAD_DIM ** -0.5)
    kn = k * lax.rsqrt(_hdot(k * k, hm) + EPS) * gk
    return qn, kn


def _f_dil_branch(qb, kp, kc, vp, vc, b0, b1, b2, b3, first):
    kcat = jnp.concatenate([kp, kc], axis=0)
    vcat = jnp.concatenate([vp, vc], axis=0)
    qi = lax.broadcasted_iota(jnp.int32, (SPAN, 2 * SPAN), 0) + SPAN
    kj = lax.broadcasted_iota(jnp.int32, (SPAN, 2 * SPAN), 1)
    delta = qi - kj
    valid = (delta >= 0) & (delta <= SPAN) & jnp.logical_not(first & (kj < SPAN))
    o = jnp.zeros((SPAN, GROUP_W), f32)
    m_full = jnp.zeros((SPAN, GROUP_W), f32)
    l_full = jnp.zeros((SPAN, GROUP_W), f32)
    for h, bias in enumerate((b0, b1, b2, b3)):
        hm = _head_mask(h)
        logits = jnp.where(valid, _bmm_nt(qb * hm, kcat) + bias, NEG_INF)
        m = jnp.max(logits, axis=-1, keepdims=True)
        p = jnp.exp(logits - m)
        o = o + _bmm(p, vcat) * hm
        m_full = m_full + m * hm
        l_full = l_full + jnp.sum(p, axis=-1, keepdims=True) * hm
    return o, m_full, l_full


def _dil_branch_specs(d, nb):
    cur = pl.BlockSpec((1, SPAN, GROUP_W), lambda r, n: (r, n, 0))
    prev = pl.BlockSpec((1, SPAN, GROUP_W), lambda r, n: (r, jnp.maximum(n - 1, 0), 0))
    bias = pl.BlockSpec((1, SPAN, 2 * SPAN), lambda r, n: (0, 0, 0))
    return cur, prev, bias


def _dil_branch_fwd(q, k, v, biases, name):
    d, L, _ = q.shape
    nb = L // SPAN
    cur, prev, bias = _dil_branch_specs(d, nb)

    def body(q_ref, kp_ref, kc_ref, vp_ref, vc_ref, b0, b1, b2, b3, o_ref, m_ref, l_ref):
        o, m, l = _f_dil_branch(q_ref[0], kp_ref[0], kc_ref[0], vp_ref[0], vc_ref[0], b0[0], b1[0], b2[0], b3[0],
                                pl.program_id(1) == 0)
        o_ref[0] = o
        m_ref[0] = m
        l_ref[0] = l

    return pl.pallas_call(
        body, name=name, grid=(d, nb), in_specs=[cur, prev, cur, prev, cur] + [bias] * 4,
        out_specs=[cur] * 3, out_shape=[jax.ShapeDtypeStruct(q.shape, f32)] * 3,
        compiler_params=_cparams(('parallel', 'parallel')),
    )(q, k, k, v, v, *biases)


def _dil_branch_bwd(q, k, v, biases, do, dm, dl, name):
    d, L, _ = q.shape
    nb = L // SPAN
    cur, prev, bias = _dil_branch_specs(d, nb)
    whole = pl.BlockSpec((1, L, GROUP_W), lambda r, n: (r, 0, 0))

    def body(q_ref, kp_ref, kc_ref, vp_ref, vc_ref, b0, b1, b2, b3, do_ref, dm_ref, dl_ref,
             dq_ref, dk_ref, dv_ref, db0, db1, db2, db3):
        r, n = pl.program_id(0), pl.program_id(1)
        first = n == 0
        _, vjp = jax.vjp(lambda *a: _f_dil_branch(*a, first), q_ref[0], kp_ref[0], kc_ref[0], vp_ref[0], vc_ref[0],
                         b0[0], b1[0], b2[0], b3[0])
        dq, dkp, dkc, dvp, dvc, g0, g1, g2, g3 = vjp((do_ref[0], dm_ref[0], dl_ref[0]))
        dq_ref[0] = dq

        @pl.when(first)
        def _():
            dk_ref[...] = jnp.zeros_like(dk_ref)
            dv_ref[...] = jnp.zeros_like(dv_ref)

        rows = pl.ds(pl.multiple_of(n * SPAN, SPAN), SPAN)
        dk_ref[0, rows, :] += dkc
        dv_ref[0, rows, :] += dvc

        @pl.when(n > 0)
        def _():
            before = pl.ds(pl.multiple_of((n - 1) * SPAN, SPAN), SPAN)
            dk_ref[0, before, :] += dkp
            dv_ref[0, before, :] += dvp

        start = first & (r == 0)
        for o, g in zip((db0, db1, db2, db3), (g0, g1, g2, g3)):
            @pl.when(start)
            def _(o=o, g=g):
                o[0] = g

            @pl.when(jnp.logical_not(start))
            def _(o=o, g=g):
                o[0] += g

    res = pl.pallas_call(
        body, name=name, grid=(d, nb), in_specs=[cur, prev, cur, prev, cur] + [bias] * 4 + [cur] * 3,
        out_specs=[cur, whole, whole] + [bias] * 4,
        out_shape=[jax.ShapeDtypeStruct(q.shape, f32)] * 3 + [jax.ShapeDtypeStruct((1, SPAN, 2 * SPAN), f32)] * 4,
        compiler_params=_cparams(('arbitrary', 'arbitrary')),
    )(q, k, k, v, v, *biases, do, dm, dl)
    return res[0], res[1], res[2], res[3:]


def _f_dil_merge(o1, m1, l1, o2, m2, l2, o3, m3, l3):
    mx = jnp.maximum(jnp.maximum(m1, m2), m3)
    w1, w2, w3 = jnp.exp(m1 - mx), jnp.exp(m2 - mx), jnp.exp(m3 - mx)
    return ((w1 * o1 + w2 * o2 + w3 * o3) / (w1 * l1 + w2 * l2 + w3 * l3),)


def _bias_onehot(dilation):
    qi = jnp.arange(SPAN, dtype=jnp.int32)[:, None] + SPAN
    kj = jnp.arange(2 * SPAN, dtype=jnp.int32)[None, :]
    bucket = _t5_bucket(jnp.clip(qi - kj, 0, SPAN) * dilation).reshape(-1)
    return (bucket[None, :] == jnp.arange(T5_BUCKETS, dtype=jnp.int32)[:, None]).astype(f32)


def _bias_tables(t5_t, onehot, name):
    N = onehot.shape[1]
    tn = _pick(N, (4096, 2048, 1024))

    def body(t_ref, oh_ref, o_ref):
        o_ref[...] = _hdot(t_ref[...], oh_ref[...])

    return pl.pallas_call(
        body, name=name, grid=(N // tn,),
        in_specs=[pl.BlockSpec((8, T5_BUCKETS), lambda i: (0, 0)), pl.BlockSpec((T5_BUCKETS, tn), lambda i: (0, i))],
        out_specs=pl.BlockSpec((8, tn), lambda i: (0, i)), out_shape=jax.ShapeDtypeStruct((8, N), f32),
        compiler_params=_cparams(('parallel',)),
    )(t5_t, onehot)


def _bias_tables_bwd(d_tab, onehot, name):
    N = onehot.shape[1]
    tn = _pick(N, (4096, 2048, 1024))

    def body(g_ref, oh_ref, o_ref):
        part = _hdot_nt(g_ref[...], oh_ref[...])

        @pl.when(pl.program_id(0) == 0)
        def _():
            o_ref[...] = part

        @pl.when(pl.program_id(0) > 0)
        def _():
            o_ref[...] += part

    return pl.pallas_call(
        body, name=name, grid=(N // tn,),
        in_specs=[pl.BlockSpec((8, tn), lambda i: (0, i)), pl.BlockSpec((T5_BUCKETS, tn), lambda i: (0, i))],
        out_specs=pl.BlockSpec((8, T5_BUCKETS), lambda i: (0, 0)), out_shape=jax.ShapeDtypeStruct((8, T5_BUCKETS), f32),
        compiler_params=_cparams(('arbitrary',)),
    )(d_tab, onehot)


def _by_residue(t, d):
    S, C = t.shape
    return t.reshape(S // d, d, C).transpose(1, 0, 2)


def _from_residue(t):
    d, L, C = t.shape
    return t.transpose(1, 0, 2).reshape(d * L, C)


def _dil_fwd(qkv, mp, l):
    S = qkv.shape[0]
    tm = _pick(S, (256, 128))
    q, k, v = qkv[:, :GROUP_W], qkv[:, GROUP_W:2 * GROUP_W], qkv[:, 2 * GROUP_W:]
    pre_params = [jnp.tile(mp['dil_q_norm'], N_HEADS)[None], jnp.tile(mp['dil_k_norm'], N_HEADS)[None], _head_mean_matrix()]
    qn, kn = _tile_fwd(_f_dil_pre, [q, k], pre_params, [(GROUP_W, f32)] * 2, tm, f'dil_pre_fwd_{l}')
    t5_t = jnp.pad(mp['t5_bias'].T, ((0, 8 - N_HEADS), (0, 0)))
    branches, outs = [], []
    for bi, (_, d) in enumerate(DIL_PAIRS):
        onehot = _bias_onehot(d)
        tab = _bias_tables(t5_t, onehot, f'dil_bias_fwd_{l}_{bi}').reshape(8, SPAN, 2 * SPAN)
        biases = [tab[h][None] for h in range(N_HEADS)]
        qd, kd, vd = _by_residue(qn, d), _by_residue(kn, d), _by_residue(v, d)
        o, m, lsum = _dil_branch_fwd(qd, kd, vd, biases, f'dil_branch_fwd_{l}_{bi}')
        branches.append((qd, kd, vd, biases, onehot))
        outs += [_from_residue(o), _from_residue(m), _from_residue(lsum)]
    (y,) = _tile_fwd(_f_dil_merge, outs, [], [(GROUP_W, f32)], tm, f'dil_merge_fwd_{l}')
    return y, (q, k, pre_params, branches, outs)


def _dil_bwd(dy, saved, l):
    q, k, pre_params, branches, outs = saved
    S = dy.shape[0]
    tm = _pick(S, (256, 128))
    douts, _ = _tile_bwd(_f_dil_merge, outs, [], [dy], [True] * 9, [], tm, f'dil_merge_bwd_{l}')
    dqn = dkn = dv = None
    dt5_t = None
    for bi, (_, d) in enumerate(DIL_PAIRS):
        qd, kd, vd, biases, onehot = branches[bi]
        do, dm, dl = [_by_residue(t, d) for t in douts[3 * bi:3 * bi + 3]]
        dq_b, dk_b, dv_b, dbias = _dil_branch_bwd(qd, kd, vd, biases, do, dm, dl, f'dil_branch_bwd_{l}_{bi}')
        d_tab = jnp.concatenate([*dbias, jnp.zeros((8 - N_HEADS, SPAN, 2 * SPAN), f32)], axis=0).reshape(8, -1)
        g_t5 = _bias_tables_bwd(d_tab, onehot, f'dil_bias_bwd_{l}_{bi}')
        dq_b, dk_b, dv_b = _from_residue(dq_b), _from_residue(dk_b), _from_residue(dv_b)
        dqn = dq_b if dqn is None else dqn + dq_b
        dkn = dk_b if dkn is None else dkn + dk_b
        dv = dv_b if dv is None else dv + dv_b
        dt5_t = g_t5 if dt5_t is None else dt5_t + g_t5
    (dq, dk), (dgq, dgk) = _tile_bwd(_f_dil_pre, [q, k], pre_params, [dqn, dkn], [True, True], [True, True, False], tm,
                                     f'dil_pre_bwd_{l}')
    grads = {'dil_q_norm': dgq.reshape(N_HEADS, HEAD_DIM).sum(0), 'dil_k_norm': dgk.reshape(N_HEADS, HEAD_DIM).sum(0),
             't5_bias': dt5_t[:N_HEADS].T}
    return jnp.concatenate([dq, dk, dv], axis=1), grads


S5_LANES = S5_G * S5_P
SCAN_SEGMENTS = 8
SCAN_W = 256


def _f_s5_prep(bre, bim, lr, li, logdt_col, expand):
    dt = jnp.sum(jnp.exp(logdt_col) * expand, axis=0, keepdims=True)
    mag = jnp.exp(lr * dt)
    ar, ai = mag * jnp.cos(li * dt), mag * jnp.sin(li * dt)
    den = lr * lr + li * li
    nr, ni = ar - 1.0, ai
    zr = (nr * lr + ni * li) / den
    zi = (ni * lr - nr * li) / den
    bb = jnp.concatenate([zr * bre - zi * bim, zr * bim + zi * bre], axis=1)
    a_rows = jnp.broadcast_to(jnp.concatenate([ar, ai], axis=1), bb.shape)
    return bb, a_rows


def _s5_scan(xr, xi, a_rows, name, reverse=False, hpr=None, hpi=None):
    T, _, NL = xr.shape
    nblk = NL // SCAN_W

    def body(*refs):
        if reverse:
            xr_ref, xi_ref, ar_ref, ai_ref, hpr_ref, hpi_ref, hr_ref, hi_ref, dar_ref, dai_ref = refs
        else:
            xr_ref, xi_ref, ar_ref, ai_ref, hr_ref, hi_ref = refs
        ar = ar_ref[...]
        ai = -ai_ref[...] if reverse else ai_ref[...]
        zero = jnp.zeros((SCAN_SEGMENTS, SCAN_W), f32)

        def at(s):
            return T - 1 - s if reverse else s

        def local(s, c):
            hr, hi, pr, pi = c
            j = at(s)
            nhr = ar * hr - ai * hi + xr_ref[j]
            nhi = ar * hi + ai * hr + xi_ref[j]
            hr_ref[j] = nhr
            hi_ref[j] = nhi
            return nhr, nhi, ar * pr - ai * pi, ar * pi + ai * pr

        er, ei, pr, pi = lax.fori_loop(0, T, local, (zero, zero, zero + 1.0, zero), unroll=2)
        row = lax.broadcasted_iota(jnp.int32, (SCAN_SEGMENTS, SCAN_W), 0)
        cr, ci = zero, zero
        order = range(SCAN_SEGMENTS - 2, -1, -1) if reverse else range(1, SCAN_SEGMENTS)
        for k in order:
            src = k + 1 if reverse else k - 1
            tr = er + pr * cr - pi * ci
            ti = ei + pr * ci + pi * cr
            cr = jnp.where(row == k, jnp.sum(jnp.where(row == src, tr, 0.0), axis=0, keepdims=True), cr)
            ci = jnp.where(row == k, jnp.sum(jnp.where(row == src, ti, 0.0), axis=0, keepdims=True), ci)

        def fix(s, c):
            pr, pi, sr, si = c
            j = at(s)
            pr, pi = ar * pr - ai * pi, ar * pi + ai * pr
            hr = hr_ref[j] + pr * cr - pi * ci
            hi = hi_ref[j] + pr * ci + pi * cr
            hr_ref[j] = hr
            hi_ref[j] = hi
            if reverse:
                qr, qi = hpr_ref[j], hpi_ref[j]
                sr = sr + hr * qr + hi * qi
                si = si + hi * qr - hr * qi
            return pr, pi, sr, si

        _, _, sr, si = lax.fori_loop(0, T, fix, (zero + 1.0, zero, zero, zero), unroll=2)
        if reverse:
            dar_ref[...] = sr
            dai_ref[...] = si

    seq = pl.BlockSpec((T, SCAN_SEGMENTS, SCAN_W), lambda b: (0, 0, b))
    a_re = pl.BlockSpec((SCAN_SEGMENTS, SCAN_W), lambda b: (0, b))
    a_im = pl.BlockSpec((SCAN_SEGMENTS, SCAN_W), lambda b: (0, nblk + b))
    out_specs = [seq, seq]
    out_shape = [jax.ShapeDtypeStruct(xr.shape, f32)] * 2
    args = [xr, xi, a_rows, a_rows]
    in_specs = [seq, seq, a_re, a_im]
    if reverse:
        args += [hpr, hpi]
        in_specs += [seq, seq]
        out_specs += [a_re, a_re]
        out_shape += [jax.ShapeDtypeStruct((SCAN_SEGMENTS, NL), f32)] * 2
    return pl.pallas_call(body, name=name, grid=(nblk,), in_specs=in_specs, out_specs=out_specs, out_shape=out_shape,
                          compiler_params=_cparams(('parallel',), big=True))(*args)


def _to_segments(t):
    S, C = t.shape
    return t.reshape(SCAN_SEGMENTS, S // SCAN_SEGMENTS, C).transpose(1, 0, 2)


def _from_segments(t):
    T, K, C = t.shape
    return t.transpose(1, 0, 2).reshape(K * T, C)


def _f_s5_post(y, u, d, w_glu):
    z = _bmm(y + d * u, w_glu)
    return (z[:, :GROUP_W] * jax.nn.sigmoid(z[:, GROUP_W:]),)


def _block_diag(t):
    G, a, b = t.shape
    eye = jnp.eye(G, dtype=t.dtype)
    return (t[:, :, None, :] * eye[:, None, :, None]).reshape(G * a, G * b)


def _diag_blocks(m, a, b):
    G = m.shape[0] // a
    return jnp.moveaxis(jnp.diagonal(m.reshape(G, a, G, b), axis1=0, axis2=2), -1, 0)


def _s5_fwd(u, mp, l):
    S = u.shape[0]
    tm = _pick(S, (256, 128))
    bre = _block_diag(mp['s5_b_re'].transpose(0, 2, 1))
    bim = _block_diag(mp['s5_b_im'].transpose(0, 2, 1))
    expand = jnp.repeat(jnp.eye(S5_G, dtype=f32), S5_P, axis=1)
    prep_params = [mp['s5_lambda_re'].reshape(1, S5_LANES), mp['s5_lambda_im'].reshape(1, S5_LANES),
                   mp['s5_log_dt'].reshape(S5_G, 1), expand]
    bb, a_rows = _tile_fwd(_f_s5_prep, [bre, bim], prep_params, [(2 * S5_LANES, f32)] * 2, GROUP_W, f's5_prep_fwd_{l}')
    x = _mm(u, bb, 'nn', f's5_in_fwd_{l}')
    hr, hi = _s5_scan(_to_segments(x[:, :S5_LANES]), _to_segments(x[:, S5_LANES:]), a_rows, f's5_scan_fwd_{l}')
    h = jnp.concatenate([_from_segments(hr), _from_segments(hi)], axis=1)
    ccat = jnp.concatenate([_block_diag(mp['s5_c_re'].transpose(0, 2, 1)), -_block_diag(mp['s5_c_im'].transpose(0, 2, 1))],
                           axis=0)
    y = _mm(h, ccat, 'nn', f's5_out_fwd_{l}')
    post_params = [mp['s5_d'][None], mp['s5_w_glu']]
    (out,) = _tile_fwd(_f_s5_post, [y, u], post_params, [(GROUP_W, f32)], tm, f's5_post_fwd_{l}')
    return out, (u, bre, bim, prep_params, bb, a_rows, h, ccat, y, post_params)


def _s5_bwd(dout, saved, l):
    u, bre, bim, prep_params, bb, a_rows, h, ccat, y, post_params = saved
    S = u.shape[0]
    tm = _pick(S, (256, 128))
    (dy, du1), (dd, dwglu) = _tile_bwd(_f_s5_post, [y, u], post_params, [dout], [True, True], [True, True], tm,
                                       f's5_post_bwd_{l}')
    dh = _mm(dy, ccat, 'nt', f's5_out_dx_{l}')
    dccat = _mm(h, dy, 'tn', f's5_out_dw_{l}')
    hprev = jnp.pad(h[:-1], ((1, 0), (0, 0)))
    lr_, li_, dar, dai = _s5_scan(_to_segments(dh[:, :S5_LANES]), _to_segments(dh[:, S5_LANES:]), a_rows,
                                  f's5_scan_bwd_{l}', reverse=True, hpr=_to_segments(hprev[:, :S5_LANES]),
                                  hpi=_to_segments(hprev[:, S5_LANES:]))
    dx = jnp.concatenate([_from_segments(lr_), _from_segments(li_)], axis=1)
    du2 = _mm(dx, bb, 'nt', f's5_in_dx_{l}')
    dbb = _mm(u, dx, 'tn', f's5_in_dw_{l}')
    da_rows = jnp.pad(jnp.concatenate([dar, dai], axis=1), ((0, GROUP_W - SCAN_SEGMENTS), (0, 0)))
    (dbre, dbim), (dlr, dli, dlogdt) = _tile_bwd(_f_s5_prep, [bre, bim], prep_params, [dbb, da_rows], [True, True],
                                                 [True, True, True, False], GROUP_W, f's5_prep_bwd_{l}')
    grads = {
        's5_lambda_re': dlr.reshape(S5_G, S5_P), 's5_lambda_im': dli.reshape(S5_G, S5_P), 's5_log_dt': dlogdt[:, 0],
        's5_b_re': _diag_blocks(dbre, S5_CG, S5_P).transpose(0, 2, 1),
        's5_b_im': _diag_blocks(dbim, S5_CG, S5_P).transpose(0, 2, 1),
        's5_c_re': _diag_blocks(dccat[:S5_LANES], S5_P, S5_CG).transpose(0, 2, 1),
        's5_c_im': -_diag_blocks(dccat[S5_LANES:], S5_P, S5_CG).transpose(0, 2, 1),
        's5_d': dd[0], 's5_w_glu': dwglu}
    return du1 + du2, grads


DN_CONV = 4


def _head_sum_matrix():
    h = np.arange(GROUP_W) // HEAD_DIM
    return jnp.asarray((h[:, None] == h[None, :]).astype(np.float32))


def _f_dn_pre(x0, x1, x2, x3, ab, w0, w1, w2, w3, alog, dtb, ea, eb, hs):
    c = w0 * x0 + w1 * x1 + w2 * x2 + w3 * x3
    s = c * jax.nn.sigmoid(c)
    q, k, v = s[:, :GROUP_W], s[:, GROUP_W:2 * GROUP_W], s[:, 2 * GROUP_W:]
    q = q * lax.rsqrt(_hdot(q * q, hs) + EPS) * (HEAD_DIM ** -0.5)
    k = k * lax.rsqrt(_hdot(k * k, hs) + EPS)
    beta = jax.nn.sigmoid(_hdot(ab, eb))
    g = -jnp.exp(alog) * jax.nn.softplus(_hdot(ab, ea) + dtb)
    return q, k, v, g, beta


def _f_dn_chunk(q, k, v, g, beta):
    C = DN_CHUNK
    r = lax.broadcasted_iota(jnp.int32, (C, C), 0)
    c = lax.broadcasted_iota(jnp.int32, (C, C), 1)
    causal, strict = r >= c, r > c
    eye = (r == c).astype(f32)
    gc = _hdot(causal.astype(f32), g)
    glast = jnp.sum(g, axis=0, keepdims=True)
    eg = jnp.exp(gc)
    kb = k * beta
    ones = jnp.ones((C, GROUP_W), f32)
    w = jnp.zeros((C, GROUP_W), f32)
    u = jnp.zeros((C, GROUP_W), f32)
    a_qk = []
    for h in range(N_HEADS):
        hm = _head_mask(h)
        gcol = jnp.sum(gc * hm, axis=1, keepdims=True) * (1.0 / HEAD_DIM)
        grow = _hdot_nt(ones * (hm * (1.0 / HEAD_DIM)), gc)
        dec = jnp.exp(jnp.where(causal, gcol - grow, NEG_INF))
        lmat = jnp.where(strict, _hdot_nt(kb * hm, k) * dec, 0.0)
        t = eye - lmat
        p = lmat
        for _ in range(5):
            p = _hdot(p, p)
            t = t + _hdot(t, p)
        a_qk.append(jnp.where(causal, _hdot_nt(q * hm, k) * dec, 0.0))
        w = w + _hdot(t, kb * eg) * hm
        u = u + _hdot(t, v * beta) * hm
    return (w, u, q * eg, k * jnp.exp(glast - gc), *a_qk, jnp.broadcast_to(jnp.exp(glast), (C, GROUP_W)))


def _f_dn_step(w, u, qd, kdec, a0, a1, a2, a3, dfull, state, bd):
    row0 = (lax.broadcasted_iota(jnp.int32, dfull.shape, 0) == 0).astype(f32)
    dvec = jnp.sum(dfull * row0, axis=0, keepdims=True)
    vnew = u - _hdot(w, state)
    o = _hdot(qd, state)
    for h, a in enumerate((a0, a1, a2, a3)):
        o = o + _hdot(a, vnew) * _head_mask(h)
    return o, state * dvec + bd * _hdot_tn(kdec, vnew)


def _dn_scan_fwd(ins, name):
    S = ins[0].shape[0]
    N = S // DN_CHUNK
    bd = _head_sum_matrix()

    def body(*refs):
        o_ref, s_ref, state = refs[10], refs[11], refs[12]

        @pl.when(pl.program_id(0) == 0)
        def _():
            state[...] = jnp.zeros_like(state)

        s_in = state[...]
        s_ref[0] = s_in
        o, s_out = _f_dn_step(*[r[...] for r in refs[:9]], s_in, refs[9][...])
        o_ref[...] = o
        state[...] = s_out

    return pl.pallas_call(
        body, name=name, grid=(N,),
        in_specs=[pl.BlockSpec((DN_CHUNK, t.shape[1]), lambda n: (n, 0)) for t in ins] + [_full_spec(bd)],
        out_specs=[pl.BlockSpec((DN_CHUNK, GROUP_W), lambda n: (n, 0)), pl.BlockSpec((1, GROUP_W, GROUP_W), lambda n: (n, 0, 0))],
        out_shape=[jax.ShapeDtypeStruct((S, GROUP_W), f32), jax.ShapeDtypeStruct((N, GROUP_W, GROUP_W), f32)],
        scratch_shapes=[pltpu.VMEM((GROUP_W, GROUP_W), f32)],
        compiler_params=_cparams(('arbitrary',)),
    )(*ins, bd)


def _dn_scan_bwd(ins, states, do, name):
    S = ins[0].shape[0]
    N = S // DN_CHUNK
    bd = _head_sum_matrix()

    def body(*refs):
        s_ref, do_ref = refs[9], refs[10]
        bd_ref = refs[11]
        outs = refs[12:21]
        dstate = refs[21]

        @pl.when(pl.program_id(0) == 0)
        def _():
            dstate[...] = jnp.zeros_like(dstate)

        bd_val = bd_ref[...]
        _, vjp = jax.vjp(lambda *a: _f_dn_step(*a, bd_val), *[r[...] for r in refs[:9]], s_ref[0])
        grads = vjp((do_ref[...], dstate[...]))
        for o, g in zip(outs, grads[:9]):
            o[...] = g
        dstate[...] = grads[9]

    def rev(n):
        return (N - 1 - n, 0)

    res = pl.pallas_call(
        body, name=name, grid=(N,),
        in_specs=[pl.BlockSpec((DN_CHUNK, t.shape[1]), rev) for t in ins] +
                 [pl.BlockSpec((1, GROUP_W, GROUP_W), lambda n: (N - 1 - n, 0, 0)), pl.BlockSpec((DN_CHUNK, GROUP_W), rev),
                  _full_spec(bd)],
        out_specs=[pl.BlockSpec((DN_CHUNK, t.shape[1]), rev) for t in ins],
        out_shape=[jax.ShapeDtypeStruct(t.shape, f32) for t in ins],
        scratch_shapes=[pltpu.VMEM((GROUP_W, GROUP_W), f32)],
        compiler_params=_cparams(('arbitrary',)),
    )(*ins, states, do, bd)
    return list(res)


def _f_dn_post(o, gate, gain, hmean):
    return (o * lax.rsqrt(_hdot(o * o, hmean) + EPS) * gain * (gate * jax.nn.sigmoid(gate)),)


def _delay(t, j):
    return t if j == 0 else jnp.pad(t[:-j], ((j, 0), (0, 0)))


def _advance(t, j):
    return t if j == 0 else jnp.pad(t[j:], ((0, j), (0, 0)))


def _dn_fwd(qkv, a, b, gate, mp, l):
    S = qkv.shape[0]
    tm = _pick(S, (256, 128))
    xs = [_delay(qkv, DN_CONV - 1 - j) for j in range(DN_CONV)]
    ab = jnp.pad(jnp.concatenate([a, b], axis=1), ((0, 0), (0, LANES - 2 * N_HEADS)))
    sel = np.zeros((2, LANES, GROUP_W), np.float32)
    for h in range(N_HEADS):
        sel[0, h, h * HEAD_DIM:(h + 1) * HEAD_DIM] = 1.0
        sel[1, N_HEADS + h, h * HEAD_DIM:(h + 1) * HEAD_DIM] = 1.0
    pre_params = [*[mp['dn_conv'][j][None] for j in range(DN_CONV)], jnp.repeat(mp['dn_a_log'], HEAD_DIM)[None],
                  jnp.repeat(mp['dn_dt_bias'], HEAD_DIM)[None], jnp.asarray(sel[0]), jnp.asarray(sel[1]), _head_sum_matrix()]
    pre = _tile_fwd(_f_dn_pre, [*xs, ab], pre_params, [(GROUP_W, f32)] * 5, tm, f'dn_pre_fwd_{l}')
    chunk_outs = [(GROUP_W, f32)] * 4 + [(HEAD_DIM, f32)] * 4 + [(GROUP_W, f32)]
    parts = _tile_fwd(_f_dn_chunk, pre, [], chunk_outs, DN_CHUNK, f'dn_chunk_fwd_{l}')
    o, states = _dn_scan_fwd(parts, f'dn_scan_fwd_{l}')
    post_params = [jnp.tile(mp['dn_o_norm'], N_HEADS)[None], _head_mean_matrix()]
    (y,) = _tile_fwd(_f_dn_post, [o, gate], post_params, [(GROUP_W, f32)], tm, f'dn_post_fwd_{l}')
    return y, (xs, ab, pre_params, pre, parts, states, o, gate, post_params)


def _dn_bwd(dy, saved, l):
    xs, ab, pre_params, pre, parts, states, o, gate, post_params = saved
    S = dy.shape[0]
    tm = _pick(S, (256, 128))
    (do, dgate), (dgain,) = _tile_bwd(_f_dn_post, [o, gate], post_params, [dy], [True, True], [True, False], tm,
                                      f'dn_post_bwd_{l}')
    dparts = _dn_scan_bwd(parts, states, do, f'dn_scan_bwd_{l}')
    dpre, _ = _tile_bwd(_f_dn_chunk, pre, [], dparts, [True] * 5, [], DN_CHUNK, f'dn_chunk_bwd_{l}')
    dins, dpar = _tile_bwd(_f_dn_pre, [*xs, ab], pre_params, dpre, [True] * 5, [True] * 6 + [False] * 3, tm,
                           f'dn_pre_bwd_{l}')
    dqkv = dins[DN_CONV - 1]
    for j in range(DN_CONV - 1):
        dqkv = dqkv + _advance(dins[j], DN_CONV - 1 - j)
    dab = dins[DN_CONV]
    grads = {'dn_conv': jnp.concatenate(dpar[:DN_CONV], axis=0),
             'dn_a_log': dpar[4].reshape(N_HEADS, HEAD_DIM).sum(1), 'dn_dt_bias': dpar[5].reshape(N_HEADS, HEAD_DIM).sum(1),
             'dn_o_norm': dgain.reshape(N_HEADS, HEAD_DIM).sum(0)}
    return dqkv, dab[:, :N_HEADS], dab[:, N_HEADS:2 * N_HEADS], dgate, grads


def _t5_bucket(dist):
    exact = T5_BUCKETS // 2
    df = jnp.maximum(dist, 1).astype(f32)
    large = exact + (jnp.log(df / exact) / math.log(T5_MAX_DIST / exact) * (T5_BUCKETS - exact)).astype(jnp.int32)
    large = jnp.minimum(large, T5_BUCKETS - 1)
    return jnp.where(dist < exact, dist, large)


def _split_cols(t, sizes):
    out, start = [], 0
    for s in sizes:
        out.append(t[..., start:start + s])
        start += s
    return out


def _mixers_fwd(proj, mp, l):
    c_q, c_kv, k_rope, u_s5, qkv_dil, qkv_dn, a_dn, b_dn, gate_dn = _split_cols(proj, IN_SPLITS)
    y_mla, s_mla = _mla_fwd(c_q, c_kv, k_rope, mp, l)
    y_s5, s_s5 = _s5_fwd(u_s5, mp, l)
    y_dil, s_dil = _dil_fwd(qkv_dil, mp, l)
    y_dn, s_dn = _dn_fwd(qkv_dn, a_dn, b_dn, gate_dn, mp, l)
    return jnp.concatenate([y_mla, y_s5, y_dil, y_dn], axis=-1), (s_mla, s_s5, s_dil, s_dn)


def _mixers_bwd(dmixed, saved, l):
    s_mla, s_s5, s_dil, s_dn = saved
    d_mla, d_s5, d_dil, d_dn = _split_cols(dmixed, (GROUP_W,) * 4)
    dc_q, dc_kv, dk_rope, g_mla = _mla_bwd(d_mla, s_mla, l)
    du, g_s5 = _s5_bwd(d_s5, s_s5, l)
    dqkv_dil, g_dil = _dil_bwd(d_dil, s_dil, l)
    dqkv_dn, da, db, dgate, g_dn = _dn_bwd(d_dn, s_dn, l)
    dproj = jnp.concatenate([dc_q, dc_kv, dk_rope, du, dqkv_dil, dqkv_dn, da, db, dgate], axis=-1)
    return dproj, {**g_mla, **g_s5, **g_dil, **g_dn}


MIXER_PARAMS = ['mla_q_norm', 'mla_kv_norm', 'mla_w_uq', 'mla_w_ukv', 'mla_qk_q', 'mla_qk_k', 's5_lambda_re',
                's5_lambda_im', 's5_log_dt', 's5_b_re', 's5_b_im', 's5_c_re', 's5_c_im', 's5_d', 's5_w_glu',
                'dil_q_norm', 'dil_k_norm', 't5_bias', 'dn_conv', 'dn_a_log', 'dn_dt_bias', 'dn_o_norm']


def _layer_fwd(h, W, l):
    S = h.shape[0]
    tm = _pick(S, (256, 128))
    g1 = W['attn_norm'][l][None]
    g2 = W['ffn_norm'][l][None]
    (n1,) = _tile_fwd(_f_rms, [h], [g1], [(D_MODEL, bf16)], tm, f'rms1_fwd_{l}')
    proj = _mm(n1, W['w_in'][l], 'nn', f'proj_fwd_{l}')
    mp = {k: (W[k] if k == 't5_bias' else W[k][l]).astype(f32) for k in MIXER_PARAMS}
    mixed, mix_saved = _mixers_fwd(proj, mp, l)
    mixed_b = mixed.astype(bf16)
    h2 = _mm(mixed_b, W['w_out'][l], 'nn', f'out_fwd_{l}', add=h)
    (n2,) = _tile_fwd(_f_rms, [h2], [g2], [(D_MODEL, bf16)], tm, f'rms2_fwd_{l}')
    w13 = jnp.concatenate([W['ffn_w1'][l], W['ffn_w3'][l]], axis=1)
    uv = _mm(n2, w13, 'nn', f'ffn13_fwd_{l}')
    (act,) = _tile_fwd(_f_swiglu, [uv], [], [(FFN_HIDDEN, bf16)], tm, f'swiglu_fwd_{l}')
    h3 = _mm(act, W['ffn_w2'][l], 'nn', f'ffn2_fwd_{l}', add=h2)
    saved = dict(h=h, n1=n1, mix=mix_saved, mixed=mixed_b, h2=h2, n2=n2, uv=uv, act=act, w13=w13)
    return h3, saved


def _layer_bwd(dh3, saved, W, l):
    S = dh3.shape[0]
    tm = _pick(S, (256, 128))
    g1 = W['attn_norm'][l][None]
    g2 = W['ffn_norm'][l][None]
    grads = {}
    dact = _mm(dh3, W['ffn_w2'][l], 'nt', f'ffn2_dx_{l}')
    grads['ffn_w2'] = _mm(saved['act'], dh3, 'tn', f'ffn2_dw_{l}')
    (duv,), _ = _tile_bwd(_f_swiglu, [saved['uv']], [], [dact], [True], [], tm, f'swiglu_bwd_{l}', dt_dtypes=[bf16])
    dn2 = _mm(duv, saved['w13'], 'nt', f'ffn13_dx_{l}')
    dw13 = _mm(saved['n2'], duv, 'tn', f'ffn13_dw_{l}')
    grads['ffn_w1'], grads['ffn_w3'] = dw13[:, :FFN_HIDDEN], dw13[:, FFN_HIDDEN:]
    (dh2n,), (dg2,) = _tile_bwd(_f_rms, [saved['h2']], [g2], [dn2], [True], [True], tm, f'rms2_bwd_{l}')
    grads['ffn_norm'] = dg2[0]
    dh2 = dh3 + dh2n
    dmixed = _mm(dh2, W['w_out'][l], 'nt', f'out_dx_{l}')
    grads['w_out'] = _mm(saved['mixed'], dh2, 'tn', f'out_dw_{l}')
    dproj, dmp = _mixers_bwd(dmixed, saved['mix'], l)
    for k in MIXER_PARAMS:
        grads[k] = dmp[k]
    dn1 = _mm(dproj, W['w_in'][l], 'nt', f'proj_dx_{l}')
    grads['w_in'] = _mm(saved['n1'], dproj, 'tn', f'proj_dw_{l}')
    (dh1n,), (dg1,) = _tile_bwd(_f_rms, [saved['h']], [g1], [dn1], [True], [True], tm, f'rms1_bwd_{l}')
    grads['attn_norm'] = dg1[0]
    return dh2 + dh1n, grads


def kernel(x, attn_norm, w_in, w_out, mla_q_norm, mla_kv_norm, mla_w_uq, mla_w_ukv, mla_qk_q, mla_qk_k, s5_lambda_re, s5_lambda_im, s5_log_dt, s5_b_re, s5_b_im, s5_c_re, s5_c_im, s5_d, s5_w_glu, dil_q_norm, dil_k_norm, t5_bias, dn_conv, dn_a_log, dn_dt_bias, dn_o_norm, ffn_norm, ffn_w1, ffn_w3, ffn_w2, loss_target, m_attn_norm, m_w_in, m_w_out, m_mla_q_norm, m_mla_kv_norm, m_mla_w_uq, m_mla_w_ukv, m_mla_qk_q, m_mla_qk_k, m_s5_lambda_re, m_s5_lambda_im, m_s5_log_dt, m_s5_b_re, m_s5_b_im, m_s5_c_re, m_s5_c_im, m_s5_d, m_s5_w_glu, m_dil_q_norm, m_dil_k_norm, m_t5_bias, m_dn_conv, m_dn_a_log, m_dn_dt_bias, m_dn_o_norm, m_ffn_norm, m_ffn_w1, m_ffn_w3, m_ffn_w2, v_attn_norm, v_w_in, v_w_out, v_mla_q_norm, v_mla_kv_norm, v_mla_w_uq, v_mla_w_ukv, v_mla_qk_q, v_mla_qk_k, v_s5_lambda_re, v_s5_lambda_im, v_s5_log_dt, v_s5_b_re, v_s5_b_im, v_s5_c_re, v_s5_c_im, v_s5_d, v_s5_w_glu, v_dil_q_norm, v_dil_k_norm, v_t5_bias, v_dn_conv, v_dn_a_log, v_dn_dt_bias, v_dn_o_norm, v_ffn_norm, v_ffn_w1, v_ffn_w3, v_ffn_w2):
    given = dict(locals())
    w_loc = {n: given[n] for n in WEIGHTS}
    m_loc = {n: given['m_' + n] for n in WEIGHTS}
    v_loc = {n: given['v_' + n] for n in WEIGHTS}
    big_names = list(BIG)

    blk_shapes = [w_loc[n].shape for n in big_names]
    n_big = sum(math.prod(s) for s in blk_shapes)
    R = _pack_rows(n_big, 32)
    gathered = _gather_weights(_pack([w_loc[n] for n in big_names], R, bf16))
    parts = [_unpack(gathered[s], blk_shapes) for s in range(N_SHARDS)]
    W = {n: jnp.concatenate([parts[s][i] for s in range(N_SHARDS)], axis=BIG[n]) for i, n in enumerate(big_names)}
    for n in SMALL:
        W[n] = w_loc[n]

    h = x[0]
    saved = []
    for l in range(DEPTH):
        h, sv = _layer_fwd(h, W, l)
        saved.append(sv)
    parts_loss, dh = _loss_head(h, loss_target[0])
    loss = lax.psum(jnp.sum(parts_loss), ('x', 'y', 'c'))

    layer_grads = [None] * DEPTH
    for l in reversed(range(DEPTH)):
        dh, layer_grads[l] = _layer_bwd(dh, saved[l], W, l)
    grad_x = dh[None]
    full = {}
    for n in WEIGHTS:
        if n == 't5_bias':
            full[n] = layer_grads[0][n] + layer_grads[1][n]
        else:
            full[n] = jnp.stack([layer_grads[l][n] for l in range(DEPTH)])

    big_pack = jnp.stack([_pack([_shard_slice(n, full[n], s) for n in big_names], R, bf16) for s in range(N_SHARDS)])
    small_shapes = [w_loc[n].shape for n in SMALL]
    Rs = _pack_rows(sum(math.prod(s) for s in small_shapes), 8)
    small_pack = _pack([full[n] for n in SMALL], Rs, f32)
    recv_big, recv_small = _sibling_swap(big_pack, small_pack)
    chip_big = _chip_sum(big_pack, recv_big)
    chip_small = _small_chip_sum(small_pack, recv_small)
    from_chips_big, from_chips_small = _chip_exchange(chip_big, chip_small)
    my_half = _shard_total(big_pack, recv_big, from_chips_big)
    g_big = _unpack(_sibling_join(my_half), blk_shapes)

    g_small_p, d_small_p, m_small_p, v_small_p = _small_update(
        small_pack, recv_small, from_chips_small, _pack([w_loc[n] for n in SMALL], Rs, f32),
        _pack([m_loc[n] for n in SMALL], Rs, f32), _pack([v_loc[n] for n in SMALL], Rs, f32))
    grad, delta, new_m, new_v = {}, {}, {}, {}
    for n, g_, d_, m_, v_ in zip(SMALL, _unpack(g_small_p, small_shapes), _unpack(d_small_p, small_shapes),
                                 _unpack(m_small_p, small_shapes), _unpack(v_small_p, small_shapes)):
        grad[n], delta[n], new_m[n], new_v[n] = g_, d_, m_, v_
    for n, g_ in zip(big_names, g_big):
        grad[n] = g_
        delta[n], new_m[n], new_v[n] = _adamw(w_loc[n], g_, m_loc[n], v_loc[n], 'adamw_' + n)
    return (loss, grad_x, *[grad[n] for n in WEIGHTS], *[delta[n] for n in WEIGHTS],
            *[new_m[n] for n in WEIGHTS], *[new_v[n] for n in WEIGHTS])
```

```python
import functools
import math

import numpy as np
import jax
import jax.numpy as jnp
from jax import lax
from jax.experimental import pallas as pl
from jax.experimental.pallas import tpu as pltpu

f32 = jnp.float32
bf16 = jnp.bfloat16
HI = lax.Precision.HIGHEST
MESH = pl.DeviceIdType.MESH

VMEM_LIMIT_BYTES = 48 * 1024 * 1024
MM_VMEM_BUDGET_BYTES = 32 * 1024 * 1024
LANES = 128

D_MODEL = 1024
DEPTH = 2
GROUP_W = 256
HEAD_DIM = 64
EPS = 1e-6
NEG_INF = -1e30
N_HEADS = 4
MLA_NOPE, MLA_ROPE = 64, 32
MLA_DQK = MLA_NOPE + MLA_ROPE
ROPE_THETA = 10000.0
Q_BLOCK = 128
S5_G, S5_CG, S5_P = 16, 16, 64
DIL_PAIRS = ((128, 1), (512, 4), (2048, 16))
T5_BUCKETS, T5_MAX_DIST = 32, 2048
DN_CHUNK = 64
FFN_HIDDEN = 2816
IN_SPLITS = (256, 128, 32, 256, 768, 768, 4, 4, 256)
IN_COLS = sum(IN_SPLITS)

ADAM_LR, ADAM_B1, ADAM_B2, ADAM_EPS, ADAM_WD, ADAM_STEP = 0.001, 0.9, 0.999, 1e-08, 0.01, 10

WEIGHTS = ['attn_norm', 'w_in', 'w_out', 'mla_q_norm', 'mla_kv_norm', 'mla_w_uq', 'mla_w_ukv', 'mla_qk_q', 'mla_qk_k',
           's5_lambda_re', 's5_lambda_im', 's5_log_dt', 's5_b_re', 's5_b_im', 's5_c_re', 's5_c_im', 's5_d', 's5_w_glu',
           'dil_q_norm', 'dil_k_norm', 't5_bias', 'dn_conv', 'dn_a_log', 'dn_dt_bias', 'dn_o_norm', 'ffn_norm',
           'ffn_w1', 'ffn_w3', 'ffn_w2']
BIG = {'w_in': 2, 'w_out': 1, 'mla_w_uq': 2, 'mla_w_ukv': 2, 's5_w_glu': 2, 'dn_conv': 2, 'ffn_w1': 2, 'ffn_w3': 2,
       'ffn_w2': 1}
SMALL = [n for n in WEIGHTS if n not in BIG]
N_SHARDS = 4
PACK_COLS = 1024


def _cparams(sem=None, big=False):
    kw = {}
    if sem is not None:
        kw['dimension_semantics'] = sem
    if big:
        kw['vmem_limit_bytes'] = VMEM_LIMIT_BYTES
    return pltpu.CompilerParams(**kw)


def _pick(n, prefs):
    for p in prefs:
        if p <= n and n % p == 0:
            return p
    return n


def _lane_tile(n, cap):
    for t in range(cap - cap % LANES, 0, -LANES):
        if n % t == 0:
            return t
    return n


def _mm(a, b, mode, name, add=None, out_dtype=f32):
    if mode == 'nn':
        (M, K), (K2, N) = a.shape, b.shape
    elif mode == 'nt':
        (M, K), (N, K2) = a.shape, b.shape
    else:
        (K, M), (K2, N) = a.shape, b.shape
    assert K == K2, (name, a.shape, b.shape)
    tk = K if K <= 2816 else _pick(K, (2816, 2048, 1408, 1024, 512))
    cap_m, cap_n = (1408 if mode == 'tn' else 512), 1408

    def need(tm_, tn_):
        per_step = tm_ * tk * a.dtype.itemsize + tk * tn_ * b.dtype.itemsize + tm_ * tn_ * jnp.dtype(out_dtype).itemsize
        if add is not None:
            per_step += tm_ * tn_ * add.dtype.itemsize
        return 2 * per_step + tm_ * tn_ * 4

    tm, tn = _lane_tile(M, cap_m), _lane_tile(N, cap_n)
    while need(tm, tn) > MM_VMEM_BUDGET_BYTES and cap_m > LANES:
        cap_m //= 2
        tm = _lane_tile(M, cap_m)
    nk = K // tk
    dims = {'nn': (((1,), (0,)), ((), ())), 'nt': (((1,), (1,)), ((), ())), 'tn': (((0,), (0,)), ((), ()))}[mode]
    has_add = add is not None

    def body(*refs):
        a_ref, b_ref = refs[0], refs[1]
        add_ref = refs[2] if has_add else None
        o_ref = refs[3] if has_add else refs[2]
        part = lax.dot_general(a_ref[...].astype(bf16), b_ref[...].astype(bf16), dims, preferred_element_type=f32)
        if nk == 1:
            if has_add:
                part = part + add_ref[...].astype(f32)
            o_ref[...] = part.astype(out_dtype)
        else:
            acc_ref = refs[-1]
            k = pl.program_id(2)

            @pl.when(k == 0)
            def _():
                acc_ref[...] = part

            @pl.when(k > 0)
            def _():
                acc_ref[...] += part

            @pl.when(k == nk - 1)
            def _():
                r = acc_ref[...]
                if has_add:
                    r = r + add_ref[...].astype(f32)
                o_ref[...] = r.astype(out_dtype)

    if mode == 'nn':
        a_spec = pl.BlockSpec((tm, tk), lambda i, j, k: (i, k))
        b_spec = pl.BlockSpec((tk, tn), lambda i, j, k: (k, j))
    elif mode == 'nt':
        a_spec = pl.BlockSpec((tm, tk), lambda i, j, k: (i, k))
        b_spec = pl.BlockSpec((tn, tk), lambda i, j, k: (j, k))
    else:
        a_spec = pl.BlockSpec((tk, tm), lambda i, j, k: (k, i))
        b_spec = pl.BlockSpec((tk, tn), lambda i, j, k: (k, j))
    in_specs = [a_spec, b_spec]
    args = [a, b]
    if has_add:
        in_specs.append(pl.BlockSpec((tm, tn), lambda i, j, k: (i, j)))
        args.append(add)
    return pl.pallas_call(
        body, name=name, grid=(M // tm, N // tn, nk), in_specs=in_specs,
        out_specs=pl.BlockSpec((tm, tn), lambda i, j, k: (i, j)),
        out_shape=jax.ShapeDtypeStruct((M, N), out_dtype),
        scratch_shapes=[pltpu.VMEM((tm, tn), f32)] if nk > 1 else [],
        compiler_params=_cparams(('parallel', 'parallel', 'arbitrary'), big=True),
    )(*args)


def _full_spec(p):
    nd = p.ndim
    return pl.BlockSpec(p.shape, lambda i, _nd=nd: (0,) * _nd)


def _tile_fwd(f, tiled, params, outs, tm, name):
    S = tiled[0].shape[0]
    nt, npar = len(tiled), len(params)

    def body(*refs):
        vals = [r[...].astype(f32) for r in refs[:nt + npar]]
        res = f(*vals)
        for r, o in zip(res, refs[nt + npar:]):
            o[...] = r.astype(o.dtype)

    return pl.pallas_call(
        body, name=name, grid=(S // tm,),
        in_specs=[pl.BlockSpec((tm, t.shape[1]), lambda i: (i, 0)) for t in tiled] + [_full_spec(p) for p in params],
        out_specs=[pl.BlockSpec((tm, c), lambda i: (i, 0)) for c, _ in outs],
        out_shape=[jax.ShapeDtypeStruct((S, c), dt) for c, dt in outs],
        compiler_params=_cparams(('parallel',), big=True),
    )(*tiled, *params)


def _tile_bwd(f, tiled, params, cts, diff_t, diff_p, tm, name, dt_dtypes=None):
    S = tiled[0].shape[0]
    nt, npar, nc = len(tiled), len(params), len(cts)
    it = [i for i in range(nt) if diff_t[i]]
    ip = [i for i in range(npar) if diff_p[i]]
    if dt_dtypes is None:
        dt_dtypes = [f32] * len(it)

    def body(*refs):
        vals = [r[...].astype(f32) for r in refs[:nt + npar]]
        ct_vals = tuple(r[...].astype(f32) for r in refs[nt + npar:nt + npar + nc])
        out_refs = refs[nt + npar + nc:]

        def g(*dv):
            full = list(vals)
            for k, i in enumerate(it):
                full[i] = dv[k]
            for k, i in enumerate(ip):
                full[nt + i] = dv[len(it) + k]
            return tuple(f(*full))

        _, vjp = jax.vjp(g, *[vals[i] for i in it], *[vals[nt + i] for i in ip])
        grads = vjp(ct_vals)
        for k in range(len(it)):
            out_refs[k][...] = grads[k].astype(out_refs[k].dtype)
        step = pl.program_id(0)
        for k in range(len(ip)):
            o = out_refs[len(it) + k]
            gk = grads[len(it) + k]

            @pl.when(step == 0)
            def _(o=o, gk=gk):
                o[...] = gk

            @pl.when(step > 0)
            def _(o=o, gk=gk):
                o[...] += gk

    out_specs = [pl.BlockSpec((tm, tiled[i].shape[1]), lambda i_: (i_, 0)) for i in it] + [_full_spec(params[i]) for i in ip]
    out_shape = [jax.ShapeDtypeStruct(tiled[i].shape, dt_dtypes[k]) for k, i in enumerate(it)] + \
                [jax.ShapeDtypeStruct(params[i].shape, f32) for i in ip]
    res = pl.pallas_call(
        body, name=name, grid=(S // tm,),
        in_specs=[pl.BlockSpec((tm, t.shape[1]), lambda i: (i, 0)) for t in tiled] + [_full_spec(p) for p in params] +
                 [pl.BlockSpec((tm, c.shape[1]), lambda i: (i, 0)) for c in cts],
        out_specs=out_specs, out_shape=out_shape,
        compiler_params=_cparams(('arbitrary',), big=True),
    )(*tiled, *params, *cts)
    return list(res[:len(it)]), list(res[len(it):])


def _rms(x, g):
    return x * lax.rsqrt(jnp.mean(x * x, axis=-1, keepdims=True) + EPS) * g


def _f_rms(x, g):
    return (_rms(x, g),)


def _f_swiglu(uv):
    h = uv.shape[1] // 2
    u, v = uv[:, :h], uv[:, h:]
    return (u * jax.nn.sigmoid(u) * v,)


def _loss_head(y, target):
    S, D = y.shape
    tm = _pick(S, (256, 128))

    def body(y_ref, t_ref, part_ref, dy_ref):
        e = y_ref[...] - t_ref[...]
        dy_ref[...] = e * (1.0 / D)
        s = 0.5 * jnp.sum(jnp.sum(e * e, axis=1, keepdims=True), axis=0, keepdims=True) * (1.0 / D)
        r = lax.broadcasted_iota(jnp.int32, (8, LANES), 0)
        c = lax.broadcasted_iota(jnp.int32, (8, LANES), 1)
        part_ref[0] = jnp.where((r == 0) & (c == 0), s, 0.0)

    return pl.pallas_call(
        body, name='loss_head', grid=(S // tm,),
        in_specs=[pl.BlockSpec((tm, D), lambda i: (i, 0))] * 2,
        out_specs=[pl.BlockSpec((1, 8, LANES), lambda i: (i, 0, 0)), pl.BlockSpec((tm, D), lambda i: (i, 0))],
        out_shape=[jax.ShapeDtypeStruct((S // tm, 8, LANES), f32), jax.ShapeDtypeStruct((S, D), f32)],
        compiler_params=_cparams(('parallel',)),
    )(y, target)


def _pack_rows(n_elems, mult):
    rows = -(-n_elems // PACK_COLS)
    return -(-rows // mult) * mult


def _pack(arrs, rows, dtype):
    flat = jnp.concatenate([a.astype(dtype).reshape(-1) for a in arrs])
    flat = jnp.pad(flat, (0, rows * PACK_COLS - flat.shape[0]))
    return flat.reshape(rows, PACK_COLS)


def _unpack(pack, shapes):
    flat = pack.reshape(-1)
    out, off = [], 0
    for s in shapes:
        n = math.prod(s)
        out.append(flat[off:off + n].reshape(s))
        off += n
    return out


ANY = pl.BlockSpec(memory_space=pl.ANY)


def _place():
    return lax.axis_index('x'), lax.axis_index('y'), lax.axis_index('c')


def _where():
    return jnp.stack([lax.axis_index('c'), 2 * lax.axis_index('x') + lax.axis_index('y')]).astype(jnp.int32)


def _remote(src, dst, send_sems, recv_sems, k, to):
    return pltpu.make_async_remote_copy(src_ref=src, dst_ref=dst, send_sem=send_sems.at[k], recv_sem=recv_sems.at[k],
                                        device_id=to, device_id_type=MESH)


def _gather_tensors(ws):
    n = len(ws)

    def body(*refs):
        w_refs, g_refs = refs[:n], refs[n:2 * n]
        send_sems, recv_sems, local_sems = refs[2 * n:]
        x, y, c = _place()
        me, sib = (x, y, c), (x, y, 1 - c)
        chips = [(1 - x, y), (x, 1 - y), (1 - x, 1 - y)]
        mine = [pltpu.make_async_copy(w_refs[t], g_refs[t].at[2 * x + y], local_sems.at[t]) for t in range(n)]
        for cp in mine:
            cp.start()
        first = [_remote(w_refs[t].at[c], g_refs[t].at[2 * x + y, c], send_sems, recv_sems, 6 * t + j, (px, py, c))
                 for j, (px, py) in enumerate(chips) for t in range(n)]
        for cp in first:
            cp.start()
        passed = []
        for j, (px, py) in enumerate(chips):
            for t in range(n):
                here = g_refs[t].at[2 * px + py, c]
                _remote(here, here, send_sems, recv_sems, 6 * t + j, me).wait_recv()
                cp = _remote(here, here, send_sems, recv_sems, 6 * t + 3 + j, sib)
                cp.start()
                passed.append(cp)
        for j, (px, py) in enumerate(chips):
            for t in range(n):
                there = g_refs[t].at[2 * px + py, 1 - c]
                _remote(there, there, send_sems, recv_sems, 6 * t + 3 + j, me).wait_recv()
        for cp in first + passed:
            cp.wait_send()
        for cp in mine:
            cp.wait()

    return pl.pallas_call(
        body, name='gather_weights', in_specs=[ANY] * n, out_specs=[ANY] * n,
        out_shape=[jax.ShapeDtypeStruct((N_SHARDS,) + w.shape, w.dtype) for w in ws],
        scratch_shapes=[pltpu.SemaphoreType.DMA((6 * n,)), pltpu.SemaphoreType.DMA((6 * n,)), pltpu.SemaphoreType.DMA((n,))],
    )(*ws)


def _swap_with_sibling(gs, small):
    n = len(gs)

    def body(*refs):
        g_refs, s_ref = refs[:n], refs[n]
        r_refs, rs_ref = refs[n + 1:2 * n + 1], refs[2 * n + 1]
        send_sems, recv_sems = refs[2 * n + 2:]
        x, y, c = _place()
        sib = (x, y, 1 - c)
        cps = [_remote(g_refs[t].at[:, 1 - c], r_refs[t], send_sems, recv_sems, t, sib) for t in range(n)]
        cps.append(_remote(s_ref, rs_ref, send_sems, recv_sems, n, sib))
        for cp in cps:
            cp.start()
        for cp in cps:
            cp.wait()

    res = pl.pallas_call(
        body, name='swap_with_sibling', in_specs=[ANY] * (n + 1), out_specs=[ANY] * (n + 1),
        out_shape=[jax.ShapeDtypeStruct((N_SHARDS,) + g.shape[2:], g.dtype) for g in gs] +
                  [jax.ShapeDtypeStruct(small.shape, small.dtype)],
        scratch_shapes=[pltpu.SemaphoreType.DMA((n + 1,)), pltpu.SemaphoreType.DMA((n + 1,))],
    )(*gs, small)
    return list(res[:n]), res[n]


def _exchange_between_chips(cs, small):
    n = len(cs)

    def body(*refs):
        c_refs, s_ref = refs[:n], refs[n]
        r_refs, rs_ref = refs[n + 1:2 * n + 1], refs[2 * n + 1]
        send_sems, recv_sems = refs[2 * n + 2:]
        x, y, c = _place()
        chips = [(1 - x, y), (x, 1 - y), (1 - x, 1 - y)]
        cps = []
        for j, (px, py) in enumerate(chips):
            for t in range(n):
                cps.append(_remote(c_refs[t].at[2 * px + py], r_refs[t].at[j], send_sems, recv_sems, 3 * t + j, (px, py, c)))
            cps.append(_remote(s_ref, rs_ref.at[j], send_sems, recv_sems, 3 * n + j, (px, py, c)))
        for cp in cps:
            cp.start()
        for cp in cps:
            cp.wait()

    res = pl.pallas_call(
        body, name='exchange_between_chips', in_specs=[ANY] * (n + 1), out_specs=[ANY] * (n + 1),
        out_shape=[jax.ShapeDtypeStruct((3,) + c.shape[1:], c.dtype) for c in cs] +
                  [jax.ShapeDtypeStruct((3,) + small.shape, small.dtype)],
        scratch_shapes=[pltpu.SemaphoreType.DMA((3 * n + 3,)), pltpu.SemaphoreType.DMA((3 * n + 3,))],
    )(*cs, small)
    return list(res[:n]), res[n]


def _join_with_sibling(ts):
    n = len(ts)

    def body(*refs):
        t_refs, o_refs = refs[:n], refs[n:2 * n]
        send_sems, recv_sems, local_sems = refs[2 * n:]
        x, y, c = _place()
        sib = (x, y, 1 - c)
        mine = [pltpu.make_async_copy(t_refs[t], o_refs[t].at[c], local_sems.at[t]) for t in range(n)]
        sends = [_remote(t_refs[t], o_refs[t].at[c], send_sems, recv_sems, t, sib) for t in range(n)]
        for cp in mine + sends:
            cp.start()
        for t in range(n):
            sends[t].wait_send()
            _remote(t_refs[t], o_refs[t].at[1 - c], send_sems, recv_sems, t, sib).wait_recv()
        for cp in mine:
            cp.wait()

    return pl.pallas_call(
        body, name='join_with_sibling', in_specs=[ANY] * n, out_specs=[ANY] * n,
        out_shape=[jax.ShapeDtypeStruct((2,) + t.shape, t.dtype) for t in ts],
        scratch_shapes=[pltpu.SemaphoreType.DMA((n,)), pltpu.SemaphoreType.DMA((n,)), pltpu.SemaphoreType.DMA((n,))],
    )(*ts)


def _row_tile(a):
    return _pick(a, (512, 256, 128, 64, 32, 16, 8))


def _chip_sum_of(g, r, name):
    _, _, a, b = g.shape
    tr = _row_tile(a)

    def body(w_ref, g_ref, r_ref, o_ref):
        o_ref[...] = (g_ref[0].astype(f32) + r_ref[...].astype(f32)).astype(o_ref.dtype)

    return pl.pallas_call(
        body, name=name,
        grid_spec=pltpu.PrefetchScalarGridSpec(
            num_scalar_prefetch=1, grid=(N_SHARDS, a // tr),
            in_specs=[pl.BlockSpec((1, 1, tr, b), lambda s, i, w: (s, w[0], i, 0)),
                      pl.BlockSpec((1, tr, b), lambda s, i, w: (s, i, 0))],
            out_specs=pl.BlockSpec((1, tr, b), lambda s, i, w: (s, i, 0))),
        out_shape=jax.ShapeDtypeStruct((N_SHARDS, a, b), bf16),
        compiler_params=_cparams(('parallel', 'parallel')),
    )(_where(), g, r)


def _shard_total_of(g, r, rc, name):
    _, _, a, b = g.shape
    tr = _row_tile(a)

    def body(w_ref, g_ref, r_ref, rc_ref, o_ref):
        t = g_ref[0, 0].astype(f32) + r_ref[0].astype(f32)
        t = t + rc_ref[0].astype(f32)
        t = t + rc_ref[1].astype(f32)
        t = t + rc_ref[2].astype(f32)
        o_ref[...] = t

    return pl.pallas_call(
        body, name=name,
        grid_spec=pltpu.PrefetchScalarGridSpec(
            num_scalar_prefetch=1, grid=(a // tr,),
            in_specs=[pl.BlockSpec((1, 1, tr, b), lambda i, w: (w[1], w[0], i, 0)),
                      pl.BlockSpec((1, tr, b), lambda i, w: (w[1], i, 0)),
                      pl.BlockSpec((3, tr, b), lambda i, w: (0, i, 0))],
            out_specs=pl.BlockSpec((tr, b), lambda i, w: (i, 0))),
        out_shape=jax.ShapeDtypeStruct((a, b), f32),
        compiler_params=_cparams(('parallel',)),
    )(_where(), g, r, rc)


def _by_shard(name, t):
    r, c = t.shape
    if BIG[name] == 2:
        return t.reshape(r, N_SHARDS, c // N_SHARDS).transpose(1, 0, 2)
    return t.reshape(N_SHARDS, r // N_SHARDS, c)


def _from_shards(name, g):
    s, l, a, b = g.shape
    if BIG[name] == 2:
        return g.transpose(1, 2, 0, 3).reshape(l, a, s * b)
    return g.transpose(1, 0, 2, 3).reshape(l, s * a, b)


def _adam_math(w, g, m, v):
    m = ADAM_B1 * m + (1.0 - ADAM_B1) * g
    v = ADAM_B2 * v + (1.0 - ADAM_B2) * (g * g)
    m_hat = m / (1.0 - ADAM_B1 ** ADAM_STEP)
    v_hat = v / (1.0 - ADAM_B2 ** ADAM_STEP)
    delta = -ADAM_LR * (m_hat / (jnp.sqrt(v_hat) + ADAM_EPS) + ADAM_WD * w)
    return delta, m, v


def _small_update(own, sib, chips, w, m, v):
    def body(o_ref, s_ref, c_ref, w_ref, m_ref, v_ref, g_out, d_out, m_out, v_out):
        chip = o_ref[...] + s_ref[...]
        g = (chip + c_ref[0]) + (c_ref[1] + c_ref[2])
        d, mn, vn = _adam_math(w_ref[...], g, m_ref[...], v_ref[...])
        g_out[...] = g
        d_out[...] = d
        m_out[...] = mn
        v_out[...] = vn

    return pl.pallas_call(body, name='small_update', out_shape=[jax.ShapeDtypeStruct(own.shape, f32)] * 4)(
        own, sib, chips, w, m, v)


def _small_chip_sum(own, sib):
    def body(o_ref, s_ref, out):
        out[...] = o_ref[...] + s_ref[...]
    return pl.pallas_call(body, name='small_chip_sum', out_shape=jax.ShapeDtypeStruct(own.shape, f32))(own, sib)


def _adamw(w, g, m, v, name):
    shape = w.shape
    w2, g2, m2, v2 = [t.reshape(-1, shape[-1]) for t in (w, g, m, v)]
    rows, cols = w2.shape
    tr = _pick(rows, (256, 128, 64, 32, 16, 8))

    def body(w_ref, g_ref, m_ref, v_ref, d_out, m_out, v_out):
        d, mn, vn = _adam_math(w_ref[...], g_ref[...], m_ref[...], v_ref[...])
        d_out[...] = d
        m_out[...] = mn
        v_out[...] = vn

    spec = pl.BlockSpec((tr, cols), lambda i: (i, 0))
    res = pl.pallas_call(body, name=name, grid=(rows // tr,), in_specs=[spec] * 4, out_specs=[spec] * 3,
                         out_shape=[jax.ShapeDtypeStruct((rows, cols), f32)] * 3,
                         compiler_params=_cparams(('parallel',)))(w2, g2, m2, v2)
    return [r.reshape(shape) for r in res]


def _dg(a, b, ca, cb):
    return lax.dot_general(a.astype(bf16), b.astype(bf16), (((ca,), (cb,)), ((), ())), preferred_element_type=f32)


@jax.custom_vjp
def _bmm(a, b):
    return _dg(a, b, 1, 0)


_bmm.defvjp(lambda a, b: (_dg(a, b, 1, 0), (a, b)), lambda r, g: (_dg(g, r[1], 1, 1), _dg(r[0], g, 0, 0)))


@jax.custom_vjp
def _bmm_nt(a, b):
    return _dg(a, b, 1, 1)


_bmm_nt.defvjp(lambda a, b: (_dg(a, b, 1, 1), (a, b)), lambda r, g: (_dg(g, r[1], 1, 0), _dg(g, r[0], 0, 0)))


@jax.custom_vjp
def _bmm_tn(a, b):
    return _dg(a, b, 0, 0)


_bmm_tn.defvjp(lambda a, b: (_dg(a, b, 0, 0), (a, b)), lambda r, g: (_dg(r[1], g, 1, 1), _dg(r[0], g, 1, 0)))


def _hdot(a, b):
    return jnp.dot(a, b, precision=HI, preferred_element_type=f32)


def _hdot_nt(a, b):
    return lax.dot_general(a, b, (((1,), (1,)), ((), ())), precision=HI, preferred_element_type=f32)


def _hdot_tn(a, b):
    return lax.dot_general(a, b, (((0,), (0,)), ((), ())), precision=HI, preferred_element_type=f32)


def _head_mask(h, width=GROUP_W):
    lane = lax.broadcasted_iota(jnp.int32, (1, width), 1)
    return ((lane >= h * HEAD_DIM) & (lane < (h + 1) * HEAD_DIM)).astype(f32)


def _rope_perm():
    p = np.zeros((LANES, LANES), np.float32)
    half = MLA_ROPE // 2
    for i in range(half):
        p[MLA_NOPE + half + i, MLA_NOPE + i] = -1.0
        p[MLA_NOPE + i, MLA_NOPE + half + i] = 1.0
    return jnp.asarray(p)


def _rope_tables(S):
    half = MLA_ROPE // 2
    freqs = ROPE_THETA ** (-jnp.arange(half, dtype=f32) / half)
    ang = jnp.arange(S, dtype=f32)[:, None] * freqs[None, :]
    cos, sin = jnp.cos(ang), jnp.sin(ang)
    ones, zeros = jnp.ones((S, MLA_NOPE), f32), jnp.zeros((S, LANES - MLA_DQK), f32)
    c_tab = jnp.concatenate([ones, cos, cos, zeros], axis=1)
    s_tab = jnp.concatenate([jnp.zeros((S, MLA_NOPE), f32), sin, sin, zeros], axis=1)
    return c_tab, s_tab


def _f_mla_pre(c_q, c_kv, krope, c_tab, s_tab, q_norm, kv_norm, wq0, wq1, wq2, wq3, wk0, wk1, wk2, wk3, wv, gq, gk, perm):
    wq, wk = (wq0, wq1, wq2, wq3), (wk0, wk1, wk2, wk3)
    nq = _rms(c_q, q_norm)
    nkv = _rms(c_kv, kv_norm)

    def norm_rope(t, g):
        t = t * lax.rsqrt(jnp.sum(t * t, axis=-1, keepdims=True) * (1.0 / MLA_DQK) + EPS) * g
        return t * c_tab + _hdot(t, perm) * s_tab

    qs = [norm_rope(_bmm(nq, wq[h]), gq) * (MLA_DQK ** -0.5) for h in range(N_HEADS)]
    ks = [norm_rope(_bmm(nkv, wk[h]) + krope, gk) for h in range(N_HEADS)]
    return (*qs, *ks, _bmm(nkv, wv))


def _f_attn(qs, ks, v, q0):
    tq, S = qs[0].shape[0], ks[0].shape[0]
    qpos = q0 + lax.broadcasted_iota(jnp.int32, (tq, S), 0)
    kpos = lax.broadcasted_iota(jnp.int32, (tq, S), 1)
    keep = kpos <= qpos
    out = jnp.zeros((tq, GROUP_W), f32)
    for h in range(N_HEADS):
        logits = jnp.where(keep, _bmm_nt(qs[h], ks[h]), NEG_INF)
        p = jnp.exp(logits - jnp.max(logits, axis=-1, keepdims=True))
        p = p / jnp.sum(p, axis=-1, keepdims=True)
        out = out + _bmm(p, v) * _head_mask(h)
    return out


def _mla_attn_fwd(qs, ks, v, name):
    S = v.shape[0]
    tq = Q_BLOCK

    def body(*refs):
        q_vals = [r[...] for r in refs[:4]]
        k_vals = [r[...] for r in refs[4:8]]
        refs[9][...] = _f_attn(q_vals, k_vals, refs[8][...], pl.program_id(0) * tq)

    qspec = pl.BlockSpec((tq, LANES), lambda i: (i, 0))
    return pl.pallas_call(
        body, name=name, grid=(S // tq,),
        in_specs=[qspec] * 4 + [_full_spec(k) for k in ks] + [_full_spec(v)],
        out_specs=pl.BlockSpec((tq, GROUP_W), lambda i: (i, 0)),
        out_shape=jax.ShapeDtypeStruct((S, GROUP_W), f32),
        compiler_params=_cparams(('parallel',), big=True),
    )(*qs, *ks, v)


def _mla_attn_bwd(qs, ks, v, do, name):
    S = v.shape[0]
    tq = Q_BLOCK

    def body(*refs):
        q_vals = [r[...].astype(f32) for r in refs[:4]]
        k_vals = [r[...].astype(f32) for r in refs[4:8]]
        v_val = refs[8][...].astype(f32)
        q0 = pl.program_id(0) * tq
        _, vjp = jax.vjp(lambda a, b, c: _f_attn(a, b, c, q0), q_vals, k_vals, v_val)
        dqs, dks, dv = vjp(refs[9][...])
        outs = refs[10:]
        for h in range(N_HEADS):
            outs[h][...] = dqs[h]
        first = pl.program_id(0) == 0
        for o, g in zip(outs[4:], (*dks, dv)):
            @pl.when(first)
            def _(o=o, g=g):
                o[...] = g

            @pl.when(jnp.logical_not(first))
            def _(o=o, g=g):
                o[...] += g

    qspec = pl.BlockSpec((tq, LANES), lambda i: (i, 0))
    res = pl.pallas_call(
        body, name=name, grid=(S // tq,),
        in_specs=[qspec] * 4 + [_full_spec(k) for k in ks] + [_full_spec(v), pl.BlockSpec((tq, GROUP_W), lambda i: (i, 0))],
        out_specs=[qspec] * 4 + [_full_spec(k) for k in ks] + [_full_spec(v)],
        out_shape=[jax.ShapeDtypeStruct((S, LANES), f32)] * 8 + [jax.ShapeDtypeStruct((S, GROUP_W), f32)],
        compiler_params=_cparams(('arbitrary',), big=True),
    )(*qs, *ks, v, do)
    return res[:4], res[4:8], res[8]


def _mla_params(mp):
    pad = LANES - MLA_DQK
    wq = jnp.pad(mp['mla_w_uq'].reshape(GROUP_W, N_HEADS, MLA_DQK).transpose(1, 0, 2), ((0, 0), (0, 0), (0, pad)))
    wkv = mp['mla_w_ukv'].reshape(LANES, N_HEADS, MLA_NOPE + HEAD_DIM)
    wk = jnp.pad(wkv[:, :, :MLA_NOPE].transpose(1, 0, 2), ((0, 0), (0, 0), (0, LANES - MLA_NOPE)))
    wv = wkv[:, :, MLA_NOPE:].reshape(LANES, GROUP_W)
    gq = jnp.pad(mp['mla_qk_q'], (0, pad))[None]
    gk = jnp.pad(mp['mla_qk_k'], (0, pad))[None]
    return [mp['mla_q_norm'][None], mp['mla_kv_norm'][None], *[wq[h] for h in range(N_HEADS)],
            *[wk[h] for h in range(N_HEADS)], wv, gq, gk, _rope_perm()]


def _mla_fwd(c_q, c_kv, k_rope, mp, l):
    S = c_q.shape[0]
    tm = _pick(S, (256, 128))
    krope = jnp.pad(k_rope, ((0, 0), (MLA_NOPE, LANES - MLA_DQK)))
    c_tab, s_tab = _rope_tables(S)
    tiled = [c_q, c_kv, krope, c_tab, s_tab]
    params = _mla_params(mp)
    res = _tile_fwd(_f_mla_pre, tiled, params, [(LANES, bf16)] * 8 + [(GROUP_W, bf16)], tm, f'mla_pre_fwd_{l}')
    qs, ks, v = res[:4], res[4:8], res[8]
    y = _mla_attn_fwd(qs, ks, v, f'mla_attn_fwd_{l}')
    return y, (tiled, params, qs, ks, v)


def _mla_bwd(dy, saved, l):
    tiled, params, qs, ks, v = saved
    S = dy.shape[0]
    tm = _pick(S, (256, 128))
    dqs, dks, dv = _mla_attn_bwd(qs, ks, v, dy, f'mla_attn_bwd_{l}')
    (dc_q, dc_kv, dkrope), dpar = _tile_bwd(_f_mla_pre, tiled, params, [*dqs, *dks, dv], [True, True, True, False, False],
                                            [True] * 13 + [False], tm, f'mla_pre_bwd_{l}')
    dqn, dkvn = dpar[0], dpar[1]
    dwq, dwk = jnp.stack(dpar[2:6]), jnp.stack(dpar[6:10])
    dwv, dgq, dgk = dpar[10:13]
    dw_uq = dwq[:, :, :MLA_DQK].transpose(1, 0, 2).reshape(GROUP_W, N_HEADS * MLA_DQK)
    dw_ukv = jnp.concatenate([dwk[:, :, :MLA_NOPE].transpose(1, 0, 2), dwv.reshape(LANES, N_HEADS, HEAD_DIM)],
                             axis=2).reshape(LANES, N_HEADS * (MLA_NOPE + HEAD_DIM))
    grads = {'mla_q_norm': dqn[0], 'mla_kv_norm': dkvn[0], 'mla_w_uq': dw_uq, 'mla_w_ukv': dw_ukv,
             'mla_qk_q': dgq[0, :MLA_DQK], 'mla_qk_k': dgk[0, :MLA_DQK]}
    return dc_q, dc_kv, dkrope[:, MLA_NOPE:MLA_DQK], grads


SPAN = 128


def _head_mean_matrix():
    h = np.arange(GROUP_W) // HEAD_DIM
    return jnp.asarray((h[:, None] == h[None, :]).astype(np.float32) / HEAD_DIM)


def _f_dil_pre(q, k, gq, gk, hm):
    qn = q * lax.rsqrt(_hdot(q * q, hm) + EPS) * gq * (HEAD_DIM ** -0.5)
    kn = k * lax.rsqrt(_hdot(k * k, hm) + EPS) * gk
    return qn, kn


def _f_dil_branch(qb, kp, kc, vp, vc, b0, b1, b2, b3, first):
    kcat = jnp.concatenate([kp, kc], axis=0)
    vcat = jnp.concatenate([vp, vc], axis=0)
    qi = lax.broadcasted_iota(jnp.int32, (SPAN, 2 * SPAN), 0) + SPAN
    kj = lax.broadcasted_iota(jnp.int32, (SPAN, 2 * SPAN), 1)
    delta = qi - kj
    valid = (delta >= 0) & (delta <= SPAN) & jnp.logical_not(first & (kj < SPAN))
    o = jnp.zeros((SPAN, GROUP_W), f32)
    m_full = jnp.zeros((SPAN, GROUP_W), f32)
    l_full = jnp.zeros((SPAN, GROUP_W), f32)
    for h, bias in enumerate((b0, b1, b2, b3)):
        hm = _head_mask(h)
        logits = jnp.where(valid, _bmm_nt(qb * hm, kcat) + bias, NEG_INF)
        m = jnp.max(logits, axis=-1, keepdims=True)
        p = jnp.exp(logits - m)
        o = o + _bmm(p, vcat) * hm
        m_full = m_full + m * hm
        l_full = l_full + jnp.sum(p, axis=-1, keepdims=True) * hm
    return o, m_full, l_full


def _dil_branch_specs(d, nb):
    cur = pl.BlockSpec((1, SPAN, GROUP_W), lambda r, n: (r, n, 0))
    prev = pl.BlockSpec((1, SPAN, GROUP_W), lambda r, n: (r, jnp.maximum(n - 1, 0), 0))
    bias = pl.BlockSpec((1, SPAN, 2 * SPAN), lambda r, n: (0, 0, 0))
    return cur, prev, bias


def _dil_branch_fwd(q, k, v, biases, name):
    d, L, _ = q.shape
    nb = L // SPAN
    cur, prev, bias = _dil_branch_specs(d, nb)

    def body(q_ref, kp_ref, kc_ref, vp_ref, vc_ref, b0, b1, b2, b3, o_ref, m_ref, l_ref):
        o, m, l = _f_dil_branch(q_ref[0], kp_ref[0], kc_ref[0], vp_ref[0], vc_ref[0], b0[0], b1[0], b2[0], b3[0],
                                pl.program_id(1) == 0)
        o_ref[0] = o
        m_ref[0] = m
        l_ref[0] = l

    return pl.pallas_call(
        body, name=name, grid=(d, nb), in_specs=[cur, prev, cur, prev, cur] + [bias] * 4,
        out_specs=[cur] * 3, out_shape=[jax.ShapeDtypeStruct(q.shape, f32)] * 3,
        compiler_params=_cparams(('parallel', 'parallel')),
    )(q, k, k, v, v, *biases)


def _dil_branch_bwd(q, k, v, biases, do, dm, dl, name):
    d, L, _ = q.shape
    nb = L // SPAN
    cur, prev, bias = _dil_branch_specs(d, nb)
    whole = pl.BlockSpec((1, L, GROUP_W), lambda r, n: (r, 0, 0))

    def body(q_ref, kp_ref, kc_ref, vp_ref, vc_ref, b0, b1, b2, b3, do_ref, dm_ref, dl_ref,
             dq_ref, dk_ref, dv_ref, db0, db1, db2, db3):
        r, n = pl.program_id(0), pl.program_id(1)
        first = n == 0
        _, vjp = jax.vjp(lambda *a: _f_dil_branch(*a, first), q_ref[0], kp_ref[0], kc_ref[0], vp_ref[0], vc_ref[0],
                         b0[0], b1[0], b2[0], b3[0])
        dq, dkp, dkc, dvp, dvc, g0, g1, g2, g3 = vjp((do_ref[0], dm_ref[0], dl_ref[0]))
        dq_ref[0] = dq

        @pl.when(first)
        def _():
            dk_ref[...] = jnp.zeros_like(dk_ref)
            dv_ref[...] = jnp.zeros_like(dv_ref)

        rows = pl.ds(pl.multiple_of(n * SPAN, SPAN), SPAN)
        dk_ref[0, rows, :] += dkc
        dv_ref[0, rows, :] += dvc

        @pl.when(n > 0)
        def _():
            before = pl.ds(pl.multiple_of((n - 1) * SPAN, SPAN), SPAN)
            dk_ref[0, before, :] += dkp
            dv_ref[0, before, :] += dvp

        start = first & (r == 0)
        for o, g in zip((db0, db1, db2, db3), (g0, g1, g2, g3)):
            @pl.when(start)
            def _(o=o, g=g):
                o[0] = g

            @pl.when(jnp.logical_not(start))
            def _(o=o, g=g):
                o[0] += g

    res = pl.pallas_call(
        body, name=name, grid=(d, nb), in_specs=[cur, prev, cur, prev, cur] + [bias] * 4 + [cur] * 3,
        out_specs=[cur, whole, whole] + [bias] * 4,
        out_shape=[jax.ShapeDtypeStruct(q.shape, f32)] * 3 + [jax.ShapeDtypeStruct((1, SPAN, 2 * SPAN), f32)] * 4,
        compiler_params=_cparams(('arbitrary', 'arbitrary')),
    )(q, k, k, v, v, *biases, do, dm, dl)
    return res[0], res[1], res[2], res[3:]


def _f_dil_merge(o1, m1, l1, o2, m2, l2, o3, m3, l3):
    mx = jnp.maximum(jnp.maximum(m1, m2), m3)
    w1, w2, w3 = jnp.exp(m1 - mx), jnp.exp(m2 - mx), jnp.exp(m3 - mx)
    return ((w1 * o1 + w2 * o2 + w3 * o3) / (w1 * l1 + w2 * l2 + w3 * l3),)


def _bias_onehot(dilation):
    qi = jnp.arange(SPAN, dtype=jnp.int32)[:, None] + SPAN
    kj = jnp.arange(2 * SPAN, dtype=jnp.int32)[None, :]
    bucket = _t5_bucket(jnp.clip(qi - kj, 0, SPAN) * dilation).reshape(-1)
    return (bucket[None, :] == jnp.arange(T5_BUCKETS, dtype=jnp.int32)[:, None]).astype(f32)


def _bias_tables(t5_t, onehot, name):
    N = onehot.shape[1]
    tn = _pick(N, (4096, 2048, 1024))

    def body(t_ref, oh_ref, o_ref):
        o_ref[...] = _hdot(t_ref[...], oh_ref[...])

    return pl.pallas_call(
        body, name=name, grid=(N // tn,),
        in_specs=[pl.BlockSpec((8, T5_BUCKETS), lambda i: (0, 0)), pl.BlockSpec((T5_BUCKETS, tn), lambda i: (0, i))],
        out_specs=pl.BlockSpec((8, tn), lambda i: (0, i)), out_shape=jax.ShapeDtypeStruct((8, N), f32),
        compiler_params=_cparams(('parallel',)),
    )(t5_t, onehot)


def _bias_tables_bwd(d_tab, onehot, name):
    N = onehot.shape[1]
    tn = _pick(N, (4096, 2048, 1024))

    def body(g_ref, oh_ref, o_ref):
        part = _hdot_nt(g_ref[...], oh_ref[...])

        @pl.when(pl.program_id(0) == 0)
        def _():
            o_ref[...] = part

        @pl.when(pl.program_id(0) > 0)
        def _():
            o_ref[...] += part

    return pl.pallas_call(
        body, name=name, grid=(N // tn,),
        in_specs=[pl.BlockSpec((8, tn), lambda i: (0, i)), pl.BlockSpec((T5_BUCKETS, tn), lambda i: (0, i))],
        out_specs=pl.BlockSpec((8, T5_BUCKETS), lambda i: (0, 0)), out_shape=jax.ShapeDtypeStruct((8, T5_BUCKETS), f32),
        compiler_params=_cparams(('arbitrary',)),
    )(d_tab, onehot)


def _by_residue(t, d):
    S, C = t.shape
    return t.reshape(S // d, d, C).transpose(1, 0, 2)


def _from_residue(t):
    d, L, C = t.shape
    return t.transpose(1, 0, 2).reshape(d * L, C)


def _dil_fwd(qkv, mp, l):
    S = qkv.shape[0]
    tm = _pick(S, (256, 128))
    q, k, v = qkv[:, :GROUP_W], qkv[:, GROUP_W:2 * GROUP_W], qkv[:, 2 * GROUP_W:]
    pre_params = [jnp.tile(mp['dil_q_norm'], N_HEADS)[None], jnp.tile(mp['dil_k_norm'], N_HEADS)[None], _head_mean_matrix()]
    qn, kn = _tile_fwd(_f_dil_pre, [q, k], pre_params, [(GROUP_W, f32)] * 2, tm, f'dil_pre_fwd_{l}')
    t5_t = jnp.pad(mp['t5_bias'].T, ((0, 8 - N_HEADS), (0, 0)))
    branches, outs = [], []
    for bi, (_, d) in enumerate(DIL_PAIRS):
        onehot = _bias_onehot(d)
        tab = _bias_tables(t5_t, onehot, f'dil_bias_fwd_{l}_{bi}').reshape(8, SPAN, 2 * SPAN)
        biases = [tab[h][None] for h in range(N_HEADS)]
        qd, kd, vd = _by_residue(qn, d), _by_residue(kn, d), _by_residue(v, d)
        o, m, lsum = _dil_branch_fwd(qd, kd, vd, biases, f'dil_branch_fwd_{l}_{bi}')
        branches.append((qd, kd, vd, biases, onehot))
        outs += [_from_residue(o), _from_residue(m), _from_residue(lsum)]
    (y,) = _tile_fwd(_f_dil_merge, outs, [], [(GROUP_W, f32)], tm, f'dil_merge_fwd_{l}')
    return y, (q, k, pre_params, branches, outs)


def _dil_bwd(dy, saved, l):
    q, k, pre_params, branches, outs = saved
    S = dy.shape[0]
    tm = _pick(S, (256, 128))
    douts, _ = _tile_bwd(_f_dil_merge, outs, [], [dy], [True] * 9, [], tm, f'dil_merge_bwd_{l}')
    dqn = dkn = dv = None
    dt5_t = None
    for bi, (_, d) in enumerate(DIL_PAIRS):
        qd, kd, vd, biases, onehot = branches[bi]
        do, dm, dl = [_by_residue(t, d) for t in douts[3 * bi:3 * bi + 3]]
        dq_b, dk_b, dv_b, dbias = _dil_branch_bwd(qd, kd, vd, biases, do, dm, dl, f'dil_branch_bwd_{l}_{bi}')
        d_tab = jnp.concatenate([*dbias, jnp.zeros((8 - N_HEADS, SPAN, 2 * SPAN), f32)], axis=0).reshape(8, -1)
        g_t5 = _bias_tables_bwd(d_tab, onehot, f'dil_bias_bwd_{l}_{bi}')
        dq_b, dk_b, dv_b = _from_residue(dq_b), _from_residue(dk_b), _from_residue(dv_b)
        dqn = dq_b if dqn is None else dqn + dq_b
        dkn = dk_b if dkn is None else dkn + dk_b
        dv = dv_b if dv is None else dv + dv_b
        dt5_t = g_t5 if dt5_t is None else dt5_t + g_t5
    (dq, dk), (dgq, dgk) = _tile_bwd(_f_dil_pre, [q, k], pre_params, [dqn, dkn], [True, True], [True, True, False], tm,
                                     f'dil_pre_bwd_{l}')
    grads = {'dil_q_norm': dgq.reshape(N_HEADS, HEAD_DIM).sum(0), 'dil_k_norm': dgk.reshape(N_HEADS, HEAD_DIM).sum(0),
             't5_bias': dt5_t[:N_HEADS].T}
    return jnp.concatenate([dq, dk, dv], axis=1), grads


S5_LANES = S5_G * S5_P
SCAN_SEGMENTS = 8
SCAN_W = 256


def _f_s5_prep(bre, bim, lr, li, logdt_col, expand):
    dt = jnp.sum(jnp.exp(logdt_col) * expand, axis=0, keepdims=True)
    mag = jnp.exp(lr * dt)
    ar, ai = mag * jnp.cos(li * dt), mag * jnp.sin(li * dt)
    den = lr * lr + li * li
    nr, ni = ar - 1.0, ai
    zr = (nr * lr + ni * li) / den
    zi = (ni * lr - nr * li) / den
    bb = jnp.concatenate([zr * bre - zi * bim, zr * bim + zi * bre], axis=1)
    a_rows = jnp.broadcast_to(jnp.concatenate([ar, ai], axis=1), bb.shape)
    return bb, a_rows


def _s5_scan(xr, xi, a_rows, name, reverse=False, hpr=None, hpi=None):
    T, _, NL = xr.shape
    nblk = NL // SCAN_W

    def body(*refs):
        if reverse:
            xr_ref, xi_ref, ar_ref, ai_ref, hpr_ref, hpi_ref, hr_ref, hi_ref, dar_ref, dai_ref = refs
        else:
            xr_ref, xi_ref, ar_ref, ai_ref, hr_ref, hi_ref = refs
        ar = ar_ref[...]
        ai = -ai_ref[...] if reverse else ai_ref[...]
        zero = jnp.zeros((SCAN_SEGMENTS, SCAN_W), f32)

        def at(s):
            return T - 1 - s if reverse else s

        def local(s, c):
            hr, hi, pr, pi = c
            j = at(s)
            nhr = ar * hr - ai * hi + xr_ref[j]
            nhi = ar * hi + ai * hr + xi_ref[j]
            hr_ref[j] = nhr
            hi_ref[j] = nhi
            return nhr, nhi, ar * pr - ai * pi, ar * pi + ai * pr

        er, ei, pr, pi = lax.fori_loop(0, T, local, (zero, zero, zero + 1.0, zero), unroll=2)
        row = lax.broadcasted_iota(jnp.int32, (SCAN_SEGMENTS, SCAN_W), 0)
        cr, ci = zero, zero
        order = range(SCAN_SEGMENTS - 2, -1, -1) if reverse else range(1, SCAN_SEGMENTS)
        for k in order:
            src = k + 1 if reverse else k - 1
            tr = er + pr * cr - pi * ci
            ti = ei + pr * ci + pi * cr
            cr = jnp.where(row == k, jnp.sum(jnp.where(row == src, tr, 0.0), axis=0, keepdims=True), cr)
            ci = jnp.where(row == k, jnp.sum(jnp.where(row == src, ti, 0.0), axis=0, keepdims=True), ci)

        def fix(s, c):
            pr, pi, sr, si = c
            j = at(s)
            pr, pi = ar * pr - ai * pi, ar * pi + ai * pr
            hr = hr_ref[j] + pr * cr - pi * ci
            hi = hi_ref[j] + pr * ci + pi * cr
            hr_ref[j] = hr
            hi_ref[j] = hi
            if reverse:
                qr, qi = hpr_ref[j], hpi_ref[j]
                sr = sr + hr * qr + hi * qi
                si = si + hi * qr - hr * qi
            return pr, pi, sr, si

        _, _, sr, si = lax.fori_loop(0, T, fix, (zero + 1.0, zero, zero, zero), unroll=2)
        if reverse:
            dar_ref[...] = sr
            dai_ref[...] = si

    seq = pl.BlockSpec((T, SCAN_SEGMENTS, SCAN_W), lambda b: (0, 0, b))
    a_re = pl.BlockSpec((SCAN_SEGMENTS, SCAN_W), lambda b: (0, b))
    a_im = pl.BlockSpec((SCAN_SEGMENTS, SCAN_W), lambda b: (0, nblk + b))
    out_specs = [seq, seq]
    out_shape = [jax.ShapeDtypeStruct(xr.shape, f32)] * 2
    args = [xr, xi, a_rows, a_rows]
    in_specs = [seq, seq, a_re, a_im]
    if reverse:
        args += [hpr, hpi]
        in_specs += [seq, seq]
        out_specs += [a_re, a_re]
        out_shape += [jax.ShapeDtypeStruct((SCAN_SEGMENTS, NL), f32)] * 2
    return pl.pallas_call(body, name=name, grid=(nblk,), in_specs=in_specs, out_specs=out_specs, out_shape=out_shape,
                          compiler_params=_cparams(('parallel',), big=True))(*args)


def _to_segments(t):
    S, C = t.shape
    return t.reshape(SCAN_SEGMENTS, S // SCAN_SEGMENTS, C).transpose(1, 0, 2)


def _from_segments(t):
    T, K, C = t.shape
    return t.transpose(1, 0, 2).reshape(K * T, C)


def _f_s5_post(y, u, d, w_glu):
    z = _bmm(y + d * u, w_glu)
    return (z[:, :GROUP_W] * jax.nn.sigmoid(z[:, GROUP_W:]),)


def _block_diag(t):
    G, a, b = t.shape
    eye = jnp.eye(G, dtype=t.dtype)
    return (t[:, :, None, :] * eye[:, None, :, None]).reshape(G * a, G * b)


def _diag_blocks(m, a, b):
    G = m.shape[0] // a
    return jnp.moveaxis(jnp.diagonal(m.reshape(G, a, G, b), axis1=0, axis2=2), -1, 0)


def _s5_fwd(u, mp, l):
    S = u.shape[0]
    tm = _pick(S, (256, 128))
    bre = _block_diag(mp['s5_b_re'].transpose(0, 2, 1))
    bim = _block_diag(mp['s5_b_im'].transpose(0, 2, 1))
    expand = jnp.repeat(jnp.eye(S5_G, dtype=f32), S5_P, axis=1)
    prep_params = [mp['s5_lambda_re'].reshape(1, S5_LANES), mp['s5_lambda_im'].reshape(1, S5_LANES),
                   mp['s5_log_dt'].reshape(S5_G, 1), expand]
    bb, a_rows = _tile_fwd(_f_s5_prep, [bre, bim], prep_params, [(2 * S5_LANES, f32)] * 2, GROUP_W, f's5_prep_fwd_{l}')
    x = _mm(u, bb, 'nn', f's5_in_fwd_{l}')
    hr, hi = _s5_scan(_to_segments(x[:, :S5_LANES]), _to_segments(x[:, S5_LANES:]), a_rows, f's5_scan_fwd_{l}')
    h = jnp.concatenate([_from_segments(hr), _from_segments(hi)], axis=1)
    ccat = jnp.concatenate([_block_diag(mp['s5_c_re'].transpose(0, 2, 1)), -_block_diag(mp['s5_c_im'].transpose(0, 2, 1))],
                           axis=0)
    y = _mm(h, ccat, 'nn', f's5_out_fwd_{l}')
    post_params = [mp['s5_d'][None], mp['s5_w_glu']]
    (out,) = _tile_fwd(_f_s5_post, [y, u], post_params, [(GROUP_W, f32)], tm, f's5_post_fwd_{l}')
    return out, (u, bre, bim, prep_params, bb, a_rows, h, ccat, y, post_params)


def _s5_bwd(dout, saved, l):
    u, bre, bim, prep_params, bb, a_rows, h, ccat, y, post_params = saved
    S = u.shape[0]
    tm = _pick(S, (256, 128))
    (dy, du1), (dd, dwglu) = _tile_bwd(_f_s5_post, [y, u], post_params, [dout], [True, True], [True, True], tm,
                                       f's5_post_bwd_{l}')
    dh = _mm(dy, ccat, 'nt', f's5_out_dx_{l}')
    dccat = _mm(h, dy, 'tn', f's5_out_dw_{l}')
    hprev = jnp.pad(h[:-1], ((1, 0), (0, 0)))
    lr_, li_, dar, dai = _s5_scan(_to_segments(dh[:, :S5_LANES]), _to_segments(dh[:, S5_LANES:]), a_rows,
                                  f's5_scan_bwd_{l}', reverse=True, hpr=_to_segments(hprev[:, :S5_LANES]),
                                  hpi=_to_segments(hprev[:, S5_LANES:]))
    dx = jnp.concatenate([_from_segments(lr_), _from_segments(li_)], axis=1)
    du2 = _mm(dx, bb, 'nt', f's5_in_dx_{l}')
    dbb = _mm(u, dx, 'tn', f's5_in_dw_{l}')
    da_rows = jnp.pad(jnp.concatenate([dar, dai], axis=1), ((0, GROUP_W - SCAN_SEGMENTS), (0, 0)))
    (dbre, dbim), (dlr, dli, dlogdt) = _tile_bwd(_f_s5_prep, [bre, bim], prep_params, [dbb, da_rows], [True, True],
                                                 [True, True, True, False], GROUP_W, f's5_prep_bwd_{l}')
    grads = {
        's5_lambda_re': dlr.reshape(S5_G, S5_P), 's5_lambda_im': dli.reshape(S5_G, S5_P), 's5_log_dt': dlogdt[:, 0],
        's5_b_re': _diag_blocks(dbre, S5_CG, S5_P).transpose(0, 2, 1),
        's5_b_im': _diag_blocks(dbim, S5_CG, S5_P).transpose(0, 2, 1),
        's5_c_re': _diag_blocks(dccat[:S5_LANES], S5_P, S5_CG).transpose(0, 2, 1),
        's5_c_im': -_diag_blocks(dccat[S5_LANES:], S5_P, S5_CG).transpose(0, 2, 1),
        's5_d': dd[0], 's5_w_glu': dwglu}
    return du1 + du2, grads


DN_CONV = 4


def _head_sum_matrix():
    h = np.arange(GROUP_W) // HEAD_DIM
    return jnp.asarray((h[:, None] == h[None, :]).astype(np.float32))


def _f_dn_pre(x0, x1, x2, x3, ab, w0, w1, w2, w3, alog, dtb, ea, eb, hs):
    c = w0 * x0 + w1 * x1 + w2 * x2 + w3 * x3
    s = c * jax.nn.sigmoid(c)
    q, k, v = s[:, :GROUP_W], s[:, GROUP_W:2 * GROUP_W], s[:, 2 * GROUP_W:]
    q = q * lax.rsqrt(_hdot(q * q, hs) + EPS) * (HEAD_DIM ** -0.5)
    k = k * lax.rsqrt(_hdot(k * k, hs) + EPS)
    beta = jax.nn.sigmoid(_hdot(ab, eb))
    g = -jnp.exp(alog) * jax.nn.softplus(_hdot(ab, ea) + dtb)
    return q, k, v, g, beta


def _f_dn_chunk(q, k, v, g, beta):
    C = DN_CHUNK
    r = lax.broadcasted_iota(jnp.int32, (C, C), 0)
    c = lax.broadcasted_iota(jnp.int32, (C, C), 1)
    causal, strict = r >= c, r > c
    eye = (r == c).astype(f32)
    gc = _hdot(causal.astype(f32), g)
    glast = jnp.sum(g, axis=0, keepdims=True)
    eg = jnp.exp(gc)
    kb = k * beta
    ones = jnp.ones((C, GROUP_W), f32)
    w = jnp.zeros((C, GROUP_W), f32)
    u = jnp.zeros((C, GROUP_W), f32)
    a_qk = []
    for h in range(N_HEADS):
        hm = _head_mask(h)
        gcol = jnp.sum(gc * hm, axis=1, keepdims=True) * (1.0 / HEAD_DIM)
        grow = _hdot_nt(ones * (hm * (1.0 / HEAD_DIM)), gc)
        dec = jnp.exp(jnp.where(causal, gcol - grow, NEG_INF))
        lmat = jnp.where(strict, _hdot_nt(kb * hm, k) * dec, 0.0)
        t = eye - lmat
        p = lmat
        for _ in range(5):
            p = _hdot(p, p)
            t = t + _hdot(t, p)
        a_qk.append(jnp.where(causal, _hdot_nt(q * hm, k) * dec, 0.0))
        w = w + _hdot(t, kb * eg) * hm
        u = u + _hdot(t, v * beta) * hm
    return (w, u, q * eg, k * jnp.exp(glast - gc), *a_qk, jnp.broadcast_to(jnp.exp(glast), (C, GROUP_W)))


DN_CHUNKS_PER_STEP = 4


def _f_dn_chunks(q, k, v, g, beta):
    outs = [_f_dn_chunk(*[t[i * DN_CHUNK:(i + 1) * DN_CHUNK] for t in (q, k, v, g, beta)])
            for i in range(q.shape[0] // DN_CHUNK)]
    return tuple(jnp.concatenate(parts, axis=0) for parts in zip(*outs))


def _f_dn_step(w, u, qd, kdec, a0, a1, a2, a3, dfull, state, bd):
    row0 = (lax.broadcasted_iota(jnp.int32, dfull.shape, 0) == 0).astype(f32)
    dvec = jnp.sum(dfull * row0, axis=0, keepdims=True)
    vnew = u - _hdot(w, state)
    o = _hdot(qd, state)
    for h, a in enumerate((a0, a1, a2, a3)):
        o = o + _hdot(a, vnew) * _head_mask(h)
    return o, state * dvec + bd * _hdot_tn(kdec, vnew)


def _dn_scan_fwd(ins, name):
    S = ins[0].shape[0]
    N = S // DN_CHUNK
    bd = _head_sum_matrix()

    def body(*refs):
        o_ref, s_ref, state = refs[10], refs[11], refs[12]

        @pl.when(pl.program_id(0) == 0)
        def _():
            state[...] = jnp.zeros_like(state)

        s_in = state[...]
        s_ref[0] = s_in
        o, s_out = _f_dn_step(*[r[...] for r in refs[:9]], s_in, refs[9][...])
        o_ref[...] = o
        state[...] = s_out

    return pl.pallas_call(
        body, name=name, grid=(N,),
        in_specs=[pl.BlockSpec((DN_CHUNK, t.shape[1]), lambda n: (n, 0)) for t in ins] + [_full_spec(bd)],
        out_specs=[pl.BlockSpec((DN_CHUNK, GROUP_W), lambda n: (n, 0)), pl.BlockSpec((1, GROUP_W, GROUP_W), lambda n: (n, 0, 0))],
        out_shape=[jax.ShapeDtypeStruct((S, GROUP_W), f32), jax.ShapeDtypeStruct((N, GROUP_W, GROUP_W), f32)],
        scratch_shapes=[pltpu.VMEM((GROUP_W, GROUP_W), f32)],
        compiler_params=_cparams(('arbitrary',)),
    )(*ins, bd)


def _dn_scan_bwd(ins, states, do, name):
    S = ins[0].shape[0]
    N = S // DN_CHUNK
    bd = _head_sum_matrix()

    def body(*refs):
        s_ref, do_ref = refs[9], refs[10]
        bd_ref = refs[11]
        outs = refs[12:21]
        dstate = refs[21]

        @pl.when(pl.program_id(0) == 0)
        def _():
            dstate[...] = jnp.zeros_like(dstate)

        bd_val = bd_ref[...]
        _, vjp = jax.vjp(lambda *a: _f_dn_step(*a, bd_val), *[r[...] for r in refs[:9]], s_ref[0])
        grads = vjp((do_ref[...], dstate[...]))
        for o, g in zip(outs, grads[:9]):
            o[...] = g
        dstate[...] = grads[9]

    def rev(n):
        return (N - 1 - n, 0)

    res = pl.pallas_call(
        body, name=name, grid=(N,),
        in_specs=[pl.BlockSpec((DN_CHUNK, t.shape[1]), rev) for t in ins] +
                 [pl.BlockSpec((1, GROUP_W, GROUP_W), lambda n: (N - 1 - n, 0, 0)), pl.BlockSpec((DN_CHUNK, GROUP_W), rev),
                  _full_spec(bd)],
        out_specs=[pl.BlockSpec((DN_CHUNK, t.shape[1]), rev) for t in ins],
        out_shape=[jax.ShapeDtypeStruct(t.shape, f32) for t in ins],
        scratch_shapes=[pltpu.VMEM((GROUP_W, GROUP_W), f32)],
        compiler_params=_cparams(('arbitrary',)),
    )(*ins, states, do, bd)
    return list(res)


def _f_dn_post(o, gate, gain, hmean):
    return (o * lax.rsqrt(_hdot(o * o, hmean) + EPS) * gain * (gate * jax.nn.sigmoid(gate)),)


def _delay(t, j):
    return t if j == 0 else jnp.pad(t[:-j], ((j, 0), (0, 0)))


def _advance(t, j):
    return t if j == 0 else jnp.pad(t[j:], ((0, j), (0, 0)))


def _dn_fwd(qkv, a, b, gate, mp, l):
    S = qkv.shape[0]
    tm = _pick(S, (256, 128))
    xs = [_delay(qkv, DN_CONV - 1 - j) for j in range(DN_CONV)]
    ab = jnp.pad(jnp.concatenate([a, b], axis=1), ((0, 0), (0, LANES - 2 * N_HEADS)))
    sel = np.zeros((2, LANES, GROUP_W), np.float32)
    for h in range(N_HEADS):
        sel[0, h, h * HEAD_DIM:(h + 1) * HEAD_DIM] = 1.0
        sel[1, N_HEADS + h, h * HEAD_DIM:(h + 1) * HEAD_DIM] = 1.0
    pre_params = [*[mp['dn_conv'][j][None] for j in range(DN_CONV)], jnp.repeat(mp['dn_a_log'], HEAD_DIM)[None],
                  jnp.repeat(mp['dn_dt_bias'], HEAD_DIM)[None], jnp.asarray(sel[0]), jnp.asarray(sel[1]), _head_sum_matrix()]
    pre = _tile_fwd(_f_dn_pre, [*xs, ab], pre_params, [(GROUP_W, f32)] * 5, tm, f'dn_pre_fwd_{l}')
    chunk_outs = [(GROUP_W, f32)] * 4 + [(HEAD_DIM, f32)] * 4 + [(GROUP_W, f32)]
    parts = _tile_fwd(_f_dn_chunks, pre, [], chunk_outs, DN_CHUNK * DN_CHUNKS_PER_STEP, f'dn_chunk_fwd_{l}')
    o, states = _dn_scan_fwd(parts, f'dn_scan_fwd_{l}')
    post_params = [jnp.tile(mp['dn_o_norm'], N_HEADS)[None], _head_mean_matrix()]
    (y,) = _tile_fwd(_f_dn_post, [o, gate], post_params, [(GROUP_W, f32)], tm, f'dn_post_fwd_{l}')
    return y, (xs, ab, pre_params, pre, parts, states, o, gate, post_params)


def _dn_bwd(dy, saved, l):
    xs, ab, pre_params, pre, parts, states, o, gate, post_params = saved
    S = dy.shape[0]
    tm = _pick(S, (256, 128))
    (do, dgate), (dgain,) = _tile_bwd(_f_dn_post, [o, gate], post_params, [dy], [True, True], [True, False], tm,
                                      f'dn_post_bwd_{l}')
    dparts = _dn_scan_bwd(parts, states, do, f'dn_scan_bwd_{l}')
    dpre, _ = _tile_bwd(_f_dn_chunks, pre, [], dparts, [True] * 5, [], DN_CHUNK * DN_CHUNKS_PER_STEP, f'dn_chunk_bwd_{l}')
    dins, dpar = _tile_bwd(_f_dn_pre, [*xs, ab], pre_params, dpre, [True] * 5, [True] * 6 + [False] * 3, tm,
                           f'dn_pre_bwd_{l}')
    dqkv = dins[DN_CONV - 1]
    for j in range(DN_CONV - 1):
        dqkv = dqkv + _advance(dins[j], DN_CONV - 1 - j)
    dab = dins[DN_CONV]
    grads = {'dn_conv': jnp.concatenate(dpar[:DN_CONV], axis=0),
             'dn_a_log': dpar[4].reshape(N_HEADS, HEAD_DIM).sum(1), 'dn_dt_bias': dpar[5].reshape(N_HEADS, HEAD_DIM).sum(1),
             'dn_o_norm': dgain.reshape(N_HEADS, HEAD_DIM).sum(0)}
    return dqkv, dab[:, :N_HEADS], dab[:, N_HEADS:2 * N_HEADS], dgate, grads


def _t5_bucket(dist):
    exact = T5_BUCKETS // 2
    df = jnp.maximum(dist, 1).astype(f32)
    large = exact + (jnp.log(df / exact) / math.log(T5_MAX_DIST / exact) * (T5_BUCKETS - exact)).astype(jnp.int32)
    large = jnp.minimum(large, T5_BUCKETS - 1)
    return jnp.where(dist < exact, dist, large)


def _split_cols(t, sizes):
    out, start = [], 0
    for s in sizes:
        out.append(t[..., start:start + s])
        start += s
    return out


def _mixers_fwd(proj, mp, l):
    c_q, c_kv, k_rope, u_s5, qkv_dil, qkv_dn, a_dn, b_dn, gate_dn = _split_cols(proj, IN_SPLITS)
    y_mla, s_mla = _mla_fwd(c_q, c_kv, k_rope, mp, l)
    y_s5, s_s5 = _s5_fwd(u_s5, mp, l)
    y_dil, s_dil = _dil_fwd(qkv_dil, mp, l)
    y_dn, s_dn = _dn_fwd(qkv_dn, a_dn, b_dn, gate_dn, mp, l)
    return jnp.concatenate([y_mla, y_s5, y_dil, y_dn], axis=-1), (s_mla, s_s5, s_dil, s_dn)


def _mixers_bwd(dmixed, saved, l):
    s_mla, s_s5, s_dil, s_dn = saved
    d_mla, d_s5, d_dil, d_dn = _split_cols(dmixed, (GROUP_W,) * 4)
    dc_q, dc_kv, dk_rope, g_mla = _mla_bwd(d_mla, s_mla, l)
    du, g_s5 = _s5_bwd(d_s5, s_s5, l)
    dqkv_dil, g_dil = _dil_bwd(d_dil, s_dil, l)
    dqkv_dn, da, db, dgate, g_dn = _dn_bwd(d_dn, s_dn, l)
    parts = [dc_q, dc_kv, dk_rope, du, dqkv_dil, dqkv_dn, da, db, dgate]
    dproj = jnp.concatenate([p.astype(bf16) for p in parts], axis=-1)
    return dproj, {**g_mla, **g_s5, **g_dil, **g_dn}


MIXER_PARAMS = ['mla_q_norm', 'mla_kv_norm', 'mla_w_uq', 'mla_w_ukv', 'mla_qk_q', 'mla_qk_k', 's5_lambda_re',
                's5_lambda_im', 's5_log_dt', 's5_b_re', 's5_b_im', 's5_c_re', 's5_c_im', 's5_d', 's5_w_glu',
                'dil_q_norm', 'dil_k_norm', 't5_bias', 'dn_conv', 'dn_a_log', 'dn_dt_bias', 'dn_o_norm']


def _layer_fwd(h, W, l):
    S = h.shape[0]
    tm = _pick(S, (256, 128))
    g1 = W['attn_norm'][l][None]
    g2 = W['ffn_norm'][l][None]
    (n1,) = _tile_fwd(_f_rms, [h], [g1], [(D_MODEL, bf16)], tm, f'rms1_fwd_{l}')
    proj = _mm(n1, W['w_in'][l], 'nn', f'proj_fwd_{l}')
    mp = {k: (W[k] if k == 't5_bias' else W[k][l]).astype(f32) for k in MIXER_PARAMS}
    mixed, mix_saved = _mixers_fwd(proj, mp, l)
    mixed_b = mixed.astype(bf16)
    h2 = _mm(mixed_b, W['w_out'][l], 'nn', f'out_fwd_{l}', add=h)
    (n2,) = _tile_fwd(_f_rms, [h2], [g2], [(D_MODEL, bf16)], tm, f'rms2_fwd_{l}')
    w13 = jnp.concatenate([W['ffn_w1'][l], W['ffn_w3'][l]], axis=1)
    uv = _mm(n2, w13, 'nn', f'ffn13_fwd_{l}')
    (act,) = _tile_fwd(_f_swiglu, [uv], [], [(FFN_HIDDEN, bf16)], tm, f'swiglu_fwd_{l}')
    h3 = _mm(act, W['ffn_w2'][l], 'nn', f'ffn2_fwd_{l}', add=h2)
    saved = dict(h=h, n1=n1, mix=mix_saved, mixed=mixed_b, h2=h2, n2=n2, uv=uv, act=act, w13=w13)
    return h3, saved


def _layer_bwd(dh3, saved, W, l):
    S = dh3.shape[0]
    tm = _pick(S, (256, 128))
    g1 = W['attn_norm'][l][None]
    g2 = W['ffn_norm'][l][None]
    grads = {}
    dact = _mm(dh3, W['ffn_w2'][l], 'nt', f'ffn2_dx_{l}')
    grads['ffn_w2'] = _mm(saved['act'], dh3, 'tn', f'ffn2_dw_{l}')
    (duv,), _ = _tile_bwd(_f_swiglu, [saved['uv']], [], [dact], [True], [], tm, f'swiglu_bwd_{l}', dt_dtypes=[bf16])
    dn2 = _mm(duv, saved['w13'], 'nt', f'ffn13_dx_{l}')
    dw13 = _mm(saved['n2'], duv, 'tn', f'ffn13_dw_{l}')
    grads['ffn_w1'], grads['ffn_w3'] = dw13[:, :FFN_HIDDEN], dw13[:, FFN_HIDDEN:]
    (dh2n,), (dg2,) = _tile_bwd(_f_rms, [saved['h2']], [g2], [dn2], [True], [True], tm, f'rms2_bwd_{l}')
    grads['ffn_norm'] = dg2[0]
    dh2 = dh3 + dh2n
    dmixed = _mm(dh2, W['w_out'][l], 'nt', f'out_dx_{l}')
    grads['w_out'] = _mm(saved['mixed'], dh2, 'tn', f'out_dw_{l}')
    dproj, dmp = _mixers_bwd(dmixed, saved['mix'], l)
    for k in MIXER_PARAMS:
        grads[k] = dmp[k]
    dn1 = _mm(dproj, W['w_in'][l], 'nt', f'proj_dx_{l}')
    grads['w_in'] = _mm(saved['n1'], dproj, 'tn', f'proj_dw_{l}')
    (dh1n,), (dg1,) = _tile_bwd(_f_rms, [saved['h']], [g1], [dn1], [True], [True], tm, f'rms1_bwd_{l}')
    grads['attn_norm'] = dg1[0]
    return dh2 + dh1n, grads


def kernel(x, attn_norm, w_in, w_out, mla_q_norm, mla_kv_norm, mla_w_uq, mla_w_ukv, mla_qk_q, mla_qk_k, s5_lambda_re, s5_lambda_im, s5_log_dt, s5_b_re, s5_b_im, s5_c_re, s5_c_im, s5_d, s5_w_glu, dil_q_norm, dil_k_norm, t5_bias, dn_conv, dn_a_log, dn_dt_bias, dn_o_norm, ffn_norm, ffn_w1, ffn_w3, ffn_w2, loss_target, m_attn_norm, m_w_in, m_w_out, m_mla_q_norm, m_mla_kv_norm, m_mla_w_uq, m_mla_w_ukv, m_mla_qk_q, m_mla_qk_k, m_s5_lambda_re, m_s5_lambda_im, m_s5_log_dt, m_s5_b_re, m_s5_b_im, m_s5_c_re, m_s5_c_im, m_s5_d, m_s5_w_glu, m_dil_q_norm, m_dil_k_norm, m_t5_bias, m_dn_conv, m_dn_a_log, m_dn_dt_bias, m_dn_o_norm, m_ffn_norm, m_ffn_w1, m_ffn_w3, m_ffn_w2, v_attn_norm, v_w_in, v_w_out, v_mla_q_norm, v_mla_kv_norm, v_mla_w_uq, v_mla_w_ukv, v_mla_qk_q, v_mla_qk_k, v_s5_lambda_re, v_s5_lambda_im, v_s5_log_dt, v_s5_b_re, v_s5_b_im, v_s5_c_re, v_s5_c_im, v_s5_d, v_s5_w_glu, v_dil_q_norm, v_dil_k_norm, v_t5_bias, v_dn_conv, v_dn_a_log, v_dn_dt_bias, v_dn_o_norm, v_ffn_norm, v_ffn_w1, v_ffn_w3, v_ffn_w2):
    given = dict(locals())
    w_loc = {n: given[n] for n in WEIGHTS}
    m_loc = {n: given['m_' + n] for n in WEIGHTS}
    v_loc = {n: given['v_' + n] for n in WEIGHTS}
    big_names = list(BIG)

    gathered = _gather_tensors([w_loc[n].astype(bf16) for n in big_names])
    W = {n: _from_shards(n, g) for n, g in zip(big_names, gathered)}
    for n in SMALL:
        W[n] = w_loc[n]

    h = x[0]
    saved = []
    for l in range(DEPTH):
        h, sv = _layer_fwd(h, W, l)
        saved.append(sv)
    parts_loss, dh = _loss_head(h, loss_target[0])
    loss = lax.psum(jnp.sum(parts_loss), ('x', 'y', 'c'))

    layer_grads = [None] * DEPTH
    for l in reversed(range(DEPTH)):
        dh, layer_grads[l] = _layer_bwd(dh, saved[l], W, l)
    grad_x = dh[None]
    small_full = []
    for n in SMALL:
        if n == 't5_bias':
            small_full.append(layer_grads[0][n] + layer_grads[1][n])
        else:
            small_full.append(jnp.stack([layer_grads[l][n] for l in range(DEPTH)]))

    gs = [jnp.stack([_by_shard(n, layer_grads[l][n]).astype(bf16) for l in range(DEPTH)], axis=1) for n in big_names]
    small_shapes = [w_loc[n].shape for n in SMALL]
    Rs = _pack_rows(sum(math.prod(s) for s in small_shapes), 8)
    small_pack = _pack(small_full, Rs, f32)
    from_sib, recv_small = _swap_with_sibling(gs, small_pack)
    chip_sums = [_chip_sum_of(g, r, 'chip_sum_' + n) for n, g, r in zip(big_names, gs, from_sib)]
    chip_small = _small_chip_sum(small_pack, recv_small)
    from_chips, from_chips_small = _exchange_between_chips(chip_sums, chip_small)
    totals = [_shard_total_of(g, r, rc, 'shard_total_' + n) for n, g, r, rc in zip(big_names, gs, from_sib, from_chips)]
    g_big = _join_with_sibling(totals)

    g_small_p, d_small_p, m_small_p, v_small_p = _small_update(
        small_pack, recv_small, from_chips_small, _pack([w_loc[n] for n in SMALL], Rs, f32),
        _pack([m_loc[n] for n in SMALL], Rs, f32), _pack([v_loc[n] for n in SMALL], Rs, f32))
    grad, delta, new_m, new_v = {}, {}, {}, {}
    for n, g_, d_, m_, v_ in zip(SMALL, _unpack(g_small_p, small_shapes), _unpack(d_small_p, small_shapes),
                                 _unpack(m_small_p, small_shapes), _unpack(v_small_p, small_shapes)):
        grad[n], delta[n], new_m[n], new_v[n] = g_, d_, m_, v_
    for n, g_ in zip(big_names, g_big):
        grad[n] = g_
        delta[n], new_m[n], new_v[n] = _adamw(w_loc[n], g_, m_loc[n], v_loc[n], 'adamw_' + n)
    return (loss, grad_x, *[grad[n] for n in WEIGHTS], *[delta[n] for n in WEIGHTS],
            *[new_m[n] for n in WEIGHTS], *[new_v[n] for n in WEIGHTS])
```

```python
import functools
import math

import numpy as np
import jax
import jax.numpy as jnp
from jax import lax
from jax.experimental import pallas as pl
from jax.experimental.pallas import tpu as pltpu

f32 = jnp.float32
bf16 = jnp.bfloat16
HI = lax.Precision.HIGHEST
MESH = pl.DeviceIdType.MESH

VMEM_LIMIT_BYTES = 48 * 1024 * 1024
MM_VMEM_BUDGET_BYTES = 32 * 1024 * 1024
LANES = 128

D_MODEL = 1024
DEPTH = 2
GROUP_W = 256
HEAD_DIM = 64
EPS = 1e-6
NEG_INF = -1e30
N_HEADS = 4
MLA_NOPE, MLA_ROPE = 64, 32
MLA_DQK = MLA_NOPE + MLA_ROPE
ROPE_THETA = 10000.0
Q_BLOCK = 128
S5_G, S5_CG, S5_P = 16, 16, 64
DIL_PAIRS = ((128, 1), (512, 4), (2048, 16))
T5_BUCKETS, T5_MAX_DIST = 32, 2048
DN_CHUNK = 64
FFN_HIDDEN = 2816
IN_SPLITS = (256, 128, 32, 256, 768, 768, 4, 4, 256)
IN_COLS = sum(IN_SPLITS)

ADAM_LR, ADAM_B1, ADAM_B2, ADAM_EPS, ADAM_WD, ADAM_STEP = 0.001, 0.9, 0.999, 1e-08, 0.01, 10

WEIGHTS = ['attn_norm', 'w_in', 'w_out', 'mla_q_norm', 'mla_kv_norm', 'mla_w_uq', 'mla_w_ukv', 'mla_qk_q', 'mla_qk_k',
           's5_lambda_re', 's5_lambda_im', 's5_log_dt', 's5_b_re', 's5_b_im', 's5_c_re', 's5_c_im', 's5_d', 's5_w_glu',
           'dil_q_norm', 'dil_k_norm', 't5_bias', 'dn_conv', 'dn_a_log', 'dn_dt_bias', 'dn_o_norm', 'ffn_norm',
           'ffn_w1', 'ffn_w3', 'ffn_w2']
BIG = {'w_in': 2, 'w_out': 1, 'mla_w_uq': 2, 'mla_w_ukv': 2, 's5_w_glu': 2, 'dn_conv': 2, 'ffn_w1': 2, 'ffn_w3': 2,
       'ffn_w2': 1}
SMALL = [n for n in WEIGHTS if n not in BIG]
N_SHARDS = 4
PACK_COLS = 1024


def _cparams(sem=None, big=False):
    kw = {}
    if sem is not None:
        kw['dimension_semantics'] = sem
    if big:
        kw['vmem_limit_bytes'] = VMEM_LIMIT_BYTES
    return pltpu.CompilerParams(**kw)


def _pick(n, prefs):
    for p in prefs:
        if p <= n and n % p == 0:
            return p
    return n


def _lane_tile(n, cap):
    for t in range(cap - cap % LANES, 0, -LANES):
        if n % t == 0:
            return t
    return n


def _mm(a, b, mode, name, add=None, out_dtype=f32):
    if mode == 'nn':
        (M, K), (K2, N) = a.shape, b.shape
    elif mode == 'nt':
        (M, K), (N, K2) = a.shape, b.shape
    else:
        (K, M), (K2, N) = a.shape, b.shape
    assert K == K2, (name, a.shape, b.shape)
    tk = K if K <= 2816 else _pick(K, (2816, 2048, 1408, 1024, 512))
    cap_m, cap_n = (1408 if mode == 'tn' else 512), 1408

    def need(tm_, tn_):
        per_step = tm_ * tk * a.dtype.itemsize + tk * tn_ * b.dtype.itemsize + tm_ * tn_ * jnp.dtype(out_dtype).itemsize
        if add is not None:
            per_step += tm_ * tn_ * add.dtype.itemsize
        return 2 * per_step + tm_ * tn_ * 4

    tm, tn = _lane_tile(M, cap_m), _lane_tile(N, cap_n)
    while need(tm, tn) > MM_VMEM_BUDGET_BYTES and cap_m > LANES:
        cap_m //= 2
        tm = _lane_tile(M, cap_m)
    nk = K // tk
    dims = {'nn': (((1,), (0,)), ((), ())), 'nt': (((1,), (1,)), ((), ())), 'tn': (((0,), (0,)), ((), ()))}[mode]
    has_add = add is not None

    def body(*refs):
        a_ref, b_ref = refs[0], refs[1]
        add_ref = refs[2] if has_add else None
        o_ref = refs[3] if has_add else refs[2]
        part = lax.dot_general(a_ref[...].astype(bf16), b_ref[...].astype(bf16), dims, preferred_element_type=f32)
        if nk == 1:
            if has_add:
                part = part + add_ref[...].astype(f32)
            o_ref[...] = part.astype(out_dtype)
        else:
            acc_ref = refs[-1]
            k = pl.program_id(2)

            @pl.when(k == 0)
            def _():
                acc_ref[...] = part

            @pl.when(k > 0)
            def _():
                acc_ref[...] += part

            @pl.when(k == nk - 1)
            def _():
                r = acc_ref[...]
                if has_add:
                    r = r + add_ref[...].astype(f32)
                o_ref[...] = r.astype(out_dtype)

    if mode == 'nn':
        a_spec = pl.BlockSpec((tm, tk), lambda i, j, k: (i, k))
        b_spec = pl.BlockSpec((tk, tn), lambda i, j, k: (k, j))
    elif mode == 'nt':
        a_spec = pl.BlockSpec((tm, tk), lambda i, j, k: (i, k))
        b_spec = pl.BlockSpec((tn, tk), lambda i, j, k: (j, k))
    else:
        a_spec = pl.BlockSpec((tk, tm), lambda i, j, k: (k, i))
        b_spec = pl.BlockSpec((tk, tn), lambda i, j, k: (k, j))
    in_specs = [a_spec, b_spec]
    args = [a, b]
    if has_add:
        in_specs.append(pl.BlockSpec((tm, tn), lambda i, j, k: (i, j)))
        args.append(add)
    return pl.pallas_call(
        body, name=name, grid=(M // tm, N // tn, nk), in_specs=in_specs,
        out_specs=pl.BlockSpec((tm, tn), lambda i, j, k: (i, j)),
        out_shape=jax.ShapeDtypeStruct((M, N), out_dtype),
        scratch_shapes=[pltpu.VMEM((tm, tn), f32)] if nk > 1 else [],
        compiler_params=_cparams(('parallel', 'parallel', 'arbitrary'), big=True),
    )(*args)


def _full_spec(p):
    nd = p.ndim
    return pl.BlockSpec(p.shape, lambda i, _nd=nd: (0,) * _nd)


def _tile_fwd(f, tiled, params, outs, tm, name):
    S = tiled[0].shape[0]
    nt, npar = len(tiled), len(params)

    def body(*refs):
        vals = [r[...].astype(f32) for r in refs[:nt + npar]]
        res = f(*vals)
        for r, o in zip(res, refs[nt + npar:]):
            o[...] = r.astype(o.dtype)

    return pl.pallas_call(
        body, name=name, grid=(S // tm,),
        in_specs=[pl.BlockSpec((tm, t.shape[1]), lambda i: (i, 0)) for t in tiled] + [_full_spec(p) for p in params],
        out_specs=[pl.BlockSpec((tm, c), lambda i: (i, 0)) for c, _ in outs],
        out_shape=[jax.ShapeDtypeStruct((S, c), dt) for c, dt in outs],
        compiler_params=_cparams(('parallel',), big=True),
    )(*tiled, *params)


def _tile_bwd(f, tiled, params, cts, diff_t, diff_p, tm, name, dt_dtypes=None):
    S = tiled[0].shape[0]
    nt, npar, nc = len(tiled), len(params), len(cts)
    it = [i for i in range(nt) if diff_t[i]]
    ip = [i for i in range(npar) if diff_p[i]]
    if dt_dtypes is None:
        dt_dtypes = [f32] * len(it)

    def body(*refs):
        vals = [r[...].astype(f32) for r in refs[:nt + npar]]
        ct_vals = tuple(r[...].astype(f32) for r in refs[nt + npar:nt + npar + nc])
        out_refs = refs[nt + npar + nc:]

        def g(*dv):
            full = list(vals)
            for k, i in enumerate(it):
                full[i] = dv[k]
            for k, i in enumerate(ip):
                full[nt + i] = dv[len(it) + k]
            return tuple(f(*full))

        _, vjp = jax.vjp(g, *[vals[i] for i in it], *[vals[nt + i] for i in ip])
        grads = vjp(ct_vals)
        for k in range(len(it)):
            out_refs[k][...] = grads[k].astype(out_refs[k].dtype)
        step = pl.program_id(0)
        for k in range(len(ip)):
            o = out_refs[len(it) + k]
            gk = grads[len(it) + k]

            @pl.when(step == 0)
            def _(o=o, gk=gk):
                o[...] = gk

            @pl.when(step > 0)
            def _(o=o, gk=gk):
                o[...] += gk

    out_specs = [pl.BlockSpec((tm, tiled[i].shape[1]), lambda i_: (i_, 0)) for i in it] + [_full_spec(params[i]) for i in ip]
    out_shape = [jax.ShapeDtypeStruct(tiled[i].shape, dt_dtypes[k]) for k, i in enumerate(it)] + \
                [jax.ShapeDtypeStruct(params[i].shape, f32) for i in ip]
    res = pl.pallas_call(
        body, name=name, grid=(S // tm,),
        in_specs=[pl.BlockSpec((tm, t.shape[1]), lambda i: (i, 0)) for t in tiled] + [_full_spec(p) for p in params] +
                 [pl.BlockSpec((tm, c.shape[1]), lambda i: (i, 0)) for c in cts],
        out_specs=out_specs, out_shape=out_shape,
        compiler_params=_cparams(('arbitrary',), big=True),
    )(*tiled, *params, *cts)
    return list(res[:len(it)]), list(res[len(it):])


def _rms(x, g):
    return x * lax.rsqrt(jnp.mean(x * x, axis=-1, keepdims=True) + EPS) * g


def _f_rms(x, g):
    return (_rms(x, g),)


def _f_swiglu(uv):
    h = uv.shape[1] // 2
    u, v = uv[:, :h], uv[:, h:]
    return (u * jax.nn.sigmoid(u) * v,)


def _loss_head(y, target):
    S, D = y.shape
    tm = _pick(S, (256, 128))

    def body(y_ref, t_ref, part_ref, dy_ref):
        e = y_ref[...] - t_ref[...]
        dy_ref[...] = e * (1.0 / D)
        s = 0.5 * jnp.sum(jnp.sum(e * e, axis=1, keepdims=True), axis=0, keepdims=True) * (1.0 / D)
        r = lax.broadcasted_iota(jnp.int32, (8, LANES), 0)
        c = lax.broadcasted_iota(jnp.int32, (8, LANES), 1)
        part_ref[0] = jnp.where((r == 0) & (c == 0), s, 0.0)

    return pl.pallas_call(
        body, name='loss_head', grid=(S // tm,),
        in_specs=[pl.BlockSpec((tm, D), lambda i: (i, 0))] * 2,
        out_specs=[pl.BlockSpec((1, 8, LANES), lambda i: (i, 0, 0)), pl.BlockSpec((tm, D), lambda i: (i, 0))],
        out_shape=[jax.ShapeDtypeStruct((S // tm, 8, LANES), f32), jax.ShapeDtypeStruct((S, D), f32)],
        compiler_params=_cparams(('parallel',)),
    )(y, target)


def _pack_rows(n_elems, mult):
    rows = -(-n_elems // PACK_COLS)
    return -(-rows // mult) * mult


def _pack(arrs, rows, dtype):
    flat = jnp.concatenate([a.astype(dtype).reshape(-1) for a in arrs])
    flat = jnp.pad(flat, (0, rows * PACK_COLS - flat.shape[0]))
    return flat.reshape(rows, PACK_COLS)


def _unpack(pack, shapes):
    flat = pack.reshape(-1)
    out, off = [], 0
    for s in shapes:
        n = math.prod(s)
        out.append(flat[off:off + n].reshape(s))
        off += n
    return out


ANY = pl.BlockSpec(memory_space=pl.ANY)


def _place():
    return lax.axis_index('x'), lax.axis_index('y'), lax.axis_index('c')


def _where():
    return jnp.stack([lax.axis_index('c'), 2 * lax.axis_index('x') + lax.axis_index('y')]).astype(jnp.int32)


def _remote(src, dst, send_sems, recv_sems, k, to):
    return pltpu.make_async_remote_copy(src_ref=src, dst_ref=dst, send_sem=send_sems.at[k], recv_sem=recv_sems.at[k],
                                        device_id=to, device_id_type=MESH)


def _gather_tensors(ws):
    n = len(ws)

    def body(*refs):
        w_refs, g_refs = refs[:n], refs[n:2 * n]
        send_sems, recv_sems = refs[2 * n:]
        x, y, c = _place()
        me, sib = (x, y, c), (x, y, 1 - c)
        chips = [(1 - x, y), (x, 1 - y), (1 - x, 1 - y)]
        first = [_remote(w_refs[t].at[c], g_refs[t].at[2 * x + y, c], send_sems, recv_sems, 6 * t + j, (px, py, c))
                 for j, (px, py) in enumerate(chips) for t in range(n)]
        for cp in first:
            cp.start()
        passed = []
        for j, (px, py) in enumerate(chips):
            for t in range(n):
                here = g_refs[t].at[2 * px + py, c]
                _remote(here, here, send_sems, recv_sems, 6 * t + j, me).wait_recv()
                cp = _remote(here, here, send_sems, recv_sems, 6 * t + 3 + j, sib)
                cp.start()
                passed.append(cp)
        for j, (px, py) in enumerate(chips):
            for t in range(n):
                there = g_refs[t].at[2 * px + py, 1 - c]
                _remote(there, there, send_sems, recv_sems, 6 * t + 3 + j, me).wait_recv()
        for cp in first + passed:
            cp.wait_send()

    own = 2 * lax.axis_index('x') + lax.axis_index('y')
    res = pl.pallas_call(
        body, name='gather_weights', in_specs=[ANY] * n, out_specs=[ANY] * n,
        out_shape=[jax.ShapeDtypeStruct((N_SHARDS,) + w.shape, w.dtype) for w in ws],
        scratch_shapes=[pltpu.SemaphoreType.DMA((6 * n,)), pltpu.SemaphoreType.DMA((6 * n,))],
    )(*ws)
    return [lax.dynamic_update_slice(g, w[None], (own, 0, 0, 0)) for g, w in zip(res, ws)]


def _swap_with_sibling(gs, small):
    n = len(gs)

    def body(*refs):
        g_refs, s_ref = refs[:n], refs[n]
        r_refs, rs_ref = refs[n + 1:2 * n + 1], refs[2 * n + 1]
        send_sems, recv_sems = refs[2 * n + 2:]
        x, y, c = _place()
        sib = (x, y, 1 - c)
        cps = [_remote(g_refs[t].at[:, 1 - c], r_refs[t], send_sems, recv_sems, t, sib) for t in range(n)]
        cps.append(_remote(s_ref, rs_ref, send_sems, recv_sems, n, sib))
        for cp in cps:
            cp.start()
        for cp in cps:
            cp.wait()

    res = pl.pallas_call(
        body, name='swap_with_sibling', in_specs=[ANY] * (n + 1), out_specs=[ANY] * (n + 1),
        out_shape=[jax.ShapeDtypeStruct((N_SHARDS,) + g.shape[2:], g.dtype) for g in gs] +
                  [jax.ShapeDtypeStruct(small.shape, small.dtype)],
        scratch_shapes=[pltpu.SemaphoreType.DMA((n + 1,)), pltpu.SemaphoreType.DMA((n + 1,))],
    )(*gs, small)
    return list(res[:n]), res[n]


def _exchange_between_chips(cs, small):
    n = len(cs)

    def body(*refs):
        c_refs, s_ref = refs[:n], refs[n]
        r_refs, rs_ref = refs[n + 1:2 * n + 1], refs[2 * n + 1]
        send_sems, recv_sems = refs[2 * n + 2:]
        x, y, c = _place()
        chips = [(1 - x, y), (x, 1 - y), (1 - x, 1 - y)]
        cps = []
        for j, (px, py) in enumerate(chips):
            for t in range(n):
                cps.append(_remote(c_refs[t].at[2 * px + py], r_refs[t].at[j], send_sems, recv_sems, 3 * t + j, (px, py, c)))
            cps.append(_remote(s_ref, rs_ref.at[j], send_sems, recv_sems, 3 * n + j, (px, py, c)))
        for cp in cps:
            cp.start()
        for cp in cps:
            cp.wait()

    res = pl.pallas_call(
        body, name='exchange_between_chips', in_specs=[ANY] * (n + 1), out_specs=[ANY] * (n + 1),
        out_shape=[jax.ShapeDtypeStruct((3,) + c.shape[1:], c.dtype) for c in cs] +
                  [jax.ShapeDtypeStruct((3,) + small.shape, small.dtype)],
        scratch_shapes=[pltpu.SemaphoreType.DMA((3 * n + 3,)), pltpu.SemaphoreType.DMA((3 * n + 3,))],
    )(*cs, small)
    return list(res[:n]), res[n]


def _join_with_sibling(ts):
    n = len(ts)

    def body(*refs):
        t_refs, o_refs = refs[:n], refs[n:2 * n]
        send_sems, recv_sems = refs[2 * n:]
        x, y, c = _place()
        cps = [_remote(t_refs[t], o_refs[t], send_sems, recv_sems, t, (x, y, 1 - c)) for t in range(n)]
        for cp in cps:
            cp.start()
        for cp in cps:
            cp.wait()

    theirs = pl.pallas_call(
        body, name='join_with_sibling', in_specs=[ANY] * n, out_specs=[ANY] * n,
        out_shape=[jax.ShapeDtypeStruct(t.shape, t.dtype) for t in ts],
        scratch_shapes=[pltpu.SemaphoreType.DMA((n,)), pltpu.SemaphoreType.DMA((n,))],
    )(*ts)
    south = lax.axis_index('c') == 0
    return [jnp.stack([jnp.where(south, mine, other), jnp.where(south, other, mine)]) for mine, other in zip(ts, theirs)]


def _row_tile(a):
    return _pick(a, (512, 256, 128, 64, 32, 16, 8))


def _chip_sum_of(g, r, name):
    _, _, a, b = g.shape
    tr = _row_tile(a)

    def body(w_ref, g_ref, r_ref, o_ref):
        o_ref[...] = (g_ref[0].astype(f32) + r_ref[...].astype(f32)).astype(o_ref.dtype)

    return pl.pallas_call(
        body, name=name,
        grid_spec=pltpu.PrefetchScalarGridSpec(
            num_scalar_prefetch=1, grid=(N_SHARDS, a // tr),
            in_specs=[pl.BlockSpec((1, 1, tr, b), lambda s, i, w: (s, w[0], i, 0)),
                      pl.BlockSpec((1, tr, b), lambda s, i, w: (s, i, 0))],
            out_specs=pl.BlockSpec((1, tr, b), lambda s, i, w: (s, i, 0))),
        out_shape=jax.ShapeDtypeStruct((N_SHARDS, a, b), bf16),
        compiler_params=_cparams(('parallel', 'parallel')),
    )(_where(), g, r)


def _shard_total_of(g, r, rc, name):
    _, _, a, b = g.shape
    tr = _row_tile(a)

    def body(w_ref, g_ref, r_ref, rc_ref, o_ref):
        t = g_ref[0, 0].astype(f32) + r_ref[0].astype(f32)
        t = t + rc_ref[0].astype(f32)
        t = t + rc_ref[1].astype(f32)
        t = t + rc_ref[2].astype(f32)
        o_ref[...] = t

    return pl.pallas_call(
        body, name=name,
        grid_spec=pltpu.PrefetchScalarGridSpec(
            num_scalar_prefetch=1, grid=(a // tr,),
            in_specs=[pl.BlockSpec((1, 1, tr, b), lambda i, w: (w[1], w[0], i, 0)),
                      pl.BlockSpec((1, tr, b), lambda i, w: (w[1], i, 0)),
                      pl.BlockSpec((3, tr, b), lambda i, w: (0, i, 0))],
            out_specs=pl.BlockSpec((tr, b), lambda i, w: (i, 0))),
        out_shape=jax.ShapeDtypeStruct((a, b), f32),
        compiler_params=_cparams(('parallel',)),
    )(_where(), g, r, rc)


def _by_shard(name, t):
    r, c = t.shape
    if BIG[name] == 2:
        return t.reshape(r, N_SHARDS, c // N_SHARDS).transpose(1, 0, 2)
    return t.reshape(N_SHARDS, r // N_SHARDS, c)


def _from_shards(name, g):
    s, l, a, b = g.shape
    if BIG[name] == 2:
        return g.transpose(1, 2, 0, 3).reshape(l, a, s * b)
    return g.transpose(1, 0, 2, 3).reshape(l, s * a, b)


def _adam_math(w, g, m, v):
    m = ADAM_B1 * m + (1.0 - ADAM_B1) * g
    v = ADAM_B2 * v + (1.0 - ADAM_B2) * (g * g)
    m_hat = m / (1.0 - ADAM_B1 ** ADAM_STEP)
    v_hat = v / (1.0 - ADAM_B2 ** ADAM_STEP)
    delta = -ADAM_LR * (m_hat / (jnp.sqrt(v_hat) + ADAM_EPS) + ADAM_WD * w)
    return delta, m, v


def _small_update(own, sib, chips, w, m, v):
    def body(o_ref, s_ref, c_ref, w_ref, m_ref, v_ref, g_out, d_out, m_out, v_out):
        chip = o_ref[...] + s_ref[...]
        g = (chip + c_ref[0]) + (c_ref[1] + c_ref[2])
        d, mn, vn = _adam_math(w_ref[...], g, m_ref[...], v_ref[...])
        g_out[...] = g
        d_out[...] = d
        m_out[...] = mn
        v_out[...] = vn

    return pl.pallas_call(body, name='small_update', out_shape=[jax.ShapeDtypeStruct(own.shape, f32)] * 4)(
        own, sib, chips, w, m, v)


def _small_chip_sum(own, sib):
    def body(o_ref, s_ref, out):
        out[...] = o_ref[...] + s_ref[...]
    return pl.pallas_call(body, name='small_chip_sum', out_shape=jax.ShapeDtypeStruct(own.shape, f32))(own, sib)


def _adamw(w, g, m, v, name):
    shape = w.shape
    w2, g2, m2, v2 = [t.reshape(-1, shape[-1]) for t in (w, g, m, v)]
    rows, cols = w2.shape
    tr = _pick(rows, (256, 128, 64, 32, 16, 8))

    def body(w_ref, g_ref, m_ref, v_ref, d_out, m_out, v_out):
        d, mn, vn = _adam_math(w_ref[...], g_ref[...], m_ref[...], v_ref[...])
        d_out[...] = d
        m_out[...] = mn
        v_out[...] = vn

    spec = pl.BlockSpec((tr, cols), lambda i: (i, 0))
    res = pl.pallas_call(body, name=name, grid=(rows // tr,), in_specs=[spec] * 4, out_specs=[spec] * 3,
                         out_shape=[jax.ShapeDtypeStruct((rows, cols), f32)] * 3,
                         compiler_params=_cparams(('parallel',)))(w2, g2, m2, v2)
    return [r.reshape(shape) for r in res]


def _dg(a, b, ca, cb):
    return lax.dot_general(a.astype(bf16), b.astype(bf16), (((ca,), (cb,)), ((), ())), preferred_element_type=f32)


@jax.custom_vjp
def _bmm(a, b):
    return _dg(a, b, 1, 0)


_bmm.defvjp(lambda a, b: (_dg(a, b, 1, 0), (a, b)), lambda r, g: (_dg(g, r[1], 1, 1), _dg(r[0], g, 0, 0)))


@jax.custom_vjp
def _bmm_nt(a, b):
    return _dg(a, b, 1, 1)


_bmm_nt.defvjp(lambda a, b: (_dg(a, b, 1, 1), (a, b)), lambda r, g: (_dg(g, r[1], 1, 0), _dg(g, r[0], 0, 0)))


@jax.custom_vjp
def _bmm_tn(a, b):
    return _dg(a, b, 0, 0)


_bmm_tn.defvjp(lambda a, b: (_dg(a, b, 0, 0), (a, b)), lambda r, g: (_dg(r[1], g, 1, 1), _dg(r[0], g, 1, 0)))


def _hdot(a, b):
    return jnp.dot(a, b, precision=HI, preferred_element_type=f32)


def _hdot_nt(a, b):
    return lax.dot_general(a, b, (((1,), (1,)), ((), ())), precision=HI, preferred_element_type=f32)


def _hdot_tn(a, b):
    return lax.dot_general(a, b, (((0,), (0,)), ((), ())), precision=HI, preferred_element_type=f32)


def _head_mask(h, width=GROUP_W):
    lane = lax.broadcasted_iota(jnp.int32, (1, width), 1)
    return ((lane >= h * HEAD_DIM) & (lane < (h + 1) * HEAD_DIM)).astype(f32)


def _rope_perm():
    p = np.zeros((LANES, LANES), np.float32)
    half = MLA_ROPE // 2
    for i in range(half):
        p[MLA_NOPE + half + i, MLA_NOPE + i] = -1.0
        p[MLA_NOPE + i, MLA_NOPE + half + i] = 1.0
    return jnp.asarray(p)


def _rope_tables(S):
    half = MLA_ROPE // 2
    freqs = ROPE_THETA ** (-jnp.arange(half, dtype=f32) / half)
    ang = jnp.arange(S, dtype=f32)[:, None] * freqs[None, :]
    cos, sin = jnp.cos(ang), jnp.sin(ang)
    ones, zeros = jnp.ones((S, MLA_NOPE), f32), jnp.zeros((S, LANES - MLA_DQK), f32)
    c_tab = jnp.concatenate([ones, cos, cos, zeros], axis=1)
    s_tab = jnp.concatenate([jnp.zeros((S, MLA_NOPE), f32), sin, sin, zeros], axis=1)
    return c_tab, s_tab


def _f_mla_pre(c_q, c_kv, krope, c_tab, s_tab, q_norm, kv_norm, wq0, wq1, wq2, wq3, wk0, wk1, wk2, wk3, wv, gq, gk, perm):
    wq, wk = (wq0, wq1, wq2, wq3), (wk0, wk1, wk2, wk3)
    nq = _rms(c_q, q_norm)
    nkv = _rms(c_kv, kv_norm)

    def norm_rope(t, g):
        t = t * lax.rsqrt(jnp.sum(t * t, axis=-1, keepdims=True) * (1.0 / MLA_DQK) + EPS) * g
        return t * c_tab + _hdot(t, perm) * s_tab

    qs = [norm_rope(_bmm(nq, wq[h]), gq) * (MLA_DQK ** -0.5) for h in range(N_HEADS)]
    ks = [norm_rope(_bmm(nkv, wk[h]) + krope, gk) for h in range(N_HEADS)]
    return (*qs, *ks, _bmm(nkv, wv))


def _f_attn(qs, ks, v, q0):
    tq, S = qs[0].shape[0], ks[0].shape[0]
    qpos = q0 + lax.broadcasted_iota(jnp.int32, (tq, S), 0)
    kpos = lax.broadcasted_iota(jnp.int32, (tq, S), 1)
    keep = kpos <= qpos
    logits = [jnp.where(keep, _bmm_nt(qs[h], ks[h]), NEG_INF) for h in range(N_HEADS)]
    ps = [jnp.exp(lg - jnp.max(lg, axis=-1, keepdims=True)) for lg in logits]
    ps = [p / jnp.sum(p, axis=-1, keepdims=True) for p in ps]
    return sum(_bmm(p, v) * _head_mask(h) for h, p in enumerate(ps))


def _mla_attn_fwd(qs, ks, v, name):
    S = v.shape[0]
    tq = Q_BLOCK

    def body(*refs):
        q_vals = [r[...] for r in refs[:4]]
        k_vals = [r[...] for r in refs[4:8]]
        refs[9][...] = _f_attn(q_vals, k_vals, refs[8][...], pl.program_id(0) * tq)

    qspec = pl.BlockSpec((tq, LANES), lambda i: (i, 0))
    return pl.pallas_call(
        body, name=name, grid=(S // tq,),
        in_specs=[qspec] * 4 + [_full_spec(k) for k in ks] + [_full_spec(v)],
        out_specs=pl.BlockSpec((tq, GROUP_W), lambda i: (i, 0)),
        out_shape=jax.ShapeDtypeStruct((S, GROUP_W), f32),
        compiler_params=_cparams(('parallel',), big=True),
    )(*qs, *ks, v)


def _mla_attn_bwd(qs, ks, v, do, name):
    S = v.shape[0]
    tq = Q_BLOCK

    def body(*refs):
        q_vals = [r[...].astype(f32) for r in refs[:4]]
        k_vals = [r[...].astype(f32) for r in refs[4:8]]
        v_val = refs[8][...].astype(f32)
        q0 = pl.program_id(0) * tq
        _, vjp = jax.vjp(lambda a, b, c: _f_attn(a, b, c, q0), q_vals, k_vals, v_val)
        dqs, dks, dv = vjp(refs[9][...])
        outs = refs[10:]
        for h in range(N_HEADS):
            outs[h][...] = dqs[h]
        first = pl.program_id(0) == 0
        for o, g in zip(outs[4:], (*dks, dv)):
            @pl.when(first)
            def _(o=o, g=g):
                o[...] = g

            @pl.when(jnp.logical_not(first))
            def _(o=o, g=g):
                o[...] += g

    qspec = pl.BlockSpec((tq, LANES), lambda i: (i, 0))
    res = pl.pallas_call(
        body, name=name, grid=(S // tq,),
        in_specs=[qspec] * 4 + [_full_spec(k) for k in ks] + [_full_spec(v), pl.BlockSpec((tq, GROUP_W), lambda i: (i, 0))],
        out_specs=[qspec] * 4 + [_full_spec(k) for k in ks] + [_full_spec(v)],
        out_shape=[jax.ShapeDtypeStruct((S, LANES), f32)] * 8 + [jax.ShapeDtypeStruct((S, GROUP_W), f32)],
        compiler_params=_cparams(('arbitrary',), big=True),
    )(*qs, *ks, v, do)
    return res[:4], res[4:8], res[8]


def _mla_params(mp):
    pad = LANES - MLA_DQK
    wq = jnp.pad(mp['mla_w_uq'].reshape(GROUP_W, N_HEADS, MLA_DQK).transpose(1, 0, 2), ((0, 0), (0, 0), (0, pad)))
    wkv = mp['mla_w_ukv'].reshape(LANES, N_HEADS, MLA_NOPE + HEAD_DIM)
    wk = jnp.pad(wkv[:, :, :MLA_NOPE].transpose(1, 0, 2), ((0, 0), (0, 0), (0, LANES - MLA_NOPE)))
    wv = wkv[:, :, MLA_NOPE:].reshape(LANES, GROUP_W)
    gq = jnp.pad(mp['mla_qk_q'], (0, pad))[None]
    gk = jnp.pad(mp['mla_qk_k'], (0, pad))[None]
    return [mp['mla_q_norm'][None], mp['mla_kv_norm'][None], *[wq[h] for h in range(N_HEADS)],
            *[wk[h] for h in range(N_HEADS)], wv, gq, gk, _rope_perm()]


def _mla_fwd(c_q, c_kv, k_rope, mp, l):
    S = c_q.shape[0]
    tm = _pick(S, (256, 128))
    krope = jnp.pad(k_rope, ((0, 0), (MLA_NOPE, LANES - MLA_DQK)))
    c_tab, s_tab = _rope_tables(S)
    tiled = [c_q, c_kv, krope, c_tab, s_tab]
    params = _mla_params(mp)
    res = _tile_fwd(_f_mla_pre, tiled, params, [(LANES, bf16)] * 8 + [(GROUP_W, bf16)], tm, f'mla_pre_fwd_{l}')
    qs, ks, v = res[:4], res[4:8], res[8]
    y = _mla_attn_fwd(qs, ks, v, f'mla_attn_fwd_{l}')
    return y, (tiled, params, qs, ks, v)


def _mla_bwd(dy, saved, l):
    tiled, params, qs, ks, v = saved
    S = dy.shape[0]
    tm = _pick(S, (256, 128))
    dqs, dks, dv = _mla_attn_bwd(qs, ks, v, dy, f'mla_attn_bwd_{l}')
    (dc_q, dc_kv, dkrope), dpar = _tile_bwd(_f_mla_pre, tiled, params, [*dqs, *dks, dv], [True, True, True, False, False],
                                            [True] * 13 + [False], tm, f'mla_pre_bwd_{l}')
    dqn, dkvn = dpar[0], dpar[1]
    dwq, dwk = jnp.stack(dpar[2:6]), jnp.stack(dpar[6:10])
    dwv, dgq, dgk = dpar[10:13]
    dw_uq = dwq[:, :, :MLA_DQK].transpose(1, 0, 2).reshape(GROUP_W, N_HEADS * MLA_DQK)
    dw_ukv = jnp.concatenate([dwk[:, :, :MLA_NOPE].transpose(1, 0, 2), dwv.reshape(LANES, N_HEADS, HEAD_DIM)],
                             axis=2).reshape(LANES, N_HEADS * (MLA_NOPE + HEAD_DIM))
    grads = {'mla_q_norm': dqn[0], 'mla_kv_norm': dkvn[0], 'mla_w_uq': dw_uq, 'mla_w_ukv': dw_ukv,
             'mla_qk_q': dgq[0, :MLA_DQK], 'mla_qk_k': dgk[0, :MLA_DQK]}
    return dc_q, dc_kv, dkrope[:, MLA_NOPE:MLA_DQK], grads


SPAN = 128


def _head_mean_matrix():
    h = np.arange(GROUP_W) // HEAD_DIM
    return jnp.asarray((h[:, None] == h[None, :]).astype(np.float32) / HEAD_DIM)


def _f_dil_pre(q, k, gq, gk, hm):
    qn = q * lax.rsqrt(_hdot(q * q, hm) + EPS) * gq * (HEAD_DIM ** -0.5)
    kn = k * lax.rsqrt(_hdot(k * k, hm) + EPS) * gk
    return qn, kn


def _f_dil_branch(qb, kp, kc, vp, vc, b0, b1, b2, b3, first):
    kcat = jnp.concatenate([kp, kc], axis=0)
    vcat = jnp.concatenate([vp, vc], axis=0)
    qi = lax.broadcasted_iota(jnp.int32, (SPAN, 2 * SPAN), 0) + SPAN
    kj = lax.broadcasted_iota(jnp.int32, (SPAN, 2 * SPAN), 1)
    delta = qi - kj
    valid = (delta >= 0) & (delta <= SPAN) & jnp.logical_not(first & (kj < SPAN))
    masks = [_head_mask(h) for h in range(N_HEADS)]
    raw = [_bmm_nt(qb * hm, kcat) for hm in masks]
    logits = [jnp.where(valid, r + bias, NEG_INF) for r, bias in zip(raw, (b0, b1, b2, b3))]
    ms = [jnp.max(lg, axis=-1, keepdims=True) for lg in logits]
    ps = [jnp.exp(lg - m) for lg, m in zip(logits, ms)]
    pvs = [_bmm(p, vcat) for p in ps]
    o = sum(pv * hm for pv, hm in zip(pvs, masks))
    m_full = sum(m * hm for m, hm in zip(ms, masks))
    l_full = sum(jnp.sum(p, axis=-1, keepdims=True) * hm for p, hm in zip(ps, masks))
    return o, m_full, l_full


def _dil_branch_specs(d, nb):
    cur = pl.BlockSpec((1, SPAN, GROUP_W), lambda r, n: (r, n, 0))
    prev = pl.BlockSpec((1, SPAN, GROUP_W), lambda r, n: (r, jnp.maximum(n - 1, 0), 0))
    bias = pl.BlockSpec((1, SPAN, 2 * SPAN), lambda r, n: (0, 0, 0))
    return cur, prev, bias


def _dil_branch_fwd(q, k, v, biases, name):
    d, L, _ = q.shape
    nb = L // SPAN
    cur, prev, bias = _dil_branch_specs(d, nb)

    def body(q_ref, kp_ref, kc_ref, vp_ref, vc_ref, b0, b1, b2, b3, o_ref, m_ref, l_ref):
        o, m, l = _f_dil_branch(q_ref[0], kp_ref[0], kc_ref[0], vp_ref[0], vc_ref[0], b0[0], b1[0], b2[0], b3[0],
                                pl.program_id(1) == 0)
        o_ref[0] = o
        m_ref[0] = m
        l_ref[0] = l

    return pl.pallas_call(
        body, name=name, grid=(d, nb), in_specs=[cur, prev, cur, prev, cur] + [bias] * 4,
        out_specs=[cur] * 3, out_shape=[jax.ShapeDtypeStruct(q.shape, f32)] * 3,
        compiler_params=_cparams(('parallel', 'parallel')),
    )(q, k, k, v, v, *biases)


def _dil_branch_bwd(q, k, v, biases, do, dm, dl, name):
    d, L, _ = q.shape
    nb = L // SPAN
    cur, prev, bias = _dil_branch_specs(d, nb)
    whole = pl.BlockSpec((1, L, GROUP_W), lambda r, n: (r, 0, 0))

    def body(q_ref, kp_ref, kc_ref, vp_ref, vc_ref, b0, b1, b2, b3, do_ref, dm_ref, dl_ref,
             dq_ref, dk_ref, dv_ref, db0, db1, db2, db3):
        r, n = pl.program_id(0), pl.program_id(1)
        first = n == 0
        _, vjp = jax.vjp(lambda *a: _f_dil_branch(*a, first), q_ref[0], kp_ref[0], kc_ref[0], vp_ref[0], vc_ref[0],
                         b0[0], b1[0], b2[0], b3[0])
        dq, dkp, dkc, dvp, dvc, g0, g1, g2, g3 = vjp((do_ref[0], dm_ref[0], dl_ref[0]))
        dq_ref[0] = dq

        @pl.when(first)
        def _():
            dk_ref[...] = jnp.zeros_like(dk_ref)
            dv_ref[...] = jnp.zeros_like(dv_ref)

        rows = pl.ds(pl.multiple_of(n * SPAN, SPAN), SPAN)
        dk_ref[0, rows, :] += dkc
        dv_ref[0, rows, :] += dvc

        @pl.when(n > 0)
        def _():
            before = pl.ds(pl.multiple_of((n - 1) * SPAN, SPAN), SPAN)
            dk_ref[0, before, :] += dkp
            dv_ref[0, before, :] += dvp

        start = first & (r == 0)
        for o, g in zip((db0, db1, db2, db3), (g0, g1, g2, g3)):
            @pl.when(start)
            def _(o=o, g=g):
                o[0] = g

            @pl.when(jnp.logical_not(start))
            def _(o=o, g=g):
                o[0] += g

    res = pl.pallas_call(
        body, name=name, grid=(d, nb), in_specs=[cur, prev, cur, prev, cur] + [bias] * 4 + [cur] * 3,
        out_specs=[cur, whole, whole] + [bias] * 4,
        out_shape=[jax.ShapeDtypeStruct(q.shape, f32)] * 3 + [jax.ShapeDtypeStruct((1, SPAN, 2 * SPAN), f32)] * 4,
        compiler_params=_cparams(('arbitrary', 'arbitrary')),
    )(q, k, k, v, v, *biases, do, dm, dl)
    return res[0], res[1], res[2], res[3:]


def _f_dil_merge(o1, m1, l1, o2, m2, l2, o3, m3, l3):
    mx = jnp.maximum(jnp.maximum(m1, m2), m3)
    w1, w2, w3 = jnp.exp(m1 - mx), jnp.exp(m2 - mx), jnp.exp(m3 - mx)
    return ((w1 * o1 + w2 * o2 + w3 * o3) / (w1 * l1 + w2 * l2 + w3 * l3),)


def _bias_onehot(dilation):
    qi = jnp.arange(SPAN, dtype=jnp.int32)[:, None] + SPAN
    kj = jnp.arange(2 * SPAN, dtype=jnp.int32)[None, :]
    bucket = _t5_bucket(jnp.clip(qi - kj, 0, SPAN) * dilation).reshape(-1)
    return (bucket[None, :] == jnp.arange(T5_BUCKETS, dtype=jnp.int32)[:, None]).astype(f32)


def _bias_tables(t5_t, onehot, name):
    N = onehot.shape[1]
    tn = _pick(N, (4096, 2048, 1024))

    def body(t_ref, oh_ref, o_ref):
        o_ref[...] = _hdot(t_ref[...], oh_ref[...])

    return pl.pallas_call(
        body, name=name, grid=(N // tn,),
        in_specs=[pl.BlockSpec((8, T5_BUCKETS), lambda i: (0, 0)), pl.BlockSpec((T5_BUCKETS, tn), lambda i: (0, i))],
        out_specs=pl.BlockSpec((8, tn), lambda i: (0, i)), out_shape=jax.ShapeDtypeStruct((8, N), f32),
        compiler_params=_cparams(('parallel',)),
    )(t5_t, onehot)


def _bias_tables_bwd(d_tab, onehot, name):
    N = onehot.shape[1]
    tn = _pick(N, (4096, 2048, 1024))

    def body(g_ref, oh_ref, o_ref):
        part = _hdot_nt(g_ref[...], oh_ref[...])

        @pl.when(pl.program_id(0) == 0)
        def _():
            o_ref[...] = part

        @pl.when(pl.program_id(0) > 0)
        def _():
            o_ref[...] += part

    return pl.pallas_call(
        body, name=name, grid=(N // tn,),
        in_specs=[pl.BlockSpec((8, tn), lambda i: (0, i)), pl.BlockSpec((T5_BUCKETS, tn), lambda i: (0, i))],
        out_specs=pl.BlockSpec((8, T5_BUCKETS), lambda i: (0, 0)), out_shape=jax.ShapeDtypeStruct((8, T5_BUCKETS), f32),
        compiler_params=_cparams(('arbitrary',)),
    )(d_tab, onehot)


def _by_residue(t, d):
    S, C = t.shape
    return t.reshape(S // d, d, C).transpose(1, 0, 2)


def _from_residue(t):
    d, L, C = t.shape
    return t.transpose(1, 0, 2).reshape(d * L, C)


def _dil_fwd(qkv, mp, l):
    S = qkv.shape[0]
    tm = _pick(S, (256, 128))
    q, k, v = qkv[:, :GROUP_W], qkv[:, GROUP_W:2 * GROUP_W], qkv[:, 2 * GROUP_W:]
    pre_params = [jnp.tile(mp['dil_q_norm'], N_HEADS)[None], jnp.tile(mp['dil_k_norm'], N_HEADS)[None], _head_mean_matrix()]
    qn, kn = _tile_fwd(_f_dil_pre, [q, k], pre_params, [(GROUP_W, f32)] * 2, tm, f'dil_pre_fwd_{l}')
    t5_t = jnp.pad(mp['t5_bias'].T, ((0, 8 - N_HEADS), (0, 0)))
    branches, outs = [], []
    for bi, (_, d) in enumerate(DIL_PAIRS):
        onehot = _bias_onehot(d)
        tab = _bias_tables(t5_t, onehot, f'dil_bias_fwd_{l}_{bi}').reshape(8, SPAN, 2 * SPAN)
        biases = [tab[h][None] for h in range(N_HEADS)]
        qd, kd, vd = _by_residue(qn, d), _by_residue(kn, d), _by_residue(v, d)
        o, m, lsum = _dil_branch_fwd(qd, kd, vd, biases, f'dil_branch_fwd_{l}_{bi}')
        branches.append((qd, kd, vd, biases, onehot))
        outs += [_from_residue(o), _from_residue(m), _from_residue(lsum)]
    (y,) = _tile_fwd(_f_dil_merge, outs, [], [(GROUP_W, f32)], tm, f'dil_merge_fwd_{l}')
    return y, (q, k, pre_params, branches, outs)


def _dil_bwd(dy, saved, l):
    q, k, pre_params, branches, outs = saved
    S = dy.shape[0]
    tm = _pick(S, (256, 128))
    douts, _ = _tile_bwd(_f_dil_merge, outs, [], [dy], [True] * 9, [], tm, f'dil_merge_bwd_{l}')
    dqn = dkn = dv = None
    dt5_t = None
    for bi, (_, d) in enumerate(DIL_PAIRS):
        qd, kd, vd, biases, onehot = branches[bi]
        do, dm, dl = [_by_residue(t, d) for t in douts[3 * bi:3 * bi + 3]]
        dq_b, dk_b, dv_b, dbias = _dil_branch_bwd(qd, kd, vd, biases, do, dm, dl, f'dil_branch_bwd_{l}_{bi}')
        d_tab = jnp.concatenate([*dbias, jnp.zeros((8 - N_HEADS, SPAN, 2 * SPAN), f32)], axis=0).reshape(8, -1)
        g_t5 = _bias_tables_bwd(d_tab, onehot, f'dil_bias_bwd_{l}_{bi}')
        dq_b, dk_b, dv_b = _from_residue(dq_b), _from_residue(dk_b), _from_residue(dv_b)
        dqn = dq_b if dqn is None else dqn + dq_b
        dkn = dk_b if dkn is None else dkn + dk_b
        dv = dv_b if dv is None else dv + dv_b
        dt5_t = g_t5 if dt5_t is None else dt5_t + g_t5
    (dq, dk), (dgq, dgk) = _tile_bwd(_f_dil_pre, [q, k], pre_params, [dqn, dkn], [True, True], [True, True, False], tm,
                                     f'dil_pre_bwd_{l}')
    grads = {'dil_q_norm': dgq.reshape(N_HEADS, HEAD_DIM).sum(0), 'dil_k_norm': dgk.reshape(N_HEADS, HEAD_DIM).sum(0),
             't5_bias': dt5_t[:N_HEADS].T}
    return jnp.concatenate([dq, dk, dv], axis=1), grads


S5_LANES = S5_G * S5_P
SCAN_SEGMENTS = 8
SCAN_W = LANES


def _f_s5_prep(bre, bim, lr, li, logdt_col, expand):
    dt = jnp.sum(jnp.exp(logdt_col) * expand, axis=0, keepdims=True)
    mag = jnp.exp(lr * dt)
    ar, ai = mag * jnp.cos(li * dt), mag * jnp.sin(li * dt)
    den = lr * lr + li * li
    nr, ni = ar - 1.0, ai
    zr = (nr * lr + ni * li) / den
    zi = (ni * lr - nr * li) / den
    bb = jnp.concatenate([zr * bre - zi * bim, zr * bim + zi * bre], axis=1)
    a_rows = jnp.broadcast_to(jnp.concatenate([ar, ai], axis=1), bb.shape)
    return bb, a_rows


def _s5_scan(x, a_rows, name, reverse=False, h=None):
    S = x.shape[0]
    NL = x.shape[1] // 2
    T = S // SCAN_SEGMENTS
    nblk = NL // SCAN_W

    def body(*refs):
        if reverse:
            xr_ref, xi_ref, ar_ref, ai_ref, pr_ref, pi_ref, hr_ref, hi_ref, dar_ref, dai_ref = refs
        else:
            xr_ref, xi_ref, ar_ref, ai_ref, hr_ref, hi_ref = refs
        ar = ar_ref[...]
        ai = -ai_ref[...] if reverse else ai_ref[...]
        zero = jnp.zeros((SCAN_SEGMENTS, SCAN_W), f32)

        def tile(j):
            return pl.ds(j, SCAN_SEGMENTS, stride=T)

        def at(s):
            return T - 1 - s if reverse else s

        def local(s, c):
            hr, hi, pr, pi = c
            j = at(s)
            nhr = ar * hr - ai * hi + xr_ref[tile(j), :]
            nhi = ar * hi + ai * hr + xi_ref[tile(j), :]
            hr_ref[tile(j), :] = nhr
            hi_ref[tile(j), :] = nhi
            return nhr, nhi, ar * pr - ai * pi, ar * pi + ai * pr

        er, ei, pr, pi = lax.fori_loop(0, T, local, (zero, zero, zero + 1.0, zero), unroll=2)
        row = lax.broadcasted_iota(jnp.int32, (SCAN_SEGMENTS, SCAN_W), 0)
        cr, ci = zero, zero
        order = range(SCAN_SEGMENTS - 2, -1, -1) if reverse else range(1, SCAN_SEGMENTS)
        for k in order:
            src = k + 1 if reverse else k - 1
            tr = er + pr * cr - pi * ci
            ti = ei + pr * ci + pi * cr
            cr = jnp.where(row == k, jnp.sum(jnp.where(row == src, tr, 0.0), axis=0, keepdims=True), cr)
            ci = jnp.where(row == k, jnp.sum(jnp.where(row == src, ti, 0.0), axis=0, keepdims=True), ci)

        def fix_at(j, c, before):
            pr, pi, sr, si = c
            pr, pi = ar * pr - ai * pi, ar * pi + ai * pr
            hr = hr_ref[tile(j), :] + pr * cr - pi * ci
            hi = hi_ref[tile(j), :] + pr * ci + pi * cr
            hr_ref[tile(j), :] = hr
            hi_ref[tile(j), :] = hi
            if reverse:
                qr, qi = before
                sr = sr + hr * qr + hi * qi
                si = si + hi * qr - hr * qi
            return pr, pi, sr, si

        start = (zero + 1.0, zero, zero, zero)
        if reverse:
            def fix(s, c):
                j = T - 1 - s
                return fix_at(j, c, (pr_ref[tile(j - 1), :], pi_ref[tile(j - 1), :]))

            c = lax.fori_loop(0, T - 1, fix, start, unroll=2)
            last_r = jnp.where(row == 0, 0.0, pltpu.roll(pr_ref[tile(T - 1), :], 1, 0))
            last_i = jnp.where(row == 0, 0.0, pltpu.roll(pi_ref[tile(T - 1), :], 1, 0))
            _, _, sr, si = fix_at(0, c, (last_r, last_i))
            dar_ref[...] = sr
            dai_ref[...] = si
        else:
            lax.fori_loop(0, T, lambda s, c: fix_at(s, c, None), start, unroll=2)

    re = pl.BlockSpec((S, SCAN_W), lambda b: (0, b))
    im = pl.BlockSpec((S, SCAN_W), lambda b: (0, nblk + b))
    a_re = pl.BlockSpec((SCAN_SEGMENTS, SCAN_W), lambda b: (0, b))
    a_im = pl.BlockSpec((SCAN_SEGMENTS, SCAN_W), lambda b: (0, nblk + b))
    in_specs, args = [re, im, a_re, a_im], [x, x, a_rows, a_rows]
    out_specs = [re, re]
    out_shape = [jax.ShapeDtypeStruct((S, NL), f32)] * 2
    if reverse:
        in_specs += [re, re]
        args += [h[0], h[1]]
        out_specs += [a_re, a_re]
        out_shape += [jax.ShapeDtypeStruct((SCAN_SEGMENTS, NL), f32)] * 2
    return pl.pallas_call(body, name=name, grid=(nblk,), in_specs=in_specs, out_specs=out_specs, out_shape=out_shape,
                          compiler_params=_cparams(('parallel',), big=True))(*args)


def _f_s5_post(y, u, d, w_glu):
    z = _bmm(y + d * u, w_glu)
    return (z[:, :GROUP_W] * jax.nn.sigmoid(z[:, GROUP_W:]),)


def _block_diag(t):
    G, a, b = t.shape
    eye = jnp.eye(G, dtype=t.dtype)
    return (t[:, :, None, :] * eye[:, None, :, None]).reshape(G * a, G * b)


def _diag_blocks(m, a, b):
    G = m.shape[0] // a
    return jnp.moveaxis(jnp.diagonal(m.reshape(G, a, G, b), axis1=0, axis2=2), -1, 0)


def _s5_fwd(u, mp, l):
    S = u.shape[0]
    tm = _pick(S, (256, 128))
    bre = _block_diag(mp['s5_b_re'].transpose(0, 2, 1))
    bim = _block_diag(mp['s5_b_im'].transpose(0, 2, 1))
    expand = jnp.repeat(jnp.eye(S5_G, dtype=f32), S5_P, axis=1)
    prep_params = [mp['s5_lambda_re'].reshape(1, S5_LANES), mp['s5_lambda_im'].reshape(1, S5_LANES),
                   mp['s5_log_dt'].reshape(S5_G, 1), expand]
    bb, a_rows = _tile_fwd(_f_s5_prep, [bre, bim], prep_params, [(2 * S5_LANES, f32)] * 2, GROUP_W, f's5_prep_fwd_{l}')
    x = _mm(u, bb, 'nn', f's5_in_fwd_{l}')
    hr, hi = _s5_scan(x, a_rows, f's5_scan_fwd_{l}')
    c_re, c_im = _block_diag(mp['s5_c_re'].transpose(0, 2, 1)), -_block_diag(mp['s5_c_im'].transpose(0, 2, 1))
    y = _mm(hi, c_im, 'nn', f's5_out_im_fwd_{l}', add=_mm(hr, c_re, 'nn', f's5_out_re_fwd_{l}'))
    post_params = [mp['s5_d'][None], mp['s5_w_glu']]
    (out,) = _tile_fwd(_f_s5_post, [y, u], post_params, [(GROUP_W, f32)], tm, f's5_post_fwd_{l}')
    return out, (u, bre, bim, prep_params, bb, a_rows, hr, hi, c_re, c_im, y, post_params)


def _s5_bwd(dout, saved, l):
    u, bre, bim, prep_params, bb, a_rows, hr, hi, c_re, c_im, y, post_params = saved
    S = u.shape[0]
    tm = _pick(S, (256, 128))
    (dy, du1), (dd, dwglu) = _tile_bwd(_f_s5_post, [y, u], post_params, [dout], [True, True], [True, True], tm,
                                       f's5_post_bwd_{l}')
    ccat = jnp.concatenate([c_re, c_im], axis=0)
    dh = _mm(dy, ccat, 'nt', f's5_out_dx_{l}')
    dccat = jnp.concatenate([_mm(hr, dy, 'tn', f's5_out_re_dw_{l}'), _mm(hi, dy, 'tn', f's5_out_im_dw_{l}')], axis=0)
    lr_, li_, dar, dai = _s5_scan(dh, a_rows, f's5_scan_bwd_{l}', reverse=True, h=(hr, hi))
    du2 = _mm(li_, bb[:, S5_LANES:], 'nt', f's5_in_im_dx_{l}', add=_mm(lr_, bb[:, :S5_LANES], 'nt', f's5_in_re_dx_{l}'))
    dbb = jnp.concatenate([_mm(u, lr_, 'tn', f's5_in_re_dw_{l}'), _mm(u, li_, 'tn', f's5_in_im_dw_{l}')], axis=1)
    da_rows = jnp.pad(jnp.concatenate([dar, dai], axis=1), ((0, GROUP_W - SCAN_SEGMENTS), (0, 0)))
    (dbre, dbim), (dlr, dli, dlogdt) = _tile_bwd(_f_s5_prep, [bre, bim], prep_params, [dbb, da_rows], [True, True],
                                                 [True, True, True, False], GROUP_W, f's5_prep_bwd_{l}')
    grads = {
        's5_lambda_re': dlr.reshape(S5_G, S5_P), 's5_lambda_im': dli.reshape(S5_G, S5_P), 's5_log_dt': dlogdt[:, 0],
        's5_b_re': _diag_blocks(dbre, S5_CG, S5_P).transpose(0, 2, 1),
        's5_b_im': _diag_blocks(dbim, S5_CG, S5_P).transpose(0, 2, 1),
        's5_c_re': _diag_blocks(dccat[:S5_LANES], S5_P, S5_CG).transpose(0, 2, 1),
        's5_c_im': -_diag_blocks(dccat[S5_LANES:], S5_P, S5_CG).transpose(0, 2, 1),
        's5_d': dd[0], 's5_w_glu': dwglu}
    return du1 + du2, grads


DN_CONV = 4


def _head_sum_matrix():
    h = np.arange(GROUP_W) // HEAD_DIM
    return jnp.asarray((h[:, None] == h[None, :]).astype(np.float32))


def _f_dn_pre(x0, x1, x2, x3, ab, w0, w1, w2, w3, alog, dtb, ea, eb, hs):
    c = w0 * x0 + w1 * x1 + w2 * x2 + w3 * x3
    s = c * jax.nn.sigmoid(c)
    q, k, v = s[:, :GROUP_W], s[:, GROUP_W:2 * GROUP_W], s[:, 2 * GROUP_W:]
    q = q * lax.rsqrt(_hdot(q * q, hs) + EPS) * (HEAD_DIM ** -0.5)
    k = k * lax.rsqrt(_hdot(k * k, hs) + EPS)
    beta = jax.nn.sigmoid(_hdot(ab, eb))
    g = -jnp.exp(alog) * jax.nn.softplus(_hdot(ab, ea) + dtb)
    return q, k, v, g, beta


DN_CHUNKS_PER_STEP = 4


def _f_dn_chunks(q, k, v, g, beta):
    C = DN_CHUNK
    n_chunks = q.shape[0] // C
    r = lax.broadcasted_iota(jnp.int32, (C, C), 0)
    c = lax.broadcasted_iota(jnp.int32, (C, C), 1)
    causal, strict = r >= c, r > c
    eye = (r == c).astype(f32)
    tril = causal.astype(f32)
    ones = jnp.ones((C, GROUP_W), f32)
    masks = [_head_mask(h) for h in range(N_HEADS)]
    rows = [tuple(t[i * C:(i + 1) * C] for t in (q, k, v, g, beta)) for i in range(n_chunks)]
    gcs = [_hdot(tril, gi) for (_, _, _, gi, _) in rows]
    items = [(i, h) for i in range(n_chunks) for h in range(N_HEADS)]
    grows = [_hdot_nt(ones * (masks[h] * (1.0 / HEAD_DIM)), gcs[i]) for i, h in items]
    decs = []
    for (i, h), grow in zip(items, grows):
        gcol = jnp.sum(gcs[i] * masks[h], axis=1, keepdims=True) * (1.0 / HEAD_DIM)
        decs.append(jnp.exp(jnp.where(causal, gcol - grow, NEG_INF)))
    kbs = [ki * bi for (_, ki, _, _, bi) in rows]
    kks = [_bmm_nt(kbs[i] * masks[h], rows[i][1]) for i, h in items]
    qks = [_bmm_nt(rows[i][0] * masks[h], rows[i][1]) for i, h in items]
    lmats = [jnp.where(strict, kk * dec, 0.0) for kk, dec in zip(kks, decs)]
    a_qk = [jnp.where(causal, qk * dec, 0.0) for qk, dec in zip(qks, decs)]
    ts = [eye - lm for lm in lmats]
    ps = lmats
    for _ in range(5):
        ps = [_bmm(p, p) for p in ps]
        ts = [t + _bmm(t, p) for t, p in zip(ts, ps)]
    egs = [jnp.exp(gc) for gc in gcs]
    tw = [_bmm(t, kbs[i] * egs[i]) for (i, h), t in zip(items, ts)]
    tu = [_bmm(t, rows[i][2] * rows[i][4]) for (i, h), t in zip(items, ts)]
    outs = []
    for i in range(n_chunks):
        qi, ki, _, gi, _ = rows[i]
        glast = jnp.sum(gi, axis=0, keepdims=True)
        w = sum(tw[i * N_HEADS + h] * masks[h] for h in range(N_HEADS))
        u = sum(tu[i * N_HEADS + h] * masks[h] for h in range(N_HEADS))
        outs.append((w, u, qi * egs[i], ki * jnp.exp(glast - gcs[i]), *a_qk[i * N_HEADS:(i + 1) * N_HEADS],
                     jnp.broadcast_to(jnp.exp(glast), (C, GROUP_W))))
    return tuple(jnp.concatenate(parts, axis=0) for parts in zip(*outs))


def _f_dn_step(w, u, qd, kdec, a0, a1, a2, a3, dfull, state, bd):
    row0 = (lax.broadcasted_iota(jnp.int32, dfull.shape, 0) == 0).astype(f32)
    dvec = jnp.sum(dfull * row0, axis=0, keepdims=True)
    ws, qs = _bmm(w, state), _bmm(qd, state)
    vnew = u - ws
    avs = [_bmm(a, vnew) for a in (a0, a1, a2, a3)]
    kv = _bmm_tn(kdec, vnew)
    o = qs + sum(av * _head_mask(h) for h, av in enumerate(avs))
    return o, state * dvec + bd * kv


def _dn_scan_fwd(ins, name):
    S = ins[0].shape[0]
    N = S // DN_CHUNK
    bd = _head_sum_matrix()

    def body(*refs):
        o_ref, s_ref, state = refs[10], refs[11], refs[12]

        @pl.when(pl.program_id(0) == 0)
        def _():
            state[...] = jnp.zeros_like(state)

        s_in = state[...]
        s_ref[0] = s_in
        o, s_out = _f_dn_step(*[r[...] for r in refs[:9]], s_in, refs[9][...])
        o_ref[...] = o
        state[...] = s_out

    return pl.pallas_call(
        body, name=name, grid=(N,),
        in_specs=[pl.BlockSpec((DN_CHUNK, t.shape[1]), lambda n: (n, 0)) for t in ins] + [_full_spec(bd)],
        out_specs=[pl.BlockSpec((DN_CHUNK, GROUP_W), lambda n: (n, 0)), pl.BlockSpec((1, GROUP_W, GROUP_W), lambda n: (n, 0, 0))],
        out_shape=[jax.ShapeDtypeStruct((S, GROUP_W), f32), jax.ShapeDtypeStruct((N, GROUP_W, GROUP_W), f32)],
        scratch_shapes=[pltpu.VMEM((GROUP_W, GROUP_W), f32)],
        compiler_params=_cparams(('arbitrary',)),
    )(*ins, bd)


def _dn_scan_bwd(ins, states, do, name):
    S = ins[0].shape[0]
    N = S // DN_CHUNK
    bd = _head_sum_matrix()

    def body(*refs):
        s_ref, do_ref = refs[9], refs[10]
        bd_ref = refs[11]
        outs = refs[12:21]
        dstate = refs[21]

        @pl.when(pl.program_id(0) == 0)
        def _():
            dstate[...] = jnp.zeros_like(dstate)

        bd_val = bd_ref[...]
        _, vjp = jax.vjp(lambda *a: _f_dn_step(*a, bd_val), *[r[...] for r in refs[:9]], s_ref[0])
        grads = vjp((do_ref[...], dstate[...]))
        for o, g in zip(outs, grads[:9]):
            o[...] = g
        dstate[...] = grads[9]

    def rev(n):
        return (N - 1 - n, 0)

    res = pl.pallas_call(
        body, name=name, grid=(N,),
        in_specs=[pl.BlockSpec((DN_CHUNK, t.shape[1]), rev) for t in ins] +
                 [pl.BlockSpec((1, GROUP_W, GROUP_W), lambda n: (N - 1 - n, 0, 0)), pl.BlockSpec((DN_CHUNK, GROUP_W), rev),
                  _full_spec(bd)],
        out_specs=[pl.BlockSpec((DN_CHUNK, t.shape[1]), rev) for t in ins],
        out_shape=[jax.ShapeDtypeStruct(t.shape, f32) for t in ins],
        scratch_shapes=[pltpu.VMEM((GROUP_W, GROUP_W), f32)],
        compiler_params=_cparams(('arbitrary',)),
    )(*ins, states, do, bd)
    return list(res)


def _f_dn_post(o, gate, gain, hmean):
    return (o * lax.rsqrt(_hdot(o * o, hmean) + EPS) * gain * (gate * jax.nn.sigmoid(gate)),)


def _delay(t, j):
    return t if j == 0 else jnp.pad(t[:-j], ((j, 0), (0, 0)))


def _advance(t, j):
    return t if j == 0 else jnp.pad(t[j:], ((0, j), (0, 0)))


def _dn_fwd(qkv, a, b, gate, mp, l):
    S = qkv.shape[0]
    tm = _pick(S, (256, 128))
    xs = [_delay(qkv, DN_CONV - 1 - j) for j in range(DN_CONV)]
    ab = jnp.pad(jnp.concatenate([a, b], axis=1), ((0, 0), (0, LANES - 2 * N_HEADS)))
    sel = np.zeros((2, LANES, GROUP_W), np.float32)
    for h in range(N_HEADS):
        sel[0, h, h * HEAD_DIM:(h + 1) * HEAD_DIM] = 1.0
        sel[1, N_HEADS + h, h * HEAD_DIM:(h + 1) * HEAD_DIM] = 1.0
    pre_params = [*[mp['dn_conv'][j][None] for j in range(DN_CONV)], jnp.repeat(mp['dn_a_log'], HEAD_DIM)[None],
                  jnp.repeat(mp['dn_dt_bias'], HEAD_DIM)[None], jnp.asarray(sel[0]), jnp.asarray(sel[1]), _head_sum_matrix()]
    pre = _tile_fwd(_f_dn_pre, [*xs, ab], pre_params, [(GROUP_W, f32)] * 5, tm, f'dn_pre_fwd_{l}')
    chunk_outs = [(GROUP_W, f32)] * 4 + [(HEAD_DIM, f32)] * 4 + [(GROUP_W, f32)]
    parts = _tile_fwd(_f_dn_chunks, pre, [], chunk_outs, DN_CHUNK * DN_CHUNKS_PER_STEP, f'dn_chunk_fwd_{l}')
    o, states = _dn_scan_fwd(parts, f'dn_scan_fwd_{l}')
    post_params = [jnp.tile(mp['dn_o_norm'], N_HEADS)[None], _head_mean_matrix()]
    (y,) = _tile_fwd(_f_dn_post, [o, gate], post_params, [(GROUP_W, f32)], tm, f'dn_post_fwd_{l}')
    return y, (xs, ab, pre_params, pre, parts, states, o, gate, post_params)


def _dn_bwd(dy, saved, l):
    xs, ab, pre_params, pre, parts, states, o, gate, post_params = saved
    S = dy.shape[0]
    tm = _pick(S, (256, 128))
    (do, dgate), (dgain,) = _tile_bwd(_f_dn_post, [o, gate], post_params, [dy], [True, True], [True, False], tm,
                                      f'dn_post_bwd_{l}')
    dparts = _dn_scan_bwd(parts, states, do, f'dn_scan_bwd_{l}')
    dpre, _ = _tile_bwd(_f_dn_chunks, pre, [], dparts, [True] * 5, [], DN_CHUNK * DN_CHUNKS_PER_STEP, f'dn_chunk_bwd_{l}')
    dins, dpar = _tile_bwd(_f_dn_pre, [*xs, ab], pre_params, dpre, [True] * 5, [True] * 6 + [False] * 3, tm,
                           f'dn_pre_bwd_{l}')
    dqkv = dins[DN_CONV - 1]
    for j in range(DN_CONV - 1):
        dqkv = dqkv + _advance(dins[j], DN_CONV - 1 - j)
    dab = dins[DN_CONV]
    grads = {'dn_conv': jnp.concatenate(dpar[:DN_CONV], axis=0),
             'dn_a_log': dpar[4].reshape(N_HEADS, HEAD_DIM).sum(1), 'dn_dt_bias': dpar[5].reshape(N_HEADS, HEAD_DIM).sum(1),
             'dn_o_norm': dgain.reshape(N_HEADS, HEAD_DIM).sum(0)}
    return dqkv, dab[:, :N_HEADS], dab[:, N_HEADS:2 * N_HEADS], dgate, grads


def _t5_bucket(dist):
    exact = T5_BUCKETS // 2
    df = jnp.maximum(dist, 1).astype(f32)
    large = exact + (jnp.log(df / exact) / math.log(T5_MAX_DIST / exact) * (T5_BUCKETS - exact)).astype(jnp.int32)
    large = jnp.minimum(large, T5_BUCKETS - 1)
    return jnp.where(dist < exact, dist, large)


def _split_cols(t, sizes):
    out, start = [], 0
    for s in sizes:
        out.append(t[..., start:start + s])
        start += s
    return out


def _mixers_fwd(proj, mp, l):
    c_q, c_kv, k_rope, u_s5, qkv_dil, qkv_dn, a_dn, b_dn, gate_dn = _split_cols(proj, IN_SPLITS)
    y_mla, s_mla = _mla_fwd(c_q, c_kv, k_rope, mp, l)
    y_s5, s_s5 = _s5_fwd(u_s5, mp, l)
    y_dil, s_dil = _dil_fwd(qkv_dil, mp, l)
    y_dn, s_dn = _dn_fwd(qkv_dn, a_dn, b_dn, gate_dn, mp, l)
    return jnp.concatenate([y_mla, y_s5, y_dil, y_dn], axis=-1), (s_mla, s_s5, s_dil, s_dn)


def _mixers_bwd(dmixed, saved, l):
    s_mla, s_s5, s_dil, s_dn = saved
    d_mla, d_s5, d_dil, d_dn = _split_cols(dmixed, (GROUP_W,) * 4)
    dc_q, dc_kv, dk_rope, g_mla = _mla_bwd(d_mla, s_mla, l)
    du, g_s5 = _s5_bwd(d_s5, s_s5, l)
    dqkv_dil, g_dil = _dil_bwd(d_dil, s_dil, l)
    dqkv_dn, da, db, dgate, g_dn = _dn_bwd(d_dn, s_dn, l)
    parts = [dc_q, dc_kv, dk_rope, du, dqkv_dil, dqkv_dn, da, db, dgate]
    dproj = jnp.concatenate([p.astype(bf16) for p in parts], axis=-1)
    return dproj, {**g_mla, **g_s5, **g_dil, **g_dn}


MIXER_PARAMS = ['mla_q_norm', 'mla_kv_norm', 'mla_w_uq', 'mla_w_ukv', 'mla_qk_q', 'mla_qk_k', 's5_lambda_re',
                's5_lambda_im', 's5_log_dt', 's5_b_re', 's5_b_im', 's5_c_re', 's5_c_im', 's5_d', 's5_w_glu',
                'dil_q_norm', 'dil_k_norm', 't5_bias', 'dn_conv', 'dn_a_log', 'dn_dt_bias', 'dn_o_norm']


def _layer_fwd(h, W, l):
    S = h.shape[0]
    tm = _pick(S, (256, 128))
    g1 = W['attn_norm'][l][None]
    g2 = W['ffn_norm'][l][None]
    (n1,) = _tile_fwd(_f_rms, [h], [g1], [(D_MODEL, bf16)], tm, f'rms1_fwd_{l}')
    proj = _mm(n1, W['w_in'][l], 'nn', f'proj_fwd_{l}')
    mp = {k: (W[k] if k == 't5_bias' else W[k][l]).astype(f32) for k in MIXER_PARAMS}
    mixed, mix_saved = _mixers_fwd(proj, mp, l)
    mixed_b = mixed.astype(bf16)
    h2 = _mm(mixed_b, W['w_out'][l], 'nn', f'out_fwd_{l}', add=h)
    (n2,) = _tile_fwd(_f_rms, [h2], [g2], [(D_MODEL, bf16)], tm, f'rms2_fwd_{l}')
    w13 = jnp.concatenate([W['ffn_w1'][l], W['ffn_w3'][l]], axis=1)
    uv = _mm(n2, w13, 'nn', f'ffn13_fwd_{l}')
    (act,) = _tile_fwd(_f_swiglu, [uv], [], [(FFN_HIDDEN, bf16)], tm, f'swiglu_fwd_{l}')
    h3 = _mm(act, W['ffn_w2'][l], 'nn', f'ffn2_fwd_{l}', add=h2)
    saved = dict(h=h, n1=n1, mix=mix_saved, mixed=mixed_b, h2=h2, n2=n2, uv=uv, act=act, w13=w13)
    return h3, saved


def _layer_bwd(dh3, saved, W, l):
    S = dh3.shape[0]
    tm = _pick(S, (256, 128))
    g1 = W['attn_norm'][l][None]
    g2 = W['ffn_norm'][l][None]
    grads = {}
    dact = _mm(dh3, W['ffn_w2'][l], 'nt', f'ffn2_dx_{l}')
    grads['ffn_w2'] = _mm(saved['act'], dh3, 'tn', f'ffn2_dw_{l}')
    (duv,), _ = _tile_bwd(_f_swiglu, [saved['uv']], [], [dact], [True], [], tm, f'swiglu_bwd_{l}', dt_dtypes=[bf16])
    dn2 = _mm(duv, saved['w13'], 'nt', f'ffn13_dx_{l}')
    dw13 = _mm(saved['n2'], duv, 'tn', f'ffn13_dw_{l}')
    grads['ffn_w1'], grads['ffn_w3'] = dw13[:, :FFN_HIDDEN], dw13[:, FFN_HIDDEN:]
    (dh2n,), (dg2,) = _tile_bwd(_f_rms, [saved['h2']], [g2], [dn2], [True], [True], tm, f'rms2_bwd_{l}')
    grads['ffn_norm'] = dg2[0]
    dh2 = dh3 + dh2n
    dmixed = _mm(dh2, W['w_out'][l], 'nt', f'out_dx_{l}')
    grads['w_out'] = _mm(saved['mixed'], dh2, 'tn', f'out_dw_{l}')
    dproj, dmp = _mixers_bwd(dmixed, saved['mix'], l)
    for k in MIXER_PARAMS:
        grads[k] = dmp[k]
    dn1 = _mm(dproj, W['w_in'][l], 'nt', f'proj_dx_{l}')
    grads['w_in'] = _mm(saved['n1'], dproj, 'tn', f'proj_dw_{l}')
    (dh1n,), (dg1,) = _tile_bwd(_f_rms, [saved['h']], [g1], [dn1], [True], [True], tm, f'rms1_bwd_{l}')
    grads['attn_norm'] = dg1[0]
    return dh2 + dh1n, grads


def kernel(x, attn_norm, w_in, w_out, mla_q_norm, mla_kv_norm, mla_w_uq, mla_w_ukv, mla_qk_q, mla_qk_k, s5_lambda_re, s5_lambda_im, s5_log_dt, s5_b_re, s5_b_im, s5_c_re, s5_c_im, s5_d, s5_w_glu, dil_q_norm, dil_k_norm, t5_bias, dn_conv, dn_a_log, dn_dt_bias, dn_o_norm, ffn_norm, ffn_w1, ffn_w3, ffn_w2, loss_target, m_attn_norm, m_w_in, m_w_out, m_mla_q_norm, m_mla_kv_norm, m_mla_w_uq, m_mla_w_ukv, m_mla_qk_q, m_mla_qk_k, m_s5_lambda_re, m_s5_lambda_im, m_s5_log_dt, m_s5_b_re, m_s5_b_im, m_s5_c_re, m_s5_c_im, m_s5_d, m_s5_w_glu, m_dil_q_norm, m_dil_k_norm, m_t5_bias, m_dn_conv, m_dn_a_log, m_dn_dt_bias, m_dn_o_norm, m_ffn_norm, m_ffn_w1, m_ffn_w3, m_ffn_w2, v_attn_norm, v_w_in, v_w_out, v_mla_q_norm, v_mla_kv_norm, v_mla_w_uq, v_mla_w_ukv, v_mla_qk_q, v_mla_qk_k, v_s5_lambda_re, v_s5_lambda_im, v_s5_log_dt, v_s5_b_re, v_s5_b_im, v_s5_c_re, v_s5_c_im, v_s5_d, v_s5_w_glu, v_dil_q_norm, v_dil_k_norm, v_t5_bias, v_dn_conv, v_dn_a_log, v_dn_dt_bias, v_dn_o_norm, v_ffn_norm, v_ffn_w1, v_ffn_w3, v_ffn_w2):
    given = dict(locals())
    w_loc = {n: given[n] for n in WEIGHTS}
    m_loc = {n: given['m_' + n] for n in WEIGHTS}
    v_loc = {n: given['v_' + n] for n in WEIGHTS}
    big_names = list(BIG)

    gathered = _gather_tensors([w_loc[n].astype(bf16) for n in big_names])
    W = {n: _from_shards(n, g) for n, g in zip(big_names, gathered)}
    for n in SMALL:
        W[n] = w_loc[n]

    h = x[0]
    saved = []
    for l in range(DEPTH):
        h, sv = _layer_fwd(h, W, l)
        saved.append(sv)
    parts_loss, dh = _loss_head(h, loss_target[0])
    loss = lax.psum(jnp.sum(parts_loss), ('x', 'y', 'c'))

    layer_grads = [None] * DEPTH
    for l in reversed(range(DEPTH)):
        dh, layer_grads[l] = _layer_bwd(dh, saved[l], W, l)
    grad_x = dh[None]
    small_full = []
    for n in SMALL:
        if n == 't5_bias':
            small_full.append(layer_grads[0][n] + layer_grads[1][n])
        else:
            small_full.append(jnp.stack([layer_grads[l][n] for l in range(DEPTH)]))

    gs = [jnp.stack([_by_shard(n, layer_grads[l][n]).astype(bf16) for l in range(DEPTH)], axis=1) for n in big_names]
    small_shapes = [w_loc[n].shape for n in SMALL]
    Rs = _pack_rows(sum(math.prod(s) for s in small_shapes), 8)
    small_pack = _pack(small_full, Rs, f32)
    from_sib, recv_small = _swap_with_sibling(gs, small_pack)
    chip_sums = [_chip_sum_of(g, r, 'chip_sum_' + n) for n, g, r in zip(big_names, gs, from_sib)]
    chip_small = _small_chip_sum(small_pack, recv_small)
    from_chips, from_chips_small = _exchange_between_chips(chip_sums, chip_small)
    totals = [_shard_total_of(g, r, rc, 'shard_total_' + n) for n, g, r, rc in zip(big_names, gs, from_sib, from_chips)]
    g_big = _join_with_sibling(totals)

    g_small_p, d_small_p, m_small_p, v_small_p = _small_update(
        small_pack, recv_small, from_chips_small, _pack([w_loc[n] for n in SMALL], Rs, f32),
        _pack([m_loc[n] for n in SMALL], Rs, f32), _pack([v_loc[n] for n in SMALL], Rs, f32))
    grad, delta, new_m, new_v = {}, {}, {}, {}
    for n, g_, d_, m_, v_ in zip(SMALL, _unpack(g_small_p, small_shapes), _unpack(d_small_p, small_shapes),
                                 _unpack(m_small_p, small_shapes), _unpack(v_small_p, small_shapes)):
        grad[n], delta[n], new_m[n], new_v[n] = g_, d_, m_, v_
    for n, g_ in zip(big_names, g_big):
        grad[n] = g_
        delta[n], new_m[n], new_v[n] = _adamw(w_loc[n], g_, m_loc[n], v_loc[n], 'adamw_' + n)
    return (loss, grad_x, *[grad[n] for n in WEIGHTS], *[delta[n] for n in WEIGHTS],
            *[new_m[n] for n in WEIGHTS], *[new_v[n] for n in WEIGHTS])
```

```python
import functools
import math

import numpy as np
import jax
import jax.numpy as jnp
from jax import lax
from jax.experimental import pallas as pl
from jax.experimental.pallas import tpu as pltpu

f32 = jnp.float32
bf16 = jnp.bfloat16
HI = lax.Precision.HIGHEST
MESH = pl.DeviceIdType.MESH

VMEM_LIMIT_BYTES = 48 * 1024 * 1024
MM_VMEM_BUDGET_BYTES = 32 * 1024 * 1024
LANES = 128

D_MODEL = 1024
DEPTH = 2
GROUP_W = 256
HEAD_DIM = 64
EPS = 1e-6
NEG_INF = -1e30
N_HEADS = 4
MLA_NOPE, MLA_ROPE = 64, 32
MLA_DQK = MLA_NOPE + MLA_ROPE
ROPE_THETA = 10000.0
Q_BLOCK = 128
S5_G, S5_CG, S5_P = 16, 16, 64
DIL_PAIRS = ((128, 1), (512, 4), (2048, 16))
T5_BUCKETS, T5_MAX_DIST = 32, 2048
DN_CHUNK = 64
FFN_HIDDEN = 2816
IN_SPLITS = (256, 128, 32, 256, 768, 768, 4, 4, 256)
IN_COLS = sum(IN_SPLITS)

ADAM_LR, ADAM_B1, ADAM_B2, ADAM_EPS, ADAM_WD, ADAM_STEP = 0.001, 0.9, 0.999, 1e-08, 0.01, 10

WEIGHTS = ['attn_norm', 'w_in', 'w_out', 'mla_q_norm', 'mla_kv_norm', 'mla_w_uq', 'mla_w_ukv', 'mla_qk_q', 'mla_qk_k',
           's5_lambda_re', 's5_lambda_im', 's5_log_dt', 's5_b_re', 's5_b_im', 's5_c_re', 's5_c_im', 's5_d', 's5_w_glu',
           'dil_q_norm', 'dil_k_norm', 't5_bias', 'dn_conv', 'dn_a_log', 'dn_dt_bias', 'dn_o_norm', 'ffn_norm',
           'ffn_w1', 'ffn_w3', 'ffn_w2']
BIG = {'w_in': 2, 'w_out': 1, 'mla_w_uq': 2, 'mla_w_ukv': 2, 's5_w_glu': 2, 'dn_conv': 2, 'ffn_w1': 2, 'ffn_w3': 2,
       'ffn_w2': 1}
SMALL = [n for n in WEIGHTS if n not in BIG]
N_SHARDS = 4
PACK_COLS = 1024


def _cparams(sem=None, big=False):
    kw = {}
    if sem is not None:
        kw['dimension_semantics'] = sem
    if big:
        kw['vmem_limit_bytes'] = VMEM_LIMIT_BYTES
    return pltpu.CompilerParams(**kw)


def _pick(n, prefs):
    for p in prefs:
        if p <= n and n % p == 0:
            return p
    return n


def _lane_tile(n, cap):
    for t in range(cap - cap % LANES, 0, -LANES):
        if n % t == 0:
            return t
    return n


def _mm(a, b, mode, name, add=None, out_dtype=f32):
    if mode == 'nn':
        (M, K), (K2, N) = a.shape, b.shape
    elif mode == 'nt':
        (M, K), (N, K2) = a.shape, b.shape
    else:
        (K, M), (K2, N) = a.shape, b.shape
    assert K == K2, (name, a.shape, b.shape)
    tk = K if K <= 2816 else _pick(K, (2816, 2048, 1408, 1024, 512))
    cap_m, cap_n = (1408 if mode == 'tn' else 512), 1408

    def need(tm_, tn_):
        per_step = tm_ * tk * a.dtype.itemsize + tk * tn_ * b.dtype.itemsize + tm_ * tn_ * jnp.dtype(out_dtype).itemsize
        if add is not None:
            per_step += tm_ * tn_ * add.dtype.itemsize
        return 2 * per_step + tm_ * tn_ * 4

    tm, tn = _lane_tile(M, cap_m), _lane_tile(N, cap_n)
    while need(tm, tn) > MM_VMEM_BUDGET_BYTES and cap_m > LANES:
        cap_m //= 2
        tm = _lane_tile(M, cap_m)
    nk = K // tk
    dims = {'nn': (((1,), (0,)), ((), ())), 'nt': (((1,), (1,)), ((), ())), 'tn': (((0,), (0,)), ((), ()))}[mode]
    has_add = add is not None

    def body(*refs):
        a_ref, b_ref = refs[0], refs[1]
        add_ref = refs[2] if has_add else None
        o_ref = refs[3] if has_add else refs[2]
        part = lax.dot_general(a_ref[...].astype(bf16), b_ref[...].astype(bf16), dims, preferred_element_type=f32)
        if nk == 1:
            if has_add:
                part = part + add_ref[...].astype(f32)
            o_ref[...] = part.astype(out_dtype)
        else:
            acc_ref = refs[-1]
            k = pl.program_id(2)

            @pl.when(k == 0)
            def _():
                acc_ref[...] = part

            @pl.when(k > 0)
            def _():
                acc_ref[...] += part

            @pl.when(k == nk - 1)
            def _():
                r = acc_ref[...]
                if has_add:
                    r = r + add_ref[...].astype(f32)
                o_ref[...] = r.astype(out_dtype)

    if mode == 'nn':
        a_spec = pl.BlockSpec((tm, tk), lambda i, j, k: (i, k))
        b_spec = pl.BlockSpec((tk, tn), lambda i, j, k: (k, j))
    elif mode == 'nt':
        a_spec = pl.BlockSpec((tm, tk), lambda i, j, k: (i, k))
        b_spec = pl.BlockSpec((tn, tk), lambda i, j, k: (j, k))
    else:
        a_spec = pl.BlockSpec((tk, tm), lambda i, j, k: (k, i))
        b_spec = pl.BlockSpec((tk, tn), lambda i, j, k: (k, j))
    in_specs = [a_spec, b_spec]
    args = [a, b]
    if has_add:
        in_specs.append(pl.BlockSpec((tm, tn), lambda i, j, k: (i, j)))
        args.append(add)
    return pl.pallas_call(
        body, name=name, grid=(M // tm, N // tn, nk), in_specs=in_specs,
        out_specs=pl.BlockSpec((tm, tn), lambda i, j, k: (i, j)),
        out_shape=jax.ShapeDtypeStruct((M, N), out_dtype),
        scratch_shapes=[pltpu.VMEM((tm, tn), f32)] if nk > 1 else [],
        compiler_params=_cparams(('parallel', 'parallel', 'arbitrary'), big=True),
    )(*args)


def _full_spec(p):
    nd = p.ndim
    return pl.BlockSpec(p.shape, lambda i, _nd=nd: (0,) * _nd)


def _tile_fwd(f, tiled, params, outs, tm, name):
    S = tiled[0].shape[0]
    nt, npar = len(tiled), len(params)

    def body(*refs):
        vals = [r[...].astype(f32) for r in refs[:nt + npar]]
        res = f(*vals)
        for r, o in zip(res, refs[nt + npar:]):
            o[...] = r.astype(o.dtype)

    return pl.pallas_call(
        body, name=name, grid=(S // tm,),
        in_specs=[pl.BlockSpec((tm, t.shape[1]), lambda i: (i, 0)) for t in tiled] + [_full_spec(p) for p in params],
        out_specs=[pl.BlockSpec((tm, c), lambda i: (i, 0)) for c, _ in outs],
        out_shape=[jax.ShapeDtypeStruct((S, c), dt) for c, dt in outs],
        compiler_params=_cparams(('parallel',), big=True),
    )(*tiled, *params)


def _tile_bwd(f, tiled, params, cts, diff_t, diff_p, tm, name, dt_dtypes=None):
    S = tiled[0].shape[0]
    nt, npar, nc = len(tiled), len(params), len(cts)
    it = [i for i in range(nt) if diff_t[i]]
    ip = [i for i in range(npar) if diff_p[i]]
    if dt_dtypes is None:
        dt_dtypes = [f32] * len(it)

    def body(*refs):
        vals = [r[...].astype(f32) for r in refs[:nt + npar]]
        ct_vals = tuple(r[...].astype(f32) for r in refs[nt + npar:nt + npar + nc])
        out_refs = refs[nt + npar + nc:]

        def g(*dv):
            full = list(vals)
            for k, i in enumerate(it):
                full[i] = dv[k]
            for k, i in enumerate(ip):
                full[nt + i] = dv[len(it) + k]
            return tuple(f(*full))

        _, vjp = jax.vjp(g, *[vals[i] for i in it], *[vals[nt + i] for i in ip])
        grads = vjp(ct_vals)
        for k in range(len(it)):
            out_refs[k][...] = grads[k].astype(out_refs[k].dtype)
        step = pl.program_id(0)
        for k in range(len(ip)):
            o = out_refs[len(it) + k]
            gk = grads[len(it) + k]

            @pl.when(step == 0)
            def _(o=o, gk=gk):
                o[...] = gk

            @pl.when(step > 0)
            def _(o=o, gk=gk):
                o[...] += gk

    out_specs = [pl.BlockSpec((tm, tiled[i].shape[1]), lambda i_: (i_, 0)) for i in it] + [_full_spec(params[i]) for i in ip]
    out_shape = [jax.ShapeDtypeStruct(tiled[i].shape, dt_dtypes[k]) for k, i in enumerate(it)] + \
                [jax.ShapeDtypeStruct(params[i].shape, f32) for i in ip]
    res = pl.pallas_call(
        body, name=name, grid=(S // tm,),
        in_specs=[pl.BlockSpec((tm, t.shape[1]), lambda i: (i, 0)) for t in tiled] + [_full_spec(p) for p in params] +
                 [pl.BlockSpec((tm, c.shape[1]), lambda i: (i, 0)) for c in cts],
        out_specs=out_specs, out_shape=out_shape,
        compiler_params=_cparams(('arbitrary',), big=True),
    )(*tiled, *params, *cts)
    return list(res[:len(it)]), list(res[len(it):])


def _rms(x, g):
    return x * lax.rsqrt(jnp.mean(x * x, axis=-1, keepdims=True) + EPS) * g


def _f_rms(x, g):
    return (_rms(x, g),)


def _f_swiglu(uv):
    h = uv.shape[1] // 2
    u, v = uv[:, :h], uv[:, h:]
    return (u * jax.nn.sigmoid(u) * v,)


def _loss_head(y, target):
    S, D = y.shape
    tm = _pick(S, (256, 128))

    def body(y_ref, t_ref, part_ref, dy_ref):
        e = y_ref[...] - t_ref[...]
        dy_ref[...] = e * (1.0 / D)
        s = 0.5 * jnp.sum(jnp.sum(e * e, axis=1, keepdims=True), axis=0, keepdims=True) * (1.0 / D)
        r = lax.broadcasted_iota(jnp.int32, (8, LANES), 0)
        c = lax.broadcasted_iota(jnp.int32, (8, LANES), 1)
        part_ref[0] = jnp.where((r == 0) & (c == 0), s, 0.0)

    return pl.pallas_call(
        body, name='loss_head', grid=(S // tm,),
        in_specs=[pl.BlockSpec((tm, D), lambda i: (i, 0))] * 2,
        out_specs=[pl.BlockSpec((1, 8, LANES), lambda i: (i, 0, 0)), pl.BlockSpec((tm, D), lambda i: (i, 0))],
        out_shape=[jax.ShapeDtypeStruct((S // tm, 8, LANES), f32), jax.ShapeDtypeStruct((S, D), f32)],
        compiler_params=_cparams(('parallel',)),
    )(y, target)


def _pack_rows_of(shape):
    rows = -(-math.prod(shape) // PACK_COLS)
    return -(-rows // 8) * 8


def _pack(arrs):
    parts = []
    for a in arrs:
        rows = _pack_rows_of(a.shape)
        flat = a.astype(f32).reshape(-1)
        parts.append(jnp.pad(flat, (0, rows * PACK_COLS - flat.shape[0])).reshape(rows, PACK_COLS))
    return jnp.concatenate(parts, axis=0)


def _unpack(pack, shapes):
    out, row = [], 0
    for s in shapes:
        rows = _pack_rows_of(s)
        out.append(pack[row:row + rows].reshape(-1)[:math.prod(s)].reshape(s))
        row += rows
    return out


ANY = pl.BlockSpec(memory_space=pl.ANY)


def _place():
    return lax.axis_index('x'), lax.axis_index('y'), lax.axis_index('c')


def _where():
    return jnp.stack([lax.axis_index('c'), 2 * lax.axis_index('x') + lax.axis_index('y')]).astype(jnp.int32)


def _remote(src, dst, send_sems, recv_sems, k, to):
    return pltpu.make_async_remote_copy(src_ref=src, dst_ref=dst, send_sem=send_sems.at[k], recv_sem=recv_sems.at[k],
                                        device_id=to, device_id_type=MESH)


def _gather_tensors(ws):
    n = len(ws)

    def body(*refs):
        w_refs, g_refs = refs[:n], refs[n:2 * n]
        send_sems, recv_sems = refs[2 * n:]
        x, y, c = _place()
        me, sib = (x, y, c), (x, y, 1 - c)
        chips = [(1 - x, y), (x, 1 - y), (1 - x, 1 - y)]
        first = [_remote(w_refs[t].at[c], g_refs[t].at[2 * x + y, c], send_sems, recv_sems, 6 * t + j, (px, py, c))
                 for j, (px, py) in enumerate(chips) for t in range(n)]
        for cp in first:
            cp.start()
        passed = []
        for j, (px, py) in enumerate(chips):
            for t in range(n):
                here = g_refs[t].at[2 * px + py, c]
                _remote(here, here, send_sems, recv_sems, 6 * t + j, me).wait_recv()
                cp = _remote(here, here, send_sems, recv_sems, 6 * t + 3 + j, sib)
                cp.start()
                passed.append(cp)
        for j, (px, py) in enumerate(chips):
            for t in range(n):
                there = g_refs[t].at[2 * px + py, 1 - c]
                _remote(there, there, send_sems, recv_sems, 6 * t + 3 + j, me).wait_recv()
        for cp in first + passed:
            cp.wait_send()

    own = 2 * lax.axis_index('x') + lax.axis_index('y')
    res = pl.pallas_call(
        body, name='gather_weights', in_specs=[ANY] * n, out_specs=[ANY] * n,
        out_shape=[jax.ShapeDtypeStruct((N_SHARDS,) + w.shape, w.dtype) for w in ws],
        scratch_shapes=[pltpu.SemaphoreType.DMA((6 * n,)), pltpu.SemaphoreType.DMA((6 * n,))],
    )(*ws)
    return [lax.dynamic_update_slice(g, w[None], (own, 0, 0, 0)) for g, w in zip(res, ws)]


def _swap_with_sibling(gs, small):
    n = len(gs)

    def body(*refs):
        g_refs, s_ref = refs[:n], refs[n]
        r_refs, rs_ref = refs[n + 1:2 * n + 1], refs[2 * n + 1]
        send_sems, recv_sems = refs[2 * n + 2:]
        x, y, c = _place()
        sib = (x, y, 1 - c)
        cps = [_remote(g_refs[t].at[:, 1 - c], r_refs[t], send_sems, recv_sems, t, sib) for t in range(n)]
        cps.append(_remote(s_ref, rs_ref, send_sems, recv_sems, n, sib))
        for cp in cps:
            cp.start()
        for cp in cps:
            cp.wait()

    res = pl.pallas_call(
        body, name='swap_with_sibling', in_specs=[ANY] * (n + 1), out_specs=[ANY] * (n + 1),
        out_shape=[jax.ShapeDtypeStruct((N_SHARDS,) + g.shape[2:], g.dtype) for g in gs] +
                  [jax.ShapeDtypeStruct(small.shape, small.dtype)],
        scratch_shapes=[pltpu.SemaphoreType.DMA((n + 1,)), pltpu.SemaphoreType.DMA((n + 1,))],
    )(*gs, small)
    return list(res[:n]), res[n]


def _exchange_between_chips(cs, small):
    n = len(cs)

    def body(*refs):
        c_refs, s_ref = refs[:n], refs[n]
        r_refs, rs_ref = refs[n + 1:2 * n + 1], refs[2 * n + 1]
        send_sems, recv_sems = refs[2 * n + 2:]
        x, y, c = _place()
        chips = [(1 - x, y), (x, 1 - y), (1 - x, 1 - y)]
        cps = []
        for j, (px, py) in enumerate(chips):
            for t in range(n):
                cps.append(_remote(c_refs[t].at[2 * px + py], r_refs[t].at[j], send_sems, recv_sems, 3 * t + j, (px, py, c)))
            cps.append(_remote(s_ref, rs_ref.at[j], send_sems, recv_sems, 3 * n + j, (px, py, c)))
        for cp in cps:
            cp.start()
        for cp in cps:
            cp.wait()

    res = pl.pallas_call(
        body, name='exchange_between_chips', in_specs=[ANY] * (n + 1), out_specs=[ANY] * (n + 1),
        out_shape=[jax.ShapeDtypeStruct((3,) + c.shape[1:], c.dtype) for c in cs] +
                  [jax.ShapeDtypeStruct((3,) + small.shape, small.dtype)],
        scratch_shapes=[pltpu.SemaphoreType.DMA((3 * n + 3,)), pltpu.SemaphoreType.DMA((3 * n + 3,))],
    )(*cs, small)
    return list(res[:n]), res[n]


def _join_with_sibling(ts):
    n = len(ts)

    def body(*refs):
        t_refs, o_refs = refs[:n], refs[n:2 * n]
        send_sems, recv_sems = refs[2 * n:]
        x, y, c = _place()
        cps = [_remote(t_refs[t], o_refs[t], send_sems, recv_sems, t, (x, y, 1 - c)) for t in range(n)]
        for cp in cps:
            cp.start()
        for cp in cps:
            cp.wait()

    theirs = pl.pallas_call(
        body, name='join_with_sibling', in_specs=[ANY] * n, out_specs=[ANY] * n,
        out_shape=[jax.ShapeDtypeStruct(t.shape, t.dtype) for t in ts],
        scratch_shapes=[pltpu.SemaphoreType.DMA((n,)), pltpu.SemaphoreType.DMA((n,))],
    )(*ts)
    south = lax.axis_index('c') == 0
    return [jnp.stack([jnp.where(south, mine, other), jnp.where(south, other, mine)]) for mine, other in zip(ts, theirs)]


def _row_tile(a):
    return _pick(a, (512, 256, 128, 64, 32, 16, 8))


def _chip_sum_of(g, r, name):
    _, _, a, b = g.shape
    tr = _row_tile(a)

    def body(w_ref, g_ref, r_ref, o_ref):
        o_ref[...] = (g_ref[0].astype(f32) + r_ref[...].astype(f32)).astype(o_ref.dtype)

    return pl.pallas_call(
        body, name=name,
        grid_spec=pltpu.PrefetchScalarGridSpec(
            num_scalar_prefetch=1, grid=(N_SHARDS, a // tr),
            in_specs=[pl.BlockSpec((1, 1, tr, b), lambda s, i, w: (s, w[0], i, 0)),
                      pl.BlockSpec((1, tr, b), lambda s, i, w: (s, i, 0))],
            out_specs=pl.BlockSpec((1, tr, b), lambda s, i, w: (s, i, 0))),
        out_shape=jax.ShapeDtypeStruct((N_SHARDS, a, b), bf16),
        compiler_params=_cparams(('parallel', 'parallel')),
    )(_where(), g, r)


def _shard_total_of(g, r, rc, name):
    _, _, a, b = g.shape
    tr = _row_tile(a)

    def body(w_ref, g_ref, r_ref, rc_ref, o_ref):
        t = g_ref[0, 0].astype(f32) + r_ref[0].astype(f32)
        t = t + rc_ref[0].astype(f32)
        t = t + rc_ref[1].astype(f32)
        t = t + rc_ref[2].astype(f32)
        o_ref[...] = t

    return pl.pallas_call(
        body, name=name,
        grid_spec=pltpu.PrefetchScalarGridSpec(
            num_scalar_prefetch=1, grid=(a // tr,),
            in_specs=[pl.BlockSpec((1, 1, tr, b), lambda i, w: (w[1], w[0], i, 0)),
                      pl.BlockSpec((1, tr, b), lambda i, w: (w[1], i, 0)),
                      pl.BlockSpec((3, tr, b), lambda i, w: (0, i, 0))],
            out_specs=pl.BlockSpec((tr, b), lambda i, w: (i, 0))),
        out_shape=jax.ShapeDtypeStruct((a, b), f32),
        compiler_params=_cparams(('parallel',)),
    )(_where(), g, r, rc)


def _by_shard(name, t):
    r, c = t.shape
    if BIG[name] == 2:
        return t.reshape(r, N_SHARDS, c // N_SHARDS).transpose(1, 0, 2)
    return t.reshape(N_SHARDS, r // N_SHARDS, c)


def _from_shards(name, g):
    s, layers, a, b = g.shape
    if BIG[name] == 2:
        return [g[:, l].transpose(1, 0, 2).reshape(a, s * b) for l in range(layers)]
    return [g[:, l].reshape(s * a, b) for l in range(layers)]


def _adam_math(w, g, m, v):
    m = ADAM_B1 * m + (1.0 - ADAM_B1) * g
    v = ADAM_B2 * v + (1.0 - ADAM_B2) * (g * g)
    m_hat = m / (1.0 - ADAM_B1 ** ADAM_STEP)
    v_hat = v / (1.0 - ADAM_B2 ** ADAM_STEP)
    delta = -ADAM_LR * (m_hat / (jnp.sqrt(v_hat) + ADAM_EPS) + ADAM_WD * w)
    return delta, m, v


def _small_update(own, sib, chips, w, m, v):
    def body(o_ref, s_ref, c_ref, w_ref, m_ref, v_ref, g_out, d_out, m_out, v_out):
        chip = o_ref[...] + s_ref[...]
        g = (chip + c_ref[0]) + (c_ref[1] + c_ref[2])
        d, mn, vn = _adam_math(w_ref[...], g, m_ref[...], v_ref[...])
        g_out[...] = g
        d_out[...] = d
        m_out[...] = mn
        v_out[...] = vn

    return pl.pallas_call(body, name='small_update', out_shape=[jax.ShapeDtypeStruct(own.shape, f32)] * 4)(
        own, sib, chips, w, m, v)


def _small_chip_sum(own, sib):
    def body(o_ref, s_ref, out):
        out[...] = o_ref[...] + s_ref[...]
    return pl.pallas_call(body, name='small_chip_sum', out_shape=jax.ShapeDtypeStruct(own.shape, f32))(own, sib)


def _adamw(w, g, m, v, name):
    shape = w.shape
    w2, g2, m2, v2 = [t.reshape(-1, shape[-1]) for t in (w, g, m, v)]
    rows, cols = w2.shape
    tr = _pick(rows, (256, 128, 64, 32, 16, 8))

    def body(w_ref, g_ref, m_ref, v_ref, d_out, m_out, v_out):
        d, mn, vn = _adam_math(w_ref[...], g_ref[...], m_ref[...], v_ref[...])
        d_out[...] = d
        m_out[...] = mn
        v_out[...] = vn

    spec = pl.BlockSpec((tr, cols), lambda i: (i, 0))
    res = pl.pallas_call(body, name=name, grid=(rows // tr,), in_specs=[spec] * 4, out_specs=[spec] * 3,
                         out_shape=[jax.ShapeDtypeStruct((rows, cols), f32)] * 3,
                         compiler_params=_cparams(('parallel',)))(w2, g2, m2, v2)
    return [r.reshape(shape) for r in res]


def _dg(a, b, ca, cb):
    return lax.dot_general(a.astype(bf16), b.astype(bf16), (((ca,), (cb,)), ((), ())), preferred_element_type=f32)


@jax.custom_vjp
def _bmm(a, b):
    return _dg(a, b, 1, 0)


_bmm.defvjp(lambda a, b: (_dg(a, b, 1, 0), (a, b)), lambda r, g: (_dg(g, r[1], 1, 1), _dg(r[0], g, 0, 0)))


@jax.custom_vjp
def _bmm_nt(a, b):
    return _dg(a, b, 1, 1)


_bmm_nt.defvjp(lambda a, b: (_dg(a, b, 1, 1), (a, b)), lambda r, g: (_dg(g, r[1], 1, 0), _dg(g, r[0], 0, 0)))


@jax.custom_vjp
def _bmm_tn(a, b):
    return _dg(a, b, 0, 0)


_bmm_tn.defvjp(lambda a, b: (_dg(a, b, 0, 0), (a, b)), lambda r, g: (_dg(r[1], g, 1, 1), _dg(r[0], g, 1, 0)))


def _hdot(a, b):
    return jnp.dot(a, b, precision=HI, preferred_element_type=f32)


def _hdot_nt(a, b):
    return lax.dot_general(a, b, (((1,), (1,)), ((), ())), precision=HI, preferred_element_type=f32)


def _hdot_tn(a, b):
    return lax.dot_general(a, b, (((0,), (0,)), ((), ())), precision=HI, preferred_element_type=f32)


def _head_mask(h, width=GROUP_W):
    lane = lax.broadcasted_iota(jnp.int32, (1, width), 1)
    return ((lane >= h * HEAD_DIM) & (lane < (h + 1) * HEAD_DIM)).astype(f32)


def _rope_perm():
    p = np.zeros((LANES, LANES), np.float32)
    half = MLA_ROPE // 2
    for i in range(half):
        p[MLA_NOPE + half + i, MLA_NOPE + i] = -1.0
        p[MLA_NOPE + i, MLA_NOPE + half + i] = 1.0
    return jnp.asarray(p)


def _rope_tables(S):
    half = MLA_ROPE // 2
    freqs = ROPE_THETA ** (-jnp.arange(half, dtype=f32) / half)
    ang = jnp.arange(S, dtype=f32)[:, None] * freqs[None, :]
    cos, sin = jnp.cos(ang), jnp.sin(ang)
    ones, zeros = jnp.ones((S, MLA_NOPE), f32), jnp.zeros((S, LANES - MLA_DQK), f32)
    c_tab = jnp.concatenate([ones, cos, cos, zeros], axis=1)
    s_tab = jnp.concatenate([jnp.zeros((S, MLA_NOPE), f32), sin, sin, zeros], axis=1)
    return c_tab, s_tab


def _f_mla_pre(c_q, c_kv, krope, c_tab, s_tab, q_norm, kv_norm, wq0, wq1, wq2, wq3, wk0, wk1, wk2, wk3, wv, gq, gk, perm):
    wq, wk = (wq0, wq1, wq2, wq3), (wk0, wk1, wk2, wk3)
    nq = _rms(c_q, q_norm)
    nkv = _rms(c_kv, kv_norm)

    def norm_rope(t, g):
        t = t * lax.rsqrt(jnp.sum(t * t, axis=-1, keepdims=True) * (1.0 / MLA_DQK) + EPS) * g
        return t * c_tab + _hdot(t, perm) * s_tab

    qs = [norm_rope(_bmm(nq, wq[h]), gq) * (MLA_DQK ** -0.5) for h in range(N_HEADS)]
    ks = [norm_rope(_bmm(nkv, wk[h]) + krope, gk) for h in range(N_HEADS)]
    return (*qs, *ks, _bmm(nkv, wv))


def _f_attn(qs, ks, v, q0):
    tq, S = qs[0].shape[0], ks[0].shape[0]
    qpos = q0 + lax.broadcasted_iota(jnp.int32, (tq, S), 0)
    kpos = lax.broadcasted_iota(jnp.int32, (tq, S), 1)
    keep = kpos <= qpos
    logits = [jnp.where(keep, _bmm_nt(qs[h], ks[h]), NEG_INF) for h in range(N_HEADS)]
    ps = [jnp.exp(lg - jnp.max(lg, axis=-1, keepdims=True)) for lg in logits]
    ps = [p / jnp.sum(p, axis=-1, keepdims=True) for p in ps]
    return sum(_bmm(p, v) * _head_mask(h) for h, p in enumerate(ps))


def _mla_attn_fwd(qs, ks, v, name):
    S = v.shape[0]
    tq = Q_BLOCK

    def body(*refs):
        q_vals = [r[...] for r in refs[:4]]
        k_vals = [r[...] for r in refs[4:8]]
        refs[9][...] = _f_attn(q_vals, k_vals, refs[8][...], pl.program_id(0) * tq)

    qspec = pl.BlockSpec((tq, LANES), lambda i: (i, 0))
    return pl.pallas_call(
        body, name=name, grid=(S // tq,),
        in_specs=[qspec] * 4 + [_full_spec(k) for k in ks] + [_full_spec(v)],
        out_specs=pl.BlockSpec((tq, GROUP_W), lambda i: (i, 0)),
        out_shape=jax.ShapeDtypeStruct((S, GROUP_W), f32),
        compiler_params=_cparams(('parallel',), big=True),
    )(*qs, *ks, v)


def _mla_attn_bwd(qs, ks, v, do, name):
    S = v.shape[0]
    tq = Q_BLOCK

    def body(*refs):
        q_vals = [r[...].astype(f32) for r in refs[:4]]
        k_vals = [r[...].astype(f32) for r in refs[4:8]]
        v_val = refs[8][...].astype(f32)
        q0 = pl.program_id(0) * tq
        _, vjp = jax.vjp(lambda a, b, c: _f_attn(a, b, c, q0), q_vals, k_vals, v_val)
        dqs, dks, dv = vjp(refs[9][...])
        outs = refs[10:]
        for h in range(N_HEADS):
            outs[h][...] = dqs[h]
        first = pl.program_id(0) == 0
        for o, g in zip(outs[4:], (*dks, dv)):
            @pl.when(first)
            def _(o=o, g=g):
                o[...] = g

            @pl.when(jnp.logical_not(first))
            def _(o=o, g=g):
                o[...] += g

    qspec = pl.BlockSpec((tq, LANES), lambda i: (i, 0))
    res = pl.pallas_call(
        body, name=name, grid=(S // tq,),
        in_specs=[qspec] * 4 + [_full_spec(k) for k in ks] + [_full_spec(v), pl.BlockSpec((tq, GROUP_W), lambda i: (i, 0))],
        out_specs=[qspec] * 4 + [_full_spec(k) for k in ks] + [_full_spec(v)],
        out_shape=[jax.ShapeDtypeStruct((S, LANES), f32)] * 8 + [jax.ShapeDtypeStruct((S, GROUP_W), f32)],
        compiler_params=_cparams(('arbitrary',), big=True),
    )(*qs, *ks, v, do)
    return res[:4], res[4:8], res[8]


def _mla_params(mp):
    pad = LANES - MLA_DQK
    wq = jnp.pad(mp['mla_w_uq'].reshape(GROUP_W, N_HEADS, MLA_DQK).transpose(1, 0, 2), ((0, 0), (0, 0), (0, pad)))
    wkv = mp['mla_w_ukv'].reshape(LANES, N_HEADS, MLA_NOPE + HEAD_DIM)
    wk = jnp.pad(wkv[:, :, :MLA_NOPE].transpose(1, 0, 2), ((0, 0), (0, 0), (0, LANES - MLA_NOPE)))
    wv = wkv[:, :, MLA_NOPE:].reshape(LANES, GROUP_W)
    gq = jnp.pad(mp['mla_qk_q'], (0, pad))[None]
    gk = jnp.pad(mp['mla_qk_k'], (0, pad))[None]
    return [mp['mla_q_norm'][None], mp['mla_kv_norm'][None], *[wq[h] for h in range(N_HEADS)],
            *[wk[h] for h in range(N_HEADS)], wv, gq, gk, _rope_perm()]


def _mla_fwd(c_q, c_kv, k_rope, mp, l):
    S = c_q.shape[0]
    tm = _pick(S, (256, 128))
    krope = jnp.pad(k_rope, ((0, 0), (MLA_NOPE, LANES - MLA_DQK)))
    c_tab, s_tab = _rope_tables(S)
    tiled = [c_q, c_kv, krope, c_tab, s_tab]
    params = _mla_params(mp)
    res = _tile_fwd(_f_mla_pre, tiled, params, [(LANES, bf16)] * 8 + [(GROUP_W, bf16)], tm, f'mla_pre_fwd_{l}')
    qs, ks, v = res[:4], res[4:8], res[8]
    y = _mla_attn_fwd(qs, ks, v, f'mla_attn_fwd_{l}')
    return y, (tiled, params, qs, ks, v)


def _mla_bwd(dy, saved, l):
    tiled, params, qs, ks, v = saved
    S = dy.shape[0]
    tm = _pick(S, (256, 128))
    dqs, dks, dv = _mla_attn_bwd(qs, ks, v, dy, f'mla_attn_bwd_{l}')
    (dc_q, dc_kv, dkrope), dpar = _tile_bwd(_f_mla_pre, tiled, params, [*dqs, *dks, dv], [True, True, True, False, False],
                                            [True] * 13 + [False], tm, f'mla_pre_bwd_{l}')
    dqn, dkvn = dpar[0], dpar[1]
    dwq, dwk = jnp.stack(dpar[2:6]), jnp.stack(dpar[6:10])
    dwv, dgq, dgk = dpar[10:13]
    dw_uq = dwq[:, :, :MLA_DQK].transpose(1, 0, 2).reshape(GROUP_W, N_HEADS * MLA_DQK)
    dw_ukv = jnp.concatenate([dwk[:, :, :MLA_NOPE].transpose(1, 0, 2), dwv.reshape(LANES, N_HEADS, HEAD_DIM)],
                             axis=2).reshape(LANES, N_HEADS * (MLA_NOPE + HEAD_DIM))
    grads = {'mla_q_norm': dqn[0], 'mla_kv_norm': dkvn[0], 'mla_w_uq': dw_uq, 'mla_w_ukv': dw_ukv,
             'mla_qk_q': dgq[0, :MLA_DQK], 'mla_qk_k': dgk[0, :MLA_DQK]}
    return dc_q, dc_kv, dkrope[:, MLA_NOPE:MLA_DQK], grads


SPAN = 128


def _head_mean_matrix():
    h = np.arange(GROUP_W) // HEAD_DIM
    return jnp.asarray((h[:, None] == h[None, :]).astype(np.float32) / HEAD_DIM)


def _f_dil_pre(q, k, gq, gk, hm):
    qn = q * lax.rsqrt(_hdot(q * q, hm) + EPS) * gq * (HEAD_DIM ** -0.5)
    kn = k * lax.rsqrt(_hdot(k * k, hm) + EPS) * gk
    return qn, kn


def _f_dil_branch(qb, kp, kc, vp, vc, b0, b1, b2, b3, first):
    kcat = jnp.concatenate([kp, kc], axis=0)
    vcat = jnp.concatenate([vp, vc], axis=0)
    qi = lax.broadcasted_iota(jnp.int32, (SPAN, 2 * SPAN), 0) + SPAN
    kj = lax.broadcasted_iota(jnp.int32, (SPAN, 2 * SPAN), 1)
    delta = qi - kj
    valid = (delta >= 0) & (delta <= SPAN) & jnp.logical_not(first & (kj < SPAN))
    masks = [_head_mask(h) for h in range(N_HEADS)]
    raw = [_bmm_nt(qb * hm, kcat) for hm in masks]
    logits = [jnp.where(valid, r + bias, NEG_INF) for r, bias in zip(raw, (b0, b1, b2, b3))]
    ms = [jnp.max(lg, axis=-1, keepdims=True) for lg in logits]
    ps = [jnp.exp(lg - m) for lg, m in zip(logits, ms)]
    pvs = [_bmm(p, vcat) for p in ps]
    o = sum(pv * hm for pv, hm in zip(pvs, masks))
    m_full = sum(m * hm for m, hm in zip(ms, masks))
    l_full = sum(jnp.sum(p, axis=-1, keepdims=True) * hm for p, hm in zip(ps, masks))
    return o, m_full, l_full


def _dil_branch_specs(d, nb):
    cur = pl.BlockSpec((1, SPAN, GROUP_W), lambda r, n: (r, n, 0))
    prev = pl.BlockSpec((1, SPAN, GROUP_W), lambda r, n: (r, jnp.maximum(n - 1, 0), 0))
    bias = pl.BlockSpec((1, SPAN, 2 * SPAN), lambda r, n: (0, 0, 0))
    return cur, prev, bias


def _head_table_specs():
    return [pl.BlockSpec((1, SPAN, 2 * SPAN), lambda r, n, h=h: (h, 0, 0)) for h in range(N_HEADS)]


def _dil_branch_fwd(q, k, v, table, name):
    d, L, _ = q.shape
    nb = L // SPAN
    cur, prev, bias = _dil_branch_specs(d, nb)

    def body(q_ref, kp_ref, kc_ref, vp_ref, vc_ref, b0, b1, b2, b3, o_ref, m_ref, l_ref):
        o, m, l = _f_dil_branch(q_ref[0], kp_ref[0], kc_ref[0], vp_ref[0], vc_ref[0], b0[0], b1[0], b2[0], b3[0],
                                pl.program_id(1) == 0)
        o_ref[0] = o
        m_ref[0] = m
        l_ref[0] = l

    return pl.pallas_call(
        body, name=name, grid=(d, nb), in_specs=[cur, prev, cur, prev, cur] + _head_table_specs(),
        out_specs=[cur] * 3, out_shape=[jax.ShapeDtypeStruct(q.shape, f32)] * 3,
        compiler_params=_cparams(('parallel', 'parallel')),
    )(q, k, k, v, v, *[table] * N_HEADS)


def _dil_branch_bwd(q, k, v, table, do, dm, dl, name):
    d, L, _ = q.shape
    nb = L // SPAN
    cur, prev, bias = _dil_branch_specs(d, nb)
    whole = pl.BlockSpec((1, L, GROUP_W), lambda r, n: (r, 0, 0))

    def body(q_ref, kp_ref, kc_ref, vp_ref, vc_ref, b0, b1, b2, b3, do_ref, dm_ref, dl_ref,
             dq_ref, dk_ref, dv_ref, db0, db1, db2, db3):
        r, n = pl.program_id(0), pl.program_id(1)
        first = n == 0
        _, vjp = jax.vjp(lambda *a: _f_dil_branch(*a, first), q_ref[0], kp_ref[0], kc_ref[0], vp_ref[0], vc_ref[0],
                         b0[0], b1[0], b2[0], b3[0])
        dq, dkp, dkc, dvp, dvc, g0, g1, g2, g3 = vjp((do_ref[0], dm_ref[0], dl_ref[0]))
        dq_ref[0] = dq

        @pl.when(first)
        def _():
            dk_ref[...] = jnp.zeros_like(dk_ref)
            dv_ref[...] = jnp.zeros_like(dv_ref)

        rows = pl.ds(pl.multiple_of(n * SPAN, SPAN), SPAN)
        dk_ref[0, rows, :] += dkc
        dv_ref[0, rows, :] += dvc

        @pl.when(n > 0)
        def _():
            before = pl.ds(pl.multiple_of((n - 1) * SPAN, SPAN), SPAN)
            dk_ref[0, before, :] += dkp
            dv_ref[0, before, :] += dvp

        start = first & (r == 0)
        for o, g in zip((db0, db1, db2, db3), (g0, g1, g2, g3)):
            @pl.when(start)
            def _(o=o, g=g):
                o[0] = g

            @pl.when(jnp.logical_not(start))
            def _(o=o, g=g):
                o[0] += g

    res = pl.pallas_call(
        body, name=name, grid=(d, nb), in_specs=[cur, prev, cur, prev, cur] + _head_table_specs() + [cur] * 3,
        out_specs=[cur, whole, whole] + [bias] * 4,
        out_shape=[jax.ShapeDtypeStruct(q.shape, f32)] * 3 + [jax.ShapeDtypeStruct((1, SPAN, 2 * SPAN), f32)] * 4,
        compiler_params=_cparams(('arbitrary', 'arbitrary')),
    )(q, k, k, v, v, *[table] * N_HEADS, do, dm, dl)
    return res[0], res[1], res[2], res[3:]


def _f_dil_merge(o1, m1, l1, o2, m2, l2, o3, m3, l3):
    mx = jnp.maximum(jnp.maximum(m1, m2), m3)
    w1, w2, w3 = jnp.exp(m1 - mx), jnp.exp(m2 - mx), jnp.exp(m3 - mx)
    return ((w1 * o1 + w2 * o2 + w3 * o3) / (w1 * l1 + w2 * l2 + w3 * l3),)


def _bias_onehot(dilation):
    qi = jnp.arange(SPAN, dtype=jnp.int32)[:, None] + SPAN
    kj = jnp.arange(2 * SPAN, dtype=jnp.int32)[None, :]
    bucket = _t5_bucket(jnp.clip(qi - kj, 0, SPAN) * dilation).reshape(-1)
    return (bucket[None, :] == jnp.arange(T5_BUCKETS, dtype=jnp.int32)[:, None]).astype(f32)


def _bias_tables(t5_t, onehot, name):
    N = onehot.shape[1]
    tn = _pick(N, (4096, 2048, 1024))

    def body(t_ref, oh_ref, o_ref):
        o_ref[...] = _hdot(t_ref[...], oh_ref[...])

    return pl.pallas_call(
        body, name=name, grid=(N // tn,),
        in_specs=[pl.BlockSpec((8, T5_BUCKETS), lambda i: (0, 0)), pl.BlockSpec((T5_BUCKETS, tn), lambda i: (0, i))],
        out_specs=pl.BlockSpec((8, tn), lambda i: (0, i)), out_shape=jax.ShapeDtypeStruct((8, N), f32),
        compiler_params=_cparams(('parallel',)),
    )(t5_t, onehot)


def _bias_tables_bwd(d_tab, onehot, name):
    N = onehot.shape[1]
    tn = _pick(N, (4096, 2048, 1024))

    def body(g_ref, oh_ref, o_ref):
        part = _hdot_nt(g_ref[...], oh_ref[...])

        @pl.when(pl.program_id(0) == 0)
        def _():
            o_ref[...] = part

        @pl.when(pl.program_id(0) > 0)
        def _():
            o_ref[...] += part

    return pl.pallas_call(
        body, name=name, grid=(N // tn,),
        in_specs=[pl.BlockSpec((8, tn), lambda i: (0, i)), pl.BlockSpec((T5_BUCKETS, tn), lambda i: (0, i))],
        out_specs=pl.BlockSpec((8, T5_BUCKETS), lambda i: (0, 0)), out_shape=jax.ShapeDtypeStruct((8, T5_BUCKETS), f32),
        compiler_params=_cparams(('arbitrary',)),
    )(d_tab, onehot)


def _by_residue(t, d):
    S, C = t.shape
    return t.reshape(S // d, d, C).transpose(1, 0, 2)


def _from_residue(t):
    d, L, C = t.shape
    return t.transpose(1, 0, 2).reshape(d * L, C)


def _dil_fwd(qkv, mp, l):
    S = qkv.shape[0]
    tm = _pick(S, (256, 128))
    q, k, v = qkv[:, :GROUP_W], qkv[:, GROUP_W:2 * GROUP_W], qkv[:, 2 * GROUP_W:]
    pre_params = [jnp.tile(mp['dil_q_norm'], N_HEADS)[None], jnp.tile(mp['dil_k_norm'], N_HEADS)[None], _head_mean_matrix()]
    qn, kn = _tile_fwd(_f_dil_pre, [q, k], pre_params, [(GROUP_W, f32)] * 2, tm, f'dil_pre_fwd_{l}')
    t5_t = jnp.pad(mp['t5_bias'].T, ((0, 8 - N_HEADS), (0, 0)))
    branches, outs = [], []
    for bi, (_, d) in enumerate(DIL_PAIRS):
        onehot = _bias_onehot(d)
        tab = _bias_tables(t5_t, onehot, f'dil_bias_fwd_{l}_{bi}').reshape(8, SPAN, 2 * SPAN)
        qd, kd, vd = _by_residue(qn, d), _by_residue(kn, d), _by_residue(v, d)
        o, m, lsum = _dil_branch_fwd(qd, kd, vd, tab, f'dil_branch_fwd_{l}_{bi}')
        branches.append((qd, kd, vd, tab, onehot))
        outs += [_from_residue(o), _from_residue(m), _from_residue(lsum)]
    (y,) = _tile_fwd(_f_dil_merge, outs, [], [(GROUP_W, f32)], tm, f'dil_merge_fwd_{l}')
    return y, (q, k, pre_params, branches, outs)


def _dil_bwd(dy, saved, l):
    q, k, pre_params, branches, outs = saved
    S = dy.shape[0]
    tm = _pick(S, (256, 128))
    douts, _ = _tile_bwd(_f_dil_merge, outs, [], [dy], [True] * 9, [], tm, f'dil_merge_bwd_{l}')
    dqn = dkn = dv = None
    dt5_t = None
    for bi, (_, d) in enumerate(DIL_PAIRS):
        qd, kd, vd, tab, onehot = branches[bi]
        do, dm, dl = [_by_residue(t, d) for t in douts[3 * bi:3 * bi + 3]]
        dq_b, dk_b, dv_b, dbias = _dil_branch_bwd(qd, kd, vd, tab, do, dm, dl, f'dil_branch_bwd_{l}_{bi}')
        d_tab = jnp.concatenate([*dbias, jnp.zeros((8 - N_HEADS, SPAN, 2 * SPAN), f32)], axis=0).reshape(8, -1)
        g_t5 = _bias_tables_bwd(d_tab, onehot, f'dil_bias_bwd_{l}_{bi}')
        dq_b, dk_b, dv_b = _from_residue(dq_b), _from_residue(dk_b), _from_residue(dv_b)
        dqn = dq_b if dqn is None else dqn + dq_b
        dkn = dk_b if dkn is None else dkn + dk_b
        dv = dv_b if dv is None else dv + dv_b
        dt5_t = g_t5 if dt5_t is None else dt5_t + g_t5
    (dq, dk), (dgq, dgk) = _tile_bwd(_f_dil_pre, [q, k], pre_params, [dqn, dkn], [True, True], [True, True, False], tm,
                                     f'dil_pre_bwd_{l}')
    grads = {'dil_q_norm': dgq.reshape(N_HEADS, HEAD_DIM).sum(0), 'dil_k_norm': dgk.reshape(N_HEADS, HEAD_DIM).sum(0),
             't5_bias': dt5_t[:N_HEADS].T}
    return jnp.concatenate([dq, dk, dv], axis=1), grads


S5_LANES = S5_G * S5_P
SCAN_SEGMENTS = 8
SCAN_W = LANES


def _f_s5_prep(bre, bim, lr, li, logdt_col, expand):
    dt = jnp.sum(jnp.exp(logdt_col) * expand, axis=0, keepdims=True)
    mag = jnp.exp(lr * dt)
    ar, ai = mag * jnp.cos(li * dt), mag * jnp.sin(li * dt)
    den = lr * lr + li * li
    nr, ni = ar - 1.0, ai
    zr = (nr * lr + ni * li) / den
    zi = (ni * lr - nr * li) / den
    bb = jnp.concatenate([zr * bre - zi * bim, zr * bim + zi * bre], axis=1)
    a_rows = jnp.broadcast_to(jnp.concatenate([ar, ai], axis=1), bb.shape)
    return bb, a_rows


def _s5_scan(x, a_rows, name, reverse=False, h=None):
    S = x.shape[0]
    NL = x.shape[1] // 2
    T = S // SCAN_SEGMENTS
    nblk = NL // SCAN_W

    def body(*refs):
        if reverse:
            xr_ref, xi_ref, ar_ref, ai_ref, pr_ref, pi_ref, hr_ref, hi_ref, dar_ref, dai_ref = refs
        else:
            xr_ref, xi_ref, ar_ref, ai_ref, hr_ref, hi_ref = refs
        ar = ar_ref[...]
        ai = -ai_ref[...] if reverse else ai_ref[...]
        zero = jnp.zeros((SCAN_SEGMENTS, SCAN_W), f32)

        def tile(j):
            return pl.ds(j, SCAN_SEGMENTS, stride=T)

        def at(s):
            return T - 1 - s if reverse else s

        def local(s, c):
            hr, hi, pr, pi = c
            j = at(s)
            nhr = ar * hr - ai * hi + xr_ref[tile(j), :]
            nhi = ar * hi + ai * hr + xi_ref[tile(j), :]
            hr_ref[tile(j), :] = nhr
            hi_ref[tile(j), :] = nhi
            return nhr, nhi, ar * pr - ai * pi, ar * pi + ai * pr

        er, ei, pr, pi = lax.fori_loop(0, T, local, (zero, zero, zero + 1.0, zero), unroll=2)
        row = lax.broadcasted_iota(jnp.int32, (SCAN_SEGMENTS, SCAN_W), 0)
        cr, ci = zero, zero
        order = range(SCAN_SEGMENTS - 2, -1, -1) if reverse else range(1, SCAN_SEGMENTS)
        for k in order:
            src = k + 1 if reverse else k - 1
            tr = er + pr * cr - pi * ci
            ti = ei + pr * ci + pi * cr
            cr = jnp.where(row == k, jnp.sum(jnp.where(row == src, tr, 0.0), axis=0, keepdims=True), cr)
            ci = jnp.where(row == k, jnp.sum(jnp.where(row == src, ti, 0.0), axis=0, keepdims=True), ci)

        def fix_at(j, c, before):
            pr, pi, sr, si = c
            pr, pi = ar * pr - ai * pi, ar * pi + ai * pr
            hr = hr_ref[tile(j), :] + pr * cr - pi * ci
            hi = hi_ref[tile(j), :] + pr * ci + pi * cr
            hr_ref[tile(j), :] = hr
            hi_ref[tile(j), :] = hi
            if reverse:
                qr, qi = before
                sr = sr + hr * qr + hi * qi
                si = si + hi * qr - hr * qi
            return pr, pi, sr, si

        start = (zero + 1.0, zero, zero, zero)
        if reverse:
            def fix(s, c):
                j = T - 1 - s
                return fix_at(j, c, (pr_ref[tile(j - 1), :], pi_ref[tile(j - 1), :]))

            c = lax.fori_loop(0, T - 1, fix, start, unroll=2)
            last_r = jnp.where(row == 0, 0.0, pltpu.roll(pr_ref[tile(T - 1), :], 1, 0))
            last_i = jnp.where(row == 0, 0.0, pltpu.roll(pi_ref[tile(T - 1), :], 1, 0))
            _, _, sr, si = fix_at(0, c, (last_r, last_i))
            dar_ref[...] = sr
            dai_ref[...] = si
        else:
            lax.fori_loop(0, T, lambda s, c: fix_at(s, c, None), start, unroll=2)

    re = pl.BlockSpec((S, SCAN_W), lambda b: (0, b))
    im = pl.BlockSpec((S, SCAN_W), lambda b: (0, nblk + b))
    a_re = pl.BlockSpec((SCAN_SEGMENTS, SCAN_W), lambda b: (0, b))
    a_im = pl.BlockSpec((SCAN_SEGMENTS, SCAN_W), lambda b: (0, nblk + b))
    in_specs, args = [re, im, a_re, a_im], [x, x, a_rows, a_rows]
    out_specs = [re, re]
    out_shape = [jax.ShapeDtypeStruct((S, NL), f32)] * 2
    if reverse:
        in_specs += [re, re]
        args += [h[0], h[1]]
        out_specs += [a_re, a_re]
        out_shape += [jax.ShapeDtypeStruct((SCAN_SEGMENTS, NL), f32)] * 2
    return pl.pallas_call(body, name=name, grid=(nblk,), in_specs=in_specs, out_specs=out_specs, out_shape=out_shape,
                          compiler_params=_cparams(('parallel',), big=True))(*args)


def _f_s5_post(y, u, d, w_glu):
    z = _bmm(y + d * u, w_glu)
    return (z[:, :GROUP_W] * jax.nn.sigmoid(z[:, GROUP_W:]),)


def _block_diag(t):
    G, a, b = t.shape
    eye = jnp.eye(G, dtype=t.dtype)
    return (t[:, :, None, :] * eye[:, None, :, None]).reshape(G * a, G * b)


def _diag_blocks(m, a, b):
    G = m.shape[0] // a
    return jnp.moveaxis(jnp.diagonal(m.reshape(G, a, G, b), axis1=0, axis2=2), -1, 0)


def _s5_fwd(u, mp, l):
    S = u.shape[0]
    tm = _pick(S, (256, 128))
    bre = _block_diag(mp['s5_b_re'].transpose(0, 2, 1))
    bim = _block_diag(mp['s5_b_im'].transpose(0, 2, 1))
    expand = jnp.repeat(jnp.eye(S5_G, dtype=f32), S5_P, axis=1)
    prep_params = [mp['s5_lambda_re'].reshape(1, S5_LANES), mp['s5_lambda_im'].reshape(1, S5_LANES),
                   mp['s5_log_dt'].reshape(S5_G, 1), expand]
    bb, a_rows = _tile_fwd(_f_s5_prep, [bre, bim], prep_params, [(2 * S5_LANES, f32)] * 2, GROUP_W, f's5_prep_fwd_{l}')
    x = _mm(u, bb, 'nn', f's5_in_fwd_{l}')
    hr, hi = _s5_scan(x, a_rows, f's5_scan_fwd_{l}')
    c_re, c_im = _block_diag(mp['s5_c_re'].transpose(0, 2, 1)), -_block_diag(mp['s5_c_im'].transpose(0, 2, 1))
    y = _mm(hi, c_im, 'nn', f's5_out_im_fwd_{l}', add=_mm(hr, c_re, 'nn', f's5_out_re_fwd_{l}'))
    post_params = [mp['s5_d'][None], mp['s5_w_glu']]
    (out,) = _tile_fwd(_f_s5_post, [y, u], post_params, [(GROUP_W, f32)], tm, f's5_post_fwd_{l}')
    return out, (u, bre, bim, prep_params, bb, a_rows, hr, hi, c_re, c_im, y, post_params)


def _s5_bwd(dout, saved, l):
    u, bre, bim, prep_params, bb, a_rows, hr, hi, c_re, c_im, y, post_params = saved
    S = u.shape[0]
    tm = _pick(S, (256, 128))
    (dy, du1), (dd, dwglu) = _tile_bwd(_f_s5_post, [y, u], post_params, [dout], [True, True], [True, True], tm,
                                       f's5_post_bwd_{l}')
    ccat = jnp.concatenate([c_re, c_im], axis=0)
    dh = _mm(dy, ccat, 'nt', f's5_out_dx_{l}')
    dccat = jnp.concatenate([_mm(hr, dy, 'tn', f's5_out_re_dw_{l}'), _mm(hi, dy, 'tn', f's5_out_im_dw_{l}')], axis=0)
    lr_, li_, dar, dai = _s5_scan(dh, a_rows, f's5_scan_bwd_{l}', reverse=True, h=(hr, hi))
    du2 = _mm(li_, bb[:, S5_LANES:], 'nt', f's5_in_im_dx_{l}', add=_mm(lr_, bb[:, :S5_LANES], 'nt', f's5_in_re_dx_{l}'))
    dbb = jnp.concatenate([_mm(u, lr_, 'tn', f's5_in_re_dw_{l}'), _mm(u, li_, 'tn', f's5_in_im_dw_{l}')], axis=1)
    da_rows = jnp.pad(jnp.concatenate([dar, dai], axis=1), ((0, GROUP_W - SCAN_SEGMENTS), (0, 0)))
    (dbre, dbim), (dlr, dli, dlogdt) = _tile_bwd(_f_s5_prep, [bre, bim], prep_params, [dbb, da_rows], [True, True],
                                                 [True, True, True, False], GROUP_W, f's5_prep_bwd_{l}')
    grads = {
        's5_lambda_re': dlr.reshape(S5_G, S5_P), 's5_lambda_im': dli.reshape(S5_G, S5_P), 's5_log_dt': dlogdt[:, 0],
        's5_b_re': _diag_blocks(dbre, S5_CG, S5_P).transpose(0, 2, 1),
        's5_b_im': _diag_blocks(dbim, S5_CG, S5_P).transpose(0, 2, 1),
        's5_c_re': _diag_blocks(dccat[:S5_LANES], S5_P, S5_CG).transpose(0, 2, 1),
        's5_c_im': -_diag_blocks(dccat[S5_LANES:], S5_P, S5_CG).transpose(0, 2, 1),
        's5_d': dd[0], 's5_w_glu': dwglu}
    return du1 + du2, grads


DN_CONV = 4


def _head_sum_matrix():
    h = np.arange(GROUP_W) // HEAD_DIM
    return jnp.asarray((h[:, None] == h[None, :]).astype(np.float32))


def _f_dn_pre(x0, x1, x2, x3, ab, w0, w1, w2, w3, alog, dtb, ea, eb, hs):
    c = w0 * x0 + w1 * x1 + w2 * x2 + w3 * x3
    s = c * jax.nn.sigmoid(c)
    q, k, v = s[:, :GROUP_W], s[:, GROUP_W:2 * GROUP_W], s[:, 2 * GROUP_W:]
    q = q * lax.rsqrt(_hdot(q * q, hs) + EPS) * (HEAD_DIM ** -0.5)
    k = k * lax.rsqrt(_hdot(k * k, hs) + EPS)
    beta = jax.nn.sigmoid(_hdot(ab, eb))
    g = -jnp.exp(alog) * jax.nn.softplus(_hdot(ab, ea) + dtb)
    return q, k, v, g, beta


DN_CHUNKS_PER_STEP = 4


def _f_dn_chunks(q, k, v, g, beta):
    C = DN_CHUNK
    n_chunks = q.shape[0] // C
    r = lax.broadcasted_iota(jnp.int32, (C, C), 0)
    c = lax.broadcasted_iota(jnp.int32, (C, C), 1)
    causal, strict = r >= c, r > c
    eye = (r == c).astype(f32)
    tril = causal.astype(f32)
    ones = jnp.ones((C, GROUP_W), f32)
    masks = [_head_mask(h) for h in range(N_HEADS)]
    rows = [tuple(t[i * C:(i + 1) * C] for t in (q, k, v, g, beta)) for i in range(n_chunks)]
    gcs = [_hdot(tril, gi) for (_, _, _, gi, _) in rows]
    items = [(i, h) for i in range(n_chunks) for h in range(N_HEADS)]
    grows = [_hdot_nt(ones * (masks[h] * (1.0 / HEAD_DIM)), gcs[i]) for i, h in items]
    decs = []
    for (i, h), grow in zip(items, grows):
        gcol = jnp.sum(gcs[i] * masks[h], axis=1, keepdims=True) * (1.0 / HEAD_DIM)
        decs.append(jnp.exp(jnp.where(causal, gcol - grow, NEG_INF)))
    kbs = [ki * bi for (_, ki, _, _, bi) in rows]
    kks = [_bmm_nt(kbs[i] * masks[h], rows[i][1]) for i, h in items]
    qks = [_bmm_nt(rows[i][0] * masks[h], rows[i][1]) for i, h in items]
    lmats = [jnp.where(strict, kk * dec, 0.0) for kk, dec in zip(kks, decs)]
    a_qk = [jnp.where(causal, qk * dec, 0.0) for qk, dec in zip(qks, decs)]
    ts = [eye - lm for lm in lmats]
    ps = lmats
    for _ in range(5):
        ps = [_bmm(p, p) for p in ps]
        ts = [t + _bmm(t, p) for t, p in zip(ts, ps)]
    egs = [jnp.exp(gc) for gc in gcs]
    tw = [_bmm(t, kbs[i] * egs[i]) for (i, h), t in zip(items, ts)]
    tu = [_bmm(t, rows[i][2] * rows[i][4]) for (i, h), t in zip(items, ts)]
    outs = []
    for i in range(n_chunks):
        qi, ki, _, gi, _ = rows[i]
        glast = jnp.sum(gi, axis=0, keepdims=True)
        w = sum(tw[i * N_HEADS + h] * masks[h] for h in range(N_HEADS))
        u = sum(tu[i * N_HEADS + h] * masks[h] for h in range(N_HEADS))
        outs.append((w, u, qi * egs[i], ki * jnp.exp(glast - gcs[i]), *a_qk[i * N_HEADS:(i + 1) * N_HEADS],
                     jnp.broadcast_to(jnp.exp(glast), (C, GROUP_W))))
    return tuple(jnp.concatenate(parts, axis=0) for parts in zip(*outs))


def _f_dn_step(w, u, qd, kdec, a0, a1, a2, a3, dfull, state, bd):
    row0 = (lax.broadcasted_iota(jnp.int32, dfull.shape, 0) == 0).astype(f32)
    dvec = jnp.sum(dfull * row0, axis=0, keepdims=True)
    ws, qs = _bmm(w, state), _bmm(qd, state)
    vnew = u - ws
    avs = [_bmm(a, vnew) for a in (a0, a1, a2, a3)]
    kv = _bmm_tn(kdec, vnew)
    o = qs + sum(av * _head_mask(h) for h, av in enumerate(avs))
    return o, state * dvec + bd * kv


def _dn_scan_fwd(ins, name):
    S = ins[0].shape[0]
    N = S // DN_CHUNK
    bd = _head_sum_matrix()

    def body(*refs):
        o_ref, s_ref, state = refs[10], refs[11], refs[12]

        @pl.when(pl.program_id(0) == 0)
        def _():
            state[...] = jnp.zeros_like(state)

        s_in = state[...]
        s_ref[0] = s_in
        o, s_out = _f_dn_step(*[r[...] for r in refs[:9]], s_in, refs[9][...])
        o_ref[...] = o
        state[...] = s_out

    return pl.pallas_call(
        body, name=name, grid=(N,),
        in_specs=[pl.BlockSpec((DN_CHUNK, t.shape[1]), lambda n: (n, 0)) for t in ins] + [_full_spec(bd)],
        out_specs=[pl.BlockSpec((DN_CHUNK, GROUP_W), lambda n: (n, 0)), pl.BlockSpec((1, GROUP_W, GROUP_W), lambda n: (n, 0, 0))],
        out_shape=[jax.ShapeDtypeStruct((S, GROUP_W), f32), jax.ShapeDtypeStruct((N, GROUP_W, GROUP_W), f32)],
        scratch_shapes=[pltpu.VMEM((GROUP_W, GROUP_W), f32)],
        compiler_params=_cparams(('arbitrary',)),
    )(*ins, bd)


def _dn_scan_bwd(ins, states, do, name):
    S = ins[0].shape[0]
    N = S // DN_CHUNK
    bd = _head_sum_matrix()

    def body(*refs):
        s_ref, do_ref = refs[9], refs[10]
        bd_ref = refs[11]
        outs = refs[12:21]
        dstate = refs[21]

        @pl.when(pl.program_id(0) == 0)
        def _():
            dstate[...] = jnp.zeros_like(dstate)

        bd_val = bd_ref[...]
        _, vjp = jax.vjp(lambda *a: _f_dn_step(*a, bd_val), *[r[...] for r in refs[:9]], s_ref[0])
        grads = vjp((do_ref[...], dstate[...]))
        for o, g in zip(outs, grads[:9]):
            o[...] = g
        dstate[...] = grads[9]

    def rev(n):
        return (N - 1 - n, 0)

    res = pl.pallas_call(
        body, name=name, grid=(N,),
        in_specs=[pl.BlockSpec((DN_CHUNK, t.shape[1]), rev) for t in ins] +
                 [pl.BlockSpec((1, GROUP_W, GROUP_W), lambda n: (N - 1 - n, 0, 0)), pl.BlockSpec((DN_CHUNK, GROUP_W), rev),
                  _full_spec(bd)],
        out_specs=[pl.BlockSpec((DN_CHUNK, t.shape[1]), rev) for t in ins],
        out_shape=[jax.ShapeDtypeStruct(t.shape, f32) for t in ins],
        scratch_shapes=[pltpu.VMEM((GROUP_W, GROUP_W), f32)],
        compiler_params=_cparams(('arbitrary',)),
    )(*ins, states, do, bd)
    return list(res)


def _f_dn_post(o, gate, gain, hmean):
    return (o * lax.rsqrt(_hdot(o * o, hmean) + EPS) * gain * (gate * jax.nn.sigmoid(gate)),)


def _delay(t, j):
    return t if j == 0 else jnp.pad(t[:-j], ((j, 0), (0, 0)))


def _advance(t, j):
    return t if j == 0 else jnp.pad(t[j:], ((0, j), (0, 0)))


def _dn_fwd(qkv, a, b, gate, mp, l):
    S = qkv.shape[0]
    tm = _pick(S, (256, 128))
    xs = [_delay(qkv, DN_CONV - 1 - j) for j in range(DN_CONV)]
    ab = jnp.pad(jnp.concatenate([a, b], axis=1), ((0, 0), (0, LANES - 2 * N_HEADS)))
    sel = np.zeros((2, LANES, GROUP_W), np.float32)
    for h in range(N_HEADS):
        sel[0, h, h * HEAD_DIM:(h + 1) * HEAD_DIM] = 1.0
        sel[1, N_HEADS + h, h * HEAD_DIM:(h + 1) * HEAD_DIM] = 1.0
    pre_params = [*[mp['dn_conv'][j][None] for j in range(DN_CONV)], jnp.repeat(mp['dn_a_log'], HEAD_DIM)[None],
                  jnp.repeat(mp['dn_dt_bias'], HEAD_DIM)[None], jnp.asarray(sel[0]), jnp.asarray(sel[1]), _head_sum_matrix()]
    pre = _tile_fwd(_f_dn_pre, [*xs, ab], pre_params, [(GROUP_W, f32)] * 5, tm, f'dn_pre_fwd_{l}')
    chunk_outs = [(GROUP_W, f32)] * 4 + [(HEAD_DIM, f32)] * 4 + [(GROUP_W, f32)]
    parts = _tile_fwd(_f_dn_chunks, pre, [], chunk_outs, DN_CHUNK * DN_CHUNKS_PER_STEP, f'dn_chunk_fwd_{l}')
    o, states = _dn_scan_fwd(parts, f'dn_scan_fwd_{l}')
    post_params = [jnp.tile(mp['dn_o_norm'], N_HEADS)[None], _head_mean_matrix()]
    (y,) = _tile_fwd(_f_dn_post, [o, gate], post_params, [(GROUP_W, f32)], tm, f'dn_post_fwd_{l}')
    return y, (xs, ab, pre_params, pre, parts, states, o, gate, post_params)


def _dn_bwd(dy, saved, l):
    xs, ab, pre_params, pre, parts, states, o, gate, post_params = saved
    S = dy.shape[0]
    tm = _pick(S, (256, 128))
    (do, dgate), (dgain,) = _tile_bwd(_f_dn_post, [o, gate], post_params, [dy], [True, True], [True, False], tm,
                                      f'dn_post_bwd_{l}')
    dparts = _dn_scan_bwd(parts, states, do, f'dn_scan_bwd_{l}')
    dpre, _ = _tile_bwd(_f_dn_chunks, pre, [], dparts, [True] * 5, [], DN_CHUNK * DN_CHUNKS_PER_STEP, f'dn_chunk_bwd_{l}')
    dins, dpar = _tile_bwd(_f_dn_pre, [*xs, ab], pre_params, dpre, [True] * 5, [True] * 6 + [False] * 3, tm,
                           f'dn_pre_bwd_{l}')
    dqkv = dins[DN_CONV - 1]
    for j in range(DN_CONV - 1):
        dqkv = dqkv + _advance(dins[j], DN_CONV - 1 - j)
    dab = dins[DN_CONV]
    grads = {'dn_conv': jnp.concatenate(dpar[:DN_CONV], axis=0),
             'dn_a_log': dpar[4].reshape(N_HEADS, HEAD_DIM).sum(1), 'dn_dt_bias': dpar[5].reshape(N_HEADS, HEAD_DIM).sum(1),
             'dn_o_norm': dgain.reshape(N_HEADS, HEAD_DIM).sum(0)}
    return dqkv, dab[:, :N_HEADS], dab[:, N_HEADS:2 * N_HEADS], dgate, grads


def _t5_bucket(dist):
    exact = T5_BUCKETS // 2
    df = jnp.maximum(dist, 1).astype(f32)
    large = exact + (jnp.log(df / exact) / math.log(T5_MAX_DIST / exact) * (T5_BUCKETS - exact)).astype(jnp.int32)
    large = jnp.minimum(large, T5_BUCKETS - 1)
    return jnp.where(dist < exact, dist, large)


def _split_cols(t, sizes):
    out, start = [], 0
    for s in sizes:
        out.append(t[..., start:start + s])
        start += s
    return out


def _mixers_fwd(proj, mp, l):
    c_q, c_kv, k_rope, u_s5, qkv_dil, qkv_dn, a_dn, b_dn, gate_dn = _split_cols(proj, IN_SPLITS)
    y_mla, s_mla = _mla_fwd(c_q, c_kv, k_rope, mp, l)
    y_s5, s_s5 = _s5_fwd(u_s5, mp, l)
    y_dil, s_dil = _dil_fwd(qkv_dil, mp, l)
    y_dn, s_dn = _dn_fwd(qkv_dn, a_dn, b_dn, gate_dn, mp, l)
    return jnp.concatenate([y_mla, y_s5, y_dil, y_dn], axis=-1), (s_mla, s_s5, s_dil, s_dn)


def _mixers_bwd(dmixed, saved, l):
    s_mla, s_s5, s_dil, s_dn = saved
    d_mla, d_s5, d_dil, d_dn = _split_cols(dmixed, (GROUP_W,) * 4)
    dc_q, dc_kv, dk_rope, g_mla = _mla_bwd(d_mla, s_mla, l)
    du, g_s5 = _s5_bwd(d_s5, s_s5, l)
    dqkv_dil, g_dil = _dil_bwd(d_dil, s_dil, l)
    dqkv_dn, da, db, dgate, g_dn = _dn_bwd(d_dn, s_dn, l)
    parts = [dc_q, dc_kv, dk_rope, du, dqkv_dil, dqkv_dn, da, db, dgate]
    dproj = jnp.concatenate([p.astype(bf16) for p in parts], axis=-1)
    return dproj, {**g_mla, **g_s5, **g_dil, **g_dn}


MIXER_PARAMS = ['mla_q_norm', 'mla_kv_norm', 'mla_w_uq', 'mla_w_ukv', 'mla_qk_q', 'mla_qk_k', 's5_lambda_re',
                's5_lambda_im', 's5_log_dt', 's5_b_re', 's5_b_im', 's5_c_re', 's5_c_im', 's5_d', 's5_w_glu',
                'dil_q_norm', 'dil_k_norm', 't5_bias', 'dn_conv', 'dn_a_log', 'dn_dt_bias', 'dn_o_norm']


def _layer_fwd(h, W, l):
    S = h.shape[0]
    tm = _pick(S, (256, 128))
    g1 = W['attn_norm'][l][None]
    g2 = W['ffn_norm'][l][None]
    (n1,) = _tile_fwd(_f_rms, [h], [g1], [(D_MODEL, bf16)], tm, f'rms1_fwd_{l}')
    proj = _mm(n1, W['w_in'][l], 'nn', f'proj_fwd_{l}')
    mp = {k: (W[k] if k == 't5_bias' else W[k][l]).astype(f32) for k in MIXER_PARAMS}
    mixed, mix_saved = _mixers_fwd(proj, mp, l)
    mixed_b = mixed.astype(bf16)
    h2 = _mm(mixed_b, W['w_out'][l], 'nn', f'out_fwd_{l}', add=h)
    (n2,) = _tile_fwd(_f_rms, [h2], [g2], [(D_MODEL, bf16)], tm, f'rms2_fwd_{l}')
    w13 = jnp.concatenate([W['ffn_w1'][l], W['ffn_w3'][l]], axis=1)
    uv = _mm(n2, w13, 'nn', f'ffn13_fwd_{l}')
    (act,) = _tile_fwd(_f_swiglu, [uv], [], [(FFN_HIDDEN, bf16)], tm, f'swiglu_fwd_{l}')
    h3 = _mm(act, W['ffn_w2'][l], 'nn', f'ffn2_fwd_{l}', add=h2)
    saved = dict(h=h, n1=n1, mix=mix_saved, mixed=mixed_b, h2=h2, n2=n2, uv=uv, act=act, w13=w13)
    return h3, saved


def _layer_bwd(dh3, saved, W, l):
    S = dh3.shape[0]
    tm = _pick(S, (256, 128))
    g1 = W['attn_norm'][l][None]
    g2 = W['ffn_norm'][l][None]
    grads = {}
    dact = _mm(dh3, W['ffn_w2'][l], 'nt', f'ffn2_dx_{l}')
    grads['ffn_w2'] = _mm(saved['act'], dh3, 'tn', f'ffn2_dw_{l}', out_dtype=bf16)
    (duv,), _ = _tile_bwd(_f_swiglu, [saved['uv']], [], [dact], [True], [], tm, f'swiglu_bwd_{l}', dt_dtypes=[bf16])
    dn2 = _mm(duv, saved['w13'], 'nt', f'ffn13_dx_{l}')
    dw13 = _mm(saved['n2'], duv, 'tn', f'ffn13_dw_{l}', out_dtype=bf16)
    grads['ffn_w1'], grads['ffn_w3'] = dw13[:, :FFN_HIDDEN], dw13[:, FFN_HIDDEN:]
    (dh2n,), (dg2,) = _tile_bwd(_f_rms, [saved['h2']], [g2], [dn2], [True], [True], tm, f'rms2_bwd_{l}')
    grads['ffn_norm'] = dg2[0]
    dh2 = dh3 + dh2n
    dmixed = _mm(dh2, W['w_out'][l], 'nt', f'out_dx_{l}')
    grads['w_out'] = _mm(saved['mixed'], dh2, 'tn', f'out_dw_{l}', out_dtype=bf16)
    dproj, dmp = _mixers_bwd(dmixed, saved['mix'], l)
    for k in MIXER_PARAMS:
        grads[k] = dmp[k]
    dn1 = _mm(dproj, W['w_in'][l], 'nt', f'proj_dx_{l}')
    grads['w_in'] = _mm(saved['n1'], dproj, 'tn', f'proj_dw_{l}', out_dtype=bf16)
    (dh1n,), (dg1,) = _tile_bwd(_f_rms, [saved['h']], [g1], [dn1], [True], [True], tm, f'rms1_bwd_{l}')
    grads['attn_norm'] = dg1[0]
    return dh2 + dh1n, grads


def kernel(x, attn_norm, w_in, w_out, mla_q_norm, mla_kv_norm, mla_w_uq, mla_w_ukv, mla_qk_q, mla_qk_k, s5_lambda_re, s5_lambda_im, s5_log_dt, s5_b_re, s5_b_im, s5_c_re, s5_c_im, s5_d, s5_w_glu, dil_q_norm, dil_k_norm, t5_bias, dn_conv, dn_a_log, dn_dt_bias, dn_o_norm, ffn_norm, ffn_w1, ffn_w3, ffn_w2, loss_target, m_attn_norm, m_w_in, m_w_out, m_mla_q_norm, m_mla_kv_norm, m_mla_w_uq, m_mla_w_ukv, m_mla_qk_q, m_mla_qk_k, m_s5_lambda_re, m_s5_lambda_im, m_s5_log_dt, m_s5_b_re, m_s5_b_im, m_s5_c_re, m_s5_c_im, m_s5_d, m_s5_w_glu, m_dil_q_norm, m_dil_k_norm, m_t5_bias, m_dn_conv, m_dn_a_log, m_dn_dt_bias, m_dn_o_norm, m_ffn_norm, m_ffn_w1, m_ffn_w3, m_ffn_w2, v_attn_norm, v_w_in, v_w_out, v_mla_q_norm, v_mla_kv_norm, v_mla_w_uq, v_mla_w_ukv, v_mla_qk_q, v_mla_qk_k, v_s5_lambda_re, v_s5_lambda_im, v_s5_log_dt, v_s5_b_re, v_s5_b_im, v_s5_c_re, v_s5_c_im, v_s5_d, v_s5_w_glu, v_dil_q_norm, v_dil_k_norm, v_t5_bias, v_dn_conv, v_dn_a_log, v_dn_dt_bias, v_dn_o_norm, v_ffn_norm, v_ffn_w1, v_ffn_w3, v_ffn_w2):
    given = dict(locals())
    w_loc = {n: given[n] for n in WEIGHTS}
    m_loc = {n: given['m_' + n] for n in WEIGHTS}
    v_loc = {n: given['v_' + n] for n in WEIGHTS}
    big_names = list(BIG)

    gathered = _gather_tensors([w_loc[n].astype(bf16) for n in big_names])
    W = {n: _from_shards(n, g) for n, g in zip(big_names, gathered)}
    for n in SMALL:
        W[n] = w_loc[n]

    h = x[0]
    saved = []
    for l in range(DEPTH):
        h, sv = _layer_fwd(h, W, l)
        saved.append(sv)
    parts_loss, dh = _loss_head(h, loss_target[0])
    loss = lax.psum(jnp.sum(parts_loss), ('x', 'y', 'c'))

    layer_grads = [None] * DEPTH
    for l in reversed(range(DEPTH)):
        dh, layer_grads[l] = _layer_bwd(dh, saved[l], W, l)
    grad_x = dh[None]
    small_full = []
    for n in SMALL:
        if n == 't5_bias':
            small_full.append(layer_grads[0][n] + layer_grads[1][n])
        else:
            small_full.append(jnp.stack([layer_grads[l][n] for l in range(DEPTH)]))

    gs = [jnp.stack([_by_shard(n, layer_grads[l][n]).astype(bf16) for l in range(DEPTH)], axis=1) for n in big_names]
    small_shapes = [w_loc[n].shape for n in SMALL]
    small_pack = _pack(small_full)
    from_sib, recv_small = _swap_with_sibling(gs, small_pack)
    chip_sums = [_chip_sum_of(g, r, 'chip_sum_' + n) for n, g, r in zip(big_names, gs, from_sib)]
    chip_small = _small_chip_sum(small_pack, recv_small)
    from_chips, from_chips_small = _exchange_between_chips(chip_sums, chip_small)
    totals = [_shard_total_of(g, r, rc, 'shard_total_' + n) for n, g, r, rc in zip(big_names, gs, from_sib, from_chips)]
    g_big = _join_with_sibling(totals)

    g_small_p, d_small_p, m_small_p, v_small_p = _small_update(
        small_pack, recv_small, from_chips_small, _pack([w_loc[n] for n in SMALL]),
        _pack([m_loc[n] for n in SMALL]), _pack([v_loc[n] for n in SMALL]))
    grad, delta, new_m, new_v = {}, {}, {}, {}
    for n, g_, d_, m_, v_ in zip(SMALL, _unpack(g_small_p, small_shapes), _unpack(d_small_p, small_shapes),
                                 _unpack(m_small_p, small_shapes), _unpack(v_small_p, small_shapes)):
        grad[n], delta[n], new_m[n], new_v[n] = g_, d_, m_, v_
    for n, g_ in zip(big_names, g_big):
        grad[n] = g_
        delta[n], new_m[n], new_v[n] = _adamw(w_loc[n], g_, m_loc[n], v_loc[n], 'adamw_' + n)
    return (loss, grad_x, *[grad[n] for n in WEIGHTS], *[delta[n] for n in WEIGHTS],
            *[new_m[n] for n in WEIGHTS], *[new_v[n] for n in WEIGHTS])
```

```python
import functools
import math

import numpy as np
import jax
import jax.numpy as jnp
from jax import lax
from jax.experimental import pallas as pl
from jax.experimental.pallas import tpu as pltpu

f32 = jnp.float32
bf16 = jnp.bfloat16
HI = lax.Precision.HIGHEST
MESH = pl.DeviceIdType.MESH

VMEM_LIMIT_BYTES = 48 * 1024 * 1024
MM_VMEM_BUDGET_BYTES = 32 * 1024 * 1024
LANES = 128

D_MODEL = 1024
DEPTH = 2
GROUP_W = 256
HEAD_DIM = 64
EPS = 1e-6
NEG_INF = -1e30
N_HEADS = 4
MLA_NOPE, MLA_ROPE = 64, 32
MLA_DQK = MLA_NOPE + MLA_ROPE
ROPE_THETA = 10000.0
Q_BLOCK = 128
S5_G, S5_CG, S5_P = 16, 16, 64
DIL_PAIRS = ((128, 1), (512, 4), (2048, 16))
T5_BUCKETS, T5_MAX_DIST = 32, 2048
DN_CHUNK = 64
FFN_HIDDEN = 2816
IN_SPLITS = (256, 128, 32, 256, 768, 768, 4, 4, 256)
IN_COLS = sum(IN_SPLITS)

ADAM_LR, ADAM_B1, ADAM_B2, ADAM_EPS, ADAM_WD, ADAM_STEP = 0.001, 0.9, 0.999, 1e-08, 0.01, 10

WEIGHTS = ['attn_norm', 'w_in', 'w_out', 'mla_q_norm', 'mla_kv_norm', 'mla_w_uq', 'mla_w_ukv', 'mla_qk_q', 'mla_qk_k',
           's5_lambda_re', 's5_lambda_im', 's5_log_dt', 's5_b_re', 's5_b_im', 's5_c_re', 's5_c_im', 's5_d', 's5_w_glu',
           'dil_q_norm', 'dil_k_norm', 't5_bias', 'dn_conv', 'dn_a_log', 'dn_dt_bias', 'dn_o_norm', 'ffn_norm',
           'ffn_w1', 'ffn_w3', 'ffn_w2']
BIG = {'w_in': 2, 'w_out': 1, 'mla_w_uq': 2, 'mla_w_ukv': 2, 's5_w_glu': 2, 'dn_conv': 2, 'ffn_w1': 2, 'ffn_w3': 2,
       'ffn_w2': 1}
SMALL = [n for n in WEIGHTS if n not in BIG]
N_SHARDS = 4
PACK_COLS = 1024


def _cparams(sem=None, big=False):
    kw = {}
    if sem is not None:
        kw['dimension_semantics'] = sem
    if big:
        kw['vmem_limit_bytes'] = VMEM_LIMIT_BYTES
    return pltpu.CompilerParams(**kw)


def _pick(n, prefs):
    for p in prefs:
        if p <= n and n % p == 0:
            return p
    return n


def _lane_tile(n, cap):
    for t in range(cap - cap % LANES, 0, -LANES):
        if n % t == 0:
            return t
    return n


def _mm(a, b, mode, name, add=None, out_dtype=f32):
    if mode == 'nn':
        (M, K), (K2, N) = a.shape, b.shape
    elif mode == 'nt':
        (M, K), (N, K2) = a.shape, b.shape
    else:
        (K, M), (K2, N) = a.shape, b.shape
    assert K == K2, (name, a.shape, b.shape)
    tk = K if K <= 2816 else _pick(K, (2816, 2048, 1408, 1024, 512))
    cap_m, cap_n = (1408 if mode == 'tn' else 512), 1408

    def need(tm_, tn_):
        per_step = tm_ * tk * a.dtype.itemsize + tk * tn_ * b.dtype.itemsize + tm_ * tn_ * jnp.dtype(out_dtype).itemsize
        if add is not None:
            per_step += tm_ * tn_ * add.dtype.itemsize
        return 2 * per_step + tm_ * tn_ * 4

    tm, tn = _lane_tile(M, cap_m), _lane_tile(N, cap_n)
    while need(tm, tn) > MM_VMEM_BUDGET_BYTES and cap_m > LANES:
        cap_m //= 2
        tm = _lane_tile(M, cap_m)
    nk = K // tk
    dims = {'nn': (((1,), (0,)), ((), ())), 'nt': (((1,), (1,)), ((), ())), 'tn': (((0,), (0,)), ((), ()))}[mode]
    has_add = add is not None

    def body(*refs):
        a_ref, b_ref = refs[0], refs[1]
        add_ref = refs[2] if has_add else None
        o_ref = refs[3] if has_add else refs[2]
        part = lax.dot_general(a_ref[...].astype(bf16), b_ref[...].astype(bf16), dims, preferred_element_type=f32)
        if nk == 1:
            if has_add:
                part = part + add_ref[...].astype(f32)
            o_ref[...] = part.astype(out_dtype)
        else:
            acc_ref = refs[-1]
            k = pl.program_id(2)

            @pl.when(k == 0)
            def _():
                acc_ref[...] = part

            @pl.when(k > 0)
            def _():
                acc_ref[...] += part

            @pl.when(k == nk - 1)
            def _():
                r = acc_ref[...]
                if has_add:
                    r = r + add_ref[...].astype(f32)
                o_ref[...] = r.astype(out_dtype)

    if mode == 'nn':
        a_spec = pl.BlockSpec((tm, tk), lambda i, j, k: (i, k))
        b_spec = pl.BlockSpec((tk, tn), lambda i, j, k: (k, j))
    elif mode == 'nt':
        a_spec = pl.BlockSpec((tm, tk), lambda i, j, k: (i, k))
        b_spec = pl.BlockSpec((tn, tk), lambda i, j, k: (j, k))
    else:
        a_spec = pl.BlockSpec((tk, tm), lambda i, j, k: (k, i))
        b_spec = pl.BlockSpec((tk, tn), lambda i, j, k: (k, j))
    in_specs = [a_spec, b_spec]
    args = [a, b]
    if has_add:
        in_specs.append(pl.BlockSpec((tm, tn), lambda i, j, k: (i, j)))
        args.append(add)
    return pl.pallas_call(
        body, name=name, grid=(M // tm, N // tn, nk), in_specs=in_specs,
        out_specs=pl.BlockSpec((tm, tn), lambda i, j, k: (i, j)),
        out_shape=jax.ShapeDtypeStruct((M, N), out_dtype),
        scratch_shapes=[pltpu.VMEM((tm, tn), f32)] if nk > 1 else [],
        compiler_params=_cparams(('parallel', 'parallel', 'arbitrary'), big=True),
    )(*args)


def _full_spec(p):
    nd = p.ndim
    return pl.BlockSpec(p.shape, lambda i, _nd=nd: (0,) * _nd)


def _tile_fwd(f, tiled, params, outs, tm, name):
    S = tiled[0].shape[0]
    nt, npar = len(tiled), len(params)

    def body(*refs):
        vals = [r[...].astype(f32) for r in refs[:nt + npar]]
        res = f(*vals)
        for r, o in zip(res, refs[nt + npar:]):
            o[...] = r.astype(o.dtype)

    return pl.pallas_call(
        body, name=name, grid=(S // tm,),
        in_specs=[pl.BlockSpec((tm, t.shape[1]), lambda i: (i, 0)) for t in tiled] + [_full_spec(p) for p in params],
        out_specs=[pl.BlockSpec((tm, c), lambda i: (i, 0)) for c, _ in outs],
        out_shape=[jax.ShapeDtypeStruct((S, c), dt) for c, dt in outs],
        compiler_params=_cparams(('parallel',), big=True),
    )(*tiled, *params)


def _tile_bwd(f, tiled, params, cts, diff_t, diff_p, tm, name, dt_dtypes=None):
    S = tiled[0].shape[0]
    nt, npar, nc = len(tiled), len(params), len(cts)
    it = [i for i in range(nt) if diff_t[i]]
    ip = [i for i in range(npar) if diff_p[i]]
    if dt_dtypes is None:
        dt_dtypes = [f32] * len(it)

    def body(*refs):
        vals = [r[...].astype(f32) for r in refs[:nt + npar]]
        ct_vals = tuple(r[...].astype(f32) for r in refs[nt + npar:nt + npar + nc])
        out_refs = refs[nt + npar + nc:]

        def g(*dv):
            full = list(vals)
            for k, i in enumerate(it):
                full[i] = dv[k]
            for k, i in enumerate(ip):
                full[nt + i] = dv[len(it) + k]
            return tuple(f(*full))

        _, vjp = jax.vjp(g, *[vals[i] for i in it], *[vals[nt + i] for i in ip])
        grads = vjp(ct_vals)
        for k in range(len(it)):
            out_refs[k][...] = grads[k].astype(out_refs[k].dtype)
        step = pl.program_id(0)
        for k in range(len(ip)):
            o = out_refs[len(it) + k]
            gk = grads[len(it) + k]

            @pl.when(step == 0)
            def _(o=o, gk=gk):
                o[...] = gk

            @pl.when(step > 0)
            def _(o=o, gk=gk):
                o[...] += gk

    out_specs = [pl.BlockSpec((tm, tiled[i].shape[1]), lambda i_: (i_, 0)) for i in it] + [_full_spec(params[i]) for i in ip]
    out_shape = [jax.ShapeDtypeStruct(tiled[i].shape, dt_dtypes[k]) for k, i in enumerate(it)] + \
                [jax.ShapeDtypeStruct(params[i].shape, f32) for i in ip]
    res = pl.pallas_call(
        body, name=name, grid=(S // tm,),
        in_specs=[pl.BlockSpec((tm, t.shape[1]), lambda i: (i, 0)) for t in tiled] + [_full_spec(p) for p in params] +
                 [pl.BlockSpec((tm, c.shape[1]), lambda i: (i, 0)) for c in cts],
        out_specs=out_specs, out_shape=out_shape,
        compiler_params=_cparams(('arbitrary',), big=True),
    )(*tiled, *params, *cts)
    return list(res[:len(it)]), list(res[len(it):])


def _rms(x, g):
    return x * lax.rsqrt(jnp.mean(x * x, axis=-1, keepdims=True) + EPS) * g


def _f_rms(x, g):
    return (_rms(x, g),)


def _f_swiglu(uv):
    h = uv.shape[1] // 2
    u, v = uv[:, :h], uv[:, h:]
    return (u * jax.nn.sigmoid(u) * v,)


def _loss_head(y, target):
    S, D = y.shape
    tm = _pick(S, (256, 128))

    def body(y_ref, t_ref, part_ref, dy_ref):
        e = y_ref[...] - t_ref[...]
        dy_ref[...] = e * (1.0 / D)
        s = 0.5 * jnp.sum(jnp.sum(e * e, axis=1, keepdims=True), axis=0, keepdims=True) * (1.0 / D)
        r = lax.broadcasted_iota(jnp.int32, (8, LANES), 0)
        c = lax.broadcasted_iota(jnp.int32, (8, LANES), 1)
        part_ref[0] = jnp.where((r == 0) & (c == 0), s, 0.0)

    return pl.pallas_call(
        body, name='loss_head', grid=(S // tm,),
        in_specs=[pl.BlockSpec((tm, D), lambda i: (i, 0))] * 2,
        out_specs=[pl.BlockSpec((1, 8, LANES), lambda i: (i, 0, 0)), pl.BlockSpec((tm, D), lambda i: (i, 0))],
        out_shape=[jax.ShapeDtypeStruct((S // tm, 8, LANES), f32), jax.ShapeDtypeStruct((S, D), f32)],
        compiler_params=_cparams(('parallel',)),
    )(y, target)


def _pack_rows_of(shape):
    rows = -(-math.prod(shape) // PACK_COLS)
    return -(-rows // 8) * 8


def _pack(arrs):
    parts = []
    for a in arrs:
        rows = _pack_rows_of(a.shape)
        flat = a.astype(f32).reshape(-1)
        parts.append(jnp.pad(flat, (0, rows * PACK_COLS - flat.shape[0])).reshape(rows, PACK_COLS))
    return jnp.concatenate(parts, axis=0)


def _unpack(pack, shapes):
    out, row = [], 0
    for s in shapes:
        rows = _pack_rows_of(s)
        out.append(pack[row:row + rows].reshape(-1)[:math.prod(s)].reshape(s))
        row += rows
    return out


ANY = pl.BlockSpec(memory_space=pl.ANY)


def _place():
    return lax.axis_index('x'), lax.axis_index('y'), lax.axis_index('c')


def _where():
    return jnp.stack([lax.axis_index('c'), 2 * lax.axis_index('x') + lax.axis_index('y')]).astype(jnp.int32)


def _remote(src, dst, send_sems, recv_sems, k, to):
    return pltpu.make_async_remote_copy(src_ref=src, dst_ref=dst, send_sem=send_sems.at[k], recv_sem=recv_sems.at[k],
                                        device_id=to, device_id_type=MESH)


def _gather_tensors(ws):
    n = len(ws)

    def body(*refs):
        w_refs, g_refs = refs[:n], refs[n:2 * n]
        send_sems, recv_sems = refs[2 * n:]
        x, y, c = _place()
        me, sib = (x, y, c), (x, y, 1 - c)
        chips = [(1 - x, y), (x, 1 - y), (1 - x, 1 - y)]
        first = [_remote(w_refs[t].at[c], g_refs[t].at[2 * x + y, c], send_sems, recv_sems, 6 * t + j, (px, py, c))
                 for j, (px, py) in enumerate(chips) for t in range(n)]
        for cp in first:
            cp.start()
        passed = []
        for j, (px, py) in enumerate(chips):
            for t in range(n):
                here = g_refs[t].at[2 * px + py, c]
                _remote(here, here, send_sems, recv_sems, 6 * t + j, me).wait_recv()
                cp = _remote(here, here, send_sems, recv_sems, 6 * t + 3 + j, sib)
                cp.start()
                passed.append(cp)
        for j, (px, py) in enumerate(chips):
            for t in range(n):
                there = g_refs[t].at[2 * px + py, 1 - c]
                _remote(there, there, send_sems, recv_sems, 6 * t + 3 + j, me).wait_recv()
        for cp in first + passed:
            cp.wait_send()

    own = 2 * lax.axis_index('x') + lax.axis_index('y')
    res = pl.pallas_call(
        body, name='gather_weights', in_specs=[ANY] * n, out_specs=[ANY] * n,
        out_shape=[jax.ShapeDtypeStruct((N_SHARDS,) + w.shape, w.dtype) for w in ws],
        scratch_shapes=[pltpu.SemaphoreType.DMA((6 * n,)), pltpu.SemaphoreType.DMA((6 * n,))],
    )(*ws)
    return [lax.dynamic_update_slice(g, w[None], (own, 0, 0, 0)) for g, w in zip(res, ws)]


def _swap_with_sibling(gs, small):
    n = len(gs)

    def body(*refs):
        g_refs, s_ref = refs[:n], refs[n]
        r_refs, rs_ref = refs[n + 1:2 * n + 1], refs[2 * n + 1]
        send_sems, recv_sems = refs[2 * n + 2:]
        x, y, c = _place()
        sib = (x, y, 1 - c)
        cps = [_remote(g_refs[t].at[:, 1 - c], r_refs[t], send_sems, recv_sems, t, sib) for t in range(n)]
        cps.append(_remote(s_ref, rs_ref, send_sems, recv_sems, n, sib))
        for cp in cps:
            cp.start()
        for cp in cps:
            cp.wait()

    res = pl.pallas_call(
        body, name='swap_with_sibling', in_specs=[ANY] * (n + 1), out_specs=[ANY] * (n + 1),
        out_shape=[jax.ShapeDtypeStruct((N_SHARDS,) + g.shape[2:], g.dtype) for g in gs] +
                  [jax.ShapeDtypeStruct(small.shape, small.dtype)],
        scratch_shapes=[pltpu.SemaphoreType.DMA((n + 1,)), pltpu.SemaphoreType.DMA((n + 1,))],
    )(*gs, small)
    return list(res[:n]), res[n]


def _exchange_between_chips(cs, small):
    n = len(cs)

    def body(*refs):
        c_refs, s_ref = refs[:n], refs[n]
        r_refs, rs_ref = refs[n + 1:2 * n + 1], refs[2 * n + 1]
        send_sems, recv_sems = refs[2 * n + 2:]
        x, y, c = _place()
        chips = [(1 - x, y), (x, 1 - y), (1 - x, 1 - y)]
        cps = []
        for j, (px, py) in enumerate(chips):
            for t in range(n):
                cps.append(_remote(c_refs[t].at[2 * px + py], r_refs[t].at[j], send_sems, recv_sems, 3 * t + j, (px, py, c)))
            cps.append(_remote(s_ref, rs_ref.at[j], send_sems, recv_sems, 3 * n + j, (px, py, c)))
        for cp in cps:
            cp.start()
        for cp in cps:
            cp.wait()

    res = pl.pallas_call(
        body, name='exchange_between_chips', in_specs=[ANY] * (n + 1), out_specs=[ANY] * (n + 1),
        out_shape=[jax.ShapeDtypeStruct((3,) + c.shape[1:], c.dtype) for c in cs] +
                  [jax.ShapeDtypeStruct((3,) + small.shape, small.dtype)],
        scratch_shapes=[pltpu.SemaphoreType.DMA((3 * n + 3,)), pltpu.SemaphoreType.DMA((3 * n + 3,))],
    )(*cs, small)
    return list(res[:n]), res[n]


def _join_with_sibling(ts):
    n = len(ts)

    def body(*refs):
        t_refs, o_refs = refs[:n], refs[n:2 * n]
        send_sems, recv_sems = refs[2 * n:]
        x, y, c = _place()
        cps = [_remote(t_refs[t], o_refs[t], send_sems, recv_sems, t, (x, y, 1 - c)) for t in range(n)]
        for cp in cps:
            cp.start()
        for cp in cps:
            cp.wait()

    theirs = pl.pallas_call(
        body, name='join_with_sibling', in_specs=[ANY] * n, out_specs=[ANY] * n,
        out_shape=[jax.ShapeDtypeStruct(t.shape, t.dtype) for t in ts],
        scratch_shapes=[pltpu.SemaphoreType.DMA((n,)), pltpu.SemaphoreType.DMA((n,))],
    )(*ts)
    south = lax.axis_index('c') == 0
    return [jnp.stack([jnp.where(south, mine, other), jnp.where(south, other, mine)]) for mine, other in zip(ts, theirs)]


def _row_tile(a):
    return _pick(a, (512, 256, 128, 64, 32, 16, 8))


def _chip_sum_of(g, r, name):
    _, _, a, b = g.shape
    tr = _row_tile(a)

    def body(w_ref, g_ref, r_ref, o_ref):
        o_ref[...] = (g_ref[0].astype(f32) + r_ref[...].astype(f32)).astype(o_ref.dtype)

    return pl.pallas_call(
        body, name=name,
        grid_spec=pltpu.PrefetchScalarGridSpec(
            num_scalar_prefetch=1, grid=(N_SHARDS, a // tr),
            in_specs=[pl.BlockSpec((1, 1, tr, b), lambda s, i, w: (s, w[0], i, 0)),
                      pl.BlockSpec((1, tr, b), lambda s, i, w: (s, i, 0))],
            out_specs=pl.BlockSpec((1, tr, b), lambda s, i, w: (s, i, 0))),
        out_shape=jax.ShapeDtypeStruct((N_SHARDS, a, b), bf16),
        compiler_params=_cparams(('parallel', 'parallel')),
    )(_where(), g, r)


def _shard_total_of(g, r, rc, name):
    _, _, a, b = g.shape
    tr = _row_tile(a)

    def body(w_ref, g_ref, r_ref, rc_ref, o_ref):
        t = g_ref[0, 0].astype(f32) + r_ref[0].astype(f32)
        t = t + rc_ref[0].astype(f32)
        t = t + rc_ref[1].astype(f32)
        t = t + rc_ref[2].astype(f32)
        o_ref[...] = t

    return pl.pallas_call(
        body, name=name,
        grid_spec=pltpu.PrefetchScalarGridSpec(
            num_scalar_prefetch=1, grid=(a // tr,),
            in_specs=[pl.BlockSpec((1, 1, tr, b), lambda i, w: (w[1], w[0], i, 0)),
                      pl.BlockSpec((1, tr, b), lambda i, w: (w[1], i, 0)),
                      pl.BlockSpec((3, tr, b), lambda i, w: (0, i, 0))],
            out_specs=pl.BlockSpec((tr, b), lambda i, w: (i, 0))),
        out_shape=jax.ShapeDtypeStruct((a, b), f32),
        compiler_params=_cparams(('parallel',)),
    )(_where(), g, r, rc)


def _by_shard(name, t):
    r, c = t.shape
    if BIG[name] == 2:
        return t.reshape(r, N_SHARDS, c // N_SHARDS).transpose(1, 0, 2)
    return t.reshape(N_SHARDS, r // N_SHARDS, c)


def _from_shards(name, g):
    s, layers, a, b = g.shape
    if BIG[name] == 2:
        return [g[:, l].transpose(1, 0, 2).reshape(a, s * b) for l in range(layers)]
    return [g[:, l].reshape(s * a, b) for l in range(layers)]


def _adam_math(w, g, m, v):
    m = ADAM_B1 * m + (1.0 - ADAM_B1) * g
    v = ADAM_B2 * v + (1.0 - ADAM_B2) * (g * g)
    m_hat = m / (1.0 - ADAM_B1 ** ADAM_STEP)
    v_hat = v / (1.0 - ADAM_B2 ** ADAM_STEP)
    delta = -ADAM_LR * (m_hat / (jnp.sqrt(v_hat) + ADAM_EPS) + ADAM_WD * w)
    return delta, m, v


def _small_update(own, sib, chips, w, m, v):
    def body(o_ref, s_ref, c_ref, w_ref, m_ref, v_ref, g_out, d_out, m_out, v_out):
        chip = o_ref[...] + s_ref[...]
        g = (chip + c_ref[0]) + (c_ref[1] + c_ref[2])
        d, mn, vn = _adam_math(w_ref[...], g, m_ref[...], v_ref[...])
        g_out[...] = g
        d_out[...] = d
        m_out[...] = mn
        v_out[...] = vn

    return pl.pallas_call(body, name='small_update', out_shape=[jax.ShapeDtypeStruct(own.shape, f32)] * 4)(
        own, sib, chips, w, m, v)


def _small_chip_sum(own, sib):
    def body(o_ref, s_ref, out):
        out[...] = o_ref[...] + s_ref[...]
    return pl.pallas_call(body, name='small_chip_sum', out_shape=jax.ShapeDtypeStruct(own.shape, f32))(own, sib)


def _adamw(w, g, m, v, name):
    shape = w.shape
    w2, g2, m2, v2 = [t.reshape(-1, shape[-1]) for t in (w, g, m, v)]
    rows, cols = w2.shape
    tr = _pick(rows, (256, 128, 64, 32, 16, 8))

    def body(w_ref, g_ref, m_ref, v_ref, d_out, m_out, v_out):
        d, mn, vn = _adam_math(w_ref[...], g_ref[...], m_ref[...], v_ref[...])
        d_out[...] = d
        m_out[...] = mn
        v_out[...] = vn

    spec = pl.BlockSpec((tr, cols), lambda i: (i, 0))
    res = pl.pallas_call(body, name=name, grid=(rows // tr,), in_specs=[spec] * 4, out_specs=[spec] * 3,
                         out_shape=[jax.ShapeDtypeStruct((rows, cols), f32)] * 3,
                         compiler_params=_cparams(('parallel',)))(w2, g2, m2, v2)
    return [r.reshape(shape) for r in res]


def _dg(a, b, ca, cb):
    return lax.dot_general(a.astype(bf16), b.astype(bf16), (((ca,), (cb,)), ((), ())), preferred_element_type=f32)


@jax.custom_vjp
def _bmm(a, b):
    return _dg(a, b, 1, 0)


_bmm.defvjp(lambda a, b: (_dg(a, b, 1, 0), (a, b)), lambda r, g: (_dg(g, r[1], 1, 1), _dg(r[0], g, 0, 0)))


@jax.custom_vjp
def _bmm_nt(a, b):
    return _dg(a, b, 1, 1)


_bmm_nt.defvjp(lambda a, b: (_dg(a, b, 1, 1), (a, b)), lambda r, g: (_dg(g, r[1], 1, 0), _dg(g, r[0], 0, 0)))


@jax.custom_vjp
def _bmm_tn(a, b):
    return _dg(a, b, 0, 0)


_bmm_tn.defvjp(lambda a, b: (_dg(a, b, 0, 0), (a, b)), lambda r, g: (_dg(r[1], g, 1, 1), _dg(r[0], g, 1, 0)))


def _hdot(a, b):
    return jnp.dot(a, b, precision=HI, preferred_element_type=f32)


def _hdot_nt(a, b):
    return lax.dot_general(a, b, (((1,), (1,)), ((), ())), precision=HI, preferred_element_type=f32)


def _hdot_tn(a, b):
    return lax.dot_general(a, b, (((0,), (0,)), ((), ())), precision=HI, preferred_element_type=f32)


def _head_mask(h, width=GROUP_W):
    lane = lax.broadcasted_iota(jnp.int32, (1, width), 1)
    return ((lane >= h * HEAD_DIM) & (lane < (h + 1) * HEAD_DIM)).astype(f32)


def _rope_perm():
    p = np.zeros((LANES, LANES), np.float32)
    half = MLA_ROPE // 2
    for i in range(half):
        p[MLA_NOPE + half + i, MLA_NOPE + i] = -1.0
        p[MLA_NOPE + i, MLA_NOPE + half + i] = 1.0
    return jnp.asarray(p)


def _rope_tables(S):
    half = MLA_ROPE // 2
    freqs = ROPE_THETA ** (-jnp.arange(half, dtype=f32) / half)
    ang = jnp.arange(S, dtype=f32)[:, None] * freqs[None, :]
    cos, sin = jnp.cos(ang), jnp.sin(ang)
    ones, zeros = jnp.ones((S, MLA_NOPE), f32), jnp.zeros((S, LANES - MLA_DQK), f32)
    c_tab = jnp.concatenate([ones, cos, cos, zeros], axis=1)
    s_tab = jnp.concatenate([jnp.zeros((S, MLA_NOPE), f32), sin, sin, zeros], axis=1)
    return c_tab, s_tab


def _f_mla_pre(c_q, c_kv, krope, c_tab, s_tab, q_norm, kv_norm, wq0, wq1, wq2, wq3, wk0, wk1, wk2, wk3, wv, gq, gk, perm):
    wq, wk = (wq0, wq1, wq2, wq3), (wk0, wk1, wk2, wk3)
    nq = _rms(c_q, q_norm)
    nkv = _rms(c_kv, kv_norm)

    def norm_rope(t, g):
        t = t * lax.rsqrt(jnp.sum(t * t, axis=-1, keepdims=True) * (1.0 / MLA_DQK) + EPS) * g
        return t * c_tab + _hdot(t, perm) * s_tab

    qs = [norm_rope(_bmm(nq, wq[h]), gq) * (MLA_DQK ** -0.5) for h in range(N_HEADS)]
    ks = [norm_rope(_bmm(nkv, wk[h]) + krope, gk) for h in range(N_HEADS)]
    return (*qs, *ks, _bmm(nkv, wv))


def _f_attn(qs, ks, v, q0):
    tq, S = qs[0].shape[0], ks[0].shape[0]
    qpos = q0 + lax.broadcasted_iota(jnp.int32, (tq, S), 0)
    kpos = lax.broadcasted_iota(jnp.int32, (tq, S), 1)
    keep = kpos <= qpos
    logits = [jnp.where(keep, _bmm_nt(qs[h], ks[h]), NEG_INF) for h in range(N_HEADS)]
    ps = [jnp.exp(lg - jnp.max(lg, axis=-1, keepdims=True)) for lg in logits]
    ps = [p / jnp.sum(p, axis=-1, keepdims=True) for p in ps]
    return sum(_bmm(p, v) * _head_mask(h) for h, p in enumerate(ps))


ATTN_PARTS = 4


def _mla_attn_fwd(qs, ks, v, name):
    S = v.shape[0]
    tq = Q_BLOCK
    parts = ATTN_PARTS if S % (ATTN_PARTS * tq) == 0 else 1
    per = S // parts
    outs = []
    for p in range(parts):
        n_keys = (p + 1) * per
        first_block = p * (per // tq)

        def body(*refs, first_block=first_block):
            q_vals = [r[...] for r in refs[:4]]
            k_vals = [r[...] for r in refs[4:8]]
            refs[9][...] = _f_attn(q_vals, k_vals, refs[8][...], (first_block + pl.program_id(0)) * tq)

        qspec = pl.BlockSpec((tq, LANES), lambda i, fb=first_block: (fb + i, 0))
        outs.append(pl.pallas_call(
            body, name=f'{name}_{p}', grid=(per // tq,),
            in_specs=[qspec] * 4 + [pl.BlockSpec((n_keys, LANES), lambda i: (0, 0))] * 4 +
                     [pl.BlockSpec((n_keys, GROUP_W), lambda i: (0, 0))],
            out_specs=pl.BlockSpec((tq, GROUP_W), lambda i: (i, 0)),
            out_shape=jax.ShapeDtypeStruct((per, GROUP_W), f32),
            compiler_params=_cparams(('parallel',), big=True),
        )(*qs, *ks, v))
    return jnp.concatenate(outs, axis=0)


def _mla_attn_bwd(qs, ks, v, do, name):
    S = v.shape[0]
    tq = Q_BLOCK
    parts = ATTN_PARTS if S % (ATTN_PARTS * tq) == 0 else 1
    per = S // parts
    dq_parts, dkv_sum = [], None
    for p in range(parts):
        n_keys = (p + 1) * per
        first_block = p * (per // tq)

        def body(*refs, first_block=first_block):
            q_vals = [r[...].astype(f32) for r in refs[:4]]
            k_vals = [r[...].astype(f32) for r in refs[4:8]]
            v_val = refs[8][...].astype(f32)
            q0 = (first_block + pl.program_id(0)) * tq
            _, vjp = jax.vjp(lambda a, b, c: _f_attn(a, b, c, q0), q_vals, k_vals, v_val)
            dqs, dks, dv = vjp(refs[9][...])
            outs = refs[10:]
            for h in range(N_HEADS):
                outs[h][...] = dqs[h]
            first = pl.program_id(0) == 0
            for o, g in zip(outs[4:], (*dks, dv)):
                @pl.when(first)
                def _(o=o, g=g):
                    o[...] = g

                @pl.when(jnp.logical_not(first))
                def _(o=o, g=g):
                    o[...] += g

        qspec = pl.BlockSpec((tq, LANES), lambda i, fb=first_block: (fb + i, 0))
        kspec = pl.BlockSpec((n_keys, LANES), lambda i: (0, 0))
        vspec = pl.BlockSpec((n_keys, GROUP_W), lambda i: (0, 0))
        res = pl.pallas_call(
            body, name=f'{name}_{p}', grid=(per // tq,),
            in_specs=[qspec] * 4 + [kspec] * 4 + [vspec, pl.BlockSpec((tq, GROUP_W), lambda i, fb=first_block: (fb + i, 0))],
            out_specs=[pl.BlockSpec((tq, LANES), lambda i: (i, 0))] * 4 + [kspec] * 4 + [vspec],
            out_shape=[jax.ShapeDtypeStruct((per, LANES), f32)] * 4 + [jax.ShapeDtypeStruct((n_keys, LANES), f32)] * 4 +
                      [jax.ShapeDtypeStruct((n_keys, GROUP_W), f32)],
            compiler_params=_cparams(('arbitrary',), big=True),
        )(*qs, *ks, v, do)
        dq_parts.append(res[:4])
        dkv = [jnp.pad(t, ((0, S - n_keys), (0, 0))) for t in res[4:]]
        dkv_sum = dkv if dkv_sum is None else [a_ + b_ for a_, b_ in zip(dkv_sum, dkv)]
    dqs = [jnp.concatenate([dq_parts[p][h] for p in range(parts)], axis=0) for h in range(N_HEADS)]
    return dqs, dkv_sum[:4], dkv_sum[4]


def _mla_params(mp):
    pad = LANES - MLA_DQK
    wq = jnp.pad(mp['mla_w_uq'].reshape(GROUP_W, N_HEADS, MLA_DQK).transpose(1, 0, 2), ((0, 0), (0, 0), (0, pad)))
    wkv = mp['mla_w_ukv'].reshape(LANES, N_HEADS, MLA_NOPE + HEAD_DIM)
    wk = jnp.pad(wkv[:, :, :MLA_NOPE].transpose(1, 0, 2), ((0, 0), (0, 0), (0, LANES - MLA_NOPE)))
    wv = wkv[:, :, MLA_NOPE:].reshape(LANES, GROUP_W)
    gq = jnp.pad(mp['mla_qk_q'], (0, pad))[None]
    gk = jnp.pad(mp['mla_qk_k'], (0, pad))[None]
    return [mp['mla_q_norm'][None], mp['mla_kv_norm'][None], *[wq[h] for h in range(N_HEADS)],
            *[wk[h] for h in range(N_HEADS)], wv, gq, gk, _rope_perm()]


def _mla_fwd(c_q, c_kv, k_rope, mp, l):
    S = c_q.shape[0]
    tm = _pick(S, (256, 128))
    krope = jnp.pad(k_rope, ((0, 0), (MLA_NOPE, LANES - MLA_DQK)))
    c_tab, s_tab = _rope_tables(S)
    tiled = [c_q, c_kv, krope, c_tab, s_tab]
    params = _mla_params(mp)
    res = _tile_fwd(_f_mla_pre, tiled, params, [(LANES, bf16)] * 8 + [(GROUP_W, bf16)], tm, f'mla_pre_fwd_{l}')
    qs, ks, v = res[:4], res[4:8], res[8]
    y = _mla_attn_fwd(qs, ks, v, f'mla_attn_fwd_{l}')
    return y, (tiled, params, qs, ks, v)


def _mla_bwd(dy, saved, l):
    tiled, params, qs, ks, v = saved
    S = dy.shape[0]
    tm = _pick(S, (256, 128))
    dqs, dks, dv = _mla_attn_bwd(qs, ks, v, dy, f'mla_attn_bwd_{l}')
    (dc_q, dc_kv, dkrope), dpar = _tile_bwd(_f_mla_pre, tiled, params, [*dqs, *dks, dv], [True, True, True, False, False],
                                            [True] * 13 + [False], tm, f'mla_pre_bwd_{l}')
    dqn, dkvn = dpar[0], dpar[1]
    dwq, dwk = jnp.stack(dpar[2:6]), jnp.stack(dpar[6:10])
    dwv, dgq, dgk = dpar[10:13]
    dw_uq = dwq[:, :, :MLA_DQK].transpose(1, 0, 2).reshape(GROUP_W, N_HEADS * MLA_DQK)
    dw_ukv = jnp.concatenate([dwk[:, :, :MLA_NOPE].transpose(1, 0, 2), dwv.reshape(LANES, N_HEADS, HEAD_DIM)],
                             axis=2).reshape(LANES, N_HEADS * (MLA_NOPE + HEAD_DIM))
    grads = {'mla_q_norm': dqn[0], 'mla_kv_norm': dkvn[0], 'mla_w_uq': dw_uq, 'mla_w_ukv': dw_ukv,
             'mla_qk_q': dgq[0, :MLA_DQK], 'mla_qk_k': dgk[0, :MLA_DQK]}
    return dc_q, dc_kv, dkrope[:, MLA_NOPE:MLA_DQK], grads


SPAN = 128


def _head_mean_matrix():
    h = np.arange(GROUP_W) // HEAD_DIM
    return jnp.asarray((h[:, None] == h[None, :]).astype(np.float32) / HEAD_DIM)


def _f_dil_pre(q, k, gq, gk, hm):
    qn = q * lax.rsqrt(_hdot(q * q, hm) + EPS) * gq * (HEAD_DIM ** -0.5)
    kn = k * lax.rsqrt(_hdot(k * k, hm) + EPS) * gk
    return qn, kn


def _f_dil_branch(qb, kp, kc, vp, vc, b0, b1, b2, b3, first):
    kcat = jnp.concatenate([kp, kc], axis=0)
    vcat = jnp.concatenate([vp, vc], axis=0)
    qi = lax.broadcasted_iota(jnp.int32, (SPAN, 2 * SPAN), 0) + SPAN
    kj = lax.broadcasted_iota(jnp.int32, (SPAN, 2 * SPAN), 1)
    delta = qi - kj
    valid = (delta >= 0) & (delta <= SPAN) & jnp.logical_not(first & (kj < SPAN))
    masks = [_head_mask(h) for h in range(N_HEADS)]
    raw = [_bmm_nt(qb * hm, kcat) for hm in masks]
    logits = [jnp.where(valid, r + bias, NEG_INF) for r, bias in zip(raw, (b0, b1, b2, b3))]
    ms = [jnp.max(lg, axis=-1, keepdims=True) for lg in logits]
    ps = [jnp.exp(lg - m) for lg, m in zip(logits, ms)]
    pvs = [_bmm(p, vcat) for p in ps]
    o = sum(pv * hm for pv, hm in zip(pvs, masks))
    m_full = sum(m * hm for m, hm in zip(ms, masks))
    l_full = sum(jnp.sum(p, axis=-1, keepdims=True) * hm for p, hm in zip(ps, masks))
    return o, m_full, l_full


def _dil_branch_specs(d, nb):
    cur = pl.BlockSpec((1, SPAN, GROUP_W), lambda r, n: (r, n, 0))
    prev = pl.BlockSpec((1, SPAN, GROUP_W), lambda r, n: (r, jnp.maximum(n - 1, 0), 0))
    bias = pl.BlockSpec((1, SPAN, 2 * SPAN), lambda r, n: (0, 0, 0))
    return cur, prev, bias


def _head_table_specs():
    return [pl.BlockSpec((1, SPAN, 2 * SPAN), lambda r, n, h=h: (h, 0, 0)) for h in range(N_HEADS)]


def _dil_branch_fwd(q, k, v, table, name):
    d, L, _ = q.shape
    nb = L // SPAN
    cur, prev, bias = _dil_branch_specs(d, nb)

    def body(q_ref, kp_ref, kc_ref, vp_ref, vc_ref, b0, b1, b2, b3, o_ref, m_ref, l_ref):
        o, m, l = _f_dil_branch(q_ref[0], kp_ref[0], kc_ref[0], vp_ref[0], vc_ref[0], b0[0], b1[0], b2[0], b3[0],
                                pl.program_id(1) == 0)
        o_ref[0] = o
        m_ref[0] = m
        l_ref[0] = l

    return pl.pallas_call(
        body, name=name, grid=(d, nb), in_specs=[cur, prev, cur, prev, cur] + _head_table_specs(),
        out_specs=[cur] * 3, out_shape=[jax.ShapeDtypeStruct(q.shape, f32)] * 3,
        compiler_params=_cparams(('parallel', 'parallel')),
    )(q, k, k, v, v, *[table] * N_HEADS)


def _dil_branch_bwd(q, k, v, table, do, dm, dl, name):
    d, L, _ = q.shape
    nb = L // SPAN
    cur, prev, bias = _dil_branch_specs(d, nb)
    whole = pl.BlockSpec((1, L, GROUP_W), lambda r, n: (r, 0, 0))

    def body(q_ref, kp_ref, kc_ref, vp_ref, vc_ref, b0, b1, b2, b3, do_ref, dm_ref, dl_ref,
             dq_ref, dk_ref, dv_ref, db0, db1, db2, db3):
        r, n = pl.program_id(0), pl.program_id(1)
        first = n == 0
        _, vjp = jax.vjp(lambda *a: _f_dil_branch(*a, first), q_ref[0], kp_ref[0], kc_ref[0], vp_ref[0], vc_ref[0],
                         b0[0], b1[0], b2[0], b3[0])
        dq, dkp, dkc, dvp, dvc, g0, g1, g2, g3 = vjp((do_ref[0], dm_ref[0], dl_ref[0]))
        dq_ref[0] = dq

        @pl.when(first)
        def _():
            dk_ref[...] = jnp.zeros_like(dk_ref)
            dv_ref[...] = jnp.zeros_like(dv_ref)

        rows = pl.ds(pl.multiple_of(n * SPAN, SPAN), SPAN)
        dk_ref[0, rows, :] += dkc
        dv_ref[0, rows, :] += dvc

        @pl.when(n > 0)
        def _():
            before = pl.ds(pl.multiple_of((n - 1) * SPAN, SPAN), SPAN)
            dk_ref[0, before, :] += dkp
            dv_ref[0, before, :] += dvp

        start = first & (r == 0)
        for o, g in zip((db0, db1, db2, db3), (g0, g1, g2, g3)):
            @pl.when(start)
            def _(o=o, g=g):
                o[0] = g

            @pl.when(jnp.logical_not(start))
            def _(o=o, g=g):
                o[0] += g

    res = pl.pallas_call(
        body, name=name, grid=(d, nb), in_specs=[cur, prev, cur, prev, cur] + _head_table_specs() + [cur] * 3,
        out_specs=[cur, whole, whole] + [bias] * 4,
        out_shape=[jax.ShapeDtypeStruct(q.shape, f32)] * 3 + [jax.ShapeDtypeStruct((1, SPAN, 2 * SPAN), f32)] * 4,
        compiler_params=_cparams(('arbitrary', 'arbitrary')),
    )(q, k, k, v, v, *[table] * N_HEADS, do, dm, dl)
    return res[0], res[1], res[2], res[3:]


def _f_dil_merge(o1, m1, l1, o2, m2, l2, o3, m3, l3):
    mx = jnp.maximum(jnp.maximum(m1, m2), m3)
    w1, w2, w3 = jnp.exp(m1 - mx), jnp.exp(m2 - mx), jnp.exp(m3 - mx)
    return ((w1 * o1 + w2 * o2 + w3 * o3) / (w1 * l1 + w2 * l2 + w3 * l3),)


def _bias_onehot(dilation):
    qi = jnp.arange(SPAN, dtype=jnp.int32)[:, None] + SPAN
    kj = jnp.arange(2 * SPAN, dtype=jnp.int32)[None, :]
    bucket = _t5_bucket(jnp.clip(qi - kj, 0, SPAN) * dilation).reshape(-1)
    return (bucket[None, :] == jnp.arange(T5_BUCKETS, dtype=jnp.int32)[:, None]).astype(f32)


def _bias_tables(t5_t, onehot, name):
    N = onehot.shape[1]
    tn = _pick(N, (4096, 2048, 1024))

    def body(t_ref, oh_ref, o_ref):
        o_ref[...] = _hdot(t_ref[...], oh_ref[...])

    return pl.pallas_call(
        body, name=name, grid=(N // tn,),
        in_specs=[pl.BlockSpec((8, T5_BUCKETS), lambda i: (0, 0)), pl.BlockSpec((T5_BUCKETS, tn), lambda i: (0, i))],
        out_specs=pl.BlockSpec((8, tn), lambda i: (0, i)), out_shape=jax.ShapeDtypeStruct((8, N), f32),
        compiler_params=_cparams(('parallel',)),
    )(t5_t, onehot)


def _bias_tables_bwd(d_tab, onehot, name):
    N = onehot.shape[1]
    tn = _pick(N, (4096, 2048, 1024))

    def body(g_ref, oh_ref, o_ref):
        part = _hdot_nt(g_ref[...], oh_ref[...])

        @pl.when(pl.program_id(0) == 0)
        def _():
            o_ref[...] = part

        @pl.when(pl.program_id(0) > 0)
        def _():
            o_ref[...] += part

    return pl.pallas_call(
        body, name=name, grid=(N // tn,),
        in_specs=[pl.BlockSpec((8, tn), lambda i: (0, i)), pl.BlockSpec((T5_BUCKETS, tn), lambda i: (0, i))],
        out_specs=pl.BlockSpec((8, T5_BUCKETS), lambda i: (0, 0)), out_shape=jax.ShapeDtypeStruct((8, T5_BUCKETS), f32),
        compiler_params=_cparams(('arbitrary',)),
    )(d_tab, onehot)


def _by_residue(t, d):
    S, C = t.shape
    return t.reshape(S // d, d, C).transpose(1, 0, 2)


def _from_residue(t):
    d, L, C = t.shape
    return t.transpose(1, 0, 2).reshape(d * L, C)


def _dil_fwd(qkv, mp, l):
    S = qkv.shape[0]
    tm = _pick(S, (256, 128))
    q, k, v = qkv[:, :GROUP_W], qkv[:, GROUP_W:2 * GROUP_W], qkv[:, 2 * GROUP_W:]
    pre_params = [jnp.tile(mp['dil_q_norm'], N_HEADS)[None], jnp.tile(mp['dil_k_norm'], N_HEADS)[None], _head_mean_matrix()]
    qn, kn = _tile_fwd(_f_dil_pre, [q, k], pre_params, [(GROUP_W, f32)] * 2, tm, f'dil_pre_fwd_{l}')
    t5_t = jnp.pad(mp['t5_bias'].T, ((0, 8 - N_HEADS), (0, 0)))
    branches, outs = [], []
    for bi, (_, d) in enumerate(DIL_PAIRS):
        onehot = _bias_onehot(d)
        tab = _bias_tables(t5_t, onehot, f'dil_bias_fwd_{l}_{bi}').reshape(8, SPAN, 2 * SPAN)
        qd, kd, vd = _by_residue(qn, d), _by_residue(kn, d), _by_residue(v, d)
        o, m, lsum = _dil_branch_fwd(qd, kd, vd, tab, f'dil_branch_fwd_{l}_{bi}')
        branches.append((qd, kd, vd, tab, onehot))
        outs += [_from_residue(o), _from_residue(m), _from_residue(lsum)]
    (y,) = _tile_fwd(_f_dil_merge, outs, [], [(GROUP_W, f32)], tm, f'dil_merge_fwd_{l}')
    return y, (q, k, pre_params, branches, outs)


def _dil_bwd(dy, saved, l):
    q, k, pre_params, branches, outs = saved
    S = dy.shape[0]
    tm = _pick(S, (256, 128))
    douts, _ = _tile_bwd(_f_dil_merge, outs, [], [dy], [True] * 9, [], tm, f'dil_merge_bwd_{l}')
    dqn = dkn = dv = None
    dt5_t = None
    for bi, (_, d) in enumerate(DIL_PAIRS):
        qd, kd, vd, tab, onehot = branches[bi]
        do, dm, dl = [_by_residue(t, d) for t in douts[3 * bi:3 * bi + 3]]
        dq_b, dk_b, dv_b, dbias = _dil_branch_bwd(qd, kd, vd, tab, do, dm, dl, f'dil_branch_bwd_{l}_{bi}')
        d_tab = jnp.concatenate([*dbias, jnp.zeros((8 - N_HEADS, SPAN, 2 * SPAN), f32)], axis=0).reshape(8, -1)
        g_t5 = _bias_tables_bwd(d_tab, onehot, f'dil_bias_bwd_{l}_{bi}')
        dq_b, dk_b, dv_b = _from_residue(dq_b), _from_residue(dk_b), _from_residue(dv_b)
        dqn = dq_b if dqn is None else dqn + dq_b
        dkn = dk_b if dkn is None else dkn + dk_b
        dv = dv_b if dv is None else dv + dv_b
        dt5_t = g_t5 if dt5_t is None else dt5_t + g_t5
    (dq, dk), (dgq, dgk) = _tile_bwd(_f_dil_pre, [q, k], pre_params, [dqn, dkn], [True, True], [True, True, False], tm,
                                     f'dil_pre_bwd_{l}')
    grads = {'dil_q_norm': dgq.reshape(N_HEADS, HEAD_DIM).sum(0), 'dil_k_norm': dgk.reshape(N_HEADS, HEAD_DIM).sum(0),
             't5_bias': dt5_t[:N_HEADS].T}
    return jnp.concatenate([dq, dk, dv], axis=1), grads


S5_LANES = S5_G * S5_P
SCAN_SEGMENTS = 8
SCAN_W = 256


def _f_s5_prep(bre, bim, lr, li, logdt_col, expand):
    dt = jnp.sum(jnp.exp(logdt_col) * expand, axis=0, keepdims=True)
    mag = jnp.exp(lr * dt)
    ar, ai = mag * jnp.cos(li * dt), mag * jnp.sin(li * dt)
    den = lr * lr + li * li
    nr, ni = ar - 1.0, ai
    zr = (nr * lr + ni * li) / den
    zi = (ni * lr - nr * li) / den
    bb = jnp.concatenate([zr * bre - zi * bim, zr * bim + zi * bre], axis=1)
    a_rows = jnp.broadcast_to(jnp.concatenate([ar, ai], axis=1), bb.shape)
    return bb, a_rows


def _s5_scan(x, a_rows, name, reverse=False, h=None):
    S = x.shape[0]
    NL = x.shape[1] // 2
    T = S // SCAN_SEGMENTS
    nblk = NL // SCAN_W
    n_in = 4 if reverse else 2

    def body(*refs):
        if reverse:
            (x_hbm, pr_hbm, pi_hbm, ar_ref, ai_ref, hr_hbm, hi_hbm, dar_ref, dai_ref,
             xr_s, xi_s, pr_s, pi_s, hr_s, hi_s, in_sems, out_sems) = refs
        else:
            x_hbm, ar_ref, ai_ref, hr_hbm, hi_hbm, xr_s, xi_s, hr_s, hi_s, in_sems, out_sems = refs
        col = pl.multiple_of(pl.program_id(0) * SCAN_W, SCAN_W)
        loads = []
        for k in range(SCAN_SEGMENTS):
            rows = pl.ds(k * T, T)
            sources = [(x_hbm, col, xr_s), (x_hbm, NL + col, xi_s)]
            if reverse:
                sources += [(pr_hbm, col, pr_s), (pi_hbm, col, pi_s)]
            for i, (src, c0, dst) in enumerate(sources):
                loads.append(pltpu.make_async_copy(src.at[rows, pl.ds(c0, SCAN_W)], dst.at[:, k, :],
                                                   in_sems.at[i * SCAN_SEGMENTS + k]))
        for cp in loads:
            cp.start()
        for cp in loads:
            cp.wait()
        ar = ar_ref[...]
        ai = -ai_ref[...] if reverse else ai_ref[...]
        zero = jnp.zeros((SCAN_SEGMENTS, SCAN_W), f32)

        def at(s):
            return T - 1 - s if reverse else s

        def local(s, c):
            hr, hi, pr, pi = c
            j = at(s)
            nhr = ar * hr - ai * hi + xr_s[j]
            nhi = ar * hi + ai * hr + xi_s[j]
            hr_s[j] = nhr
            hi_s[j] = nhi
            return nhr, nhi, ar * pr - ai * pi, ar * pi + ai * pr

        er, ei, pr, pi = lax.fori_loop(0, T, local, (zero, zero, zero + 1.0, zero), unroll=2)
        row = lax.broadcasted_iota(jnp.int32, (SCAN_SEGMENTS, SCAN_W), 0)
        cr, ci = zero, zero
        order = range(SCAN_SEGMENTS - 2, -1, -1) if reverse else range(1, SCAN_SEGMENTS)
        for k in order:
            src = k + 1 if reverse else k - 1
            tr = er + pr * cr - pi * ci
            ti = ei + pr * ci + pi * cr
            cr = jnp.where(row == k, jnp.sum(jnp.where(row == src, tr, 0.0), axis=0, keepdims=True), cr)
            ci = jnp.where(row == k, jnp.sum(jnp.where(row == src, ti, 0.0), axis=0, keepdims=True), ci)

        def fix_at(j, c, before):
            pr, pi, sr, si = c
            pr, pi = ar * pr - ai * pi, ar * pi + ai * pr
            hr = hr_s[j] + pr * cr - pi * ci
            hi = hi_s[j] + pr * ci + pi * cr
            hr_s[j] = hr
            hi_s[j] = hi
            if reverse:
                qr, qi = before
                sr = sr + hr * qr + hi * qi
                si = si + hi * qr - hr * qi
            return pr, pi, sr, si

        start = (zero + 1.0, zero, zero, zero)
        if reverse:
            def fix(s, c):
                j = T - 1 - s
                return fix_at(j, c, (pr_s[j - 1], pi_s[j - 1]))

            c = lax.fori_loop(0, T - 1, fix, start, unroll=2)
            last_r = jnp.where(row == 0, 0.0, pltpu.roll(pr_s[T - 1], 1, 0))
            last_i = jnp.where(row == 0, 0.0, pltpu.roll(pi_s[T - 1], 1, 0))
            _, _, sr, si = fix_at(0, c, (last_r, last_i))
            dar_ref[...] = sr
            dai_ref[...] = si
        else:
            lax.fori_loop(0, T, lambda s, c: fix_at(s, c, None), start, unroll=2)
        stores = []
        for k in range(SCAN_SEGMENTS):
            rows = pl.ds(k * T, T)
            stores.append(pltpu.make_async_copy(hr_s.at[:, k, :], hr_hbm.at[rows, pl.ds(col, SCAN_W)], out_sems.at[k]))
            stores.append(pltpu.make_async_copy(hi_s.at[:, k, :], hi_hbm.at[rows, pl.ds(col, SCAN_W)],
                                                out_sems.at[SCAN_SEGMENTS + k]))
        for cp in stores:
            cp.start()
        for cp in stores:
            cp.wait()

    a_re = pl.BlockSpec((SCAN_SEGMENTS, SCAN_W), lambda b: (0, b))
    a_im = pl.BlockSpec((SCAN_SEGMENTS, SCAN_W), lambda b: (0, nblk + b))
    seq = pltpu.VMEM((T, SCAN_SEGMENTS, SCAN_W), f32)
    if reverse:
        in_specs, args = [ANY, ANY, ANY, a_re, a_im], [x, h[0], h[1], a_rows, a_rows]
        out_specs = [ANY, ANY, a_re, a_re]
        out_shape = [jax.ShapeDtypeStruct((S, NL), f32)] * 2 + [jax.ShapeDtypeStruct((SCAN_SEGMENTS, NL), f32)] * 2
    else:
        in_specs, args = [ANY, a_re, a_im], [x, a_rows, a_rows]
        out_specs = [ANY, ANY]
        out_shape = [jax.ShapeDtypeStruct((S, NL), f32)] * 2
    scratch = [seq] * (n_in + 2) + [pltpu.SemaphoreType.DMA((n_in * SCAN_SEGMENTS,)),
                                    pltpu.SemaphoreType.DMA((2 * SCAN_SEGMENTS,))]
    return pl.pallas_call(body, name=name, grid=(nblk,), in_specs=in_specs, out_specs=out_specs, out_shape=out_shape,
                          scratch_shapes=scratch, compiler_params=_cparams(('arbitrary',), big=True))(*args)


def _f_s5_post(y, u, d, w_glu):
    z = _bmm(y + d * u, w_glu)
    return (z[:, :GROUP_W] * jax.nn.sigmoid(z[:, GROUP_W:]),)


def _block_diag(t):
    G, a, b = t.shape
    eye = jnp.eye(G, dtype=t.dtype)
    return (t[:, :, None, :] * eye[:, None, :, None]).reshape(G * a, G * b)


def _diag_blocks(m, a, b):
    G = m.shape[0] // a
    return jnp.moveaxis(jnp.diagonal(m.reshape(G, a, G, b), axis1=0, axis2=2), -1, 0)


def _s5_fwd(u, mp, l):
    S = u.shape[0]
    tm = _pick(S, (256, 128))
    bre = _block_diag(mp['s5_b_re'].transpose(0, 2, 1))
    bim = _block_diag(mp['s5_b_im'].transpose(0, 2, 1))
    expand = jnp.repeat(jnp.eye(S5_G, dtype=f32), S5_P, axis=1)
    prep_params = [mp['s5_lambda_re'].reshape(1, S5_LANES), mp['s5_lambda_im'].reshape(1, S5_LANES),
                   mp['s5_log_dt'].reshape(S5_G, 1), expand]
    bb, a_rows = _tile_fwd(_f_s5_prep, [bre, bim], prep_params, [(2 * S5_LANES, f32)] * 2, GROUP_W, f's5_prep_fwd_{l}')
    x = _mm(u, bb, 'nn', f's5_in_fwd_{l}')
    hr, hi = _s5_scan(x, a_rows, f's5_scan_fwd_{l}')
    c_re, c_im = _block_diag(mp['s5_c_re'].transpose(0, 2, 1)), -_block_diag(mp['s5_c_im'].transpose(0, 2, 1))
    y = _mm(hi, c_im, 'nn', f's5_out_im_fwd_{l}', add=_mm(hr, c_re, 'nn', f's5_out_re_fwd_{l}'))
    post_params = [mp['s5_d'][None], mp['s5_w_glu']]
    (out,) = _tile_fwd(_f_s5_post, [y, u], post_params, [(GROUP_W, f32)], tm, f's5_post_fwd_{l}')
    return out, (u, bre, bim, prep_params, bb, a_rows, hr, hi, c_re, c_im, y, post_params)


def _s5_bwd(dout, saved, l):
    u, bre, bim, prep_params, bb, a_rows, hr, hi, c_re, c_im, y, post_params = saved
    S = u.shape[0]
    tm = _pick(S, (256, 128))
    (dy, du1), (dd, dwglu) = _tile_bwd(_f_s5_post, [y, u], post_params, [dout], [True, True], [True, True], tm,
                                       f's5_post_bwd_{l}')
    ccat = jnp.concatenate([c_re, c_im], axis=0)
    dh = _mm(dy, ccat, 'nt', f's5_out_dx_{l}')
    dccat = jnp.concatenate([_mm(hr, dy, 'tn', f's5_out_re_dw_{l}'), _mm(hi, dy, 'tn', f's5_out_im_dw_{l}')], axis=0)
    lr_, li_, dar, dai = _s5_scan(dh, a_rows, f's5_scan_bwd_{l}', reverse=True, h=(hr, hi))
    du2 = _mm(li_, bb[:, S5_LANES:], 'nt', f's5_in_im_dx_{l}', add=_mm(lr_, bb[:, :S5_LANES], 'nt', f's5_in_re_dx_{l}'))
    dbb = jnp.concatenate([_mm(u, lr_, 'tn', f's5_in_re_dw_{l}'), _mm(u, li_, 'tn', f's5_in_im_dw_{l}')], axis=1)
    da_rows = jnp.pad(jnp.concatenate([dar, dai], axis=1), ((0, GROUP_W - SCAN_SEGMENTS), (0, 0)))
    (dbre, dbim), (dlr, dli, dlogdt) = _tile_bwd(_f_s5_prep, [bre, bim], prep_params, [dbb, da_rows], [True, True],
                                                 [True, True, True, False], GROUP_W, f's5_prep_bwd_{l}')
    grads = {
        's5_lambda_re': dlr.reshape(S5_G, S5_P), 's5_lambda_im': dli.reshape(S5_G, S5_P), 's5_log_dt': dlogdt[:, 0],
        's5_b_re': _diag_blocks(dbre, S5_CG, S5_P).transpose(0, 2, 1),
        's5_b_im': _diag_blocks(dbim, S5_CG, S5_P).transpose(0, 2, 1),
        's5_c_re': _diag_blocks(dccat[:S5_LANES], S5_P, S5_CG).transpose(0, 2, 1),
        's5_c_im': -_diag_blocks(dccat[S5_LANES:], S5_P, S5_CG).transpose(0, 2, 1),
        's5_d': dd[0], 's5_w_glu': dwglu}
    return du1 + du2, grads


DN_CONV = 4


def _head_sum_matrix():
    h = np.arange(GROUP_W) // HEAD_DIM
    return jnp.asarray((h[:, None] == h[None, :]).astype(np.float32))


def _f_dn_pre(x0, x1, x2, x3, ab, w0, w1, w2, w3, alog, dtb, ea, eb, hs):
    c = w0 * x0 + w1 * x1 + w2 * x2 + w3 * x3
    s = c * jax.nn.sigmoid(c)
    q, k, v = s[:, :GROUP_W], s[:, GROUP_W:2 * GROUP_W], s[:, 2 * GROUP_W:]
    q = q * lax.rsqrt(_hdot(q * q, hs) + EPS) * (HEAD_DIM ** -0.5)
    k = k * lax.rsqrt(_hdot(k * k, hs) + EPS)
    beta = jax.nn.sigmoid(_hdot(ab, eb))
    g = -jnp.exp(alog) * jax.nn.softplus(_hdot(ab, ea) + dtb)
    return q, k, v, g, beta


DN_CHUNKS_PER_STEP = 4


def _f_dn_chunks(q, k, v, g, beta):
    C = DN_CHUNK
    n_chunks = q.shape[0] // C
    r = lax.broadcasted_iota(jnp.int32, (C, C), 0)
    c = lax.broadcasted_iota(jnp.int32, (C, C), 1)
    causal, strict = r >= c, r > c
    eye = (r == c).astype(f32)
    tril = causal.astype(f32)
    ones = jnp.ones((C, GROUP_W), f32)
    masks = [_head_mask(h) for h in range(N_HEADS)]
    rows = [tuple(t[i * C:(i + 1) * C] for t in (q, k, v, g, beta)) for i in range(n_chunks)]
    gcs = [_hdot(tril, gi) for (_, _, _, gi, _) in rows]
    items = [(i, h) for i in range(n_chunks) for h in range(N_HEADS)]
    grows = [_hdot_nt(ones * (masks[h] * (1.0 / HEAD_DIM)), gcs[i]) for i, h in items]
    decs = []
    for (i, h), grow in zip(items, grows):
        gcol = jnp.sum(gcs[i] * masks[h], axis=1, keepdims=True) * (1.0 / HEAD_DIM)
        decs.append(jnp.exp(jnp.where(causal, gcol - grow, NEG_INF)))
    kbs = [ki * bi for (_, ki, _, _, bi) in rows]
    kks = [_bmm_nt(kbs[i] * masks[h], rows[i][1]) for i, h in items]
    qks = [_bmm_nt(rows[i][0] * masks[h], rows[i][1]) for i, h in items]
    lmats = [jnp.where(strict, kk * dec, 0.0) for kk, dec in zip(kks, decs)]
    a_qk = [jnp.where(causal, qk * dec, 0.0) for qk, dec in zip(qks, decs)]
    ts = [eye - lm for lm in lmats]
    ps = lmats
    for _ in range(5):
        ps = [_bmm(p, p) for p in ps]
        ts = [t + _bmm(t, p) for t, p in zip(ts, ps)]
    egs = [jnp.exp(gc) for gc in gcs]
    tw = [_bmm(t, kbs[i] * egs[i]) for (i, h), t in zip(items, ts)]
    tu = [_bmm(t, rows[i][2] * rows[i][4]) for (i, h), t in zip(items, ts)]
    outs = []
    for i in range(n_chunks):
        qi, ki, _, gi, _ = rows[i]
        glast = jnp.sum(gi, axis=0, keepdims=True)
        w = sum(tw[i * N_HEADS + h] * masks[h] for h in range(N_HEADS))
        u = sum(tu[i * N_HEADS + h] * masks[h] for h in range(N_HEADS))
        outs.append((w, u, qi * egs[i], ki * jnp.exp(glast - gcs[i]), *a_qk[i * N_HEADS:(i + 1) * N_HEADS],
                     jnp.broadcast_to(jnp.exp(glast), (C, GROUP_W))))
    return tuple(jnp.concatenate(parts, axis=0) for parts in zip(*outs))


def _f_dn_step(w, u, qd, kdec, a0, a1, a2, a3, dfull, state, bd):
    row0 = (lax.broadcasted_iota(jnp.int32, dfull.shape, 0) == 0).astype(f32)
    dvec = jnp.sum(dfull * row0, axis=0, keepdims=True)
    ws, qs = _bmm(w, state), _bmm(qd, state)
    vnew = u - ws
    avs = [_bmm(a, vnew) for a in (a0, a1, a2, a3)]
    kv = _bmm_tn(kdec, vnew)
    o = qs + sum(av * _head_mask(h) for h, av in enumerate(avs))
    return o, state * dvec + bd * kv


def _dn_scan_fwd(ins, name):
    S = ins[0].shape[0]
    N = S // DN_CHUNK
    bd = _head_sum_matrix()

    def body(*refs):
        o_ref, s_ref, state = refs[10], refs[11], refs[12]

        @pl.when(pl.program_id(0) == 0)
        def _():
            state[...] = jnp.zeros_like(state)

        s_in = state[...]
        s_ref[0] = s_in
        o, s_out = _f_dn_step(*[r[...] for r in refs[:9]], s_in, refs[9][...])
        o_ref[...] = o
        state[...] = s_out

    return pl.pallas_call(
        body, name=name, grid=(N,),
        in_specs=[pl.BlockSpec((DN_CHUNK, t.shape[1]), lambda n: (n, 0)) for t in ins] + [_full_spec(bd)],
        out_specs=[pl.BlockSpec((DN_CHUNK, GROUP_W), lambda n: (n, 0)), pl.BlockSpec((1, GROUP_W, GROUP_W), lambda n: (n, 0, 0))],
        out_shape=[jax.ShapeDtypeStruct((S, GROUP_W), f32), jax.ShapeDtypeStruct((N, GROUP_W, GROUP_W), f32)],
        scratch_shapes=[pltpu.VMEM((GROUP_W, GROUP_W), f32)],
        compiler_params=_cparams(('arbitrary',)),
    )(*ins, bd)


def _dn_scan_bwd(ins, states, do, name):
    S = ins[0].shape[0]
    N = S // DN_CHUNK
    bd = _head_sum_matrix()

    def body(*refs):
        s_ref, do_ref = refs[9], refs[10]
        bd_ref = refs[11]
        outs = refs[12:21]
        dstate = refs[21]

        @pl.when(pl.program_id(0) == 0)
        def _():
            dstate[...] = jnp.zeros_like(dstate)

        bd_val = bd_ref[...]
        _, vjp = jax.vjp(lambda *a: _f_dn_step(*a, bd_val), *[r[...] for r in refs[:9]], s_ref[0])
        grads = vjp((do_ref[...], dstate[...]))
        for o, g in zip(outs, grads[:9]):
            o[...] = g
        dstate[...] = grads[9]

    def rev(n):
        return (N - 1 - n, 0)

    res = pl.pallas_call(
        body, name=name, grid=(N,),
        in_specs=[pl.BlockSpec((DN_CHUNK, t.shape[1]), rev) for t in ins] +
                 [pl.BlockSpec((1, GROUP_W, GROUP_W), lambda n: (N - 1 - n, 0, 0)), pl.BlockSpec((DN_CHUNK, GROUP_W), rev),
                  _full_spec(bd)],
        out_specs=[pl.BlockSpec((DN_CHUNK, t.shape[1]), rev) for t in ins],
        out_shape=[jax.ShapeDtypeStruct(t.shape, f32) for t in ins],
        scratch_shapes=[pltpu.VMEM((GROUP_W, GROUP_W), f32)],
        compiler_params=_cparams(('arbitrary',)),
    )(*ins, states, do, bd)
    return list(res)


def _f_dn_post(o, gate, gain, hmean):
    return (o * lax.rsqrt(_hdot(o * o, hmean) + EPS) * gain * (gate * jax.nn.sigmoid(gate)),)


def _delay(t, j):
    return t if j == 0 else jnp.pad(t[:-j], ((j, 0), (0, 0)))


def _advance(t, j):
    return t if j == 0 else jnp.pad(t[j:], ((0, j), (0, 0)))


def _dn_fwd(qkv, a, b, gate, mp, l):
    S = qkv.shape[0]
    tm = _pick(S, (256, 128))
    xs = [_delay(qkv, DN_CONV - 1 - j) for j in range(DN_CONV)]
    ab = jnp.pad(jnp.concatenate([a, b], axis=1), ((0, 0), (0, LANES - 2 * N_HEADS)))
    sel = np.zeros((2, LANES, GROUP_W), np.float32)
    for h in range(N_HEADS):
        sel[0, h, h * HEAD_DIM:(h + 1) * HEAD_DIM] = 1.0
        sel[1, N_HEADS + h, h * HEAD_DIM:(h + 1) * HEAD_DIM] = 1.0
    pre_params = [*[mp['dn_conv'][j][None] for j in range(DN_CONV)], jnp.repeat(mp['dn_a_log'], HEAD_DIM)[None],
                  jnp.repeat(mp['dn_dt_bias'], HEAD_DIM)[None], jnp.asarray(sel[0]), jnp.asarray(sel[1]), _head_sum_matrix()]
    pre = _tile_fwd(_f_dn_pre, [*xs, ab], pre_params, [(GROUP_W, f32)] * 5, tm, f'dn_pre_fwd_{l}')
    chunk_outs = [(GROUP_W, f32)] * 4 + [(HEAD_DIM, f32)] * 4 + [(GROUP_W, f32)]
    parts = _tile_fwd(_f_dn_chunks, pre, [], chunk_outs, DN_CHUNK * DN_CHUNKS_PER_STEP, f'dn_chunk_fwd_{l}')
    o, states = _dn_scan_fwd(parts, f'dn_scan_fwd_{l}')
    post_params = [jnp.tile(mp['dn_o_norm'], N_HEADS)[None], _head_mean_matrix()]
    (y,) = _tile_fwd(_f_dn_post, [o, gate], post_params, [(GROUP_W, f32)], tm, f'dn_post_fwd_{l}')
    return y, (xs, ab, pre_params, pre, parts, states, o, gate, post_params)


def _dn_bwd(dy, saved, l):
    xs, ab, pre_params, pre, parts, states, o, gate, post_params = saved
    S = dy.shape[0]
    tm = _pick(S, (256, 128))
    (do, dgate), (dgain,) = _tile_bwd(_f_dn_post, [o, gate], post_params, [dy], [True, True], [True, False], tm,
                                      f'dn_post_bwd_{l}')
    dparts = _dn_scan_bwd(parts, states, do, f'dn_scan_bwd_{l}')
    dpre, _ = _tile_bwd(_f_dn_chunks, pre, [], dparts, [True] * 5, [], DN_CHUNK * DN_CHUNKS_PER_STEP, f'dn_chunk_bwd_{l}')
    dins, dpar = _tile_bwd(_f_dn_pre, [*xs, ab], pre_params, dpre, [True] * 5, [True] * 6 + [False] * 3, tm,
                           f'dn_pre_bwd_{l}')
    dqkv = dins[DN_CONV - 1]
    for j in range(DN_CONV - 1):
        dqkv = dqkv + _advance(dins[j], DN_CONV - 1 - j)
    dab = dins[DN_CONV]
    grads = {'dn_conv': jnp.concatenate(dpar[:DN_CONV], axis=0),
             'dn_a_log': dpar[4].reshape(N_HEADS, HEAD_DIM).sum(1), 'dn_dt_bias': dpar[5].reshape(N_HEADS, HEAD_DIM).sum(1),
             'dn_o_norm': dgain.reshape(N_HEADS, HEAD_DIM).sum(0)}
    return dqkv, dab[:, :N_HEADS], dab[:, N_HEADS:2 * N_HEADS], dgate, grads


def _t5_bucket(dist):
    exact = T5_BUCKETS // 2
    df = jnp.maximum(dist, 1).astype(f32)
    large = exact + (jnp.log(df / exact) / math.log(T5_MAX_DIST / exact) * (T5_BUCKETS - exact)).astype(jnp.int32)
    large = jnp.minimum(large, T5_BUCKETS - 1)
    return jnp.where(dist < exact, dist, large)


def _split_cols(t, sizes):
    out, start = [], 0
    for s in sizes:
        out.append(t[..., start:start + s])
        start += s
    return out


def _mixers_fwd(proj, mp, l):
    c_q, c_kv, k_rope, u_s5, qkv_dil, qkv_dn, a_dn, b_dn, gate_dn = _split_cols(proj, IN_SPLITS)
    y_mla, s_mla = _mla_fwd(c_q, c_kv, k_rope, mp, l)
    y_s5, s_s5 = _s5_fwd(u_s5, mp, l)
    y_dil, s_dil = _dil_fwd(qkv_dil, mp, l)
    y_dn, s_dn = _dn_fwd(qkv_dn, a_dn, b_dn, gate_dn, mp, l)
    return jnp.concatenate([y_mla, y_s5, y_dil, y_dn], axis=-1), (s_mla, s_s5, s_dil, s_dn)


def _mixers_bwd(dmixed, saved, l):
    s_mla, s_s5, s_dil, s_dn = saved
    d_mla, d_s5, d_dil, d_dn = _split_cols(dmixed, (GROUP_W,) * 4)
    dc_q, dc_kv, dk_rope, g_mla = _mla_bwd(d_mla, s_mla, l)
    du, g_s5 = _s5_bwd(d_s5, s_s5, l)
    dqkv_dil, g_dil = _dil_bwd(d_dil, s_dil, l)
    dqkv_dn, da, db, dgate, g_dn = _dn_bwd(d_dn, s_dn, l)
    parts = [dc_q, dc_kv, dk_rope, du, dqkv_dil, dqkv_dn, da, db, dgate]
    dproj = jnp.concatenate([p.astype(bf16) for p in parts], axis=-1)
    return dproj, {**g_mla, **g_s5, **g_dil, **g_dn}


MIXER_PARAMS = ['mla_q_norm', 'mla_kv_norm', 'mla_w_uq', 'mla_w_ukv', 'mla_qk_q', 'mla_qk_k', 's5_lambda_re',
                's5_lambda_im', 's5_log_dt', 's5_b_re', 's5_b_im', 's5_c_re', 's5_c_im', 's5_d', 's5_w_glu',
                'dil_q_norm', 'dil_k_norm', 't5_bias', 'dn_conv', 'dn_a_log', 'dn_dt_bias', 'dn_o_norm']


def _layer_fwd(h, W, l):
    S = h.shape[0]
    tm = _pick(S, (256, 128))
    g1 = W['attn_norm'][l][None]
    g2 = W['ffn_norm'][l][None]
    (n1,) = _tile_fwd(_f_rms, [h], [g1], [(D_MODEL, bf16)], tm, f'rms1_fwd_{l}')
    proj = _mm(n1, W['w_in'][l], 'nn', f'proj_fwd_{l}')
    mp = {k: (W[k] if k == 't5_bias' else W[k][l]).astype(f32) for k in MIXER_PARAMS}
    mixed, mix_saved = _mixers_fwd(proj, mp, l)
    mixed_b = mixed.astype(bf16)
    h2 = _mm(mixed_b, W['w_out'][l], 'nn', f'out_fwd_{l}', add=h)
    (n2,) = _tile_fwd(_f_rms, [h2], [g2], [(D_MODEL, bf16)], tm, f'rms2_fwd_{l}')
    w13 = jnp.concatenate([W['ffn_w1'][l], W['ffn_w3'][l]], axis=1)
    uv = _mm(n2, w13, 'nn', f'ffn13_fwd_{l}')
    (act,) = _tile_fwd(_f_swiglu, [uv], [], [(FFN_HIDDEN, bf16)], tm, f'swiglu_fwd_{l}')
    h3 = _mm(act, W['ffn_w2'][l], 'nn', f'ffn2_fwd_{l}', add=h2)
    saved = dict(h=h, n1=n1, mix=mix_saved, mixed=mixed_b, h2=h2, n2=n2, uv=uv, act=act, w13=w13)
    return h3, saved


def _layer_bwd(dh3, saved, W, l):
    S = dh3.shape[0]
    tm = _pick(S, (256, 128))
    g1 = W['attn_norm'][l][None]
    g2 = W['ffn_norm'][l][None]
    grads = {}
    dact = _mm(dh3, W['ffn_w2'][l], 'nt', f'ffn2_dx_{l}')
    grads['ffn_w2'] = _mm(saved['act'], dh3, 'tn', f'ffn2_dw_{l}', out_dtype=bf16)
    (duv,), _ = _tile_bwd(_f_swiglu, [saved['uv']], [], [dact], [True], [], tm, f'swiglu_bwd_{l}', dt_dtypes=[bf16])
    dn2 = _mm(duv, saved['w13'], 'nt', f'ffn13_dx_{l}')
    dw13 = _mm(saved['n2'], duv, 'tn', f'ffn13_dw_{l}', out_dtype=bf16)
    grads['ffn_w1'], grads['ffn_w3'] = dw13[:, :FFN_HIDDEN], dw13[:, FFN_HIDDEN:]
    (dh2n,), (dg2,) = _tile_bwd(_f_rms, [saved['h2']], [g2], [dn2], [True], [True], tm, f'rms2_bwd_{l}')
    grads['ffn_norm'] = dg2[0]
    dh2 = dh3 + dh2n
    dmixed = _mm(dh2, W['w_out'][l], 'nt', f'out_dx_{l}')
    grads['w_out'] = _mm(saved['mixed'], dh2, 'tn', f'out_dw_{l}', out_dtype=bf16)
    dproj, dmp = _mixers_bwd(dmixed, saved['mix'], l)
    for k in MIXER_PARAMS:
        grads[k] = dmp[k]
    dn1 = _mm(dproj, W['w_in'][l], 'nt', f'proj_dx_{l}')
    grads['w_in'] = _mm(saved['n1'], dproj, 'tn', f'proj_dw_{l}', out_dtype=bf16)
    (dh1n,), (dg1,) = _tile_bwd(_f_rms, [saved['h']], [g1], [dn1], [True], [True], tm, f'rms1_bwd_{l}')
    grads['attn_norm'] = dg1[0]
    return dh2 + dh1n, grads


def kernel(x, attn_norm, w_in, w_out, mla_q_norm, mla_kv_norm, mla_w_uq, mla_w_ukv, mla_qk_q, mla_qk_k, s5_lambda_re, s5_lambda_im, s5_log_dt, s5_b_re, s5_b_im, s5_c_re, s5_c_im, s5_d, s5_w_glu, dil_q_norm, dil_k_norm, t5_bias, dn_conv, dn_a_log, dn_dt_bias, dn_o_norm, ffn_norm, ffn_w1, ffn_w3, ffn_w2, loss_target, m_attn_norm, m_w_in, m_w_out, m_mla_q_norm, m_mla_kv_norm, m_mla_w_uq, m_mla_w_ukv, m_mla_qk_q, m_mla_qk_k, m_s5_lambda_re, m_s5_lambda_im, m_s5_log_dt, m_s5_b_re, m_s5_b_im, m_s5_c_re, m_s5_c_im, m_s5_d, m_s5_w_glu, m_dil_q_norm, m_dil_k_norm, m_t5_bias, m_dn_conv, m_dn_a_log, m_dn_dt_bias, m_dn_o_norm, m_ffn_norm, m_ffn_w1, m_ffn_w3, m_ffn_w2, v_attn_norm, v_w_in, v_w_out, v_mla_q_norm, v_mla_kv_norm, v_mla_w_uq, v_mla_w_ukv, v_mla_qk_q, v_mla_qk_k, v_s5_lambda_re, v_s5_lambda_im, v_s5_log_dt, v_s5_b_re, v_s5_b_im, v_s5_c_re, v_s5_c_im, v_s5_d, v_s5_w_glu, v_dil_q_norm, v_dil_k_norm, v_t5_bias, v_dn_conv, v_dn_a_log, v_dn_dt_bias, v_dn_o_norm, v_ffn_norm, v_ffn_w1, v_ffn_w3, v_ffn_w2):
    given = dict(locals())
    w_loc = {n: given[n] for n in WEIGHTS}
    m_loc = {n: given['m_' + n] for n in WEIGHTS}
    v_loc = {n: given['v_' + n] for n in WEIGHTS}
    big_names = list(BIG)

    gathered = _gather_tensors([w_loc[n].astype(bf16) for n in big_names])
    W = {n: _from_shards(n, g) for n, g in zip(big_names, gathered)}
    for n in SMALL:
        W[n] = w_loc[n]

    h = x[0]
    saved = []
    for l in range(DEPTH):
        h, sv = _layer_fwd(h, W, l)
        saved.append(sv)
    parts_loss, dh = _loss_head(h, loss_target[0])
    loss = lax.psum(jnp.sum(parts_loss), ('x', 'y', 'c'))

    layer_grads = [None] * DEPTH
    for l in reversed(range(DEPTH)):
        dh, layer_grads[l] = _layer_bwd(dh, saved[l], W, l)
    grad_x = dh[None]
    small_full = []
    for n in SMALL:
        if n == 't5_bias':
            small_full.append(layer_grads[0][n] + layer_grads[1][n])
        else:
            small_full.append(jnp.stack([layer_grads[l][n] for l in range(DEPTH)]))

    gs = [jnp.stack([_by_shard(n, layer_grads[l][n]).astype(bf16) for l in range(DEPTH)], axis=1) for n in big_names]
    small_shapes = [w_loc[n].shape for n in SMALL]
    small_pack = _pack(small_full)
    from_sib, recv_small = _swap_with_sibling(gs, small_pack)
    chip_sums = [_chip_sum_of(g, r, 'chip_sum_' + n) for n, g, r in zip(big_names, gs, from_sib)]
    chip_small = _small_chip_sum(small_pack, recv_small)
    from_chips, from_chips_small = _exchange_between_chips(chip_sums, chip_small)
    totals = [_shard_total_of(g, r, rc, 'shard_total_' + n) for n, g, r, rc in zip(big_names, gs, from_sib, from_chips)]
    g_big = _join_with_sibling(totals)

    g_small_p, d_small_p, m_small_p, v_small_p = _small_update(
        small_pack, recv_small, from_chips_small, _pack([w_loc[n] for n in SMALL]),
        _pack([m_loc[n] for n in SMALL]), _pack([v_loc[n] for n in SMALL]))
    grad, delta, new_m, new_v = {}, {}, {}, {}
    for n, g_, d_, m_, v_ in zip(SMALL, _unpack(g_small_p, small_shapes), _unpack(d_small_p, small_shapes),
                                 _unpack(m_small_p, small_shapes), _unpack(v_small_p, small_shapes)):
        grad[n], delta[n], new_m[n], new_v[n] = g_, d_, m_, v_
    for n, g_ in zip(big_names, g_big):
        grad[n] = g_
        delta[n], new_m[n], new_v[n] = _adamw(w_loc[n], g_, m_loc[n], v_loc[n], 'adamw_' + n)
    return (loss, grad_x, *[grad[n] for n in WEIGHTS], *[delta[n] for n in WEIGHTS],
            *[new_m[n] for n in WEIGHTS], *[new_v[n] for n in WEIGHTS])
```

```python
import functools
import math

import numpy as np
import jax
import jax.numpy as jnp
from jax import lax
from jax.experimental import pallas as pl
from jax.experimental.pallas import tpu as pltpu

f32 = jnp.float32
bf16 = jnp.bfloat16
HI = lax.Precision.HIGHEST
MESH = pl.DeviceIdType.MESH

VMEM_LIMIT_BYTES = 48 * 1024 * 1024
MM_VMEM_BUDGET_BYTES = 32 * 1024 * 1024
LANES = 128

D_MODEL = 1024
DEPTH = 2
GROUP_W = 256
HEAD_DIM = 64
EPS = 1e-6
NEG_INF = -1e30
N_HEADS = 4
MLA_NOPE, MLA_ROPE = 64, 32
MLA_DQK = MLA_NOPE + MLA_ROPE
ROPE_THETA = 10000.0
Q_BLOCK = 128
S5_G, S5_CG, S5_P = 16, 16, 64
DIL_PAIRS = ((128, 1), (512, 4), (2048, 16))
T5_BUCKETS, T5_MAX_DIST = 32, 2048
DN_CHUNK = 64
FFN_HIDDEN = 2816
IN_SPLITS = (256, 128, 32, 256, 768, 768, 4, 4, 256)
IN_COLS = sum(IN_SPLITS)

ADAM_LR, ADAM_B1, ADAM_B2, ADAM_EPS, ADAM_WD, ADAM_STEP = 0.001, 0.9, 0.999, 1e-08, 0.01, 10

WEIGHTS = ['attn_norm', 'w_in', 'w_out', 'mla_q_norm', 'mla_kv_norm', 'mla_w_uq', 'mla_w_ukv', 'mla_qk_q', 'mla_qk_k',
           's5_lambda_re', 's5_lambda_im', 's5_log_dt', 's5_b_re', 's5_b_im', 's5_c_re', 's5_c_im', 's5_d', 's5_w_glu',
           'dil_q_norm', 'dil_k_norm', 't5_bias', 'dn_conv', 'dn_a_log', 'dn_dt_bias', 'dn_o_norm', 'ffn_norm',
           'ffn_w1', 'ffn_w3', 'ffn_w2']
BIG = {'w_in': 2, 'w_out': 1, 'mla_w_uq': 2, 'mla_w_ukv': 2, 's5_w_glu': 2, 'dn_conv': 2, 'ffn_w1': 2, 'ffn_w3': 2,
       'ffn_w2': 1}
SMALL = [n for n in WEIGHTS if n not in BIG]
GATHER_FIRST = ['w_in', 'mla_w_uq', 'mla_w_ukv', 's5_w_glu', 'dn_conv', 'w_out']
GATHER_FFN = ['ffn_w1', 'ffn_w3', 'ffn_w2']
N_SHARDS = 4
PACK_COLS = 1024


def _cparams(sem=None, big=False):
    kw = {}
    if sem is not None:
        kw['dimension_semantics'] = sem
    if big:
        kw['vmem_limit_bytes'] = VMEM_LIMIT_BYTES
    return pltpu.CompilerParams(**kw)


def _pick(n, prefs):
    for p in prefs:
        if p <= n and n % p == 0:
            return p
    return n


def _lane_tile(n, cap):
    for t in range(cap - cap % LANES, 0, -LANES):
        if n % t == 0:
            return t
    return n


def _mm(a, b, mode, name, add=None, out_dtype=f32):
    if mode == 'nn':
        (M, K), (K2, N) = a.shape, b.shape
    elif mode == 'nt':
        (M, K), (N, K2) = a.shape, b.shape
    else:
        (K, M), (K2, N) = a.shape, b.shape
    assert K == K2, (name, a.shape, b.shape)
    tk = K if K <= 2816 else _pick(K, (2816, 2048, 1408, 1024, 512))
    cap_m, cap_n = (1408 if mode == 'tn' else 512), 1408

    def need(tm_, tn_):
        per_step = tm_ * tk * a.dtype.itemsize + tk * tn_ * b.dtype.itemsize + tm_ * tn_ * jnp.dtype(out_dtype).itemsize
        if add is not None:
            per_step += tm_ * tn_ * add.dtype.itemsize
        return 2 * per_step + tm_ * tn_ * 4

    tm, tn = _lane_tile(M, cap_m), _lane_tile(N, cap_n)
    while need(tm, tn) > MM_VMEM_BUDGET_BYTES and cap_m > LANES:
        cap_m //= 2
        tm = _lane_tile(M, cap_m)
    nk = K // tk
    dims = {'nn': (((1,), (0,)), ((), ())), 'nt': (((1,), (1,)), ((), ())), 'tn': (((0,), (0,)), ((), ()))}[mode]
    has_add = add is not None

    def body(*refs):
        a_ref, b_ref = refs[0], refs[1]
        add_ref = refs[2] if has_add else None
        o_ref = refs[3] if has_add else refs[2]
        part = lax.dot_general(a_ref[...].astype(bf16), b_ref[...].astype(bf16), dims, preferred_element_type=f32)
        if nk == 1:
            if has_add:
                part = part + add_ref[...].astype(f32)
            o_ref[...] = part.astype(out_dtype)
        else:
            acc_ref = refs[-1]
            k = pl.program_id(2)

            @pl.when(k == 0)
            def _():
                acc_ref[...] = part

            @pl.when(k > 0)
            def _():
                acc_ref[...] += part

            @pl.when(k == nk - 1)
            def _():
                r = acc_ref[...]
                if has_add:
                    r = r + add_ref[...].astype(f32)
                o_ref[...] = r.astype(out_dtype)

    if mode == 'nn':
        a_spec = pl.BlockSpec((tm, tk), lambda i, j, k: (i, k))
        b_spec = pl.BlockSpec((tk, tn), lambda i, j, k: (k, j))
    elif mode == 'nt':
        a_spec = pl.BlockSpec((tm, tk), lambda i, j, k: (i, k))
        b_spec = pl.BlockSpec((tn, tk), lambda i, j, k: (j, k))
    else:
        a_spec = pl.BlockSpec((tk, tm), lambda i, j, k: (k, i))
        b_spec = pl.BlockSpec((tk, tn), lambda i, j, k: (k, j))
    in_specs = [a_spec, b_spec]
    args = [a, b]
    if has_add:
        in_specs.append(pl.BlockSpec((tm, tn), lambda i, j, k: (i, j)))
        args.append(add)
    return pl.pallas_call(
        body, name=name, grid=(M // tm, N // tn, nk), in_specs=in_specs,
        out_specs=pl.BlockSpec((tm, tn), lambda i, j, k: (i, j)),
        out_shape=jax.ShapeDtypeStruct((M, N), out_dtype),
        scratch_shapes=[pltpu.VMEM((tm, tn), f32)] if nk > 1 else [],
        compiler_params=_cparams(('parallel', 'parallel', 'arbitrary'), big=True),
    )(*args)


def _full_spec(p):
    nd = p.ndim
    return pl.BlockSpec(p.shape, lambda i, _nd=nd: (0,) * _nd)


def _tile_fwd(f, tiled, params, outs, tm, name):
    S = tiled[0].shape[0]
    nt, npar = len(tiled), len(params)

    def body(*refs):
        vals = [r[...].astype(f32) for r in refs[:nt + npar]]
        res = f(*vals)
        for r, o in zip(res, refs[nt + npar:]):
            o[...] = r.astype(o.dtype)

    return pl.pallas_call(
        body, name=name, grid=(S // tm,),
        in_specs=[pl.BlockSpec((tm, t.shape[1]), lambda i: (i, 0)) for t in tiled] + [_full_spec(p) for p in params],
        out_specs=[pl.BlockSpec((tm, c), lambda i: (i, 0)) for c, _ in outs],
        out_shape=[jax.ShapeDtypeStruct((S, c), dt) for c, dt in outs],
        compiler_params=_cparams(('parallel',), big=True),
    )(*tiled, *params)


def _tile_bwd(f, tiled, params, cts, diff_t, diff_p, tm, name, dt_dtypes=None):
    S = tiled[0].shape[0]
    nt, npar, nc = len(tiled), len(params), len(cts)
    it = [i for i in range(nt) if diff_t[i]]
    ip = [i for i in range(npar) if diff_p[i]]
    if dt_dtypes is None:
        dt_dtypes = [f32] * len(it)

    def body(*refs):
        vals = [r[...].astype(f32) for r in refs[:nt + npar]]
        ct_vals = tuple(r[...].astype(f32) for r in refs[nt + npar:nt + npar + nc])
        out_refs = refs[nt + npar + nc:]

        def g(*dv):
            full = list(vals)
            for k, i in enumerate(it):
                full[i] = dv[k]
            for k, i in enumerate(ip):
                full[nt + i] = dv[len(it) + k]
            return tuple(f(*full))

        _, vjp = jax.vjp(g, *[vals[i] for i in it], *[vals[nt + i] for i in ip])
        grads = vjp(ct_vals)
        for k in range(len(it)):
            out_refs[k][...] = grads[k].astype(out_refs[k].dtype)
        step = pl.program_id(0)
        for k in range(len(ip)):
            o = out_refs[len(it) + k]
            gk = grads[len(it) + k]

            @pl.when(step == 0)
            def _(o=o, gk=gk):
                o[...] = gk

            @pl.when(step > 0)
            def _(o=o, gk=gk):
                o[...] += gk

    out_specs = [pl.BlockSpec((tm, tiled[i].shape[1]), lambda i_: (i_, 0)) for i in it] + [_full_spec(params[i]) for i in ip]
    out_shape = [jax.ShapeDtypeStruct(tiled[i].shape, dt_dtypes[k]) for k, i in enumerate(it)] + \
                [jax.ShapeDtypeStruct(params[i].shape, f32) for i in ip]
    res = pl.pallas_call(
        body, name=name, grid=(S // tm,),
        in_specs=[pl.BlockSpec((tm, t.shape[1]), lambda i: (i, 0)) for t in tiled] + [_full_spec(p) for p in params] +
                 [pl.BlockSpec((tm, c.shape[1]), lambda i: (i, 0)) for c in cts],
        out_specs=out_specs, out_shape=out_shape,
        compiler_params=_cparams(('arbitrary',), big=True),
    )(*tiled, *params, *cts)
    return list(res[:len(it)]), list(res[len(it):])


def _rms(x, g):
    return x * lax.rsqrt(jnp.mean(x * x, axis=-1, keepdims=True) + EPS) * g


def _f_rms(x, g):
    return (_rms(x, g),)


def _f_swiglu(uv):
    h = uv.shape[1] // 2
    u, v = uv[:, :h], uv[:, h:]
    return (u * jax.nn.sigmoid(u) * v,)


def _loss_head(y, target):
    S, D = y.shape
    tm = _pick(S, (256, 128))

    def body(y_ref, t_ref, part_ref, dy_ref):
        e = y_ref[...] - t_ref[...]
        dy_ref[...] = e * (1.0 / D)
        s = 0.5 * jnp.sum(jnp.sum(e * e, axis=1, keepdims=True), axis=0, keepdims=True) * (1.0 / D)
        r = lax.broadcasted_iota(jnp.int32, (8, LANES), 0)
        c = lax.broadcasted_iota(jnp.int32, (8, LANES), 1)
        part_ref[0] = jnp.where((r == 0) & (c == 0), s, 0.0)

    return pl.pallas_call(
        body, name='loss_head', grid=(S // tm,),
        in_specs=[pl.BlockSpec((tm, D), lambda i: (i, 0))] * 2,
        out_specs=[pl.BlockSpec((1, 8, LANES), lambda i: (i, 0, 0)), pl.BlockSpec((tm, D), lambda i: (i, 0))],
        out_shape=[jax.ShapeDtypeStruct((S // tm, 8, LANES), f32), jax.ShapeDtypeStruct((S, D), f32)],
        compiler_params=_cparams(('parallel',)),
    )(y, target)


def _pack_rows_of(shape):
    rows = -(-math.prod(shape) // PACK_COLS)
    return -(-rows // 8) * 8


def _pack(arrs):
    parts = []
    for a in arrs:
        rows = _pack_rows_of(a.shape)
        flat = a.astype(f32).reshape(-1)
        parts.append(jnp.pad(flat, (0, rows * PACK_COLS - flat.shape[0])).reshape(rows, PACK_COLS))
    return jnp.concatenate(parts, axis=0)


def _unpack(pack, shapes):
    out, row = [], 0
    for s in shapes:
        rows = _pack_rows_of(s)
        out.append(pack[row:row + rows].reshape(-1)[:math.prod(s)].reshape(s))
        row += rows
    return out


ANY = pl.BlockSpec(memory_space=pl.ANY)


def _place():
    return lax.axis_index('x'), lax.axis_index('y'), lax.axis_index('c')


def _where():
    return jnp.stack([lax.axis_index('c'), 2 * lax.axis_index('x') + lax.axis_index('y')]).astype(jnp.int32)


def _remote(src, dst, send_sems, recv_sems, k, to):
    return pltpu.make_async_remote_copy(src_ref=src, dst_ref=dst, send_sem=send_sems.at[k], recv_sem=recv_sems.at[k],
                                        device_id=to, device_id_type=MESH)


def _gather_tensors(ws):
    n = len(ws)

    def body(*refs):
        w_refs, g_refs = refs[:n], refs[n:2 * n]
        send_sems, recv_sems = refs[2 * n:]
        x, y, c = _place()
        me, sib = (x, y, c), (x, y, 1 - c)
        chips = [(1 - x, y), (x, 1 - y), (1 - x, 1 - y)]
        first = [_remote(w_refs[t].at[c], g_refs[t].at[2 * x + y, c], send_sems, recv_sems, 6 * t + j, (px, py, c))
                 for j, (px, py) in enumerate(chips) for t in range(n)]
        for cp in first:
            cp.start()
        passed = []
        for j, (px, py) in enumerate(chips):
            for t in range(n):
                here = g_refs[t].at[2 * px + py, c]
                _remote(here, here, send_sems, recv_sems, 6 * t + j, me).wait_recv()
                cp = _remote(here, here, send_sems, recv_sems, 6 * t + 3 + j, sib)
                cp.start()
                passed.append(cp)
        for j, (px, py) in enumerate(chips):
            for t in range(n):
                there = g_refs[t].at[2 * px + py, 1 - c]
                _remote(there, there, send_sems, recv_sems, 6 * t + 3 + j, me).wait_recv()
        for cp in first + passed:
            cp.wait_send()

    own = 2 * lax.axis_index('x') + lax.axis_index('y')
    res = pl.pallas_call(
        body, name='gather_weights', in_specs=[ANY] * n, out_specs=[ANY] * n,
        out_shape=[jax.ShapeDtypeStruct((N_SHARDS,) + w.shape, w.dtype) for w in ws],
        scratch_shapes=[pltpu.SemaphoreType.DMA((6 * n,)), pltpu.SemaphoreType.DMA((6 * n,))],
    )(*ws)
    return [lax.dynamic_update_slice(g, w[None], (own, 0, 0, 0)) for g, w in zip(res, ws)]


def _swap_with_sibling(gs, small):
    n = len(gs)

    def body(*refs):
        g_refs, s_ref = refs[:n], refs[n]
        r_refs, rs_ref = refs[n + 1:2 * n + 1], refs[2 * n + 1]
        send_sems, recv_sems = refs[2 * n + 2:]
        x, y, c = _place()
        sib = (x, y, 1 - c)
        cps = [_remote(g_refs[t].at[:, 1 - c], r_refs[t], send_sems, recv_sems, t, sib) for t in range(n)]
        cps.append(_remote(s_ref, rs_ref, send_sems, recv_sems, n, sib))
        for cp in cps:
            cp.start()
        for cp in cps:
            cp.wait()

    res = pl.pallas_call(
        body, name='swap_with_sibling', in_specs=[ANY] * (n + 1), out_specs=[ANY] * (n + 1),
        out_shape=[jax.ShapeDtypeStruct((N_SHARDS,) + g.shape[2:], g.dtype) for g in gs] +
                  [jax.ShapeDtypeStruct(small.shape, small.dtype)],
        scratch_shapes=[pltpu.SemaphoreType.DMA((n + 1,)), pltpu.SemaphoreType.DMA((n + 1,))],
    )(*gs, small)
    return list(res[:n]), res[n]


def _exchange_between_chips(cs, small):
    n = len(cs)

    def body(*refs):
        c_refs, s_ref = refs[:n], refs[n]
        r_refs, rs_ref = refs[n + 1:2 * n + 1], refs[2 * n + 1]
        send_sems, recv_sems = refs[2 * n + 2:]
        x, y, c = _place()
        chips = [(1 - x, y), (x, 1 - y), (1 - x, 1 - y)]
        cps = []
        for j, (px, py) in enumerate(chips):
            for t in range(n):
                cps.append(_remote(c_refs[t].at[2 * px + py], r_refs[t].at[j], send_sems, recv_sems, 3 * t + j, (px, py, c)))
            cps.append(_remote(s_ref, rs_ref.at[j], send_sems, recv_sems, 3 * n + j, (px, py, c)))
        for cp in cps:
            cp.start()
        for cp in cps:
            cp.wait()

    res = pl.pallas_call(
        body, name='exchange_between_chips', in_specs=[ANY] * (n + 1), out_specs=[ANY] * (n + 1),
        out_shape=[jax.ShapeDtypeStruct((3,) + c.shape[1:], c.dtype) for c in cs] +
                  [jax.ShapeDtypeStruct((3,) + small.shape, small.dtype)],
        scratch_shapes=[pltpu.SemaphoreType.DMA((3 * n + 3,)), pltpu.SemaphoreType.DMA((3 * n + 3,))],
    )(*cs, small)
    return list(res[:n]), res[n]


def _join_with_sibling(ts):
    n = len(ts)

    def body(*refs):
        t_refs, o_refs = refs[:n], refs[n:2 * n]
        send_sems, recv_sems = refs[2 * n:]
        x, y, c = _place()
        cps = [_remote(t_refs[t], o_refs[t], send_sems, recv_sems, t, (x, y, 1 - c)) for t in range(n)]
        for cp in cps:
            cp.start()
        for cp in cps:
            cp.wait()

    theirs = pl.pallas_call(
        body, name='join_with_sibling', in_specs=[ANY] * n, out_specs=[ANY] * n,
        out_shape=[jax.ShapeDtypeStruct(t.shape, t.dtype) for t in ts],
        scratch_shapes=[pltpu.SemaphoreType.DMA((n,)), pltpu.SemaphoreType.DMA((n,))],
    )(*ts)
    south = lax.axis_index('c') == 0
    return [jnp.stack([jnp.where(south, mine, other), jnp.where(south, other, mine)]) for mine, other in zip(ts, theirs)]


HBM = pl.BlockSpec(memory_space=pltpu.HBM)
SEM = pl.BlockSpec(memory_space=pltpu.SEMAPHORE)
DATAFLOW = pltpu.SideEffectType.DATAFLOW_SIDE_EFFECTING


def _in_hbm(t):
    return pltpu.with_memory_space_constraint(t, pltpu.HBM)


def _other_chips():
    x, y, c = _place()
    return [(1 - x, y, c), (x, 1 - y, c), (1 - x, 1 - y, c)]


def _to_chips_copies(src_refs, land_refs, send_sems, recv_sems, per_peer):
    x, y, _ = _place()
    cps = []
    for t, (src, land) in enumerate(zip(src_refs, land_refs)):
        for j, (px, py, pc) in enumerate(_other_chips()):
            s = src.at[2 * px + py] if per_peer else src
            d = land.at[j] if per_peer else land.at[2 * x + y]
            cps.append(_remote(s, d, send_sems, recv_sems, 3 * t + j, (px, py, pc)))
    return cps


def _to_chips_start(srcs, lands, per_peer, order, name):
    n = len(srcs)

    def body(*refs):
        src_refs, land_refs = refs[:n], refs[n:2 * n]
        send_sems, recv_sems = refs[2 * n + 1], refs[2 * n + 2]
        token = refs[-1]
        for cp in _to_chips_copies(src_refs, land_refs, send_sems, recv_sems, per_peer):
            cp.start()
        token[...] = jnp.zeros_like(token)

    res = pl.pallas_call(
        body, name=name, in_specs=[HBM] * (2 * n) + [ANY],
        out_specs=[SEM, SEM] + [HBM] * (2 * n) + [pl.BlockSpec(memory_space=pltpu.VMEM)],
        out_shape=[pltpu.SemaphoreType.DMA((3 * n,)), pltpu.SemaphoreType.DMA((3 * n,))] +
                  [pltpu.HBM(t.shape, t.dtype) for t in srcs] + [pltpu.HBM(t.shape, t.dtype) for t in lands] +
                  [jax.ShapeDtypeStruct((8, LANES), f32)],
        input_output_aliases={i: 2 + i for i in range(2 * n)},
        compiler_params=pltpu.CompilerParams(has_side_effects=DATAFLOW),
    )(*[_in_hbm(t) for t in srcs], *[_in_hbm(t) for t in lands], order)
    return res[0], res[1], list(res[2:2 + n]), list(res[2 + n:2 + 2 * n]), res[-1]


def _to_chips_wait(send_sems, recv_sems, srcs, lands, per_peer, after, name):
    n = len(srcs)

    def body(*refs):
        src_refs, land_refs = refs[:n], refs[n:2 * n]
        send_ref, recv_ref = refs[2 * n], refs[2 * n + 1]
        for cp in _to_chips_copies(src_refs, land_refs, send_ref, recv_ref, per_peer):
            cp.wait_send()
            cp.wait_recv()

    res = pl.pallas_call(
        body, name=name, in_specs=[HBM] * (2 * n) + [SEM, SEM, ANY],
        out_specs=[HBM] * (2 * n),
        out_shape=[pltpu.HBM(t.shape, t.dtype) for t in srcs] + [pltpu.HBM(t.shape, t.dtype) for t in lands],
        input_output_aliases={i: i for i in range(2 * n)},
        compiler_params=pltpu.CompilerParams(has_side_effects=DATAFLOW),
    )(*srcs, *lands, send_sems, recv_sems, after)
    return list(res[n:])


def _row_tile(a):
    return _pick(a, (512, 256, 128, 64, 32, 16, 8))


def _chip_sum_of(g, r, name):
    _, _, a, b = g.shape
    tr = _row_tile(a)

    def body(w_ref, g_ref, r_ref, o_ref):
        o_ref[...] = (g_ref[0].astype(f32) + r_ref[...].astype(f32)).astype(o_ref.dtype)

    return pl.pallas_call(
        body, name=name,
        grid_spec=pltpu.PrefetchScalarGridSpec(
            num_scalar_prefetch=1, grid=(N_SHARDS, a // tr),
            in_specs=[pl.BlockSpec((1, 1, tr, b), lambda s, i, w: (s, w[0], i, 0)),
                      pl.BlockSpec((1, tr, b), lambda s, i, w: (s, i, 0))],
            out_specs=pl.BlockSpec((1, tr, b), lambda s, i, w: (s, i, 0))),
        out_shape=jax.ShapeDtypeStruct((N_SHARDS, a, b), bf16),
        compiler_params=_cparams(('parallel', 'parallel')),
    )(_where(), g, r)


def _shard_total_of(g, r, rc, name):
    _, _, a, b = g.shape
    tr = _row_tile(a)

    def body(w_ref, g_ref, r_ref, rc_ref, o_ref):
        t = g_ref[0, 0].astype(f32) + r_ref[0].astype(f32)
        t = t + rc_ref[0].astype(f32)
        t = t + rc_ref[1].astype(f32)
        t = t + rc_ref[2].astype(f32)
        o_ref[...] = t

    return pl.pallas_call(
        body, name=name,
        grid_spec=pltpu.PrefetchScalarGridSpec(
            num_scalar_prefetch=1, grid=(a // tr,),
            in_specs=[pl.BlockSpec((1, 1, tr, b), lambda i, w: (w[1], w[0], i, 0)),
                      pl.BlockSpec((1, tr, b), lambda i, w: (w[1], i, 0)),
                      pl.BlockSpec((3, tr, b), lambda i, w: (0, i, 0))],
            out_specs=pl.BlockSpec((tr, b), lambda i, w: (i, 0))),
        out_shape=jax.ShapeDtypeStruct((a, b), f32),
        compiler_params=_cparams(('parallel',)),
    )(_where(), g, r, rc)


def _by_shard(name, t):
    r, c = t.shape
    if BIG[name] == 2:
        return t.reshape(r, N_SHARDS, c // N_SHARDS).transpose(1, 0, 2)
    return t.reshape(N_SHARDS, r // N_SHARDS, c)


def _from_shards(name, g):
    s, a, b = g.shape
    if BIG[name] == 2:
        return g.transpose(1, 0, 2).reshape(a, s * b)
    return g.reshape(s * a, b)


def _adam_math(w, g, m, v):
    m = ADAM_B1 * m + (1.0 - ADAM_B1) * g
    v = ADAM_B2 * v + (1.0 - ADAM_B2) * (g * g)
    m_hat = m / (1.0 - ADAM_B1 ** ADAM_STEP)
    v_hat = v / (1.0 - ADAM_B2 ** ADAM_STEP)
    delta = -ADAM_LR * (m_hat / (jnp.sqrt(v_hat) + ADAM_EPS) + ADAM_WD * w)
    return delta, m, v


def _small_update(own, sib, chips, w, m, v):
    def body(o_ref, s_ref, c_ref, w_ref, m_ref, v_ref, g_out, d_out, m_out, v_out):
        chip = o_ref[...] + s_ref[...]
        g = (chip + c_ref[0]) + (c_ref[1] + c_ref[2])
        d, mn, vn = _adam_math(w_ref[...], g, m_ref[...], v_ref[...])
        g_out[...] = g
        d_out[...] = d
        m_out[...] = mn
        v_out[...] = vn

    return pl.pallas_call(body, name='small_update', out_shape=[jax.ShapeDtypeStruct(own.shape, f32)] * 4)(
        own, sib, chips, w, m, v)


def _small_chip_sum(own, sib):
    def body(o_ref, s_ref, out):
        out[...] = o_ref[...] + s_ref[...]
    return pl.pallas_call(body, name='small_chip_sum', out_shape=jax.ShapeDtypeStruct(own.shape, f32))(own, sib)


def _adamw(w, g, m, v, name):
    shape = w.shape
    w2, g2, m2, v2 = [t.reshape(-1, shape[-1]) for t in (w, g, m, v)]
    rows, cols = w2.shape
    tr = _pick(rows, (256, 128, 64, 32, 16, 8))

    def body(w_ref, g_ref, m_ref, v_ref, d_out, m_out, v_out):
        d, mn, vn = _adam_math(w_ref[...], g_ref[...], m_ref[...], v_ref[...])
        d_out[...] = d
        m_out[...] = mn
        v_out[...] = vn

    spec = pl.BlockSpec((tr, cols), lambda i: (i, 0))
    res = pl.pallas_call(body, name=name, grid=(rows // tr,), in_specs=[spec] * 4, out_specs=[spec] * 3,
                         out_shape=[jax.ShapeDtypeStruct((rows, cols), f32)] * 3,
                         compiler_params=_cparams(('parallel',)))(w2, g2, m2, v2)
    return [r.reshape(shape) for r in res]


def _dg(a, b, ca, cb):
    return lax.dot_general(a.astype(bf16), b.astype(bf16), (((ca,), (cb,)), ((), ())), preferred_element_type=f32)


@jax.custom_vjp
def _bmm(a, b):
    return _dg(a, b, 1, 0)


_bmm.defvjp(lambda a, b: (_dg(a, b, 1, 0), (a, b)), lambda r, g: (_dg(g, r[1], 1, 1), _dg(r[0], g, 0, 0)))


@jax.custom_vjp
def _bmm_nt(a, b):
    return _dg(a, b, 1, 1)


_bmm_nt.defvjp(lambda a, b: (_dg(a, b, 1, 1), (a, b)), lambda r, g: (_dg(g, r[1], 1, 0), _dg(g, r[0], 0, 0)))


@jax.custom_vjp
def _bmm_tn(a, b):
    return _dg(a, b, 0, 0)


_bmm_tn.defvjp(lambda a, b: (_dg(a, b, 0, 0), (a, b)), lambda r, g: (_dg(r[1], g, 1, 1), _dg(r[0], g, 1, 0)))


def _hdot(a, b):
    return jnp.dot(a, b, precision=HI, preferred_element_type=f32)


def _hdot_nt(a, b):
    return lax.dot_general(a, b, (((1,), (1,)), ((), ())), precision=HI, preferred_element_type=f32)


def _hdot_tn(a, b):
    return lax.dot_general(a, b, (((0,), (0,)), ((), ())), precision=HI, preferred_element_type=f32)


def _head_mask(h, width=GROUP_W):
    lane = lax.broadcasted_iota(jnp.int32, (1, width), 1)
    return ((lane >= h * HEAD_DIM) & (lane < (h + 1) * HEAD_DIM)).astype(f32)


def _rope_perm():
    p = np.zeros((LANES, LANES), np.float32)
    half = MLA_ROPE // 2
    for i in range(half):
        p[MLA_NOPE + half + i, MLA_NOPE + i] = -1.0
        p[MLA_NOPE + i, MLA_NOPE + half + i] = 1.0
    return jnp.asarray(p)


def _rope_tables(S):
    half = MLA_ROPE // 2
    freqs = ROPE_THETA ** (-jnp.arange(half, dtype=f32) / half)
    ang = jnp.arange(S, dtype=f32)[:, None] * freqs[None, :]
    cos, sin = jnp.cos(ang), jnp.sin(ang)
    ones, zeros = jnp.ones((S, MLA_NOPE), f32), jnp.zeros((S, LANES - MLA_DQK), f32)
    c_tab = jnp.concatenate([ones, cos, cos, zeros], axis=1)
    s_tab = jnp.concatenate([jnp.zeros((S, MLA_NOPE), f32), sin, sin, zeros], axis=1)
    return c_tab, s_tab


def _f_mla_pre(c_q, c_kv, krope, c_tab, s_tab, q_norm, kv_norm, wq0, wq1, wq2, wq3, wk0, wk1, wk2, wk3, wv, gq, gk, perm):
    wq, wk = (wq0, wq1, wq2, wq3), (wk0, wk1, wk2, wk3)
    nq = _rms(c_q, q_norm)
    nkv = _rms(c_kv, kv_norm)

    def norm_rope(t, g):
        t = t * lax.rsqrt(jnp.sum(t * t, axis=-1, keepdims=True) * (1.0 / MLA_DQK) + EPS) * g
        return t * c_tab + _hdot(t, perm) * s_tab

    qs = [norm_rope(_bmm(nq, wq[h]), gq) * (MLA_DQK ** -0.5) for h in range(N_HEADS)]
    ks = [norm_rope(_bmm(nkv, wk[h]) + krope, gk) for h in range(N_HEADS)]
    return (*qs, *ks, _bmm(nkv, wv))


def _f_attn(qs, ks, v, q0):
    tq, S = qs[0].shape[0], ks[0].shape[0]
    qpos = q0 + lax.broadcasted_iota(jnp.int32, (tq, S), 0)
    kpos = lax.broadcasted_iota(jnp.int32, (tq, S), 1)
    keep = kpos <= qpos
    logits = [jnp.where(keep, _bmm_nt(qs[h], ks[h]), NEG_INF) for h in range(N_HEADS)]
    ps = [jnp.exp(lg - jnp.max(lg, axis=-1, keepdims=True)) for lg in logits]
    ps = [p / jnp.sum(p, axis=-1, keepdims=True) for p in ps]
    return sum(_bmm(p, v) * _head_mask(h) for h, p in enumerate(ps))


ATTN_PARTS = 4


def _mla_attn_fwd(qs, ks, v, name):
    S = v.shape[0]
    tq = Q_BLOCK
    parts = ATTN_PARTS if S % (ATTN_PARTS * tq) == 0 else 1
    per = S // parts
    outs = []
    for p in range(parts):
        n_keys = (p + 1) * per
        first_block = p * (per // tq)

        def body(*refs, first_block=first_block):
            q_vals = [r[...] for r in refs[:4]]
            k_vals = [r[...] for r in refs[4:8]]
            refs[9][...] = _f_attn(q_vals, k_vals, refs[8][...], (first_block + pl.program_id(0)) * tq)

        qspec = pl.BlockSpec((tq, LANES), lambda i, fb=first_block: (fb + i, 0))
        outs.append(pl.pallas_call(
            body, name=f'{name}_{p}', grid=(per // tq,),
            in_specs=[qspec] * 4 + [pl.BlockSpec((n_keys, LANES), lambda i: (0, 0))] * 4 +
                     [pl.BlockSpec((n_keys, GROUP_W), lambda i: (0, 0))],
            out_specs=pl.BlockSpec((tq, GROUP_W), lambda i: (i, 0)),
            out_shape=jax.ShapeDtypeStruct((per, GROUP_W), f32),
            compiler_params=_cparams(('parallel',), big=True),
        )(*qs, *ks, v))
    return jnp.concatenate(outs, axis=0)


def _mla_attn_bwd(qs, ks, v, do, name):
    S = v.shape[0]
    tq = Q_BLOCK
    parts = ATTN_PARTS if S % (ATTN_PARTS * tq) == 0 else 1
    per = S // parts
    dq_parts, dkv_sum = [], None
    for p in range(parts):
        n_keys = (p + 1) * per
        first_block = p * (per // tq)

        def body(*refs, first_block=first_block):
            q_vals = [r[...].astype(f32) for r in refs[:4]]
            k_vals = [r[...].astype(f32) for r in refs[4:8]]
            v_val = refs[8][...].astype(f32)
            q0 = (first_block + pl.program_id(0)) * tq
            _, vjp = jax.vjp(lambda a, b, c: _f_attn(a, b, c, q0), q_vals, k_vals, v_val)
            dqs, dks, dv = vjp(refs[9][...])
            outs = refs[10:]
            for h in range(N_HEADS):
                outs[h][...] = dqs[h]
            first = pl.program_id(0) == 0
            for o, g in zip(outs[4:], (*dks, dv)):
                @pl.when(first)
                def _(o=o, g=g):
                    o[...] = g

                @pl.when(jnp.logical_not(first))
                def _(o=o, g=g):
                    o[...] += g

        qspec = pl.BlockSpec((tq, LANES), lambda i, fb=first_block: (fb + i, 0))
        kspec = pl.BlockSpec((n_keys, LANES), lambda i: (0, 0))
        vspec = pl.BlockSpec((n_keys, GROUP_W), lambda i: (0, 0))
        res = pl.pallas_call(
            body, name=f'{name}_{p}', grid=(per // tq,),
            in_specs=[qspec] * 4 + [kspec] * 4 + [vspec, pl.BlockSpec((tq, GROUP_W), lambda i, fb=first_block: (fb + i, 0))],
            out_specs=[pl.BlockSpec((tq, LANES), lambda i: (i, 0))] * 4 + [kspec] * 4 + [vspec],
            out_shape=[jax.ShapeDtypeStruct((per, LANES), f32)] * 4 + [jax.ShapeDtypeStruct((n_keys, LANES), f32)] * 4 +
                      [jax.ShapeDtypeStruct((n_keys, GROUP_W), f32)],
            compiler_params=_cparams(('arbitrary',), big=True),
        )(*qs, *ks, v, do)
        dq_parts.append(res[:4])
        dkv = [jnp.pad(t, ((0, S - n_keys), (0, 0))) for t in res[4:]]
        dkv_sum = dkv if dkv_sum is None else [a_ + b_ for a_, b_ in zip(dkv_sum, dkv)]
    dqs = [jnp.concatenate([dq_parts[p][h] for p in range(parts)], axis=0) for h in range(N_HEADS)]
    return dqs, dkv_sum[:4], dkv_sum[4]


def _mla_params(mp):
    pad = LANES - MLA_DQK
    wq = jnp.pad(mp['mla_w_uq'].reshape(GROUP_W, N_HEADS, MLA_DQK).transpose(1, 0, 2), ((0, 0), (0, 0), (0, pad)))
    wkv = mp['mla_w_ukv'].reshape(LANES, N_HEADS, MLA_NOPE + HEAD_DIM)
    wk = jnp.pad(wkv[:, :, :MLA_NOPE].transpose(1, 0, 2), ((0, 0), (0, 0), (0, LANES - MLA_NOPE)))
    wv = wkv[:, :, MLA_NOPE:].reshape(LANES, GROUP_W)
    gq = jnp.pad(mp['mla_qk_q'], (0, pad))[None]
    gk = jnp.pad(mp['mla_qk_k'], (0, pad))[None]
    return [mp['mla_q_norm'][None], mp['mla_kv_norm'][None], *[wq[h] for h in range(N_HEADS)],
            *[wk[h] for h in range(N_HEADS)], wv, gq, gk, _rope_perm()]


def _mla_fwd(c_q, c_kv, k_rope, mp, l):
    S = c_q.shape[0]
    tm = _pick(S, (256, 128))
    krope = jnp.pad(k_rope, ((0, 0), (MLA_NOPE, LANES - MLA_DQK)))
    c_tab, s_tab = _rope_tables(S)
    tiled = [c_q, c_kv, krope, c_tab, s_tab]
    params = _mla_params(mp)
    res = _tile_fwd(_f_mla_pre, tiled, params, [(LANES, bf16)] * 8 + [(GROUP_W, bf16)], tm, f'mla_pre_fwd_{l}')
    qs, ks, v = res[:4], res[4:8], res[8]
    y = _mla_attn_fwd(qs, ks, v, f'mla_attn_fwd_{l}')
    return y, (tiled, params, qs, ks, v)


def _mla_bwd(dy, saved, l):
    tiled, params, qs, ks, v = saved
    S = dy.shape[0]
    tm = _pick(S, (256, 128))
    dqs, dks, dv = _mla_attn_bwd(qs, ks, v, dy, f'mla_attn_bwd_{l}')
    (dc_q, dc_kv, dkrope), dpar = _tile_bwd(_f_mla_pre, tiled, params, [*dqs, *dks, dv], [True, True, True, False, False],
                                            [True] * 13 + [False], tm, f'mla_pre_bwd_{l}')
    dqn, dkvn = dpar[0], dpar[1]
    dwq, dwk = jnp.stack(dpar[2:6]), jnp.stack(dpar[6:10])
    dwv, dgq, dgk = dpar[10:13]
    dw_uq = dwq[:, :, :MLA_DQK].transpose(1, 0, 2).reshape(GROUP_W, N_HEADS * MLA_DQK)
    dw_ukv = jnp.concatenate([dwk[:, :, :MLA_NOPE].transpose(1, 0, 2), dwv.reshape(LANES, N_HEADS, HEAD_DIM)],
                             axis=2).reshape(LANES, N_HEADS * (MLA_NOPE + HEAD_DIM))
    grads = {'mla_q_norm': dqn[0], 'mla_kv_norm': dkvn[0], 'mla_w_uq': dw_uq, 'mla_w_ukv': dw_ukv,
             'mla_qk_q': dgq[0, :MLA_DQK], 'mla_qk_k': dgk[0, :MLA_DQK]}
    return dc_q, dc_kv, dkrope[:, MLA_NOPE:MLA_DQK], grads


SPAN = 128


def _head_mean_matrix():
    h = np.arange(GROUP_W) // HEAD_DIM
    return jnp.asarray((h[:, None] == h[None, :]).astype(np.float32) / HEAD_DIM)


def _f_dil_pre(q, k, gq, gk, hm):
    qn = q * lax.rsqrt(_hdot(q * q, hm) + EPS) * gq * (HEAD_DIM ** -0.5)
    kn = k * lax.rsqrt(_hdot(k * k, hm) + EPS) * gk
    return qn, kn


def _f_dil_branch(qb, kp, kc, vp, vc, b0, b1, b2, b3, first):
    kcat = jnp.concatenate([kp, kc], axis=0)
    vcat = jnp.concatenate([vp, vc], axis=0)
    qi = lax.broadcasted_iota(jnp.int32, (SPAN, 2 * SPAN), 0) + SPAN
    kj = lax.broadcasted_iota(jnp.int32, (SPAN, 2 * SPAN), 1)
    delta = qi - kj
    valid = (delta >= 0) & (delta <= SPAN) & jnp.logical_not(first & (kj < SPAN))
    masks = [_head_mask(h) for h in range(N_HEADS)]
    raw = [_bmm_nt(qb * hm, kcat) for hm in masks]
    logits = [jnp.where(valid, r + bias, NEG_INF) for r, bias in zip(raw, (b0, b1, b2, b3))]
    ms = [jnp.max(lg, axis=-1, keepdims=True) for lg in logits]
    ps = [jnp.exp(lg - m) for lg, m in zip(logits, ms)]
    pvs = [_bmm(p, vcat) for p in ps]
    o = sum(pv * hm for pv, hm in zip(pvs, masks))
    m_full = sum(m * hm for m, hm in zip(ms, masks))
    l_full = sum(jnp.sum(p, axis=-1, keepdims=True) * hm for p, hm in zip(ps, masks))
    return o, m_full, l_full


def _dil_branch_specs(d, nb):
    cur = pl.BlockSpec((1, SPAN, GROUP_W), lambda r, n: (r, n, 0))
    prev = pl.BlockSpec((1, SPAN, GROUP_W), lambda r, n: (r, jnp.maximum(n - 1, 0), 0))
    bias = pl.BlockSpec((1, SPAN, 2 * SPAN), lambda r, n: (0, 0, 0))
    return cur, prev, bias


def _head_table_specs():
    return [pl.BlockSpec((1, SPAN, 2 * SPAN), lambda r, n, h=h: (h, 0, 0)) for h in range(N_HEADS)]


def _dil_branch_fwd(q, k, v, table, name):
    d, L, _ = q.shape
    nb = L // SPAN
    cur, prev, bias = _dil_branch_specs(d, nb)

    def body(q_ref, kp_ref, kc_ref, vp_ref, vc_ref, b0, b1, b2, b3, o_ref, m_ref, l_ref):
        o, m, l = _f_dil_branch(q_ref[0], kp_ref[0], kc_ref[0], vp_ref[0], vc_ref[0], b0[0], b1[0], b2[0], b3[0],
                                pl.program_id(1) == 0)
        o_ref[0] = o
        m_ref[0] = m
        l_ref[0] = l

    return pl.pallas_call(
        body, name=name, grid=(d, nb), in_specs=[cur, prev, cur, prev, cur] + _head_table_specs(),
        out_specs=[cur] * 3, out_shape=[jax.ShapeDtypeStruct(q.shape, f32)] * 3,
        compiler_params=_cparams(('parallel', 'parallel')),
    )(q, k, k, v, v, *[table] * N_HEADS)


def _dil_branch_bwd(q, k, v, table, do, dm, dl, name):
    d, L, _ = q.shape
    nb = L // SPAN
    cur, prev, bias = _dil_branch_specs(d, nb)
    whole = pl.BlockSpec((1, L, GROUP_W), lambda r, n: (r, 0, 0))

    def body(q_ref, kp_ref, kc_ref, vp_ref, vc_ref, b0, b1, b2, b3, do_ref, dm_ref, dl_ref,
             dq_ref, dk_ref, dv_ref, db0, db1, db2, db3):
        r, n = pl.program_id(0), pl.program_id(1)
        first = n == 0
        _, vjp = jax.vjp(lambda *a: _f_dil_branch(*a, first), q_ref[0], kp_ref[0], kc_ref[0], vp_ref[0], vc_ref[0],
                         b0[0], b1[0], b2[0], b3[0])
        dq, dkp, dkc, dvp, dvc, g0, g1, g2, g3 = vjp((do_ref[0], dm_ref[0], dl_ref[0]))
        dq_ref[0] = dq

        @pl.when(first)
        def _():
            dk_ref[...] = jnp.zeros_like(dk_ref)
            dv_ref[...] = jnp.zeros_like(dv_ref)

        rows = pl.ds(pl.multiple_of(n * SPAN, SPAN), SPAN)
        dk_ref[0, rows, :] += dkc
        dv_ref[0, rows, :] += dvc

        @pl.when(n > 0)
        def _():
            before = pl.ds(pl.multiple_of((n - 1) * SPAN, SPAN), SPAN)
            dk_ref[0, before, :] += dkp
            dv_ref[0, before, :] += dvp

        start = first & (r == 0)
        for o, g in zip((db0, db1, db2, db3), (g0, g1, g2, g3)):
            @pl.when(start)
            def _(o=o, g=g):
                o[0] = g

            @pl.when(jnp.logical_not(start))
            def _(o=o, g=g):
                o[0] += g

    res = pl.pallas_call(
        body, name=name, grid=(d, nb), in_specs=[cur, prev, cur, prev, cur] + _head_table_specs() + [cur] * 3,
        out_specs=[cur, whole, whole] + [bias] * 4,
        out_shape=[jax.ShapeDtypeStruct(q.shape, f32)] * 3 + [jax.ShapeDtypeStruct((1, SPAN, 2 * SPAN), f32)] * 4,
        compiler_params=_cparams(('arbitrary', 'arbitrary')),
    )(q, k, k, v, v, *[table] * N_HEADS, do, dm, dl)
    return res[0], res[1], res[2], res[3:]


def _f_dil_merge(o1, m1, l1, o2, m2, l2, o3, m3, l3):
    mx = jnp.maximum(jnp.maximum(m1, m2), m3)
    w1, w2, w3 = jnp.exp(m1 - mx), jnp.exp(m2 - mx), jnp.exp(m3 - mx)
    return ((w1 * o1 + w2 * o2 + w3 * o3) / (w1 * l1 + w2 * l2 + w3 * l3),)


def _bias_onehot(dilation):
    qi = jnp.arange(SPAN, dtype=jnp.int32)[:, None] + SPAN
    kj = jnp.arange(2 * SPAN, dtype=jnp.int32)[None, :]
    bucket = _t5_bucket(jnp.clip(qi - kj, 0, SPAN) * dilation).reshape(-1)
    return (bucket[None, :] == jnp.arange(T5_BUCKETS, dtype=jnp.int32)[:, None]).astype(f32)


def _bias_tables(t5_t, onehot, name):
    N = onehot.shape[1]
    tn = _pick(N, (4096, 2048, 1024))

    def body(t_ref, oh_ref, o_ref):
        o_ref[...] = _hdot(t_ref[...], oh_ref[...])

    return pl.pallas_call(
        body, name=name, grid=(N // tn,),
        in_specs=[pl.BlockSpec((8, T5_BUCKETS), lambda i: (0, 0)), pl.BlockSpec((T5_BUCKETS, tn), lambda i: (0, i))],
        out_specs=pl.BlockSpec((8, tn), lambda i: (0, i)), out_shape=jax.ShapeDtypeStruct((8, N), f32),
        compiler_params=_cparams(('parallel',)),
    )(t5_t, onehot)


def _bias_tables_bwd(d_tab, onehot, name):
    N = onehot.shape[1]
    tn = _pick(N, (4096, 2048, 1024))

    def body(g_ref, oh_ref, o_ref):
        part = _hdot_nt(g_ref[...], oh_ref[...])

        @pl.when(pl.program_id(0) == 0)
        def _():
            o_ref[...] = part

        @pl.when(pl.program_id(0) > 0)
        def _():
            o_ref[...] += part

    return pl.pallas_call(
        body, name=name, grid=(N // tn,),
        in_specs=[pl.BlockSpec((8, tn), lambda i: (0, i)), pl.BlockSpec((T5_BUCKETS, tn), lambda i: (0, i))],
        out_specs=pl.BlockSpec((8, T5_BUCKETS), lambda i: (0, 0)), out_shape=jax.ShapeDtypeStruct((8, T5_BUCKETS), f32),
        compiler_params=_cparams(('arbitrary',)),
    )(d_tab, onehot)


def _by_residue(t, d):
    S, C = t.shape
    return t.reshape(S // d, d, C).transpose(1, 0, 2)


def _from_residue(t):
    d, L, C = t.shape
    return t.transpose(1, 0, 2).reshape(d * L, C)


def _dil_fwd(qkv, mp, l):
    S = qkv.shape[0]
    tm = _pick(S, (256, 128))
    q, k, v = qkv[:, :GROUP_W], qkv[:, GROUP_W:2 * GROUP_W], qkv[:, 2 * GROUP_W:]
    pre_params = [jnp.tile(mp['dil_q_norm'], N_HEADS)[None], jnp.tile(mp['dil_k_norm'], N_HEADS)[None], _head_mean_matrix()]
    qn, kn = _tile_fwd(_f_dil_pre, [q, k], pre_params, [(GROUP_W, f32)] * 2, tm, f'dil_pre_fwd_{l}')
    t5_t = jnp.pad(mp['t5_bias'].T, ((0, 8 - N_HEADS), (0, 0)))
    branches, outs = [], []
    for bi, (_, d) in enumerate(DIL_PAIRS):
        onehot = _bias_onehot(d)
        tab = _bias_tables(t5_t, onehot, f'dil_bias_fwd_{l}_{bi}').reshape(8, SPAN, 2 * SPAN)
        qd, kd, vd = _by_residue(qn, d), _by_residue(kn, d), _by_residue(v, d)
        o, m, lsum = _dil_branch_fwd(qd, kd, vd, tab, f'dil_branch_fwd_{l}_{bi}')
        branches.append((qd, kd, vd, tab, onehot))
        outs += [_from_residue(o), _from_residue(m), _from_residue(lsum)]
    (y,) = _tile_fwd(_f_dil_merge, outs, [], [(GROUP_W, f32)], tm, f'dil_merge_fwd_{l}')
    return y, (q, k, pre_params, branches, outs)


def _dil_bwd(dy, saved, l):
    q, k, pre_params, branches, outs = saved
    S = dy.shape[0]
    tm = _pick(S, (256, 128))
    douts, _ = _tile_bwd(_f_dil_merge, outs, [], [dy], [True] * 9, [], tm, f'dil_merge_bwd_{l}')
    dqn = dkn = dv = None
    dt5_t = None
    for bi, (_, d) in enumerate(DIL_PAIRS):
        qd, kd, vd, tab, onehot = branches[bi]
        do, dm, dl = [_by_residue(t, d) for t in douts[3 * bi:3 * bi + 3]]
        dq_b, dk_b, dv_b, dbias = _dil_branch_bwd(qd, kd, vd, tab, do, dm, dl, f'dil_branch_bwd_{l}_{bi}')
        d_tab = jnp.concatenate([*dbias, jnp.zeros((8 - N_HEADS, SPAN, 2 * SPAN), f32)], axis=0).reshape(8, -1)
        g_t5 = _bias_tables_bwd(d_tab, onehot, f'dil_bias_bwd_{l}_{bi}')
        dq_b, dk_b, dv_b = _from_residue(dq_b), _from_residue(dk_b), _from_residue(dv_b)
        dqn = dq_b if dqn is None else dqn + dq_b
        dkn = dk_b if dkn is None else dkn + dk_b
        dv = dv_b if dv is None else dv + dv_b
        dt5_t = g_t5 if dt5_t is None else dt5_t + g_t5
    (dq, dk), (dgq, dgk) = _tile_bwd(_f_dil_pre, [q, k], pre_params, [dqn, dkn], [True, True], [True, True, False], tm,
                                     f'dil_pre_bwd_{l}')
    grads = {'dil_q_norm': dgq.reshape(N_HEADS, HEAD_DIM).sum(0), 'dil_k_norm': dgk.reshape(N_HEADS, HEAD_DIM).sum(0),
             't5_bias': dt5_t[:N_HEADS].T}
    return jnp.concatenate([dq, dk, dv], axis=1), grads


S5_LANES = S5_G * S5_P
SCAN_SEGMENTS = 8
SCAN_W = 256


def _f_s5_prep(bre, bim, lr, li, logdt_col, expand):
    dt = jnp.sum(jnp.exp(logdt_col) * expand, axis=0, keepdims=True)
    mag = jnp.exp(lr * dt)
    ar, ai = mag * jnp.cos(li * dt), mag * jnp.sin(li * dt)
    den = lr * lr + li * li
    nr, ni = ar - 1.0, ai
    zr = (nr * lr + ni * li) / den
    zi = (ni * lr - nr * li) / den
    bb = jnp.concatenate([zr * bre - zi * bim, zr * bim + zi * bre], axis=1)
    a_rows = jnp.broadcast_to(jnp.concatenate([ar, ai], axis=1), bb.shape)
    return bb, a_rows


def _s5_scan(x, a_rows, name, reverse=False, h=None):
    S = x.shape[0]
    NL = x.shape[1] // 2
    T = S // SCAN_SEGMENTS
    nblk = NL // SCAN_W
    n_in = 4 if reverse else 2

    def body(*refs):
        if reverse:
            (x_hbm, pr_hbm, pi_hbm, ar_ref, ai_ref, hr_hbm, hi_hbm, dar_ref, dai_ref,
             xr_s, xi_s, pr_s, pi_s, hr_s, hi_s, in_sems, out_sems) = refs
        else:
            x_hbm, ar_ref, ai_ref, hr_hbm, hi_hbm, xr_s, xi_s, hr_s, hi_s, in_sems, out_sems = refs
        col = pl.multiple_of(pl.program_id(0) * SCAN_W, SCAN_W)
        loads = []
        for k in range(SCAN_SEGMENTS):
            rows = pl.ds(k * T, T)
            sources = [(x_hbm, col, xr_s), (x_hbm, NL + col, xi_s)]
            if reverse:
                sources += [(pr_hbm, col, pr_s), (pi_hbm, col, pi_s)]
            for i, (src, c0, dst) in enumerate(sources):
                loads.append(pltpu.make_async_copy(src.at[rows, pl.ds(c0, SCAN_W)], dst.at[:, k, :],
                                                   in_sems.at[i * SCAN_SEGMENTS + k]))
        for cp in loads:
            cp.start()
        for cp in loads:
            cp.wait()
        ar = ar_ref[...]
        ai = -ai_ref[...] if reverse else ai_ref[...]
        zero = jnp.zeros((SCAN_SEGMENTS, SCAN_W), f32)

        def at(s):
            return T - 1 - s if reverse else s

        def local(s, c):
            hr, hi, pr, pi = c
            j = at(s)
            nhr = ar * hr - ai * hi + xr_s[j]
            nhi = ar * hi + ai * hr + xi_s[j]
            hr_s[j] = nhr
            hi_s[j] = nhi
            return nhr, nhi, ar * pr - ai * pi, ar * pi + ai * pr

        er, ei, pr, pi = lax.fori_loop(0, T, local, (zero, zero, zero + 1.0, zero), unroll=2)
        row = lax.broadcasted_iota(jnp.int32, (SCAN_SEGMENTS, SCAN_W), 0)
        cr, ci = zero, zero
        order = range(SCAN_SEGMENTS - 2, -1, -1) if reverse else range(1, SCAN_SEGMENTS)
        for k in order:
            src = k + 1 if reverse else k - 1
            tr = er + pr * cr - pi * ci
            ti = ei + pr * ci + pi * cr
            cr = jnp.where(row == k, jnp.sum(jnp.where(row == src, tr, 0.0), axis=0, keepdims=True), cr)
            ci = jnp.where(row == k, jnp.sum(jnp.where(row == src, ti, 0.0), axis=0, keepdims=True), ci)

        def fix_at(j, c, before):
            pr, pi, sr, si = c
            pr, pi = ar * pr - ai * pi, ar * pi + ai * pr
            hr = hr_s[j] + pr * cr - pi * ci
            hi = hi_s[j] + pr * ci + pi * cr
            hr_s[j] = hr
            hi_s[j] = hi
            if reverse:
                qr, qi = before
                sr = sr + hr * qr + hi * qi
                si = si + hi * qr - hr * qi
            return pr, pi, sr, si

        start = (zero + 1.0, zero, zero, zero)
        if reverse:
            def fix(s, c):
                j = T - 1 - s
                return fix_at(j, c, (pr_s[j - 1], pi_s[j - 1]))

            c = lax.fori_loop(0, T - 1, fix, start, unroll=2)
            last_r = jnp.where(row == 0, 0.0, pltpu.roll(pr_s[T - 1], 1, 0))
            last_i = jnp.where(row == 0, 0.0, pltpu.roll(pi_s[T - 1], 1, 0))
            _, _, sr, si = fix_at(0, c, (last_r, last_i))
            dar_ref[...] = sr
            dai_ref[...] = si
        else:
            lax.fori_loop(0, T, lambda s, c: fix_at(s, c, None), start, unroll=2)
        stores = []
        for k in range(SCAN_SEGMENTS):
            rows = pl.ds(k * T, T)
            stores.append(pltpu.make_async_copy(hr_s.at[:, k, :], hr_hbm.at[rows, pl.ds(col, SCAN_W)], out_sems.at[k]))
            stores.append(pltpu.make_async_copy(hi_s.at[:, k, :], hi_hbm.at[rows, pl.ds(col, SCAN_W)],
                                                out_sems.at[SCAN_SEGMENTS + k]))
        for cp in stores:
            cp.start()
        for cp in stores:
            cp.wait()

    a_re = pl.BlockSpec((SCAN_SEGMENTS, SCAN_W), lambda b: (0, b))
    a_im = pl.BlockSpec((SCAN_SEGMENTS, SCAN_W), lambda b: (0, nblk + b))
    seq = pltpu.VMEM((T, SCAN_SEGMENTS, SCAN_W), f32)
    if reverse:
        in_specs, args = [ANY, ANY, ANY, a_re, a_im], [x, h[0], h[1], a_rows, a_rows]
        out_specs = [ANY, ANY, a_re, a_re]
        out_shape = [jax.ShapeDtypeStruct((S, NL), f32)] * 2 + [jax.ShapeDtypeStruct((SCAN_SEGMENTS, NL), f32)] * 2
    else:
        in_specs, args = [ANY, a_re, a_im], [x, a_rows, a_rows]
        out_specs = [ANY, ANY]
        out_shape = [jax.ShapeDtypeStruct((S, NL), f32)] * 2
    scratch = [seq] * (n_in + 2) + [pltpu.SemaphoreType.DMA((n_in * SCAN_SEGMENTS,)),
                                    pltpu.SemaphoreType.DMA((2 * SCAN_SEGMENTS,))]
    return pl.pallas_call(body, name=name, grid=(nblk,), in_specs=in_specs, out_specs=out_specs, out_shape=out_shape,
                          scratch_shapes=scratch, compiler_params=_cparams(('arbitrary',), big=True))(*args)


def _f_s5_post(y, u, d, w_glu):
    z = _bmm(y + d * u, w_glu)
    return (z[:, :GROUP_W] * jax.nn.sigmoid(z[:, GROUP_W:]),)


def _block_diag(t):
    G, a, b = t.shape
    eye = jnp.eye(G, dtype=t.dtype)
    return (t[:, :, None, :] * eye[:, None, :, None]).reshape(G * a, G * b)


def _diag_blocks(m, a, b):
    G = m.shape[0] // a
    return jnp.moveaxis(jnp.diagonal(m.reshape(G, a, G, b), axis1=0, axis2=2), -1, 0)


def _s5_fwd(u, mp, l):
    S = u.shape[0]
    tm = _pick(S, (256, 128))
    bre = _block_diag(mp['s5_b_re'].transpose(0, 2, 1))
    bim = _block_diag(mp['s5_b_im'].transpose(0, 2, 1))
    expand = jnp.repeat(jnp.eye(S5_G, dtype=f32), S5_P, axis=1)
    prep_params = [mp['s5_lambda_re'].reshape(1, S5_LANES), mp['s5_lambda_im'].reshape(1, S5_LANES),
                   mp['s5_log_dt'].reshape(S5_G, 1), expand]
    bb, a_rows = _tile_fwd(_f_s5_prep, [bre, bim], prep_params, [(2 * S5_LANES, f32)] * 2, GROUP_W, f's5_prep_fwd_{l}')
    x = _mm(u, bb, 'nn', f's5_in_fwd_{l}')
    hr, hi = _s5_scan(x, a_rows, f's5_scan_fwd_{l}')
    c_re, c_im = _block_diag(mp['s5_c_re'].transpose(0, 2, 1)), -_block_diag(mp['s5_c_im'].transpose(0, 2, 1))
    y = _mm(hi, c_im, 'nn', f's5_out_im_fwd_{l}', add=_mm(hr, c_re, 'nn', f's5_out_re_fwd_{l}'))
    post_params = [mp['s5_d'][None], mp['s5_w_glu']]
    (out,) = _tile_fwd(_f_s5_post, [y, u], post_params, [(GROUP_W, f32)], tm, f's5_post_fwd_{l}')
    return out, (u, bre, bim, prep_params, bb, a_rows, hr, hi, c_re, c_im, y, post_params)


def _s5_bwd(dout, saved, l):
    u, bre, bim, prep_params, bb, a_rows, hr, hi, c_re, c_im, y, post_params = saved
    S = u.shape[0]
    tm = _pick(S, (256, 128))
    (dy, du1), (dd, dwglu) = _tile_bwd(_f_s5_post, [y, u], post_params, [dout], [True, True], [True, True], tm,
                                       f's5_post_bwd_{l}')
    ccat = jnp.concatenate([c_re, c_im], axis=0)
    dh = _mm(dy, ccat, 'nt', f's5_out_dx_{l}')
    dccat = jnp.concatenate([_mm(hr, dy, 'tn', f's5_out_re_dw_{l}'), _mm(hi, dy, 'tn', f's5_out_im_dw_{l}')], axis=0)
    lr_, li_, dar, dai = _s5_scan(dh, a_rows, f's5_scan_bwd_{l}', reverse=True, h=(hr, hi))
    du2 = _mm(li_, bb[:, S5_LANES:], 'nt', f's5_in_im_dx_{l}', add=_mm(lr_, bb[:, :S5_LANES], 'nt', f's5_in_re_dx_{l}'))
    dbb = jnp.concatenate([_mm(u, lr_, 'tn', f's5_in_re_dw_{l}'), _mm(u, li_, 'tn', f's5_in_im_dw_{l}')], axis=1)
    da_rows = jnp.pad(jnp.concatenate([dar, dai], axis=1), ((0, GROUP_W - SCAN_SEGMENTS), (0, 0)))
    (dbre, dbim), (dlr, dli, dlogdt) = _tile_bwd(_f_s5_prep, [bre, bim], prep_params, [dbb, da_rows], [True, True],
                                                 [True, True, True, False], GROUP_W, f's5_prep_bwd_{l}')
    grads = {
        's5_lambda_re': dlr.reshape(S5_G, S5_P), 's5_lambda_im': dli.reshape(S5_G, S5_P), 's5_log_dt': dlogdt[:, 0],
        's5_b_re': _diag_blocks(dbre, S5_CG, S5_P).transpose(0, 2, 1),
        's5_b_im': _diag_blocks(dbim, S5_CG, S5_P).transpose(0, 2, 1),
        's5_c_re': _diag_blocks(dccat[:S5_LANES], S5_P, S5_CG).transpose(0, 2, 1),
        's5_c_im': -_diag_blocks(dccat[S5_LANES:], S5_P, S5_CG).transpose(0, 2, 1),
        's5_d': dd[0], 's5_w_glu': dwglu}
    return du1 + du2, grads


DN_CONV = 4


def _head_sum_matrix():
    h = np.arange(GROUP_W) // HEAD_DIM
    return jnp.asarray((h[:, None] == h[None, :]).astype(np.float32))


def _f_dn_pre(x0, x1, x2, x3, ab, w0, w1, w2, w3, alog, dtb, ea, eb, hs):
    c = w0 * x0 + w1 * x1 + w2 * x2 + w3 * x3
    s = c * jax.nn.sigmoid(c)
    q, k, v = s[:, :GROUP_W], s[:, GROUP_W:2 * GROUP_W], s[:, 2 * GROUP_W:]
    q = q * lax.rsqrt(_hdot(q * q, hs) + EPS) * (HEAD_DIM ** -0.5)
    k = k * lax.rsqrt(_hdot(k * k, hs) + EPS)
    beta = jax.nn.sigmoid(_hdot(ab, eb))
    g = -jnp.exp(alog) * jax.nn.softplus(_hdot(ab, ea) + dtb)
    return q, k, v, g, beta


DN_CHUNKS_PER_STEP = 4


def _f_dn_chunks(q, k, v, g, beta):
    C = DN_CHUNK
    n_chunks = q.shape[0] // C
    r = lax.broadcasted_iota(jnp.int32, (C, C), 0)
    c = lax.broadcasted_iota(jnp.int32, (C, C), 1)
    causal, strict = r >= c, r > c
    eye = (r == c).astype(f32)
    tril = causal.astype(f32)
    ones = jnp.ones((C, GROUP_W), f32)
    masks = [_head_mask(h) for h in range(N_HEADS)]
    rows = [tuple(t[i * C:(i + 1) * C] for t in (q, k, v, g, beta)) for i in range(n_chunks)]
    gcs = [_hdot(tril, gi) for (_, _, _, gi, _) in rows]
    items = [(i, h) for i in range(n_chunks) for h in range(N_HEADS)]
    grows = [_hdot_nt(ones * (masks[h] * (1.0 / HEAD_DIM)), gcs[i]) for i, h in items]
    decs = []
    for (i, h), grow in zip(items, grows):
        gcol = jnp.sum(gcs[i] * masks[h], axis=1, keepdims=True) * (1.0 / HEAD_DIM)
        decs.append(jnp.exp(jnp.where(causal, gcol - grow, NEG_INF)))
    kbs = [ki * bi for (_, ki, _, _, bi) in rows]
    kks = [_bmm_nt(kbs[i] * masks[h], rows[i][1]) for i, h in items]
    qks = [_bmm_nt(rows[i][0] * masks[h], rows[i][1]) for i, h in items]
    lmats = [jnp.where(strict, kk * dec, 0.0) for kk, dec in zip(kks, decs)]
    a_qk = [jnp.where(causal, qk * dec, 0.0) for qk, dec in zip(qks, decs)]
    ts = [eye - lm for lm in lmats]
    ps = lmats
    for _ in range(5):
        ps = [_bmm(p, p) for p in ps]
        ts = [t + _bmm(t, p) for t, p in zip(ts, ps)]
    egs = [jnp.exp(gc) for gc in gcs]
    tw = [_bmm(t, kbs[i] * egs[i]) for (i, h), t in zip(items, ts)]
    tu = [_bmm(t, rows[i][2] * rows[i][4]) for (i, h), t in zip(items, ts)]
    outs = []
    for i in range(n_chunks):
        qi, ki, _, gi, _ = rows[i]
        glast = jnp.sum(gi, axis=0, keepdims=True)
        w = sum(tw[i * N_HEADS + h] * masks[h] for h in range(N_HEADS))
        u = sum(tu[i * N_HEADS + h] * masks[h] for h in range(N_HEADS))
        outs.append((w, u, qi * egs[i], ki * jnp.exp(glast - gcs[i]), *a_qk[i * N_HEADS:(i + 1) * N_HEADS],
                     jnp.broadcast_to(jnp.exp(glast), (C, GROUP_W))))
    return tuple(jnp.concatenate(parts, axis=0) for parts in zip(*outs))


def _f_dn_step(w, u, qd, kdec, a0, a1, a2, a3, dfull, state, bd):
    row0 = (lax.broadcasted_iota(jnp.int32, dfull.shape, 0) == 0).astype(f32)
    dvec = jnp.sum(dfull * row0, axis=0, keepdims=True)
    ws, qs = _bmm(w, state), _bmm(qd, state)
    vnew = u - ws
    avs = [_bmm(a, vnew) for a in (a0, a1, a2, a3)]
    kv = _bmm_tn(kdec, vnew)
    o = qs + sum(av * _head_mask(h) for h, av in enumerate(avs))
    return o, state * dvec + bd * kv


def _dn_scan_fwd(ins, name):
    S = ins[0].shape[0]
    N = S // DN_CHUNK
    bd = _head_sum_matrix()

    def body(*refs):
        o_ref, s_ref, state = refs[10], refs[11], refs[12]

        @pl.when(pl.program_id(0) == 0)
        def _():
            state[...] = jnp.zeros_like(state)

        s_in = state[...]
        s_ref[0] = s_in
        o, s_out = _f_dn_step(*[r[...] for r in refs[:9]], s_in, refs[9][...])
        o_ref[...] = o
        state[...] = s_out

    return pl.pallas_call(
        body, name=name, grid=(N,),
        in_specs=[pl.BlockSpec((DN_CHUNK, t.shape[1]), lambda n: (n, 0)) for t in ins] + [_full_spec(bd)],
        out_specs=[pl.BlockSpec((DN_CHUNK, GROUP_W), lambda n: (n, 0)), pl.BlockSpec((1, GROUP_W, GROUP_W), lambda n: (n, 0, 0))],
        out_shape=[jax.ShapeDtypeStruct((S, GROUP_W), f32), jax.ShapeDtypeStruct((N, GROUP_W, GROUP_W), f32)],
        scratch_shapes=[pltpu.VMEM((GROUP_W, GROUP_W), f32)],
        compiler_params=_cparams(('arbitrary',)),
    )(*ins, bd)


def _dn_scan_bwd(ins, states, do, name):
    S = ins[0].shape[0]
    N = S // DN_CHUNK
    bd = _head_sum_matrix()

    def body(*refs):
        s_ref, do_ref = refs[9], refs[10]
        bd_ref = refs[11]
        outs = refs[12:21]
        dstate = refs[21]

        @pl.when(pl.program_id(0) == 0)
        def _():
            dstate[...] = jnp.zeros_like(dstate)

        bd_val = bd_ref[...]
        _, vjp = jax.vjp(lambda *a: _f_dn_step(*a, bd_val), *[r[...] for r in refs[:9]], s_ref[0])
        grads = vjp((do_ref[...], dstate[...]))
        for o, g in zip(outs, grads[:9]):
            o[...] = g
        dstate[...] = grads[9]

    def rev(n):
        return (N - 1 - n, 0)

    res = pl.pallas_call(
        body, name=name, grid=(N,),
        in_specs=[pl.BlockSpec((DN_CHUNK, t.shape[1]), rev) for t in ins] +
                 [pl.BlockSpec((1, GROUP_W, GROUP_W), lambda n: (N - 1 - n, 0, 0)), pl.BlockSpec((DN_CHUNK, GROUP_W), rev),
                  _full_spec(bd)],
        out_specs=[pl.BlockSpec((DN_CHUNK, t.shape[1]), rev) for t in ins],
        out_shape=[jax.ShapeDtypeStruct(t.shape, f32) for t in ins],
        scratch_shapes=[pltpu.VMEM((GROUP_W, GROUP_W), f32)],
        compiler_params=_cparams(('arbitrary',)),
    )(*ins, states, do, bd)
    return list(res)


def _f_dn_post(o, gate, gain, hmean):
    return (o * lax.rsqrt(_hdot(o * o, hmean) + EPS) * gain * (gate * jax.nn.sigmoid(gate)),)


def _delay(t, j):
    return t if j == 0 else jnp.pad(t[:-j], ((j, 0), (0, 0)))


def _advance(t, j):
    return t if j == 0 else jnp.pad(t[j:], ((0, j), (0, 0)))


def _dn_fwd(qkv, a, b, gate, mp, l):
    S = qkv.shape[0]
    tm = _pick(S, (256, 128))
    xs = [_delay(qkv, DN_CONV - 1 - j) for j in range(DN_CONV)]
    ab = jnp.pad(jnp.concatenate([a, b], axis=1), ((0, 0), (0, LANES - 2 * N_HEADS)))
    sel = np.zeros((2, LANES, GROUP_W), np.float32)
    for h in range(N_HEADS):
        sel[0, h, h * HEAD_DIM:(h + 1) * HEAD_DIM] = 1.0
        sel[1, N_HEADS + h, h * HEAD_DIM:(h + 1) * HEAD_DIM] = 1.0
    pre_params = [*[mp['dn_conv'][j][None] for j in range(DN_CONV)], jnp.repeat(mp['dn_a_log'], HEAD_DIM)[None],
                  jnp.repeat(mp['dn_dt_bias'], HEAD_DIM)[None], jnp.asarray(sel[0]), jnp.asarray(sel[1]), _head_sum_matrix()]
    pre = _tile_fwd(_f_dn_pre, [*xs, ab], pre_params, [(GROUP_W, f32)] * 5, tm, f'dn_pre_fwd_{l}')
    chunk_outs = [(GROUP_W, f32)] * 4 + [(HEAD_DIM, f32)] * 4 + [(GROUP_W, f32)]
    parts = _tile_fwd(_f_dn_chunks, pre, [], chunk_outs, DN_CHUNK * DN_CHUNKS_PER_STEP, f'dn_chunk_fwd_{l}')
    o, states = _dn_scan_fwd(parts, f'dn_scan_fwd_{l}')
    post_params = [jnp.tile(mp['dn_o_norm'], N_HEADS)[None], _head_mean_matrix()]
    (y,) = _tile_fwd(_f_dn_post, [o, gate], post_params, [(GROUP_W, f32)], tm, f'dn_post_fwd_{l}')
    return y, (xs, ab, pre_params, pre, parts, states, o, gate, post_params)


def _dn_bwd(dy, saved, l):
    xs, ab, pre_params, pre, parts, states, o, gate, post_params = saved
    S = dy.shape[0]
    tm = _pick(S, (256, 128))
    (do, dgate), (dgain,) = _tile_bwd(_f_dn_post, [o, gate], post_params, [dy], [True, True], [True, False], tm,
                                      f'dn_post_bwd_{l}')
    dparts = _dn_scan_bwd(parts, states, do, f'dn_scan_bwd_{l}')
    dpre, _ = _tile_bwd(_f_dn_chunks, pre, [], dparts, [True] * 5, [], DN_CHUNK * DN_CHUNKS_PER_STEP, f'dn_chunk_bwd_{l}')
    dins, dpar = _tile_bwd(_f_dn_pre, [*xs, ab], pre_params, dpre, [True] * 5, [True] * 6 + [False] * 3, tm,
                           f'dn_pre_bwd_{l}')
    dqkv = dins[DN_CONV - 1]
    for j in range(DN_CONV - 1):
        dqkv = dqkv + _advance(dins[j], DN_CONV - 1 - j)
    dab = dins[DN_CONV]
    grads = {'dn_conv': jnp.concatenate(dpar[:DN_CONV], axis=0),
             'dn_a_log': dpar[4].reshape(N_HEADS, HEAD_DIM).sum(1), 'dn_dt_bias': dpar[5].reshape(N_HEADS, HEAD_DIM).sum(1),
             'dn_o_norm': dgain.reshape(N_HEADS, HEAD_DIM).sum(0)}
    return dqkv, dab[:, :N_HEADS], dab[:, N_HEADS:2 * N_HEADS], dgate, grads


def _t5_bucket(dist):
    exact = T5_BUCKETS // 2
    df = jnp.maximum(dist, 1).astype(f32)
    large = exact + (jnp.log(df / exact) / math.log(T5_MAX_DIST / exact) * (T5_BUCKETS - exact)).astype(jnp.int32)
    large = jnp.minimum(large, T5_BUCKETS - 1)
    return jnp.where(dist < exact, dist, large)


def _split_cols(t, sizes):
    out, start = [], 0
    for s in sizes:
        out.append(t[..., start:start + s])
        start += s
    return out


def _mixers_fwd(proj, mp, l):
    c_q, c_kv, k_rope, u_s5, qkv_dil, qkv_dn, a_dn, b_dn, gate_dn = _split_cols(proj, IN_SPLITS)
    y_mla, s_mla = _mla_fwd(c_q, c_kv, k_rope, mp, l)
    y_s5, s_s5 = _s5_fwd(u_s5, mp, l)
    y_dil, s_dil = _dil_fwd(qkv_dil, mp, l)
    y_dn, s_dn = _dn_fwd(qkv_dn, a_dn, b_dn, gate_dn, mp, l)
    return jnp.concatenate([y_mla, y_s5, y_dil, y_dn], axis=-1), (s_mla, s_s5, s_dil, s_dn)


def _mixers_bwd(dmixed, saved, l):
    s_mla, s_s5, s_dil, s_dn = saved
    d_mla, d_s5, d_dil, d_dn = _split_cols(dmixed, (GROUP_W,) * 4)
    dc_q, dc_kv, dk_rope, g_mla = _mla_bwd(d_mla, s_mla, l)
    du, g_s5 = _s5_bwd(d_s5, s_s5, l)
    dqkv_dil, g_dil = _dil_bwd(d_dil, s_dil, l)
    dqkv_dn, da, db, dgate, g_dn = _dn_bwd(d_dn, s_dn, l)
    parts = [dc_q, dc_kv, dk_rope, du, dqkv_dil, dqkv_dn, da, db, dgate]
    dproj = jnp.concatenate([p.astype(bf16) for p in parts], axis=-1)
    return dproj, {**g_mla, **g_s5, **g_dil, **g_dn}


MIXER_PARAMS = ['mla_q_norm', 'mla_kv_norm', 'mla_w_uq', 'mla_w_ukv', 'mla_qk_q', 'mla_qk_k', 's5_lambda_re',
                's5_lambda_im', 's5_log_dt', 's5_b_re', 's5_b_im', 's5_c_re', 's5_c_im', 's5_d', 's5_w_glu',
                'dil_q_norm', 'dil_k_norm', 't5_bias', 'dn_conv', 'dn_a_log', 'dn_dt_bias', 'dn_o_norm']


def _layer_fwd_mix(h, W, l):
    S = h.shape[0]
    tm = _pick(S, (256, 128))
    g1 = W['attn_norm'][l][None]
    (n1,) = _tile_fwd(_f_rms, [h], [g1], [(D_MODEL, bf16)], tm, f'rms1_fwd_{l}')
    proj = _mm(n1, W['w_in'][l], 'nn', f'proj_fwd_{l}')
    mp = {k: (W[k] if k == 't5_bias' else W[k][l]).astype(f32) for k in MIXER_PARAMS}
    mixed, mix_saved = _mixers_fwd(proj, mp, l)
    mixed_b = mixed.astype(bf16)
    h2 = _mm(mixed_b, W['w_out'][l], 'nn', f'out_fwd_{l}', add=h)
    return h2, dict(h=h, n1=n1, mix=mix_saved, mixed=mixed_b, h2=h2)


def _layer_fwd_ffn(h2, W, l, saved):
    S = h2.shape[0]
    tm = _pick(S, (256, 128))
    g2 = W['ffn_norm'][l][None]
    (n2,) = _tile_fwd(_f_rms, [h2], [g2], [(D_MODEL, bf16)], tm, f'rms2_fwd_{l}')
    w13 = jnp.concatenate([W['ffn_w1'][l], W['ffn_w3'][l]], axis=1)
    uv = _mm(n2, w13, 'nn', f'ffn13_fwd_{l}')
    (act,) = _tile_fwd(_f_swiglu, [uv], [], [(FFN_HIDDEN, bf16)], tm, f'swiglu_fwd_{l}')
    h3 = _mm(act, W['ffn_w2'][l], 'nn', f'ffn2_fwd_{l}', add=h2)
    saved.update(n2=n2, uv=uv, act=act, w13=w13)
    return h3


def _layer_bwd(dh3, saved, W, l):
    S = dh3.shape[0]
    tm = _pick(S, (256, 128))
    g1 = W['attn_norm'][l][None]
    g2 = W['ffn_norm'][l][None]
    grads = {}
    dact = _mm(dh3, W['ffn_w2'][l], 'nt', f'ffn2_dx_{l}')
    grads['ffn_w2'] = _mm(saved['act'], dh3, 'tn', f'ffn2_dw_{l}', out_dtype=bf16)
    (duv,), _ = _tile_bwd(_f_swiglu, [saved['uv']], [], [dact], [True], [], tm, f'swiglu_bwd_{l}', dt_dtypes=[bf16])
    dn2 = _mm(duv, saved['w13'], 'nt', f'ffn13_dx_{l}')
    dw13 = _mm(saved['n2'], duv, 'tn', f'ffn13_dw_{l}', out_dtype=bf16)
    grads['ffn_w1'], grads['ffn_w3'] = dw13[:, :FFN_HIDDEN], dw13[:, FFN_HIDDEN:]
    (dh2n,), (dg2,) = _tile_bwd(_f_rms, [saved['h2']], [g2], [dn2], [True], [True], tm, f'rms2_bwd_{l}')
    grads['ffn_norm'] = dg2[0]
    dh2 = dh3 + dh2n
    dmixed = _mm(dh2, W['w_out'][l], 'nt', f'out_dx_{l}')
    grads['w_out'] = _mm(saved['mixed'], dh2, 'tn', f'out_dw_{l}', out_dtype=bf16)
    dproj, dmp = _mixers_bwd(dmixed, saved['mix'], l)
    for k in MIXER_PARAMS:
        grads[k] = dmp[k]
    dn1 = _mm(dproj, W['w_in'][l], 'nt', f'proj_dx_{l}')
    grads['w_in'] = _mm(saved['n1'], dproj, 'tn', f'proj_dw_{l}', out_dtype=bf16)
    (dh1n,), (dg1,) = _tile_bwd(_f_rms, [saved['h']], [g1], [dn1], [True], [True], tm, f'rms1_bwd_{l}')
    grads['attn_norm'] = dg1[0]
    return dh2 + dh1n, grads


def kernel(x, attn_norm, w_in, w_out, mla_q_norm, mla_kv_norm, mla_w_uq, mla_w_ukv, mla_qk_q, mla_qk_k, s5_lambda_re, s5_lambda_im, s5_log_dt, s5_b_re, s5_b_im, s5_c_re, s5_c_im, s5_d, s5_w_glu, dil_q_norm, dil_k_norm, t5_bias, dn_conv, dn_a_log, dn_dt_bias, dn_o_norm, ffn_norm, ffn_w1, ffn_w3, ffn_w2, loss_target, m_attn_norm, m_w_in, m_w_out, m_mla_q_norm, m_mla_kv_norm, m_mla_w_uq, m_mla_w_ukv, m_mla_qk_q, m_mla_qk_k, m_s5_lambda_re, m_s5_lambda_im, m_s5_log_dt, m_s5_b_re, m_s5_b_im, m_s5_c_re, m_s5_c_im, m_s5_d, m_s5_w_glu, m_dil_q_norm, m_dil_k_norm, m_t5_bias, m_dn_conv, m_dn_a_log, m_dn_dt_bias, m_dn_o_norm, m_ffn_norm, m_ffn_w1, m_ffn_w3, m_ffn_w2, v_attn_norm, v_w_in, v_w_out, v_mla_q_norm, v_mla_kv_norm, v_mla_w_uq, v_mla_w_ukv, v_mla_qk_q, v_mla_qk_k, v_s5_lambda_re, v_s5_lambda_im, v_s5_log_dt, v_s5_b_re, v_s5_b_im, v_s5_c_re, v_s5_c_im, v_s5_d, v_s5_w_glu, v_dil_q_norm, v_dil_k_norm, v_t5_bias, v_dn_conv, v_dn_a_log, v_dn_dt_bias, v_dn_o_norm, v_ffn_norm, v_ffn_w1, v_ffn_w3, v_ffn_w2):
    given = dict(locals())
    w_loc = {n: given[n] for n in WEIGHTS}
    m_loc = {n: given['m_' + n] for n in WEIGHTS}
    v_loc = {n: given['v_' + n] for n in WEIGHTS}
    big_names = list(BIG)

    own = 2 * lax.axis_index('x') + lax.axis_index('y')
    groups = [[(n, 0) for n in GATHER_FIRST], [(n, 0) for n in GATHER_FFN], [(n, 1) for n in big_names]]
    started, order = [], jnp.zeros((8, LANES), f32)
    for gi, group in enumerate(groups):
        blocks = [w_loc[n][l].astype(bf16) for n, l in group]
        lands = [lax.empty((N_SHARDS,) + b.shape, bf16) for b in blocks]
        send_sems, recv_sems, blocks, lands, order = _to_chips_start(blocks, lands, False, order, f'gather_start_{gi}')
        started.append((send_sems, recv_sems, blocks, lands))
    W = {n: [None] * DEPTH for n in big_names}
    for n in SMALL:
        W[n] = w_loc[n]

    def arrive(gi, after):
        send_sems, recv_sems, blocks, lands = started[gi]
        lands = _to_chips_wait(send_sems, recv_sems, blocks, lands, False, after, f'gather_wait_{gi}')
        for (n, l), block, land in zip(groups[gi], blocks, lands):
            W[n][l] = _from_shards(n, lax.dynamic_update_slice(land, block[None], (own, 0, 0)))

    arrive(0, order)
    h = x[0]
    saved = []
    for l in range(DEPTH):
        h2, sv = _layer_fwd_mix(h, W, l)
        if l == 0:
            arrive(1, h2)
        h = _layer_fwd_ffn(h2, W, l, sv)
        if l == 0:
            arrive(2, h)
        saved.append(sv)
    parts_loss, dh = _loss_head(h, loss_target[0])
    loss = lax.psum(jnp.sum(parts_loss), ('x', 'y', 'c'))

    layer_grads = [None] * DEPTH
    for l in reversed(range(DEPTH)):
        dh, layer_grads[l] = _layer_bwd(dh, saved[l], W, l)
    grad_x = dh[None]
    small_full = []
    for n in SMALL:
        if n == 't5_bias':
            small_full.append(layer_grads[0][n] + layer_grads[1][n])
        else:
            small_full.append(jnp.stack([layer_grads[l][n] for l in range(DEPTH)]))

    gs = [jnp.stack([_by_shard(n, layer_grads[l][n]).astype(bf16) for l in range(DEPTH)], axis=1) for n in big_names]
    small_shapes = [w_loc[n].shape for n in SMALL]
    small_pack = _pack(small_full)
    from_sib, recv_small = _swap_with_sibling(gs, small_pack)
    chip_sums = [_chip_sum_of(g, r, 'chip_sum_' + n) for n, g, r in zip(big_names, gs, from_sib)]
    chip_small = _small_chip_sum(small_pack, recv_small)
    from_chips, from_chips_small = _exchange_between_chips(chip_sums, chip_small)
    totals = [_shard_total_of(g, r, rc, 'shard_total_' + n) for n, g, r, rc in zip(big_names, gs, from_sib, from_chips)]
    g_big = _join_with_sibling(totals)

    g_small_p, d_small_p, m_small_p, v_small_p = _small_update(
        small_pack, recv_small, from_chips_small, _pack([w_loc[n] for n in SMALL]),
        _pack([m_loc[n] for n in SMALL]), _pack([v_loc[n] for n in SMALL]))
    grad, delta, new_m, new_v = {}, {}, {}, {}
    for n, g_, d_, m_, v_ in zip(SMALL, _unpack(g_small_p, small_shapes), _unpack(d_small_p, small_shapes),
                                 _unpack(m_small_p, small_shapes), _unpack(v_small_p, small_shapes)):
        grad[n], delta[n], new_m[n], new_v[n] = g_, d_, m_, v_
    for n, g_ in zip(big_names, g_big):
        grad[n] = g_
        delta[n], new_m[n], new_v[n] = _adamw(w_loc[n], g_, m_loc[n], v_loc[n], 'adamw_' + n)
    return (loss, grad_x, *[grad[n] for n in WEIGHTS], *[delta[n] for n in WEIGHTS],
            *[new_m[n] for n in WEIGHTS], *[new_v[n] for n in WEIGHTS])
```

```python
import functools
import math

import numpy as np
import jax
import jax.numpy as jnp
from jax import lax
from jax.experimental import pallas as pl
from jax.experimental.pallas import tpu as pltpu

f32 = jnp.float32
bf16 = jnp.bfloat16
HI = lax.Precision.HIGHEST
MESH = pl.DeviceIdType.MESH

VMEM_LIMIT_BYTES = 48 * 1024 * 1024
MM_VMEM_BUDGET_BYTES = 32 * 1024 * 1024
LANES = 128

D_MODEL = 1024
DEPTH = 2
GROUP_W = 256
HEAD_DIM = 64
EPS = 1e-6
NEG_INF = -1e30
N_HEADS = 4
MLA_NOPE, MLA_ROPE = 64, 32
MLA_DQK = MLA_NOPE + MLA_ROPE
ROPE_THETA = 10000.0
Q_BLOCK = 128
S5_G, S5_CG, S5_P = 16, 16, 64
DIL_PAIRS = ((128, 1), (512, 4), (2048, 16))
T5_BUCKETS, T5_MAX_DIST = 32, 2048
DN_CHUNK = 64
FFN_HIDDEN = 2816
IN_SPLITS = (256, 128, 32, 256, 768, 768, 4, 4, 256)
IN_COLS = sum(IN_SPLITS)

ADAM_LR, ADAM_B1, ADAM_B2, ADAM_EPS, ADAM_WD, ADAM_STEP = 0.001, 0.9, 0.999, 1e-08, 0.01, 10

WEIGHTS = ['attn_norm', 'w_in', 'w_out', 'mla_q_norm', 'mla_kv_norm', 'mla_w_uq', 'mla_w_ukv', 'mla_qk_q', 'mla_qk_k',
           's5_lambda_re', 's5_lambda_im', 's5_log_dt', 's5_b_re', 's5_b_im', 's5_c_re', 's5_c_im', 's5_d', 's5_w_glu',
           'dil_q_norm', 'dil_k_norm', 't5_bias', 'dn_conv', 'dn_a_log', 'dn_dt_bias', 'dn_o_norm', 'ffn_norm',
           'ffn_w1', 'ffn_w3', 'ffn_w2']
BIG = {'w_in': 2, 'w_out': 1, 'mla_w_uq': 2, 'mla_w_ukv': 2, 's5_w_glu': 2, 'dn_conv': 2, 'ffn_w1': 2, 'ffn_w3': 2,
       'ffn_w2': 1}
SMALL = [n for n in WEIGHTS if n not in BIG]
GATHER_FIRST = ['w_in', 'mla_w_uq', 'mla_w_ukv', 's5_w_glu', 'dn_conv', 'w_out']
GATHER_FFN = ['ffn_w1', 'ffn_w3', 'ffn_w2']
N_SHARDS = 4
PACK_COLS = 1024


def _cparams(sem=None, big=False):
    kw = {}
    if sem is not None:
        kw['dimension_semantics'] = sem
    if big:
        kw['vmem_limit_bytes'] = VMEM_LIMIT_BYTES
    return pltpu.CompilerParams(**kw)


def _pick(n, prefs):
    for p in prefs:
        if p <= n and n % p == 0:
            return p
    return n


def _lane_tile(n, cap):
    for t in range(cap - cap % LANES, 0, -LANES):
        if n % t == 0:
            return t
    return n


def _mm(a, b, mode, name, add=None, out_dtype=f32):
    if mode == 'nn':
        (M, K), (K2, N) = a.shape, b.shape
    elif mode == 'nt':
        (M, K), (N, K2) = a.shape, b.shape
    else:
        (K, M), (K2, N) = a.shape, b.shape
    assert K == K2, (name, a.shape, b.shape)
    tk = K if K <= 2816 else _pick(K, (2816, 2048, 1408, 1024, 512))
    cap_m, cap_n = (1408 if mode == 'tn' else 512), 1408

    def need(tm_, tn_):
        per_step = tm_ * tk * a.dtype.itemsize + tk * tn_ * b.dtype.itemsize + tm_ * tn_ * jnp.dtype(out_dtype).itemsize
        if add is not None:
            per_step += tm_ * tn_ * add.dtype.itemsize
        return 2 * per_step + tm_ * tn_ * 4

    tm, tn = _lane_tile(M, cap_m), _lane_tile(N, cap_n)
    while need(tm, tn) > MM_VMEM_BUDGET_BYTES and cap_m > LANES:
        cap_m //= 2
        tm = _lane_tile(M, cap_m)
    nk = K // tk
    dims = {'nn': (((1,), (0,)), ((), ())), 'nt': (((1,), (1,)), ((), ())), 'tn': (((0,), (0,)), ((), ()))}[mode]
    has_add = add is not None

    def body(*refs):
        a_ref, b_ref = refs[0], refs[1]
        add_ref = refs[2] if has_add else None
        o_ref = refs[3] if has_add else refs[2]
        part = lax.dot_general(a_ref[...].astype(bf16), b_ref[...].astype(bf16), dims, preferred_element_type=f32)
        if nk == 1:
            if has_add:
                part = part + add_ref[...].astype(f32)
            o_ref[...] = part.astype(out_dtype)
        else:
            acc_ref = refs[-1]
            k = pl.program_id(2)

            @pl.when(k == 0)
            def _():
                acc_ref[...] = part

            @pl.when(k > 0)
            def _():
                acc_ref[...] += part

            @pl.when(k == nk - 1)
            def _():
                r = acc_ref[...]
                if has_add:
                    r = r + add_ref[...].astype(f32)
                o_ref[...] = r.astype(out_dtype)

    if mode == 'nn':
        a_spec = pl.BlockSpec((tm, tk), lambda i, j, k: (i, k))
        b_spec = pl.BlockSpec((tk, tn), lambda i, j, k: (k, j))
    elif mode == 'nt':
        a_spec = pl.BlockSpec((tm, tk), lambda i, j, k: (i, k))
        b_spec = pl.BlockSpec((tn, tk), lambda i, j, k: (j, k))
    else:
        a_spec = pl.BlockSpec((tk, tm), lambda i, j, k: (k, i))
        b_spec = pl.BlockSpec((tk, tn), lambda i, j, k: (k, j))
    in_specs = [a_spec, b_spec]
    args = [a, b]
    if has_add:
        in_specs.append(pl.BlockSpec((tm, tn), lambda i, j, k: (i, j)))
        args.append(add)
    return pl.pallas_call(
        body, name=name, grid=(M // tm, N // tn, nk), in_specs=in_specs,
        out_specs=pl.BlockSpec((tm, tn), lambda i, j, k: (i, j)),
        out_shape=jax.ShapeDtypeStruct((M, N), out_dtype),
        scratch_shapes=[pltpu.VMEM((tm, tn), f32)] if nk > 1 else [],
        compiler_params=_cparams(('parallel', 'parallel', 'arbitrary'), big=True),
    )(*args)


def _full_spec(p):
    nd = p.ndim
    return pl.BlockSpec(p.shape, lambda i, _nd=nd: (0,) * _nd)


def _tile_fwd(f, tiled, params, outs, tm, name):
    S = tiled[0].shape[0]
    nt, npar = len(tiled), len(params)

    def body(*refs):
        vals = [r[...].astype(f32) for r in refs[:nt + npar]]
        res = f(*vals)
        for r, o in zip(res, refs[nt + npar:]):
            o[...] = r.astype(o.dtype)

    return pl.pallas_call(
        body, name=name, grid=(S // tm,),
        in_specs=[pl.BlockSpec((tm, t.shape[1]), lambda i: (i, 0)) for t in tiled] + [_full_spec(p) for p in params],
        out_specs=[pl.BlockSpec((tm, c), lambda i: (i, 0)) for c, _ in outs],
        out_shape=[jax.ShapeDtypeStruct((S, c), dt) for c, dt in outs],
        compiler_params=_cparams(('parallel',), big=True),
    )(*tiled, *params)


def _tile_bwd(f, tiled, params, cts, diff_t, diff_p, tm, name, dt_dtypes=None):
    S = tiled[0].shape[0]
    nt, npar, nc = len(tiled), len(params), len(cts)
    it = [i for i in range(nt) if diff_t[i]]
    ip = [i for i in range(npar) if diff_p[i]]
    if dt_dtypes is None:
        dt_dtypes = [f32] * len(it)

    def body(*refs):
        vals = [r[...].astype(f32) for r in refs[:nt + npar]]
        ct_vals = tuple(r[...].astype(f32) for r in refs[nt + npar:nt + npar + nc])
        out_refs = refs[nt + npar + nc:]

        def g(*dv):
            full = list(vals)
            for k, i in enumerate(it):
                full[i] = dv[k]
            for k, i in enumerate(ip):
                full[nt + i] = dv[len(it) + k]
            return tuple(f(*full))

        _, vjp = jax.vjp(g, *[vals[i] for i in it], *[vals[nt + i] for i in ip])
        grads = vjp(ct_vals)
        for k in range(len(it)):
            out_refs[k][...] = grads[k].astype(out_refs[k].dtype)
        step = pl.program_id(0)
        for k in range(len(ip)):
            o = out_refs[len(it) + k]
            gk = grads[len(it) + k]

            @pl.when(step == 0)
            def _(o=o, gk=gk):
                o[...] = gk

            @pl.when(step > 0)
            def _(o=o, gk=gk):
                o[...] += gk

    out_specs = [pl.BlockSpec((tm, tiled[i].shape[1]), lambda i_: (i_, 0)) for i in it] + [_full_spec(params[i]) for i in ip]
    out_shape = [jax.ShapeDtypeStruct(tiled[i].shape, dt_dtypes[k]) for k, i in enumerate(it)] + \
                [jax.ShapeDtypeStruct(params[i].shape, f32) for i in ip]
    res = pl.pallas_call(
        body, name=name, grid=(S // tm,),
        in_specs=[pl.BlockSpec((tm, t.shape[1]), lambda i: (i, 0)) for t in tiled] + [_full_spec(p) for p in params] +
                 [pl.BlockSpec((tm, c.shape[1]), lambda i: (i, 0)) for c in cts],
        out_specs=out_specs, out_shape=out_shape,
        compiler_params=_cparams(('arbitrary',), big=True),
    )(*tiled, *params, *cts)
    return list(res[:len(it)]), list(res[len(it):])


def _rms(x, g):
    return x * lax.rsqrt(jnp.mean(x * x, axis=-1, keepdims=True) + EPS) * g


def _f_rms(x, g):
    return (_rms(x, g),)


def _f_swiglu(uv):
    h = uv.shape[1] // 2
    u, v = uv[:, :h], uv[:, h:]
    return (u * jax.nn.sigmoid(u) * v,)


def _loss_head(y, target):
    S, D = y.shape
    tm = _pick(S, (256, 128))

    def body(y_ref, t_ref, part_ref, dy_ref):
        e = y_ref[...] - t_ref[...]
        dy_ref[...] = e * (1.0 / D)
        s = 0.5 * jnp.sum(jnp.sum(e * e, axis=1, keepdims=True), axis=0, keepdims=True) * (1.0 / D)
        r = lax.broadcasted_iota(jnp.int32, (8, LANES), 0)
        c = lax.broadcasted_iota(jnp.int32, (8, LANES), 1)
        part_ref[0] = jnp.where((r == 0) & (c == 0), s, 0.0)

    return pl.pallas_call(
        body, name='loss_head', grid=(S // tm,),
        in_specs=[pl.BlockSpec((tm, D), lambda i: (i, 0))] * 2,
        out_specs=[pl.BlockSpec((1, 8, LANES), lambda i: (i, 0, 0)), pl.BlockSpec((tm, D), lambda i: (i, 0))],
        out_shape=[jax.ShapeDtypeStruct((S // tm, 8, LANES), f32), jax.ShapeDtypeStruct((S, D), f32)],
        compiler_params=_cparams(('parallel',)),
    )(y, target)


def _pack_rows_of(shape):
    rows = -(-math.prod(shape) // PACK_COLS)
    return -(-rows // 8) * 8


def _pack(arrs):
    parts = []
    for a in arrs:
        rows = _pack_rows_of(a.shape)
        flat = a.astype(f32).reshape(-1)
        parts.append(jnp.pad(flat, (0, rows * PACK_COLS - flat.shape[0])).reshape(rows, PACK_COLS))
    return jnp.concatenate(parts, axis=0)


def _unpack(pack, shapes):
    out, row = [], 0
    for s in shapes:
        rows = _pack_rows_of(s)
        out.append(pack[row:row + rows].reshape(-1)[:math.prod(s)].reshape(s))
        row += rows
    return out


ANY = pl.BlockSpec(memory_space=pl.ANY)


def _place():
    return lax.axis_index('x'), lax.axis_index('y'), lax.axis_index('c')


def _where():
    return jnp.stack([lax.axis_index('c'), 2 * lax.axis_index('x') + lax.axis_index('y')]).astype(jnp.int32)


def _remote(src, dst, send_sems, recv_sems, k, to):
    return pltpu.make_async_remote_copy(src_ref=src, dst_ref=dst, send_sem=send_sems.at[k], recv_sem=recv_sems.at[k],
                                        device_id=to, device_id_type=MESH)


def _swap_with_sibling(gs, small):
    n = len(gs)

    def body(*refs):
        g_refs, s_ref = refs[:n], refs[n]
        r_refs, rs_ref = refs[n + 1:2 * n + 1], refs[2 * n + 1]
        send_sems, recv_sems = refs[2 * n + 2:]
        x, y, c = _place()
        sib = (x, y, 1 - c)
        cps = [_remote(g_refs[t].at[:, 1 - c], r_refs[t], send_sems, recv_sems, t, sib) for t in range(n)]
        cps.append(_remote(s_ref, rs_ref, send_sems, recv_sems, n, sib))
        for cp in cps:
            cp.start()
        for cp in cps:
            cp.wait()

    res = pl.pallas_call(
        body, name='swap_with_sibling', in_specs=[ANY] * (n + 1), out_specs=[ANY] * (n + 1),
        out_shape=[jax.ShapeDtypeStruct((N_SHARDS,) + g.shape[2:], g.dtype) for g in gs] +
                  [jax.ShapeDtypeStruct(small.shape, small.dtype)],
        scratch_shapes=[pltpu.SemaphoreType.DMA((n + 1,)), pltpu.SemaphoreType.DMA((n + 1,))],
    )(*gs, small)
    return list(res[:n]), res[n]


def _exchange_between_chips(cs, small):
    n = len(cs)

    def body(*refs):
        c_refs, s_ref = refs[:n], refs[n]
        r_refs, rs_ref = refs[n + 1:2 * n + 1], refs[2 * n + 1]
        send_sems, recv_sems = refs[2 * n + 2:]
        x, y, c = _place()
        chips = [(1 - x, y), (x, 1 - y), (1 - x, 1 - y)]
        cps = []
        for j, (px, py) in enumerate(chips):
            for t in range(n):
                cps.append(_remote(c_refs[t].at[2 * px + py], r_refs[t].at[j], send_sems, recv_sems, 3 * t + j, (px, py, c)))
            cps.append(_remote(s_ref, rs_ref.at[j], send_sems, recv_sems, 3 * n + j, (px, py, c)))
        for cp in cps:
            cp.start()
        for cp in cps:
            cp.wait()

    res = pl.pallas_call(
        body, name='exchange_between_chips', in_specs=[ANY] * (n + 1), out_specs=[ANY] * (n + 1),
        out_shape=[jax.ShapeDtypeStruct((3,) + c.shape[1:], c.dtype) for c in cs] +
                  [jax.ShapeDtypeStruct((3,) + small.shape, small.dtype)],
        scratch_shapes=[pltpu.SemaphoreType.DMA((3 * n + 3,)), pltpu.SemaphoreType.DMA((3 * n + 3,))],
    )(*cs, small)
    return list(res[:n]), res[n]


def _swap_partials(ts):
    n = len(ts)

    def body(*refs):
        t_refs, o_refs = refs[:n], refs[n:2 * n]
        send_sems, recv_sems = refs[2 * n:]
        x, y, c = _place()
        cps = [_remote(t_refs[t], o_refs[t], send_sems, recv_sems, t, (x, y, 1 - c)) for t in range(n)]
        for cp in cps:
            cp.start()
        for cp in cps:
            cp.wait()

    return pl.pallas_call(
        body, name='swap_partials', in_specs=[ANY] * n, out_specs=[ANY] * n,
        out_shape=[jax.ShapeDtypeStruct(t.shape, t.dtype) for t in ts],
        scratch_shapes=[pltpu.SemaphoreType.DMA((n,)), pltpu.SemaphoreType.DMA((n,))],
    )(*ts)


HBM = pl.BlockSpec(memory_space=pltpu.HBM)
SEM = pl.BlockSpec(memory_space=pltpu.SEMAPHORE)
DATAFLOW = pltpu.SideEffectType.DATAFLOW_SIDE_EFFECTING


def _in_hbm(t):
    return pltpu.with_memory_space_constraint(t, pltpu.HBM)


def _other_chips():
    x, y, c = _place()
    return [(1 - x, y, c), (x, 1 - y, c), (1 - x, 1 - y, c)]


def _to_chips_copies(src_refs, land_refs, send_sems, recv_sems, per_peer):
    x, y, _ = _place()
    cps = []
    for t, (src, land) in enumerate(zip(src_refs, land_refs)):
        for j, (px, py, pc) in enumerate(_other_chips()):
            s = src.at[2 * px + py] if per_peer else src
            d = land.at[j] if per_peer else land.at[2 * x + y]
            cps.append(_remote(s, d, send_sems, recv_sems, 3 * t + j, (px, py, pc)))
    return cps


def _to_chips_start(srcs, lands, per_peer, order, name):
    n = len(srcs)

    def body(*refs):
        src_refs, land_refs = refs[:n], refs[n:2 * n]
        send_sems, recv_sems = refs[2 * n + 1], refs[2 * n + 2]
        token = refs[-1]
        for cp in _to_chips_copies(src_refs, land_refs, send_sems, recv_sems, per_peer):
            cp.start()
        token[...] = jnp.zeros_like(token)

    res = pl.pallas_call(
        body, name=name, in_specs=[HBM] * (2 * n) + [ANY],
        out_specs=[SEM, SEM] + [HBM] * (2 * n) + [pl.BlockSpec(memory_space=pltpu.VMEM)],
        out_shape=[pltpu.SemaphoreType.DMA((3 * n,)), pltpu.SemaphoreType.DMA((3 * n,))] +
                  [pltpu.HBM(t.shape, t.dtype) for t in srcs] + [pltpu.HBM(t.shape, t.dtype) for t in lands] +
                  [jax.ShapeDtypeStruct((8, LANES), f32)],
        input_output_aliases={i: 2 + i for i in range(2 * n)},
        compiler_params=pltpu.CompilerParams(has_side_effects=DATAFLOW),
    )(*[_in_hbm(t) for t in srcs], *[_in_hbm(t) for t in lands], order)
    return res[0], res[1], list(res[2:2 + n]), list(res[2 + n:2 + 2 * n]), res[-1]


def _to_chips_wait(send_sems, recv_sems, srcs, lands, per_peer, after, name):
    n = len(srcs)

    def body(*refs):
        src_refs, land_refs = refs[:n], refs[n:2 * n]
        send_ref, recv_ref = refs[2 * n], refs[2 * n + 1]
        for cp in _to_chips_copies(src_refs, land_refs, send_ref, recv_ref, per_peer):
            cp.wait_send()
            cp.wait_recv()

    res = pl.pallas_call(
        body, name=name, in_specs=[HBM] * (2 * n) + [SEM, SEM, ANY],
        out_specs=[HBM] * (2 * n),
        out_shape=[pltpu.HBM(t.shape, t.dtype) for t in srcs] + [pltpu.HBM(t.shape, t.dtype) for t in lands],
        input_output_aliases={i: i for i in range(2 * n)},
        compiler_params=pltpu.CompilerParams(has_side_effects=DATAFLOW),
    )(*srcs, *lands, send_sems, recv_sems, after)
    return list(res[n:])


def _row_tile(a):
    return _pick(a, (512, 256, 128, 64, 32, 16, 8))


def _partial_sum(g, land, name):
    _, a, b = g.shape
    tr = _row_tile(a)

    def body(w_ref, g_ref, r_ref, o_ref):
        t = g_ref[0].astype(f32) + r_ref[0].astype(f32)
        t = t + r_ref[1].astype(f32)
        t = t + r_ref[2].astype(f32)
        o_ref[...] = t.astype(o_ref.dtype)

    return pl.pallas_call(
        body, name=name,
        grid_spec=pltpu.PrefetchScalarGridSpec(
            num_scalar_prefetch=1, grid=(a // tr,),
            in_specs=[pl.BlockSpec((1, tr, b), lambda i, w: (w[1], i, 0)), pl.BlockSpec((3, tr, b), lambda i, w: (0, i, 0))],
            out_specs=pl.BlockSpec((tr, b), lambda i, w: (i, 0))),
        out_shape=jax.ShapeDtypeStruct((a, b), bf16),
        compiler_params=_cparams(('parallel',)),
    )(_where(), g, land)


def _by_shard(name, t):
    r, c = t.shape
    if BIG[name] == 2:
        return t.reshape(r, N_SHARDS, c // N_SHARDS).transpose(1, 0, 2)
    return t.reshape(N_SHARDS, r // N_SHARDS, c)


def _from_shards(name, g):
    s, a, b = g.shape
    if BIG[name] == 2:
        return g.transpose(1, 0, 2).reshape(a, s * b)
    return g.reshape(s * a, b)


def _adam_math(w, g, m, v):
    m = ADAM_B1 * m + (1.0 - ADAM_B1) * g
    v = ADAM_B2 * v + (1.0 - ADAM_B2) * (g * g)
    m_hat = m / (1.0 - ADAM_B1 ** ADAM_STEP)
    v_hat = v / (1.0 - ADAM_B2 ** ADAM_STEP)
    delta = -ADAM_LR * (m_hat / (jnp.sqrt(v_hat) + ADAM_EPS) + ADAM_WD * w)
    return delta, m, v


def _small_update(own, sib, chips, w, m, v):
    def body(o_ref, s_ref, c_ref, w_ref, m_ref, v_ref, g_out, d_out, m_out, v_out):
        chip = o_ref[...] + s_ref[...]
        g = (chip + c_ref[0]) + (c_ref[1] + c_ref[2])
        d, mn, vn = _adam_math(w_ref[...], g, m_ref[...], v_ref[...])
        g_out[...] = g
        d_out[...] = d
        m_out[...] = mn
        v_out[...] = vn

    return pl.pallas_call(body, name='small_update', out_shape=[jax.ShapeDtypeStruct(own.shape, f32)] * 4)(
        own, sib, chips, w, m, v)


def _small_chip_sum(own, sib):
    def body(o_ref, s_ref, out):
        out[...] = o_ref[...] + s_ref[...]
    return pl.pallas_call(body, name='small_chip_sum', out_shape=jax.ShapeDtypeStruct(own.shape, f32))(own, sib)


def _adamw(w, m, v, mine, theirs, name):
    layers, a, b = w.shape
    tr = _row_tile(a)

    def body(w_ref, m_ref, v_ref, p0, p1, q0, q1, g_out, d_out, m_out, v_out):
        first = pl.program_id(0) == 0
        g = jnp.where(first, p0[...].astype(f32) + q0[...].astype(f32), p1[...].astype(f32) + q1[...].astype(f32))
        d, mn, vn = _adam_math(w_ref[0], g, m_ref[0], v_ref[0])
        g_out[0] = g
        d_out[0] = d
        m_out[0] = mn
        v_out[0] = vn

    full = pl.BlockSpec((1, tr, b), lambda l, i: (l, i, 0))
    part = pl.BlockSpec((tr, b), lambda l, i: (i, 0))
    return pl.pallas_call(body, name=name, grid=(layers, a // tr), in_specs=[full] * 3 + [part] * 4, out_specs=[full] * 4,
                          out_shape=[jax.ShapeDtypeStruct(w.shape, f32)] * 4,
                          compiler_params=_cparams(('parallel', 'parallel')))(w, m, v, *mine, *theirs)


def _dg(a, b, ca, cb):
    return lax.dot_general(a.astype(bf16), b.astype(bf16), (((ca,), (cb,)), ((), ())), preferred_element_type=f32)


@jax.custom_vjp
def _bmm(a, b):
    return _dg(a, b, 1, 0)


_bmm.defvjp(lambda a, b: (_dg(a, b, 1, 0), (a, b)), lambda r, g: (_dg(g, r[1], 1, 1), _dg(r[0], g, 0, 0)))


@jax.custom_vjp
def _bmm_nt(a, b):
    return _dg(a, b, 1, 1)


_bmm_nt.defvjp(lambda a, b: (_dg(a, b, 1, 1), (a, b)), lambda r, g: (_dg(g, r[1], 1, 0), _dg(g, r[0], 0, 0)))


@jax.custom_vjp
def _bmm_tn(a, b):
    return _dg(a, b, 0, 0)


_bmm_tn.defvjp(lambda a, b: (_dg(a, b, 0, 0), (a, b)), lambda r, g: (_dg(r[1], g, 1, 1), _dg(r[0], g, 1, 0)))


def _hdot(a, b):
    return jnp.dot(a, b, precision=HI, preferred_element_type=f32)


def _hdot_nt(a, b):
    return lax.dot_general(a, b, (((1,), (1,)), ((), ())), precision=HI, preferred_element_type=f32)


def _hdot_tn(a, b):
    return lax.dot_general(a, b, (((0,), (0,)), ((), ())), precision=HI, preferred_element_type=f32)


def _head_mask(h, width=GROUP_W):
    lane = lax.broadcasted_iota(jnp.int32, (1, width), 1)
    return ((lane >= h * HEAD_DIM) & (lane < (h + 1) * HEAD_DIM)).astype(f32)


def _rope_perm():
    p = np.zeros((LANES, LANES), np.float32)
    half = MLA_ROPE // 2
    for i in range(half):
        p[MLA_NOPE + half + i, MLA_NOPE + i] = -1.0
        p[MLA_NOPE + i, MLA_NOPE + half + i] = 1.0
    return jnp.asarray(p)


def _rope_tables(S):
    half = MLA_ROPE // 2
    freqs = ROPE_THETA ** (-jnp.arange(half, dtype=f32) / half)
    ang = jnp.arange(S, dtype=f32)[:, None] * freqs[None, :]
    cos, sin = jnp.cos(ang), jnp.sin(ang)
    ones, zeros = jnp.ones((S, MLA_NOPE), f32), jnp.zeros((S, LANES - MLA_DQK), f32)
    c_tab = jnp.concatenate([ones, cos, cos, zeros], axis=1)
    s_tab = jnp.concatenate([jnp.zeros((S, MLA_NOPE), f32), sin, sin, zeros], axis=1)
    return c_tab, s_tab


def _f_mla_pre(c_q, c_kv, krope, c_tab, s_tab, q_norm, kv_norm, wq0, wq1, wq2, wq3, wk0, wk1, wk2, wk3, wv, gq, gk, perm):
    wq, wk = (wq0, wq1, wq2, wq3), (wk0, wk1, wk2, wk3)
    nq = _rms(c_q, q_norm)
    nkv = _rms(c_kv, kv_norm)

    def norm_rope(t, g):
        t = t * lax.rsqrt(jnp.sum(t * t, axis=-1, keepdims=True) * (1.0 / MLA_DQK) + EPS) * g
        return t * c_tab + _hdot(t, perm) * s_tab

    qs = [norm_rope(_bmm(nq, wq[h]), gq) * (MLA_DQK ** -0.5) for h in range(N_HEADS)]
    ks = [norm_rope(_bmm(nkv, wk[h]) + krope, gk) for h in range(N_HEADS)]
    return (*qs, *ks, _bmm(nkv, wv))


def _f_attn(qs, ks, v, q0):
    tq, S = qs[0].shape[0], ks[0].shape[0]
    qpos = q0 + lax.broadcasted_iota(jnp.int32, (tq, S), 0)
    kpos = lax.broadcasted_iota(jnp.int32, (tq, S), 1)
    keep = kpos <= qpos
    logits = [jnp.where(keep, _bmm_nt(qs[h], ks[h]), NEG_INF) for h in range(N_HEADS)]
    ps = [jnp.exp(lg - jnp.max(lg, axis=-1, keepdims=True)) for lg in logits]
    ps = [p / jnp.sum(p, axis=-1, keepdims=True) for p in ps]
    return sum(_bmm(p, v) * _head_mask(h) for h, p in enumerate(ps))


ATTN_PARTS = 4


def _mla_attn_fwd(qs, ks, v, name):
    S = v.shape[0]
    tq = Q_BLOCK
    parts = ATTN_PARTS if S % (ATTN_PARTS * tq) == 0 else 1
    per = S // parts
    outs = []
    for p in range(parts):
        n_keys = (p + 1) * per
        first_block = p * (per // tq)

        def body(*refs, first_block=first_block):
            q_vals = [r[...] for r in refs[:4]]
            k_vals = [r[...] for r in refs[4:8]]
            refs[9][...] = _f_attn(q_vals, k_vals, refs[8][...], (first_block + pl.program_id(0)) * tq)

        qspec = pl.BlockSpec((tq, LANES), lambda i, fb=first_block: (fb + i, 0))
        outs.append(pl.pallas_call(
            body, name=f'{name}_{p}', grid=(per // tq,),
            in_specs=[qspec] * 4 + [pl.BlockSpec((n_keys, LANES), lambda i: (0, 0))] * 4 +
                     [pl.BlockSpec((n_keys, GROUP_W), lambda i: (0, 0))],
            out_specs=pl.BlockSpec((tq, GROUP_W), lambda i: (i, 0)),
            out_shape=jax.ShapeDtypeStruct((per, GROUP_W), f32),
            compiler_params=_cparams(('parallel',), big=True),
        )(*qs, *ks, v))
    return jnp.concatenate(outs, axis=0)


def _mla_attn_bwd(qs, ks, v, do, name):
    S = v.shape[0]
    tq = Q_BLOCK
    parts = ATTN_PARTS if S % (ATTN_PARTS * tq) == 0 else 1
    per = S // parts
    dq_parts, dkv_sum = [], None
    for p in range(parts):
        n_keys = (p + 1) * per
        first_block = p * (per // tq)

        def body(*refs, first_block=first_block):
            q_vals = [r[...].astype(f32) for r in refs[:4]]
            k_vals = [r[...].astype(f32) for r in refs[4:8]]
            v_val = refs[8][...].astype(f32)
            q0 = (first_block + pl.program_id(0)) * tq
            _, vjp = jax.vjp(lambda a, b, c: _f_attn(a, b, c, q0), q_vals, k_vals, v_val)
            dqs, dks, dv = vjp(refs[9][...])
            outs = refs[10:]
            for h in range(N_HEADS):
                outs[h][...] = dqs[h]
            first = pl.program_id(0) == 0
            for o, g in zip(outs[4:], (*dks, dv)):
                @pl.when(first)
                def _(o=o, g=g):
                    o[...] = g

                @pl.when(jnp.logical_not(first))
                def _(o=o, g=g):
                    o[...] += g

        qspec = pl.BlockSpec((tq, LANES), lambda i, fb=first_block: (fb + i, 0))
        kspec = pl.BlockSpec((n_keys, LANES), lambda i: (0, 0))
        vspec = pl.BlockSpec((n_keys, GROUP_W), lambda i: (0, 0))
        res = pl.pallas_call(
            body, name=f'{name}_{p}', grid=(per // tq,),
            in_specs=[qspec] * 4 + [kspec] * 4 + [vspec, pl.BlockSpec((tq, GROUP_W), lambda i, fb=first_block: (fb + i, 0))],
            out_specs=[pl.BlockSpec((tq, LANES), lambda i: (i, 0))] * 4 + [kspec] * 4 + [vspec],
            out_shape=[jax.ShapeDtypeStruct((per, LANES), f32)] * 4 + [jax.ShapeDtypeStruct((n_keys, LANES), f32)] * 4 +
                      [jax.ShapeDtypeStruct((n_keys, GROUP_W), f32)],
            compiler_params=_cparams(('arbitrary',), big=True),
        )(*qs, *ks, v, do)
        dq_parts.append(res[:4])
        dkv = [jnp.pad(t, ((0, S - n_keys), (0, 0))) for t in res[4:]]
        dkv_sum = dkv if dkv_sum is None else [a_ + b_ for a_, b_ in zip(dkv_sum, dkv)]
    dqs = [jnp.concatenate([dq_parts[p][h] for p in range(parts)], axis=0) for h in range(N_HEADS)]
    return dqs, dkv_sum[:4], dkv_sum[4]


def _mla_params(mp):
    pad = LANES - MLA_DQK
    wq = jnp.pad(mp['mla_w_uq'].reshape(GROUP_W, N_HEADS, MLA_DQK).transpose(1, 0, 2), ((0, 0), (0, 0), (0, pad)))
    wkv = mp['mla_w_ukv'].reshape(LANES, N_HEADS, MLA_NOPE + HEAD_DIM)
    wk = jnp.pad(wkv[:, :, :MLA_NOPE].transpose(1, 0, 2), ((0, 0), (0, 0), (0, LANES - MLA_NOPE)))
    wv = wkv[:, :, MLA_NOPE:].reshape(LANES, GROUP_W)
    gq = jnp.pad(mp['mla_qk_q'], (0, pad))[None]
    gk = jnp.pad(mp['mla_qk_k'], (0, pad))[None]
    return [mp['mla_q_norm'][None], mp['mla_kv_norm'][None], *[wq[h] for h in range(N_HEADS)],
            *[wk[h] for h in range(N_HEADS)], wv, gq, gk, _rope_perm()]


def _mla_fwd(c_q, c_kv, k_rope, mp, l):
    S = c_q.shape[0]
    tm = _pick(S, (256, 128))
    krope = jnp.pad(k_rope, ((0, 0), (MLA_NOPE, LANES - MLA_DQK)))
    c_tab, s_tab = _rope_tables(S)
    tiled = [c_q, c_kv, krope, c_tab, s_tab]
    params = _mla_params(mp)
    res = _tile_fwd(_f_mla_pre, tiled, params, [(LANES, bf16)] * 8 + [(GROUP_W, bf16)], tm, f'mla_pre_fwd_{l}')
    qs, ks, v = res[:4], res[4:8], res[8]
    y = _mla_attn_fwd(qs, ks, v, f'mla_attn_fwd_{l}')
    return y, (tiled, params, qs, ks, v)


def _mla_bwd(dy, saved, l):
    tiled, params, qs, ks, v = saved
    S = dy.shape[0]
    tm = _pick(S, (256, 128))
    dqs, dks, dv = _mla_attn_bwd(qs, ks, v, dy, f'mla_attn_bwd_{l}')
    (dc_q, dc_kv, dkrope), dpar = _tile_bwd(_f_mla_pre, tiled, params, [*dqs, *dks, dv], [True, True, True, False, False],
                                            [True] * 13 + [False], tm, f'mla_pre_bwd_{l}')
    dqn, dkvn = dpar[0], dpar[1]
    dwq, dwk = jnp.stack(dpar[2:6]), jnp.stack(dpar[6:10])
    dwv, dgq, dgk = dpar[10:13]
    dw_uq = dwq[:, :, :MLA_DQK].transpose(1, 0, 2).reshape(GROUP_W, N_HEADS * MLA_DQK)
    dw_ukv = jnp.concatenate([dwk[:, :, :MLA_NOPE].transpose(1, 0, 2), dwv.reshape(LANES, N_HEADS, HEAD_DIM)],
                             axis=2).reshape(LANES, N_HEADS * (MLA_NOPE + HEAD_DIM))
    grads = {'mla_q_norm': dqn[0], 'mla_kv_norm': dkvn[0], 'mla_w_uq': dw_uq, 'mla_w_ukv': dw_ukv,
             'mla_qk_q': dgq[0, :MLA_DQK], 'mla_qk_k': dgk[0, :MLA_DQK]}
    return dc_q, dc_kv, dkrope[:, MLA_NOPE:MLA_DQK], grads


SPAN = 128


def _head_mean_matrix():
    h = np.arange(GROUP_W) // HEAD_DIM
    return jnp.asarray((h[:, None] == h[None, :]).astype(np.float32) / HEAD_DIM)


def _f_dil_pre(q, k, gq, gk, hm):
    qn = q * lax.rsqrt(_hdot(q * q, hm) + EPS) * gq * (HEAD_DIM ** -0.5)
    kn = k * lax.rsqrt(_hdot(k * k, hm) + EPS) * gk
    return qn, kn


def _f_dil_branch(qb, kp, kc, vp, vc, b0, b1, b2, b3, first):
    kcat = jnp.concatenate([kp, kc], axis=0)
    vcat = jnp.concatenate([vp, vc], axis=0)
    qi = lax.broadcasted_iota(jnp.int32, (SPAN, 2 * SPAN), 0) + SPAN
    kj = lax.broadcasted_iota(jnp.int32, (SPAN, 2 * SPAN), 1)
    delta = qi - kj
    valid = (delta >= 0) & (delta <= SPAN) & jnp.logical_not(first & (kj < SPAN))
    masks = [_head_mask(h) for h in range(N_HEADS)]
    raw = [_bmm_nt(qb * hm, kcat) for hm in masks]
    logits = [jnp.where(valid, r + bias, NEG_INF) for r, bias in zip(raw, (b0, b1, b2, b3))]
    ms = [jnp.max(lg, axis=-1, keepdims=True) for lg in logits]
    ps = [jnp.exp(lg - m) for lg, m in zip(logits, ms)]
    pvs = [_bmm(p, vcat) for p in ps]
    o = sum(pv * hm for pv, hm in zip(pvs, masks))
    m_full = sum(m * hm for m, hm in zip(ms, masks))
    l_full = sum(jnp.sum(p, axis=-1, keepdims=True) * hm for p, hm in zip(ps, masks))
    return o, m_full, l_full


def _dil_branch_specs(d, nb):
    cur = pl.BlockSpec((1, SPAN, GROUP_W), lambda r, n: (r, n, 0))
    prev = pl.BlockSpec((1, SPAN, GROUP_W), lambda r, n: (r, jnp.maximum(n - 1, 0), 0))
    bias = pl.BlockSpec((1, SPAN, 2 * SPAN), lambda r, n: (0, 0, 0))
    return cur, prev, bias


def _head_table_specs():
    return [pl.BlockSpec((1, SPAN, 2 * SPAN), lambda r, n, h=h: (h, 0, 0)) for h in range(N_HEADS)]


def _dil_branch_fwd(q, k, v, table, name):
    d, L, _ = q.shape
    nb = L // SPAN
    cur, prev, bias = _dil_branch_specs(d, nb)

    def body(q_ref, kp_ref, kc_ref, vp_ref, vc_ref, b0, b1, b2, b3, o_ref, m_ref, l_ref):
        o, m, l = _f_dil_branch(q_ref[0], kp_ref[0], kc_ref[0], vp_ref[0], vc_ref[0], b0[0], b1[0], b2[0], b3[0],
                                pl.program_id(1) == 0)
        o_ref[0] = o
        m_ref[0] = m
        l_ref[0] = l

    return pl.pallas_call(
        body, name=name, grid=(d, nb), in_specs=[cur, prev, cur, prev, cur] + _head_table_specs(),
        out_specs=[cur] * 3, out_shape=[jax.ShapeDtypeStruct(q.shape, f32)] * 3,
        compiler_params=_cparams(('parallel', 'parallel')),
    )(q, k, k, v, v, *[table] * N_HEADS)


def _dil_branch_bwd(q, k, v, table, do, dm, dl, name):
    d, L, _ = q.shape
    nb = L // SPAN
    cur, prev, bias = _dil_branch_specs(d, nb)
    whole = pl.BlockSpec((1, L, GROUP_W), lambda r, n: (r, 0, 0))

    def body(q_ref, kp_ref, kc_ref, vp_ref, vc_ref, b0, b1, b2, b3, do_ref, dm_ref, dl_ref,
             dq_ref, dk_ref, dv_ref, db0, db1, db2, db3):
        r, n = pl.program_id(0), pl.program_id(1)
        first = n == 0
        _, vjp = jax.vjp(lambda *a: _f_dil_branch(*a, first), q_ref[0], kp_ref[0], kc_ref[0], vp_ref[0], vc_ref[0],
                         b0[0], b1[0], b2[0], b3[0])
        dq, dkp, dkc, dvp, dvc, g0, g1, g2, g3 = vjp((do_ref[0], dm_ref[0], dl_ref[0]))
        dq_ref[0] = dq

        @pl.when(first)
        def _():
            dk_ref[...] = jnp.zeros_like(dk_ref)
            dv_ref[...] = jnp.zeros_like(dv_ref)

        rows = pl.ds(pl.multiple_of(n * SPAN, SPAN), SPAN)
        dk_ref[0, rows, :] += dkc
        dv_ref[0, rows, :] += dvc

        @pl.when(n > 0)
        def _():
            before = pl.ds(pl.multiple_of((n - 1) * SPAN, SPAN), SPAN)
            dk_ref[0, before, :] += dkp
            dv_ref[0, before, :] += dvp

        start = first & (r == 0)
        for o, g in zip((db0, db1, db2, db3), (g0, g1, g2, g3)):
            @pl.when(start)
            def _(o=o, g=g):
                o[0] = g

            @pl.when(jnp.logical_not(start))
            def _(o=o, g=g):
                o[0] += g

    res = pl.pallas_call(
        body, name=name, grid=(d, nb), in_specs=[cur, prev, cur, prev, cur] + _head_table_specs() + [cur] * 3,
        out_specs=[cur, whole, whole] + [bias] * 4,
        out_shape=[jax.ShapeDtypeStruct(q.shape, f32)] * 3 + [jax.ShapeDtypeStruct((1, SPAN, 2 * SPAN), f32)] * 4,
        compiler_params=_cparams(('arbitrary', 'arbitrary')),
    )(q, k, k, v, v, *[table] * N_HEADS, do, dm, dl)
    return res[0], res[1], res[2], res[3:]


def _f_dil_merge(o1, m1, l1, o2, m2, l2, o3, m3, l3):
    mx = jnp.maximum(jnp.maximum(m1, m2), m3)
    w1, w2, w3 = jnp.exp(m1 - mx), jnp.exp(m2 - mx), jnp.exp(m3 - mx)
    return ((w1 * o1 + w2 * o2 + w3 * o3) / (w1 * l1 + w2 * l2 + w3 * l3),)


def _bias_onehot(dilation):
    qi = jnp.arange(SPAN, dtype=jnp.int32)[:, None] + SPAN
    kj = jnp.arange(2 * SPAN, dtype=jnp.int32)[None, :]
    bucket = _t5_bucket(jnp.clip(qi - kj, 0, SPAN) * dilation).reshape(-1)
    return (bucket[None, :] == jnp.arange(T5_BUCKETS, dtype=jnp.int32)[:, None]).astype(f32)


def _bias_tables(t5_t, onehot, name):
    N = onehot.shape[1]
    tn = _pick(N, (4096, 2048, 1024))

    def body(t_ref, oh_ref, o_ref):
        o_ref[...] = _hdot(t_ref[...], oh_ref[...])

    return pl.pallas_call(
        body, name=name, grid=(N // tn,),
        in_specs=[pl.BlockSpec((8, T5_BUCKETS), lambda i: (0, 0)), pl.BlockSpec((T5_BUCKETS, tn), lambda i: (0, i))],
        out_specs=pl.BlockSpec((8, tn), lambda i: (0, i)), out_shape=jax.ShapeDtypeStruct((8, N), f32),
        compiler_params=_cparams(('parallel',)),
    )(t5_t, onehot)


def _bias_tables_bwd(d_tab, onehot, name):
    N = onehot.shape[1]
    tn = _pick(N, (4096, 2048, 1024))

    def body(g_ref, oh_ref, o_ref):
        part = _hdot_nt(g_ref[...], oh_ref[...])

        @pl.when(pl.program_id(0) == 0)
        def _():
            o_ref[...] = part

        @pl.when(pl.program_id(0) > 0)
        def _():
            o_ref[...] += part

    return pl.pallas_call(
        body, name=name, grid=(N // tn,),
        in_specs=[pl.BlockSpec((8, tn), lambda i: (0, i)), pl.BlockSpec((T5_BUCKETS, tn), lambda i: (0, i))],
        out_specs=pl.BlockSpec((8, T5_BUCKETS), lambda i: (0, 0)), out_shape=jax.ShapeDtypeStruct((8, T5_BUCKETS), f32),
        compiler_params=_cparams(('arbitrary',)),
    )(d_tab, onehot)


def _by_residue(t, d):
    S, C = t.shape
    return t.reshape(S // d, d, C).transpose(1, 0, 2)


def _from_residue(t):
    d, L, C = t.shape
    return t.transpose(1, 0, 2).reshape(d * L, C)


def _dil_fwd(qkv, mp, l):
    S = qkv.shape[0]
    tm = _pick(S, (256, 128))
    q, k, v = qkv[:, :GROUP_W], qkv[:, GROUP_W:2 * GROUP_W], qkv[:, 2 * GROUP_W:]
    pre_params = [jnp.tile(mp['dil_q_norm'], N_HEADS)[None], jnp.tile(mp['dil_k_norm'], N_HEADS)[None], _head_mean_matrix()]
    qn, kn = _tile_fwd(_f_dil_pre, [q, k], pre_params, [(GROUP_W, f32)] * 2, tm, f'dil_pre_fwd_{l}')
    t5_t = jnp.pad(mp['t5_bias'].T, ((0, 8 - N_HEADS), (0, 0)))
    branches, outs = [], []
    for bi, (_, d) in enumerate(DIL_PAIRS):
        onehot = _bias_onehot(d)
        tab = _bias_tables(t5_t, onehot, f'dil_bias_fwd_{l}_{bi}').reshape(8, SPAN, 2 * SPAN)
        qd, kd, vd = _by_residue(qn, d), _by_residue(kn, d), _by_residue(v, d)
        o, m, lsum = _dil_branch_fwd(qd, kd, vd, tab, f'dil_branch_fwd_{l}_{bi}')
        branches.append((qd, kd, vd, tab, onehot))
        outs += [_from_residue(o), _from_residue(m), _from_residue(lsum)]
    (y,) = _tile_fwd(_f_dil_merge, outs, [], [(GROUP_W, f32)], tm, f'dil_merge_fwd_{l}')
    return y, (q, k, pre_params, branches, outs)


def _dil_bwd(dy, saved, l):
    q, k, pre_params, branches, outs = saved
    S = dy.shape[0]
    tm = _pick(S, (256, 128))
    douts, _ = _tile_bwd(_f_dil_merge, outs, [], [dy], [True] * 9, [], tm, f'dil_merge_bwd_{l}')
    dqn = dkn = dv = None
    dt5_t = None
    for bi, (_, d) in enumerate(DIL_PAIRS):
        qd, kd, vd, tab, onehot = branches[bi]
        do, dm, dl = [_by_residue(t, d) for t in douts[3 * bi:3 * bi + 3]]
        dq_b, dk_b, dv_b, dbias = _dil_branch_bwd(qd, kd, vd, tab, do, dm, dl, f'dil_branch_bwd_{l}_{bi}')
        d_tab = jnp.concatenate([*dbias, jnp.zeros((8 - N_HEADS, SPAN, 2 * SPAN), f32)], axis=0).reshape(8, -1)
        g_t5 = _bias_tables_bwd(d_tab, onehot, f'dil_bias_bwd_{l}_{bi}')
        dq_b, dk_b, dv_b = _from_residue(dq_b), _from_residue(dk_b), _from_residue(dv_b)
        dqn = dq_b if dqn is None else dqn + dq_b
        dkn = dk_b if dkn is None else dkn + dk_b
        dv = dv_b if dv is None else dv + dv_b
        dt5_t = g_t5 if dt5_t is None else dt5_t + g_t5
    (dq, dk), (dgq, dgk) = _tile_bwd(_f_dil_pre, [q, k], pre_params, [dqn, dkn], [True, True], [True, True, False], tm,
                                     f'dil_pre_bwd_{l}')
    grads = {'dil_q_norm': dgq.reshape(N_HEADS, HEAD_DIM).sum(0), 'dil_k_norm': dgk.reshape(N_HEADS, HEAD_DIM).sum(0),
             't5_bias': dt5_t[:N_HEADS].T}
    return jnp.concatenate([dq, dk, dv], axis=1), grads


S5_LANES = S5_G * S5_P
SCAN_SEGMENTS = 8
SCAN_W = 256


def _f_s5_prep(bre, bim, lr, li, logdt_col, expand):
    dt = jnp.sum(jnp.exp(logdt_col) * expand, axis=0, keepdims=True)
    mag = jnp.exp(lr * dt)
    ar, ai = mag * jnp.cos(li * dt), mag * jnp.sin(li * dt)
    den = lr * lr + li * li
    nr, ni = ar - 1.0, ai
    zr = (nr * lr + ni * li) / den
    zi = (ni * lr - nr * li) / den
    bb = jnp.concatenate([zr * bre - zi * bim, zr * bim + zi * bre], axis=1)
    a_rows = jnp.broadcast_to(jnp.concatenate([ar, ai], axis=1), bb.shape)
    return bb, a_rows


def _s5_scan(x, a_rows, name, reverse=False, h=None):
    S = x.shape[0]
    NL = x.shape[1] // 2
    T = S // SCAN_SEGMENTS
    nblk = NL // SCAN_W
    n_in = 4 if reverse else 2

    def body(*refs):
        if reverse:
            (x_hbm, pr_hbm, pi_hbm, ar_ref, ai_ref, hr_hbm, hi_hbm, dar_ref, dai_ref,
             xr_s, xi_s, pr_s, pi_s, hr_s, hi_s, in_sems, out_sems) = refs
        else:
            x_hbm, ar_ref, ai_ref, hr_hbm, hi_hbm, xr_s, xi_s, hr_s, hi_s, in_sems, out_sems = refs
        col = pl.multiple_of(pl.program_id(0) * SCAN_W, SCAN_W)
        loads = []
        for k in range(SCAN_SEGMENTS):
            rows = pl.ds(k * T, T)
            sources = [(x_hbm, col, xr_s), (x_hbm, NL + col, xi_s)]
            if reverse:
                sources += [(pr_hbm, col, pr_s), (pi_hbm, col, pi_s)]
            for i, (src, c0, dst) in enumerate(sources):
                loads.append(pltpu.make_async_copy(src.at[rows, pl.ds(c0, SCAN_W)], dst.at[:, k, :],
                                                   in_sems.at[i * SCAN_SEGMENTS + k]))
        for cp in loads:
            cp.start()
        for cp in loads:
            cp.wait()
        ar = ar_ref[...]
        ai = -ai_ref[...] if reverse else ai_ref[...]
        zero = jnp.zeros((SCAN_SEGMENTS, SCAN_W), f32)

        def at(s):
            return T - 1 - s if reverse else s

        def local(s, c):
            hr, hi, pr, pi = c
            j = at(s)
            nhr = ar * hr - ai * hi + xr_s[j]
            nhi = ar * hi + ai * hr + xi_s[j]
            hr_s[j] = nhr
            hi_s[j] = nhi
            return nhr, nhi, ar * pr - ai * pi, ar * pi + ai * pr

        er, ei, pr, pi = lax.fori_loop(0, T, local, (zero, zero, zero + 1.0, zero), unroll=2)
        row = lax.broadcasted_iota(jnp.int32, (SCAN_SEGMENTS, SCAN_W), 0)
        cr, ci = zero, zero
        order = range(SCAN_SEGMENTS - 2, -1, -1) if reverse else range(1, SCAN_SEGMENTS)
        for k in order:
            src = k + 1 if reverse else k - 1
            tr = er + pr * cr - pi * ci
            ti = ei + pr * ci + pi * cr
            cr = jnp.where(row == k, jnp.sum(jnp.where(row == src, tr, 0.0), axis=0, keepdims=True), cr)
            ci = jnp.where(row == k, jnp.sum(jnp.where(row == src, ti, 0.0), axis=0, keepdims=True), ci)

        def fix_at(j, c, before):
            pr, pi, sr, si = c
            pr, pi = ar * pr - ai * pi, ar * pi + ai * pr
            hr = hr_s[j] + pr * cr - pi * ci
            hi = hi_s[j] + pr * ci + pi * cr
            hr_s[j] = hr
            hi_s[j] = hi
            if reverse:
                qr, qi = before
                sr = sr + hr * qr + hi * qi
                si = si + hi * qr - hr * qi
            return pr, pi, sr, si

        start = (zero + 1.0, zero, zero, zero)
        if reverse:
            def fix(s, c):
                j = T - 1 - s
                return fix_at(j, c, (pr_s[j - 1], pi_s[j - 1]))

            c = lax.fori_loop(0, T - 1, fix, start, unroll=2)
            last_r = jnp.where(row == 0, 0.0, pltpu.roll(pr_s[T - 1], 1, 0))
            last_i = jnp.where(row == 0, 0.0, pltpu.roll(pi_s[T - 1], 1, 0))
            _, _, sr, si = fix_at(0, c, (last_r, last_i))
            dar_ref[...] = sr
            dai_ref[...] = si
        else:
            lax.fori_loop(0, T, lambda s, c: fix_at(s, c, None), start, unroll=2)
        stores = []
        for k in range(SCAN_SEGMENTS):
            rows = pl.ds(k * T, T)
            stores.append(pltpu.make_async_copy(hr_s.at[:, k, :], hr_hbm.at[rows, pl.ds(col, SCAN_W)], out_sems.at[k]))
            stores.append(pltpu.make_async_copy(hi_s.at[:, k, :], hi_hbm.at[rows, pl.ds(col, SCAN_W)],
                                                out_sems.at[SCAN_SEGMENTS + k]))
        for cp in stores:
            cp.start()
        for cp in stores:
            cp.wait()

    a_re = pl.BlockSpec((SCAN_SEGMENTS, SCAN_W), lambda b: (0, b))
    a_im = pl.BlockSpec((SCAN_SEGMENTS, SCAN_W), lambda b: (0, nblk + b))
    seq = pltpu.VMEM((T, SCAN_SEGMENTS, SCAN_W), f32)
    if reverse:
        in_specs, args = [ANY, ANY, ANY, a_re, a_im], [x, h[0], h[1], a_rows, a_rows]
        out_specs = [ANY, ANY, a_re, a_re]
        out_shape = [jax.ShapeDtypeStruct((S, NL), f32)] * 2 + [jax.ShapeDtypeStruct((SCAN_SEGMENTS, NL), f32)] * 2
    else:
        in_specs, args = [ANY, a_re, a_im], [x, a_rows, a_rows]
        out_specs = [ANY, ANY]
        out_shape = [jax.ShapeDtypeStruct((S, NL), f32)] * 2
    scratch = [seq] * (n_in + 2) + [pltpu.SemaphoreType.DMA((n_in * SCAN_SEGMENTS,)),
                                    pltpu.SemaphoreType.DMA((2 * SCAN_SEGMENTS,))]
    return pl.pallas_call(body, name=name, grid=(nblk,), in_specs=in_specs, out_specs=out_specs, out_shape=out_shape,
                          scratch_shapes=scratch, compiler_params=_cparams(('arbitrary',), big=True))(*args)


def _f_s5_post(y, u, d, w_glu):
    z = _bmm(y + d * u, w_glu)
    return (z[:, :GROUP_W] * jax.nn.sigmoid(z[:, GROUP_W:]),)


def _block_diag(t):
    G, a, b = t.shape
    eye = jnp.eye(G, dtype=t.dtype)
    return (t[:, :, None, :] * eye[:, None, :, None]).reshape(G * a, G * b)


def _diag_blocks(m, a, b):
    G = m.shape[0] // a
    return jnp.moveaxis(jnp.diagonal(m.reshape(G, a, G, b), axis1=0, axis2=2), -1, 0)


def _s5_fwd(u, mp, l):
    S = u.shape[0]
    tm = _pick(S, (256, 128))
    bre = _block_diag(mp['s5_b_re'].transpose(0, 2, 1))
    bim = _block_diag(mp['s5_b_im'].transpose(0, 2, 1))
    expand = jnp.repeat(jnp.eye(S5_G, dtype=f32), S5_P, axis=1)
    prep_params = [mp['s5_lambda_re'].reshape(1, S5_LANES), mp['s5_lambda_im'].reshape(1, S5_LANES),
                   mp['s5_log_dt'].reshape(S5_G, 1), expand]
    bb, a_rows = _tile_fwd(_f_s5_prep, [bre, bim], prep_params, [(2 * S5_LANES, f32)] * 2, GROUP_W, f's5_prep_fwd_{l}')
    x = _mm(u, bb, 'nn', f's5_in_fwd_{l}')
    hr, hi = _s5_scan(x, a_rows, f's5_scan_fwd_{l}')
    c_re, c_im = _block_diag(mp['s5_c_re'].transpose(0, 2, 1)), -_block_diag(mp['s5_c_im'].transpose(0, 2, 1))
    y = _mm(hi, c_im, 'nn', f's5_out_im_fwd_{l}', add=_mm(hr, c_re, 'nn', f's5_out_re_fwd_{l}'))
    post_params = [mp['s5_d'][None], mp['s5_w_glu']]
    (out,) = _tile_fwd(_f_s5_post, [y, u], post_params, [(GROUP_W, f32)], tm, f's5_post_fwd_{l}')
    return out, (u, bre, bim, prep_params, bb, a_rows, hr, hi, c_re, c_im, y, post_params)


def _s5_bwd(dout, saved, l):
    u, bre, bim, prep_params, bb, a_rows, hr, hi, c_re, c_im, y, post_params = saved
    S = u.shape[0]
    tm = _pick(S, (256, 128))
    (dy, du1), (dd, dwglu) = _tile_bwd(_f_s5_post, [y, u], post_params, [dout], [True, True], [True, True], tm,
                                       f's5_post_bwd_{l}')
    ccat = jnp.concatenate([c_re, c_im], axis=0)
    dh = _mm(dy, ccat, 'nt', f's5_out_dx_{l}')
    dccat = jnp.concatenate([_mm(hr, dy, 'tn', f's5_out_re_dw_{l}'), _mm(hi, dy, 'tn', f's5_out_im_dw_{l}')], axis=0)
    lr_, li_, dar, dai = _s5_scan(dh, a_rows, f's5_scan_bwd_{l}', reverse=True, h=(hr, hi))
    du2 = _mm(li_, bb[:, S5_LANES:], 'nt', f's5_in_im_dx_{l}', add=_mm(lr_, bb[:, :S5_LANES], 'nt', f's5_in_re_dx_{l}'))
    dbb = jnp.concatenate([_mm(u, lr_, 'tn', f's5_in_re_dw_{l}'), _mm(u, li_, 'tn', f's5_in_im_dw_{l}')], axis=1)
    da_rows = jnp.pad(jnp.concatenate([dar, dai], axis=1), ((0, GROUP_W - SCAN_SEGMENTS), (0, 0)))
    (dbre, dbim), (dlr, dli, dlogdt) = _tile_bwd(_f_s5_prep, [bre, bim], prep_params, [dbb, da_rows], [True, True],
                                                 [True, True, True, False], GROUP_W, f's5_prep_bwd_{l}')
    grads = {
        's5_lambda_re': dlr.reshape(S5_G, S5_P), 's5_lambda_im': dli.reshape(S5_G, S5_P), 's5_log_dt': dlogdt[:, 0],
        's5_b_re': _diag_blocks(dbre, S5_CG, S5_P).transpose(0, 2, 1),
        's5_b_im': _diag_blocks(dbim, S5_CG, S5_P).transpose(0, 2, 1),
        's5_c_re': _diag_blocks(dccat[:S5_LANES], S5_P, S5_CG).transpose(0, 2, 1),
        's5_c_im': -_diag_blocks(dccat[S5_LANES:], S5_P, S5_CG).transpose(0, 2, 1),
        's5_d': dd[0], 's5_w_glu': dwglu}
    return du1 + du2, grads


DN_CONV = 4


def _head_sum_matrix():
    h = np.arange(GROUP_W) // HEAD_DIM
    return jnp.asarray((h[:, None] == h[None, :]).astype(np.float32))


def _f_dn_pre(x0, x1, x2, x3, ab, w0, w1, w2, w3, alog, dtb, ea, eb, hs):
    c = w0 * x0 + w1 * x1 + w2 * x2 + w3 * x3
    s = c * jax.nn.sigmoid(c)
    q, k, v = s[:, :GROUP_W], s[:, GROUP_W:2 * GROUP_W], s[:, 2 * GROUP_W:]
    q = q * lax.rsqrt(_hdot(q * q, hs) + EPS) * (HEAD_DIM ** -0.5)
    k = k * lax.rsqrt(_hdot(k * k, hs) + EPS)
    beta = jax.nn.sigmoid(_hdot(ab, eb))
    g = -jnp.exp(alog) * jax.nn.softplus(_hdot(ab, ea) + dtb)
    return q, k, v, g, beta


DN_CHUNKS_PER_STEP = 4


def _f_dn_chunks(q, k, v, g, beta):
    C = DN_CHUNK
    n_chunks = q.shape[0] // C
    r = lax.broadcasted_iota(jnp.int32, (C, C), 0)
    c = lax.broadcasted_iota(jnp.int32, (C, C), 1)
    causal, strict = r >= c, r > c
    eye = (r == c).astype(f32)
    tril = causal.astype(f32)
    ones = jnp.ones((C, GROUP_W), f32)
    masks = [_head_mask(h) for h in range(N_HEADS)]
    rows = [tuple(t[i * C:(i + 1) * C] for t in (q, k, v, g, beta)) for i in range(n_chunks)]
    gcs = [_hdot(tril, gi) for (_, _, _, gi, _) in rows]
    items = [(i, h) for i in range(n_chunks) for h in range(N_HEADS)]
    grows = [_hdot_nt(ones * (masks[h] * (1.0 / HEAD_DIM)), gcs[i]) for i, h in items]
    decs = []
    for (i, h), grow in zip(items, grows):
        gcol = jnp.sum(gcs[i] * masks[h], axis=1, keepdims=True) * (1.0 / HEAD_DIM)
        decs.append(jnp.exp(jnp.where(causal, gcol - grow, NEG_INF)))
    kbs = [ki * bi for (_, ki, _, _, bi) in rows]
    kks = [_bmm_nt(kbs[i] * masks[h], rows[i][1]) for i, h in items]
    qks = [_bmm_nt(rows[i][0] * masks[h], rows[i][1]) for i, h in items]
    lmats = [jnp.where(strict, kk * dec, 0.0) for kk, dec in zip(kks, decs)]
    a_qk = [jnp.where(causal, qk * dec, 0.0) for qk, dec in zip(qks, decs)]
    ts = [eye - lm for lm in lmats]
    ps = lmats
    for _ in range(5):
        ps = [_bmm(p, p) for p in ps]
        ts = [t + _bmm(t, p) for t, p in zip(ts, ps)]
    egs = [jnp.exp(gc) for gc in gcs]
    tw = [_bmm(t, kbs[i] * egs[i]) for (i, h), t in zip(items, ts)]
    tu = [_bmm(t, rows[i][2] * rows[i][4]) for (i, h), t in zip(items, ts)]
    outs = []
    for i in range(n_chunks):
        qi, ki, _, gi, _ = rows[i]
        glast = jnp.sum(gi, axis=0, keepdims=True)
        w = sum(tw[i * N_HEADS + h] * masks[h] for h in range(N_HEADS))
        u = sum(tu[i * N_HEADS + h] * masks[h] for h in range(N_HEADS))
        outs.append((w, u, qi * egs[i], ki * jnp.exp(glast - gcs[i]), *a_qk[i * N_HEADS:(i + 1) * N_HEADS],
                     jnp.broadcast_to(jnp.exp(glast), (C, GROUP_W))))
    return tuple(jnp.concatenate(parts, axis=0) for parts in zip(*outs))


def _f_dn_step(w, u, qd, kdec, a0, a1, a2, a3, dfull, state, bd):
    row0 = (lax.broadcasted_iota(jnp.int32, dfull.shape, 0) == 0).astype(f32)
    dvec = jnp.sum(dfull * row0, axis=0, keepdims=True)
    ws, qs = _bmm(w, state), _bmm(qd, state)
    vnew = u - ws
    avs = [_bmm(a, vnew) for a in (a0, a1, a2, a3)]
    kv = _bmm_tn(kdec, vnew)
    o = qs + sum(av * _head_mask(h) for h, av in enumerate(avs))
    return o, state * dvec + bd * kv


def _dn_scan_fwd(ins, name):
    S = ins[0].shape[0]
    N = S // DN_CHUNK
    bd = _head_sum_matrix()

    def body(*refs):
        o_ref, s_ref, state = refs[10], refs[11], refs[12]

        @pl.when(pl.program_id(0) == 0)
        def _():
            state[...] = jnp.zeros_like(state)

        s_in = state[...]
        s_ref[0] = s_in
        o, s_out = _f_dn_step(*[r[...] for r in refs[:9]], s_in, refs[9][...])
        o_ref[...] = o
        state[...] = s_out

    return pl.pallas_call(
        body, name=name, grid=(N,),
        in_specs=[pl.BlockSpec((DN_CHUNK, t.shape[1]), lambda n: (n, 0)) for t in ins] + [_full_spec(bd)],
        out_specs=[pl.BlockSpec((DN_CHUNK, GROUP_W), lambda n: (n, 0)), pl.BlockSpec((1, GROUP_W, GROUP_W), lambda n: (n, 0, 0))],
        out_shape=[jax.ShapeDtypeStruct((S, GROUP_W), f32), jax.ShapeDtypeStruct((N, GROUP_W, GROUP_W), f32)],
        scratch_shapes=[pltpu.VMEM((GROUP_W, GROUP_W), f32)],
        compiler_params=_cparams(('arbitrary',)),
    )(*ins, bd)


def _dn_scan_bwd(ins, states, do, name):
    S = ins[0].shape[0]
    N = S // DN_CHUNK
    bd = _head_sum_matrix()

    def body(*refs):
        s_ref, do_ref = refs[9], refs[10]
        bd_ref = refs[11]
        outs = refs[12:21]
        dstate = refs[21]

        @pl.when(pl.program_id(0) == 0)
        def _():
            dstate[...] = jnp.zeros_like(dstate)

        bd_val = bd_ref[...]
        _, vjp = jax.vjp(lambda *a: _f_dn_step(*a, bd_val), *[r[...] for r in refs[:9]], s_ref[0])
        grads = vjp((do_ref[...], dstate[...]))
        for o, g in zip(outs, grads[:9]):
            o[...] = g
        dstate[...] = grads[9]

    def rev(n):
        return (N - 1 - n, 0)

    res = pl.pallas_call(
        body, name=name, grid=(N,),
        in_specs=[pl.BlockSpec((DN_CHUNK, t.shape[1]), rev) for t in ins] +
                 [pl.BlockSpec((1, GROUP_W, GROUP_W), lambda n: (N - 1 - n, 0, 0)), pl.BlockSpec((DN_CHUNK, GROUP_W), rev),
                  _full_spec(bd)],
        out_specs=[pl.BlockSpec((DN_CHUNK, t.shape[1]), rev) for t in ins],
        out_shape=[jax.ShapeDtypeStruct(t.shape, f32) for t in ins],
        scratch_shapes=[pltpu.VMEM((GROUP_W, GROUP_W), f32)],
        compiler_params=_cparams(('arbitrary',)),
    )(*ins, states, do, bd)
    return list(res)


def _f_dn_post(o, gate, gain, hmean):
    return (o * lax.rsqrt(_hdot(o * o, hmean) + EPS) * gain * (gate * jax.nn.sigmoid(gate)),)


def _delay(t, j):
    return t if j == 0 else jnp.pad(t[:-j], ((j, 0), (0, 0)))


def _advance(t, j):
    return t if j == 0 else jnp.pad(t[j:], ((0, j), (0, 0)))


def _dn_fwd(qkv, a, b, gate, mp, l):
    S = qkv.shape[0]
    tm = _pick(S, (256, 128))
    xs = [_delay(qkv, DN_CONV - 1 - j) for j in range(DN_CONV)]
    ab = jnp.pad(jnp.concatenate([a, b], axis=1), ((0, 0), (0, LANES - 2 * N_HEADS)))
    sel = np.zeros((2, LANES, GROUP_W), np.float32)
    for h in range(N_HEADS):
        sel[0, h, h * HEAD_DIM:(h + 1) * HEAD_DIM] = 1.0
        sel[1, N_HEADS + h, h * HEAD_DIM:(h + 1) * HEAD_DIM] = 1.0
    pre_params = [*[mp['dn_conv'][j][None] for j in range(DN_CONV)], jnp.repeat(mp['dn_a_log'], HEAD_DIM)[None],
                  jnp.repeat(mp['dn_dt_bias'], HEAD_DIM)[None], jnp.asarray(sel[0]), jnp.asarray(sel[1]), _head_sum_matrix()]
    pre = _tile_fwd(_f_dn_pre, [*xs, ab], pre_params, [(GROUP_W, f32)] * 5, tm, f'dn_pre_fwd_{l}')
    chunk_outs = [(GROUP_W, f32)] * 4 + [(HEAD_DIM, f32)] * 4 + [(GROUP_W, f32)]
    parts = _tile_fwd(_f_dn_chunks, pre, [], chunk_outs, DN_CHUNK * DN_CHUNKS_PER_STEP, f'dn_chunk_fwd_{l}')
    o, states = _dn_scan_fwd(parts, f'dn_scan_fwd_{l}')
    post_params = [jnp.tile(mp['dn_o_norm'], N_HEADS)[None], _head_mean_matrix()]
    (y,) = _tile_fwd(_f_dn_post, [o, gate], post_params, [(GROUP_W, f32)], tm, f'dn_post_fwd_{l}')
    return y, (xs, ab, pre_params, pre, parts, states, o, gate, post_params)


def _dn_bwd(dy, saved, l):
    xs, ab, pre_params, pre, parts, states, o, gate, post_params = saved
    S = dy.shape[0]
    tm = _pick(S, (256, 128))
    (do, dgate), (dgain,) = _tile_bwd(_f_dn_post, [o, gate], post_params, [dy], [True, True], [True, False], tm,
                                      f'dn_post_bwd_{l}')
    dparts = _dn_scan_bwd(parts, states, do, f'dn_scan_bwd_{l}')
    dpre, _ = _tile_bwd(_f_dn_chunks, pre, [], dparts, [True] * 5, [], DN_CHUNK * DN_CHUNKS_PER_STEP, f'dn_chunk_bwd_{l}')
    dins, dpar = _tile_bwd(_f_dn_pre, [*xs, ab], pre_params, dpre, [True] * 5, [True] * 6 + [False] * 3, tm,
                           f'dn_pre_bwd_{l}')
    dqkv = dins[DN_CONV - 1]
    for j in range(DN_CONV - 1):
        dqkv = dqkv + _advance(dins[j], DN_CONV - 1 - j)
    dab = dins[DN_CONV]
    grads = {'dn_conv': jnp.concatenate(dpar[:DN_CONV], axis=0),
             'dn_a_log': dpar[4].reshape(N_HEADS, HEAD_DIM).sum(1), 'dn_dt_bias': dpar[5].reshape(N_HEADS, HEAD_DIM).sum(1),
             'dn_o_norm': dgain.reshape(N_HEADS, HEAD_DIM).sum(0)}
    return dqkv, dab[:, :N_HEADS], dab[:, N_HEADS:2 * N_HEADS], dgate, grads


def _t5_bucket(dist):
    exact = T5_BUCKETS // 2
    df = jnp.maximum(dist, 1).astype(f32)
    large = exact + (jnp.log(df / exact) / math.log(T5_MAX_DIST / exact) * (T5_BUCKETS - exact)).astype(jnp.int32)
    large = jnp.minimum(large, T5_BUCKETS - 1)
    return jnp.where(dist < exact, dist, large)


def _split_cols(t, sizes):
    out, start = [], 0
    for s in sizes:
        out.append(t[..., start:start + s])
        start += s
    return out


def _mixers_fwd(proj, mp, l):
    c_q, c_kv, k_rope, u_s5, qkv_dil, qkv_dn, a_dn, b_dn, gate_dn = _split_cols(proj, IN_SPLITS)
    y_mla, s_mla = _mla_fwd(c_q, c_kv, k_rope, mp, l)
    y_s5, s_s5 = _s5_fwd(u_s5, mp, l)
    y_dil, s_dil = _dil_fwd(qkv_dil, mp, l)
    y_dn, s_dn = _dn_fwd(qkv_dn, a_dn, b_dn, gate_dn, mp, l)
    return jnp.concatenate([y_mla, y_s5, y_dil, y_dn], axis=-1), (s_mla, s_s5, s_dil, s_dn)


def _mixers_bwd(dmixed, saved, l):
    s_mla, s_s5, s_dil, s_dn = saved
    d_mla, d_s5, d_dil, d_dn = _split_cols(dmixed, (GROUP_W,) * 4)
    dc_q, dc_kv, dk_rope, g_mla = _mla_bwd(d_mla, s_mla, l)
    du, g_s5 = _s5_bwd(d_s5, s_s5, l)
    dqkv_dil, g_dil = _dil_bwd(d_dil, s_dil, l)
    dqkv_dn, da, db, dgate, g_dn = _dn_bwd(d_dn, s_dn, l)
    parts = [dc_q, dc_kv, dk_rope, du, dqkv_dil, dqkv_dn, da, db, dgate]
    dproj = jnp.concatenate([p.astype(bf16) for p in parts], axis=-1)
    return dproj, {**g_mla, **g_s5, **g_dil, **g_dn}


MIXER_PARAMS = ['mla_q_norm', 'mla_kv_norm', 'mla_w_uq', 'mla_w_ukv', 'mla_qk_q', 'mla_qk_k', 's5_lambda_re',
                's5_lambda_im', 's5_log_dt', 's5_b_re', 's5_b_im', 's5_c_re', 's5_c_im', 's5_d', 's5_w_glu',
                'dil_q_norm', 'dil_k_norm', 't5_bias', 'dn_conv', 'dn_a_log', 'dn_dt_bias', 'dn_o_norm']


def _layer_fwd_mix(h, W, l):
    S = h.shape[0]
    tm = _pick(S, (256, 128))
    g1 = W['attn_norm'][l][None]
    (n1,) = _tile_fwd(_f_rms, [h], [g1], [(D_MODEL, bf16)], tm, f'rms1_fwd_{l}')
    proj = _mm(n1, W['w_in'][l], 'nn', f'proj_fwd_{l}')
    mp = {k: (W[k] if k == 't5_bias' else W[k][l]).astype(f32) for k in MIXER_PARAMS}
    mixed, mix_saved = _mixers_fwd(proj, mp, l)
    mixed_b = mixed.astype(bf16)
    h2 = _mm(mixed_b, W['w_out'][l], 'nn', f'out_fwd_{l}', add=h)
    return h2, dict(h=h, n1=n1, mix=mix_saved, mixed=mixed_b, h2=h2)


def _layer_fwd_ffn(h2, W, l, saved):
    S = h2.shape[0]
    tm = _pick(S, (256, 128))
    g2 = W['ffn_norm'][l][None]
    (n2,) = _tile_fwd(_f_rms, [h2], [g2], [(D_MODEL, bf16)], tm, f'rms2_fwd_{l}')
    w13 = jnp.concatenate([W['ffn_w1'][l], W['ffn_w3'][l]], axis=1)
    uv = _mm(n2, w13, 'nn', f'ffn13_fwd_{l}')
    (act,) = _tile_fwd(_f_swiglu, [uv], [], [(FFN_HIDDEN, bf16)], tm, f'swiglu_fwd_{l}')
    h3 = _mm(act, W['ffn_w2'][l], 'nn', f'ffn2_fwd_{l}', add=h2)
    saved.update(n2=n2, uv=uv, act=act, w13=w13)
    return h3


def _layer_bwd_ffn(dh3, saved, W, l):
    S = dh3.shape[0]
    tm = _pick(S, (256, 128))
    g2 = W['ffn_norm'][l][None]
    grads = {}
    dact = _mm(dh3, W['ffn_w2'][l], 'nt', f'ffn2_dx_{l}')
    grads['ffn_w2'] = _mm(saved['act'], dh3, 'tn', f'ffn2_dw_{l}', out_dtype=bf16)
    (duv,), _ = _tile_bwd(_f_swiglu, [saved['uv']], [], [dact], [True], [], tm, f'swiglu_bwd_{l}', dt_dtypes=[bf16])
    dn2 = _mm(duv, saved['w13'], 'nt', f'ffn13_dx_{l}')
    dw13 = _mm(saved['n2'], duv, 'tn', f'ffn13_dw_{l}', out_dtype=bf16)
    grads['ffn_w1'], grads['ffn_w3'] = dw13[:, :FFN_HIDDEN], dw13[:, FFN_HIDDEN:]
    (dh2n,), (dg2,) = _tile_bwd(_f_rms, [saved['h2']], [g2], [dn2], [True], [True], tm, f'rms2_bwd_{l}')
    grads['ffn_norm'] = dg2[0]
    return (dh3, dh2n), grads


def _layer_bwd_mix(dh2, saved, W, l):
    S = dh2.shape[0]
    tm = _pick(S, (256, 128))
    g1 = W['attn_norm'][l][None]
    grads = {}
    dmixed = _mm(dh2, W['w_out'][l], 'nt', f'out_dx_{l}')
    grads['w_out'] = _mm(saved['mixed'], dh2, 'tn', f'out_dw_{l}', out_dtype=bf16)
    dproj, dmp = _mixers_bwd(dmixed, saved['mix'], l)
    for k in MIXER_PARAMS:
        grads[k] = dmp[k]
    dn1 = _mm(dproj, W['w_in'][l], 'nt', f'proj_dx_{l}')
    grads['w_in'] = _mm(saved['n1'], dproj, 'tn', f'proj_dw_{l}', out_dtype=bf16)
    (dh1n,), (dg1,) = _tile_bwd(_f_rms, [saved['h']], [g1], [dn1], [True], [True], tm, f'rms1_bwd_{l}')
    grads['attn_norm'] = dg1[0]
    return (dh2, dh1n), grads


def kernel(x, attn_norm, w_in, w_out, mla_q_norm, mla_kv_norm, mla_w_uq, mla_w_ukv, mla_qk_q, mla_qk_k, s5_lambda_re, s5_lambda_im, s5_log_dt, s5_b_re, s5_b_im, s5_c_re, s5_c_im, s5_d, s5_w_glu, dil_q_norm, dil_k_norm, t5_bias, dn_conv, dn_a_log, dn_dt_bias, dn_o_norm, ffn_norm, ffn_w1, ffn_w3, ffn_w2, loss_target, m_attn_norm, m_w_in, m_w_out, m_mla_q_norm, m_mla_kv_norm, m_mla_w_uq, m_mla_w_ukv, m_mla_qk_q, m_mla_qk_k, m_s5_lambda_re, m_s5_lambda_im, m_s5_log_dt, m_s5_b_re, m_s5_b_im, m_s5_c_re, m_s5_c_im, m_s5_d, m_s5_w_glu, m_dil_q_norm, m_dil_k_norm, m_t5_bias, m_dn_conv, m_dn_a_log, m_dn_dt_bias, m_dn_o_norm, m_ffn_norm, m_ffn_w1, m_ffn_w3, m_ffn_w2, v_attn_norm, v_w_in, v_w_out, v_mla_q_norm, v_mla_kv_norm, v_mla_w_uq, v_mla_w_ukv, v_mla_qk_q, v_mla_qk_k, v_s5_lambda_re, v_s5_lambda_im, v_s5_log_dt, v_s5_b_re, v_s5_b_im, v_s5_c_re, v_s5_c_im, v_s5_d, v_s5_w_glu, v_dil_q_norm, v_dil_k_norm, v_t5_bias, v_dn_conv, v_dn_a_log, v_dn_dt_bias, v_dn_o_norm, v_ffn_norm, v_ffn_w1, v_ffn_w3, v_ffn_w2):
    given = dict(locals())
    w_loc = {n: given[n] for n in WEIGHTS}
    m_loc = {n: given['m_' + n] for n in WEIGHTS}
    v_loc = {n: given['v_' + n] for n in WEIGHTS}
    big_names = list(BIG)

    own = 2 * lax.axis_index('x') + lax.axis_index('y')
    groups = [[(n, 0) for n in GATHER_FIRST], [(n, 0) for n in GATHER_FFN], [(n, 1) for n in big_names]]
    started, order = [], jnp.zeros((8, LANES), f32)
    for gi, group in enumerate(groups):
        blocks = [w_loc[n][l].astype(bf16) for n, l in group]
        lands = [lax.empty((N_SHARDS,) + b.shape, bf16) for b in blocks]
        send_sems, recv_sems, blocks, lands, order = _to_chips_start(blocks, lands, False, order, f'gather_start_{gi}')
        started.append((send_sems, recv_sems, blocks, lands))
    W = {n: [None] * DEPTH for n in big_names}
    for n in SMALL:
        W[n] = w_loc[n]

    def arrive(gi, after):
        send_sems, recv_sems, blocks, lands = started[gi]
        lands = _to_chips_wait(send_sems, recv_sems, blocks, lands, False, after, f'gather_wait_{gi}')
        for (n, l), block, land in zip(groups[gi], blocks, lands):
            W[n][l] = _from_shards(n, lax.dynamic_update_slice(land, block[None], (own, 0, 0)))

    arrive(0, order)
    h = x[0]
    saved = []
    for l in range(DEPTH):
        h2, sv = _layer_fwd_mix(h, W, l)
        if l == 0:
            arrive(1, h2)
        h = _layer_fwd_ffn(h2, W, l, sv)
        if l == 0:
            arrive(2, h)
        saved.append(sv)
    parts_loss, dh = _loss_head(h, loss_target[0])
    loss = lax.psum(jnp.sum(parts_loss), ('x', 'y', 'c'))

    layer_grads = [dict() for _ in range(DEPTH)]
    sent = []

    def send(group, tag):
        srcs = [_by_shard(n, layer_grads[l][n]).astype(bf16) for n, l in group]
        lands = [lax.empty((3,) + s.shape[1:], bf16) for s in srcs]
        send_sems, recv_sems, srcs, lands, token = _to_chips_start(srcs, lands, True, jnp.zeros((8, LANES), f32),
                                                                   f'reduce_start_{tag}')
        sent.append((group, tag, send_sems, recv_sems, srcs, lands))
        return token[0, 0]

    for l in reversed(range(DEPTH)):
        (dh3, dh2n), g_ffn = _layer_bwd_ffn(dh, saved[l], W, l)
        layer_grads[l].update(g_ffn)
        dh2 = dh3 + dh2n
        if l == 0:
            dh2 = dh2 + send([(n, 0) for n in GATHER_FFN], 'ffn0')
        (dh2, dh1n), g_mix = _layer_bwd_mix(dh2, saved[l], W, l)
        layer_grads[l].update(g_mix)
        dh = dh2 + dh1n
        if l == 1:
            dh = dh + send([(n, 1) for n in big_names], 'layer1')
    last = send([(n, 0) for n in GATHER_FIRST], 'first0')
    grad_x = dh[None]
    small_full = []
    for n in SMALL:
        if n == 't5_bias':
            small_full.append(layer_grads[0][n] + layer_grads[1][n])
        else:
            small_full.append(jnp.stack([layer_grads[l][n] for l in range(DEPTH)]))

    small_shapes = [w_loc[n].shape for n in SMALL]
    small_pack = _pack(small_full) + last
    _, recv_small = _swap_with_sibling([], small_pack)
    chip_small = _small_chip_sum(small_pack, recv_small)
    _, from_chips_small = _exchange_between_chips([], chip_small)

    mine = {}
    for group, tag, send_sems, recv_sems, srcs, lands in sent:
        lands = _to_chips_wait(send_sems, recv_sems, srcs, lands, True, from_chips_small, f'reduce_wait_{tag}')
        for (n, l), src, land in zip(group, srcs, lands):
            mine[(n, l)] = _partial_sum(src, land, f'partial_{n}_{l}')
    keys = [(n, l) for n in big_names for l in range(DEPTH)]
    theirs = dict(zip(keys, _swap_partials([mine[k] for k in keys])))

    g_small_p, d_small_p, m_small_p, v_small_p = _small_update(
        small_pack, recv_small, from_chips_small, _pack([w_loc[n] for n in SMALL]),
        _pack([m_loc[n] for n in SMALL]), _pack([v_loc[n] for n in SMALL]))
    grad, delta, new_m, new_v = {}, {}, {}, {}
    for n, g_, d_, m_, v_ in zip(SMALL, _unpack(g_small_p, small_shapes), _unpack(d_small_p, small_shapes),
                                 _unpack(m_small_p, small_shapes), _unpack(v_small_p, small_shapes)):
        grad[n], delta[n], new_m[n], new_v[n] = g_, d_, m_, v_
    for n in big_names:
        grad[n], delta[n], new_m[n], new_v[n] = _adamw(
            w_loc[n], m_loc[n], v_loc[n], [mine[(n, l)] for l in range(DEPTH)], [theirs[(n, l)] for l in range(DEPTH)],
            'adamw_' + n)
    return (loss, grad_x, *[grad[n] for n in WEIGHTS], *[delta[n] for n in WEIGHTS],
            *[new_m[n] for n in WEIGHTS], *[new_v[n] for n in WEIGHTS])
```

```python
import functools
import math

import numpy as np
import jax
import jax.numpy as jnp
from jax import lax
from jax.experimental import pallas as pl
from jax.experimental.pallas import tpu as pltpu

f32 = jnp.float32
bf16 = jnp.bfloat16
HI = lax.Precision.HIGHEST
MESH = pl.DeviceIdType.MESH

VMEM_LIMIT_BYTES = 48 * 1024 * 1024
MM_VMEM_BUDGET_BYTES = 32 * 1024 * 1024
LANES = 128

D_MODEL = 1024
DEPTH = 2
GROUP_W = 256
HEAD_DIM = 64
EPS = 1e-6
NEG_INF = -1e30
N_HEADS = 4
MLA_NOPE, MLA_ROPE = 64, 32
MLA_DQK = MLA_NOPE + MLA_ROPE
ROPE_THETA = 10000.0
Q_BLOCK = 128
S5_G, S5_CG, S5_P = 16, 16, 64
DIL_PAIRS = ((128, 1), (512, 4), (2048, 16))
T5_BUCKETS, T5_MAX_DIST = 32, 2048
DN_CHUNK = 64
FFN_HIDDEN = 2816
IN_SPLITS = (256, 128, 32, 256, 768, 768, 4, 4, 256)
IN_COLS = sum(IN_SPLITS)

ADAM_LR, ADAM_B1, ADAM_B2, ADAM_EPS, ADAM_WD, ADAM_STEP = 0.001, 0.9, 0.999, 1e-08, 0.01, 10

WEIGHTS = ['attn_norm', 'w_in', 'w_out', 'mla_q_norm', 'mla_kv_norm', 'mla_w_uq', 'mla_w_ukv', 'mla_qk_q', 'mla_qk_k',
           's5_lambda_re', 's5_lambda_im', 's5_log_dt', 's5_b_re', 's5_b_im', 's5_c_re', 's5_c_im', 's5_d', 's5_w_glu',
           'dil_q_norm', 'dil_k_norm', 't5_bias', 'dn_conv', 'dn_a_log', 'dn_dt_bias', 'dn_o_norm', 'ffn_norm',
           'ffn_w1', 'ffn_w3', 'ffn_w2']
BIG = {'w_in': 2, 'w_out': 1, 'mla_w_uq': 2, 'mla_w_ukv': 2, 's5_w_glu': 2, 'dn_conv': 2, 'ffn_w1': 2, 'ffn_w3': 2,
       'ffn_w2': 1}
SMALL = [n for n in WEIGHTS if n not in BIG]
GATHER_FIRST = ['w_in', 'mla_w_uq', 'mla_w_ukv', 's5_w_glu', 'dn_conv', 'w_out']
GATHER_FFN = ['ffn_w1', 'ffn_w3', 'ffn_w2']
N_SHARDS = 4
PACK_COLS = 1024


def _cparams(sem=None, big=False):
    kw = {}
    if sem is not None:
        kw['dimension_semantics'] = sem
    if big:
        kw['vmem_limit_bytes'] = VMEM_LIMIT_BYTES
    return pltpu.CompilerParams(**kw)


def _pick(n, prefs):
    for p in prefs:
        if p <= n and n % p == 0:
            return p
    return n


def _lane_tile(n, cap):
    for t in range(cap - cap % LANES, 0, -LANES):
        if n % t == 0:
            return t
    return n


def _mm(a, b, mode, name, add=None, out_dtype=f32):
    if mode == 'nn':
        (M, K), (K2, N) = a.shape, b.shape
    elif mode == 'nt':
        (M, K), (N, K2) = a.shape, b.shape
    else:
        (K, M), (K2, N) = a.shape, b.shape
    assert K == K2, (name, a.shape, b.shape)
    tk = K if K <= 2816 else _pick(K, (2816, 2048, 1408, 1024, 512))
    cap_m, cap_n = (1408 if mode == 'tn' else 512), 1408

    def need(tm_, tn_):
        per_step = tm_ * tk * a.dtype.itemsize + tk * tn_ * b.dtype.itemsize + tm_ * tn_ * jnp.dtype(out_dtype).itemsize
        if add is not None:
            per_step += tm_ * tn_ * add.dtype.itemsize
        return 2 * per_step + tm_ * tn_ * 4

    tm, tn = _lane_tile(M, cap_m), _lane_tile(N, cap_n)
    while need(tm, tn) > MM_VMEM_BUDGET_BYTES and cap_m > LANES:
        cap_m //= 2
        tm = _lane_tile(M, cap_m)
    nk = K // tk
    dims = {'nn': (((1,), (0,)), ((), ())), 'nt': (((1,), (1,)), ((), ())), 'tn': (((0,), (0,)), ((), ()))}[mode]
    has_add = add is not None

    def body(*refs):
        a_ref, b_ref = refs[0], refs[1]
        add_ref = refs[2] if has_add else None
        o_ref = refs[3] if has_add else refs[2]
        part = lax.dot_general(a_ref[...].astype(bf16), b_ref[...].astype(bf16), dims, preferred_element_type=f32)
        if nk == 1:
            if has_add:
                part = part + add_ref[...].astype(f32)
            o_ref[...] = part.astype(out_dtype)
        else:
            acc_ref = refs[-1]
            k = pl.program_id(2)

            @pl.when(k == 0)
            def _():
                acc_ref[...] = part

            @pl.when(k > 0)
            def _():
                acc_ref[...] += part

            @pl.when(k == nk - 1)
            def _():
                r = acc_ref[...]
                if has_add:
                    r = r + add_ref[...].astype(f32)
                o_ref[...] = r.astype(out_dtype)

    if mode == 'nn':
        a_spec = pl.BlockSpec((tm, tk), lambda i, j, k: (i, k))
        b_spec = pl.BlockSpec((tk, tn), lambda i, j, k: (k, j))
    elif mode == 'nt':
        a_spec = pl.BlockSpec((tm, tk), lambda i, j, k: (i, k))
        b_spec = pl.BlockSpec((tn, tk), lambda i, j, k: (j, k))
    else:
        a_spec = pl.BlockSpec((tk, tm), lambda i, j, k: (k, i))
        b_spec = pl.BlockSpec((tk, tn), lambda i, j, k: (k, j))
    in_specs = [a_spec, b_spec]
    args = [a, b]
    if has_add:
        in_specs.append(pl.BlockSpec((tm, tn), lambda i, j, k: (i, j)))
        args.append(add)
    return pl.pallas_call(
        body, name=name, grid=(M // tm, N // tn, nk), in_specs=in_specs,
        out_specs=pl.BlockSpec((tm, tn), lambda i, j, k: (i, j)),
        out_shape=jax.ShapeDtypeStruct((M, N), out_dtype),
        scratch_shapes=[pltpu.VMEM((tm, tn), f32)] if nk > 1 else [],
        compiler_params=_cparams(('parallel', 'parallel', 'arbitrary'), big=True),
    )(*args)


def _full_spec(p):
    nd = p.ndim
    return pl.BlockSpec(p.shape, lambda i, _nd=nd: (0,) * _nd)


def _tile_fwd(f, tiled, params, outs, tm, name):
    S = tiled[0].shape[0]
    nt, npar = len(tiled), len(params)

    def body(*refs):
        vals = [r[...].astype(f32) for r in refs[:nt + npar]]
        res = f(*vals)
        for r, o in zip(res, refs[nt + npar:]):
            o[...] = r.astype(o.dtype)

    return pl.pallas_call(
        body, name=name, grid=(S // tm,),
        in_specs=[pl.BlockSpec((tm, t.shape[1]), lambda i: (i, 0)) for t in tiled] + [_full_spec(p) for p in params],
        out_specs=[pl.BlockSpec((tm, c), lambda i: (i, 0)) for c, _ in outs],
        out_shape=[jax.ShapeDtypeStruct((S, c), dt) for c, dt in outs],
        compiler_params=_cparams(('parallel',), big=True),
    )(*tiled, *params)


def _tile_bwd(f, tiled, params, cts, diff_t, diff_p, tm, name, dt_dtypes=None):
    S = tiled[0].shape[0]
    nt, npar, nc = len(tiled), len(params), len(cts)
    it = [i for i in range(nt) if diff_t[i]]
    ip = [i for i in range(npar) if diff_p[i]]
    if dt_dtypes is None:
        dt_dtypes = [f32] * len(it)

    def body(*refs):
        vals = [r[...].astype(f32) for r in refs[:nt + npar]]
        ct_vals = tuple(r[...].astype(f32) for r in refs[nt + npar:nt + npar + nc])
        out_refs = refs[nt + npar + nc:]

        def g(*dv):
            full = list(vals)
            for k, i in enumerate(it):
                full[i] = dv[k]
            for k, i in enumerate(ip):
                full[nt + i] = dv[len(it) + k]
            return tuple(f(*full))

        _, vjp = jax.vjp(g, *[vals[i] for i in it], *[vals[nt + i] for i in ip])
        grads = vjp(ct_vals)
        for k in range(len(it)):
            out_refs[k][...] = grads[k].astype(out_refs[k].dtype)
        step = pl.program_id(0)
        for k in range(len(ip)):
            o = out_refs[len(it) + k]
            gk = grads[len(it) + k]

            @pl.when(step == 0)
            def _(o=o, gk=gk):
                o[...] = gk

            @pl.when(step > 0)
            def _(o=o, gk=gk):
                o[...] += gk

    out_specs = [pl.BlockSpec((tm, tiled[i].shape[1]), lambda i_: (i_, 0)) for i in it] + [_full_spec(params[i]) for i in ip]
    out_shape = [jax.ShapeDtypeStruct(tiled[i].shape, dt_dtypes[k]) for k, i in enumerate(it)] + \
                [jax.ShapeDtypeStruct(params[i].shape, f32) for i in ip]
    res = pl.pallas_call(
        body, name=name, grid=(S // tm,),
        in_specs=[pl.BlockSpec((tm, t.shape[1]), lambda i: (i, 0)) for t in tiled] + [_full_spec(p) for p in params] +
                 [pl.BlockSpec((tm, c.shape[1]), lambda i: (i, 0)) for c in cts],
        out_specs=out_specs, out_shape=out_shape,
        compiler_params=_cparams(('arbitrary',), big=True),
    )(*tiled, *params, *cts)
    return list(res[:len(it)]), list(res[len(it):])


def _rms(x, g):
    return x * lax.rsqrt(jnp.mean(x * x, axis=-1, keepdims=True) + EPS) * g


def _f_rms(x, g):
    return (_rms(x, g),)


def _f_swiglu(u, v):
    return (u * jax.nn.sigmoid(u) * v,)


def _loss_head(y, target):
    S, D = y.shape
    tm = _pick(S, (256, 128))

    def body(y_ref, t_ref, part_ref, dy_ref):
        e = y_ref[...] - t_ref[...]
        dy_ref[...] = e * (1.0 / D)
        s = 0.5 * jnp.sum(jnp.sum(e * e, axis=1, keepdims=True), axis=0, keepdims=True) * (1.0 / D)
        r = lax.broadcasted_iota(jnp.int32, (8, LANES), 0)
        c = lax.broadcasted_iota(jnp.int32, (8, LANES), 1)
        part_ref[0] = jnp.where((r == 0) & (c == 0), s, 0.0)

    return pl.pallas_call(
        body, name='loss_head', grid=(S // tm,),
        in_specs=[pl.BlockSpec((tm, D), lambda i: (i, 0))] * 2,
        out_specs=[pl.BlockSpec((1, 8, LANES), lambda i: (i, 0, 0)), pl.BlockSpec((tm, D), lambda i: (i, 0))],
        out_shape=[jax.ShapeDtypeStruct((S // tm, 8, LANES), f32), jax.ShapeDtypeStruct((S, D), f32)],
        compiler_params=_cparams(('parallel',)),
    )(y, target)


def _pack_rows_of(shape):
    rows = -(-math.prod(shape) // PACK_COLS)
    return -(-rows // 8) * 8


def _pack(arrs):
    parts = []
    for a in arrs:
        rows = _pack_rows_of(a.shape)
        flat = a.astype(f32).reshape(-1)
        parts.append(jnp.pad(flat, (0, rows * PACK_COLS - flat.shape[0])).reshape(rows, PACK_COLS))
    return jnp.concatenate(parts, axis=0)


def _unpack(pack, shapes):
    out, row = [], 0
    for s in shapes:
        rows = _pack_rows_of(s)
        out.append(pack[row:row + rows].reshape(-1)[:math.prod(s)].reshape(s))
        row += rows
    return out


ANY = pl.BlockSpec(memory_space=pl.ANY)


def _place():
    return lax.axis_index('x'), lax.axis_index('y'), lax.axis_index('c')


def _where():
    return jnp.stack([lax.axis_index('c'), 2 * lax.axis_index('x') + lax.axis_index('y')]).astype(jnp.int32)


def _remote(src, dst, send_sems, recv_sems, k, to):
    return pltpu.make_async_remote_copy(src_ref=src, dst_ref=dst, send_sem=send_sems.at[k], recv_sem=recv_sems.at[k],
                                        device_id=to, device_id_type=MESH)


def _swap_with_sibling(gs, small):
    n = len(gs)

    def body(*refs):
        g_refs, s_ref = refs[:n], refs[n]
        r_refs, rs_ref = refs[n + 1:2 * n + 1], refs[2 * n + 1]
        send_sems, recv_sems = refs[2 * n + 2:]
        x, y, c = _place()
        sib = (x, y, 1 - c)
        cps = [_remote(g_refs[t].at[:, 1 - c], r_refs[t], send_sems, recv_sems, t, sib) for t in range(n)]
        cps.append(_remote(s_ref, rs_ref, send_sems, recv_sems, n, sib))
        for cp in cps:
            cp.start()
        for cp in cps:
            cp.wait()

    res = pl.pallas_call(
        body, name='swap_with_sibling', in_specs=[ANY] * (n + 1), out_specs=[ANY] * (n + 1),
        out_shape=[jax.ShapeDtypeStruct((N_SHARDS,) + g.shape[2:], g.dtype) for g in gs] +
                  [jax.ShapeDtypeStruct(small.shape, small.dtype)],
        scratch_shapes=[pltpu.SemaphoreType.DMA((n + 1,)), pltpu.SemaphoreType.DMA((n + 1,))],
    )(*gs, small)
    return list(res[:n]), res[n]


def _exchange_between_chips(cs, small):
    n = len(cs)

    def body(*refs):
        c_refs, s_ref = refs[:n], refs[n]
        r_refs, rs_ref = refs[n + 1:2 * n + 1], refs[2 * n + 1]
        send_sems, recv_sems = refs[2 * n + 2:]
        x, y, c = _place()
        chips = [(1 - x, y), (x, 1 - y), (1 - x, 1 - y)]
        cps = []
        for j, (px, py) in enumerate(chips):
            for t in range(n):
                cps.append(_remote(c_refs[t].at[2 * px + py], r_refs[t].at[j], send_sems, recv_sems, 3 * t + j, (px, py, c)))
            cps.append(_remote(s_ref, rs_ref.at[j], send_sems, recv_sems, 3 * n + j, (px, py, c)))
        for cp in cps:
            cp.start()
        for cp in cps:
            cp.wait()

    res = pl.pallas_call(
        body, name='exchange_between_chips', in_specs=[ANY] * (n + 1), out_specs=[ANY] * (n + 1),
        out_shape=[jax.ShapeDtypeStruct((3,) + c.shape[1:], c.dtype) for c in cs] +
                  [jax.ShapeDtypeStruct((3,) + small.shape, small.dtype)],
        scratch_shapes=[pltpu.SemaphoreType.DMA((3 * n + 3,)), pltpu.SemaphoreType.DMA((3 * n + 3,))],
    )(*cs, small)
    return list(res[:n]), res[n]


def _swap_partials(ts):
    n = len(ts)

    def body(*refs):
        t_refs, o_refs = refs[:n], refs[n:2 * n]
        send_sems, recv_sems = refs[2 * n:]
        x, y, c = _place()
        cps = [_remote(t_refs[t], o_refs[t], send_sems, recv_sems, t, (x, y, 1 - c)) for t in range(n)]
        for cp in cps:
            cp.start()
        for cp in cps:
            cp.wait()

    return pl.pallas_call(
        body, name='swap_partials', in_specs=[ANY] * n, out_specs=[ANY] * n,
        out_shape=[jax.ShapeDtypeStruct(t.shape, t.dtype) for t in ts],
        scratch_shapes=[pltpu.SemaphoreType.DMA((n,)), pltpu.SemaphoreType.DMA((n,))],
    )(*ts)


HBM = pl.BlockSpec(memory_space=pltpu.HBM)
SEM = pl.BlockSpec(memory_space=pltpu.SEMAPHORE)
DATAFLOW = pltpu.SideEffectType.DATAFLOW_SIDE_EFFECTING


def _in_hbm(t):
    return pltpu.with_memory_space_constraint(t, pltpu.HBM)


def _other_chips():
    x, y, c = _place()
    return [(1 - x, y, c), (x, 1 - y, c), (1 - x, 1 - y, c)]


def _to_chips_copies(src_refs, land_refs, send_sems, recv_sems, per_peer):
    x, y, _ = _place()
    cps = []
    for t, (src, land) in enumerate(zip(src_refs, land_refs)):
        for j, (px, py, pc) in enumerate(_other_chips()):
            s = src.at[2 * px + py] if per_peer else src
            d = land.at[j] if per_peer else land.at[2 * x + y]
            cps.append(_remote(s, d, send_sems, recv_sems, 3 * t + j, (px, py, pc)))
    return cps


def _to_chips_start(srcs, lands, per_peer, order, name):
    n = len(srcs)

    def body(*refs):
        src_refs, land_refs = refs[:n], refs[n:2 * n]
        send_sems, recv_sems = refs[2 * n + 1], refs[2 * n + 2]
        token = refs[-1]
        for cp in _to_chips_copies(src_refs, land_refs, send_sems, recv_sems, per_peer):
            cp.start()
        token[...] = jnp.zeros_like(token)

    res = pl.pallas_call(
        body, name=name, in_specs=[HBM] * (2 * n) + [ANY],
        out_specs=[SEM, SEM] + [HBM] * (2 * n) + [pl.BlockSpec(memory_space=pltpu.VMEM)],
        out_shape=[pltpu.SemaphoreType.DMA((3 * n,)), pltpu.SemaphoreType.DMA((3 * n,))] +
                  [pltpu.HBM(t.shape, t.dtype) for t in srcs] + [pltpu.HBM(t.shape, t.dtype) for t in lands] +
                  [jax.ShapeDtypeStruct((8, LANES), f32)],
        input_output_aliases={i: 2 + i for i in range(2 * n)},
        compiler_params=pltpu.CompilerParams(has_side_effects=DATAFLOW),
    )(*[_in_hbm(t) for t in srcs], *[_in_hbm(t) for t in lands], order)
    return res[0], res[1], list(res[2:2 + n]), list(res[2 + n:2 + 2 * n]), res[-1]


def _to_chips_wait(send_sems, recv_sems, srcs, lands, per_peer, after, name):
    n = len(srcs)

    def body(*refs):
        src_refs, land_refs = refs[:n], refs[n:2 * n]
        send_ref, recv_ref = refs[2 * n], refs[2 * n + 1]
        for cp in _to_chips_copies(src_refs, land_refs, send_ref, recv_ref, per_peer):
            cp.wait_send()
            cp.wait_recv()

    res = pl.pallas_call(
        body, name=name, in_specs=[HBM] * (2 * n) + [SEM, SEM, ANY],
        out_specs=[HBM] * (2 * n),
        out_shape=[pltpu.HBM(t.shape, t.dtype) for t in srcs] + [pltpu.HBM(t.shape, t.dtype) for t in lands],
        input_output_aliases={i: i for i in range(2 * n)},
        compiler_params=pltpu.CompilerParams(has_side_effects=DATAFLOW),
    )(*srcs, *lands, send_sems, recv_sems, after)
    return list(res[:n]), list(res[n:])


def _row_tile(a):
    return _pick(a, (512, 256, 128, 64, 32, 16, 8))


def _partial_sum(g, land, name):
    _, a, b = g.shape
    tr = _row_tile(a)

    def body(w_ref, g_ref, r_ref, o_ref):
        t = g_ref[0].astype(f32) + r_ref[0].astype(f32)
        t = t + r_ref[1].astype(f32)
        t = t + r_ref[2].astype(f32)
        o_ref[...] = t.astype(o_ref.dtype)

    return pl.pallas_call(
        body, name=name,
        grid_spec=pltpu.PrefetchScalarGridSpec(
            num_scalar_prefetch=1, grid=(a // tr,),
            in_specs=[pl.BlockSpec((1, tr, b), lambda i, w: (w[1], i, 0)), pl.BlockSpec((3, tr, b), lambda i, w: (0, i, 0))],
            out_specs=pl.BlockSpec((tr, b), lambda i, w: (i, 0))),
        out_shape=jax.ShapeDtypeStruct((a, b), bf16),
        compiler_params=_cparams(('parallel',)),
    )(_where(), g, land)


def _by_shard(name, t):
    r, c = t.shape
    if BIG[name] == 2:
        return t.reshape(r, N_SHARDS, c // N_SHARDS).transpose(1, 0, 2)
    return t.reshape(N_SHARDS, r // N_SHARDS, c)


def _from_shards(name, g):
    s, a, b = g.shape
    if BIG[name] == 2:
        return g.transpose(1, 0, 2).reshape(a, s * b)
    return g.reshape(s * a, b)


def _adam_math(w, g, m, v):
    m = ADAM_B1 * m + (1.0 - ADAM_B1) * g
    v = ADAM_B2 * v + (1.0 - ADAM_B2) * (g * g)
    m_hat = m / (1.0 - ADAM_B1 ** ADAM_STEP)
    v_hat = v / (1.0 - ADAM_B2 ** ADAM_STEP)
    delta = -ADAM_LR * (m_hat / (jnp.sqrt(v_hat) + ADAM_EPS) + ADAM_WD * w)
    return delta, m, v


def _small_update(own, sib, chips, w, m, v):
    def body(o_ref, s_ref, c_ref, w_ref, m_ref, v_ref, g_out, d_out, m_out, v_out):
        chip = o_ref[...] + s_ref[...]
        g = (chip + c_ref[0]) + (c_ref[1] + c_ref[2])
        d, mn, vn = _adam_math(w_ref[...], g, m_ref[...], v_ref[...])
        g_out[...] = g
        d_out[...] = d
        m_out[...] = mn
        v_out[...] = vn

    return pl.pallas_call(body, name='small_update', out_shape=[jax.ShapeDtypeStruct(own.shape, f32)] * 4)(
        own, sib, chips, w, m, v)


def _small_chip_sum(own, sib):
    def body(o_ref, s_ref, out):
        out[...] = o_ref[...] + s_ref[...]
    return pl.pallas_call(body, name='small_chip_sum', out_shape=jax.ShapeDtypeStruct(own.shape, f32))(own, sib)


def _adamw(w, m, v, mine, theirs, name):
    layers, a, b = w.shape
    tr = _row_tile(a)

    def body(w_ref, m_ref, v_ref, p0, p1, q0, q1, g_out, d_out, m_out, v_out):
        first = pl.program_id(0) == 0
        g = jnp.where(first, p0[...].astype(f32) + q0[...].astype(f32), p1[...].astype(f32) + q1[...].astype(f32))
        d, mn, vn = _adam_math(w_ref[0], g, m_ref[0], v_ref[0])
        g_out[0] = g
        d_out[0] = d
        m_out[0] = mn
        v_out[0] = vn

    full = pl.BlockSpec((1, tr, b), lambda l, i: (l, i, 0))
    part = pl.BlockSpec((tr, b), lambda l, i: (i, 0))
    return pl.pallas_call(body, name=name, grid=(layers, a // tr), in_specs=[full] * 3 + [part] * 4, out_specs=[full] * 4,
                          out_shape=[jax.ShapeDtypeStruct(w.shape, f32)] * 4,
                          compiler_params=_cparams(('parallel', 'parallel')))(w, m, v, *mine, *theirs)


def _dg(a, b, ca, cb):
    return lax.dot_general(a.astype(bf16), b.astype(bf16), (((ca,), (cb,)), ((), ())), preferred_element_type=f32)


@jax.custom_vjp
def _bmm(a, b):
    return _dg(a, b, 1, 0)


_bmm.defvjp(lambda a, b: (_dg(a, b, 1, 0), (a, b)), lambda r, g: (_dg(g, r[1], 1, 1), _dg(r[0], g, 0, 0)))


@jax.custom_vjp
def _bmm_nt(a, b):
    return _dg(a, b, 1, 1)


_bmm_nt.defvjp(lambda a, b: (_dg(a, b, 1, 1), (a, b)), lambda r, g: (_dg(g, r[1], 1, 0), _dg(g, r[0], 0, 0)))


@jax.custom_vjp
def _bmm_tn(a, b):
    return _dg(a, b, 0, 0)


_bmm_tn.defvjp(lambda a, b: (_dg(a, b, 0, 0), (a, b)), lambda r, g: (_dg(r[1], g, 1, 1), _dg(r[0], g, 1, 0)))


def _hdot(a, b):
    return jnp.dot(a, b, precision=HI, preferred_element_type=f32)


def _hdot_nt(a, b):
    return lax.dot_general(a, b, (((1,), (1,)), ((), ())), precision=HI, preferred_element_type=f32)


def _hdot_tn(a, b):
    return lax.dot_general(a, b, (((0,), (0,)), ((), ())), precision=HI, preferred_element_type=f32)


def _head_mask(h, width=GROUP_W):
    lane = lax.broadcasted_iota(jnp.int32, (1, width), 1)
    return ((lane >= h * HEAD_DIM) & (lane < (h + 1) * HEAD_DIM)).astype(f32)


def _rope_perm():
    p = np.zeros((LANES, LANES), np.float32)
    half = MLA_ROPE // 2
    for i in range(half):
        p[MLA_NOPE + half + i, MLA_NOPE + i] = -1.0
        p[MLA_NOPE + i, MLA_NOPE + half + i] = 1.0
    return jnp.asarray(p)


def _rope_tables(S):
    half = MLA_ROPE // 2
    freqs = ROPE_THETA ** (-jnp.arange(half, dtype=f32) / half)
    ang = jnp.arange(S, dtype=f32)[:, None] * freqs[None, :]
    cos, sin = jnp.cos(ang), jnp.sin(ang)
    ones, zeros = jnp.ones((S, MLA_NOPE), f32), jnp.zeros((S, LANES - MLA_DQK), f32)
    c_tab = jnp.concatenate([ones, cos, cos, zeros], axis=1)
    s_tab = jnp.concatenate([jnp.zeros((S, MLA_NOPE), f32), sin, sin, zeros], axis=1)
    return c_tab, s_tab


def _f_mla_pre(c_q, c_kv, krope, c_tab, s_tab, q_norm, kv_norm, wq0, wq1, wq2, wq3, wk0, wk1, wk2, wk3, wv, gq, gk, perm):
    wq, wk = (wq0, wq1, wq2, wq3), (wk0, wk1, wk2, wk3)
    nq = _rms(c_q, q_norm)
    nkv = _rms(c_kv, kv_norm)

    def norm_rope(t, g):
        t = t * lax.rsqrt(jnp.sum(t * t, axis=-1, keepdims=True) * (1.0 / MLA_DQK) + EPS) * g
        return t * c_tab + _hdot(t, perm) * s_tab

    qs = [norm_rope(_bmm(nq, wq[h]), gq) * (MLA_DQK ** -0.5) for h in range(N_HEADS)]
    ks = [norm_rope(_bmm(nkv, wk[h]) + krope, gk) for h in range(N_HEADS)]
    return (*qs, *ks, _bmm(nkv, wv))


def _f_attn(qs, ks, v, q0):
    tq, S = qs[0].shape[0], ks[0].shape[0]
    qpos = q0 + lax.broadcasted_iota(jnp.int32, (tq, S), 0)
    kpos = lax.broadcasted_iota(jnp.int32, (tq, S), 1)
    keep = kpos <= qpos
    logits = [jnp.where(keep, _bmm_nt(qs[h], ks[h]), NEG_INF) for h in range(N_HEADS)]
    ps = [jnp.exp(lg - jnp.max(lg, axis=-1, keepdims=True)) for lg in logits]
    ps = [p / jnp.sum(p, axis=-1, keepdims=True) for p in ps]
    return sum(_bmm(p, v) * _head_mask(h) for h, p in enumerate(ps))


ATTN_PARTS = 4


def _mla_attn_fwd(qs, ks, v, name):
    S = v.shape[0]
    tq = Q_BLOCK
    parts = ATTN_PARTS if S % (ATTN_PARTS * tq) == 0 else 1
    per = S // parts
    outs = []
    for p in range(parts):
        n_keys = (p + 1) * per
        first_block = p * (per // tq)

        def body(*refs, first_block=first_block):
            q_vals = [r[...] for r in refs[:4]]
            k_vals = [r[...] for r in refs[4:8]]
            refs[9][...] = _f_attn(q_vals, k_vals, refs[8][...], (first_block + pl.program_id(0)) * tq)

        qspec = pl.BlockSpec((tq, LANES), lambda i, fb=first_block: (fb + i, 0))
        outs.append(pl.pallas_call(
            body, name=f'{name}_{p}', grid=(per // tq,),
            in_specs=[qspec] * 4 + [pl.BlockSpec((n_keys, LANES), lambda i: (0, 0))] * 4 +
                     [pl.BlockSpec((n_keys, GROUP_W), lambda i: (0, 0))],
            out_specs=pl.BlockSpec((tq, GROUP_W), lambda i: (i, 0)),
            out_shape=jax.ShapeDtypeStruct((per, GROUP_W), f32),
            compiler_params=_cparams(('parallel',), big=True),
        )(*qs, *ks, v))
    return jnp.concatenate(outs, axis=0)


def _mla_attn_bwd(qs, ks, v, do, name):
    S = v.shape[0]
    tq = Q_BLOCK
    parts = ATTN_PARTS if S % (ATTN_PARTS * tq) == 0 else 1
    per = S // parts
    dq_parts, dkv_sum = [], None
    for p in range(parts):
        n_keys = (p + 1) * per
        first_block = p * (per // tq)

        def body(*refs, first_block=first_block):
            q_vals = [r[...].astype(f32) for r in refs[:4]]
            k_vals = [r[...].astype(f32) for r in refs[4:8]]
            v_val = refs[8][...].astype(f32)
            q0 = (first_block + pl.program_id(0)) * tq
            _, vjp = jax.vjp(lambda a, b, c: _f_attn(a, b, c, q0), q_vals, k_vals, v_val)
            dqs, dks, dv = vjp(refs[9][...])
            outs = refs[10:]
            for h in range(N_HEADS):
                outs[h][...] = dqs[h]
            first = pl.program_id(0) == 0
            for o, g in zip(outs[4:], (*dks, dv)):
                @pl.when(first)
                def _(o=o, g=g):
                    o[...] = g

                @pl.when(jnp.logical_not(first))
                def _(o=o, g=g):
                    o[...] += g

        qspec = pl.BlockSpec((tq, LANES), lambda i, fb=first_block: (fb + i, 0))
        kspec = pl.BlockSpec((n_keys, LANES), lambda i: (0, 0))
        vspec = pl.BlockSpec((n_keys, GROUP_W), lambda i: (0, 0))
        res = pl.pallas_call(
            body, name=f'{name}_{p}', grid=(per // tq,),
            in_specs=[qspec] * 4 + [kspec] * 4 + [vspec, pl.BlockSpec((tq, GROUP_W), lambda i, fb=first_block: (fb + i, 0))],
            out_specs=[pl.BlockSpec((tq, LANES), lambda i: (i, 0))] * 4 + [kspec] * 4 + [vspec],
            out_shape=[jax.ShapeDtypeStruct((per, LANES), f32)] * 4 + [jax.ShapeDtypeStruct((n_keys, LANES), f32)] * 4 +
                      [jax.ShapeDtypeStruct((n_keys, GROUP_W), f32)],
            compiler_params=_cparams(('arbitrary',), big=True),
        )(*qs, *ks, v, do)
        dq_parts.append(res[:4])
        dkv = [jnp.pad(t, ((0, S - n_keys), (0, 0))) for t in res[4:]]
        dkv_sum = dkv if dkv_sum is None else [a_ + b_ for a_, b_ in zip(dkv_sum, dkv)]
    dqs = [jnp.concatenate([dq_parts[p][h] for p in range(parts)], axis=0) for h in range(N_HEADS)]
    return dqs, dkv_sum[:4], dkv_sum[4]


def _mla_params(mp):
    pad = LANES - MLA_DQK
    wq = jnp.pad(mp['mla_w_uq'].reshape(GROUP_W, N_HEADS, MLA_DQK).transpose(1, 0, 2), ((0, 0), (0, 0), (0, pad)))
    wkv = mp['mla_w_ukv'].reshape(LANES, N_HEADS, MLA_NOPE + HEAD_DIM)
    wk = jnp.pad(wkv[:, :, :MLA_NOPE].transpose(1, 0, 2), ((0, 0), (0, 0), (0, LANES - MLA_NOPE)))
    wv = wkv[:, :, MLA_NOPE:].reshape(LANES, GROUP_W)
    gq = jnp.pad(mp['mla_qk_q'], (0, pad))[None]
    gk = jnp.pad(mp['mla_qk_k'], (0, pad))[None]
    return [mp['mla_q_norm'][None], mp['mla_kv_norm'][None], *[wq[h] for h in range(N_HEADS)],
            *[wk[h] for h in range(N_HEADS)], wv, gq, gk, _rope_perm()]


def _mla_fwd(c_q, c_kv, k_rope, mp, l):
    S = c_q.shape[0]
    tm = _pick(S, (256, 128))
    krope = jnp.pad(k_rope, ((0, 0), (MLA_NOPE, LANES - MLA_DQK)))
    c_tab, s_tab = _rope_tables(S)
    tiled = [c_q, c_kv, krope, c_tab, s_tab]
    params = _mla_params(mp)
    res = _tile_fwd(_f_mla_pre, tiled, params, [(LANES, bf16)] * 8 + [(GROUP_W, bf16)], tm, f'mla_pre_fwd_{l}')
    qs, ks, v = res[:4], res[4:8], res[8]
    y = _mla_attn_fwd(qs, ks, v, f'mla_attn_fwd_{l}')
    return y, (tiled, params, qs, ks, v)


def _mla_bwd(dy, saved, l):
    tiled, params, qs, ks, v = saved
    S = dy.shape[0]
    tm = _pick(S, (256, 128))
    dqs, dks, dv = _mla_attn_bwd(qs, ks, v, dy, f'mla_attn_bwd_{l}')
    (dc_q, dc_kv, dkrope), dpar = _tile_bwd(_f_mla_pre, tiled, params, [*dqs, *dks, dv], [True, True, True, False, False],
                                            [True] * 13 + [False], tm, f'mla_pre_bwd_{l}')
    dqn, dkvn = dpar[0], dpar[1]
    dwq, dwk = jnp.stack(dpar[2:6]), jnp.stack(dpar[6:10])
    dwv, dgq, dgk = dpar[10:13]
    dw_uq = dwq[:, :, :MLA_DQK].transpose(1, 0, 2).reshape(GROUP_W, N_HEADS * MLA_DQK)
    dw_ukv = jnp.concatenate([dwk[:, :, :MLA_NOPE].transpose(1, 0, 2), dwv.reshape(LANES, N_HEADS, HEAD_DIM)],
                             axis=2).reshape(LANES, N_HEADS * (MLA_NOPE + HEAD_DIM))
    grads = {'mla_q_norm': dqn[0], 'mla_kv_norm': dkvn[0], 'mla_w_uq': dw_uq, 'mla_w_ukv': dw_ukv,
             'mla_qk_q': dgq[0, :MLA_DQK], 'mla_qk_k': dgk[0, :MLA_DQK]}
    return dc_q, dc_kv, dkrope[:, MLA_NOPE:MLA_DQK], grads


SPAN = 128


def _head_mean_matrix():
    h = np.arange(GROUP_W) // HEAD_DIM
    return jnp.asarray((h[:, None] == h[None, :]).astype(np.float32) / HEAD_DIM)


def _f_dil_pre(q, k, gq, gk, hm):
    qn = q * lax.rsqrt(_hdot(q * q, hm) + EPS) * gq * (HEAD_DIM ** -0.5)
    kn = k * lax.rsqrt(_hdot(k * k, hm) + EPS) * gk
    return qn, kn


def _f_dil_branch(qb, kp, kc, vp, vc, b0, b1, b2, b3, first):
    kcat = jnp.concatenate([kp, kc], axis=0)
    vcat = jnp.concatenate([vp, vc], axis=0)
    qi = lax.broadcasted_iota(jnp.int32, (SPAN, 2 * SPAN), 0) + SPAN
    kj = lax.broadcasted_iota(jnp.int32, (SPAN, 2 * SPAN), 1)
    delta = qi - kj
    valid = (delta >= 0) & (delta <= SPAN) & jnp.logical_not(first & (kj < SPAN))
    masks = [_head_mask(h) for h in range(N_HEADS)]
    raw = [_bmm_nt(qb * hm, kcat) for hm in masks]
    logits = [jnp.where(valid, r + bias, NEG_INF) for r, bias in zip(raw, (b0, b1, b2, b3))]
    ms = [jnp.max(lg, axis=-1, keepdims=True) for lg in logits]
    ps = [jnp.exp(lg - m) for lg, m in zip(logits, ms)]
    pvs = [_bmm(p, vcat) for p in ps]
    o = sum(pv * hm for pv, hm in zip(pvs, masks))
    m_full = sum(m * hm for m, hm in zip(ms, masks))
    l_full = sum(jnp.sum(p, axis=-1, keepdims=True) * hm for p, hm in zip(ps, masks))
    return o, m_full, l_full


def _dil_branch_specs(d, nb):
    cur = pl.BlockSpec((SPAN, GROUP_W), lambda r, n: (n, r))
    prev = pl.BlockSpec((SPAN, GROUP_W), lambda r, n: (jnp.maximum(n - 1, 0), r))
    bias = pl.BlockSpec((1, SPAN, 2 * SPAN), lambda r, n: (0, 0, 0))
    return cur, prev, bias


def _head_table_specs():
    return [pl.BlockSpec((1, SPAN, 2 * SPAN), lambda r, n, h=h: (h, 0, 0)) for h in range(N_HEADS)]


def _dil_branch_fwd(q, k, v, table, name):
    L, d = q.shape[0], q.shape[1] // GROUP_W
    nb = L // SPAN
    cur, prev, bias = _dil_branch_specs(d, nb)

    def body(q_ref, kp_ref, kc_ref, vp_ref, vc_ref, b0, b1, b2, b3, o_ref, m_ref, l_ref):
        o, m, l = _f_dil_branch(q_ref[...], kp_ref[...], kc_ref[...], vp_ref[...], vc_ref[...], b0[0], b1[0], b2[0], b3[0],
                                pl.program_id(1) == 0)
        o_ref[...] = o
        m_ref[...] = m
        l_ref[...] = l

    return pl.pallas_call(
        body, name=name, grid=(d, nb), in_specs=[cur, prev, cur, prev, cur] + _head_table_specs(),
        out_specs=[cur] * 3, out_shape=[jax.ShapeDtypeStruct(q.shape, f32)] * 3,
        compiler_params=_cparams(('parallel', 'parallel')),
    )(q, k, k, v, v, *[table] * N_HEADS)


def _dil_branch_bwd(q, k, v, table, do, dm, dl, name):
    L, d = q.shape[0], q.shape[1] // GROUP_W
    nb = L // SPAN
    cur, prev, bias = _dil_branch_specs(d, nb)
    whole = pl.BlockSpec((L, GROUP_W), lambda r, n: (0, r))

    def body(q_ref, kp_ref, kc_ref, vp_ref, vc_ref, b0, b1, b2, b3, do_ref, dm_ref, dl_ref,
             dq_ref, dk_ref, dv_ref, db0, db1, db2, db3):
        r, n = pl.program_id(0), pl.program_id(1)
        first = n == 0
        _, vjp = jax.vjp(lambda *a: _f_dil_branch(*a, first), q_ref[...], kp_ref[...], kc_ref[...], vp_ref[...], vc_ref[...],
                         b0[0], b1[0], b2[0], b3[0])
        dq, dkp, dkc, dvp, dvc, g0, g1, g2, g3 = vjp((do_ref[...], dm_ref[...], dl_ref[...]))
        dq_ref[...] = dq

        @pl.when(first)
        def _():
            dk_ref[...] = jnp.zeros_like(dk_ref)
            dv_ref[...] = jnp.zeros_like(dv_ref)

        rows = pl.ds(pl.multiple_of(n * SPAN, SPAN), SPAN)
        dk_ref[rows, :] += dkc
        dv_ref[rows, :] += dvc

        @pl.when(n > 0)
        def _():
            before = pl.ds(pl.multiple_of((n - 1) * SPAN, SPAN), SPAN)
            dk_ref[before, :] += dkp
            dv_ref[before, :] += dvp

        start = first & (r == 0)
        for o, g in zip((db0, db1, db2, db3), (g0, g1, g2, g3)):
            @pl.when(start)
            def _(o=o, g=g):
                o[0] = g

            @pl.when(jnp.logical_not(start))
            def _(o=o, g=g):
                o[0] += g

    res = pl.pallas_call(
        body, name=name, grid=(d, nb), in_specs=[cur, prev, cur, prev, cur] + _head_table_specs() + [cur] * 3,
        out_specs=[cur, whole, whole] + [bias] * 4,
        out_shape=[jax.ShapeDtypeStruct(q.shape, f32)] * 3 + [jax.ShapeDtypeStruct((1, SPAN, 2 * SPAN), f32)] * 4,
        compiler_params=_cparams(('arbitrary', 'arbitrary')),
    )(q, k, k, v, v, *[table] * N_HEADS, do, dm, dl)
    return res[0], res[1], res[2], res[3:]


def _f_dil_merge(o1, m1, l1, o2, m2, l2, o3, m3, l3):
    mx = jnp.maximum(jnp.maximum(m1, m2), m3)
    w1, w2, w3 = jnp.exp(m1 - mx), jnp.exp(m2 - mx), jnp.exp(m3 - mx)
    return ((w1 * o1 + w2 * o2 + w3 * o3) / (w1 * l1 + w2 * l2 + w3 * l3),)


def _bias_onehot(dilation):
    qi = jnp.arange(SPAN, dtype=jnp.int32)[:, None] + SPAN
    kj = jnp.arange(2 * SPAN, dtype=jnp.int32)[None, :]
    bucket = _t5_bucket(jnp.clip(qi - kj, 0, SPAN) * dilation).reshape(-1)
    return (bucket[None, :] == jnp.arange(T5_BUCKETS, dtype=jnp.int32)[:, None]).astype(f32)


def _bias_tables(t5_t, onehot, name):
    N = onehot.shape[1]
    tn = _pick(N, (4096, 2048, 1024))

    def body(t_ref, oh_ref, o_ref):
        o_ref[...] = _hdot(t_ref[...], oh_ref[...])

    return pl.pallas_call(
        body, name=name, grid=(N // tn,),
        in_specs=[pl.BlockSpec((8, T5_BUCKETS), lambda i: (0, 0)), pl.BlockSpec((T5_BUCKETS, tn), lambda i: (0, i))],
        out_specs=pl.BlockSpec((8, tn), lambda i: (0, i)), out_shape=jax.ShapeDtypeStruct((8, N), f32),
        compiler_params=_cparams(('parallel',)),
    )(t5_t, onehot)


def _bias_tables_bwd(d_tab, onehot, name):
    N = onehot.shape[1]
    tn = _pick(N, (4096, 2048, 1024))

    def body(g_ref, oh_ref, o_ref):
        part = _hdot_nt(g_ref[...], oh_ref[...])

        @pl.when(pl.program_id(0) == 0)
        def _():
            o_ref[...] = part

        @pl.when(pl.program_id(0) > 0)
        def _():
            o_ref[...] += part

    return pl.pallas_call(
        body, name=name, grid=(N // tn,),
        in_specs=[pl.BlockSpec((8, tn), lambda i: (0, i)), pl.BlockSpec((T5_BUCKETS, tn), lambda i: (0, i))],
        out_specs=pl.BlockSpec((8, T5_BUCKETS), lambda i: (0, 0)), out_shape=jax.ShapeDtypeStruct((8, T5_BUCKETS), f32),
        compiler_params=_cparams(('arbitrary',)),
    )(d_tab, onehot)


def _by_residue(t, d):
    S, C = t.shape
    return t.reshape(S // d, d * C)


def _from_residue(t):
    return t.reshape(-1, GROUP_W)


def _dil_fwd(qkv, mp, l):
    S = qkv.shape[0]
    tm = _pick(S, (256, 128))
    q, k, v = qkv[:, :GROUP_W], qkv[:, GROUP_W:2 * GROUP_W], qkv[:, 2 * GROUP_W:]
    pre_params = [jnp.tile(mp['dil_q_norm'], N_HEADS)[None], jnp.tile(mp['dil_k_norm'], N_HEADS)[None], _head_mean_matrix()]
    qn, kn = _tile_fwd(_f_dil_pre, [q, k], pre_params, [(GROUP_W, f32)] * 2, tm, f'dil_pre_fwd_{l}')
    t5_t = jnp.pad(mp['t5_bias'].T, ((0, 8 - N_HEADS), (0, 0)))
    branches, outs = [], []
    for bi, (_, d) in enumerate(DIL_PAIRS):
        onehot = _bias_onehot(d)
        tab = _bias_tables(t5_t, onehot, f'dil_bias_fwd_{l}_{bi}').reshape(8, SPAN, 2 * SPAN)
        qd, kd, vd = _by_residue(qn, d), _by_residue(kn, d), _by_residue(v, d)
        o, m, lsum = _dil_branch_fwd(qd, kd, vd, tab, f'dil_branch_fwd_{l}_{bi}')
        branches.append((qd, kd, vd, tab, onehot))
        outs += [_from_residue(o), _from_residue(m), _from_residue(lsum)]
    (y,) = _tile_fwd(_f_dil_merge, outs, [], [(GROUP_W, f32)], tm, f'dil_merge_fwd_{l}')
    return y, (q, k, pre_params, branches, outs)


def _dil_bwd(dy, saved, l):
    q, k, pre_params, branches, outs = saved
    S = dy.shape[0]
    tm = _pick(S, (256, 128))
    douts, _ = _tile_bwd(_f_dil_merge, outs, [], [dy], [True] * 9, [], tm, f'dil_merge_bwd_{l}')
    dqn = dkn = dv = None
    dt5_t = None
    for bi, (_, d) in enumerate(DIL_PAIRS):
        qd, kd, vd, tab, onehot = branches[bi]
        do, dm, dl = [_by_residue(t, d) for t in douts[3 * bi:3 * bi + 3]]
        dq_b, dk_b, dv_b, dbias = _dil_branch_bwd(qd, kd, vd, tab, do, dm, dl, f'dil_branch_bwd_{l}_{bi}')
        d_tab = jnp.concatenate([*dbias, jnp.zeros((8 - N_HEADS, SPAN, 2 * SPAN), f32)], axis=0).reshape(8, -1)
        g_t5 = _bias_tables_bwd(d_tab, onehot, f'dil_bias_bwd_{l}_{bi}')
        dq_b, dk_b, dv_b = _from_residue(dq_b), _from_residue(dk_b), _from_residue(dv_b)
        dqn = dq_b if dqn is None else dqn + dq_b
        dkn = dk_b if dkn is None else dkn + dk_b
        dv = dv_b if dv is None else dv + dv_b
        dt5_t = g_t5 if dt5_t is None else dt5_t + g_t5
    (dq, dk), (dgq, dgk) = _tile_bwd(_f_dil_pre, [q, k], pre_params, [dqn, dkn], [True, True], [True, True, False], tm,
                                     f'dil_pre_bwd_{l}')
    grads = {'dil_q_norm': dgq.reshape(N_HEADS, HEAD_DIM).sum(0), 'dil_k_norm': dgk.reshape(N_HEADS, HEAD_DIM).sum(0),
             't5_bias': dt5_t[:N_HEADS].T}
    return jnp.concatenate([dq, dk, dv], axis=1), grads


S5_LANES = S5_G * S5_P
SCAN_SEGMENTS = 8
SCAN_W = 256


def _f_s5_prep(bre, bim, lr, li, logdt_col, expand):
    dt = jnp.sum(jnp.exp(logdt_col) * expand, axis=0, keepdims=True)
    mag = jnp.exp(lr * dt)
    ar, ai = mag * jnp.cos(li * dt), mag * jnp.sin(li * dt)
    den = lr * lr + li * li
    nr, ni = ar - 1.0, ai
    zr = (nr * lr + ni * li) / den
    zi = (ni * lr - nr * li) / den
    bb = jnp.concatenate([zr * bre - zi * bim, zr * bim + zi * bre], axis=1)
    a_rows = jnp.broadcast_to(jnp.concatenate([ar, ai], axis=1), bb.shape)
    return bb, a_rows


def _s5_scan(x, a_rows, name, reverse=False, h=None):
    S = x.shape[0]
    NL = x.shape[1] // 2
    T = S // SCAN_SEGMENTS
    nblk = NL // SCAN_W
    n_in = 4 if reverse else 2

    def body(*refs):
        if reverse:
            (x_hbm, pr_hbm, pi_hbm, ar_ref, ai_ref, hr_hbm, hi_hbm, dar_ref, dai_ref,
             xr_s, xi_s, pr_s, pi_s, hr_s, hi_s, in_sems, out_sems) = refs
        else:
            x_hbm, ar_ref, ai_ref, hr_hbm, hi_hbm, xr_s, xi_s, hr_s, hi_s, in_sems, out_sems = refs
        col = pl.multiple_of(pl.program_id(0) * SCAN_W, SCAN_W)
        loads = []
        for k in range(SCAN_SEGMENTS):
            rows = pl.ds(k * T, T)
            sources = [(x_hbm, col, xr_s), (x_hbm, NL + col, xi_s)]
            if reverse:
                sources += [(pr_hbm, col, pr_s), (pi_hbm, col, pi_s)]
            for i, (src, c0, dst) in enumerate(sources):
                loads.append(pltpu.make_async_copy(src.at[rows, pl.ds(c0, SCAN_W)], dst.at[:, k, :],
                                                   in_sems.at[i * SCAN_SEGMENTS + k]))
        for cp in loads:
            cp.start()
        for cp in loads:
            cp.wait()
        ar = ar_ref[...]
        ai = -ai_ref[...] if reverse else ai_ref[...]
        zero = jnp.zeros((SCAN_SEGMENTS, SCAN_W), f32)

        def at(s):
            return T - 1 - s if reverse else s

        def local(s, c):
            hr, hi, pr, pi = c
            j = at(s)
            nhr = ar * hr - ai * hi + xr_s[j]
            nhi = ar * hi + ai * hr + xi_s[j]
            hr_s[j] = nhr
            hi_s[j] = nhi
            return nhr, nhi, ar * pr - ai * pi, ar * pi + ai * pr

        er, ei, pr, pi = lax.fori_loop(0, T, local, (zero, zero, zero + 1.0, zero), unroll=2)
        row = lax.broadcasted_iota(jnp.int32, (SCAN_SEGMENTS, SCAN_W), 0)
        cr, ci = zero, zero
        order = range(SCAN_SEGMENTS - 2, -1, -1) if reverse else range(1, SCAN_SEGMENTS)
        for k in order:
            src = k + 1 if reverse else k - 1
            tr = er + pr * cr - pi * ci
            ti = ei + pr * ci + pi * cr
            cr = jnp.where(row == k, jnp.sum(jnp.where(row == src, tr, 0.0), axis=0, keepdims=True), cr)
            ci = jnp.where(row == k, jnp.sum(jnp.where(row == src, ti, 0.0), axis=0, keepdims=True), ci)

        def fix_at(j, c, before):
            pr, pi, sr, si = c
            pr, pi = ar * pr - ai * pi, ar * pi + ai * pr
            hr = hr_s[j] + pr * cr - pi * ci
            hi = hi_s[j] + pr * ci + pi * cr
            hr_s[j] = hr
            hi_s[j] = hi
            if reverse:
                qr, qi = before
                sr = sr + hr * qr + hi * qi
                si = si + hi * qr - hr * qi
            return pr, pi, sr, si

        start = (zero + 1.0, zero, zero, zero)
        if reverse:
            def fix(s, c):
                j = T - 1 - s
                return fix_at(j, c, (pr_s[j - 1], pi_s[j - 1]))

            c = lax.fori_loop(0, T - 1, fix, start, unroll=2)
            last_r = jnp.where(row == 0, 0.0, pltpu.roll(pr_s[T - 1], 1, 0))
            last_i = jnp.where(row == 0, 0.0, pltpu.roll(pi_s[T - 1], 1, 0))
            _, _, sr, si = fix_at(0, c, (last_r, last_i))
            dar_ref[...] = sr
            dai_ref[...] = si
        else:
            lax.fori_loop(0, T, lambda s, c: fix_at(s, c, None), start, unroll=2)
        stores = []
        for k in range(SCAN_SEGMENTS):
            rows = pl.ds(k * T, T)
            stores.append(pltpu.make_async_copy(hr_s.at[:, k, :], hr_hbm.at[rows, pl.ds(col, SCAN_W)], out_sems.at[k]))
            stores.append(pltpu.make_async_copy(hi_s.at[:, k, :], hi_hbm.at[rows, pl.ds(col, SCAN_W)],
                                                out_sems.at[SCAN_SEGMENTS + k]))
        for cp in stores:
            cp.start()
        for cp in stores:
            cp.wait()

    a_re = pl.BlockSpec((SCAN_SEGMENTS, SCAN_W), lambda b: (0, b))
    a_im = pl.BlockSpec((SCAN_SEGMENTS, SCAN_W), lambda b: (0, nblk + b))
    seq = pltpu.VMEM((T, SCAN_SEGMENTS, SCAN_W), f32)
    if reverse:
        in_specs, args = [ANY, ANY, ANY, a_re, a_im], [x, h[0], h[1], a_rows, a_rows]
        out_specs = [ANY, ANY, a_re, a_re]
        out_shape = [jax.ShapeDtypeStruct((S, NL), f32)] * 2 + [jax.ShapeDtypeStruct((SCAN_SEGMENTS, NL), f32)] * 2
    else:
        in_specs, args = [ANY, a_re, a_im], [x, a_rows, a_rows]
        out_specs = [ANY, ANY]
        out_shape = [jax.ShapeDtypeStruct((S, NL), f32)] * 2
    scratch = [seq] * (n_in + 2) + [pltpu.SemaphoreType.DMA((n_in * SCAN_SEGMENTS,)),
                                    pltpu.SemaphoreType.DMA((2 * SCAN_SEGMENTS,))]
    return pl.pallas_call(body, name=name, grid=(nblk,), in_specs=in_specs, out_specs=out_specs, out_shape=out_shape,
                          scratch_shapes=scratch, compiler_params=_cparams(('arbitrary',), big=True))(*args)


def _f_s5_post(y, u, d, w_glu):
    z = _bmm(y + d * u, w_glu)
    return (z[:, :GROUP_W] * jax.nn.sigmoid(z[:, GROUP_W:]),)


def _block_diag(t):
    G, a, b = t.shape
    eye = jnp.eye(G, dtype=t.dtype)
    return (t[:, :, None, :] * eye[:, None, :, None]).reshape(G * a, G * b)


def _diag_blocks(m, a, b):
    G = m.shape[0] // a
    return jnp.moveaxis(jnp.diagonal(m.reshape(G, a, G, b), axis1=0, axis2=2), -1, 0)


def _s5_fwd(u, mp, l):
    S = u.shape[0]
    tm = _pick(S, (256, 128))
    bre = _block_diag(mp['s5_b_re'].transpose(0, 2, 1))
    bim = _block_diag(mp['s5_b_im'].transpose(0, 2, 1))
    expand = jnp.repeat(jnp.eye(S5_G, dtype=f32), S5_P, axis=1)
    prep_params = [mp['s5_lambda_re'].reshape(1, S5_LANES), mp['s5_lambda_im'].reshape(1, S5_LANES),
                   mp['s5_log_dt'].reshape(S5_G, 1), expand]
    bb, a_rows = _tile_fwd(_f_s5_prep, [bre, bim], prep_params, [(2 * S5_LANES, f32)] * 2, GROUP_W, f's5_prep_fwd_{l}')
    x = _mm(u, bb, 'nn', f's5_in_fwd_{l}')
    hr, hi = _s5_scan(x, a_rows, f's5_scan_fwd_{l}')
    c_re, c_im = _block_diag(mp['s5_c_re'].transpose(0, 2, 1)), -_block_diag(mp['s5_c_im'].transpose(0, 2, 1))
    y = _mm(hi, c_im, 'nn', f's5_out_im_fwd_{l}', add=_mm(hr, c_re, 'nn', f's5_out_re_fwd_{l}'))
    post_params = [mp['s5_d'][None], mp['s5_w_glu']]
    (out,) = _tile_fwd(_f_s5_post, [y, u], post_params, [(GROUP_W, f32)], tm, f's5_post_fwd_{l}')
    return out, (u, bre, bim, prep_params, bb, a_rows, hr, hi, c_re, c_im, y, post_params)


def _s5_bwd(dout, saved, l):
    u, bre, bim, prep_params, bb, a_rows, hr, hi, c_re, c_im, y, post_params = saved
    S = u.shape[0]
    tm = _pick(S, (256, 128))
    (dy, du1), (dd, dwglu) = _tile_bwd(_f_s5_post, [y, u], post_params, [dout], [True, True], [True, True], tm,
                                       f's5_post_bwd_{l}')
    ccat = jnp.concatenate([c_re, c_im], axis=0)
    dh = _mm(dy, ccat, 'nt', f's5_out_dx_{l}')
    dccat = jnp.concatenate([_mm(hr, dy, 'tn', f's5_out_re_dw_{l}'), _mm(hi, dy, 'tn', f's5_out_im_dw_{l}')], axis=0)
    lr_, li_, dar, dai = _s5_scan(dh, a_rows, f's5_scan_bwd_{l}', reverse=True, h=(hr, hi))
    du2 = _mm(li_, bb[:, S5_LANES:], 'nt', f's5_in_im_dx_{l}', add=_mm(lr_, bb[:, :S5_LANES], 'nt', f's5_in_re_dx_{l}'))
    dbb = jnp.concatenate([_mm(u, lr_, 'tn', f's5_in_re_dw_{l}'), _mm(u, li_, 'tn', f's5_in_im_dw_{l}')], axis=1)
    da_rows = jnp.pad(jnp.concatenate([dar, dai], axis=1), ((0, GROUP_W - SCAN_SEGMENTS), (0, 0)))
    (dbre, dbim), (dlr, dli, dlogdt) = _tile_bwd(_f_s5_prep, [bre, bim], prep_params, [dbb, da_rows], [True, True],
                                                 [True, True, True, False], GROUP_W, f's5_prep_bwd_{l}')
    grads = {
        's5_lambda_re': dlr.reshape(S5_G, S5_P), 's5_lambda_im': dli.reshape(S5_G, S5_P), 's5_log_dt': dlogdt[:, 0],
        's5_b_re': _diag_blocks(dbre, S5_CG, S5_P).transpose(0, 2, 1),
        's5_b_im': _diag_blocks(dbim, S5_CG, S5_P).transpose(0, 2, 1),
        's5_c_re': _diag_blocks(dccat[:S5_LANES], S5_P, S5_CG).transpose(0, 2, 1),
        's5_c_im': -_diag_blocks(dccat[S5_LANES:], S5_P, S5_CG).transpose(0, 2, 1),
        's5_d': dd[0], 's5_w_glu': dwglu}
    return du1 + du2, grads


DN_CONV = 4


def _head_sum_matrix():
    h = np.arange(GROUP_W) // HEAD_DIM
    return jnp.asarray((h[:, None] == h[None, :]).astype(np.float32))


def _f_dn_pre(x0, x1, x2, x3, ab, w0, w1, w2, w3, alog, dtb, ea, eb, hs):
    c = w0 * x0 + w1 * x1 + w2 * x2 + w3 * x3
    s = c * jax.nn.sigmoid(c)
    q, k, v = s[:, :GROUP_W], s[:, GROUP_W:2 * GROUP_W], s[:, 2 * GROUP_W:]
    q = q * lax.rsqrt(_hdot(q * q, hs) + EPS) * (HEAD_DIM ** -0.5)
    k = k * lax.rsqrt(_hdot(k * k, hs) + EPS)
    beta = jax.nn.sigmoid(_hdot(ab, eb))
    g = -jnp.exp(alog) * jax.nn.softplus(_hdot(ab, ea) + dtb)
    return q, k, v, g, beta


DN_CHUNKS_PER_STEP = 4


def _f_dn_chunks(q, k, v, g, beta):
    C = DN_CHUNK
    n_chunks = q.shape[0] // C
    r = lax.broadcasted_iota(jnp.int32, (C, C), 0)
    c = lax.broadcasted_iota(jnp.int32, (C, C), 1)
    causal, strict = r >= c, r > c
    eye = (r == c).astype(f32)
    tril = causal.astype(f32)
    ones = jnp.ones((C, GROUP_W), f32)
    masks = [_head_mask(h) for h in range(N_HEADS)]
    rows = [tuple(t[i * C:(i + 1) * C] for t in (q, k, v, g, beta)) for i in range(n_chunks)]
    gcs = [_hdot(tril, gi) for (_, _, _, gi, _) in rows]
    items = [(i, h) for i in range(n_chunks) for h in range(N_HEADS)]
    grows = [_hdot_nt(ones * (masks[h] * (1.0 / HEAD_DIM)), gcs[i]) for i, h in items]
    decs = []
    for (i, h), grow in zip(items, grows):
        gcol = jnp.sum(gcs[i] * masks[h], axis=1, keepdims=True) * (1.0 / HEAD_DIM)
        decs.append(jnp.exp(jnp.where(causal, gcol - grow, NEG_INF)))
    kbs = [ki * bi for (_, ki, _, _, bi) in rows]
    kks = [_bmm_nt(kbs[i] * masks[h], rows[i][1]) for i, h in items]
    qks = [_bmm_nt(rows[i][0] * masks[h], rows[i][1]) for i, h in items]
    lmats = [jnp.where(strict, kk * dec, 0.0) for kk, dec in zip(kks, decs)]
    a_qk = [jnp.where(causal, qk * dec, 0.0) for qk, dec in zip(qks, decs)]
    ts = [eye - lm for lm in lmats]
    ps = lmats
    for _ in range(5):
        ps = [_bmm(p, p) for p in ps]
        ts = [t + _bmm(t, p) for t, p in zip(ts, ps)]
    egs = [jnp.exp(gc) for gc in gcs]
    tw = [_bmm(t, kbs[i] * egs[i]) for (i, h), t in zip(items, ts)]
    tu = [_bmm(t, rows[i][2] * rows[i][4]) for (i, h), t in zip(items, ts)]
    outs = []
    for i in range(n_chunks):
        qi, ki, _, gi, _ = rows[i]
        glast = jnp.sum(gi, axis=0, keepdims=True)
        w = sum(tw[i * N_HEADS + h] * masks[h] for h in range(N_HEADS))
        u = sum(tu[i * N_HEADS + h] * masks[h] for h in range(N_HEADS))
        outs.append((w, u, qi * egs[i], ki * jnp.exp(glast - gcs[i]), *a_qk[i * N_HEADS:(i + 1) * N_HEADS],
                     jnp.broadcast_to(jnp.exp(glast), (C, GROUP_W))))
    return tuple(jnp.concatenate(parts, axis=0) for parts in zip(*outs))


def _f_dn_step(w, u, qd, kdec, a0, a1, a2, a3, dfull, state, bd):
    row0 = (lax.broadcasted_iota(jnp.int32, dfull.shape, 0) == 0).astype(f32)
    dvec = jnp.sum(dfull * row0, axis=0, keepdims=True)
    ws, qs = _bmm(w, state), _bmm(qd, state)
    vnew = u - ws
    avs = [_bmm(a, vnew) for a in (a0, a1, a2, a3)]
    kv = _bmm_tn(kdec, vnew)
    o = qs + sum(av * _head_mask(h) for h, av in enumerate(avs))
    return o, state * dvec + bd * kv


def _dn_scan_fwd(ins, name):
    S = ins[0].shape[0]
    N = S // DN_CHUNK
    bd = _head_sum_matrix()

    def body(*refs):
        o_ref, s_ref, state = refs[10], refs[11], refs[12]

        @pl.when(pl.program_id(0) == 0)
        def _():
            state[...] = jnp.zeros_like(state)

        s_in = state[...]
        s_ref[0] = s_in
        o, s_out = _f_dn_step(*[r[...] for r in refs[:9]], s_in, refs[9][...])
        o_ref[...] = o
        state[...] = s_out

    return pl.pallas_call(
        body, name=name, grid=(N,),
        in_specs=[pl.BlockSpec((DN_CHUNK, t.shape[1]), lambda n: (n, 0)) for t in ins] + [_full_spec(bd)],
        out_specs=[pl.BlockSpec((DN_CHUNK, GROUP_W), lambda n: (n, 0)), pl.BlockSpec((1, GROUP_W, GROUP_W), lambda n: (n, 0, 0))],
        out_shape=[jax.ShapeDtypeStruct((S, GROUP_W), f32), jax.ShapeDtypeStruct((N, GROUP_W, GROUP_W), f32)],
        scratch_shapes=[pltpu.VMEM((GROUP_W, GROUP_W), f32)],
        compiler_params=_cparams(('arbitrary',)),
    )(*ins, bd)


def _dn_scan_bwd(ins, states, do, name):
    S = ins[0].shape[0]
    N = S // DN_CHUNK
    bd = _head_sum_matrix()

    def body(*refs):
        s_ref, do_ref = refs[9], refs[10]
        bd_ref = refs[11]
        outs = refs[12:21]
        dstate = refs[21]

        @pl.when(pl.program_id(0) == 0)
        def _():
            dstate[...] = jnp.zeros_like(dstate)

        bd_val = bd_ref[...]
        _, vjp = jax.vjp(lambda *a: _f_dn_step(*a, bd_val), *[r[...] for r in refs[:9]], s_ref[0])
        grads = vjp((do_ref[...], dstate[...]))
        for o, g in zip(outs, grads[:9]):
            o[...] = g
        dstate[...] = grads[9]

    def rev(n):
        return (N - 1 - n, 0)

    res = pl.pallas_call(
        body, name=name, grid=(N,),
        in_specs=[pl.BlockSpec((DN_CHUNK, t.shape[1]), rev) for t in ins] +
                 [pl.BlockSpec((1, GROUP_W, GROUP_W), lambda n: (N - 1 - n, 0, 0)), pl.BlockSpec((DN_CHUNK, GROUP_W), rev),
                  _full_spec(bd)],
        out_specs=[pl.BlockSpec((DN_CHUNK, t.shape[1]), rev) for t in ins],
        out_shape=[jax.ShapeDtypeStruct(t.shape, f32) for t in ins],
        scratch_shapes=[pltpu.VMEM((GROUP_W, GROUP_W), f32)],
        compiler_params=_cparams(('arbitrary',)),
    )(*ins, states, do, bd)
    return list(res)


def _f_dn_post(o, gate, gain, hmean):
    return (o * lax.rsqrt(_hdot(o * o, hmean) + EPS) * gain * (gate * jax.nn.sigmoid(gate)),)


def _delay(t, j):
    return t if j == 0 else jnp.pad(t[:-j], ((j, 0), (0, 0)))


def _advance(t, j):
    return t if j == 0 else jnp.pad(t[j:], ((0, j), (0, 0)))


def _dn_fwd(qkv, a, b, gate, mp, l):
    S = qkv.shape[0]
    tm = _pick(S, (256, 128))
    xs = [_delay(qkv, DN_CONV - 1 - j) for j in range(DN_CONV)]
    ab = jnp.pad(jnp.concatenate([a, b], axis=1), ((0, 0), (0, LANES - 2 * N_HEADS)))
    sel = np.zeros((2, LANES, GROUP_W), np.float32)
    for h in range(N_HEADS):
        sel[0, h, h * HEAD_DIM:(h + 1) * HEAD_DIM] = 1.0
        sel[1, N_HEADS + h, h * HEAD_DIM:(h + 1) * HEAD_DIM] = 1.0
    pre_params = [*[mp['dn_conv'][j][None] for j in range(DN_CONV)], jnp.repeat(mp['dn_a_log'], HEAD_DIM)[None],
                  jnp.repeat(mp['dn_dt_bias'], HEAD_DIM)[None], jnp.asarray(sel[0]), jnp.asarray(sel[1]), _head_sum_matrix()]
    pre = _tile_fwd(_f_dn_pre, [*xs, ab], pre_params, [(GROUP_W, f32)] * 5, tm, f'dn_pre_fwd_{l}')
    chunk_outs = [(GROUP_W, f32)] * 4 + [(HEAD_DIM, f32)] * 4 + [(GROUP_W, f32)]
    parts = _tile_fwd(_f_dn_chunks, pre, [], chunk_outs, DN_CHUNK * DN_CHUNKS_PER_STEP, f'dn_chunk_fwd_{l}')
    o, states = _dn_scan_fwd(parts, f'dn_scan_fwd_{l}')
    post_params = [jnp.tile(mp['dn_o_norm'], N_HEADS)[None], _head_mean_matrix()]
    (y,) = _tile_fwd(_f_dn_post, [o, gate], post_params, [(GROUP_W, f32)], tm, f'dn_post_fwd_{l}')
    return y, (xs, ab, pre_params, pre, parts, states, o, gate, post_params)


def _dn_bwd(dy, saved, l):
    xs, ab, pre_params, pre, parts, states, o, gate, post_params = saved
    S = dy.shape[0]
    tm = _pick(S, (256, 128))
    (do, dgate), (dgain,) = _tile_bwd(_f_dn_post, [o, gate], post_params, [dy], [True, True], [True, False], tm,
                                      f'dn_post_bwd_{l}')
    dparts = _dn_scan_bwd(parts, states, do, f'dn_scan_bwd_{l}')
    dpre, _ = _tile_bwd(_f_dn_chunks, pre, [], dparts, [True] * 5, [], DN_CHUNK * DN_CHUNKS_PER_STEP, f'dn_chunk_bwd_{l}')
    dins, dpar = _tile_bwd(_f_dn_pre, [*xs, ab], pre_params, dpre, [True] * 5, [True] * 6 + [False] * 3, tm,
                           f'dn_pre_bwd_{l}')
    dqkv = dins[DN_CONV - 1]
    for j in range(DN_CONV - 1):
        dqkv = dqkv + _advance(dins[j], DN_CONV - 1 - j)
    dab = dins[DN_CONV]
    grads = {'dn_conv': jnp.concatenate(dpar[:DN_CONV], axis=0),
             'dn_a_log': dpar[4].reshape(N_HEADS, HEAD_DIM).sum(1), 'dn_dt_bias': dpar[5].reshape(N_HEADS, HEAD_DIM).sum(1),
             'dn_o_norm': dgain.reshape(N_HEADS, HEAD_DIM).sum(0)}
    return dqkv, dab[:, :N_HEADS], dab[:, N_HEADS:2 * N_HEADS], dgate, grads


def _t5_bucket(dist):
    exact = T5_BUCKETS // 2
    df = jnp.maximum(dist, 1).astype(f32)
    large = exact + (jnp.log(df / exact) / math.log(T5_MAX_DIST / exact) * (T5_BUCKETS - exact)).astype(jnp.int32)
    large = jnp.minimum(large, T5_BUCKETS - 1)
    return jnp.where(dist < exact, dist, large)


def _split_cols(t, sizes):
    out, start = [], 0
    for s in sizes:
        out.append(t[..., start:start + s])
        start += s
    return out


def _mixers_fwd(proj, mp, l):
    c_q, c_kv, k_rope, u_s5, qkv_dil, qkv_dn, a_dn, b_dn, gate_dn = _split_cols(proj, IN_SPLITS)
    y_mla, s_mla = _mla_fwd(c_q, c_kv, k_rope, mp, l)
    y_s5, s_s5 = _s5_fwd(u_s5, mp, l)
    y_dil, s_dil = _dil_fwd(qkv_dil, mp, l)
    y_dn, s_dn = _dn_fwd(qkv_dn, a_dn, b_dn, gate_dn, mp, l)
    return jnp.concatenate([y_mla, y_s5, y_dil, y_dn], axis=-1), (s_mla, s_s5, s_dil, s_dn)


def _mixers_bwd(dmixed, saved, l):
    s_mla, s_s5, s_dil, s_dn = saved
    d_mla, d_s5, d_dil, d_dn = _split_cols(dmixed, (GROUP_W,) * 4)
    dc_q, dc_kv, dk_rope, g_mla = _mla_bwd(d_mla, s_mla, l)
    du, g_s5 = _s5_bwd(d_s5, s_s5, l)
    dqkv_dil, g_dil = _dil_bwd(d_dil, s_dil, l)
    dqkv_dn, da, db, dgate, g_dn = _dn_bwd(d_dn, s_dn, l)
    parts = [dc_q, dc_kv, dk_rope, du, dqkv_dil, dqkv_dn, da, db, dgate]
    dproj = jnp.concatenate([p.astype(bf16) for p in parts], axis=-1)
    return dproj, {**g_mla, **g_s5, **g_dil, **g_dn}


MIXER_PARAMS = ['mla_q_norm', 'mla_kv_norm', 'mla_w_uq', 'mla_w_ukv', 'mla_qk_q', 'mla_qk_k', 's5_lambda_re',
                's5_lambda_im', 's5_log_dt', 's5_b_re', 's5_b_im', 's5_c_re', 's5_c_im', 's5_d', 's5_w_glu',
                'dil_q_norm', 'dil_k_norm', 't5_bias', 'dn_conv', 'dn_a_log', 'dn_dt_bias', 'dn_o_norm']


def _layer_fwd_mix(h, W, l):
    S = h.shape[0]
    tm = _pick(S, (256, 128))
    g1 = W['attn_norm'][l][None]
    (n1,) = _tile_fwd(_f_rms, [h], [g1], [(D_MODEL, bf16)], tm, f'rms1_fwd_{l}')
    proj = _mm(n1, W['w_in'][l], 'nn', f'proj_fwd_{l}')
    mp = {k: (W[k] if k == 't5_bias' else W[k][l]).astype(f32) for k in MIXER_PARAMS}
    mixed, mix_saved = _mixers_fwd(proj, mp, l)
    mixed_b = mixed.astype(bf16)
    h2 = _mm(mixed_b, W['w_out'][l], 'nn', f'out_fwd_{l}', add=h)
    return h2, dict(h=h, n1=n1, mix=mix_saved, mixed=mixed_b, h2=h2)


def _layer_fwd_ffn(h2, W, l, saved):
    S = h2.shape[0]
    tm = _pick(S, (256, 128))
    g2 = W['ffn_norm'][l][None]
    (n2,) = _tile_fwd(_f_rms, [h2], [g2], [(D_MODEL, bf16)], tm, f'rms2_fwd_{l}')
    u = _mm(n2, W['ffn_w1'][l], 'nn', f'ffn1_fwd_{l}')
    v = _mm(n2, W['ffn_w3'][l], 'nn', f'ffn3_fwd_{l}')
    (act,) = _tile_fwd(_f_swiglu, [u, v], [], [(FFN_HIDDEN, bf16)], tm, f'swiglu_fwd_{l}')
    h3 = _mm(act, W['ffn_w2'][l], 'nn', f'ffn2_fwd_{l}', add=h2)
    saved.update(n2=n2, u=u, v=v, act=act)
    return h3


def _layer_bwd_ffn(dh3, saved, W, l):
    S = dh3.shape[0]
    tm = _pick(S, (256, 128))
    g2 = W['ffn_norm'][l][None]
    grads = {}
    dact = _mm(dh3, W['ffn_w2'][l], 'nt', f'ffn2_dx_{l}')
    grads['ffn_w2'] = _mm(saved['act'], dh3, 'tn', f'ffn2_dw_{l}', out_dtype=bf16)
    (du, dv), _ = _tile_bwd(_f_swiglu, [saved['u'], saved['v']], [], [dact], [True, True], [], tm, f'swiglu_bwd_{l}',
                            dt_dtypes=[bf16, bf16])
    dn2 = _mm(dv, W['ffn_w3'][l], 'nt', f'ffn3_dx_{l}', add=_mm(du, W['ffn_w1'][l], 'nt', f'ffn1_dx_{l}'))
    grads['ffn_w1'] = _mm(saved['n2'], du, 'tn', f'ffn1_dw_{l}', out_dtype=bf16)
    grads['ffn_w3'] = _mm(saved['n2'], dv, 'tn', f'ffn3_dw_{l}', out_dtype=bf16)
    (dh2n,), (dg2,) = _tile_bwd(_f_rms, [saved['h2']], [g2], [dn2], [True], [True], tm, f'rms2_bwd_{l}')
    grads['ffn_norm'] = dg2[0]
    return (dh3, dh2n), grads


def _layer_bwd_mix(dh2, saved, W, l):
    S = dh2.shape[0]
    tm = _pick(S, (256, 128))
    g1 = W['attn_norm'][l][None]
    grads = {}
    dmixed = _mm(dh2, W['w_out'][l], 'nt', f'out_dx_{l}')
    grads['w_out'] = _mm(saved['mixed'], dh2, 'tn', f'out_dw_{l}', out_dtype=bf16)
    dproj, dmp = _mixers_bwd(dmixed, saved['mix'], l)
    for k in MIXER_PARAMS:
        grads[k] = dmp[k]
    dn1 = _mm(dproj, W['w_in'][l], 'nt', f'proj_dx_{l}')
    grads['w_in'] = _mm(saved['n1'], dproj, 'tn', f'proj_dw_{l}', out_dtype=bf16)
    (dh1n,), (dg1,) = _tile_bwd(_f_rms, [saved['h']], [g1], [dn1], [True], [True], tm, f'rms1_bwd_{l}')
    grads['attn_norm'] = dg1[0]
    return (dh2, dh1n), grads


def kernel(x, attn_norm, w_in, w_out, mla_q_norm, mla_kv_norm, mla_w_uq, mla_w_ukv, mla_qk_q, mla_qk_k, s5_lambda_re, s5_lambda_im, s5_log_dt, s5_b_re, s5_b_im, s5_c_re, s5_c_im, s5_d, s5_w_glu, dil_q_norm, dil_k_norm, t5_bias, dn_conv, dn_a_log, dn_dt_bias, dn_o_norm, ffn_norm, ffn_w1, ffn_w3, ffn_w2, loss_target, m_attn_norm, m_w_in, m_w_out, m_mla_q_norm, m_mla_kv_norm, m_mla_w_uq, m_mla_w_ukv, m_mla_qk_q, m_mla_qk_k, m_s5_lambda_re, m_s5_lambda_im, m_s5_log_dt, m_s5_b_re, m_s5_b_im, m_s5_c_re, m_s5_c_im, m_s5_d, m_s5_w_glu, m_dil_q_norm, m_dil_k_norm, m_t5_bias, m_dn_conv, m_dn_a_log, m_dn_dt_bias, m_dn_o_norm, m_ffn_norm, m_ffn_w1, m_ffn_w3, m_ffn_w2, v_attn_norm, v_w_in, v_w_out, v_mla_q_norm, v_mla_kv_norm, v_mla_w_uq, v_mla_w_ukv, v_mla_qk_q, v_mla_qk_k, v_s5_lambda_re, v_s5_lambda_im, v_s5_log_dt, v_s5_b_re, v_s5_b_im, v_s5_c_re, v_s5_c_im, v_s5_d, v_s5_w_glu, v_dil_q_norm, v_dil_k_norm, v_t5_bias, v_dn_conv, v_dn_a_log, v_dn_dt_bias, v_dn_o_norm, v_ffn_norm, v_ffn_w1, v_ffn_w3, v_ffn_w2):
    given = dict(locals())
    w_loc = {n: given[n] for n in WEIGHTS}
    m_loc = {n: given['m_' + n] for n in WEIGHTS}
    v_loc = {n: given['v_' + n] for n in WEIGHTS}
    big_names = list(BIG)

    own = 2 * lax.axis_index('x') + lax.axis_index('y')
    groups = [[(n, 0) for n in GATHER_FIRST], [(n, 0) for n in GATHER_FFN], [(n, 1) for n in big_names]]
    started, order = [], jnp.zeros((8, LANES), f32)
    for gi, group in enumerate(groups):
        blocks = [w_loc[n][l].astype(bf16) for n, l in group]
        lands = [lax.empty((N_SHARDS,) + b.shape, bf16) for b in blocks]
        send_sems, recv_sems, blocks, lands, order = _to_chips_start(blocks, lands, False, order, f'gather_start_{gi}')
        started.append((send_sems, recv_sems, blocks, lands))
    W = {n: [None] * DEPTH for n in big_names}
    for n in SMALL:
        W[n] = w_loc[n]

    def arrive(gi, after):
        send_sems, recv_sems, blocks, lands = started[gi]
        blocks, lands = _to_chips_wait(send_sems, recv_sems, blocks, lands, False, after, f'gather_wait_{gi}')
        for (n, l), block, land in zip(groups[gi], blocks, lands):
            W[n][l] = _from_shards(n, lax.dynamic_update_slice(land, block[None], (own, 0, 0)))

    arrive(0, order)
    h = x[0]
    saved = []
    for l in range(DEPTH):
        h2, sv = _layer_fwd_mix(h, W, l)
        if l == 0:
            arrive(1, h2)
        h = _layer_fwd_ffn(h2, W, l, sv)
        if l == 0:
            arrive(2, h)
        saved.append(sv)
    parts_loss, dh = _loss_head(h, loss_target[0])
    local_loss = jnp.sum(parts_loss)

    layer_grads = [dict() for _ in range(DEPTH)]
    sent = []

    def send(group, tag):
        srcs = [_by_shard(n, layer_grads[l][n]).astype(bf16) for n, l in group]
        lands = [lax.empty((3,) + s.shape[1:], bf16) for s in srcs]
        send_sems, recv_sems, srcs, lands, token = _to_chips_start(srcs, lands, True, jnp.zeros((8, LANES), f32),
                                                                   f'reduce_start_{tag}')
        sent.append((group, tag, send_sems, recv_sems, srcs, lands))
        return token[0, 0]

    for l in reversed(range(DEPTH)):
        (dh3, dh2n), g_ffn = _layer_bwd_ffn(dh, saved[l], W, l)
        layer_grads[l].update(g_ffn)
        dh2 = dh3 + dh2n
        if l == 0:
            dh2 = dh2 + send([(n, 0) for n in GATHER_FFN], 'ffn0')
        (dh2, dh1n), g_mix = _layer_bwd_mix(dh2, saved[l], W, l)
        layer_grads[l].update(g_mix)
        dh = dh2 + dh1n
        if l == 1:
            dh = dh + send([(n, 1) for n in big_names], 'layer1')
    last = send([(n, 0) for n in GATHER_FIRST], 'first0')
    grad_x = dh[None]
    small_full = []
    for n in SMALL:
        if n == 't5_bias':
            small_full.append(layer_grads[0][n] + layer_grads[1][n])
        else:
            small_full.append(jnp.stack([layer_grads[l][n] for l in range(DEPTH)]))

    small_shapes = [w_loc[n].shape for n in SMALL] + [(1,)]
    nothing = [jnp.zeros((1,), f32)]
    small_pack = _pack(small_full + [local_loss.reshape(1)]) + last
    _, recv_small = _swap_with_sibling([], small_pack)
    chip_small = _small_chip_sum(small_pack, recv_small)
    _, from_chips_small = _exchange_between_chips([], chip_small)

    mine = {}
    for group, tag, send_sems, recv_sems, srcs, lands in sent:
        srcs, lands = _to_chips_wait(send_sems, recv_sems, srcs, lands, True, from_chips_small, f'reduce_wait_{tag}')
        for (n, l), src, land in zip(group, srcs, lands):
            mine[(n, l)] = _partial_sum(src, land, f'partial_{n}_{l}')
    keys = [(n, l) for n in big_names for l in range(DEPTH)]
    theirs = dict(zip(keys, _swap_partials([mine[k] for k in keys])))

    g_small_p, d_small_p, m_small_p, v_small_p = _small_update(
        small_pack, recv_small, from_chips_small, _pack([w_loc[n] for n in SMALL] + nothing),
        _pack([m_loc[n] for n in SMALL] + nothing), _pack([v_loc[n] for n in SMALL] + nothing))
    loss = _unpack(g_small_p, small_shapes)[-1][0]
    grad, delta, new_m, new_v = {}, {}, {}, {}
    for n, g_, d_, m_, v_ in zip(SMALL, _unpack(g_small_p, small_shapes), _unpack(d_small_p, small_shapes),
                                 _unpack(m_small_p, small_shapes), _unpack(v_small_p, small_shapes)):
        grad[n], delta[n], new_m[n], new_v[n] = g_, d_, m_, v_
    for n in big_names:
        grad[n], delta[n], new_m[n], new_v[n] = _adamw(
            w_loc[n], m_loc[n], v_loc[n], [mine[(n, l)] for l in range(DEPTH)], [theirs[(n, l)] for l in range(DEPTH)],
            'adamw_' + n)
    return (loss, grad_x, *[grad[n] for n in WEIGHTS], *[delta[n] for n in WEIGHTS],
            *[new_m[n] for n in WEIGHTS], *[new_v[n] for n in WEIGHTS])
```

```python
import functools
import math

import numpy as np
import jax
import jax.numpy as jnp
from jax import lax
from jax.experimental import pallas as pl
from jax.experimental.pallas import tpu as pltpu

f32 = jnp.float32
bf16 = jnp.bfloat16
HI = lax.Precision.HIGHEST
MESH = pl.DeviceIdType.MESH

VMEM_LIMIT_BYTES = 48 * 1024 * 1024
MM_VMEM_BUDGET_BYTES = 32 * 1024 * 1024
LANES = 128

D_MODEL = 1024
DEPTH = 2
GROUP_W = 256
HEAD_DIM = 64
EPS = 1e-6
NEG_INF = -1e30
N_HEADS = 4
MLA_NOPE, MLA_ROPE = 64, 32
MLA_DQK = MLA_NOPE + MLA_ROPE
ROPE_THETA = 10000.0
Q_BLOCK = 128
S5_G, S5_CG, S5_P = 16, 16, 64
DIL_PAIRS = ((128, 1), (512, 4), (2048, 16))
T5_BUCKETS, T5_MAX_DIST = 32, 2048
DN_CHUNK = 64
FFN_HIDDEN = 2816
IN_SPLITS = (256, 128, 32, 256, 768, 768, 4, 4, 256)
IN_COLS = sum(IN_SPLITS)

ADAM_LR, ADAM_B1, ADAM_B2, ADAM_EPS, ADAM_WD, ADAM_STEP = 0.001, 0.9, 0.999, 1e-08, 0.01, 10

WEIGHTS = ['attn_norm', 'w_in', 'w_out', 'mla_q_norm', 'mla_kv_norm', 'mla_w_uq', 'mla_w_ukv', 'mla_qk_q', 'mla_qk_k',
           's5_lambda_re', 's5_lambda_im', 's5_log_dt', 's5_b_re', 's5_b_im', 's5_c_re', 's5_c_im', 's5_d', 's5_w_glu',
           'dil_q_norm', 'dil_k_norm', 't5_bias', 'dn_conv', 'dn_a_log', 'dn_dt_bias', 'dn_o_norm', 'ffn_norm',
           'ffn_w1', 'ffn_w3', 'ffn_w2']
BIG = {'w_in': 2, 'w_out': 1, 'mla_w_uq': 2, 'mla_w_ukv': 2, 's5_w_glu': 2, 'dn_conv': 2, 'ffn_w1': 2, 'ffn_w3': 2,
       'ffn_w2': 1}
SMALL = [n for n in WEIGHTS if n not in BIG]
GATHER_FIRST = ['w_in', 'mla_w_uq', 'mla_w_ukv', 's5_w_glu', 'dn_conv', 'w_out']
GATHER_FFN = ['ffn_w1', 'ffn_w3', 'ffn_w2']
N_SHARDS = 4
PACK_COLS = 1024


def _cparams(sem=None, big=False):
    kw = {}
    if sem is not None:
        kw['dimension_semantics'] = sem
    if big:
        kw['vmem_limit_bytes'] = VMEM_LIMIT_BYTES
    return pltpu.CompilerParams(**kw)


def _pick(n, prefs):
    for p in prefs:
        if p <= n and n % p == 0:
            return p
    return n


def _lane_tile(n, cap):
    for t in range(cap - cap % LANES, 0, -LANES):
        if n % t == 0:
            return t
    return n


def _mm(a, b, mode, name, add=None, out_dtype=f32):
    if mode == 'nn':
        (M, K), (K2, N) = a.shape, b.shape
    elif mode == 'nt':
        (M, K), (N, K2) = a.shape, b.shape
    else:
        (K, M), (K2, N) = a.shape, b.shape
    assert K == K2, (name, a.shape, b.shape)
    tk = K if K <= 2816 else _pick(K, (2816, 2048, 1408, 1024, 512))
    cap_m, cap_n = (1408 if mode == 'tn' else 512), 1408

    def need(tm_, tn_):
        per_step = tm_ * tk * a.dtype.itemsize + tk * tn_ * b.dtype.itemsize + tm_ * tn_ * jnp.dtype(out_dtype).itemsize
        if add is not None:
            per_step += tm_ * tn_ * add.dtype.itemsize
        return 2 * per_step + tm_ * tn_ * 4

    tm, tn = _lane_tile(M, cap_m), _lane_tile(N, cap_n)
    while need(tm, tn) > MM_VMEM_BUDGET_BYTES and cap_m > LANES:
        cap_m //= 2
        tm = _lane_tile(M, cap_m)
    nk = K // tk
    dims = {'nn': (((1,), (0,)), ((), ())), 'nt': (((1,), (1,)), ((), ())), 'tn': (((0,), (0,)), ((), ()))}[mode]
    has_add = add is not None

    def body(*refs):
        a_ref, b_ref = refs[0], refs[1]
        add_ref = refs[2] if has_add else None
        o_ref = refs[3] if has_add else refs[2]
        part = lax.dot_general(a_ref[...].astype(bf16), b_ref[...].astype(bf16), dims, preferred_element_type=f32)
        if nk == 1:
            if has_add:
                part = part + add_ref[...].astype(f32)
            o_ref[...] = part.astype(out_dtype)
        else:
            acc_ref = refs[-1]
            k = pl.program_id(2)

            @pl.when(k == 0)
            def _():
                acc_ref[...] = part

            @pl.when(k > 0)
            def _():
                acc_ref[...] += part

            @pl.when(k == nk - 1)
            def _():
                r = acc_ref[...]
                if has_add:
                    r = r + add_ref[...].astype(f32)
                o_ref[...] = r.astype(out_dtype)

    if mode == 'nn':
        a_spec = pl.BlockSpec((tm, tk), lambda i, j, k: (i, k))
        b_spec = pl.BlockSpec((tk, tn), lambda i, j, k: (k, j))
    elif mode == 'nt':
        a_spec = pl.BlockSpec((tm, tk), lambda i, j, k: (i, k))
        b_spec = pl.BlockSpec((tn, tk), lambda i, j, k: (j, k))
    else:
        a_spec = pl.BlockSpec((tk, tm), lambda i, j, k: (k, i))
        b_spec = pl.BlockSpec((tk, tn), lambda i, j, k: (k, j))
    in_specs = [a_spec, b_spec]
    args = [a, b]
    if has_add:
        in_specs.append(pl.BlockSpec((tm, tn), lambda i, j, k: (i, j)))
        args.append(add)
    return pl.pallas_call(
        body, name=name, grid=(M // tm, N // tn, nk), in_specs=in_specs,
        out_specs=pl.BlockSpec((tm, tn), lambda i, j, k: (i, j)),
        out_shape=jax.ShapeDtypeStruct((M, N), out_dtype),
        scratch_shapes=[pltpu.VMEM((tm, tn), f32)] if nk > 1 else [],
        compiler_params=_cparams(('parallel', 'parallel', 'arbitrary'), big=True),
    )(*args)


def _full_spec(p):
    nd = p.ndim
    return pl.BlockSpec(p.shape, lambda i, _nd=nd: (0,) * _nd)


def _tile_fwd(f, tiled, params, outs, tm, name):
    S = tiled[0].shape[0]
    nt, npar = len(tiled), len(params)

    def body(*refs):
        vals = [r[...].astype(f32) for r in refs[:nt + npar]]
        res = f(*vals)
        for r, o in zip(res, refs[nt + npar:]):
            o[...] = r.astype(o.dtype)

    return pl.pallas_call(
        body, name=name, grid=(S // tm,),
        in_specs=[pl.BlockSpec((tm, t.shape[1]), lambda i: (i, 0)) for t in tiled] + [_full_spec(p) for p in params],
        out_specs=[pl.BlockSpec((tm, c), lambda i: (i, 0)) for c, _ in outs],
        out_shape=[jax.ShapeDtypeStruct((S, c), dt) for c, dt in outs],
        compiler_params=_cparams(('parallel',), big=True),
    )(*tiled, *params)


def _tile_bwd(f, tiled, params, cts, diff_t, diff_p, tm, name, dt_dtypes=None):
    S = tiled[0].shape[0]
    nt, npar, nc = len(tiled), len(params), len(cts)
    it = [i for i in range(nt) if diff_t[i]]
    ip = [i for i in range(npar) if diff_p[i]]
    if dt_dtypes is None:
        dt_dtypes = [f32] * len(it)

    def body(*refs):
        vals = [r[...].astype(f32) for r in refs[:nt + npar]]
        ct_vals = tuple(r[...].astype(f32) for r in refs[nt + npar:nt + npar + nc])
        out_refs = refs[nt + npar + nc:]

        def g(*dv):
            full = list(vals)
            for k, i in enumerate(it):
                full[i] = dv[k]
            for k, i in enumerate(ip):
                full[nt + i] = dv[len(it) + k]
            return tuple(f(*full))

        _, vjp = jax.vjp(g, *[vals[i] for i in it], *[vals[nt + i] for i in ip])
        grads = vjp(ct_vals)
        for k in range(len(it)):
            out_refs[k][...] = grads[k].astype(out_refs[k].dtype)
        step = pl.program_id(0)
        for k in range(len(ip)):
            o = out_refs[len(it) + k]
            gk = grads[len(it) + k]

            @pl.when(step == 0)
            def _(o=o, gk=gk):
                o[...] = gk

            @pl.when(step > 0)
            def _(o=o, gk=gk):
                o[...] += gk

    out_specs = [pl.BlockSpec((tm, tiled[i].shape[1]), lambda i_: (i_, 0)) for i in it] + [_full_spec(params[i]) for i in ip]
    out_shape = [jax.ShapeDtypeStruct(tiled[i].shape, dt_dtypes[k]) for k, i in enumerate(it)] + \
                [jax.ShapeDtypeStruct(params[i].shape, f32) for i in ip]
    res = pl.pallas_call(
        body, name=name, grid=(S // tm,),
        in_specs=[pl.BlockSpec((tm, t.shape[1]), lambda i: (i, 0)) for t in tiled] + [_full_spec(p) for p in params] +
                 [pl.BlockSpec((tm, c.shape[1]), lambda i: (i, 0)) for c in cts],
        out_specs=out_specs, out_shape=out_shape,
        compiler_params=_cparams(('arbitrary',), big=True),
    )(*tiled, *params, *cts)
    return list(res[:len(it)]), list(res[len(it):])


def _rms(x, g):
    return x * lax.rsqrt(jnp.mean(x * x, axis=-1, keepdims=True) + EPS) * g


def _f_rms(x, g):
    return (_rms(x, g),)


def _f_swiglu(u, v):
    return (u * jax.nn.sigmoid(u) * v,)


def _loss_head(y, target):
    S, D = y.shape
    tm = _pick(S, (256, 128))

    def body(y_ref, t_ref, part_ref, dy_ref):
        e = y_ref[...] - t_ref[...]
        dy_ref[...] = e * (1.0 / D)
        s = 0.5 * jnp.sum(jnp.sum(e * e, axis=1, keepdims=True), axis=0, keepdims=True) * (1.0 / D)
        r = lax.broadcasted_iota(jnp.int32, (8, LANES), 0)
        c = lax.broadcasted_iota(jnp.int32, (8, LANES), 1)
        part_ref[0] = jnp.where((r == 0) & (c == 0), s, 0.0)

    return pl.pallas_call(
        body, name='loss_head', grid=(S // tm,),
        in_specs=[pl.BlockSpec((tm, D), lambda i: (i, 0))] * 2,
        out_specs=[pl.BlockSpec((1, 8, LANES), lambda i: (i, 0, 0)), pl.BlockSpec((tm, D), lambda i: (i, 0))],
        out_shape=[jax.ShapeDtypeStruct((S // tm, 8, LANES), f32), jax.ShapeDtypeStruct((S, D), f32)],
        compiler_params=_cparams(('parallel',)),
    )(y, target)


def _pack_rows_of(shape):
    rows = -(-math.prod(shape) // PACK_COLS)
    return -(-rows // 8) * 8


def _pack(arrs):
    parts = []
    for a in arrs:
        rows = _pack_rows_of(a.shape)
        flat = a.astype(f32).reshape(-1)
        parts.append(jnp.pad(flat, (0, rows * PACK_COLS - flat.shape[0])).reshape(rows, PACK_COLS))
    return jnp.concatenate(parts, axis=0)


def _unpack(pack, shapes):
    out, row = [], 0
    for s in shapes:
        rows = _pack_rows_of(s)
        out.append(pack[row:row + rows].reshape(-1)[:math.prod(s)].reshape(s))
        row += rows
    return out


ANY = pl.BlockSpec(memory_space=pl.ANY)


def _place():
    return lax.axis_index('x'), lax.axis_index('y'), lax.axis_index('c')


def _where():
    return jnp.stack([lax.axis_index('c'), 2 * lax.axis_index('x') + lax.axis_index('y')]).astype(jnp.int32)


def _remote(src, dst, send_sems, recv_sems, k, to):
    return pltpu.make_async_remote_copy(src_ref=src, dst_ref=dst, send_sem=send_sems.at[k], recv_sem=recv_sems.at[k],
                                        device_id=to, device_id_type=MESH)


def _swap_with_sibling(gs, small):
    n = len(gs)

    def body(*refs):
        g_refs, s_ref = refs[:n], refs[n]
        r_refs, rs_ref = refs[n + 1:2 * n + 1], refs[2 * n + 1]
        send_sems, recv_sems = refs[2 * n + 2:]
        x, y, c = _place()
        sib = (x, y, 1 - c)
        cps = [_remote(g_refs[t].at[:, 1 - c], r_refs[t], send_sems, recv_sems, t, sib) for t in range(n)]
        cps.append(_remote(s_ref, rs_ref, send_sems, recv_sems, n, sib))
        for cp in cps:
            cp.start()
        for cp in cps:
            cp.wait()

    res = pl.pallas_call(
        body, name='swap_with_sibling', in_specs=[ANY] * (n + 1), out_specs=[ANY] * (n + 1),
        out_shape=[jax.ShapeDtypeStruct((N_SHARDS,) + g.shape[2:], g.dtype) for g in gs] +
                  [jax.ShapeDtypeStruct(small.shape, small.dtype)],
        scratch_shapes=[pltpu.SemaphoreType.DMA((n + 1,)), pltpu.SemaphoreType.DMA((n + 1,))],
    )(*gs, small)
    return list(res[:n]), res[n]


def _exchange_between_chips(cs, small):
    n = len(cs)

    def body(*refs):
        c_refs, s_ref = refs[:n], refs[n]
        r_refs, rs_ref = refs[n + 1:2 * n + 1], refs[2 * n + 1]
        send_sems, recv_sems = refs[2 * n + 2:]
        x, y, c = _place()
        chips = [(1 - x, y), (x, 1 - y), (1 - x, 1 - y)]
        cps = []
        for j, (px, py) in enumerate(chips):
            for t in range(n):
                cps.append(_remote(c_refs[t].at[2 * px + py], r_refs[t].at[j], send_sems, recv_sems, 3 * t + j, (px, py, c)))
            cps.append(_remote(s_ref, rs_ref.at[j], send_sems, recv_sems, 3 * n + j, (px, py, c)))
        for cp in cps:
            cp.start()
        for cp in cps:
            cp.wait()

    res = pl.pallas_call(
        body, name='exchange_between_chips', in_specs=[ANY] * (n + 1), out_specs=[ANY] * (n + 1),
        out_shape=[jax.ShapeDtypeStruct((3,) + c.shape[1:], c.dtype) for c in cs] +
                  [jax.ShapeDtypeStruct((3,) + small.shape, small.dtype)],
        scratch_shapes=[pltpu.SemaphoreType.DMA((3 * n + 3,)), pltpu.SemaphoreType.DMA((3 * n + 3,))],
    )(*cs, small)
    return list(res[:n]), res[n]


def _swap_partials(ts):
    n = len(ts)

    def body(*refs):
        t_refs, o_refs = refs[:n], refs[n:2 * n]
        send_sems, recv_sems = refs[2 * n:]
        x, y, c = _place()
        cps = [_remote(t_refs[t], o_refs[t], send_sems, recv_sems, t, (x, y, 1 - c)) for t in range(n)]
        for cp in cps:
            cp.start()
        for cp in cps:
            cp.wait()

    return pl.pallas_call(
        body, name='swap_partials', in_specs=[ANY] * n, out_specs=[ANY] * n,
        out_shape=[jax.ShapeDtypeStruct(t.shape, t.dtype) for t in ts],
        scratch_shapes=[pltpu.SemaphoreType.DMA((n,)), pltpu.SemaphoreType.DMA((n,))],
    )(*ts)


HBM = pl.BlockSpec(memory_space=pltpu.HBM)
SEM = pl.BlockSpec(memory_space=pltpu.SEMAPHORE)
DATAFLOW = pltpu.SideEffectType.DATAFLOW_SIDE_EFFECTING


def _in_hbm(t):
    return pltpu.with_memory_space_constraint(t, pltpu.HBM)


def _other_chips():
    x, y, c = _place()
    return [(1 - x, y, c), (x, 1 - y, c), (1 - x, 1 - y, c)]


def _to_chips_copies(src_refs, land_refs, send_sems, recv_sems, per_peer):
    x, y, _ = _place()
    cps = []
    for t, (src, land) in enumerate(zip(src_refs, land_refs)):
        for j, (px, py, pc) in enumerate(_other_chips()):
            s = src.at[2 * px + py] if per_peer else src
            d = land.at[j] if per_peer else land.at[2 * x + y]
            cps.append(_remote(s, d, send_sems, recv_sems, 3 * t + j, (px, py, pc)))
    return cps


def _to_chips_start(srcs, lands, per_peer, order, name):
    n = len(srcs)

    def body(*refs):
        src_refs, land_refs = refs[:n], refs[n:2 * n]
        send_sems, recv_sems = refs[2 * n + 1], refs[2 * n + 2]
        token = refs[-1]
        for cp in _to_chips_copies(src_refs, land_refs, send_sems, recv_sems, per_peer):
            cp.start()
        token[...] = jnp.zeros_like(token)

    res = pl.pallas_call(
        body, name=name, in_specs=[HBM] * (2 * n) + [ANY],
        out_specs=[SEM, SEM] + [HBM] * (2 * n) + [pl.BlockSpec(memory_space=pltpu.VMEM)],
        out_shape=[pltpu.SemaphoreType.DMA((3 * n,)), pltpu.SemaphoreType.DMA((3 * n,))] +
                  [pltpu.HBM(t.shape, t.dtype) for t in srcs] + [pltpu.HBM(t.shape, t.dtype) for t in lands] +
                  [jax.ShapeDtypeStruct((8, LANES), f32)],
        input_output_aliases={i: 2 + i for i in range(2 * n)},
        compiler_params=pltpu.CompilerParams(has_side_effects=DATAFLOW),
    )(*[_in_hbm(t) for t in srcs], *[_in_hbm(t) for t in lands], order)
    return res[0], res[1], list(res[2:2 + n]), list(res[2 + n:2 + 2 * n]), res[-1]


def _to_chips_wait(send_sems, recv_sems, srcs, lands, per_peer, after, name):
    n = len(srcs)

    def body(*refs):
        src_refs, land_refs = refs[:n], refs[n:2 * n]
        send_ref, recv_ref = refs[2 * n], refs[2 * n + 1]
        for cp in _to_chips_copies(src_refs, land_refs, send_ref, recv_ref, per_peer):
            cp.wait_send()
            cp.wait_recv()

    res = pl.pallas_call(
        body, name=name, in_specs=[HBM] * (2 * n) + [SEM, SEM, ANY],
        out_specs=[HBM] * (2 * n),
        out_shape=[pltpu.HBM(t.shape, t.dtype) for t in srcs] + [pltpu.HBM(t.shape, t.dtype) for t in lands],
        input_output_aliases={i: i for i in range(2 * n)},
        compiler_params=pltpu.CompilerParams(has_side_effects=DATAFLOW),
    )(*srcs, *lands, send_sems, recv_sems, after)
    return list(res[:n]), list(res[n:])


def _row_tile(a):
    return _pick(a, (512, 256, 128, 64, 32, 16, 8))


def _partial_sum(g, land, name):
    _, a, b = g.shape
    tr = _row_tile(a)

    def body(w_ref, g_ref, r_ref, o_ref):
        t = g_ref[0].astype(f32) + r_ref[0].astype(f32)
        t = t + r_ref[1].astype(f32)
        t = t + r_ref[2].astype(f32)
        o_ref[...] = t.astype(o_ref.dtype)

    return pl.pallas_call(
        body, name=name,
        grid_spec=pltpu.PrefetchScalarGridSpec(
            num_scalar_prefetch=1, grid=(a // tr,),
            in_specs=[pl.BlockSpec((1, tr, b), lambda i, w: (w[1], i, 0)), pl.BlockSpec((3, tr, b), lambda i, w: (0, i, 0))],
            out_specs=pl.BlockSpec((tr, b), lambda i, w: (i, 0))),
        out_shape=jax.ShapeDtypeStruct((a, b), bf16),
        compiler_params=_cparams(('parallel',)),
    )(_where(), g, land)


def _by_shard(name, t):
    r, c = t.shape
    if BIG[name] == 2:
        return t.reshape(r, N_SHARDS, c // N_SHARDS).transpose(1, 0, 2)
    return t.reshape(N_SHARDS, r // N_SHARDS, c)


def _from_shards(name, g):
    s, a, b = g.shape
    if BIG[name] == 2:
        return g.transpose(1, 0, 2).reshape(a, s * b)
    return g.reshape(s * a, b)


def _adam_math(w, g, m, v):
    m = ADAM_B1 * m + (1.0 - ADAM_B1) * g
    v = ADAM_B2 * v + (1.0 - ADAM_B2) * (g * g)
    m_hat = m / (1.0 - ADAM_B1 ** ADAM_STEP)
    v_hat = v / (1.0 - ADAM_B2 ** ADAM_STEP)
    delta = -ADAM_LR * (m_hat / (jnp.sqrt(v_hat) + ADAM_EPS) + ADAM_WD * w)
    return delta, m, v


def _small_update(own, sib, chips, w, m, v):
    def body(o_ref, s_ref, c_ref, w_ref, m_ref, v_ref, g_out, d_out, m_out, v_out):
        chip = o_ref[...] + s_ref[...]
        g = (chip + c_ref[0]) + (c_ref[1] + c_ref[2])
        d, mn, vn = _adam_math(w_ref[...], g, m_ref[...], v_ref[...])
        g_out[...] = g
        d_out[...] = d
        m_out[...] = mn
        v_out[...] = vn

    return pl.pallas_call(body, name='small_update', out_shape=[jax.ShapeDtypeStruct(own.shape, f32)] * 4)(
        own, sib, chips, w, m, v)


def _small_chip_sum(own, sib):
    def body(o_ref, s_ref, out):
        out[...] = o_ref[...] + s_ref[...]
    return pl.pallas_call(body, name='small_chip_sum', out_shape=jax.ShapeDtypeStruct(own.shape, f32))(own, sib)


def _adamw(w, m, v, mine, theirs, name):
    layers, a, b = w.shape
    tr = _row_tile(a)

    def body(w_ref, m_ref, v_ref, p0, p1, q0, q1, g_out, d_out, m_out, v_out):
        first = pl.program_id(0) == 0
        g = jnp.where(first, p0[...].astype(f32) + q0[...].astype(f32), p1[...].astype(f32) + q1[...].astype(f32))
        d, mn, vn = _adam_math(w_ref[0], g, m_ref[0], v_ref[0])
        g_out[0] = g
        d_out[0] = d
        m_out[0] = mn
        v_out[0] = vn

    full = pl.BlockSpec((1, tr, b), lambda l, i: (l, i, 0))
    part = pl.BlockSpec((tr, b), lambda l, i: (i, 0))
    return pl.pallas_call(body, name=name, grid=(layers, a // tr), in_specs=[full] * 3 + [part] * 4, out_specs=[full] * 4,
                          out_shape=[jax.ShapeDtypeStruct(w.shape, f32)] * 4,
                          compiler_params=_cparams(('parallel', 'parallel')))(w, m, v, *mine, *theirs)


def _dg(a, b, ca, cb):
    return lax.dot_general(a.astype(bf16), b.astype(bf16), (((ca,), (cb,)), ((), ())), preferred_element_type=f32)


@jax.custom_vjp
def _bmm(a, b):
    return _dg(a, b, 1, 0)


_bmm.defvjp(lambda a, b: (_dg(a, b, 1, 0), (a, b)), lambda r, g: (_dg(g, r[1], 1, 1), _dg(r[0], g, 0, 0)))


@jax.custom_vjp
def _bmm_nt(a, b):
    return _dg(a, b, 1, 1)


_bmm_nt.defvjp(lambda a, b: (_dg(a, b, 1, 1), (a, b)), lambda r, g: (_dg(g, r[1], 1, 0), _dg(g, r[0], 0, 0)))


@jax.custom_vjp
def _bmm_tn(a, b):
    return _dg(a, b, 0, 0)


_bmm_tn.defvjp(lambda a, b: (_dg(a, b, 0, 0), (a, b)), lambda r, g: (_dg(r[1], g, 1, 1), _dg(r[0], g, 1, 0)))


def _hdot(a, b):
    return jnp.dot(a, b, precision=HI, preferred_element_type=f32)


def _hdot_nt(a, b):
    return lax.dot_general(a, b, (((1,), (1,)), ((), ())), precision=HI, preferred_element_type=f32)


def _hdot_tn(a, b):
    return lax.dot_general(a, b, (((0,), (0,)), ((), ())), precision=HI, preferred_element_type=f32)


def _head_mask(h, width=GROUP_W):
    lane = lax.broadcasted_iota(jnp.int32, (1, width), 1)
    return ((lane >= h * HEAD_DIM) & (lane < (h + 1) * HEAD_DIM)).astype(f32)


def _rope_perm():
    p = np.zeros((LANES, LANES), np.float32)
    half = MLA_ROPE // 2
    for i in range(half):
        p[MLA_NOPE + half + i, MLA_NOPE + i] = -1.0
        p[MLA_NOPE + i, MLA_NOPE + half + i] = 1.0
    return jnp.asarray(p)


def _rope_tables(S):
    half = MLA_ROPE // 2
    freqs = ROPE_THETA ** (-jnp.arange(half, dtype=f32) / half)
    ang = jnp.arange(S, dtype=f32)[:, None] * freqs[None, :]
    cos, sin = jnp.cos(ang), jnp.sin(ang)
    ones, zeros = jnp.ones((S, MLA_NOPE), f32), jnp.zeros((S, LANES - MLA_DQK), f32)
    c_tab = jnp.concatenate([ones, cos, cos, zeros], axis=1)
    s_tab = jnp.concatenate([jnp.zeros((S, MLA_NOPE), f32), sin, sin, zeros], axis=1)
    return c_tab, s_tab


def _f_mla_pre(c_q, c_kv, krope, c_tab, s_tab, q_norm, kv_norm, wq0, wq1, wq2, wq3, wk0, wk1, wk2, wk3, wv, gq, gk, perm):
    wq, wk = (wq0, wq1, wq2, wq3), (wk0, wk1, wk2, wk3)
    nq = _rms(c_q, q_norm)
    nkv = _rms(c_kv, kv_norm)

    def norm_rope(t, g):
        t = t * lax.rsqrt(jnp.sum(t * t, axis=-1, keepdims=True) * (1.0 / MLA_DQK) + EPS) * g
        return t * c_tab + _hdot(t, perm) * s_tab

    qs = [norm_rope(_bmm(nq, wq[h]), gq) * (MLA_DQK ** -0.5) for h in range(N_HEADS)]
    ks = [norm_rope(_bmm(nkv, wk[h]) + krope, gk) for h in range(N_HEADS)]
    return (*qs, *ks, _bmm(nkv, wv))


def _f_attn(qs, ks, v, q0):
    tq, S = qs[0].shape[0], ks[0].shape[0]
    qpos = q0 + lax.broadcasted_iota(jnp.int32, (tq, S), 0)
    kpos = lax.broadcasted_iota(jnp.int32, (tq, S), 1)
    keep = kpos <= qpos
    logits = [jnp.where(keep, _bmm_nt(qs[h], ks[h]), NEG_INF) for h in range(N_HEADS)]
    ps = [jnp.exp(lg - jnp.max(lg, axis=-1, keepdims=True)) for lg in logits]
    ps = [p / jnp.sum(p, axis=-1, keepdims=True) for p in ps]
    return sum(_bmm(p, v) * _head_mask(h) for h, p in enumerate(ps))


ATTN_PARTS = 4


def _mla_attn_fwd(qs, ks, v, name):
    S = v.shape[0]
    tq = Q_BLOCK
    parts = ATTN_PARTS if S % (ATTN_PARTS * tq) == 0 else 1
    per = S // parts
    outs = []
    for p in range(parts):
        n_keys = (p + 1) * per
        first_block = p * (per // tq)

        def body(*refs, first_block=first_block):
            q_vals = [r[...] for r in refs[:4]]
            k_vals = [r[...] for r in refs[4:8]]
            refs[9][...] = _f_attn(q_vals, k_vals, refs[8][...], (first_block + pl.program_id(0)) * tq)

        qspec = pl.BlockSpec((tq, LANES), lambda i, fb=first_block: (fb + i, 0))
        outs.append(pl.pallas_call(
            body, name=f'{name}_{p}', grid=(per // tq,),
            in_specs=[qspec] * 4 + [pl.BlockSpec((n_keys, LANES), lambda i: (0, 0))] * 4 +
                     [pl.BlockSpec((n_keys, GROUP_W), lambda i: (0, 0))],
            out_specs=pl.BlockSpec((tq, GROUP_W), lambda i: (i, 0)),
            out_shape=jax.ShapeDtypeStruct((per, GROUP_W), f32),
            compiler_params=_cparams(('parallel',), big=True),
        )(*qs, *ks, v))
    return jnp.concatenate(outs, axis=0)


def _mla_attn_bwd(qs, ks, v, do, name):
    S = v.shape[0]
    tq = Q_BLOCK
    parts = ATTN_PARTS if S % (ATTN_PARTS * tq) == 0 else 1
    per = S // parts
    dq_parts, dkv_sum = [], None
    for p in range(parts):
        n_keys = (p + 1) * per
        first_block = p * (per // tq)

        def body(*refs, first_block=first_block):
            q_vals = [r[...].astype(f32) for r in refs[:4]]
            k_vals = [r[...].astype(f32) for r in refs[4:8]]
            v_val = refs[8][...].astype(f32)
            q0 = (first_block + pl.program_id(0)) * tq
            _, vjp = jax.vjp(lambda a, b, c: _f_attn(a, b, c, q0), q_vals, k_vals, v_val)
            dqs, dks, dv = vjp(refs[9][...])
            outs = refs[10:]
            for h in range(N_HEADS):
                outs[h][...] = dqs[h]
            first = pl.program_id(0) == 0
            for o, g in zip(outs[4:], (*dks, dv)):
                @pl.when(first)
                def _(o=o, g=g):
                    o[...] = g

                @pl.when(jnp.logical_not(first))
                def _(o=o, g=g):
                    o[...] += g

        qspec = pl.BlockSpec((tq, LANES), lambda i, fb=first_block: (fb + i, 0))
        kspec = pl.BlockSpec((n_keys, LANES), lambda i: (0, 0))
        vspec = pl.BlockSpec((n_keys, GROUP_W), lambda i: (0, 0))
        res = pl.pallas_call(
            body, name=f'{name}_{p}', grid=(per // tq,),
            in_specs=[qspec] * 4 + [kspec] * 4 + [vspec, pl.BlockSpec((tq, GROUP_W), lambda i, fb=first_block: (fb + i, 0))],
            out_specs=[pl.BlockSpec((tq, LANES), lambda i: (i, 0))] * 4 + [kspec] * 4 + [vspec],
            out_shape=[jax.ShapeDtypeStruct((per, LANES), f32)] * 4 + [jax.ShapeDtypeStruct((n_keys, LANES), f32)] * 4 +
                      [jax.ShapeDtypeStruct((n_keys, GROUP_W), f32)],
            compiler_params=_cparams(('arbitrary',), big=True),
        )(*qs, *ks, v, do)
        dq_parts.append(res[:4])
        dkv = [jnp.pad(t, ((0, S - n_keys), (0, 0))) for t in res[4:]]
        dkv_sum = dkv if dkv_sum is None else [a_ + b_ for a_, b_ in zip(dkv_sum, dkv)]
    dqs = [jnp.concatenate([dq_parts[p][h] for p in range(parts)], axis=0) for h in range(N_HEADS)]
    return dqs, dkv_sum[:4], dkv_sum[4]


def _mla_params(mp):
    pad = LANES - MLA_DQK
    wq = jnp.pad(mp['mla_w_uq'].reshape(GROUP_W, N_HEADS, MLA_DQK).transpose(1, 0, 2), ((0, 0), (0, 0), (0, pad)))
    wkv = mp['mla_w_ukv'].reshape(LANES, N_HEADS, MLA_NOPE + HEAD_DIM)
    wk = jnp.pad(wkv[:, :, :MLA_NOPE].transpose(1, 0, 2), ((0, 0), (0, 0), (0, LANES - MLA_NOPE)))
    wv = wkv[:, :, MLA_NOPE:].reshape(LANES, GROUP_W)
    gq = jnp.pad(mp['mla_qk_q'], (0, pad))[None]
    gk = jnp.pad(mp['mla_qk_k'], (0, pad))[None]
    return [mp['mla_q_norm'][None], mp['mla_kv_norm'][None], *[wq[h] for h in range(N_HEADS)],
            *[wk[h] for h in range(N_HEADS)], wv, gq, gk, _rope_perm()]


def _mla_fwd(c_q, c_kv, k_rope, mp, l):
    S = c_q.shape[0]
    tm = _pick(S, (256, 128))
    krope = jnp.pad(k_rope, ((0, 0), (MLA_NOPE, LANES - MLA_DQK)))
    c_tab, s_tab = _rope_tables(S)
    tiled = [c_q, c_kv, krope, c_tab, s_tab]
    params = _mla_params(mp)
    res = _tile_fwd(_f_mla_pre, tiled, params, [(LANES, bf16)] * 8 + [(GROUP_W, bf16)], tm, f'mla_pre_fwd_{l}')
    qs, ks, v = res[:4], res[4:8], res[8]
    y = _mla_attn_fwd(qs, ks, v, f'mla_attn_fwd_{l}')
    return y, (tiled, params, qs, ks, v)


def _mla_bwd(dy, saved, l):
    tiled, params, qs, ks, v = saved
    S = dy.shape[0]
    tm = _pick(S, (256, 128))
    dqs, dks, dv = _mla_attn_bwd(qs, ks, v, dy, f'mla_attn_bwd_{l}')
    (dc_q, dc_kv, dkrope), dpar = _tile_bwd(_f_mla_pre, tiled, params, [*dqs, *dks, dv], [True, True, True, False, False],
                                            [True] * 13 + [False], tm, f'mla_pre_bwd_{l}')
    dqn, dkvn = dpar[0], dpar[1]
    dwq, dwk = jnp.stack(dpar[2:6]), jnp.stack(dpar[6:10])
    dwv, dgq, dgk = dpar[10:13]
    dw_uq = dwq[:, :, :MLA_DQK].transpose(1, 0, 2).reshape(GROUP_W, N_HEADS * MLA_DQK)
    dw_ukv = jnp.concatenate([dwk[:, :, :MLA_NOPE].transpose(1, 0, 2), dwv.reshape(LANES, N_HEADS, HEAD_DIM)],
                             axis=2).reshape(LANES, N_HEADS * (MLA_NOPE + HEAD_DIM))
    grads = {'mla_q_norm': dqn[0], 'mla_kv_norm': dkvn[0], 'mla_w_uq': dw_uq, 'mla_w_ukv': dw_ukv,
             'mla_qk_q': dgq[0, :MLA_DQK], 'mla_qk_k': dgk[0, :MLA_DQK]}
    return dc_q, dc_kv, dkrope[:, MLA_NOPE:MLA_DQK], grads


SPAN = 128


def _head_mean_matrix():
    h = np.arange(GROUP_W) // HEAD_DIM
    return jnp.asarray((h[:, None] == h[None, :]).astype(np.float32) / HEAD_DIM)


def _f_dil_pre(q, k, gq, gk, hm):
    qn = q * lax.rsqrt(_hdot(q * q, hm) + EPS) * gq * (HEAD_DIM ** -0.5)
    kn = k * lax.rsqrt(_hdot(k * k, hm) + EPS) * gk
    return qn, kn


def _f_dil_branch(qb, kp, kc, vp, vc, b0, b1, b2, b3, first):
    kcat = jnp.concatenate([kp, kc], axis=0)
    vcat = jnp.concatenate([vp, vc], axis=0)
    qi = lax.broadcasted_iota(jnp.int32, (SPAN, 2 * SPAN), 0) + SPAN
    kj = lax.broadcasted_iota(jnp.int32, (SPAN, 2 * SPAN), 1)
    delta = qi - kj
    valid = (delta >= 0) & (delta <= SPAN) & jnp.logical_not(first & (kj < SPAN))
    masks = [_head_mask(h) for h in range(N_HEADS)]
    raw = [_bmm_nt(qb * hm, kcat) for hm in masks]
    logits = [jnp.where(valid, r + bias, NEG_INF) for r, bias in zip(raw, (b0, b1, b2, b3))]
    ms = [jnp.max(lg, axis=-1, keepdims=True) for lg in logits]
    ps = [jnp.exp(lg - m) for lg, m in zip(logits, ms)]
    pvs = [_bmm(p, vcat) for p in ps]
    o = sum(pv * hm for pv, hm in zip(pvs, masks))
    m_full = sum(m * hm for m, hm in zip(ms, masks))
    l_full = sum(jnp.sum(p, axis=-1, keepdims=True) * hm for p, hm in zip(ps, masks))
    return o, m_full, l_full


def _dil_branch_specs(d, nb):
    cur = pl.BlockSpec((SPAN, GROUP_W), lambda r, n: (n, r))
    prev = pl.BlockSpec((SPAN, GROUP_W), lambda r, n: (jnp.maximum(n - 1, 0), r))
    bias = pl.BlockSpec((1, SPAN, 2 * SPAN), lambda r, n: (0, 0, 0))
    return cur, prev, bias


def _head_table_specs():
    return [pl.BlockSpec((1, SPAN, 2 * SPAN), lambda r, n, h=h: (h, 0, 0)) for h in range(N_HEADS)]


def _dil_branch_fwd(q, k, v, table, name):
    L, d = q.shape[0], q.shape[1] // GROUP_W
    nb = L // SPAN
    cur, prev, bias = _dil_branch_specs(d, nb)

    def body(q_ref, kp_ref, kc_ref, vp_ref, vc_ref, b0, b1, b2, b3, o_ref, m_ref, l_ref):
        o, m, l = _f_dil_branch(*[r[...].astype(f32) for r in (q_ref, kp_ref, kc_ref, vp_ref, vc_ref)], b0[0], b1[0], b2[0], b3[0],
                                pl.program_id(1) == 0)
        o_ref[...] = o
        m_ref[...] = m
        l_ref[...] = l

    return pl.pallas_call(
        body, name=name, grid=(d, nb), in_specs=[cur, prev, cur, prev, cur] + _head_table_specs(),
        out_specs=[cur] * 3, out_shape=[jax.ShapeDtypeStruct(q.shape, f32)] * 3,
        compiler_params=_cparams(('parallel', 'parallel')),
    )(q, k, k, v, v, *[table] * N_HEADS)


def _dil_branch_bwd(q, k, v, table, do, dm, dl, name):
    L, d = q.shape[0], q.shape[1] // GROUP_W
    nb = L // SPAN
    cur, prev, bias = _dil_branch_specs(d, nb)
    whole = pl.BlockSpec((L, GROUP_W), lambda r, n: (0, r))

    def body(q_ref, kp_ref, kc_ref, vp_ref, vc_ref, b0, b1, b2, b3, do_ref, dm_ref, dl_ref,
             dq_ref, dk_ref, dv_ref, db0, db1, db2, db3):
        r, n = pl.program_id(0), pl.program_id(1)
        first = n == 0
        _, vjp = jax.vjp(lambda *a: _f_dil_branch(*a, first), *[r[...].astype(f32) for r in (q_ref, kp_ref, kc_ref, vp_ref, vc_ref)],
                         b0[0], b1[0], b2[0], b3[0])
        dq, dkp, dkc, dvp, dvc, g0, g1, g2, g3 = vjp((do_ref[...], dm_ref[...], dl_ref[...]))
        dq_ref[...] = dq

        @pl.when(first)
        def _():
            dk_ref[...] = jnp.zeros_like(dk_ref)
            dv_ref[...] = jnp.zeros_like(dv_ref)

        rows = pl.ds(pl.multiple_of(n * SPAN, SPAN), SPAN)
        dk_ref[rows, :] += dkc
        dv_ref[rows, :] += dvc

        @pl.when(n > 0)
        def _():
            before = pl.ds(pl.multiple_of((n - 1) * SPAN, SPAN), SPAN)
            dk_ref[before, :] += dkp
            dv_ref[before, :] += dvp

        start = first & (r == 0)
        for o, g in zip((db0, db1, db2, db3), (g0, g1, g2, g3)):
            @pl.when(start)
            def _(o=o, g=g):
                o[0] = g

            @pl.when(jnp.logical_not(start))
            def _(o=o, g=g):
                o[0] += g

    res = pl.pallas_call(
        body, name=name, grid=(d, nb), in_specs=[cur, prev, cur, prev, cur] + _head_table_specs() + [cur] * 3,
        out_specs=[cur, whole, whole] + [bias] * 4,
        out_shape=[jax.ShapeDtypeStruct(q.shape, f32)] * 3 + [jax.ShapeDtypeStruct((1, SPAN, 2 * SPAN), f32)] * 4,
        compiler_params=_cparams(('arbitrary', 'arbitrary')),
    )(q, k, k, v, v, *[table] * N_HEADS, do, dm, dl)
    return res[0], res[1], res[2], res[3:]


def _f_dil_merge(o1, m1, l1, o2, m2, l2, o3, m3, l3):
    mx = jnp.maximum(jnp.maximum(m1, m2), m3)
    w1, w2, w3 = jnp.exp(m1 - mx), jnp.exp(m2 - mx), jnp.exp(m3 - mx)
    return ((w1 * o1 + w2 * o2 + w3 * o3) / (w1 * l1 + w2 * l2 + w3 * l3),)


def _bias_onehot(dilation):
    qi = jnp.arange(SPAN, dtype=jnp.int32)[:, None] + SPAN
    kj = jnp.arange(2 * SPAN, dtype=jnp.int32)[None, :]
    bucket = _t5_bucket(jnp.clip(qi - kj, 0, SPAN) * dilation).reshape(-1)
    return (bucket[None, :] == jnp.arange(T5_BUCKETS, dtype=jnp.int32)[:, None]).astype(f32)


def _bias_tables(t5_t, onehot, name):
    N = onehot.shape[1]
    tn = _pick(N, (4096, 2048, 1024))

    def body(t_ref, oh_ref, o_ref):
        o_ref[...] = _hdot(t_ref[...], oh_ref[...])

    return pl.pallas_call(
        body, name=name, grid=(N // tn,),
        in_specs=[pl.BlockSpec((8, T5_BUCKETS), lambda i: (0, 0)), pl.BlockSpec((T5_BUCKETS, tn), lambda i: (0, i))],
        out_specs=pl.BlockSpec((8, tn), lambda i: (0, i)), out_shape=jax.ShapeDtypeStruct((8, N), f32),
        compiler_params=_cparams(('parallel',)),
    )(t5_t, onehot)


def _bias_tables_bwd(d_tab, onehot, name):
    N = onehot.shape[1]
    tn = _pick(N, (4096, 2048, 1024))

    def body(g_ref, oh_ref, o_ref):
        part = _hdot_nt(g_ref[...], oh_ref[...])

        @pl.when(pl.program_id(0) == 0)
        def _():
            o_ref[...] = part

        @pl.when(pl.program_id(0) > 0)
        def _():
            o_ref[...] += part

    return pl.pallas_call(
        body, name=name, grid=(N // tn,),
        in_specs=[pl.BlockSpec((8, tn), lambda i: (0, i)), pl.BlockSpec((T5_BUCKETS, tn), lambda i: (0, i))],
        out_specs=pl.BlockSpec((8, T5_BUCKETS), lambda i: (0, 0)), out_shape=jax.ShapeDtypeStruct((8, T5_BUCKETS), f32),
        compiler_params=_cparams(('arbitrary',)),
    )(d_tab, onehot)


def _by_residue(t, d):
    S, C = t.shape
    return t.reshape(S // d, d * C)


def _from_residue(t):
    return t.reshape(-1, GROUP_W)


def _dil_fwd(qkv, mp, l):
    S = qkv.shape[0]
    tm = _pick(S, (256, 128))
    q, k, v = qkv[:, :GROUP_W], qkv[:, GROUP_W:2 * GROUP_W], qkv[:, 2 * GROUP_W:]
    pre_params = [jnp.tile(mp['dil_q_norm'], N_HEADS)[None], jnp.tile(mp['dil_k_norm'], N_HEADS)[None], _head_mean_matrix()]
    qn, kn = _tile_fwd(_f_dil_pre, [q, k], pre_params, [(GROUP_W, bf16)] * 2, tm, f'dil_pre_fwd_{l}')
    v = v.astype(bf16)
    t5_t = jnp.pad(mp['t5_bias'].T, ((0, 8 - N_HEADS), (0, 0)))
    branches, outs = [], []
    for bi, (_, d) in enumerate(DIL_PAIRS):
        onehot = _bias_onehot(d)
        tab = _bias_tables(t5_t, onehot, f'dil_bias_fwd_{l}_{bi}').reshape(8, SPAN, 2 * SPAN)
        qd, kd, vd = _by_residue(qn, d), _by_residue(kn, d), _by_residue(v, d)
        o, m, lsum = _dil_branch_fwd(qd, kd, vd, tab, f'dil_branch_fwd_{l}_{bi}')
        branches.append((qd, kd, vd, tab, onehot))
        outs += [_from_residue(o), _from_residue(m), _from_residue(lsum)]
    (y,) = _tile_fwd(_f_dil_merge, outs, [], [(GROUP_W, f32)], tm, f'dil_merge_fwd_{l}')
    return y, (q, k, pre_params, branches, outs)


def _dil_bwd(dy, saved, l):
    q, k, pre_params, branches, outs = saved
    S = dy.shape[0]
    tm = _pick(S, (256, 128))
    douts, _ = _tile_bwd(_f_dil_merge, outs, [], [dy], [True] * 9, [], tm, f'dil_merge_bwd_{l}')
    dqn = dkn = dv = None
    dt5_t = None
    for bi, (_, d) in enumerate(DIL_PAIRS):
        qd, kd, vd, tab, onehot = branches[bi]
        do, dm, dl = [_by_residue(t, d) for t in douts[3 * bi:3 * bi + 3]]
        dq_b, dk_b, dv_b, dbias = _dil_branch_bwd(qd, kd, vd, tab, do, dm, dl, f'dil_branch_bwd_{l}_{bi}')
        d_tab = jnp.concatenate([*dbias, jnp.zeros((8 - N_HEADS, SPAN, 2 * SPAN), f32)], axis=0).reshape(8, -1)
        g_t5 = _bias_tables_bwd(d_tab, onehot, f'dil_bias_bwd_{l}_{bi}')
        dq_b, dk_b, dv_b = _from_residue(dq_b), _from_residue(dk_b), _from_residue(dv_b)
        dqn = dq_b if dqn is None else dqn + dq_b
        dkn = dk_b if dkn is None else dkn + dk_b
        dv = dv_b if dv is None else dv + dv_b
        dt5_t = g_t5 if dt5_t is None else dt5_t + g_t5
    (dq, dk), (dgq, dgk) = _tile_bwd(_f_dil_pre, [q, k], pre_params, [dqn, dkn], [True, True], [True, True, False], tm,
                                     f'dil_pre_bwd_{l}')
    grads = {'dil_q_norm': dgq.reshape(N_HEADS, HEAD_DIM).sum(0), 'dil_k_norm': dgk.reshape(N_HEADS, HEAD_DIM).sum(0),
             't5_bias': dt5_t[:N_HEADS].T}
    return jnp.concatenate([dq, dk, dv], axis=1), grads


S5_LANES = S5_G * S5_P
SCAN_SEGMENTS = 8
SCAN_W = 256


def _f_s5_prep(bre, bim, lr, li, logdt_col, expand):
    dt = jnp.sum(jnp.exp(logdt_col) * expand, axis=0, keepdims=True)
    mag = jnp.exp(lr * dt)
    ar, ai = mag * jnp.cos(li * dt), mag * jnp.sin(li * dt)
    den = lr * lr + li * li
    nr, ni = ar - 1.0, ai
    zr = (nr * lr + ni * li) / den
    zi = (ni * lr - nr * li) / den
    bb = jnp.concatenate([zr * bre - zi * bim, zr * bim + zi * bre], axis=1)
    a_rows = jnp.broadcast_to(jnp.concatenate([ar, ai], axis=1), bb.shape)
    return bb, a_rows


def _s5_scan(x, a_rows, name, reverse=False, h=None):
    S = x.shape[0]
    NL = x.shape[1] // 2
    T = S // SCAN_SEGMENTS
    nblk = NL // SCAN_W
    n_in = 4 if reverse else 2

    def body(*refs):
        if reverse:
            (x_hbm, pr_hbm, pi_hbm, ar_ref, ai_ref, hr_hbm, hi_hbm, dar_ref, dai_ref,
             xr_s, xi_s, pr_s, pi_s, hr_s, hi_s, in_sems, out_sems) = refs
        else:
            x_hbm, ar_ref, ai_ref, hr_hbm, hi_hbm, xr_s, xi_s, hr_s, hi_s, in_sems, out_sems = refs
        col = pl.multiple_of(pl.program_id(0) * SCAN_W, SCAN_W)
        loads = []
        for k in range(SCAN_SEGMENTS):
            rows = pl.ds(k * T, T)
            sources = [(x_hbm, col, xr_s), (x_hbm, NL + col, xi_s)]
            if reverse:
                sources += [(pr_hbm, col, pr_s), (pi_hbm, col, pi_s)]
            for i, (src, c0, dst) in enumerate(sources):
                loads.append(pltpu.make_async_copy(src.at[rows, pl.ds(c0, SCAN_W)], dst.at[:, k, :],
                                                   in_sems.at[i * SCAN_SEGMENTS + k]))
        for cp in loads:
            cp.start()
        for cp in loads:
            cp.wait()
        ar = ar_ref[...]
        ai = -ai_ref[...] if reverse else ai_ref[...]
        zero = jnp.zeros((SCAN_SEGMENTS, SCAN_W), f32)

        def at(s):
            return T - 1 - s if reverse else s

        def local(s, c):
            hr, hi, pr, pi = c
            j = at(s)
            nhr = ar * hr - ai * hi + xr_s[j]
            nhi = ar * hi + ai * hr + xi_s[j]
            hr_s[j] = nhr
            hi_s[j] = nhi
            return nhr, nhi, ar * pr - ai * pi, ar * pi + ai * pr

        er, ei, pr, pi = lax.fori_loop(0, T, local, (zero, zero, zero + 1.0, zero), unroll=2)
        row = lax.broadcasted_iota(jnp.int32, (SCAN_SEGMENTS, SCAN_W), 0)
        cr, ci = zero, zero
        order = range(SCAN_SEGMENTS - 2, -1, -1) if reverse else range(1, SCAN_SEGMENTS)
        for k in order:
            src = k + 1 if reverse else k - 1
            tr = er + pr * cr - pi * ci
            ti = ei + pr * ci + pi * cr
            cr = jnp.where(row == k, jnp.sum(jnp.where(row == src, tr, 0.0), axis=0, keepdims=True), cr)
            ci = jnp.where(row == k, jnp.sum(jnp.where(row == src, ti, 0.0), axis=0, keepdims=True), ci)

        def fix_at(j, c, before):
            pr, pi, sr, si = c
            pr, pi = ar * pr - ai * pi, ar * pi + ai * pr
            hr = hr_s[j] + pr * cr - pi * ci
            hi = hi_s[j] + pr * ci + pi * cr
            hr_s[j] = hr
            hi_s[j] = hi
            if reverse:
                qr, qi = before
                sr = sr + hr * qr + hi * qi
                si = si + hi * qr - hr * qi
            return pr, pi, sr, si

        start = (zero + 1.0, zero, zero, zero)
        if reverse:
            def fix(s, c):
                j = T - 1 - s
                return fix_at(j, c, (pr_s[j - 1], pi_s[j - 1]))

            c = lax.fori_loop(0, T - 1, fix, start, unroll=2)
            last_r = jnp.where(row == 0, 0.0, pltpu.roll(pr_s[T - 1], 1, 0))
            last_i = jnp.where(row == 0, 0.0, pltpu.roll(pi_s[T - 1], 1, 0))
            _, _, sr, si = fix_at(0, c, (last_r, last_i))
            dar_ref[...] = sr
            dai_ref[...] = si
        else:
            lax.fori_loop(0, T, lambda s, c: fix_at(s, c, None), start, unroll=2)
        stores = []
        for k in range(SCAN_SEGMENTS):
            rows = pl.ds(k * T, T)
            stores.append(pltpu.make_async_copy(hr_s.at[:, k, :], hr_hbm.at[rows, pl.ds(col, SCAN_W)], out_sems.at[k]))
            stores.append(pltpu.make_async_copy(hi_s.at[:, k, :], hi_hbm.at[rows, pl.ds(col, SCAN_W)],
                                                out_sems.at[SCAN_SEGMENTS + k]))
        for cp in stores:
            cp.start()
        for cp in stores:
            cp.wait()

    a_re = pl.BlockSpec((SCAN_SEGMENTS, SCAN_W), lambda b: (0, b))
    a_im = pl.BlockSpec((SCAN_SEGMENTS, SCAN_W), lambda b: (0, nblk + b))
    seq = pltpu.VMEM((T, SCAN_SEGMENTS, SCAN_W), f32)
    if reverse:
        in_specs, args = [ANY, ANY, ANY, a_re, a_im], [x, h[0], h[1], a_rows, a_rows]
        out_specs = [ANY, ANY, a_re, a_re]
        out_shape = [jax.ShapeDtypeStruct((S, NL), f32)] * 2 + [jax.ShapeDtypeStruct((SCAN_SEGMENTS, NL), f32)] * 2
    else:
        in_specs, args = [ANY, a_re, a_im], [x, a_rows, a_rows]
        out_specs = [ANY, ANY]
        out_shape = [jax.ShapeDtypeStruct((S, NL), f32)] * 2
    scratch = [seq] * (n_in + 2) + [pltpu.SemaphoreType.DMA((n_in * SCAN_SEGMENTS,)),
                                    pltpu.SemaphoreType.DMA((2 * SCAN_SEGMENTS,))]
    return pl.pallas_call(body, name=name, grid=(nblk,), in_specs=in_specs, out_specs=out_specs, out_shape=out_shape,
                          scratch_shapes=scratch, compiler_params=_cparams(('arbitrary',), big=True))(*args)


def _f_s5_post(y, u, d, w_glu):
    z = _bmm(y + d * u, w_glu)
    return (z[:, :GROUP_W] * jax.nn.sigmoid(z[:, GROUP_W:]),)


def _block_diag(t):
    G, a, b = t.shape
    eye = jnp.eye(G, dtype=t.dtype)
    return (t[:, :, None, :] * eye[:, None, :, None]).reshape(G * a, G * b)


def _diag_blocks(m, a, b):
    G = m.shape[0] // a
    return jnp.moveaxis(jnp.diagonal(m.reshape(G, a, G, b), axis1=0, axis2=2), -1, 0)


def _s5_fwd(u, mp, l):
    S = u.shape[0]
    tm = _pick(S, (256, 128))
    bre = _block_diag(mp['s5_b_re'].transpose(0, 2, 1))
    bim = _block_diag(mp['s5_b_im'].transpose(0, 2, 1))
    expand = jnp.repeat(jnp.eye(S5_G, dtype=f32), S5_P, axis=1)
    prep_params = [mp['s5_lambda_re'].reshape(1, S5_LANES), mp['s5_lambda_im'].reshape(1, S5_LANES),
                   mp['s5_log_dt'].reshape(S5_G, 1), expand]
    bb, a_rows = _tile_fwd(_f_s5_prep, [bre, bim], prep_params, [(2 * S5_LANES, f32)] * 2, GROUP_W, f's5_prep_fwd_{l}')
    x = _mm(u, bb, 'nn', f's5_in_fwd_{l}')
    hr, hi = _s5_scan(x, a_rows, f's5_scan_fwd_{l}')
    c_re, c_im = _block_diag(mp['s5_c_re'].transpose(0, 2, 1)), -_block_diag(mp['s5_c_im'].transpose(0, 2, 1))
    y = _mm(hi, c_im, 'nn', f's5_out_im_fwd_{l}', add=_mm(hr, c_re, 'nn', f's5_out_re_fwd_{l}'))
    post_params = [mp['s5_d'][None], mp['s5_w_glu']]
    (out,) = _tile_fwd(_f_s5_post, [y, u], post_params, [(GROUP_W, f32)], tm, f's5_post_fwd_{l}')
    return out, (u, bre, bim, prep_params, bb, a_rows, hr, hi, c_re, c_im, y, post_params)


def _s5_bwd(dout, saved, l):
    u, bre, bim, prep_params, bb, a_rows, hr, hi, c_re, c_im, y, post_params = saved
    S = u.shape[0]
    tm = _pick(S, (256, 128))
    (dy, du1), (dd, dwglu) = _tile_bwd(_f_s5_post, [y, u], post_params, [dout], [True, True], [True, True], tm,
                                       f's5_post_bwd_{l}')
    ccat = jnp.concatenate([c_re, c_im], axis=0)
    dh = _mm(dy, ccat, 'nt', f's5_out_dx_{l}')
    dccat = jnp.concatenate([_mm(hr, dy, 'tn', f's5_out_re_dw_{l}'), _mm(hi, dy, 'tn', f's5_out_im_dw_{l}')], axis=0)
    lr_, li_, dar, dai = _s5_scan(dh, a_rows, f's5_scan_bwd_{l}', reverse=True, h=(hr, hi))
    du2 = _mm(li_, bb[:, S5_LANES:], 'nt', f's5_in_im_dx_{l}', add=_mm(lr_, bb[:, :S5_LANES], 'nt', f's5_in_re_dx_{l}'))
    dbb = jnp.concatenate([_mm(u, lr_, 'tn', f's5_in_re_dw_{l}'), _mm(u, li_, 'tn', f's5_in_im_dw_{l}')], axis=1)
    da_rows = jnp.pad(jnp.concatenate([dar, dai], axis=1), ((0, GROUP_W - SCAN_SEGMENTS), (0, 0)))
    (dbre, dbim), (dlr, dli, dlogdt) = _tile_bwd(_f_s5_prep, [bre, bim], prep_params, [dbb, da_rows], [True, True],
                                                 [True, True, True, False], GROUP_W, f's5_prep_bwd_{l}')
    grads = {
        's5_lambda_re': dlr.reshape(S5_G, S5_P), 's5_lambda_im': dli.reshape(S5_G, S5_P), 's5_log_dt': dlogdt[:, 0],
        's5_b_re': _diag_blocks(dbre, S5_CG, S5_P).transpose(0, 2, 1),
        's5_b_im': _diag_blocks(dbim, S5_CG, S5_P).transpose(0, 2, 1),
        's5_c_re': _diag_blocks(dccat[:S5_LANES], S5_P, S5_CG).transpose(0, 2, 1),
        's5_c_im': -_diag_blocks(dccat[S5_LANES:], S5_P, S5_CG).transpose(0, 2, 1),
        's5_d': dd[0], 's5_w_glu': dwglu}
    return du1 + du2, grads


DN_CONV = 4


def _head_sum_matrix():
    h = np.arange(GROUP_W) // HEAD_DIM
    return jnp.asarray((h[:, None] == h[None, :]).astype(np.float32))


def _f_dn_pre(x0, x1, x2, x3, ab, w0, w1, w2, w3, alog, dtb, ea, eb, hs):
    c = w0 * x0 + w1 * x1 + w2 * x2 + w3 * x3
    s = c * jax.nn.sigmoid(c)
    q, k, v = s[:, :GROUP_W], s[:, GROUP_W:2 * GROUP_W], s[:, 2 * GROUP_W:]
    q = q * lax.rsqrt(_hdot(q * q, hs) + EPS) * (HEAD_DIM ** -0.5)
    k = k * lax.rsqrt(_hdot(k * k, hs) + EPS)
    beta = jax.nn.sigmoid(_hdot(ab, eb))
    g = -jnp.exp(alog) * jax.nn.softplus(_hdot(ab, ea) + dtb)
    return q, k, v, g, beta


DN_CHUNKS_PER_STEP = 4


def _f_dn_chunks(q, k, v, g, beta):
    C = DN_CHUNK
    n_chunks = q.shape[0] // C
    r = lax.broadcasted_iota(jnp.int32, (C, C), 0)
    c = lax.broadcasted_iota(jnp.int32, (C, C), 1)
    causal, strict = r >= c, r > c
    eye = (r == c).astype(f32)
    tril = causal.astype(f32)
    ones = jnp.ones((C, GROUP_W), f32)
    masks = [_head_mask(h) for h in range(N_HEADS)]
    rows = [tuple(t[i * C:(i + 1) * C] for t in (q, k, v, g, beta)) for i in range(n_chunks)]
    gcs = [_hdot(tril, gi) for (_, _, _, gi, _) in rows]
    items = [(i, h) for i in range(n_chunks) for h in range(N_HEADS)]
    grows = [_hdot_nt(ones * (masks[h] * (1.0 / HEAD_DIM)), gcs[i]) for i, h in items]
    decs = []
    for (i, h), grow in zip(items, grows):
        gcol = jnp.sum(gcs[i] * masks[h], axis=1, keepdims=True) * (1.0 / HEAD_DIM)
        decs.append(jnp.exp(jnp.where(causal, gcol - grow, NEG_INF)))
    kbs = [ki * bi for (_, ki, _, _, bi) in rows]
    kks = [_bmm_nt(kbs[i] * masks[h], rows[i][1]) for i, h in items]
    qks = [_bmm_nt(rows[i][0] * masks[h], rows[i][1]) for i, h in items]
    lmats = [jnp.where(strict, kk * dec, 0.0) for kk, dec in zip(kks, decs)]
    a_qk = [jnp.where(causal, qk * dec, 0.0) for qk, dec in zip(qks, decs)]
    ts = [eye - lm for lm in lmats]
    ps = lmats
    for _ in range(5):
        ps = [_bmm(p, p) for p in ps]
        ts = [t + _bmm(t, p) for t, p in zip(ts, ps)]
    egs = [jnp.exp(gc) for gc in gcs]
    tw = [_bmm(t, kbs[i] * egs[i]) for (i, h), t in zip(items, ts)]
    tu = [_bmm(t, rows[i][2] * rows[i][4]) for (i, h), t in zip(items, ts)]
    outs = []
    for i in range(n_chunks):
        qi, ki, _, gi, _ = rows[i]
        glast = jnp.sum(gi, axis=0, keepdims=True)
        w = sum(tw[i * N_HEADS + h] * masks[h] for h in range(N_HEADS))
        u = sum(tu[i * N_HEADS + h] * masks[h] for h in range(N_HEADS))
        outs.append((w, u, qi * egs[i], ki * jnp.exp(glast - gcs[i]), *a_qk[i * N_HEADS:(i + 1) * N_HEADS],
                     jnp.broadcast_to(jnp.exp(glast), (C, GROUP_W))))
    return tuple(jnp.concatenate(parts, axis=0) for parts in zip(*outs))


def _f_dn_step(w, u, qd, kdec, a0, a1, a2, a3, dfull, state, bd):
    row0 = (lax.broadcasted_iota(jnp.int32, dfull.shape, 0) == 0).astype(f32)
    dvec = jnp.sum(dfull * row0, axis=0, keepdims=True)
    ws, qs = _bmm(w, state), _bmm(qd, state)
    vnew = u - ws
    avs = [_bmm(a, vnew) for a in (a0, a1, a2, a3)]
    kv = _bmm_tn(kdec, vnew)
    o = qs + sum(av * _head_mask(h) for h, av in enumerate(avs))
    return o, state * dvec + bd * kv


def _dn_scan_fwd(ins, name):
    S = ins[0].shape[0]
    N = S // DN_CHUNK
    bd = _head_sum_matrix()

    def body(*refs):
        o_ref, s_ref, state = refs[10], refs[11], refs[12]

        @pl.when(pl.program_id(0) == 0)
        def _():
            state[...] = jnp.zeros_like(state)

        s_in = state[...]
        s_ref[0] = s_in
        o, s_out = _f_dn_step(*[r[...] for r in refs[:9]], s_in, refs[9][...])
        o_ref[...] = o
        state[...] = s_out

    return pl.pallas_call(
        body, name=name, grid=(N,),
        in_specs=[pl.BlockSpec((DN_CHUNK, t.shape[1]), lambda n: (n, 0)) for t in ins] + [_full_spec(bd)],
        out_specs=[pl.BlockSpec((DN_CHUNK, GROUP_W), lambda n: (n, 0)), pl.BlockSpec((1, GROUP_W, GROUP_W), lambda n: (n, 0, 0))],
        out_shape=[jax.ShapeDtypeStruct((S, GROUP_W), f32), jax.ShapeDtypeStruct((N, GROUP_W, GROUP_W), f32)],
        scratch_shapes=[pltpu.VMEM((GROUP_W, GROUP_W), f32)],
        compiler_params=_cparams(('arbitrary',)),
    )(*ins, bd)


def _dn_scan_bwd(ins, states, do, name):
    S = ins[0].shape[0]
    N = S // DN_CHUNK
    bd = _head_sum_matrix()

    def body(*refs):
        s_ref, do_ref = refs[9], refs[10]
        bd_ref = refs[11]
        outs = refs[12:21]
        dstate = refs[21]

        @pl.when(pl.program_id(0) == 0)
        def _():
            dstate[...] = jnp.zeros_like(dstate)

        bd_val = bd_ref[...]
        _, vjp = jax.vjp(lambda *a: _f_dn_step(*a, bd_val), *[r[...] for r in refs[:9]], s_ref[0])
        grads = vjp((do_ref[...], dstate[...]))
        for o, g in zip(outs, grads[:9]):
            o[...] = g
        dstate[...] = grads[9]

    def rev(n):
        return (N - 1 - n, 0)

    res = pl.pallas_call(
        body, name=name, grid=(N,),
        in_specs=[pl.BlockSpec((DN_CHUNK, t.shape[1]), rev) for t in ins] +
                 [pl.BlockSpec((1, GROUP_W, GROUP_W), lambda n: (N - 1 - n, 0, 0)), pl.BlockSpec((DN_CHUNK, GROUP_W), rev),
                  _full_spec(bd)],
        out_specs=[pl.BlockSpec((DN_CHUNK, t.shape[1]), rev) for t in ins],
        out_shape=[jax.ShapeDtypeStruct(t.shape, f32) for t in ins],
        scratch_shapes=[pltpu.VMEM((GROUP_W, GROUP_W), f32)],
        compiler_params=_cparams(('arbitrary',)),
    )(*ins, states, do, bd)
    return list(res)


def _f_dn_post(o, gate, gain, hmean):
    return (o * lax.rsqrt(_hdot(o * o, hmean) + EPS) * gain * (gate * jax.nn.sigmoid(gate)),)


def _dn_delays(x, name):
    S, C = x.shape
    tm = _pick(S, (256, 128))

    def body(prev_ref, cur_ref, *outs):
        before = jnp.where(pl.program_id(0) > 0, prev_ref[...], 0.0)
        both = jnp.concatenate([before, cur_ref[...]], axis=0)
        for o, k in zip(outs, range(DN_CONV - 1, 0, -1)):
            o[...] = pltpu.roll(both, k, 0)[tm:]

    spec = pl.BlockSpec((tm, C), lambda i: (i, 0))
    return pl.pallas_call(
        body, name=name, grid=(S // tm,),
        in_specs=[pl.BlockSpec((tm, C), lambda i: (jnp.maximum(i - 1, 0), 0)), spec],
        out_specs=[spec] * (DN_CONV - 1), out_shape=[jax.ShapeDtypeStruct((S, C), x.dtype)] * (DN_CONV - 1),
        compiler_params=_cparams(('parallel',), big=True),
    )(x, x)


def _dn_undelay_sum(ds, name):
    S, C = ds[0].shape
    tm = _pick(S, (256, 128))
    n = S // tm

    def body(*refs):
        o = refs[-1]
        total = refs[2 * (DN_CONV - 1)][...]
        for j in range(DN_CONV - 1):
            k = DN_CONV - 1 - j
            after = jnp.where(pl.program_id(0) < n - 1, refs[2 * j + 1][...], 0.0)
            both = jnp.concatenate([refs[2 * j][...], after], axis=0)
            total = total + pltpu.roll(both, 2 * tm - k, 0)[:tm]
        o[...] = total

    spec = pl.BlockSpec((tm, C), lambda i: (i, 0))
    nxt = pl.BlockSpec((tm, C), lambda i: (jnp.minimum(i + 1, n - 1), 0))
    args, in_specs = [], []
    for j in range(DN_CONV - 1):
        args += [ds[j], ds[j]]
        in_specs += [spec, nxt]
    return pl.pallas_call(
        body, name=name, grid=(n,), in_specs=in_specs + [spec], out_specs=spec,
        out_shape=jax.ShapeDtypeStruct((S, C), f32), compiler_params=_cparams(('parallel',), big=True),
    )(*args, ds[DN_CONV - 1])


def _dn_fwd(qkv, a, b, gate, mp, l):
    S = qkv.shape[0]
    tm = _pick(S, (256, 128))
    xs = [*_dn_delays(qkv, f'dn_delay_{l}'), qkv]
    ab = jnp.pad(jnp.concatenate([a, b], axis=1), ((0, 0), (0, LANES - 2 * N_HEADS)))
    sel = np.zeros((2, LANES, GROUP_W), np.float32)
    for h in range(N_HEADS):
        sel[0, h, h * HEAD_DIM:(h + 1) * HEAD_DIM] = 1.0
        sel[1, N_HEADS + h, h * HEAD_DIM:(h + 1) * HEAD_DIM] = 1.0
    pre_params = [*[mp['dn_conv'][j][None] for j in range(DN_CONV)], jnp.repeat(mp['dn_a_log'], HEAD_DIM)[None],
                  jnp.repeat(mp['dn_dt_bias'], HEAD_DIM)[None], jnp.asarray(sel[0]), jnp.asarray(sel[1]), _head_sum_matrix()]
    pre = _tile_fwd(_f_dn_pre, [*xs, ab], pre_params, [(GROUP_W, f32)] * 5, tm, f'dn_pre_fwd_{l}')
    chunk_outs = [(GROUP_W, f32)] * 4 + [(HEAD_DIM, f32)] * 4 + [(GROUP_W, f32)]
    parts = _tile_fwd(_f_dn_chunks, pre, [], chunk_outs, DN_CHUNK * DN_CHUNKS_PER_STEP, f'dn_chunk_fwd_{l}')
    o, states = _dn_scan_fwd(parts, f'dn_scan_fwd_{l}')
    post_params = [jnp.tile(mp['dn_o_norm'], N_HEADS)[None], _head_mean_matrix()]
    (y,) = _tile_fwd(_f_dn_post, [o, gate], post_params, [(GROUP_W, f32)], tm, f'dn_post_fwd_{l}')
    return y, (xs, ab, pre_params, pre, parts, states, o, gate, post_params)


def _dn_bwd(dy, saved, l):
    xs, ab, pre_params, pre, parts, states, o, gate, post_params = saved
    S = dy.shape[0]
    tm = _pick(S, (256, 128))
    (do, dgate), (dgain,) = _tile_bwd(_f_dn_post, [o, gate], post_params, [dy], [True, True], [True, False], tm,
                                      f'dn_post_bwd_{l}')
    dparts = _dn_scan_bwd(parts, states, do, f'dn_scan_bwd_{l}')
    dpre, _ = _tile_bwd(_f_dn_chunks, pre, [], dparts, [True] * 5, [], DN_CHUNK * DN_CHUNKS_PER_STEP, f'dn_chunk_bwd_{l}')
    dins, dpar = _tile_bwd(_f_dn_pre, [*xs, ab], pre_params, dpre, [True] * 5, [True] * 6 + [False] * 3, tm,
                           f'dn_pre_bwd_{l}')
    dqkv = _dn_undelay_sum(dins[:DN_CONV], f'dn_undelay_{l}')
    dab = dins[DN_CONV]
    grads = {'dn_conv': jnp.concatenate(dpar[:DN_CONV], axis=0),
             'dn_a_log': dpar[4].reshape(N_HEADS, HEAD_DIM).sum(1), 'dn_dt_bias': dpar[5].reshape(N_HEADS, HEAD_DIM).sum(1),
             'dn_o_norm': dgain.reshape(N_HEADS, HEAD_DIM).sum(0)}
    return dqkv, dab[:, :N_HEADS], dab[:, N_HEADS:2 * N_HEADS], dgate, grads


def _t5_bucket(dist):
    exact = T5_BUCKETS // 2
    df = jnp.maximum(dist, 1).astype(f32)
    large = exact + (jnp.log(df / exact) / math.log(T5_MAX_DIST / exact) * (T5_BUCKETS - exact)).astype(jnp.int32)
    large = jnp.minimum(large, T5_BUCKETS - 1)
    return jnp.where(dist < exact, dist, large)


def _split_cols(t, sizes):
    out, start = [], 0
    for s in sizes:
        out.append(t[..., start:start + s])
        start += s
    return out


def _mixers_fwd(proj, mp, l):
    c_q, c_kv, k_rope, u_s5, qkv_dil, qkv_dn, a_dn, b_dn, gate_dn = _split_cols(proj, IN_SPLITS)
    y_mla, s_mla = _mla_fwd(c_q, c_kv, k_rope, mp, l)
    y_s5, s_s5 = _s5_fwd(u_s5, mp, l)
    y_dil, s_dil = _dil_fwd(qkv_dil, mp, l)
    y_dn, s_dn = _dn_fwd(qkv_dn, a_dn, b_dn, gate_dn, mp, l)
    return jnp.concatenate([y_mla, y_s5, y_dil, y_dn], axis=-1), (s_mla, s_s5, s_dil, s_dn)


def _mixers_bwd(dmixed, saved, l):
    s_mla, s_s5, s_dil, s_dn = saved
    d_mla, d_s5, d_dil, d_dn = _split_cols(dmixed, (GROUP_W,) * 4)
    dc_q, dc_kv, dk_rope, g_mla = _mla_bwd(d_mla, s_mla, l)
    du, g_s5 = _s5_bwd(d_s5, s_s5, l)
    dqkv_dil, g_dil = _dil_bwd(d_dil, s_dil, l)
    dqkv_dn, da, db, dgate, g_dn = _dn_bwd(d_dn, s_dn, l)
    parts = [dc_q, dc_kv, dk_rope, du, dqkv_dil, dqkv_dn, da, db, dgate]
    dproj = jnp.concatenate([p.astype(bf16) for p in parts], axis=-1)
    return dproj, {**g_mla, **g_s5, **g_dil, **g_dn}


MIXER_PARAMS = ['mla_q_norm', 'mla_kv_norm', 'mla_w_uq', 'mla_w_ukv', 'mla_qk_q', 'mla_qk_k', 's5_lambda_re',
                's5_lambda_im', 's5_log_dt', 's5_b_re', 's5_b_im', 's5_c_re', 's5_c_im', 's5_d', 's5_w_glu',
                'dil_q_norm', 'dil_k_norm', 't5_bias', 'dn_conv', 'dn_a_log', 'dn_dt_bias', 'dn_o_norm']


def _layer_fwd_mix(h, W, l):
    S = h.shape[0]
    tm = _pick(S, (256, 128))
    g1 = W['attn_norm'][l][None]
    (n1,) = _tile_fwd(_f_rms, [h], [g1], [(D_MODEL, bf16)], tm, f'rms1_fwd_{l}')
    proj = _mm(n1, W['w_in'][l], 'nn', f'proj_fwd_{l}')
    mp = {k: (W[k] if k == 't5_bias' else W[k][l]).astype(f32) for k in MIXER_PARAMS}
    mixed, mix_saved = _mixers_fwd(proj, mp, l)
    mixed_b = mixed.astype(bf16)
    h2 = _mm(mixed_b, W['w_out'][l], 'nn', f'out_fwd_{l}', add=h)
    return h2, dict(h=h, n1=n1, mix=mix_saved, mixed=mixed_b, h2=h2)


def _layer_fwd_ffn(h2, W, l, saved):
    S = h2.shape[0]
    tm = _pick(S, (256, 128))
    g2 = W['ffn_norm'][l][None]
    (n2,) = _tile_fwd(_f_rms, [h2], [g2], [(D_MODEL, bf16)], tm, f'rms2_fwd_{l}')
    u = _mm(n2, W['ffn_w1'][l], 'nn', f'ffn1_fwd_{l}')
    v = _mm(n2, W['ffn_w3'][l], 'nn', f'ffn3_fwd_{l}')
    (act,) = _tile_fwd(_f_swiglu, [u, v], [], [(FFN_HIDDEN, bf16)], tm, f'swiglu_fwd_{l}')
    h3 = _mm(act, W['ffn_w2'][l], 'nn', f'ffn2_fwd_{l}', add=h2)
    saved.update(n2=n2, u=u, v=v, act=act)
    return h3


def _layer_bwd_ffn(dh3, saved, W, l):
    S = dh3.shape[0]
    tm = _pick(S, (256, 128))
    g2 = W['ffn_norm'][l][None]
    grads = {}
    dact = _mm(dh3, W['ffn_w2'][l], 'nt', f'ffn2_dx_{l}')
    grads['ffn_w2'] = _mm(saved['act'], dh3, 'tn', f'ffn2_dw_{l}', out_dtype=bf16)
    (du, dv), _ = _tile_bwd(_f_swiglu, [saved['u'], saved['v']], [], [dact], [True, True], [], tm, f'swiglu_bwd_{l}',
                            dt_dtypes=[bf16, bf16])
    dn2 = _mm(dv, W['ffn_w3'][l], 'nt', f'ffn3_dx_{l}', add=_mm(du, W['ffn_w1'][l], 'nt', f'ffn1_dx_{l}'))
    grads['ffn_w1'] = _mm(saved['n2'], du, 'tn', f'ffn1_dw_{l}', out_dtype=bf16)
    grads['ffn_w3'] = _mm(saved['n2'], dv, 'tn', f'ffn3_dw_{l}', out_dtype=bf16)
    (dh2n,), (dg2,) = _tile_bwd(_f_rms, [saved['h2']], [g2], [dn2], [True], [True], tm, f'rms2_bwd_{l}')
    grads['ffn_norm'] = dg2[0]
    return (dh3, dh2n), grads


def _layer_bwd_mix(dh2, saved, W, l):
    S = dh2.shape[0]
    tm = _pick(S, (256, 128))
    g1 = W['attn_norm'][l][None]
    grads = {}
    dmixed = _mm(dh2, W['w_out'][l], 'nt', f'out_dx_{l}')
    grads['w_out'] = _mm(saved['mixed'], dh2, 'tn', f'out_dw_{l}', out_dtype=bf16)
    dproj, dmp = _mixers_bwd(dmixed, saved['mix'], l)
    for k in MIXER_PARAMS:
        grads[k] = dmp[k]
    dn1 = _mm(dproj, W['w_in'][l], 'nt', f'proj_dx_{l}')
    grads['w_in'] = _mm(saved['n1'], dproj, 'tn', f'proj_dw_{l}', out_dtype=bf16)
    (dh1n,), (dg1,) = _tile_bwd(_f_rms, [saved['h']], [g1], [dn1], [True], [True], tm, f'rms1_bwd_{l}')
    grads['attn_norm'] = dg1[0]
    return (dh2, dh1n), grads


def kernel(x, attn_norm, w_in, w_out, mla_q_norm, mla_kv_norm, mla_w_uq, mla_w_ukv, mla_qk_q, mla_qk_k, s5_lambda_re, s5_lambda_im, s5_log_dt, s5_b_re, s5_b_im, s5_c_re, s5_c_im, s5_d, s5_w_glu, dil_q_norm, dil_k_norm, t5_bias, dn_conv, dn_a_log, dn_dt_bias, dn_o_norm, ffn_norm, ffn_w1, ffn_w3, ffn_w2, loss_target, m_attn_norm, m_w_in, m_w_out, m_mla_q_norm, m_mla_kv_norm, m_mla_w_uq, m_mla_w_ukv, m_mla_qk_q, m_mla_qk_k, m_s5_lambda_re, m_s5_lambda_im, m_s5_log_dt, m_s5_b_re, m_s5_b_im, m_s5_c_re, m_s5_c_im, m_s5_d, m_s5_w_glu, m_dil_q_norm, m_dil_k_norm, m_t5_bias, m_dn_conv, m_dn_a_log, m_dn_dt_bias, m_dn_o_norm, m_ffn_norm, m_ffn_w1, m_ffn_w3, m_ffn_w2, v_attn_norm, v_w_in, v_w_out, v_mla_q_norm, v_mla_kv_norm, v_mla_w_uq, v_mla_w_ukv, v_mla_qk_q, v_mla_qk_k, v_s5_lambda_re, v_s5_lambda_im, v_s5_log_dt, v_s5_b_re, v_s5_b_im, v_s5_c_re, v_s5_c_im, v_s5_d, v_s5_w_glu, v_dil_q_norm, v_dil_k_norm, v_t5_bias, v_dn_conv, v_dn_a_log, v_dn_dt_bias, v_dn_o_norm, v_ffn_norm, v_ffn_w1, v_ffn_w3, v_ffn_w2):
    given = dict(locals())
    w_loc = {n: given[n] for n in WEIGHTS}
    m_loc = {n: given['m_' + n] for n in WEIGHTS}
    v_loc = {n: given['v_' + n] for n in WEIGHTS}
    big_names = list(BIG)

    own = 2 * lax.axis_index('x') + lax.axis_index('y')
    groups = [[(n, 0) for n in GATHER_FIRST], [(n, 0) for n in GATHER_FFN], [(n, 1) for n in big_names]]
    started, order = [], jnp.zeros((8, LANES), f32)
    for gi, group in enumerate(groups):
        blocks = [w_loc[n][l].astype(bf16) for n, l in group]
        lands = [lax.empty((N_SHARDS,) + b.shape, bf16) for b in blocks]
        send_sems, recv_sems, blocks, lands, order = _to_chips_start(blocks, lands, False, order, f'gather_start_{gi}')
        started.append((send_sems, recv_sems, blocks, lands))
    W = {n: [None] * DEPTH for n in big_names}
    for n in SMALL:
        W[n] = w_loc[n]

    def arrive(gi, after):
        send_sems, recv_sems, blocks, lands = started[gi]
        blocks, lands = _to_chips_wait(send_sems, recv_sems, blocks, lands, False, after, f'gather_wait_{gi}')
        for (n, l), block, land in zip(groups[gi], blocks, lands):
            W[n][l] = _from_shards(n, lax.dynamic_update_slice(land, block[None], (own, 0, 0)))

    arrive(0, order)
    h = x[0]
    saved = []
    for l in range(DEPTH):
        h2, sv = _layer_fwd_mix(h, W, l)
        if l == 0:
            arrive(1, h2)
        h = _layer_fwd_ffn(h2, W, l, sv)
        if l == 0:
            arrive(2, h)
        saved.append(sv)
    parts_loss, dh = _loss_head(h, loss_target[0])
    local_loss = jnp.sum(parts_loss)

    layer_grads = [dict() for _ in range(DEPTH)]
    sent = []

    def send(group, tag):
        srcs = [_by_shard(n, layer_grads[l][n]).astype(bf16) for n, l in group]
        lands = [lax.empty((3,) + s.shape[1:], bf16) for s in srcs]
        send_sems, recv_sems, srcs, lands, token = _to_chips_start(srcs, lands, True, jnp.zeros((8, LANES), f32),
                                                                   f'reduce_start_{tag}')
        sent.append((group, tag, send_sems, recv_sems, srcs, lands))
        return token[0, 0]

    for l in reversed(range(DEPTH)):
        (dh3, dh2n), g_ffn = _layer_bwd_ffn(dh, saved[l], W, l)
        layer_grads[l].update(g_ffn)
        dh2 = dh3 + dh2n
        if l == 0:
            dh2 = dh2 + send([(n, 0) for n in GATHER_FFN], 'ffn0')
        (dh2, dh1n), g_mix = _layer_bwd_mix(dh2, saved[l], W, l)
        layer_grads[l].update(g_mix)
        dh = dh2 + dh1n
        if l == 1:
            dh = dh + send([(n, 1) for n in big_names], 'layer1')
    last = send([(n, 0) for n in GATHER_FIRST], 'first0')
    grad_x = dh[None]
    small_full = []
    for n in SMALL:
        if n == 't5_bias':
            small_full.append(layer_grads[0][n] + layer_grads[1][n])
        else:
            small_full.append(jnp.stack([layer_grads[l][n] for l in range(DEPTH)]))

    small_shapes = [w_loc[n].shape for n in SMALL] + [(1,)]
    nothing = [jnp.zeros((1,), f32)]
    small_pack = _pack(small_full + [local_loss.reshape(1)]) + last
    _, recv_small = _swap_with_sibling([], small_pack)
    chip_small = _small_chip_sum(small_pack, recv_small)
    _, from_chips_small = _exchange_between_chips([], chip_small)

    mine = {}
    for group, tag, send_sems, recv_sems, srcs, lands in sent:
        srcs, lands = _to_chips_wait(send_sems, recv_sems, srcs, lands, True, from_chips_small, f'reduce_wait_{tag}')
        for (n, l), src, land in zip(group, srcs, lands):
            mine[(n, l)] = _partial_sum(src, land, f'partial_{n}_{l}')
    keys = [(n, l) for n in big_names for l in range(DEPTH)]
    theirs = dict(zip(keys, _swap_partials([mine[k] for k in keys])))

    g_small_p, d_small_p, m_small_p, v_small_p = _small_update(
        small_pack, recv_small, from_chips_small, _pack([w_loc[n] for n in SMALL] + nothing),
        _pack([m_loc[n] for n in SMALL] + nothing), _pack([v_loc[n] for n in SMALL] + nothing))
    loss = _unpack(g_small_p, small_shapes)[-1][0]
    grad, delta, new_m, new_v = {}, {}, {}, {}
    for n, g_, d_, m_, v_ in zip(SMALL, _unpack(g_small_p, small_shapes), _unpack(d_small_p, small_shapes),
                                 _unpack(m_small_p, small_shapes), _unpack(v_small_p, small_shapes)):
        grad[n], delta[n], new_m[n], new_v[n] = g_, d_, m_, v_
    for n in big_names:
        grad[n], delta[n], new_m[n], new_v[n] = _adamw(
            w_loc[n], m_loc[n], v_loc[n], [mine[(n, l)] for l in range(DEPTH)], [theirs[(n, l)] for l in range(DEPTH)],
            'adamw_' + n)
    return (loss, grad_x, *[grad[n] for n in WEIGHTS], *[delta[n] for n in WEIGHTS],
            *[new_m[n] for n in WEIGHTS], *[new_v[n] for n in WEIGHTS])
```

```python
import functools
import math

import numpy as np
import jax
import jax.numpy as jnp
from jax import lax
from jax.experimental import pallas as pl
from jax.experimental.pallas import tpu as pltpu

f32 = jnp.float32
bf16 = jnp.bfloat16
HI = lax.Precision.HIGHEST
MESH = pl.DeviceIdType.MESH

VMEM_LIMIT_BYTES = 48 * 1024 * 1024
MM_VMEM_BUDGET_BYTES = 32 * 1024 * 1024
LANES = 128

D_MODEL = 1024
DEPTH = 2
GROUP_W = 256
HEAD_DIM = 64
EPS = 1e-6
NEG_INF = -1e30
N_HEADS = 4
MLA_NOPE, MLA_ROPE = 64, 32
MLA_DQK = MLA_NOPE + MLA_ROPE
ROPE_THETA = 10000.0
Q_BLOCK = 128
S5_G, S5_CG, S5_P = 16, 16, 64
DIL_PAIRS = ((128, 1), (512, 4), (2048, 16))
T5_BUCKETS, T5_MAX_DIST = 32, 2048
DN_CHUNK = 64
FFN_HIDDEN = 2816
IN_SPLITS = (256, 128, 32, 256, 768, 768, 4, 4, 256)
IN_COLS = sum(IN_SPLITS)

ADAM_LR, ADAM_B1, ADAM_B2, ADAM_EPS, ADAM_WD, ADAM_STEP = 0.001, 0.9, 0.999, 1e-08, 0.01, 10

WEIGHTS = ['attn_norm', 'w_in', 'w_out', 'mla_q_norm', 'mla_kv_norm', 'mla_w_uq', 'mla_w_ukv', 'mla_qk_q', 'mla_qk_k',
           's5_lambda_re', 's5_lambda_im', 's5_log_dt', 's5_b_re', 's5_b_im', 's5_c_re', 's5_c_im', 's5_d', 's5_w_glu',
           'dil_q_norm', 'dil_k_norm', 't5_bias', 'dn_conv', 'dn_a_log', 'dn_dt_bias', 'dn_o_norm', 'ffn_norm',
           'ffn_w1', 'ffn_w3', 'ffn_w2']
BIG = {'w_in': 2, 'w_out': 1, 'mla_w_uq': 2, 'mla_w_ukv': 2, 's5_w_glu': 2, 'dn_conv': 2, 'ffn_w1': 2, 'ffn_w3': 2,
       'ffn_w2': 1}
SMALL = [n for n in WEIGHTS if n not in BIG]
GATHER_FIRST = ['w_in', 'mla_w_uq', 'mla_w_ukv', 's5_w_glu', 'dn_conv', 'w_out']
GATHER_FFN = ['ffn_w1', 'ffn_w3', 'ffn_w2']
N_SHARDS = 4
PACK_COLS = 1024


def _cparams(sem=None, big=False):
    kw = {}
    if sem is not None:
        kw['dimension_semantics'] = sem
    if big:
        kw['vmem_limit_bytes'] = VMEM_LIMIT_BYTES
    return pltpu.CompilerParams(**kw)


def _pick(n, prefs):
    for p in prefs:
        if p <= n and n % p == 0:
            return p
    return n


def _lane_tile(n, cap):
    for t in range(cap - cap % LANES, 0, -LANES):
        if n % t == 0:
            return t
    return n


def _mm(a, b, mode, name, add=None, out_dtype=f32):
    if mode == 'nn':
        (M, K), (K2, N) = a.shape, b.shape
    elif mode == 'nt':
        (M, K), (N, K2) = a.shape, b.shape
    else:
        (K, M), (K2, N) = a.shape, b.shape
    assert K == K2, (name, a.shape, b.shape)
    tk = K if K <= 2816 else _pick(K, (2816, 2048, 1408, 1024, 512))
    cap_m, cap_n = (1408 if mode == 'tn' else 512), 1408

    def need(tm_, tn_):
        per_step = tm_ * tk * a.dtype.itemsize + tk * tn_ * b.dtype.itemsize + tm_ * tn_ * jnp.dtype(out_dtype).itemsize
        if add is not None:
            per_step += tm_ * tn_ * add.dtype.itemsize
        return 2 * per_step + tm_ * tn_ * 4

    tm, tn = _lane_tile(M, cap_m), _lane_tile(N, cap_n)
    while need(tm, tn) > MM_VMEM_BUDGET_BYTES and cap_m > LANES:
        cap_m //= 2
        tm = _lane_tile(M, cap_m)
    nk = K // tk
    dims = {'nn': (((1,), (0,)), ((), ())), 'nt': (((1,), (1,)), ((), ())), 'tn': (((0,), (0,)), ((), ()))}[mode]
    has_add = add is not None

    def body(*refs):
        a_ref, b_ref = refs[0], refs[1]
        add_ref = refs[2] if has_add else None
        o_ref = refs[3] if has_add else refs[2]
        part = lax.dot_general(a_ref[...].astype(bf16), b_ref[...].astype(bf16), dims, preferred_element_type=f32)
        if nk == 1:
            if has_add:
                part = part + add_ref[...].astype(f32)
            o_ref[...] = part.astype(out_dtype)
        else:
            acc_ref = refs[-1]
            k = pl.program_id(2)

            @pl.when(k == 0)
            def _():
                acc_ref[...] = part

            @pl.when(k > 0)
            def _():
                acc_ref[...] += part

            @pl.when(k == nk - 1)
            def _():
                r = acc_ref[...]
                if has_add:
                    r = r + add_ref[...].astype(f32)
                o_ref[...] = r.astype(out_dtype)

    if mode == 'nn':
        a_spec = pl.BlockSpec((tm, tk), lambda i, j, k: (i, k))
        b_spec = pl.BlockSpec((tk, tn), lambda i, j, k: (k, j))
    elif mode == 'nt':
        a_spec = pl.BlockSpec((tm, tk), lambda i, j, k: (i, k))
        b_spec = pl.BlockSpec((tn, tk), lambda i, j, k: (j, k))
    else:
        a_spec = pl.BlockSpec((tk, tm), lambda i, j, k: (k, i))
        b_spec = pl.BlockSpec((tk, tn), lambda i, j, k: (k, j))
    in_specs = [a_spec, b_spec]
    args = [a, b]
    if has_add:
        in_specs.append(pl.BlockSpec((tm, tn), lambda i, j, k: (i, j)))
        args.append(add)
    return pl.pallas_call(
        body, name=name, grid=(M // tm, N // tn, nk), in_specs=in_specs,
        out_specs=pl.BlockSpec((tm, tn), lambda i, j, k: (i, j)),
        out_shape=jax.ShapeDtypeStruct((M, N), out_dtype),
        scratch_shapes=[pltpu.VMEM((tm, tn), f32)] if nk > 1 else [],
        compiler_params=_cparams(('parallel', 'parallel', 'arbitrary'), big=True),
    )(*args)


def _full_spec(p):
    nd = p.ndim
    return pl.BlockSpec(p.shape, lambda i, _nd=nd: (0,) * _nd)


def _tile_fwd(f, tiled, params, outs, tm, name):
    S = tiled[0].shape[0]
    nt, npar = len(tiled), len(params)

    def body(*refs):
        vals = [r[...].astype(f32) for r in refs[:nt + npar]]
        res = f(*vals)
        for r, o in zip(res, refs[nt + npar:]):
            o[...] = r.astype(o.dtype)

    return pl.pallas_call(
        body, name=name, grid=(S // tm,),
        in_specs=[pl.BlockSpec((tm, t.shape[1]), lambda i: (i, 0)) for t in tiled] + [_full_spec(p) for p in params],
        out_specs=[pl.BlockSpec((tm, c), lambda i: (i, 0)) for c, _ in outs],
        out_shape=[jax.ShapeDtypeStruct((S, c), dt) for c, dt in outs],
        compiler_params=_cparams(('parallel',), big=True),
    )(*tiled, *params)


def _tile_bwd(f, tiled, params, cts, diff_t, diff_p, tm, name, dt_dtypes=None):
    S = tiled[0].shape[0]
    nt, npar, nc = len(tiled), len(params), len(cts)
    it = [i for i in range(nt) if diff_t[i]]
    ip = [i for i in range(npar) if diff_p[i]]
    if dt_dtypes is None:
        dt_dtypes = [f32] * len(it)

    def body(*refs):
        vals = [r[...].astype(f32) for r in refs[:nt + npar]]
        ct_vals = tuple(r[...].astype(f32) for r in refs[nt + npar:nt + npar + nc])
        out_refs = refs[nt + npar + nc:]

        def g(*dv):
            full = list(vals)
            for k, i in enumerate(it):
                full[i] = dv[k]
            for k, i in enumerate(ip):
                full[nt + i] = dv[len(it) + k]
            return tuple(f(*full))

        _, vjp = jax.vjp(g, *[vals[i] for i in it], *[vals[nt + i] for i in ip])
        grads = vjp(ct_vals)
        for k in range(len(it)):
            out_refs[k][...] = grads[k].astype(out_refs[k].dtype)
        step = pl.program_id(0)
        for k in range(len(ip)):
            o = out_refs[len(it) + k]
            gk = grads[len(it) + k]

            @pl.when(step == 0)
            def _(o=o, gk=gk):
                o[...] = gk

            @pl.when(step > 0)
            def _(o=o, gk=gk):
                o[...] += gk

    out_specs = [pl.BlockSpec((tm, tiled[i].shape[1]), lambda i_: (i_, 0)) for i in it] + [_full_spec(params[i]) for i in ip]
    out_shape = [jax.ShapeDtypeStruct(tiled[i].shape, dt_dtypes[k]) for k, i in enumerate(it)] + \
                [jax.ShapeDtypeStruct(params[i].shape, f32) for i in ip]
    res = pl.pallas_call(
        body, name=name, grid=(S // tm,),
        in_specs=[pl.BlockSpec((tm, t.shape[1]), lambda i: (i, 0)) for t in tiled] + [_full_spec(p) for p in params] +
                 [pl.BlockSpec((tm, c.shape[1]), lambda i: (i, 0)) for c in cts],
        out_specs=out_specs, out_shape=out_shape,
        compiler_params=_cparams(('arbitrary',), big=True),
    )(*tiled, *params, *cts)
    return list(res[:len(it)]), list(res[len(it):])


def _rms(x, g):
    return x * lax.rsqrt(jnp.mean(x * x, axis=-1, keepdims=True) + EPS) * g


def _f_rms(x, g):
    return (_rms(x, g),)


def _f_swiglu(u, v):
    return (u * jax.nn.sigmoid(u) * v,)


def _loss_head(y, target):
    S, D = y.shape
    tm = _pick(S, (256, 128))

    def body(y_ref, t_ref, part_ref, dy_ref):
        e = y_ref[...] - t_ref[...]
        dy_ref[...] = e * (1.0 / D)
        s = 0.5 * jnp.sum(jnp.sum(e * e, axis=1, keepdims=True), axis=0, keepdims=True) * (1.0 / D)
        r = lax.broadcasted_iota(jnp.int32, (8, LANES), 0)
        c = lax.broadcasted_iota(jnp.int32, (8, LANES), 1)
        part_ref[0] = jnp.where((r == 0) & (c == 0), s, 0.0)

    return pl.pallas_call(
        body, name='loss_head', grid=(S // tm,),
        in_specs=[pl.BlockSpec((tm, D), lambda i: (i, 0))] * 2,
        out_specs=[pl.BlockSpec((1, 8, LANES), lambda i: (i, 0, 0)), pl.BlockSpec((tm, D), lambda i: (i, 0))],
        out_shape=[jax.ShapeDtypeStruct((S // tm, 8, LANES), f32), jax.ShapeDtypeStruct((S, D), f32)],
        compiler_params=_cparams(('parallel',)),
    )(y, target)


def _pack_rows_of(shape):
    rows = -(-math.prod(shape) // PACK_COLS)
    return -(-rows // 8) * 8


def _pack(arrs):
    parts = []
    for a in arrs:
        rows = _pack_rows_of(a.shape)
        flat = a.astype(f32).reshape(-1)
        parts.append(jnp.pad(flat, (0, rows * PACK_COLS - flat.shape[0])).reshape(rows, PACK_COLS))
    return jnp.concatenate(parts, axis=0)


def _unpack(pack, shapes):
    out, row = [], 0
    for s in shapes:
        rows = _pack_rows_of(s)
        out.append(pack[row:row + rows].reshape(-1)[:math.prod(s)].reshape(s))
        row += rows
    return out


ANY = pl.BlockSpec(memory_space=pl.ANY)


def _place():
    return lax.axis_index('x'), lax.axis_index('y'), lax.axis_index('c')


def _where():
    return jnp.stack([lax.axis_index('c'), 2 * lax.axis_index('x') + lax.axis_index('y')]).astype(jnp.int32)


def _remote(src, dst, send_sems, recv_sems, k, to):
    return pltpu.make_async_remote_copy(src_ref=src, dst_ref=dst, send_sem=send_sems.at[k], recv_sem=recv_sems.at[k],
                                        device_id=to, device_id_type=MESH)


def _swap_with_sibling(gs, small):
    n = len(gs)

    def body(*refs):
        g_refs, s_ref = refs[:n], refs[n]
        r_refs, rs_ref = refs[n + 1:2 * n + 1], refs[2 * n + 1]
        send_sems, recv_sems = refs[2 * n + 2:]
        x, y, c = _place()
        sib = (x, y, 1 - c)
        cps = [_remote(g_refs[t].at[:, 1 - c], r_refs[t], send_sems, recv_sems, t, sib) for t in range(n)]
        cps.append(_remote(s_ref, rs_ref, send_sems, recv_sems, n, sib))
        for cp in cps:
            cp.start()
        for cp in cps:
            cp.wait()

    res = pl.pallas_call(
        body, name='swap_with_sibling', in_specs=[ANY] * (n + 1), out_specs=[ANY] * (n + 1),
        out_shape=[jax.ShapeDtypeStruct((N_SHARDS,) + g.shape[2:], g.dtype) for g in gs] +
                  [jax.ShapeDtypeStruct(small.shape, small.dtype)],
        scratch_shapes=[pltpu.SemaphoreType.DMA((n + 1,)), pltpu.SemaphoreType.DMA((n + 1,))],
    )(*gs, small)
    return list(res[:n]), res[n]


def _exchange_between_chips(cs, small):
    n = len(cs)

    def body(*refs):
        c_refs, s_ref = refs[:n], refs[n]
        r_refs, rs_ref = refs[n + 1:2 * n + 1], refs[2 * n + 1]
        send_sems, recv_sems = refs[2 * n + 2:]
        x, y, c = _place()
        chips = [(1 - x, y), (x, 1 - y), (1 - x, 1 - y)]
        cps = []
        for j, (px, py) in enumerate(chips):
            for t in range(n):
                cps.append(_remote(c_refs[t].at[2 * px + py], r_refs[t].at[j], send_sems, recv_sems, 3 * t + j, (px, py, c)))
            cps.append(_remote(s_ref, rs_ref.at[j], send_sems, recv_sems, 3 * n + j, (px, py, c)))
        for cp in cps:
            cp.start()
        for cp in cps:
            cp.wait()

    res = pl.pallas_call(
        body, name='exchange_between_chips', in_specs=[ANY] * (n + 1), out_specs=[ANY] * (n + 1),
        out_shape=[jax.ShapeDtypeStruct((3,) + c.shape[1:], c.dtype) for c in cs] +
                  [jax.ShapeDtypeStruct((3,) + small.shape, small.dtype)],
        scratch_shapes=[pltpu.SemaphoreType.DMA((3 * n + 3,)), pltpu.SemaphoreType.DMA((3 * n + 3,))],
    )(*cs, small)
    return list(res[:n]), res[n]


def _swap_partials(ts):
    n = len(ts)

    def body(*refs):
        t_refs, o_refs = refs[:n], refs[n:2 * n]
        send_sems, recv_sems = refs[2 * n:]
        x, y, c = _place()
        cps = [_remote(t_refs[t], o_refs[t], send_sems, recv_sems, t, (x, y, 1 - c)) for t in range(n)]
        for cp in cps:
            cp.start()
        for cp in cps:
            cp.wait()

    return pl.pallas_call(
        body, name='swap_partials', in_specs=[ANY] * n, out_specs=[ANY] * n,
        out_shape=[jax.ShapeDtypeStruct(t.shape, t.dtype) for t in ts],
        scratch_shapes=[pltpu.SemaphoreType.DMA((n,)), pltpu.SemaphoreType.DMA((n,))],
    )(*ts)


HBM = pl.BlockSpec(memory_space=pltpu.HBM)
SEM = pl.BlockSpec(memory_space=pltpu.SEMAPHORE)
DATAFLOW = pltpu.SideEffectType.DATAFLOW_SIDE_EFFECTING


def _in_hbm(t):
    return pltpu.with_memory_space_constraint(t, pltpu.HBM)


def _other_chips():
    x, y, c = _place()
    return [(1 - x, y, c), (x, 1 - y, c), (1 - x, 1 - y, c)]


def _to_chips_copies(src_refs, land_refs, send_sems, recv_sems, per_peer):
    x, y, _ = _place()
    cps = []
    for t, (src, land) in enumerate(zip(src_refs, land_refs)):
        for j, (px, py, pc) in enumerate(_other_chips()):
            s = src.at[2 * px + py] if per_peer else src
            d = land.at[j] if per_peer else land.at[2 * x + y]
            cps.append(_remote(s, d, send_sems, recv_sems, 3 * t + j, (px, py, pc)))
    return cps


def _to_chips_start(srcs, lands, per_peer, order, name):
    n = len(srcs)

    def body(*refs):
        src_refs, land_refs = refs[:n], refs[n:2 * n]
        send_sems, recv_sems = refs[2 * n + 1], refs[2 * n + 2]
        token = refs[-1]
        for cp in _to_chips_copies(src_refs, land_refs, send_sems, recv_sems, per_peer):
            cp.start()
        token[...] = jnp.zeros_like(token)

    res = pl.pallas_call(
        body, name=name, in_specs=[HBM] * (2 * n) + [ANY],
        out_specs=[SEM, SEM] + [HBM] * (2 * n) + [pl.BlockSpec(memory_space=pltpu.VMEM)],
        out_shape=[pltpu.SemaphoreType.DMA((3 * n,)), pltpu.SemaphoreType.DMA((3 * n,))] +
                  [pltpu.HBM(t.shape, t.dtype) for t in srcs] + [pltpu.HBM(t.shape, t.dtype) for t in lands] +
                  [jax.ShapeDtypeStruct((8, LANES), f32)],
        input_output_aliases={i: 2 + i for i in range(2 * n)},
        compiler_params=pltpu.CompilerParams(has_side_effects=DATAFLOW),
    )(*[_in_hbm(t) for t in srcs], *[_in_hbm(t) for t in lands], order)
    return res[0], res[1], list(res[2:2 + n]), list(res[2 + n:2 + 2 * n]), res[-1]


def _to_chips_wait(send_sems, recv_sems, srcs, lands, per_peer, after, name):
    n = len(srcs)

    def body(*refs):
        src_refs, land_refs = refs[:n], refs[n:2 * n]
        send_ref, recv_ref = refs[2 * n], refs[2 * n + 1]
        for cp in _to_chips_copies(src_refs, land_refs, send_ref, recv_ref, per_peer):
            cp.wait_send()
            cp.wait_recv()

    res = pl.pallas_call(
        body, name=name, in_specs=[HBM] * (2 * n) + [SEM, SEM, ANY],
        out_specs=[HBM] * (2 * n),
        out_shape=[pltpu.HBM(t.shape, t.dtype) for t in srcs] + [pltpu.HBM(t.shape, t.dtype) for t in lands],
        input_output_aliases={i: i for i in range(2 * n)},
        compiler_params=pltpu.CompilerParams(has_side_effects=DATAFLOW),
    )(*srcs, *lands, send_sems, recv_sems, after)
    return list(res[:n]), list(res[n:])


def _row_tile(a):
    return _pick(a, (512, 256, 128, 64, 32, 16, 8))


def _partial_sum(g, land, name):
    _, a, b = g.shape
    tr = _row_tile(a)

    def body(w_ref, g_ref, r_ref, o_ref):
        t = g_ref[0].astype(f32) + r_ref[0].astype(f32)
        t = t + r_ref[1].astype(f32)
        t = t + r_ref[2].astype(f32)
        o_ref[...] = t.astype(o_ref.dtype)

    return pl.pallas_call(
        body, name=name,
        grid_spec=pltpu.PrefetchScalarGridSpec(
            num_scalar_prefetch=1, grid=(a // tr,),
            in_specs=[pl.BlockSpec((1, tr, b), lambda i, w: (w[1], i, 0)), pl.BlockSpec((3, tr, b), lambda i, w: (0, i, 0))],
            out_specs=pl.BlockSpec((tr, b), lambda i, w: (i, 0))),
        out_shape=jax.ShapeDtypeStruct((a, b), bf16),
        compiler_params=_cparams(('parallel',)),
    )(_where(), g, land)


def _by_shard(name, t):
    r, c = t.shape
    if BIG[name] == 2:
        return t.reshape(r, N_SHARDS, c // N_SHARDS).transpose(1, 0, 2)
    return t.reshape(N_SHARDS, r // N_SHARDS, c)


def _from_shards(name, g):
    s, a, b = g.shape
    if BIG[name] == 2:
        return g.transpose(1, 0, 2).reshape(a, s * b)
    return g.reshape(s * a, b)


def _adam_math(w, g, m, v):
    m = ADAM_B1 * m + (1.0 - ADAM_B1) * g
    v = ADAM_B2 * v + (1.0 - ADAM_B2) * (g * g)
    m_hat = m / (1.0 - ADAM_B1 ** ADAM_STEP)
    v_hat = v / (1.0 - ADAM_B2 ** ADAM_STEP)
    delta = -ADAM_LR * (m_hat / (jnp.sqrt(v_hat) + ADAM_EPS) + ADAM_WD * w)
    return delta, m, v


def _small_update(own, sib, chips, w, m, v):
    def body(o_ref, s_ref, c_ref, w_ref, m_ref, v_ref, g_out, d_out, m_out, v_out):
        chip = o_ref[...] + s_ref[...]
        g = (chip + c_ref[0]) + (c_ref[1] + c_ref[2])
        d, mn, vn = _adam_math(w_ref[...], g, m_ref[...], v_ref[...])
        g_out[...] = g
        d_out[...] = d
        m_out[...] = mn
        v_out[...] = vn

    return pl.pallas_call(body, name='small_update', out_shape=[jax.ShapeDtypeStruct(own.shape, f32)] * 4)(
        own, sib, chips, w, m, v)


def _small_chip_sum(own, sib):
    def body(o_ref, s_ref, out):
        out[...] = o_ref[...] + s_ref[...]
    return pl.pallas_call(body, name='small_chip_sum', out_shape=jax.ShapeDtypeStruct(own.shape, f32))(own, sib)


def _adamw(w, m, v, mine, theirs, name):
    layers, a, b = w.shape
    tr = _row_tile(a)

    def body(w_ref, m_ref, v_ref, p0, p1, q0, q1, g_out, d_out, m_out, v_out):
        first = pl.program_id(0) == 0
        g = jnp.where(first, p0[...].astype(f32) + q0[...].astype(f32), p1[...].astype(f32) + q1[...].astype(f32))
        d, mn, vn = _adam_math(w_ref[0], g, m_ref[0], v_ref[0])
        g_out[0] = g
        d_out[0] = d
        m_out[0] = mn
        v_out[0] = vn

    full = pl.BlockSpec((1, tr, b), lambda l, i: (l, i, 0))
    part = pl.BlockSpec((tr, b), lambda l, i: (i, 0))
    return pl.pallas_call(body, name=name, grid=(layers, a // tr), in_specs=[full] * 3 + [part] * 4, out_specs=[full] * 4,
                          out_shape=[jax.ShapeDtypeStruct(w.shape, f32)] * 4,
                          compiler_params=_cparams(('parallel', 'parallel')))(w, m, v, *mine, *theirs)


def _dg(a, b, ca, cb):
    return lax.dot_general(a.astype(bf16), b.astype(bf16), (((ca,), (cb,)), ((), ())), preferred_element_type=f32)


@jax.custom_vjp
def _bmm(a, b):
    return _dg(a, b, 1, 0)


_bmm.defvjp(lambda a, b: (_dg(a, b, 1, 0), (a, b)), lambda r, g: (_dg(g, r[1], 1, 1), _dg(r[0], g, 0, 0)))


@jax.custom_vjp
def _bmm_nt(a, b):
    return _dg(a, b, 1, 1)


_bmm_nt.defvjp(lambda a, b: (_dg(a, b, 1, 1), (a, b)), lambda r, g: (_dg(g, r[1], 1, 0), _dg(g, r[0], 0, 0)))


@jax.custom_vjp
def _bmm_tn(a, b):
    return _dg(a, b, 0, 0)


_bmm_tn.defvjp(lambda a, b: (_dg(a, b, 0, 0), (a, b)), lambda r, g: (_dg(r[1], g, 1, 1), _dg(r[0], g, 1, 0)))


def _hdot(a, b):
    return jnp.dot(a, b, precision=HI, preferred_element_type=f32)


def _hdot_nt(a, b):
    return lax.dot_general(a, b, (((1,), (1,)), ((), ())), precision=HI, preferred_element_type=f32)


def _hdot_tn(a, b):
    return lax.dot_general(a, b, (((0,), (0,)), ((), ())), precision=HI, preferred_element_type=f32)


def _head_mask(h, width=GROUP_W):
    lane = lax.broadcasted_iota(jnp.int32, (1, width), 1)
    return ((lane >= h * HEAD_DIM) & (lane < (h + 1) * HEAD_DIM)).astype(f32)


def _rope_perm():
    p = np.zeros((LANES, LANES), np.float32)
    half = MLA_ROPE // 2
    for i in range(half):
        p[MLA_NOPE + half + i, MLA_NOPE + i] = -1.0
        p[MLA_NOPE + i, MLA_NOPE + half + i] = 1.0
    return jnp.asarray(p)


def _rope_tables(S):
    half = MLA_ROPE // 2
    freqs = ROPE_THETA ** (-jnp.arange(half, dtype=f32) / half)
    ang = jnp.arange(S, dtype=f32)[:, None] * freqs[None, :]
    cos, sin = jnp.cos(ang), jnp.sin(ang)
    ones, zeros = jnp.ones((S, MLA_NOPE), f32), jnp.zeros((S, LANES - MLA_DQK), f32)
    c_tab = jnp.concatenate([ones, cos, cos, zeros], axis=1)
    s_tab = jnp.concatenate([jnp.zeros((S, MLA_NOPE), f32), sin, sin, zeros], axis=1)
    return c_tab, s_tab


def _f_mla_pre(c_q, c_kv, krope, c_tab, s_tab, q_norm, kv_norm, wq0, wq1, wq2, wq3, wk0, wk1, wk2, wk3, wv, gq, gk, perm):
    wq, wk = (wq0, wq1, wq2, wq3), (wk0, wk1, wk2, wk3)
    nq = _rms(c_q, q_norm)
    nkv = _rms(c_kv, kv_norm)

    def norm_rope(t, g):
        t = t * lax.rsqrt(jnp.sum(t * t, axis=-1, keepdims=True) * (1.0 / MLA_DQK) + EPS) * g
        return t * c_tab + _hdot(t, perm) * s_tab

    qs = [norm_rope(_bmm(nq, wq[h]), gq) * (MLA_DQK ** -0.5) for h in range(N_HEADS)]
    ks = [norm_rope(_bmm(nkv, wk[h]) + krope, gk) for h in range(N_HEADS)]
    return (*qs, *ks, _bmm(nkv, wv))


def _f_attn(qs, ks, v, q0):
    tq, S = qs[0].shape[0], ks[0].shape[0]
    qpos = q0 + lax.broadcasted_iota(jnp.int32, (tq, S), 0)
    kpos = lax.broadcasted_iota(jnp.int32, (tq, S), 1)
    keep = kpos <= qpos
    logits = [jnp.where(keep, _bmm_nt(qs[h], ks[h]), NEG_INF) for h in range(N_HEADS)]
    ps = [jnp.exp(lg - jnp.max(lg, axis=-1, keepdims=True)) for lg in logits]
    ps = [p / jnp.sum(p, axis=-1, keepdims=True) for p in ps]
    return sum(_bmm(p, v) * _head_mask(h) for h, p in enumerate(ps))


ATTN_PARTS = 4


def _mla_attn_fwd(qs, ks, v, name):
    S = v.shape[0]
    tq = Q_BLOCK
    parts = ATTN_PARTS if S % (ATTN_PARTS * tq) == 0 else 1
    per = S // parts
    outs = []
    for p in range(parts):
        n_keys = (p + 1) * per
        first_block = p * (per // tq)

        def body(*refs, first_block=first_block):
            q_vals = [r[...] for r in refs[:4]]
            k_vals = [r[...] for r in refs[4:8]]
            refs[9][...] = _f_attn(q_vals, k_vals, refs[8][...], (first_block + pl.program_id(0)) * tq)

        qspec = pl.BlockSpec((tq, LANES), lambda i, fb=first_block: (fb + i, 0))
        outs.append(pl.pallas_call(
            body, name=f'{name}_{p}', grid=(per // tq,),
            in_specs=[qspec] * 4 + [pl.BlockSpec((n_keys, LANES), lambda i: (0, 0))] * 4 +
                     [pl.BlockSpec((n_keys, GROUP_W), lambda i: (0, 0))],
            out_specs=pl.BlockSpec((tq, GROUP_W), lambda i: (i, 0)),
            out_shape=jax.ShapeDtypeStruct((per, GROUP_W), f32),
            compiler_params=_cparams(('parallel',), big=True),
        )(*qs, *ks, v))
    return jnp.concatenate(outs, axis=0)


def _mla_attn_bwd(qs, ks, v, do, name):
    S = v.shape[0]
    tq = Q_BLOCK
    parts = ATTN_PARTS if S % (ATTN_PARTS * tq) == 0 else 1
    per = S // parts
    dq_parts, dkv_sum = [], None
    for p in range(parts):
        n_keys = (p + 1) * per
        first_block = p * (per // tq)

        def body(*refs, first_block=first_block):
            q_vals = [r[...].astype(f32) for r in refs[:4]]
            k_vals = [r[...].astype(f32) for r in refs[4:8]]
            v_val = refs[8][...].astype(f32)
            q0 = (first_block + pl.program_id(0)) * tq
            _, vjp = jax.vjp(lambda a, b, c: _f_attn(a, b, c, q0), q_vals, k_vals, v_val)
            dqs, dks, dv = vjp(refs[9][...])
            outs = refs[10:]
            for h in range(N_HEADS):
                outs[h][...] = dqs[h]
            first = pl.program_id(0) == 0
            for o, g in zip(outs[4:], (*dks, dv)):
                @pl.when(first)
                def _(o=o, g=g):
                    o[...] = g

                @pl.when(jnp.logical_not(first))
                def _(o=o, g=g):
                    o[...] += g

        qspec = pl.BlockSpec((tq, LANES), lambda i, fb=first_block: (fb + i, 0))
        kspec = pl.BlockSpec((n_keys, LANES), lambda i: (0, 0))
        vspec = pl.BlockSpec((n_keys, GROUP_W), lambda i: (0, 0))
        res = pl.pallas_call(
            body, name=f'{name}_{p}', grid=(per // tq,),
            in_specs=[qspec] * 4 + [kspec] * 4 + [vspec, pl.BlockSpec((tq, GROUP_W), lambda i, fb=first_block: (fb + i, 0))],
            out_specs=[pl.BlockSpec((tq, LANES), lambda i: (i, 0))] * 4 + [kspec] * 4 + [vspec],
            out_shape=[jax.ShapeDtypeStruct((per, LANES), f32)] * 4 + [jax.ShapeDtypeStruct((n_keys, LANES), f32)] * 4 +
                      [jax.ShapeDtypeStruct((n_keys, GROUP_W), f32)],
            compiler_params=_cparams(('arbitrary',), big=True),
        )(*qs, *ks, v, do)
        dq_parts.append(res[:4])
        dkv = [jnp.pad(t, ((0, S - n_keys), (0, 0))) for t in res[4:]]
        dkv_sum = dkv if dkv_sum is None else [a_ + b_ for a_, b_ in zip(dkv_sum, dkv)]
    dqs = [jnp.concatenate([dq_parts[p][h] for p in range(parts)], axis=0) for h in range(N_HEADS)]
    return dqs, dkv_sum[:4], dkv_sum[4]


def _mla_params(mp):
    pad = LANES - MLA_DQK
    wq = jnp.pad(mp['mla_w_uq'].reshape(GROUP_W, N_HEADS, MLA_DQK).transpose(1, 0, 2), ((0, 0), (0, 0), (0, pad)))
    wkv = mp['mla_w_ukv'].reshape(LANES, N_HEADS, MLA_NOPE + HEAD_DIM)
    wk = jnp.pad(wkv[:, :, :MLA_NOPE].transpose(1, 0, 2), ((0, 0), (0, 0), (0, LANES - MLA_NOPE)))
    wv = wkv[:, :, MLA_NOPE:].reshape(LANES, GROUP_W)
    gq = jnp.pad(mp['mla_qk_q'], (0, pad))[None]
    gk = jnp.pad(mp['mla_qk_k'], (0, pad))[None]
    return [mp['mla_q_norm'][None], mp['mla_kv_norm'][None], *[wq[h] for h in range(N_HEADS)],
            *[wk[h] for h in range(N_HEADS)], wv, gq, gk, _rope_perm()]


def _mla_fwd(c_q, c_kv, k_rope, mp, l):
    S = c_q.shape[0]
    tm = _pick(S, (256, 128))
    krope = jnp.pad(k_rope, ((0, 0), (MLA_NOPE, LANES - MLA_DQK)))
    c_tab, s_tab = _rope_tables(S)
    tiled = [c_q, c_kv, krope, c_tab, s_tab]
    params = _mla_params(mp)
    res = _tile_fwd(_f_mla_pre, tiled, params, [(LANES, bf16)] * 8 + [(GROUP_W, bf16)], tm, f'mla_pre_fwd_{l}')
    qs, ks, v = res[:4], res[4:8], res[8]
    y = _mla_attn_fwd(qs, ks, v, f'mla_attn_fwd_{l}')
    return y, (tiled, params, qs, ks, v)


def _mla_bwd(dy, saved, l):
    tiled, params, qs, ks, v = saved
    S = dy.shape[0]
    tm = _pick(S, (256, 128))
    dqs, dks, dv = _mla_attn_bwd(qs, ks, v, dy, f'mla_attn_bwd_{l}')
    (dc_q, dc_kv, dkrope), dpar = _tile_bwd(_f_mla_pre, tiled, params, [*dqs, *dks, dv], [True, True, True, False, False],
                                            [True] * 13 + [False], tm, f'mla_pre_bwd_{l}')
    dqn, dkvn = dpar[0], dpar[1]
    dwq, dwk = jnp.stack(dpar[2:6]), jnp.stack(dpar[6:10])
    dwv, dgq, dgk = dpar[10:13]
    dw_uq = dwq[:, :, :MLA_DQK].transpose(1, 0, 2).reshape(GROUP_W, N_HEADS * MLA_DQK)
    dw_ukv = jnp.concatenate([dwk[:, :, :MLA_NOPE].transpose(1, 0, 2), dwv.reshape(LANES, N_HEADS, HEAD_DIM)],
                             axis=2).reshape(LANES, N_HEADS * (MLA_NOPE + HEAD_DIM))
    grads = {'mla_q_norm': dqn[0], 'mla_kv_norm': dkvn[0], 'mla_w_uq': dw_uq, 'mla_w_ukv': dw_ukv,
             'mla_qk_q': dgq[0, :MLA_DQK], 'mla_qk_k': dgk[0, :MLA_DQK]}
    return dc_q, dc_kv, dkrope[:, MLA_NOPE:MLA_DQK], grads


SPAN = 128


def _head_mean_matrix():
    h = np.arange(GROUP_W) // HEAD_DIM
    return jnp.asarray((h[:, None] == h[None, :]).astype(np.float32) / HEAD_DIM)


def _f_dil_pre(q, k, gq, gk, hm):
    qn = q * lax.rsqrt(_hdot(q * q, hm) + EPS) * gq * (HEAD_DIM ** -0.5)
    kn = k * lax.rsqrt(_hdot(k * k, hm) + EPS) * gk
    return qn, kn


def _f_dil_branch(qb, kp, kc, vp, vc, b0, b1, b2, b3, first):
    kcat = jnp.concatenate([kp, kc], axis=0)
    vcat = jnp.concatenate([vp, vc], axis=0)
    qi = lax.broadcasted_iota(jnp.int32, (SPAN, 2 * SPAN), 0) + SPAN
    kj = lax.broadcasted_iota(jnp.int32, (SPAN, 2 * SPAN), 1)
    delta = qi - kj
    valid = (delta >= 0) & (delta <= SPAN) & jnp.logical_not(first & (kj < SPAN))
    masks = [_head_mask(h) for h in range(N_HEADS)]
    raw = [_bmm_nt(qb * hm, kcat) for hm in masks]
    logits = [jnp.where(valid, r + bias, NEG_INF) for r, bias in zip(raw, (b0, b1, b2, b3))]
    ms = [jnp.max(lg, axis=-1, keepdims=True) for lg in logits]
    ps = [jnp.exp(lg - m) for lg, m in zip(logits, ms)]
    pvs = [_bmm(p, vcat) for p in ps]
    o = sum(pv * hm for pv, hm in zip(pvs, masks))
    m_full = sum(m * hm for m, hm in zip(ms, masks))
    l_full = sum(jnp.sum(p, axis=-1, keepdims=True) * hm for p, hm in zip(ps, masks))
    return o, m_full, l_full


def _dil_branch_specs(d, nb):
    cur = pl.BlockSpec((SPAN, GROUP_W), lambda r, n: (n, r))
    prev = pl.BlockSpec((SPAN, GROUP_W), lambda r, n: (jnp.maximum(n - 1, 0), r))
    bias = pl.BlockSpec((1, SPAN, 2 * SPAN), lambda r, n: (0, 0, 0))
    return cur, prev, bias


def _head_table_specs():
    return [pl.BlockSpec((1, SPAN, 2 * SPAN), lambda r, n, h=h: (h, 0, 0)) for h in range(N_HEADS)]


def _dil_branch_fwd(q, k, v, table, name):
    L, d = q.shape[0], q.shape[1] // GROUP_W
    nb = L // SPAN
    cur, prev, bias = _dil_branch_specs(d, nb)

    def body(q_ref, kp_ref, kc_ref, vp_ref, vc_ref, b0, b1, b2, b3, o_ref, m_ref, l_ref):
        o, m, l = _f_dil_branch(*[r[...].astype(f32) for r in (q_ref, kp_ref, kc_ref, vp_ref, vc_ref)], b0[0], b1[0], b2[0], b3[0],
                                pl.program_id(1) == 0)
        o_ref[...] = o
        m_ref[...] = m
        l_ref[...] = l

    return pl.pallas_call(
        body, name=name, grid=(d, nb), in_specs=[cur, prev, cur, prev, cur] + _head_table_specs(),
        out_specs=[cur] * 3, out_shape=[jax.ShapeDtypeStruct(q.shape, f32)] * 3,
        compiler_params=_cparams(('parallel', 'parallel')),
    )(q, k, k, v, v, *[table] * N_HEADS)


def _dil_branch_bwd(q, k, v, table, do, dm, dl, name):
    L, d = q.shape[0], q.shape[1] // GROUP_W
    nb = L // SPAN
    cur, prev, bias = _dil_branch_specs(d, nb)
    whole = pl.BlockSpec((L, GROUP_W), lambda r, n: (0, r))

    def body(q_ref, kp_ref, kc_ref, vp_ref, vc_ref, b0, b1, b2, b3, do_ref, dm_ref, dl_ref,
             dq_ref, dk_ref, dv_ref, db0, db1, db2, db3):
        r, n = pl.program_id(0), pl.program_id(1)
        first = n == 0
        _, vjp = jax.vjp(lambda *a: _f_dil_branch(*a, first), *[r[...].astype(f32) for r in (q_ref, kp_ref, kc_ref, vp_ref, vc_ref)],
                         b0[0], b1[0], b2[0], b3[0])
        dq, dkp, dkc, dvp, dvc, g0, g1, g2, g3 = vjp((do_ref[...], dm_ref[...], dl_ref[...]))
        dq_ref[...] = dq

        @pl.when(first)
        def _():
            dk_ref[...] = jnp.zeros_like(dk_ref)
            dv_ref[...] = jnp.zeros_like(dv_ref)

        rows = pl.ds(pl.multiple_of(n * SPAN, SPAN), SPAN)
        dk_ref[rows, :] += dkc
        dv_ref[rows, :] += dvc

        @pl.when(n > 0)
        def _():
            before = pl.ds(pl.multiple_of((n - 1) * SPAN, SPAN), SPAN)
            dk_ref[before, :] += dkp
            dv_ref[before, :] += dvp

        start = first & (r == 0)
        for o, g in zip((db0, db1, db2, db3), (g0, g1, g2, g3)):
            @pl.when(start)
            def _(o=o, g=g):
                o[0] = g

            @pl.when(jnp.logical_not(start))
            def _(o=o, g=g):
                o[0] += g

    res = pl.pallas_call(
        body, name=name, grid=(d, nb), in_specs=[cur, prev, cur, prev, cur] + _head_table_specs() + [cur] * 3,
        out_specs=[cur, whole, whole] + [bias] * 4,
        out_shape=[jax.ShapeDtypeStruct(q.shape, f32)] * 3 + [jax.ShapeDtypeStruct((1, SPAN, 2 * SPAN), f32)] * 4,
        compiler_params=_cparams(('arbitrary', 'arbitrary')),
    )(q, k, k, v, v, *[table] * N_HEADS, do, dm, dl)
    return res[0], res[1], res[2], res[3:]


def _f_dil_merge(o1, m1, l1, o2, m2, l2, o3, m3, l3):
    mx = jnp.maximum(jnp.maximum(m1, m2), m3)
    w1, w2, w3 = jnp.exp(m1 - mx), jnp.exp(m2 - mx), jnp.exp(m3 - mx)
    return ((w1 * o1 + w2 * o2 + w3 * o3) / (w1 * l1 + w2 * l2 + w3 * l3),)


def _bias_onehot(dilation):
    qi = jnp.arange(SPAN, dtype=jnp.int32)[:, None] + SPAN
    kj = jnp.arange(2 * SPAN, dtype=jnp.int32)[None, :]
    bucket = _t5_bucket(jnp.clip(qi - kj, 0, SPAN) * dilation).reshape(-1)
    return (bucket[None, :] == jnp.arange(T5_BUCKETS, dtype=jnp.int32)[:, None]).astype(f32)


def _bias_tables(t5_t, onehot, name):
    N = onehot.shape[1]
    tn = _pick(N, (4096, 2048, 1024))

    def body(t_ref, oh_ref, o_ref):
        o_ref[...] = _hdot(t_ref[...], oh_ref[...])

    return pl.pallas_call(
        body, name=name, grid=(N // tn,),
        in_specs=[pl.BlockSpec((8, T5_BUCKETS), lambda i: (0, 0)), pl.BlockSpec((T5_BUCKETS, tn), lambda i: (0, i))],
        out_specs=pl.BlockSpec((8, tn), lambda i: (0, i)), out_shape=jax.ShapeDtypeStruct((8, N), f32),
        compiler_params=_cparams(('parallel',)),
    )(t5_t, onehot)


def _bias_tables_bwd(d_tab, onehot, name):
    N = onehot.shape[1]
    tn = _pick(N, (4096, 2048, 1024))

    def body(g_ref, oh_ref, o_ref):
        part = _hdot_nt(g_ref[...], oh_ref[...])

        @pl.when(pl.program_id(0) == 0)
        def _():
            o_ref[...] = part

        @pl.when(pl.program_id(0) > 0)
        def _():
            o_ref[...] += part

    return pl.pallas_call(
        body, name=name, grid=(N // tn,),
        in_specs=[pl.BlockSpec((8, tn), lambda i: (0, i)), pl.BlockSpec((T5_BUCKETS, tn), lambda i: (0, i))],
        out_specs=pl.BlockSpec((8, T5_BUCKETS), lambda i: (0, 0)), out_shape=jax.ShapeDtypeStruct((8, T5_BUCKETS), f32),
        compiler_params=_cparams(('arbitrary',)),
    )(d_tab, onehot)


def _by_residue(t, d):
    S, C = t.shape
    return t.reshape(S // d, d * C)


def _from_residue(t):
    return t.reshape(-1, GROUP_W)


def _dil_tables(t5_bias):
    t5_t = jnp.pad(t5_bias.T, ((0, 8 - N_HEADS), (0, 0)))
    return [_bias_tables(t5_t, _bias_onehot(d), f'dil_bias_fwd_{bi}').reshape(8, SPAN, 2 * SPAN)
            for bi, (_, d) in enumerate(DIL_PAIRS)]


def _t5_grad(d_tables):
    total = None
    for bi, (_, d) in enumerate(DIL_PAIRS):
        g = _bias_tables_bwd(d_tables[bi], _bias_onehot(d), f'dil_bias_bwd_{bi}')
        total = g if total is None else total + g
    return total[:N_HEADS].T


def _dil_fwd(qkv, mp, l):
    S = qkv.shape[0]
    tm = _pick(S, (256, 128))
    q, k, v = qkv[:, :GROUP_W], qkv[:, GROUP_W:2 * GROUP_W], qkv[:, 2 * GROUP_W:]
    pre_params = [jnp.tile(mp['dil_q_norm'], N_HEADS)[None], jnp.tile(mp['dil_k_norm'], N_HEADS)[None], _head_mean_matrix()]
    qn, kn = _tile_fwd(_f_dil_pre, [q, k], pre_params, [(GROUP_W, bf16)] * 2, tm, f'dil_pre_fwd_{l}')
    v = v.astype(bf16)
    tables = mp['dil_tables'] if 'dil_tables' in mp else _dil_tables(mp['t5_bias'])
    branches, outs = [], []
    for bi, (_, d) in enumerate(DIL_PAIRS):
        tab = tables[bi]
        qd, kd, vd = _by_residue(qn, d), _by_residue(kn, d), _by_residue(v, d)
        o, m, lsum = _dil_branch_fwd(qd, kd, vd, tab, f'dil_branch_fwd_{l}_{bi}')
        branches.append((qd, kd, vd, tab))
        outs += [_from_residue(o), _from_residue(m), _from_residue(lsum)]
    (y,) = _tile_fwd(_f_dil_merge, outs, [], [(GROUP_W, f32)], tm, f'dil_merge_fwd_{l}')
    return y, (q, k, pre_params, branches, outs)


def _dil_bwd(dy, saved, l):
    q, k, pre_params, branches, outs = saved
    S = dy.shape[0]
    tm = _pick(S, (256, 128))
    douts, _ = _tile_bwd(_f_dil_merge, outs, [], [dy], [True] * 9, [], tm, f'dil_merge_bwd_{l}')
    dqn = dkn = dv = None
    d_tabs = []
    for bi, (_, d) in enumerate(DIL_PAIRS):
        qd, kd, vd, tab = branches[bi]
        do, dm, dl = [_by_residue(t, d) for t in douts[3 * bi:3 * bi + 3]]
        dq_b, dk_b, dv_b, dbias = _dil_branch_bwd(qd, kd, vd, tab, do, dm, dl, f'dil_branch_bwd_{l}_{bi}')
        d_tabs.append(jnp.concatenate([*dbias, jnp.zeros((8 - N_HEADS, SPAN, 2 * SPAN), f32)], axis=0).reshape(8, -1))
        dq_b, dk_b, dv_b = _from_residue(dq_b), _from_residue(dk_b), _from_residue(dv_b)
        dqn = dq_b if dqn is None else dqn + dq_b
        dkn = dk_b if dkn is None else dkn + dk_b
        dv = dv_b if dv is None else dv + dv_b
    (dq, dk), (dgq, dgk) = _tile_bwd(_f_dil_pre, [q, k], pre_params, [dqn, dkn], [True, True], [True, True, False], tm,
                                     f'dil_pre_bwd_{l}')
    grads = {'dil_q_norm': dgq.reshape(N_HEADS, HEAD_DIM).sum(0), 'dil_k_norm': dgk.reshape(N_HEADS, HEAD_DIM).sum(0),
             't5_tables': d_tabs}
    return jnp.concatenate([dq, dk, dv], axis=1), grads


S5_LANES = S5_G * S5_P
SCAN_SEGMENTS = 8
SCAN_W = 256


def _f_s5_prep(bre, bim, lr, li, logdt_col, expand):
    dt = jnp.sum(jnp.exp(logdt_col) * expand, axis=0, keepdims=True)
    mag = jnp.exp(lr * dt)
    ar, ai = mag * jnp.cos(li * dt), mag * jnp.sin(li * dt)
    den = lr * lr + li * li
    nr, ni = ar - 1.0, ai
    zr = (nr * lr + ni * li) / den
    zi = (ni * lr - nr * li) / den
    bb = jnp.concatenate([zr * bre - zi * bim, zr * bim + zi * bre], axis=1)
    a_rows = jnp.broadcast_to(jnp.concatenate([ar, ai], axis=1), bb.shape)
    return bb, a_rows


def _s5_scan(x, a_rows, name, reverse=False, h=None):
    S = x.shape[0]
    NL = x.shape[1] // 2
    T = S // SCAN_SEGMENTS
    nblk = NL // SCAN_W
    n_in = 4 if reverse else 2

    def body(*refs):
        if reverse:
            (x_hbm, pr_hbm, pi_hbm, ar_ref, ai_ref, hr_hbm, hi_hbm, dar_ref, dai_ref,
             xr_s, xi_s, pr_s, pi_s, hr_s, hi_s, in_sems, out_sems) = refs
        else:
            x_hbm, ar_ref, ai_ref, hr_hbm, hi_hbm, xr_s, xi_s, hr_s, hi_s, in_sems, out_sems = refs
        col = pl.multiple_of(pl.program_id(0) * SCAN_W, SCAN_W)
        loads = []
        for k in range(SCAN_SEGMENTS):
            rows = pl.ds(k * T, T)
            sources = [(x_hbm, col, xr_s), (x_hbm, NL + col, xi_s)]
            if reverse:
                sources += [(pr_hbm, col, pr_s), (pi_hbm, col, pi_s)]
            for i, (src, c0, dst) in enumerate(sources):
                loads.append(pltpu.make_async_copy(src.at[rows, pl.ds(c0, SCAN_W)], dst.at[:, k, :],
                                                   in_sems.at[i * SCAN_SEGMENTS + k]))
        for cp in loads:
            cp.start()
        for cp in loads:
            cp.wait()
        ar = ar_ref[...]
        ai = -ai_ref[...] if reverse else ai_ref[...]
        zero = jnp.zeros((SCAN_SEGMENTS, SCAN_W), f32)

        def at(s):
            return T - 1 - s if reverse else s

        def local(s, c):
            hr, hi, pr, pi = c
            j = at(s)
            nhr = ar * hr - ai * hi + xr_s[j]
            nhi = ar * hi + ai * hr + xi_s[j]
            hr_s[j] = nhr
            hi_s[j] = nhi
            return nhr, nhi, ar * pr - ai * pi, ar * pi + ai * pr

        er, ei, pr, pi = lax.fori_loop(0, T, local, (zero, zero, zero + 1.0, zero), unroll=2)
        row = lax.broadcasted_iota(jnp.int32, (SCAN_SEGMENTS, SCAN_W), 0)
        cr, ci = zero, zero
        order = range(SCAN_SEGMENTS - 2, -1, -1) if reverse else range(1, SCAN_SEGMENTS)
        for k in order:
            src = k + 1 if reverse else k - 1
            tr = er + pr * cr - pi * ci
            ti = ei + pr * ci + pi * cr
            cr = jnp.where(row == k, jnp.sum(jnp.where(row == src, tr, 0.0), axis=0, keepdims=True), cr)
            ci = jnp.where(row == k, jnp.sum(jnp.where(row == src, ti, 0.0), axis=0, keepdims=True), ci)

        def fix_at(j, c, before):
            pr, pi, sr, si = c
            pr, pi = ar * pr - ai * pi, ar * pi + ai * pr
            hr = hr_s[j] + pr * cr - pi * ci
            hi = hi_s[j] + pr * ci + pi * cr
            hr_s[j] = hr
            hi_s[j] = hi
            if reverse:
                qr, qi = before
                sr = sr + hr * qr + hi * qi
                si = si + hi * qr - hr * qi
            return pr, pi, sr, si

        start = (zero + 1.0, zero, zero, zero)
        if reverse:
            def fix(s, c):
                j = T - 1 - s
                return fix_at(j, c, (pr_s[j - 1], pi_s[j - 1]))

            c = lax.fori_loop(0, T - 1, fix, start, unroll=2)
            last_r = jnp.where(row == 0, 0.0, pltpu.roll(pr_s[T - 1], 1, 0))
            last_i = jnp.where(row == 0, 0.0, pltpu.roll(pi_s[T - 1], 1, 0))
            _, _, sr, si = fix_at(0, c, (last_r, last_i))
            dar_ref[...] = sr
            dai_ref[...] = si
        else:
            lax.fori_loop(0, T, lambda s, c: fix_at(s, c, None), start, unroll=2)
        stores = []
        for k in range(SCAN_SEGMENTS):
            rows = pl.ds(k * T, T)
            stores.append(pltpu.make_async_copy(hr_s.at[:, k, :], hr_hbm.at[rows, pl.ds(col, SCAN_W)], out_sems.at[k]))
            stores.append(pltpu.make_async_copy(hi_s.at[:, k, :], hi_hbm.at[rows, pl.ds(col, SCAN_W)],
                                                out_sems.at[SCAN_SEGMENTS + k]))
        for cp in stores:
            cp.start()
        for cp in stores:
            cp.wait()

    a_re = pl.BlockSpec((SCAN_SEGMENTS, SCAN_W), lambda b: (0, b))
    a_im = pl.BlockSpec((SCAN_SEGMENTS, SCAN_W), lambda b: (0, nblk + b))
    seq = pltpu.VMEM((T, SCAN_SEGMENTS, SCAN_W), f32)
    if reverse:
        in_specs, args = [ANY, ANY, ANY, a_re, a_im], [x, h[0], h[1], a_rows, a_rows]
        out_specs = [ANY, ANY, a_re, a_re]
        out_shape = [jax.ShapeDtypeStruct((S, NL), f32)] * 2 + [jax.ShapeDtypeStruct((SCAN_SEGMENTS, NL), f32)] * 2
    else:
        in_specs, args = [ANY, a_re, a_im], [x, a_rows, a_rows]
        out_specs = [ANY, ANY]
        out_shape = [jax.ShapeDtypeStruct((S, NL), f32)] * 2
    scratch = [seq] * (n_in + 2) + [pltpu.SemaphoreType.DMA((n_in * SCAN_SEGMENTS,)),
                                    pltpu.SemaphoreType.DMA((2 * SCAN_SEGMENTS,))]
    return pl.pallas_call(body, name=name, grid=(nblk,), in_specs=in_specs, out_specs=out_specs, out_shape=out_shape,
                          scratch_shapes=scratch, compiler_params=_cparams(('arbitrary',), big=True))(*args)


def _f_s5_post(y, u, d, w_glu):
    z = _bmm(y + d * u, w_glu)
    return (z[:, :GROUP_W] * jax.nn.sigmoid(z[:, GROUP_W:]),)


def _block_diag(t):
    G, a, b = t.shape
    eye = jnp.eye(G, dtype=t.dtype)
    return (t[:, :, None, :] * eye[:, None, :, None]).reshape(G * a, G * b)


def _diag_blocks(m, a, b):
    G = m.shape[0] // a
    return jnp.moveaxis(jnp.diagonal(m.reshape(G, a, G, b), axis1=0, axis2=2), -1, 0)


def _s5_fwd(u, mp, l):
    S = u.shape[0]
    tm = _pick(S, (256, 128))
    bre = _block_diag(mp['s5_b_re'].transpose(0, 2, 1))
    bim = _block_diag(mp['s5_b_im'].transpose(0, 2, 1))
    expand = jnp.repeat(jnp.eye(S5_G, dtype=f32), S5_P, axis=1)
    prep_params = [mp['s5_lambda_re'].reshape(1, S5_LANES), mp['s5_lambda_im'].reshape(1, S5_LANES),
                   mp['s5_log_dt'].reshape(S5_G, 1), expand]
    bb, a_rows = _tile_fwd(_f_s5_prep, [bre, bim], prep_params, [(2 * S5_LANES, f32)] * 2, GROUP_W, f's5_prep_fwd_{l}')
    x = _mm(u, bb, 'nn', f's5_in_fwd_{l}')
    hr, hi = _s5_scan(x, a_rows, f's5_scan_fwd_{l}')
    c_re, c_im = _block_diag(mp['s5_c_re'].transpose(0, 2, 1)), -_block_diag(mp['s5_c_im'].transpose(0, 2, 1))
    y = _mm(hi, c_im, 'nn', f's5_out_im_fwd_{l}', add=_mm(hr, c_re, 'nn', f's5_out_re_fwd_{l}'))
    post_params = [mp['s5_d'][None], mp['s5_w_glu']]
    (out,) = _tile_fwd(_f_s5_post, [y, u], post_params, [(GROUP_W, f32)], tm, f's5_post_fwd_{l}')
    return out, (u, bre, bim, prep_params, bb, a_rows, hr, hi, c_re, c_im, y, post_params)


def _s5_bwd(dout, saved, l):
    u, bre, bim, prep_params, bb, a_rows, hr, hi, c_re, c_im, y, post_params = saved
    S = u.shape[0]
    tm = _pick(S, (256, 128))
    (dy, du1), (dd, dwglu) = _tile_bwd(_f_s5_post, [y, u], post_params, [dout], [True, True], [True, True], tm,
                                       f's5_post_bwd_{l}')
    ccat = jnp.concatenate([c_re, c_im], axis=0)
    dh = _mm(dy, ccat, 'nt', f's5_out_dx_{l}')
    dccat = jnp.concatenate([_mm(hr, dy, 'tn', f's5_out_re_dw_{l}'), _mm(hi, dy, 'tn', f's5_out_im_dw_{l}')], axis=0)
    lr_, li_, dar, dai = _s5_scan(dh, a_rows, f's5_scan_bwd_{l}', reverse=True, h=(hr, hi))
    du2 = _mm(li_, bb[:, S5_LANES:], 'nt', f's5_in_im_dx_{l}', add=_mm(lr_, bb[:, :S5_LANES], 'nt', f's5_in_re_dx_{l}'))
    dbb = jnp.concatenate([_mm(u, lr_, 'tn', f's5_in_re_dw_{l}'), _mm(u, li_, 'tn', f's5_in_im_dw_{l}')], axis=1)
    da_rows = jnp.pad(jnp.concatenate([dar, dai], axis=1), ((0, GROUP_W - SCAN_SEGMENTS), (0, 0)))
    (dbre, dbim), (dlr, dli, dlogdt) = _tile_bwd(_f_s5_prep, [bre, bim], prep_params, [dbb, da_rows], [True, True],
                                                 [True, True, True, False], GROUP_W, f's5_prep_bwd_{l}')
    grads = {
        's5_lambda_re': dlr.reshape(S5_G, S5_P), 's5_lambda_im': dli.reshape(S5_G, S5_P), 's5_log_dt': dlogdt[:, 0],
        's5_b_re': _diag_blocks(dbre, S5_CG, S5_P).transpose(0, 2, 1),
        's5_b_im': _diag_blocks(dbim, S5_CG, S5_P).transpose(0, 2, 1),
        's5_c_re': _diag_blocks(dccat[:S5_LANES], S5_P, S5_CG).transpose(0, 2, 1),
        's5_c_im': -_diag_blocks(dccat[S5_LANES:], S5_P, S5_CG).transpose(0, 2, 1),
        's5_d': dd[0], 's5_w_glu': dwglu}
    return du1 + du2, grads


DN_CONV = 4


def _head_sum_matrix():
    h = np.arange(GROUP_W) // HEAD_DIM
    return jnp.asarray((h[:, None] == h[None, :]).astype(np.float32))


def _f_dn_pre(x0, x1, x2, x3, ab, w0, w1, w2, w3, alog, dtb, ea, eb, hs):
    c = w0 * x0 + w1 * x1 + w2 * x2 + w3 * x3
    s = c * jax.nn.sigmoid(c)
    q, k, v = s[:, :GROUP_W], s[:, GROUP_W:2 * GROUP_W], s[:, 2 * GROUP_W:]
    q = q * lax.rsqrt(_hdot(q * q, hs) + EPS) * (HEAD_DIM ** -0.5)
    k = k * lax.rsqrt(_hdot(k * k, hs) + EPS)
    beta = jax.nn.sigmoid(_hdot(ab, eb))
    g = -jnp.exp(alog) * jax.nn.softplus(_hdot(ab, ea) + dtb)
    return q, k, v, g, beta


DN_CHUNKS_PER_STEP = 4


def _f_dn_chunks(q, k, v, g, beta):
    C = DN_CHUNK
    n_chunks = q.shape[0] // C
    r = lax.broadcasted_iota(jnp.int32, (C, C), 0)
    c = lax.broadcasted_iota(jnp.int32, (C, C), 1)
    causal, strict = r >= c, r > c
    eye = (r == c).astype(f32)
    tril = causal.astype(f32)
    ones = jnp.ones((C, GROUP_W), f32)
    masks = [_head_mask(h) for h in range(N_HEADS)]
    rows = [tuple(t[i * C:(i + 1) * C] for t in (q, k, v, g, beta)) for i in range(n_chunks)]
    gcs = [_hdot(tril, gi) for (_, _, _, gi, _) in rows]
    items = [(i, h) for i in range(n_chunks) for h in range(N_HEADS)]
    grows = [_hdot_nt(ones * (masks[h] * (1.0 / HEAD_DIM)), gcs[i]) for i, h in items]
    decs = []
    for (i, h), grow in zip(items, grows):
        gcol = jnp.sum(gcs[i] * masks[h], axis=1, keepdims=True) * (1.0 / HEAD_DIM)
        decs.append(jnp.exp(jnp.where(causal, gcol - grow, NEG_INF)))
    kbs = [ki * bi for (_, ki, _, _, bi) in rows]
    kks = [_bmm_nt(kbs[i] * masks[h], rows[i][1]) for i, h in items]
    qks = [_bmm_nt(rows[i][0] * masks[h], rows[i][1]) for i, h in items]
    lmats = [jnp.where(strict, kk * dec, 0.0) for kk, dec in zip(kks, decs)]
    a_qk = [jnp.where(causal, qk * dec, 0.0) for qk, dec in zip(qks, decs)]
    ts = [eye - lm for lm in lmats]
    ps = lmats
    for _ in range(5):
        ps = [_bmm(p, p) for p in ps]
        ts = [t + _bmm(t, p) for t, p in zip(ts, ps)]
    egs = [jnp.exp(gc) for gc in gcs]
    tw = [_bmm(t, kbs[i] * egs[i]) for (i, h), t in zip(items, ts)]
    tu = [_bmm(t, rows[i][2] * rows[i][4]) for (i, h), t in zip(items, ts)]
    outs = []
    for i in range(n_chunks):
        qi, ki, _, gi, _ = rows[i]
        glast = jnp.sum(gi, axis=0, keepdims=True)
        w = sum(tw[i * N_HEADS + h] * masks[h] for h in range(N_HEADS))
        u = sum(tu[i * N_HEADS + h] * masks[h] for h in range(N_HEADS))
        outs.append((w, u, qi * egs[i], ki * jnp.exp(glast - gcs[i]), *a_qk[i * N_HEADS:(i + 1) * N_HEADS],
                     jnp.broadcast_to(jnp.exp(glast), (C, GROUP_W))))
    return tuple(jnp.concatenate(parts, axis=0) for parts in zip(*outs))


def _f_dn_step(w, u, qd, kdec, a0, a1, a2, a3, dfull, state, bd):
    row0 = (lax.broadcasted_iota(jnp.int32, dfull.shape, 0) == 0).astype(f32)
    dvec = jnp.sum(dfull * row0, axis=0, keepdims=True)
    ws, qs = _bmm(w, state), _bmm(qd, state)
    vnew = u - ws
    avs = [_bmm(a, vnew) for a in (a0, a1, a2, a3)]
    kv = _bmm_tn(kdec, vnew)
    o = qs + sum(av * _head_mask(h) for h, av in enumerate(avs))
    return o, state * dvec + bd * kv


def _dn_scan_fwd(ins, name):
    S = ins[0].shape[0]
    N = S // DN_CHUNK
    bd = _head_sum_matrix()

    def body(*refs):
        o_ref, s_ref, state = refs[10], refs[11], refs[12]

        @pl.when(pl.program_id(0) == 0)
        def _():
            state[...] = jnp.zeros_like(state)

        s_in = state[...]
        s_ref[0] = s_in
        o, s_out = _f_dn_step(*[r[...] for r in refs[:9]], s_in, refs[9][...])
        o_ref[...] = o
        state[...] = s_out

    return pl.pallas_call(
        body, name=name, grid=(N,),
        in_specs=[pl.BlockSpec((DN_CHUNK, t.shape[1]), lambda n: (n, 0)) for t in ins] + [_full_spec(bd)],
        out_specs=[pl.BlockSpec((DN_CHUNK, GROUP_W), lambda n: (n, 0)), pl.BlockSpec((1, GROUP_W, GROUP_W), lambda n: (n, 0, 0))],
        out_shape=[jax.ShapeDtypeStruct((S, GROUP_W), f32), jax.ShapeDtypeStruct((N, GROUP_W, GROUP_W), f32)],
        scratch_shapes=[pltpu.VMEM((GROUP_W, GROUP_W), f32)],
        compiler_params=_cparams(('arbitrary',)),
    )(*ins, bd)


def _dn_scan_bwd(ins, states, do, name):
    S = ins[0].shape[0]
    N = S // DN_CHUNK
    bd = _head_sum_matrix()

    def body(*refs):
        s_ref, do_ref = refs[9], refs[10]
        bd_ref = refs[11]
        outs = refs[12:21]
        dstate = refs[21]

        @pl.when(pl.program_id(0) == 0)
        def _():
            dstate[...] = jnp.zeros_like(dstate)

        bd_val = bd_ref[...]
        _, vjp = jax.vjp(lambda *a: _f_dn_step(*a, bd_val), *[r[...] for r in refs[:9]], s_ref[0])
        grads = vjp((do_ref[...], dstate[...]))
        for o, g in zip(outs, grads[:9]):
            o[...] = g
        dstate[...] = grads[9]

    def rev(n):
        return (N - 1 - n, 0)

    res = pl.pallas_call(
        body, name=name, grid=(N,),
        in_specs=[pl.BlockSpec((DN_CHUNK, t.shape[1]), rev) for t in ins] +
                 [pl.BlockSpec((1, GROUP_W, GROUP_W), lambda n: (N - 1 - n, 0, 0)), pl.BlockSpec((DN_CHUNK, GROUP_W), rev),
                  _full_spec(bd)],
        out_specs=[pl.BlockSpec((DN_CHUNK, t.shape[1]), rev) for t in ins],
        out_shape=[jax.ShapeDtypeStruct(t.shape, f32) for t in ins],
        scratch_shapes=[pltpu.VMEM((GROUP_W, GROUP_W), f32)],
        compiler_params=_cparams(('arbitrary',)),
    )(*ins, states, do, bd)
    return list(res)


def _f_dn_post(o, gate, gain, hmean):
    return (o * lax.rsqrt(_hdot(o * o, hmean) + EPS) * gain * (gate * jax.nn.sigmoid(gate)),)


def _dn_delays(x, name):
    S, C = x.shape
    tm = _pick(S, (256, 128))

    def body(prev_ref, cur_ref, *outs):
        before = jnp.where(pl.program_id(0) > 0, prev_ref[...], 0.0)
        both = jnp.concatenate([before, cur_ref[...]], axis=0)
        for o, k in zip(outs, range(DN_CONV - 1, 0, -1)):
            o[...] = pltpu.roll(both, k, 0)[tm:]

    spec = pl.BlockSpec((tm, C), lambda i: (i, 0))
    return pl.pallas_call(
        body, name=name, grid=(S // tm,),
        in_specs=[pl.BlockSpec((tm, C), lambda i: (jnp.maximum(i - 1, 0), 0)), spec],
        out_specs=[spec] * (DN_CONV - 1), out_shape=[jax.ShapeDtypeStruct((S, C), x.dtype)] * (DN_CONV - 1),
        compiler_params=_cparams(('parallel',), big=True),
    )(x, x)


def _dn_undelay_sum(ds, name):
    S, C = ds[0].shape
    tm = _pick(S, (256, 128))
    n = S // tm

    def body(*refs):
        o = refs[-1]
        total = refs[2 * (DN_CONV - 1)][...]
        for j in range(DN_CONV - 1):
            k = DN_CONV - 1 - j
            after = jnp.where(pl.program_id(0) < n - 1, refs[2 * j + 1][...], 0.0)
            both = jnp.concatenate([refs[2 * j][...], after], axis=0)
            total = total + pltpu.roll(both, 2 * tm - k, 0)[:tm]
        o[...] = total

    spec = pl.BlockSpec((tm, C), lambda i: (i, 0))
    nxt = pl.BlockSpec((tm, C), lambda i: (jnp.minimum(i + 1, n - 1), 0))
    args, in_specs = [], []
    for j in range(DN_CONV - 1):
        args += [ds[j], ds[j]]
        in_specs += [spec, nxt]
    return pl.pallas_call(
        body, name=name, grid=(n,), in_specs=in_specs + [spec], out_specs=spec,
        out_shape=jax.ShapeDtypeStruct((S, C), f32), compiler_params=_cparams(('parallel',), big=True),
    )(*args, ds[DN_CONV - 1])


def _dn_fwd(qkv, a, b, gate, mp, l):
    S = qkv.shape[0]
    tm = _pick(S, (256, 128))
    xs = [*_dn_delays(qkv, f'dn_delay_{l}'), qkv]
    ab = jnp.pad(jnp.concatenate([a, b], axis=1), ((0, 0), (0, LANES - 2 * N_HEADS)))
    sel = np.zeros((2, LANES, GROUP_W), np.float32)
    for h in range(N_HEADS):
        sel[0, h, h * HEAD_DIM:(h + 1) * HEAD_DIM] = 1.0
        sel[1, N_HEADS + h, h * HEAD_DIM:(h + 1) * HEAD_DIM] = 1.0
    pre_params = [*[mp['dn_conv'][j][None] for j in range(DN_CONV)], jnp.repeat(mp['dn_a_log'], HEAD_DIM)[None],
                  jnp.repeat(mp['dn_dt_bias'], HEAD_DIM)[None], jnp.asarray(sel[0]), jnp.asarray(sel[1]), _head_sum_matrix()]
    pre = _tile_fwd(_f_dn_pre, [*xs, ab], pre_params, [(GROUP_W, f32)] * 5, tm, f'dn_pre_fwd_{l}')
    chunk_outs = [(GROUP_W, f32)] * 4 + [(HEAD_DIM, f32)] * 4 + [(GROUP_W, f32)]
    parts = _tile_fwd(_f_dn_chunks, pre, [], chunk_outs, DN_CHUNK * DN_CHUNKS_PER_STEP, f'dn_chunk_fwd_{l}')
    o, states = _dn_scan_fwd(parts, f'dn_scan_fwd_{l}')
    post_params = [jnp.tile(mp['dn_o_norm'], N_HEADS)[None], _head_mean_matrix()]
    (y,) = _tile_fwd(_f_dn_post, [o, gate], post_params, [(GROUP_W, f32)], tm, f'dn_post_fwd_{l}')
    return y, (xs, ab, pre_params, pre, parts, states, o, gate, post_params)


def _dn_bwd(dy, saved, l):
    xs, ab, pre_params, pre, parts, states, o, gate, post_params = saved
    S = dy.shape[0]
    tm = _pick(S, (256, 128))
    (do, dgate), (dgain,) = _tile_bwd(_f_dn_post, [o, gate], post_params, [dy], [True, True], [True, False], tm,
                                      f'dn_post_bwd_{l}')
    dparts = _dn_scan_bwd(parts, states, do, f'dn_scan_bwd_{l}')
    dpre, _ = _tile_bwd(_f_dn_chunks, pre, [], dparts, [True] * 5, [], DN_CHUNK * DN_CHUNKS_PER_STEP, f'dn_chunk_bwd_{l}')
    dins, dpar = _tile_bwd(_f_dn_pre, [*xs, ab], pre_params, dpre, [True] * 5, [True] * 6 + [False] * 3, tm,
                           f'dn_pre_bwd_{l}')
    dqkv = _dn_undelay_sum(dins[:DN_CONV], f'dn_undelay_{l}')
    dab = dins[DN_CONV]
    grads = {'dn_conv': jnp.concatenate(dpar[:DN_CONV], axis=0),
             'dn_a_log': dpar[4].reshape(N_HEADS, HEAD_DIM).sum(1), 'dn_dt_bias': dpar[5].reshape(N_HEADS, HEAD_DIM).sum(1),
             'dn_o_norm': dgain.reshape(N_HEADS, HEAD_DIM).sum(0)}
    return dqkv, dab[:, :N_HEADS], dab[:, N_HEADS:2 * N_HEADS], dgate, grads


def _t5_bucket(dist):
    exact = T5_BUCKETS // 2
    df = jnp.maximum(dist, 1).astype(f32)
    large = exact + (jnp.log(df / exact) / math.log(T5_MAX_DIST / exact) * (T5_BUCKETS - exact)).astype(jnp.int32)
    large = jnp.minimum(large, T5_BUCKETS - 1)
    return jnp.where(dist < exact, dist, large)


def _split_cols(t, sizes):
    out, start = [], 0
    for s in sizes:
        out.append(t[..., start:start + s])
        start += s
    return out


def _mixers_fwd(proj, mp, l):
    c_q, c_kv, k_rope, u_s5, qkv_dil, qkv_dn, a_dn, b_dn, gate_dn = _split_cols(proj, IN_SPLITS)
    y_mla, s_mla = _mla_fwd(c_q, c_kv, k_rope, mp, l)
    y_s5, s_s5 = _s5_fwd(u_s5, mp, l)
    y_dil, s_dil = _dil_fwd(qkv_dil, mp, l)
    y_dn, s_dn = _dn_fwd(qkv_dn, a_dn, b_dn, gate_dn, mp, l)
    return jnp.concatenate([y_mla, y_s5, y_dil, y_dn], axis=-1), (s_mla, s_s5, s_dil, s_dn)


def _mixers_bwd(dmixed, saved, l):
    s_mla, s_s5, s_dil, s_dn = saved
    d_mla, d_s5, d_dil, d_dn = _split_cols(dmixed, (GROUP_W,) * 4)
    dc_q, dc_kv, dk_rope, g_mla = _mla_bwd(d_mla, s_mla, l)
    du, g_s5 = _s5_bwd(d_s5, s_s5, l)
    dqkv_dil, g_dil = _dil_bwd(d_dil, s_dil, l)
    dqkv_dn, da, db, dgate, g_dn = _dn_bwd(d_dn, s_dn, l)
    parts = [dc_q, dc_kv, dk_rope, du, dqkv_dil, dqkv_dn, da, db, dgate]
    dproj = jnp.concatenate([p.astype(bf16) for p in parts], axis=-1)
    return dproj, {**g_mla, **g_s5, **g_dil, **g_dn}


MIXER_PARAMS = ['mla_q_norm', 'mla_kv_norm', 'mla_w_uq', 'mla_w_ukv', 'mla_qk_q', 'mla_qk_k', 's5_lambda_re',
                's5_lambda_im', 's5_log_dt', 's5_b_re', 's5_b_im', 's5_c_re', 's5_c_im', 's5_d', 's5_w_glu',
                'dil_q_norm', 'dil_k_norm', 't5_bias', 'dn_conv', 'dn_a_log', 'dn_dt_bias', 'dn_o_norm']


def _layer_fwd_mix(h, W, l):
    S = h.shape[0]
    tm = _pick(S, (256, 128))
    g1 = W['attn_norm'][l][None]
    (n1,) = _tile_fwd(_f_rms, [h], [g1], [(D_MODEL, bf16)], tm, f'rms1_fwd_{l}')
    proj = _mm(n1, W['w_in'][l], 'nn', f'proj_fwd_{l}')
    mp = {k: (W[k] if k == 't5_bias' else W[k][l]).astype(f32) for k in MIXER_PARAMS}
    if 'dil_tables' in W:
        mp['dil_tables'] = W['dil_tables']
    mixed, mix_saved = _mixers_fwd(proj, mp, l)
    mixed_b = mixed.astype(bf16)
    h2 = _mm(mixed_b, W['w_out'][l], 'nn', f'out_fwd_{l}', add=h)
    return h2, dict(h=h, n1=n1, mix=mix_saved, mixed=mixed_b, h2=h2)


def _layer_fwd_ffn(h2, W, l, saved):
    S = h2.shape[0]
    tm = _pick(S, (256, 128))
    g2 = W['ffn_norm'][l][None]
    (n2,) = _tile_fwd(_f_rms, [h2], [g2], [(D_MODEL, bf16)], tm, f'rms2_fwd_{l}')
    u = _mm(n2, W['ffn_w1'][l], 'nn', f'ffn1_fwd_{l}')
    v = _mm(n2, W['ffn_w3'][l], 'nn', f'ffn3_fwd_{l}')
    (act,) = _tile_fwd(_f_swiglu, [u, v], [], [(FFN_HIDDEN, bf16)], tm, f'swiglu_fwd_{l}')
    h3 = _mm(act, W['ffn_w2'][l], 'nn', f'ffn2_fwd_{l}', add=h2)
    saved.update(n2=n2, u=u, v=v, act=act)
    return h3


def _layer_bwd_ffn(dh3, saved, W, l):
    S = dh3.shape[0]
    tm = _pick(S, (256, 128))
    g2 = W['ffn_norm'][l][None]
    grads = {}
    dact = _mm(dh3, W['ffn_w2'][l], 'nt', f'ffn2_dx_{l}')
    grads['ffn_w2'] = _mm(saved['act'], dh3, 'tn', f'ffn2_dw_{l}', out_dtype=bf16)
    (du, dv), _ = _tile_bwd(_f_swiglu, [saved['u'], saved['v']], [], [dact], [True, True], [], tm, f'swiglu_bwd_{l}',
                            dt_dtypes=[bf16, bf16])
    dn2 = _mm(dv, W['ffn_w3'][l], 'nt', f'ffn3_dx_{l}', add=_mm(du, W['ffn_w1'][l], 'nt', f'ffn1_dx_{l}'))
    grads['ffn_w1'] = _mm(saved['n2'], du, 'tn', f'ffn1_dw_{l}', out_dtype=bf16)
    grads['ffn_w3'] = _mm(saved['n2'], dv, 'tn', f'ffn3_dw_{l}', out_dtype=bf16)
    (dh2n,), (dg2,) = _tile_bwd(_f_rms, [saved['h2']], [g2], [dn2], [True], [True], tm, f'rms2_bwd_{l}')
    grads['ffn_norm'] = dg2[0]
    return (dh3, dh2n), grads


def _layer_bwd_mix(dh2, saved, W, l):
    S = dh2.shape[0]
    tm = _pick(S, (256, 128))
    g1 = W['attn_norm'][l][None]
    grads = {}
    dmixed = _mm(dh2, W['w_out'][l], 'nt', f'out_dx_{l}')
    grads['w_out'] = _mm(saved['mixed'], dh2, 'tn', f'out_dw_{l}', out_dtype=bf16)
    dproj, dmp = _mixers_bwd(dmixed, saved['mix'], l)
    grads.update(dmp)
    dn1 = _mm(dproj, W['w_in'][l], 'nt', f'proj_dx_{l}')
    grads['w_in'] = _mm(saved['n1'], dproj, 'tn', f'proj_dw_{l}', out_dtype=bf16)
    (dh1n,), (dg1,) = _tile_bwd(_f_rms, [saved['h']], [g1], [dn1], [True], [True], tm, f'rms1_bwd_{l}')
    grads['attn_norm'] = dg1[0]
    return (dh2, dh1n), grads


def kernel(x, attn_norm, w_in, w_out, mla_q_norm, mla_kv_norm, mla_w_uq, mla_w_ukv, mla_qk_q, mla_qk_k, s5_lambda_re, s5_lambda_im, s5_log_dt, s5_b_re, s5_b_im, s5_c_re, s5_c_im, s5_d, s5_w_glu, dil_q_norm, dil_k_norm, t5_bias, dn_conv, dn_a_log, dn_dt_bias, dn_o_norm, ffn_norm, ffn_w1, ffn_w3, ffn_w2, loss_target, m_attn_norm, m_w_in, m_w_out, m_mla_q_norm, m_mla_kv_norm, m_mla_w_uq, m_mla_w_ukv, m_mla_qk_q, m_mla_qk_k, m_s5_lambda_re, m_s5_lambda_im, m_s5_log_dt, m_s5_b_re, m_s5_b_im, m_s5_c_re, m_s5_c_im, m_s5_d, m_s5_w_glu, m_dil_q_norm, m_dil_k_norm, m_t5_bias, m_dn_conv, m_dn_a_log, m_dn_dt_bias, m_dn_o_norm, m_ffn_norm, m_ffn_w1, m_ffn_w3, m_ffn_w2, v_attn_norm, v_w_in, v_w_out, v_mla_q_norm, v_mla_kv_norm, v_mla_w_uq, v_mla_w_ukv, v_mla_qk_q, v_mla_qk_k, v_s5_lambda_re, v_s5_lambda_im, v_s5_log_dt, v_s5_b_re, v_s5_b_im, v_s5_c_re, v_s5_c_im, v_s5_d, v_s5_w_glu, v_dil_q_norm, v_dil_k_norm, v_t5_bias, v_dn_conv, v_dn_a_log, v_dn_dt_bias, v_dn_o_norm, v_ffn_norm, v_ffn_w1, v_ffn_w3, v_ffn_w2):
    given = dict(locals())
    w_loc = {n: given[n] for n in WEIGHTS}
    m_loc = {n: given['m_' + n] for n in WEIGHTS}
    v_loc = {n: given['v_' + n] for n in WEIGHTS}
    big_names = list(BIG)

    own = 2 * lax.axis_index('x') + lax.axis_index('y')
    groups = [[(n, 0) for n in GATHER_FIRST], [(n, 0) for n in GATHER_FFN], [(n, 1) for n in big_names]]
    started, order = [], jnp.zeros((8, LANES), f32)
    for gi, group in enumerate(groups):
        blocks = [w_loc[n][l].astype(bf16) for n, l in group]
        lands = [lax.empty((N_SHARDS,) + b.shape, bf16) for b in blocks]
        send_sems, recv_sems, blocks, lands, order = _to_chips_start(blocks, lands, False, order, f'gather_start_{gi}')
        started.append((send_sems, recv_sems, blocks, lands))
    W = {n: [None] * DEPTH for n in big_names}
    for n in SMALL:
        W[n] = w_loc[n]
    W['dil_tables'] = _dil_tables(w_loc['t5_bias'])

    def arrive(gi, after):
        send_sems, recv_sems, blocks, lands = started[gi]
        blocks, lands = _to_chips_wait(send_sems, recv_sems, blocks, lands, False, after, f'gather_wait_{gi}')
        for (n, l), block, land in zip(groups[gi], blocks, lands):
            W[n][l] = _from_shards(n, lax.dynamic_update_slice(land, block[None], (own, 0, 0)))

    arrive(0, order)
    h = x[0]
    saved = []
    for l in range(DEPTH):
        h2, sv = _layer_fwd_mix(h, W, l)
        if l == 0:
            arrive(1, h2)
        h = _layer_fwd_ffn(h2, W, l, sv)
        if l == 0:
            arrive(2, h)
        saved.append(sv)
    parts_loss, dh = _loss_head(h, loss_target[0])
    local_loss = jnp.sum(parts_loss)

    layer_grads = [dict() for _ in range(DEPTH)]
    sent = []

    def send(group, tag):
        srcs = [_by_shard(n, layer_grads[l][n]).astype(bf16) for n, l in group]
        lands = [lax.empty((3,) + s.shape[1:], bf16) for s in srcs]
        send_sems, recv_sems, srcs, lands, token = _to_chips_start(srcs, lands, True, jnp.zeros((8, LANES), f32),
                                                                   f'reduce_start_{tag}')
        sent.append((group, tag, send_sems, recv_sems, srcs, lands))
        return token[0, 0]

    for l in reversed(range(DEPTH)):
        (dh3, dh2n), g_ffn = _layer_bwd_ffn(dh, saved[l], W, l)
        layer_grads[l].update(g_ffn)
        dh2 = dh3 + dh2n
        if l == 0:
            dh2 = dh2 + send([(n, 0) for n in GATHER_FFN], 'ffn0')
        (dh2, dh1n), g_mix = _layer_bwd_mix(dh2, saved[l], W, l)
        layer_grads[l].update(g_mix)
        dh = dh2 + dh1n
        if l == 1:
            dh = dh + send([(n, 1) for n in big_names], 'layer1')
    last = send([(n, 0) for n in GATHER_FIRST], 'first0')
    grad_x = dh[None]
    small_full = []
    for n in SMALL:
        if n == 't5_bias':
            small_full.append(_t5_grad([a_ + b_ for a_, b_ in zip(layer_grads[0]['t5_tables'], layer_grads[1]['t5_tables'])]))
        else:
            small_full.append(jnp.stack([layer_grads[l][n] for l in range(DEPTH)]))

    small_shapes = [w_loc[n].shape for n in SMALL] + [(1,)]
    nothing = [jnp.zeros((1,), f32)]
    small_pack = _pack(small_full + [local_loss.reshape(1)]) + last
    _, recv_small = _swap_with_sibling([], small_pack)
    chip_small = _small_chip_sum(small_pack, recv_small)
    _, from_chips_small = _exchange_between_chips([], chip_small)

    mine = {}
    for group, tag, send_sems, recv_sems, srcs, lands in sent:
        srcs, lands = _to_chips_wait(send_sems, recv_sems, srcs, lands, True, from_chips_small, f'reduce_wait_{tag}')
        for (n, l), src, land in zip(group, srcs, lands):
            mine[(n, l)] = _partial_sum(src, land, f'partial_{n}_{l}')
    keys = [(n, l) for n in big_names for l in range(DEPTH)]
    theirs = dict(zip(keys, _swap_partials([mine[k] for k in keys])))

    g_small_p, d_small_p, m_small_p, v_small_p = _small_update(
        small_pack, recv_small, from_chips_small, _pack([w_loc[n] for n in SMALL] + nothing),
        _pack([m_loc[n] for n in SMALL] + nothing), _pack([v_loc[n] for n in SMALL] + nothing))
    loss = _unpack(g_small_p, small_shapes)[-1][0]
    grad, delta, new_m, new_v = {}, {}, {}, {}
    for n, g_, d_, m_, v_ in zip(SMALL, _unpack(g_small_p, small_shapes), _unpack(d_small_p, small_shapes),
                                 _unpack(m_small_p, small_shapes), _unpack(v_small_p, small_shapes)):
        grad[n], delta[n], new_m[n], new_v[n] = g_, d_, m_, v_
    for n in big_names:
        grad[n], delta[n], new_m[n], new_v[n] = _adamw(
            w_loc[n], m_loc[n], v_loc[n], [mine[(n, l)] for l in range(DEPTH)], [theirs[(n, l)] for l in range(DEPTH)],
            'adamw_' + n)
    return (loss, grad_x, *[grad[n] for n in WEIGHTS], *[delta[n] for n in WEIGHTS],
            *[new_m[n] for n in WEIGHTS], *[new_v[n] for n in WEIGHTS])
```

```python
import functools
import math

import numpy as np
import jax
import jax.numpy as jnp
from jax import lax
from jax.experimental import pallas as pl
from jax.experimental.pallas import tpu as pltpu

f32 = jnp.float32
bf16 = jnp.bfloat16
HI = lax.Precision.HIGHEST
MESH = pl.DeviceIdType.MESH

VMEM_LIMIT_BYTES = 48 * 1024 * 1024
MM_VMEM_BUDGET_BYTES = 32 * 1024 * 1024
LANES = 128

D_MODEL = 1024
DEPTH = 2
GROUP_W = 256
HEAD_DIM = 64
EPS = 1e-6
NEG_INF = -1e30
N_HEADS = 4
MLA_NOPE, MLA_ROPE = 64, 32
MLA_DQK = MLA_NOPE + MLA_ROPE
ROPE_THETA = 10000.0
Q_BLOCK = 128
S5_G, S5_CG, S5_P = 16, 16, 64
DIL_PAIRS = ((128, 1), (512, 4), (2048, 16))
T5_BUCKETS, T5_MAX_DIST = 32, 2048
DN_CHUNK = 64
FFN_HIDDEN = 2816
IN_SPLITS = (256, 128, 32, 256, 768, 768, 4, 4, 256)
IN_COLS = sum(IN_SPLITS)

ADAM_LR, ADAM_B1, ADAM_B2, ADAM_EPS, ADAM_WD, ADAM_STEP = 0.001, 0.9, 0.999, 1e-08, 0.01, 10

WEIGHTS = ['attn_norm', 'w_in', 'w_out', 'mla_q_norm', 'mla_kv_norm', 'mla_w_uq', 'mla_w_ukv', 'mla_qk_q', 'mla_qk_k',
           's5_lambda_re', 's5_lambda_im', 's5_log_dt', 's5_b_re', 's5_b_im', 's5_c_re', 's5_c_im', 's5_d', 's5_w_glu',
           'dil_q_norm', 'dil_k_norm', 't5_bias', 'dn_conv', 'dn_a_log', 'dn_dt_bias', 'dn_o_norm', 'ffn_norm',
           'ffn_w1', 'ffn_w3', 'ffn_w2']
BIG = {'w_in': 2, 'w_out': 1, 'mla_w_uq': 2, 'mla_w_ukv': 2, 's5_w_glu': 2, 'dn_conv': 2, 'ffn_w1': 1, 'ffn_w3': 1,
       'ffn_w2': 1}
TRANSPOSED = ('ffn_w1', 'ffn_w3')
SMALL = [n for n in WEIGHTS if n not in BIG]
GATHER_FIRST = ['w_in', 'mla_w_uq', 'mla_w_ukv', 's5_w_glu', 'dn_conv', 'w_out']
GATHER_FFN = ['ffn_w1', 'ffn_w3', 'ffn_w2']
N_SHARDS = 4
PACK_COLS = 1024


def _cparams(sem=None, big=False):
    kw = {}
    if sem is not None:
        kw['dimension_semantics'] = sem
    if big:
        kw['vmem_limit_bytes'] = VMEM_LIMIT_BYTES
    return pltpu.CompilerParams(**kw)


def _pick(n, prefs):
    for p in prefs:
        if p <= n and n % p == 0:
            return p
    return n


def _lane_tile(n, cap):
    for t in range(cap - cap % LANES, 0, -LANES):
        if n % t == 0:
            return t
    return n


def _mm(a, b, mode, name, add=None, out_dtype=f32):
    if mode == 'nn':
        (M, K), (K2, N) = a.shape, b.shape
    elif mode == 'nt':
        (M, K), (N, K2) = a.shape, b.shape
    else:
        (K, M), (K2, N) = a.shape, b.shape
    assert K == K2, (name, a.shape, b.shape)
    tk = K if K <= 2816 else _pick(K, (2816, 2048, 1408, 1024, 512))
    cap_m, cap_n = (1408 if mode == 'tn' else 512), 1408

    def need(tm_, tn_):
        per_step = tm_ * tk * a.dtype.itemsize + tk * tn_ * b.dtype.itemsize + tm_ * tn_ * jnp.dtype(out_dtype).itemsize
        if add is not None:
            per_step += tm_ * tn_ * add.dtype.itemsize
        return 2 * per_step + tm_ * tn_ * 4

    tm, tn = _lane_tile(M, cap_m), _lane_tile(N, cap_n)
    while need(tm, tn) > MM_VMEM_BUDGET_BYTES and cap_m > LANES:
        cap_m //= 2
        tm = _lane_tile(M, cap_m)
    nk = K // tk
    dims = {'nn': (((1,), (0,)), ((), ())), 'nt': (((1,), (1,)), ((), ())), 'tn': (((0,), (0,)), ((), ()))}[mode]
    has_add = add is not None

    def body(*refs):
        a_ref, b_ref = refs[0], refs[1]
        add_ref = refs[2] if has_add else None
        o_ref = refs[3] if has_add else refs[2]
        part = lax.dot_general(a_ref[...].astype(bf16), b_ref[...].astype(bf16), dims, preferred_element_type=f32)
        if nk == 1:
            if has_add:
                part = part + add_ref[...].astype(f32)
            o_ref[...] = part.astype(out_dtype)
        else:
            acc_ref = refs[-1]
            k = pl.program_id(2)

            @pl.when(k == 0)
            def _():
                acc_ref[...] = part

            @pl.when(k > 0)
            def _():
                acc_ref[...] += part

            @pl.when(k == nk - 1)
            def _():
                r = acc_ref[...]
                if has_add:
                    r = r + add_ref[...].astype(f32)
                o_ref[...] = r.astype(out_dtype)

    if mode == 'nn':
        a_spec = pl.BlockSpec((tm, tk), lambda i, j, k: (i, k))
        b_spec = pl.BlockSpec((tk, tn), lambda i, j, k: (k, j))
    elif mode == 'nt':
        a_spec = pl.BlockSpec((tm, tk), lambda i, j, k: (i, k))
        b_spec = pl.BlockSpec((tn, tk), lambda i, j, k: (j, k))
    else:
        a_spec = pl.BlockSpec((tk, tm), lambda i, j, k: (k, i))
        b_spec = pl.BlockSpec((tk, tn), lambda i, j, k: (k, j))
    in_specs = [a_spec, b_spec]
    args = [a, b]
    if has_add:
        in_specs.append(pl.BlockSpec((tm, tn), lambda i, j, k: (i, j)))
        args.append(add)
    return pl.pallas_call(
        body, name=name, grid=(M // tm, N // tn, nk), in_specs=in_specs,
        out_specs=pl.BlockSpec((tm, tn), lambda i, j, k: (i, j)),
        out_shape=jax.ShapeDtypeStruct((M, N), out_dtype),
        scratch_shapes=[pltpu.VMEM((tm, tn), f32)] if nk > 1 else [],
        compiler_params=_cparams(('parallel', 'parallel', 'arbitrary'), big=True),
    )(*args)


def _full_spec(p):
    nd = p.ndim
    return pl.BlockSpec(p.shape, lambda i, _nd=nd: (0,) * _nd)


def _tile_fwd(f, tiled, params, outs, tm, name):
    S = tiled[0].shape[0]
    nt, npar = len(tiled), len(params)

    def body(*refs):
        vals = [r[...].astype(f32) for r in refs[:nt + npar]]
        res = f(*vals)
        for r, o in zip(res, refs[nt + npar:]):
            o[...] = r.astype(o.dtype)

    return pl.pallas_call(
        body, name=name, grid=(S // tm,),
        in_specs=[pl.BlockSpec((tm, t.shape[1]), lambda i: (i, 0)) for t in tiled] + [_full_spec(p) for p in params],
        out_specs=[pl.BlockSpec((tm, c), lambda i: (i, 0)) for c, _ in outs],
        out_shape=[jax.ShapeDtypeStruct((S, c), dt) for c, dt in outs],
        compiler_params=_cparams(('parallel',), big=True),
    )(*tiled, *params)


def _tile_bwd(f, tiled, params, cts, diff_t, diff_p, tm, name, dt_dtypes=None):
    S = tiled[0].shape[0]
    nt, npar, nc = len(tiled), len(params), len(cts)
    it = [i for i in range(nt) if diff_t[i]]
    ip = [i for i in range(npar) if diff_p[i]]
    if dt_dtypes is None:
        dt_dtypes = [f32] * len(it)

    def body(*refs):
        vals = [r[...].astype(f32) for r in refs[:nt + npar]]
        ct_vals = tuple(r[...].astype(f32) for r in refs[nt + npar:nt + npar + nc])
        out_refs = refs[nt + npar + nc:]

        def g(*dv):
            full = list(vals)
            for k, i in enumerate(it):
                full[i] = dv[k]
            for k, i in enumerate(ip):
                full[nt + i] = dv[len(it) + k]
            return tuple(f(*full))

        _, vjp = jax.vjp(g, *[vals[i] for i in it], *[vals[nt + i] for i in ip])
        grads = vjp(ct_vals)
        for k in range(len(it)):
            out_refs[k][...] = grads[k].astype(out_refs[k].dtype)
        step = pl.program_id(0)
        for k in range(len(ip)):
            o = out_refs[len(it) + k]
            gk = grads[len(it) + k]

            @pl.when(step == 0)
            def _(o=o, gk=gk):
                o[...] = gk

            @pl.when(step > 0)
            def _(o=o, gk=gk):
                o[...] += gk

    out_specs = [pl.BlockSpec((tm, tiled[i].shape[1]), lambda i_: (i_, 0)) for i in it] + [_full_spec(params[i]) for i in ip]
    out_shape = [jax.ShapeDtypeStruct(tiled[i].shape, dt_dtypes[k]) for k, i in enumerate(it)] + \
                [jax.ShapeDtypeStruct(params[i].shape, f32) for i in ip]
    res = pl.pallas_call(
        body, name=name, grid=(S // tm,),
        in_specs=[pl.BlockSpec((tm, t.shape[1]), lambda i: (i, 0)) for t in tiled] + [_full_spec(p) for p in params] +
                 [pl.BlockSpec((tm, c.shape[1]), lambda i: (i, 0)) for c in cts],
        out_specs=out_specs, out_shape=out_shape,
        compiler_params=_cparams(('arbitrary',), big=True),
    )(*tiled, *params, *cts)
    return list(res[:len(it)]), list(res[len(it):])


def _rms(x, g):
    return x * lax.rsqrt(jnp.mean(x * x, axis=-1, keepdims=True) + EPS) * g


def _f_rms(x, g):
    return (_rms(x, g),)


def _f_swiglu(u, v):
    return (u * jax.nn.sigmoid(u) * v,)


def _loss_head(y, target):
    S, D = y.shape
    tm = _pick(S, (256, 128))

    def body(y_ref, t_ref, part_ref, dy_ref):
        e = y_ref[...] - t_ref[...]
        dy_ref[...] = e * (1.0 / D)
        s = 0.5 * jnp.sum(jnp.sum(e * e, axis=1, keepdims=True), axis=0, keepdims=True) * (1.0 / D)
        r = lax.broadcasted_iota(jnp.int32, (8, LANES), 0)
        c = lax.broadcasted_iota(jnp.int32, (8, LANES), 1)
        part_ref[0] = jnp.where((r == 0) & (c == 0), s, 0.0)

    return pl.pallas_call(
        body, name='loss_head', grid=(S // tm,),
        in_specs=[pl.BlockSpec((tm, D), lambda i: (i, 0))] * 2,
        out_specs=[pl.BlockSpec((1, 8, LANES), lambda i: (i, 0, 0)), pl.BlockSpec((tm, D), lambda i: (i, 0))],
        out_shape=[jax.ShapeDtypeStruct((S // tm, 8, LANES), f32), jax.ShapeDtypeStruct((S, D), f32)],
        compiler_params=_cparams(('parallel',)),
    )(y, target)


def _pack_rows_of(shape):
    rows = -(-math.prod(shape) // PACK_COLS)
    return -(-rows // 8) * 8


def _pack(arrs):
    parts = []
    for a in arrs:
        rows = _pack_rows_of(a.shape)
        flat = a.astype(f32).reshape(-1)
        parts.append(jnp.pad(flat, (0, rows * PACK_COLS - flat.shape[0])).reshape(rows, PACK_COLS))
    return jnp.concatenate(parts, axis=0)


def _unpack(pack, shapes):
    out, row = [], 0
    for s in shapes:
        rows = _pack_rows_of(s)
        out.append(pack[row:row + rows].reshape(-1)[:math.prod(s)].reshape(s))
        row += rows
    return out


ANY = pl.BlockSpec(memory_space=pl.ANY)


def _place():
    return lax.axis_index('x'), lax.axis_index('y'), lax.axis_index('c')


def _where():
    return jnp.stack([lax.axis_index('c'), 2 * lax.axis_index('x') + lax.axis_index('y')]).astype(jnp.int32)


def _remote(src, dst, send_sems, recv_sems, k, to):
    return pltpu.make_async_remote_copy(src_ref=src, dst_ref=dst, send_sem=send_sems.at[k], recv_sem=recv_sems.at[k],
                                        device_id=to, device_id_type=MESH)


def _swap_with_sibling(gs, small):
    n = len(gs)

    def body(*refs):
        g_refs, s_ref = refs[:n], refs[n]
        r_refs, rs_ref = refs[n + 1:2 * n + 1], refs[2 * n + 1]
        send_sems, recv_sems = refs[2 * n + 2:]
        x, y, c = _place()
        sib = (x, y, 1 - c)
        cps = [_remote(g_refs[t].at[:, 1 - c], r_refs[t], send_sems, recv_sems, t, sib) for t in range(n)]
        cps.append(_remote(s_ref, rs_ref, send_sems, recv_sems, n, sib))
        for cp in cps:
            cp.start()
        for cp in cps:
            cp.wait()

    res = pl.pallas_call(
        body, name='swap_with_sibling', in_specs=[ANY] * (n + 1), out_specs=[ANY] * (n + 1),
        out_shape=[jax.ShapeDtypeStruct((N_SHARDS,) + g.shape[2:], g.dtype) for g in gs] +
                  [jax.ShapeDtypeStruct(small.shape, small.dtype)],
        scratch_shapes=[pltpu.SemaphoreType.DMA((n + 1,)), pltpu.SemaphoreType.DMA((n + 1,))],
    )(*gs, small)
    return list(res[:n]), res[n]


def _exchange_between_chips(cs, small):
    n = len(cs)

    def body(*refs):
        c_refs, s_ref = refs[:n], refs[n]
        r_refs, rs_ref = refs[n + 1:2 * n + 1], refs[2 * n + 1]
        send_sems, recv_sems = refs[2 * n + 2:]
        x, y, c = _place()
        chips = [(1 - x, y), (x, 1 - y), (1 - x, 1 - y)]
        cps = []
        for j, (px, py) in enumerate(chips):
            for t in range(n):
                cps.append(_remote(c_refs[t].at[2 * px + py], r_refs[t].at[j], send_sems, recv_sems, 3 * t + j, (px, py, c)))
            cps.append(_remote(s_ref, rs_ref.at[j], send_sems, recv_sems, 3 * n + j, (px, py, c)))
        for cp in cps:
            cp.start()
        for cp in cps:
            cp.wait()

    res = pl.pallas_call(
        body, name='exchange_between_chips', in_specs=[ANY] * (n + 1), out_specs=[ANY] * (n + 1),
        out_shape=[jax.ShapeDtypeStruct((3,) + c.shape[1:], c.dtype) for c in cs] +
                  [jax.ShapeDtypeStruct((3,) + small.shape, small.dtype)],
        scratch_shapes=[pltpu.SemaphoreType.DMA((3 * n + 3,)), pltpu.SemaphoreType.DMA((3 * n + 3,))],
    )(*cs, small)
    return list(res[:n]), res[n]


def _swap_partials(ts):
    n = len(ts)

    def body(*refs):
        t_refs, o_refs = refs[:n], refs[n:2 * n]
        send_sems, recv_sems = refs[2 * n:]
        x, y, c = _place()
        cps = [_remote(t_refs[t], o_refs[t], send_sems, recv_sems, t, (x, y, 1 - c)) for t in range(n)]
        for cp in cps:
            cp.start()
        for cp in cps:
            cp.wait()

    return pl.pallas_call(
        body, name='swap_partials', in_specs=[ANY] * n, out_specs=[ANY] * n,
        out_shape=[jax.ShapeDtypeStruct(t.shape, t.dtype) for t in ts],
        scratch_shapes=[pltpu.SemaphoreType.DMA((n,)), pltpu.SemaphoreType.DMA((n,))],
    )(*ts)


HBM = pl.BlockSpec(memory_space=pltpu.HBM)
SEM = pl.BlockSpec(memory_space=pltpu.SEMAPHORE)
DATAFLOW = pltpu.SideEffectType.DATAFLOW_SIDE_EFFECTING


def _in_hbm(t):
    return pltpu.with_memory_space_constraint(t, pltpu.HBM)


def _other_chips():
    x, y, c = _place()
    return [(1 - x, y, c), (x, 1 - y, c), (1 - x, 1 - y, c)]


def _to_chips_copies(src_refs, land_refs, send_sems, recv_sems, per_peer):
    x, y, _ = _place()
    cps = []
    for t, (src, land) in enumerate(zip(src_refs, land_refs)):
        for j, (px, py, pc) in enumerate(_other_chips()):
            s = src.at[2 * px + py] if per_peer else src
            d = land.at[j] if per_peer else land.at[2 * x + y]
            cps.append(_remote(s, d, send_sems, recv_sems, 3 * t + j, (px, py, pc)))
    return cps


def _to_chips_start(srcs, lands, per_peer, order, name):
    n = len(srcs)

    def body(*refs):
        src_refs, land_refs = refs[:n], refs[n:2 * n]
        send_sems, recv_sems = refs[2 * n + 1], refs[2 * n + 2]
        token = refs[-1]
        for cp in _to_chips_copies(src_refs, land_refs, send_sems, recv_sems, per_peer):
            cp.start()
        token[...] = jnp.zeros_like(token)

    res = pl.pallas_call(
        body, name=name, in_specs=[HBM] * (2 * n) + [ANY],
        out_specs=[SEM, SEM] + [HBM] * (2 * n) + [pl.BlockSpec(memory_space=pltpu.VMEM)],
        out_shape=[pltpu.SemaphoreType.DMA((3 * n,)), pltpu.SemaphoreType.DMA((3 * n,))] +
                  [pltpu.HBM(t.shape, t.dtype) for t in srcs] + [pltpu.HBM(t.shape, t.dtype) for t in lands] +
                  [jax.ShapeDtypeStruct((8, LANES), f32)],
        input_output_aliases={i: 2 + i for i in range(2 * n)},
        compiler_params=pltpu.CompilerParams(has_side_effects=DATAFLOW),
    )(*[_in_hbm(t) for t in srcs], *[_in_hbm(t) for t in lands], order)
    return res[0], res[1], list(res[2:2 + n]), list(res[2 + n:2 + 2 * n]), res[-1]


def _to_chips_wait(send_sems, recv_sems, srcs, lands, per_peer, after, name):
    n = len(srcs)

    def body(*refs):
        src_refs, land_refs = refs[:n], refs[n:2 * n]
        send_ref, recv_ref = refs[2 * n], refs[2 * n + 1]
        for cp in _to_chips_copies(src_refs, land_refs, send_ref, recv_ref, per_peer):
            cp.wait_send()
            cp.wait_recv()

    res = pl.pallas_call(
        body, name=name, in_specs=[HBM] * (2 * n) + [SEM, SEM, ANY],
        out_specs=[HBM] * (2 * n),
        out_shape=[pltpu.HBM(t.shape, t.dtype) for t in srcs] + [pltpu.HBM(t.shape, t.dtype) for t in lands],
        input_output_aliases={i: i for i in range(2 * n)},
        compiler_params=pltpu.CompilerParams(has_side_effects=DATAFLOW),
    )(*srcs, *lands, send_sems, recv_sems, after)
    return list(res[:n]), list(res[n:])


def _row_tile(a):
    return _pick(a, (512, 256, 128, 64, 32, 16, 8))


def _partial_sum(g, land, name):
    _, a, b = g.shape
    tr = _row_tile(a)

    def body(w_ref, g_ref, r_ref, o_ref):
        t = g_ref[0].astype(f32) + r_ref[0].astype(f32)
        t = t + r_ref[1].astype(f32)
        t = t + r_ref[2].astype(f32)
        o_ref[...] = t.astype(o_ref.dtype)

    return pl.pallas_call(
        body, name=name,
        grid_spec=pltpu.PrefetchScalarGridSpec(
            num_scalar_prefetch=1, grid=(a // tr,),
            in_specs=[pl.BlockSpec((1, tr, b), lambda i, w: (w[1], i, 0)), pl.BlockSpec((3, tr, b), lambda i, w: (0, i, 0))],
            out_specs=pl.BlockSpec((tr, b), lambda i, w: (i, 0))),
        out_shape=jax.ShapeDtypeStruct((a, b), bf16),
        compiler_params=_cparams(('parallel',)),
    )(_where(), g, land)


def _by_shard(name, t):
    r, c = t.shape
    if BIG[name] == 2:
        return t.reshape(r, N_SHARDS, c // N_SHARDS).transpose(1, 0, 2)
    return t.reshape(N_SHARDS, r // N_SHARDS, c)


def _from_shards(name, g):
    s, a, b = g.shape
    if BIG[name] == 2:
        return g.transpose(1, 0, 2).reshape(a, s * b)
    return g.reshape(s * a, b)


def _adam_math(w, g, m, v):
    m = ADAM_B1 * m + (1.0 - ADAM_B1) * g
    v = ADAM_B2 * v + (1.0 - ADAM_B2) * (g * g)
    m_hat = m / (1.0 - ADAM_B1 ** ADAM_STEP)
    v_hat = v / (1.0 - ADAM_B2 ** ADAM_STEP)
    delta = -ADAM_LR * (m_hat / (jnp.sqrt(v_hat) + ADAM_EPS) + ADAM_WD * w)
    return delta, m, v


def _small_update(own, sib, chips, w, m, v):
    def body(o_ref, s_ref, c_ref, w_ref, m_ref, v_ref, g_out, d_out, m_out, v_out):
        chip = o_ref[...] + s_ref[...]
        g = (chip + c_ref[0]) + (c_ref[1] + c_ref[2])
        d, mn, vn = _adam_math(w_ref[...], g, m_ref[...], v_ref[...])
        g_out[...] = g
        d_out[...] = d
        m_out[...] = mn
        v_out[...] = vn

    return pl.pallas_call(body, name='small_update', out_shape=[jax.ShapeDtypeStruct(own.shape, f32)] * 4)(
        own, sib, chips, w, m, v)


def _small_chip_sum(own, sib):
    def body(o_ref, s_ref, out):
        out[...] = o_ref[...] + s_ref[...]
    return pl.pallas_call(body, name='small_chip_sum', out_shape=jax.ShapeDtypeStruct(own.shape, f32))(own, sib)


def _adamw(w, m, v, mine, theirs, name):
    layers, a, b = w.shape
    tr = _row_tile(a)

    def body(w_ref, m_ref, v_ref, p0, p1, q0, q1, g_out, d_out, m_out, v_out):
        first = pl.program_id(0) == 0
        g = jnp.where(first, p0[...].astype(f32) + q0[...].astype(f32), p1[...].astype(f32) + q1[...].astype(f32))
        d, mn, vn = _adam_math(w_ref[0], g, m_ref[0], v_ref[0])
        g_out[0] = g
        d_out[0] = d
        m_out[0] = mn
        v_out[0] = vn

    full = pl.BlockSpec((1, tr, b), lambda l, i: (l, i, 0))
    part = pl.BlockSpec((tr, b), lambda l, i: (i, 0))
    return pl.pallas_call(body, name=name, grid=(layers, a // tr), in_specs=[full] * 3 + [part] * 4, out_specs=[full] * 4,
                          out_shape=[jax.ShapeDtypeStruct(w.shape, f32)] * 4,
                          compiler_params=_cparams(('parallel', 'parallel')))(w, m, v, *mine, *theirs)


def _dg(a, b, ca, cb):
    return lax.dot_general(a.astype(bf16), b.astype(bf16), (((ca,), (cb,)), ((), ())), preferred_element_type=f32)


@jax.custom_vjp
def _bmm(a, b):
    return _dg(a, b, 1, 0)


_bmm.defvjp(lambda a, b: (_dg(a, b, 1, 0), (a, b)), lambda r, g: (_dg(g, r[1], 1, 1), _dg(r[0], g, 0, 0)))


@jax.custom_vjp
def _bmm_nt(a, b):
    return _dg(a, b, 1, 1)


_bmm_nt.defvjp(lambda a, b: (_dg(a, b, 1, 1), (a, b)), lambda r, g: (_dg(g, r[1], 1, 0), _dg(g, r[0], 0, 0)))


@jax.custom_vjp
def _bmm_tn(a, b):
    return _dg(a, b, 0, 0)


_bmm_tn.defvjp(lambda a, b: (_dg(a, b, 0, 0), (a, b)), lambda r, g: (_dg(r[1], g, 1, 1), _dg(r[0], g, 1, 0)))


def _hdot(a, b):
    return jnp.dot(a, b, precision=HI, preferred_element_type=f32)


def _hdot_nt(a, b):
    return lax.dot_general(a, b, (((1,), (1,)), ((), ())), precision=HI, preferred_element_type=f32)


def _hdot_tn(a, b):
    return lax.dot_general(a, b, (((0,), (0,)), ((), ())), precision=HI, preferred_element_type=f32)


def _head_mask(h, width=GROUP_W):
    lane = lax.broadcasted_iota(jnp.int32, (1, width), 1)
    return ((lane >= h * HEAD_DIM) & (lane < (h + 1) * HEAD_DIM)).astype(f32)


def _rope_perm():
    p = np.zeros((LANES, LANES), np.float32)
    half = MLA_ROPE // 2
    for i in range(half):
        p[MLA_NOPE + half + i, MLA_NOPE + i] = -1.0
        p[MLA_NOPE + i, MLA_NOPE + half + i] = 1.0
    return jnp.asarray(p)


def _rope_tables(S):
    half = MLA_ROPE // 2
    freqs = ROPE_THETA ** (-jnp.arange(half, dtype=f32) / half)
    ang = jnp.arange(S, dtype=f32)[:, None] * freqs[None, :]
    cos, sin = jnp.cos(ang), jnp.sin(ang)
    ones, zeros = jnp.ones((S, MLA_NOPE), f32), jnp.zeros((S, LANES - MLA_DQK), f32)
    c_tab = jnp.concatenate([ones, cos, cos, zeros], axis=1)
    s_tab = jnp.concatenate([jnp.zeros((S, MLA_NOPE), f32), sin, sin, zeros], axis=1)
    return c_tab, s_tab


def _f_mla_pre(c_q, c_kv, krope, c_tab, s_tab, q_norm, kv_norm, wq0, wq1, wq2, wq3, wk0, wk1, wk2, wk3, wv, gq, gk, perm):
    wq, wk = (wq0, wq1, wq2, wq3), (wk0, wk1, wk2, wk3)
    nq = _rms(c_q, q_norm)
    nkv = _rms(c_kv, kv_norm)

    def norm_rope(t, g):
        t = t * lax.rsqrt(jnp.sum(t * t, axis=-1, keepdims=True) * (1.0 / MLA_DQK) + EPS) * g
        return t * c_tab + _hdot(t, perm) * s_tab

    qs = [norm_rope(_bmm(nq, wq[h]), gq) * (MLA_DQK ** -0.5) for h in range(N_HEADS)]
    ks = [norm_rope(_bmm(nkv, wk[h]) + krope, gk) for h in range(N_HEADS)]
    return (*qs, *ks, _bmm(nkv, wv))


def _f_attn(qs, ks, v, q0):
    tq, S = qs[0].shape[0], ks[0].shape[0]
    qpos = q0 + lax.broadcasted_iota(jnp.int32, (tq, S), 0)
    kpos = lax.broadcasted_iota(jnp.int32, (tq, S), 1)
    keep = kpos <= qpos
    logits = [jnp.where(keep, _bmm_nt(qs[h], ks[h]), NEG_INF) for h in range(N_HEADS)]
    ps = [jnp.exp(lg - jnp.max(lg, axis=-1, keepdims=True)) for lg in logits]
    ps = [p / jnp.sum(p, axis=-1, keepdims=True) for p in ps]
    return sum(_bmm(p, v) * _head_mask(h) for h, p in enumerate(ps))


ATTN_PARTS = 4


def _mla_attn_fwd(qs, ks, v, name):
    S = v.shape[0]
    tq = Q_BLOCK
    parts = ATTN_PARTS if S % (ATTN_PARTS * tq) == 0 else 1
    per = S // parts
    outs = []
    for p in range(parts):
        n_keys = (p + 1) * per
        first_block = p * (per // tq)

        def body(*refs, first_block=first_block):
            q_vals = [r[...] for r in refs[:4]]
            k_vals = [r[...] for r in refs[4:8]]
            refs[9][...] = _f_attn(q_vals, k_vals, refs[8][...], (first_block + pl.program_id(0)) * tq)

        qspec = pl.BlockSpec((tq, LANES), lambda i, fb=first_block: (fb + i, 0))
        outs.append(pl.pallas_call(
            body, name=f'{name}_{p}', grid=(per // tq,),
            in_specs=[qspec] * 4 + [pl.BlockSpec((n_keys, LANES), lambda i: (0, 0))] * 4 +
                     [pl.BlockSpec((n_keys, GROUP_W), lambda i: (0, 0))],
            out_specs=pl.BlockSpec((tq, GROUP_W), lambda i: (i, 0)),
            out_shape=jax.ShapeDtypeStruct((per, GROUP_W), f32),
            compiler_params=_cparams(('parallel',), big=True),
        )(*qs, *ks, v))
    return jnp.concatenate(outs, axis=0)


def _mla_attn_bwd(qs, ks, v, do, name):
    S = v.shape[0]
    tq = Q_BLOCK
    parts = ATTN_PARTS if S % (ATTN_PARTS * tq) == 0 else 1
    per = S // parts
    dq_parts, dkv_sum = [], None
    for p in range(parts):
        n_keys = (p + 1) * per
        first_block = p * (per // tq)

        def body(*refs, first_block=first_block):
            q_vals = [r[...].astype(f32) for r in refs[:4]]
            k_vals = [r[...].astype(f32) for r in refs[4:8]]
            v_val = refs[8][...].astype(f32)
            q0 = (first_block + pl.program_id(0)) * tq
            _, vjp = jax.vjp(lambda a, b, c: _f_attn(a, b, c, q0), q_vals, k_vals, v_val)
            dqs, dks, dv = vjp(refs[9][...])
            outs = refs[10:]
            for h in range(N_HEADS):
                outs[h][...] = dqs[h]
            first = pl.program_id(0) == 0
            for o, g in zip(outs[4:], (*dks, dv)):
                @pl.when(first)
                def _(o=o, g=g):
                    o[...] = g

                @pl.when(jnp.logical_not(first))
                def _(o=o, g=g):
                    o[...] += g

        qspec = pl.BlockSpec((tq, LANES), lambda i, fb=first_block: (fb + i, 0))
        kspec = pl.BlockSpec((n_keys, LANES), lambda i: (0, 0))
        vspec = pl.BlockSpec((n_keys, GROUP_W), lambda i: (0, 0))
        res = pl.pallas_call(
            body, name=f'{name}_{p}', grid=(per // tq,),
            in_specs=[qspec] * 4 + [kspec] * 4 + [vspec, pl.BlockSpec((tq, GROUP_W), lambda i, fb=first_block: (fb + i, 0))],
            out_specs=[pl.BlockSpec((tq, LANES), lambda i: (i, 0))] * 4 + [kspec] * 4 + [vspec],
            out_shape=[jax.ShapeDtypeStruct((per, LANES), f32)] * 4 + [jax.ShapeDtypeStruct((n_keys, LANES), f32)] * 4 +
                      [jax.ShapeDtypeStruct((n_keys, GROUP_W), f32)],
            compiler_params=_cparams(('arbitrary',), big=True),
        )(*qs, *ks, v, do)
        dq_parts.append(res[:4])
        dkv = [jnp.pad(t, ((0, S - n_keys), (0, 0))) for t in res[4:]]
        dkv_sum = dkv if dkv_sum is None else [a_ + b_ for a_, b_ in zip(dkv_sum, dkv)]
    dqs = [jnp.concatenate([dq_parts[p][h] for p in range(parts)], axis=0) for h in range(N_HEADS)]
    return dqs, dkv_sum[:4], dkv_sum[4]


def _mla_params(mp):
    pad = LANES - MLA_DQK
    wq = jnp.pad(mp['mla_w_uq'].reshape(GROUP_W, N_HEADS, MLA_DQK).transpose(1, 0, 2), ((0, 0), (0, 0), (0, pad)))
    wkv = mp['mla_w_ukv'].reshape(LANES, N_HEADS, MLA_NOPE + HEAD_DIM)
    wk = jnp.pad(wkv[:, :, :MLA_NOPE].transpose(1, 0, 2), ((0, 0), (0, 0), (0, LANES - MLA_NOPE)))
    wv = wkv[:, :, MLA_NOPE:].reshape(LANES, GROUP_W)
    gq = jnp.pad(mp['mla_qk_q'], (0, pad))[None]
    gk = jnp.pad(mp['mla_qk_k'], (0, pad))[None]
    return [mp['mla_q_norm'][None], mp['mla_kv_norm'][None], *[wq[h] for h in range(N_HEADS)],
            *[wk[h] for h in range(N_HEADS)], wv, gq, gk, _rope_perm()]


def _mla_fwd(c_q, c_kv, k_rope, mp, l):
    S = c_q.shape[0]
    tm = _pick(S, (256, 128))
    krope = jnp.pad(k_rope, ((0, 0), (MLA_NOPE, LANES - MLA_DQK)))
    c_tab, s_tab = _rope_tables(S)
    tiled = [c_q, c_kv, krope, c_tab, s_tab]
    params = _mla_params(mp)
    res = _tile_fwd(_f_mla_pre, tiled, params, [(LANES, bf16)] * 8 + [(GROUP_W, bf16)], tm, f'mla_pre_fwd_{l}')
    qs, ks, v = res[:4], res[4:8], res[8]
    y = _mla_attn_fwd(qs, ks, v, f'mla_attn_fwd_{l}')
    return y, (tiled, params, qs, ks, v)


def _mla_bwd(dy, saved, l):
    tiled, params, qs, ks, v = saved
    S = dy.shape[0]
    tm = _pick(S, (256, 128))
    dqs, dks, dv = _mla_attn_bwd(qs, ks, v, dy, f'mla_attn_bwd_{l}')
    (dc_q, dc_kv, dkrope), dpar = _tile_bwd(_f_mla_pre, tiled, params, [*dqs, *dks, dv], [True, True, True, False, False],
                                            [True] * 13 + [False], tm, f'mla_pre_bwd_{l}')
    dqn, dkvn = dpar[0], dpar[1]
    dwq, dwk = jnp.stack(dpar[2:6]), jnp.stack(dpar[6:10])
    dwv, dgq, dgk = dpar[10:13]
    dw_uq = dwq[:, :, :MLA_DQK].transpose(1, 0, 2).reshape(GROUP_W, N_HEADS * MLA_DQK)
    dw_ukv = jnp.concatenate([dwk[:, :, :MLA_NOPE].transpose(1, 0, 2), dwv.reshape(LANES, N_HEADS, HEAD_DIM)],
                             axis=2).reshape(LANES, N_HEADS * (MLA_NOPE + HEAD_DIM))
    grads = {'mla_q_norm': dqn[0], 'mla_kv_norm': dkvn[0], 'mla_w_uq': dw_uq, 'mla_w_ukv': dw_ukv,
             'mla_qk_q': dgq[0, :MLA_DQK], 'mla_qk_k': dgk[0, :MLA_DQK]}
    return dc_q, dc_kv, dkrope[:, MLA_NOPE:MLA_DQK], grads


SPAN = 128


def _head_mean_matrix():
    h = np.arange(GROUP_W) // HEAD_DIM
    return jnp.asarray((h[:, None] == h[None, :]).astype(np.float32) / HEAD_DIM)


def _f_dil_pre(q, k, gq, gk, hm):
    qn = q * lax.rsqrt(_hdot(q * q, hm) + EPS) * gq * (HEAD_DIM ** -0.5)
    kn = k * lax.rsqrt(_hdot(k * k, hm) + EPS) * gk
    return qn, kn


def _f_dil_branch(qb, kp, kc, vp, vc, b0, b1, b2, b3, first):
    kcat = jnp.concatenate([kp, kc], axis=0)
    vcat = jnp.concatenate([vp, vc], axis=0)
    qi = lax.broadcasted_iota(jnp.int32, (SPAN, 2 * SPAN), 0) + SPAN
    kj = lax.broadcasted_iota(jnp.int32, (SPAN, 2 * SPAN), 1)
    delta = qi - kj
    valid = (delta >= 0) & (delta <= SPAN) & jnp.logical_not(first & (kj < SPAN))
    masks = [_head_mask(h) for h in range(N_HEADS)]
    raw = [_bmm_nt(qb * hm, kcat) for hm in masks]
    logits = [jnp.where(valid, r + bias, NEG_INF) for r, bias in zip(raw, (b0, b1, b2, b3))]
    ms = [jnp.max(lg, axis=-1, keepdims=True) for lg in logits]
    ps = [jnp.exp(lg - m) for lg, m in zip(logits, ms)]
    pvs = [_bmm(p, vcat) for p in ps]
    o = sum(pv * hm for pv, hm in zip(pvs, masks))
    m_full = sum(m * hm for m, hm in zip(ms, masks))
    l_full = sum(jnp.sum(p, axis=-1, keepdims=True) * hm for p, hm in zip(ps, masks))
    return o, m_full, l_full


def _dil_branch_specs(d, nb):
    cur = pl.BlockSpec((SPAN, GROUP_W), lambda r, n: (n, r))
    prev = pl.BlockSpec((SPAN, GROUP_W), lambda r, n: (jnp.maximum(n - 1, 0), r))
    bias = pl.BlockSpec((1, SPAN, 2 * SPAN), lambda r, n: (0, 0, 0))
    return cur, prev, bias


def _head_table_specs():
    return [pl.BlockSpec((1, SPAN, 2 * SPAN), lambda r, n, h=h: (h, 0, 0)) for h in range(N_HEADS)]


def _dil_branch_fwd(q, k, v, table, name):
    L, d = q.shape[0], q.shape[1] // GROUP_W
    nb = L // SPAN
    cur, prev, bias = _dil_branch_specs(d, nb)

    def body(q_ref, kp_ref, kc_ref, vp_ref, vc_ref, b0, b1, b2, b3, o_ref, m_ref, l_ref):
        o, m, l = _f_dil_branch(*[r[...].astype(f32) for r in (q_ref, kp_ref, kc_ref, vp_ref, vc_ref)], b0[0], b1[0], b2[0], b3[0],
                                pl.program_id(1) == 0)
        o_ref[...] = o
        m_ref[...] = m
        l_ref[...] = l

    return pl.pallas_call(
        body, name=name, grid=(d, nb), in_specs=[cur, prev, cur, prev, cur] + _head_table_specs(),
        out_specs=[cur] * 3, out_shape=[jax.ShapeDtypeStruct(q.shape, f32)] * 3,
        compiler_params=_cparams(('parallel', 'parallel')),
    )(q, k, k, v, v, *[table] * N_HEADS)


def _dil_branch_bwd(q, k, v, table, do, dm, dl, name):
    L, d = q.shape[0], q.shape[1] // GROUP_W
    nb = L // SPAN
    cur, prev, bias = _dil_branch_specs(d, nb)
    whole = pl.BlockSpec((L, GROUP_W), lambda r, n: (0, r))

    def body(q_ref, kp_ref, kc_ref, vp_ref, vc_ref, b0, b1, b2, b3, do_ref, dm_ref, dl_ref,
             dq_ref, dk_ref, dv_ref, db0, db1, db2, db3):
        r, n = pl.program_id(0), pl.program_id(1)
        first = n == 0
        _, vjp = jax.vjp(lambda *a: _f_dil_branch(*a, first), *[r[...].astype(f32) for r in (q_ref, kp_ref, kc_ref, vp_ref, vc_ref)],
                         b0[0], b1[0], b2[0], b3[0])
        dq, dkp, dkc, dvp, dvc, g0, g1, g2, g3 = vjp((do_ref[...], dm_ref[...], dl_ref[...]))
        dq_ref[...] = dq

        @pl.when(first)
        def _():
            dk_ref[...] = jnp.zeros_like(dk_ref)
            dv_ref[...] = jnp.zeros_like(dv_ref)

        rows = pl.ds(pl.multiple_of(n * SPAN, SPAN), SPAN)
        dk_ref[rows, :] += dkc
        dv_ref[rows, :] += dvc

        @pl.when(n > 0)
        def _():
            before = pl.ds(pl.multiple_of((n - 1) * SPAN, SPAN), SPAN)
            dk_ref[before, :] += dkp
            dv_ref[before, :] += dvp

        start = first & (r == 0)
        for o, g in zip((db0, db1, db2, db3), (g0, g1, g2, g3)):
            @pl.when(start)
            def _(o=o, g=g):
                o[0] = g

            @pl.when(jnp.logical_not(start))
            def _(o=o, g=g):
                o[0] += g

    res = pl.pallas_call(
        body, name=name, grid=(d, nb), in_specs=[cur, prev, cur, prev, cur] + _head_table_specs() + [cur] * 3,
        out_specs=[cur, whole, whole] + [bias] * 4,
        out_shape=[jax.ShapeDtypeStruct(q.shape, f32)] * 3 + [jax.ShapeDtypeStruct((1, SPAN, 2 * SPAN), f32)] * 4,
        compiler_params=_cparams(('arbitrary', 'arbitrary')),
    )(q, k, k, v, v, *[table] * N_HEADS, do, dm, dl)
    return res[0], res[1], res[2], res[3:]


def _f_dil_merge(o1, m1, l1, o2, m2, l2, o3, m3, l3):
    mx = jnp.maximum(jnp.maximum(m1, m2), m3)
    w1, w2, w3 = jnp.exp(m1 - mx), jnp.exp(m2 - mx), jnp.exp(m3 - mx)
    return ((w1 * o1 + w2 * o2 + w3 * o3) / (w1 * l1 + w2 * l2 + w3 * l3),)


def _bias_onehot(dilation):
    qi = jnp.arange(SPAN, dtype=jnp.int32)[:, None] + SPAN
    kj = jnp.arange(2 * SPAN, dtype=jnp.int32)[None, :]
    bucket = _t5_bucket(jnp.clip(qi - kj, 0, SPAN) * dilation).reshape(-1)
    return (bucket[None, :] == jnp.arange(T5_BUCKETS, dtype=jnp.int32)[:, None]).astype(f32)


def _bias_tables(t5_t, onehot, name):
    N = onehot.shape[1]
    tn = _pick(N, (4096, 2048, 1024))

    def body(t_ref, oh_ref, o_ref):
        o_ref[...] = _hdot(t_ref[...], oh_ref[...])

    return pl.pallas_call(
        body, name=name, grid=(N // tn,),
        in_specs=[pl.BlockSpec((8, T5_BUCKETS), lambda i: (0, 0)), pl.BlockSpec((T5_BUCKETS, tn), lambda i: (0, i))],
        out_specs=pl.BlockSpec((8, tn), lambda i: (0, i)), out_shape=jax.ShapeDtypeStruct((8, N), f32),
        compiler_params=_cparams(('parallel',)),
    )(t5_t, onehot)


def _bias_tables_bwd(d_tab, onehot, name):
    N = onehot.shape[1]
    tn = _pick(N, (4096, 2048, 1024))

    def body(g_ref, oh_ref, o_ref):
        part = _hdot_nt(g_ref[...], oh_ref[...])

        @pl.when(pl.program_id(0) == 0)
        def _():
            o_ref[...] = part

        @pl.when(pl.program_id(0) > 0)
        def _():
            o_ref[...] += part

    return pl.pallas_call(
        body, name=name, grid=(N // tn,),
        in_specs=[pl.BlockSpec((8, tn), lambda i: (0, i)), pl.BlockSpec((T5_BUCKETS, tn), lambda i: (0, i))],
        out_specs=pl.BlockSpec((8, T5_BUCKETS), lambda i: (0, 0)), out_shape=jax.ShapeDtypeStruct((8, T5_BUCKETS), f32),
        compiler_params=_cparams(('arbitrary',)),
    )(d_tab, onehot)


def _by_residue(t, d):
    S, C = t.shape
    return t.reshape(S // d, d * C)


def _from_residue(t):
    return t.reshape(-1, GROUP_W)


def _dil_tables(t5_bias):
    t5_t = jnp.pad(t5_bias.T, ((0, 8 - N_HEADS), (0, 0)))
    return [_bias_tables(t5_t, _bias_onehot(d), f'dil_bias_fwd_{bi}').reshape(8, SPAN, 2 * SPAN)
            for bi, (_, d) in enumerate(DIL_PAIRS)]


def _t5_grad(d_tables):
    total = None
    for bi, (_, d) in enumerate(DIL_PAIRS):
        g = _bias_tables_bwd(d_tables[bi], _bias_onehot(d), f'dil_bias_bwd_{bi}')
        total = g if total is None else total + g
    return total[:N_HEADS].T


def _dil_fwd(qkv, mp, l):
    S = qkv.shape[0]
    tm = _pick(S, (256, 128))
    q, k, v = qkv[:, :GROUP_W], qkv[:, GROUP_W:2 * GROUP_W], qkv[:, 2 * GROUP_W:]
    pre_params = [jnp.tile(mp['dil_q_norm'], N_HEADS)[None], jnp.tile(mp['dil_k_norm'], N_HEADS)[None], _head_mean_matrix()]
    qn, kn = _tile_fwd(_f_dil_pre, [q, k], pre_params, [(GROUP_W, bf16)] * 2, tm, f'dil_pre_fwd_{l}')
    v = v.astype(bf16)
    tables = mp['dil_tables'] if 'dil_tables' in mp else _dil_tables(mp['t5_bias'])
    branches, outs = [], []
    for bi, (_, d) in enumerate(DIL_PAIRS):
        tab = tables[bi]
        qd, kd, vd = _by_residue(qn, d), _by_residue(kn, d), _by_residue(v, d)
        o, m, lsum = _dil_branch_fwd(qd, kd, vd, tab, f'dil_branch_fwd_{l}_{bi}')
        branches.append((qd, kd, vd, tab))
        outs += [_from_residue(o), _from_residue(m), _from_residue(lsum)]
    (y,) = _tile_fwd(_f_dil_merge, outs, [], [(GROUP_W, f32)], tm, f'dil_merge_fwd_{l}')
    return y, (q, k, pre_params, branches, outs)


def _dil_bwd(dy, saved, l):
    q, k, pre_params, branches, outs = saved
    S = dy.shape[0]
    tm = _pick(S, (256, 128))
    douts, _ = _tile_bwd(_f_dil_merge, outs, [], [dy], [True] * 9, [], tm, f'dil_merge_bwd_{l}')
    dqn = dkn = dv = None
    d_tabs = []
    for bi, (_, d) in enumerate(DIL_PAIRS):
        qd, kd, vd, tab = branches[bi]
        do, dm, dl = [_by_residue(t, d) for t in douts[3 * bi:3 * bi + 3]]
        dq_b, dk_b, dv_b, dbias = _dil_branch_bwd(qd, kd, vd, tab, do, dm, dl, f'dil_branch_bwd_{l}_{bi}')
        d_tabs.append(jnp.concatenate([*dbias, jnp.zeros((8 - N_HEADS, SPAN, 2 * SPAN), f32)], axis=0).reshape(8, -1))
        dq_b, dk_b, dv_b = _from_residue(dq_b), _from_residue(dk_b), _from_residue(dv_b)
        dqn = dq_b if dqn is None else dqn + dq_b
        dkn = dk_b if dkn is None else dkn + dk_b
        dv = dv_b if dv is None else dv + dv_b
    (dq, dk), (dgq, dgk) = _tile_bwd(_f_dil_pre, [q, k], pre_params, [dqn, dkn], [True, True], [True, True, False], tm,
                                     f'dil_pre_bwd_{l}')
    grads = {'dil_q_norm': dgq.reshape(N_HEADS, HEAD_DIM).sum(0), 'dil_k_norm': dgk.reshape(N_HEADS, HEAD_DIM).sum(0),
             't5_tables': d_tabs}
    return jnp.concatenate([dq, dk, dv], axis=1), grads


S5_LANES = S5_G * S5_P
SCAN_SEGMENTS = 8
SCAN_W = 256


def _f_s5_prep(bre, bim, lr, li, logdt_col, expand):
    dt = jnp.sum(jnp.exp(logdt_col) * expand, axis=0, keepdims=True)
    mag = jnp.exp(lr * dt)
    ar, ai = mag * jnp.cos(li * dt), mag * jnp.sin(li * dt)
    den = lr * lr + li * li
    nr, ni = ar - 1.0, ai
    zr = (nr * lr + ni * li) / den
    zi = (ni * lr - nr * li) / den
    bb = jnp.concatenate([zr * bre - zi * bim, zr * bim + zi * bre], axis=1)
    a_rows = jnp.broadcast_to(jnp.concatenate([ar, ai], axis=1), bb.shape)
    return bb, a_rows


def _s5_scan(x, a_rows, name, reverse=False, h=None):
    S = x.shape[0]
    NL = x.shape[1] // 2
    T = S // SCAN_SEGMENTS
    nblk = NL // SCAN_W
    n_in = 4 if reverse else 2

    def body(*refs):
        if reverse:
            (x_hbm, pr_hbm, pi_hbm, ar_ref, ai_ref, hr_hbm, hi_hbm, dar_ref, dai_ref,
             xr_s, xi_s, pr_s, pi_s, hr_s, hi_s, in_sems, out_sems) = refs
        else:
            x_hbm, ar_ref, ai_ref, hr_hbm, hi_hbm, xr_s, xi_s, hr_s, hi_s, in_sems, out_sems = refs
        col = pl.multiple_of(pl.program_id(0) * SCAN_W, SCAN_W)
        loads = []
        for k in range(SCAN_SEGMENTS):
            rows = pl.ds(k * T, T)
            sources = [(x_hbm, col, xr_s), (x_hbm, NL + col, xi_s)]
            if reverse:
                sources += [(pr_hbm, col, pr_s), (pi_hbm, col, pi_s)]
            for i, (src, c0, dst) in enumerate(sources):
                loads.append(pltpu.make_async_copy(src.at[rows, pl.ds(c0, SCAN_W)], dst.at[:, k, :],
                                                   in_sems.at[i * SCAN_SEGMENTS + k]))
        for cp in loads:
            cp.start()
        for cp in loads:
            cp.wait()
        ar = ar_ref[...]
        ai = -ai_ref[...] if reverse else ai_ref[...]
        zero = jnp.zeros((SCAN_SEGMENTS, SCAN_W), f32)

        def at(s):
            return T - 1 - s if reverse else s

        def local(s, c):
            hr, hi, pr, pi = c
            j = at(s)
            nhr = ar * hr - ai * hi + xr_s[j]
            nhi = ar * hi + ai * hr + xi_s[j]
            hr_s[j] = nhr
            hi_s[j] = nhi
            return nhr, nhi, ar * pr - ai * pi, ar * pi + ai * pr

        er, ei, pr, pi = lax.fori_loop(0, T, local, (zero, zero, zero + 1.0, zero), unroll=2)
        row = lax.broadcasted_iota(jnp.int32, (SCAN_SEGMENTS, SCAN_W), 0)
        cr, ci = zero, zero
        order = range(SCAN_SEGMENTS - 2, -1, -1) if reverse else range(1, SCAN_SEGMENTS)
        for k in order:
            src = k + 1 if reverse else k - 1
            tr = er + pr * cr - pi * ci
            ti = ei + pr * ci + pi * cr
            cr = jnp.where(row == k, jnp.sum(jnp.where(row == src, tr, 0.0), axis=0, keepdims=True), cr)
            ci = jnp.where(row == k, jnp.sum(jnp.where(row == src, ti, 0.0), axis=0, keepdims=True), ci)

        def fix_at(j, c, before):
            pr, pi, sr, si = c
            pr, pi = ar * pr - ai * pi, ar * pi + ai * pr
            hr = hr_s[j] + pr * cr - pi * ci
            hi = hi_s[j] + pr * ci + pi * cr
            hr_s[j] = hr
            hi_s[j] = hi
            if reverse:
                qr, qi = before
                sr = sr + hr * qr + hi * qi
                si = si + hi * qr - hr * qi
            return pr, pi, sr, si

        start = (zero + 1.0, zero, zero, zero)
        if reverse:
            def fix(s, c):
                j = T - 1 - s
                return fix_at(j, c, (pr_s[j - 1], pi_s[j - 1]))

            c = lax.fori_loop(0, T - 1, fix, start, unroll=2)
            last_r = jnp.where(row == 0, 0.0, pltpu.roll(pr_s[T - 1], 1, 0))
            last_i = jnp.where(row == 0, 0.0, pltpu.roll(pi_s[T - 1], 1, 0))
            _, _, sr, si = fix_at(0, c, (last_r, last_i))
            dar_ref[...] = sr
            dai_ref[...] = si
        else:
            lax.fori_loop(0, T, lambda s, c: fix_at(s, c, None), start, unroll=2)
        stores = []
        for k in range(SCAN_SEGMENTS):
            rows = pl.ds(k * T, T)
            stores.append(pltpu.make_async_copy(hr_s.at[:, k, :], hr_hbm.at[rows, pl.ds(col, SCAN_W)], out_sems.at[k]))
            stores.append(pltpu.make_async_copy(hi_s.at[:, k, :], hi_hbm.at[rows, pl.ds(col, SCAN_W)],
                                                out_sems.at[SCAN_SEGMENTS + k]))
        for cp in stores:
            cp.start()
        for cp in stores:
            cp.wait()

    a_re = pl.BlockSpec((SCAN_SEGMENTS, SCAN_W), lambda b: (0, b))
    a_im = pl.BlockSpec((SCAN_SEGMENTS, SCAN_W), lambda b: (0, nblk + b))
    seq = pltpu.VMEM((T, SCAN_SEGMENTS, SCAN_W), f32)
    if reverse:
        in_specs, args = [ANY, ANY, ANY, a_re, a_im], [x, h[0], h[1], a_rows, a_rows]
        out_specs = [ANY, ANY, a_re, a_re]
        out_shape = [jax.ShapeDtypeStruct((S, NL), f32)] * 2 + [jax.ShapeDtypeStruct((SCAN_SEGMENTS, NL), f32)] * 2
    else:
        in_specs, args = [ANY, a_re, a_im], [x, a_rows, a_rows]
        out_specs = [ANY, ANY]
        out_shape = [jax.ShapeDtypeStruct((S, NL), f32)] * 2
    scratch = [seq] * (n_in + 2) + [pltpu.SemaphoreType.DMA((n_in * SCAN_SEGMENTS,)),
                                    pltpu.SemaphoreType.DMA((2 * SCAN_SEGMENTS,))]
    return pl.pallas_call(body, name=name, grid=(nblk,), in_specs=in_specs, out_specs=out_specs, out_shape=out_shape,
                          scratch_shapes=scratch, compiler_params=_cparams(('arbitrary',), big=True))(*args)


def _f_s5_post(y, u, d, w_glu):
    z = _bmm(y + d * u, w_glu)
    return (z[:, :GROUP_W] * jax.nn.sigmoid(z[:, GROUP_W:]),)


def _block_diag(t):
    G, a, b = t.shape
    eye = jnp.eye(G, dtype=t.dtype)
    return (t[:, :, None, :] * eye[:, None, :, None]).reshape(G * a, G * b)


def _diag_blocks(m, a, b):
    G = m.shape[0] // a
    return jnp.moveaxis(jnp.diagonal(m.reshape(G, a, G, b), axis1=0, axis2=2), -1, 0)


def _s5_fwd(u, mp, l):
    S = u.shape[0]
    tm = _pick(S, (256, 128))
    bre = _block_diag(mp['s5_b_re'].transpose(0, 2, 1))
    bim = _block_diag(mp['s5_b_im'].transpose(0, 2, 1))
    expand = jnp.repeat(jnp.eye(S5_G, dtype=f32), S5_P, axis=1)
    prep_params = [mp['s5_lambda_re'].reshape(1, S5_LANES), mp['s5_lambda_im'].reshape(1, S5_LANES),
                   mp['s5_log_dt'].reshape(S5_G, 1), expand]
    bb, a_rows = _tile_fwd(_f_s5_prep, [bre, bim], prep_params, [(2 * S5_LANES, f32)] * 2, GROUP_W, f's5_prep_fwd_{l}')
    x = _mm(u, bb, 'nn', f's5_in_fwd_{l}')
    hr, hi = _s5_scan(x, a_rows, f's5_scan_fwd_{l}')
    c_re, c_im = _block_diag(mp['s5_c_re'].transpose(0, 2, 1)), -_block_diag(mp['s5_c_im'].transpose(0, 2, 1))
    y = _mm(hi, c_im, 'nn', f's5_out_im_fwd_{l}', add=_mm(hr, c_re, 'nn', f's5_out_re_fwd_{l}'))
    post_params = [mp['s5_d'][None], mp['s5_w_glu']]
    (out,) = _tile_fwd(_f_s5_post, [y, u], post_params, [(GROUP_W, f32)], tm, f's5_post_fwd_{l}')
    return out, (u, bre, bim, prep_params, bb, a_rows, hr, hi, c_re, c_im, y, post_params)


def _s5_bwd(dout, saved, l):
    u, bre, bim, prep_params, bb, a_rows, hr, hi, c_re, c_im, y, post_params = saved
    S = u.shape[0]
    tm = _pick(S, (256, 128))
    (dy, du1), (dd, dwglu) = _tile_bwd(_f_s5_post, [y, u], post_params, [dout], [True, True], [True, True], tm,
                                       f's5_post_bwd_{l}')
    ccat = jnp.concatenate([c_re, c_im], axis=0)
    dh = _mm(dy, ccat, 'nt', f's5_out_dx_{l}')
    dccat = jnp.concatenate([_mm(hr, dy, 'tn', f's5_out_re_dw_{l}'), _mm(hi, dy, 'tn', f's5_out_im_dw_{l}')], axis=0)
    lr_, li_, dar, dai = _s5_scan(dh, a_rows, f's5_scan_bwd_{l}', reverse=True, h=(hr, hi))
    du2 = _mm(li_, bb[:, S5_LANES:], 'nt', f's5_in_im_dx_{l}', add=_mm(lr_, bb[:, :S5_LANES], 'nt', f's5_in_re_dx_{l}'))
    dbb = jnp.concatenate([_mm(u, lr_, 'tn', f's5_in_re_dw_{l}'), _mm(u, li_, 'tn', f's5_in_im_dw_{l}')], axis=1)
    da_rows = jnp.pad(jnp.concatenate([dar, dai], axis=1), ((0, GROUP_W - SCAN_SEGMENTS), (0, 0)))
    (dbre, dbim), (dlr, dli, dlogdt) = _tile_bwd(_f_s5_prep, [bre, bim], prep_params, [dbb, da_rows], [True, True],
                                                 [True, True, True, False], GROUP_W, f's5_prep_bwd_{l}')
    grads = {
        's5_lambda_re': dlr.reshape(S5_G, S5_P), 's5_lambda_im': dli.reshape(S5_G, S5_P), 's5_log_dt': dlogdt[:, 0],
        's5_b_re': _diag_blocks(dbre, S5_CG, S5_P).transpose(0, 2, 1),
        's5_b_im': _diag_blocks(dbim, S5_CG, S5_P).transpose(0, 2, 1),
        's5_c_re': _diag_blocks(dccat[:S5_LANES], S5_P, S5_CG).transpose(0, 2, 1),
        's5_c_im': -_diag_blocks(dccat[S5_LANES:], S5_P, S5_CG).transpose(0, 2, 1),
        's5_d': dd[0], 's5_w_glu': dwglu}
    return du1 + du2, grads


DN_CONV = 4


def _head_sum_matrix():
    h = np.arange(GROUP_W) // HEAD_DIM
    return jnp.asarray((h[:, None] == h[None, :]).astype(np.float32))


def _f_dn_pre(x0, x1, x2, x3, ab, w0, w1, w2, w3, alog, dtb, ea, eb, hs):
    c = w0 * x0 + w1 * x1 + w2 * x2 + w3 * x3
    s = c * jax.nn.sigmoid(c)
    q, k, v = s[:, :GROUP_W], s[:, GROUP_W:2 * GROUP_W], s[:, 2 * GROUP_W:]
    q = q * lax.rsqrt(_hdot(q * q, hs) + EPS) * (HEAD_DIM ** -0.5)
    k = k * lax.rsqrt(_hdot(k * k, hs) + EPS)
    beta = jax.nn.sigmoid(_hdot(ab, eb))
    g = -jnp.exp(alog) * jax.nn.softplus(_hdot(ab, ea) + dtb)
    return q, k, v, g, beta


DN_CHUNKS_PER_STEP = 4


def _f_dn_chunks(q, k, v, g, beta):
    C = DN_CHUNK
    n_chunks = q.shape[0] // C
    r = lax.broadcasted_iota(jnp.int32, (C, C), 0)
    c = lax.broadcasted_iota(jnp.int32, (C, C), 1)
    causal, strict = r >= c, r > c
    eye = (r == c).astype(f32)
    tril = causal.astype(f32)
    ones = jnp.ones((C, GROUP_W), f32)
    masks = [_head_mask(h) for h in range(N_HEADS)]
    rows = [tuple(t[i * C:(i + 1) * C] for t in (q, k, v, g, beta)) for i in range(n_chunks)]
    gcs = [_hdot(tril, gi) for (_, _, _, gi, _) in rows]
    items = [(i, h) for i in range(n_chunks) for h in range(N_HEADS)]
    grows = [_hdot_nt(ones * (masks[h] * (1.0 / HEAD_DIM)), gcs[i]) for i, h in items]
    decs = []
    for (i, h), grow in zip(items, grows):
        gcol = jnp.sum(gcs[i] * masks[h], axis=1, keepdims=True) * (1.0 / HEAD_DIM)
        decs.append(jnp.exp(jnp.where(causal, gcol - grow, NEG_INF)))
    kbs = [ki * bi for (_, ki, _, _, bi) in rows]
    kks = [_bmm_nt(kbs[i] * masks[h], rows[i][1]) for i, h in items]
    qks = [_bmm_nt(rows[i][0] * masks[h], rows[i][1]) for i, h in items]
    lmats = [jnp.where(strict, kk * dec, 0.0) for kk, dec in zip(kks, decs)]
    a_qk = [jnp.where(causal, qk * dec, 0.0) for qk, dec in zip(qks, decs)]
    ts = [eye - lm for lm in lmats]
    ps = lmats
    for _ in range(5):
        ps = [_bmm(p, p) for p in ps]
        ts = [t + _bmm(t, p) for t, p in zip(ts, ps)]
    egs = [jnp.exp(gc) for gc in gcs]
    tw = [_bmm(t, kbs[i] * egs[i]) for (i, h), t in zip(items, ts)]
    tu = [_bmm(t, rows[i][2] * rows[i][4]) for (i, h), t in zip(items, ts)]
    outs = []
    for i in range(n_chunks):
        qi, ki, _, gi, _ = rows[i]
        glast = jnp.sum(gi, axis=0, keepdims=True)
        w = sum(tw[i * N_HEADS + h] * masks[h] for h in range(N_HEADS))
        u = sum(tu[i * N_HEADS + h] * masks[h] for h in range(N_HEADS))
        outs.append((w, u, qi * egs[i], ki * jnp.exp(glast - gcs[i]), *a_qk[i * N_HEADS:(i + 1) * N_HEADS],
                     jnp.broadcast_to(jnp.exp(glast), (C, GROUP_W))))
    return tuple(jnp.concatenate(parts, axis=0) for parts in zip(*outs))


def _f_dn_step(w, u, qd, kdec, a0, a1, a2, a3, dfull, state, bd):
    row0 = (lax.broadcasted_iota(jnp.int32, dfull.shape, 0) == 0).astype(f32)
    dvec = jnp.sum(dfull * row0, axis=0, keepdims=True)
    ws, qs = _bmm(w, state), _bmm(qd, state)
    vnew = u - ws
    avs = [_bmm(a, vnew) for a in (a0, a1, a2, a3)]
    kv = _bmm_tn(kdec, vnew)
    o = qs + sum(av * _head_mask(h) for h, av in enumerate(avs))
    return o, state * dvec + bd * kv


def _dn_scan_fwd(ins, name):
    S = ins[0].shape[0]
    N = S // DN_CHUNK
    bd = _head_sum_matrix()

    def body(*refs):
        o_ref, s_ref, state = refs[10], refs[11], refs[12]

        @pl.when(pl.program_id(0) == 0)
        def _():
            state[...] = jnp.zeros_like(state)

        s_in = state[...]
        s_ref[0] = s_in
        o, s_out = _f_dn_step(*[r[...] for r in refs[:9]], s_in, refs[9][...])
        o_ref[...] = o
        state[...] = s_out

    return pl.pallas_call(
        body, name=name, grid=(N,),
        in_specs=[pl.BlockSpec((DN_CHUNK, t.shape[1]), lambda n: (n, 0)) for t in ins] + [_full_spec(bd)],
        out_specs=[pl.BlockSpec((DN_CHUNK, GROUP_W), lambda n: (n, 0)), pl.BlockSpec((1, GROUP_W, GROUP_W), lambda n: (n, 0, 0))],
        out_shape=[jax.ShapeDtypeStruct((S, GROUP_W), f32), jax.ShapeDtypeStruct((N, GROUP_W, GROUP_W), f32)],
        scratch_shapes=[pltpu.VMEM((GROUP_W, GROUP_W), f32)],
        compiler_params=_cparams(('arbitrary',)),
    )(*ins, bd)


def _dn_scan_bwd(ins, states, do, name):
    S = ins[0].shape[0]
    N = S // DN_CHUNK
    bd = _head_sum_matrix()

    def body(*refs):
        s_ref, do_ref = refs[9], refs[10]
        bd_ref = refs[11]
        outs = refs[12:21]
        dstate = refs[21]

        @pl.when(pl.program_id(0) == 0)
        def _():
            dstate[...] = jnp.zeros_like(dstate)

        bd_val = bd_ref[...]
        _, vjp = jax.vjp(lambda *a: _f_dn_step(*a, bd_val), *[r[...] for r in refs[:9]], s_ref[0])
        grads = vjp((do_ref[...], dstate[...]))
        for o, g in zip(outs, grads[:9]):
            o[...] = g
        dstate[...] = grads[9]

    def rev(n):
        return (N - 1 - n, 0)

    res = pl.pallas_call(
        body, name=name, grid=(N,),
        in_specs=[pl.BlockSpec((DN_CHUNK, t.shape[1]), rev) for t in ins] +
                 [pl.BlockSpec((1, GROUP_W, GROUP_W), lambda n: (N - 1 - n, 0, 0)), pl.BlockSpec((DN_CHUNK, GROUP_W), rev),
                  _full_spec(bd)],
        out_specs=[pl.BlockSpec((DN_CHUNK, t.shape[1]), rev) for t in ins],
        out_shape=[jax.ShapeDtypeStruct(t.shape, f32) for t in ins],
        scratch_shapes=[pltpu.VMEM((GROUP_W, GROUP_W), f32)],
        compiler_params=_cparams(('arbitrary',)),
    )(*ins, states, do, bd)
    return list(res)


def _f_dn_post(o, gate, gain, hmean):
    return (o * lax.rsqrt(_hdot(o * o, hmean) + EPS) * gain * (gate * jax.nn.sigmoid(gate)),)


def _dn_delays(x, name):
    S, C = x.shape
    tm = _pick(S, (256, 128))

    def body(prev_ref, cur_ref, *outs):
        before = jnp.where(pl.program_id(0) > 0, prev_ref[...], 0.0)
        both = jnp.concatenate([before, cur_ref[...]], axis=0)
        for o, k in zip(outs, range(DN_CONV - 1, 0, -1)):
            o[...] = pltpu.roll(both, k, 0)[tm:]

    spec = pl.BlockSpec((tm, C), lambda i: (i, 0))
    return pl.pallas_call(
        body, name=name, grid=(S // tm,),
        in_specs=[pl.BlockSpec((tm, C), lambda i: (jnp.maximum(i - 1, 0), 0)), spec],
        out_specs=[spec] * (DN_CONV - 1), out_shape=[jax.ShapeDtypeStruct((S, C), x.dtype)] * (DN_CONV - 1),
        compiler_params=_cparams(('parallel',), big=True),
    )(x, x)


def _dn_undelay_sum(ds, name):
    S, C = ds[0].shape
    tm = _pick(S, (256, 128))
    n = S // tm

    def body(*refs):
        o = refs[-1]
        total = refs[2 * (DN_CONV - 1)][...]
        for j in range(DN_CONV - 1):
            k = DN_CONV - 1 - j
            after = jnp.where(pl.program_id(0) < n - 1, refs[2 * j + 1][...], 0.0)
            both = jnp.concatenate([refs[2 * j][...], after], axis=0)
            total = total + pltpu.roll(both, 2 * tm - k, 0)[:tm]
        o[...] = total

    spec = pl.BlockSpec((tm, C), lambda i: (i, 0))
    nxt = pl.BlockSpec((tm, C), lambda i: (jnp.minimum(i + 1, n - 1), 0))
    args, in_specs = [], []
    for j in range(DN_CONV - 1):
        args += [ds[j], ds[j]]
        in_specs += [spec, nxt]
    return pl.pallas_call(
        body, name=name, grid=(n,), in_specs=in_specs + [spec], out_specs=spec,
        out_shape=jax.ShapeDtypeStruct((S, C), f32), compiler_params=_cparams(('parallel',), big=True),
    )(*args, ds[DN_CONV - 1])


def _dn_fwd(qkv, a, b, gate, mp, l):
    S = qkv.shape[0]
    tm = _pick(S, (256, 128))
    xs = [*_dn_delays(qkv, f'dn_delay_{l}'), qkv]
    ab = jnp.pad(jnp.concatenate([a, b], axis=1), ((0, 0), (0, LANES - 2 * N_HEADS)))
    sel = np.zeros((2, LANES, GROUP_W), np.float32)
    for h in range(N_HEADS):
        sel[0, h, h * HEAD_DIM:(h + 1) * HEAD_DIM] = 1.0
        sel[1, N_HEADS + h, h * HEAD_DIM:(h + 1) * HEAD_DIM] = 1.0
    pre_params = [*[mp['dn_conv'][j][None] for j in range(DN_CONV)], jnp.repeat(mp['dn_a_log'], HEAD_DIM)[None],
                  jnp.repeat(mp['dn_dt_bias'], HEAD_DIM)[None], jnp.asarray(sel[0]), jnp.asarray(sel[1]), _head_sum_matrix()]
    pre = _tile_fwd(_f_dn_pre, [*xs, ab], pre_params, [(GROUP_W, f32)] * 5, tm, f'dn_pre_fwd_{l}')
    chunk_outs = [(GROUP_W, f32)] * 4 + [(HEAD_DIM, f32)] * 4 + [(GROUP_W, f32)]
    parts = _tile_fwd(_f_dn_chunks, pre, [], chunk_outs, DN_CHUNK * DN_CHUNKS_PER_STEP, f'dn_chunk_fwd_{l}')
    o, states = _dn_scan_fwd(parts, f'dn_scan_fwd_{l}')
    post_params = [jnp.tile(mp['dn_o_norm'], N_HEADS)[None], _head_mean_matrix()]
    (y,) = _tile_fwd(_f_dn_post, [o, gate], post_params, [(GROUP_W, f32)], tm, f'dn_post_fwd_{l}')
    return y, (xs, ab, pre_params, pre, parts, states, o, gate, post_params)


def _dn_bwd(dy, saved, l):
    xs, ab, pre_params, pre, parts, states, o, gate, post_params = saved
    S = dy.shape[0]
    tm = _pick(S, (256, 128))
    (do, dgate), (dgain,) = _tile_bwd(_f_dn_post, [o, gate], post_params, [dy], [True, True], [True, False], tm,
                                      f'dn_post_bwd_{l}')
    dparts = _dn_scan_bwd(parts, states, do, f'dn_scan_bwd_{l}')
    dpre, _ = _tile_bwd(_f_dn_chunks, pre, [], dparts, [True] * 5, [], DN_CHUNK * DN_CHUNKS_PER_STEP, f'dn_chunk_bwd_{l}')
    dins, dpar = _tile_bwd(_f_dn_pre, [*xs, ab], pre_params, dpre, [True] * 5, [True] * 6 + [False] * 3, tm,
                           f'dn_pre_bwd_{l}')
    dqkv = _dn_undelay_sum(dins[:DN_CONV], f'dn_undelay_{l}')
    dab = dins[DN_CONV]
    grads = {'dn_conv': jnp.concatenate(dpar[:DN_CONV], axis=0),
             'dn_a_log': dpar[4].reshape(N_HEADS, HEAD_DIM).sum(1), 'dn_dt_bias': dpar[5].reshape(N_HEADS, HEAD_DIM).sum(1),
             'dn_o_norm': dgain.reshape(N_HEADS, HEAD_DIM).sum(0)}
    return dqkv, dab[:, :N_HEADS], dab[:, N_HEADS:2 * N_HEADS], dgate, grads


def _t5_bucket(dist):
    exact = T5_BUCKETS // 2
    df = jnp.maximum(dist, 1).astype(f32)
    large = exact + (jnp.log(df / exact) / math.log(T5_MAX_DIST / exact) * (T5_BUCKETS - exact)).astype(jnp.int32)
    large = jnp.minimum(large, T5_BUCKETS - 1)
    return jnp.where(dist < exact, dist, large)


def _split_cols(t, sizes):
    out, start = [], 0
    for s in sizes:
        out.append(t[..., start:start + s])
        start += s
    return out


def _mixers_fwd(proj, mp, l):
    c_q, c_kv, k_rope, u_s5, qkv_dil, qkv_dn, a_dn, b_dn, gate_dn = _split_cols(proj, IN_SPLITS)
    y_mla, s_mla = _mla_fwd(c_q, c_kv, k_rope, mp, l)
    y_s5, s_s5 = _s5_fwd(u_s5, mp, l)
    y_dil, s_dil = _dil_fwd(qkv_dil, mp, l)
    y_dn, s_dn = _dn_fwd(qkv_dn, a_dn, b_dn, gate_dn, mp, l)
    return jnp.concatenate([y_mla, y_s5, y_dil, y_dn], axis=-1), (s_mla, s_s5, s_dil, s_dn)


def _mixers_bwd(dmixed, saved, l):
    s_mla, s_s5, s_dil, s_dn = saved
    d_mla, d_s5, d_dil, d_dn = _split_cols(dmixed, (GROUP_W,) * 4)
    dc_q, dc_kv, dk_rope, g_mla = _mla_bwd(d_mla, s_mla, l)
    du, g_s5 = _s5_bwd(d_s5, s_s5, l)
    dqkv_dil, g_dil = _dil_bwd(d_dil, s_dil, l)
    dqkv_dn, da, db, dgate, g_dn = _dn_bwd(d_dn, s_dn, l)
    parts = [dc_q, dc_kv, dk_rope, du, dqkv_dil, dqkv_dn, da, db, dgate]
    dproj = jnp.concatenate([p.astype(bf16) for p in parts], axis=-1)
    return dproj, {**g_mla, **g_s5, **g_dil, **g_dn}


MIXER_PARAMS = ['mla_q_norm', 'mla_kv_norm', 'mla_w_uq', 'mla_w_ukv', 'mla_qk_q', 'mla_qk_k', 's5_lambda_re',
                's5_lambda_im', 's5_log_dt', 's5_b_re', 's5_b_im', 's5_c_re', 's5_c_im', 's5_d', 's5_w_glu',
                'dil_q_norm', 'dil_k_norm', 't5_bias', 'dn_conv', 'dn_a_log', 'dn_dt_bias', 'dn_o_norm']


def _layer_fwd_mix(h, W, l):
    S = h.shape[0]
    tm = _pick(S, (256, 128))
    g1 = W['attn_norm'][l][None]
    (n1,) = _tile_fwd(_f_rms, [h], [g1], [(D_MODEL, bf16)], tm, f'rms1_fwd_{l}')
    proj = _mm(n1, W['w_in'][l], 'nn', f'proj_fwd_{l}')
    mp = {k: (W[k] if k == 't5_bias' else W[k][l]).astype(f32) for k in MIXER_PARAMS}
    if 'dil_tables' in W:
        mp['dil_tables'] = W['dil_tables']
    mixed, mix_saved = _mixers_fwd(proj, mp, l)
    mixed_b = mixed.astype(bf16)
    h2 = _mm(mixed_b, W['w_out'][l], 'nn', f'out_fwd_{l}', add=h)
    return h2, dict(h=h, n1=n1, mix=mix_saved, mixed=mixed_b, h2=h2)


def _layer_fwd_ffn(h2, W, l, saved):
    S = h2.shape[0]
    tm = _pick(S, (256, 128))
    g2 = W['ffn_norm'][l][None]
    (n2,) = _tile_fwd(_f_rms, [h2], [g2], [(D_MODEL, bf16)], tm, f'rms2_fwd_{l}')
    u = _mm(n2, W['ffn_w1'][l], 'nt', f'ffn1_fwd_{l}')
    v = _mm(n2, W['ffn_w3'][l], 'nt', f'ffn3_fwd_{l}')
    (act,) = _tile_fwd(_f_swiglu, [u, v], [], [(FFN_HIDDEN, bf16)], tm, f'swiglu_fwd_{l}')
    h3 = _mm(act, W['ffn_w2'][l], 'nn', f'ffn2_fwd_{l}', add=h2)
    saved.update(n2=n2, u=u, v=v, act=act)
    return h3


def _layer_bwd_ffn(dh3, saved, W, l):
    S = dh3.shape[0]
    tm = _pick(S, (256, 128))
    g2 = W['ffn_norm'][l][None]
    grads = {}
    dact = _mm(dh3, W['ffn_w2'][l], 'nt', f'ffn2_dx_{l}')
    grads['ffn_w2'] = _mm(saved['act'], dh3, 'tn', f'ffn2_dw_{l}', out_dtype=bf16)
    (du, dv), _ = _tile_bwd(_f_swiglu, [saved['u'], saved['v']], [], [dact], [True, True], [], tm, f'swiglu_bwd_{l}',
                            dt_dtypes=[bf16, bf16])
    dn2 = _mm(dv, W['ffn_w3'][l], 'nn', f'ffn3_dx_{l}', add=_mm(du, W['ffn_w1'][l], 'nn', f'ffn1_dx_{l}'))
    grads['ffn_w1'] = _mm(du, saved['n2'], 'tn', f'ffn1_dw_{l}', out_dtype=bf16)
    grads['ffn_w3'] = _mm(dv, saved['n2'], 'tn', f'ffn3_dw_{l}', out_dtype=bf16)
    (dh2n,), (dg2,) = _tile_bwd(_f_rms, [saved['h2']], [g2], [dn2], [True], [True], tm, f'rms2_bwd_{l}')
    grads['ffn_norm'] = dg2[0]
    return (dh3, dh2n), grads


def _layer_bwd_mix(dh2, saved, W, l):
    S = dh2.shape[0]
    tm = _pick(S, (256, 128))
    g1 = W['attn_norm'][l][None]
    grads = {}
    dmixed = _mm(dh2, W['w_out'][l], 'nt', f'out_dx_{l}')
    grads['w_out'] = _mm(saved['mixed'], dh2, 'tn', f'out_dw_{l}', out_dtype=bf16)
    dproj, dmp = _mixers_bwd(dmixed, saved['mix'], l)
    grads.update(dmp)
    dn1 = _mm(dproj, W['w_in'][l], 'nt', f'proj_dx_{l}')
    grads['w_in'] = _mm(saved['n1'], dproj, 'tn', f'proj_dw_{l}', out_dtype=bf16)
    (dh1n,), (dg1,) = _tile_bwd(_f_rms, [saved['h']], [g1], [dn1], [True], [True], tm, f'rms1_bwd_{l}')
    grads['attn_norm'] = dg1[0]
    return (dh2, dh1n), grads


def kernel(x, attn_norm, w_in, w_out, mla_q_norm, mla_kv_norm, mla_w_uq, mla_w_ukv, mla_qk_q, mla_qk_k, s5_lambda_re, s5_lambda_im, s5_log_dt, s5_b_re, s5_b_im, s5_c_re, s5_c_im, s5_d, s5_w_glu, dil_q_norm, dil_k_norm, t5_bias, dn_conv, dn_a_log, dn_dt_bias, dn_o_norm, ffn_norm, ffn_w1, ffn_w3, ffn_w2, loss_target, m_attn_norm, m_w_in, m_w_out, m_mla_q_norm, m_mla_kv_norm, m_mla_w_uq, m_mla_w_ukv, m_mla_qk_q, m_mla_qk_k, m_s5_lambda_re, m_s5_lambda_im, m_s5_log_dt, m_s5_b_re, m_s5_b_im, m_s5_c_re, m_s5_c_im, m_s5_d, m_s5_w_glu, m_dil_q_norm, m_dil_k_norm, m_t5_bias, m_dn_conv, m_dn_a_log, m_dn_dt_bias, m_dn_o_norm, m_ffn_norm, m_ffn_w1, m_ffn_w3, m_ffn_w2, v_attn_norm, v_w_in, v_w_out, v_mla_q_norm, v_mla_kv_norm, v_mla_w_uq, v_mla_w_ukv, v_mla_qk_q, v_mla_qk_k, v_s5_lambda_re, v_s5_lambda_im, v_s5_log_dt, v_s5_b_re, v_s5_b_im, v_s5_c_re, v_s5_c_im, v_s5_d, v_s5_w_glu, v_dil_q_norm, v_dil_k_norm, v_t5_bias, v_dn_conv, v_dn_a_log, v_dn_dt_bias, v_dn_o_norm, v_ffn_norm, v_ffn_w1, v_ffn_w3, v_ffn_w2):
    given = dict(locals())
    def seen(n, t):
        return jnp.swapaxes(t, 1, 2) if n in TRANSPOSED else t

    w_loc = {n: seen(n, given[n]) for n in WEIGHTS}
    m_loc = {n: seen(n, given['m_' + n]) for n in WEIGHTS}
    v_loc = {n: seen(n, given['v_' + n]) for n in WEIGHTS}
    big_names = list(BIG)

    own = 2 * lax.axis_index('x') + lax.axis_index('y')
    groups = [[(n, 0) for n in GATHER_FIRST], [(n, 0) for n in GATHER_FFN], [(n, 1) for n in big_names]]
    started, order = [], jnp.zeros((8, LANES), f32)
    for gi, group in enumerate(groups):
        blocks = [w_loc[n][l].astype(bf16) for n, l in group]
        lands = [lax.empty((N_SHARDS,) + b.shape, bf16) for b in blocks]
        send_sems, recv_sems, blocks, lands, order = _to_chips_start(blocks, lands, False, order, f'gather_start_{gi}')
        started.append((send_sems, recv_sems, blocks, lands))
    W = {n: [None] * DEPTH for n in big_names}
    for n in SMALL:
        W[n] = w_loc[n]
    W['dil_tables'] = _dil_tables(w_loc['t5_bias'])

    def arrive(gi, after):
        send_sems, recv_sems, blocks, lands = started[gi]
        blocks, lands = _to_chips_wait(send_sems, recv_sems, blocks, lands, False, after, f'gather_wait_{gi}')
        for (n, l), block, land in zip(groups[gi], blocks, lands):
            W[n][l] = _from_shards(n, lax.dynamic_update_slice(land, block[None], (own, 0, 0)))

    arrive(0, order)
    h = x[0]
    saved = []
    for l in range(DEPTH):
        h2, sv = _layer_fwd_mix(h, W, l)
        if l == 0:
            arrive(1, h2)
        h = _layer_fwd_ffn(h2, W, l, sv)
        if l == 0:
            arrive(2, h)
        saved.append(sv)
    parts_loss, dh = _loss_head(h, loss_target[0])
    local_loss = jnp.sum(parts_loss)

    layer_grads = [dict() for _ in range(DEPTH)]
    sent = []

    def send(group, tag):
        srcs = [_by_shard(n, layer_grads[l][n]).astype(bf16) for n, l in group]
        lands = [lax.empty((3,) + s.shape[1:], bf16) for s in srcs]
        send_sems, recv_sems, srcs, lands, token = _to_chips_start(srcs, lands, True, jnp.zeros((8, LANES), f32),
                                                                   f'reduce_start_{tag}')
        sent.append((group, tag, send_sems, recv_sems, srcs, lands))
        return token[0, 0]

    for l in reversed(range(DEPTH)):
        (dh3, dh2n), g_ffn = _layer_bwd_ffn(dh, saved[l], W, l)
        layer_grads[l].update(g_ffn)
        dh2 = dh3 + dh2n
        if l == 0:
            dh2 = dh2 + send([(n, 0) for n in GATHER_FFN], 'ffn0')
        (dh2, dh1n), g_mix = _layer_bwd_mix(dh2, saved[l], W, l)
        layer_grads[l].update(g_mix)
        dh = dh2 + dh1n
        if l == 1:
            dh = dh + send([(n, 1) for n in big_names], 'layer1')
    last = send([(n, 0) for n in GATHER_FIRST], 'first0')
    grad_x = dh[None]
    small_full = []
    for n in SMALL:
        if n == 't5_bias':
            small_full.append(_t5_grad([a_ + b_ for a_, b_ in zip(layer_grads[0]['t5_tables'], layer_grads[1]['t5_tables'])]))
        else:
            small_full.append(jnp.stack([layer_grads[l][n] for l in range(DEPTH)]))

    small_shapes = [w_loc[n].shape for n in SMALL] + [(1,)]
    nothing = [jnp.zeros((1,), f32)]
    small_pack = _pack(small_full + [local_loss.reshape(1)]) + last
    _, recv_small = _swap_with_sibling([], small_pack)
    chip_small = _small_chip_sum(small_pack, recv_small)
    _, from_chips_small = _exchange_between_chips([], chip_small)

    mine = {}
    for group, tag, send_sems, recv_sems, srcs, lands in sent:
        srcs, lands = _to_chips_wait(send_sems, recv_sems, srcs, lands, True, from_chips_small, f'reduce_wait_{tag}')
        for (n, l), src, land in zip(group, srcs, lands):
            mine[(n, l)] = _partial_sum(src, land, f'partial_{n}_{l}')
    keys = [(n, l) for n in big_names for l in range(DEPTH)]
    theirs = dict(zip(keys, _swap_partials([mine[k] for k in keys])))

    g_small_p, d_small_p, m_small_p, v_small_p = _small_update(
        small_pack, recv_small, from_chips_small, _pack([w_loc[n] for n in SMALL] + nothing),
        _pack([m_loc[n] for n in SMALL] + nothing), _pack([v_loc[n] for n in SMALL] + nothing))
    loss = _unpack(g_small_p, small_shapes)[-1][0]
    grad, delta, new_m, new_v = {}, {}, {}, {}
    for n, g_, d_, m_, v_ in zip(SMALL, _unpack(g_small_p, small_shapes), _unpack(d_small_p, small_shapes),
                                 _unpack(m_small_p, small_shapes), _unpack(v_small_p, small_shapes)):
        grad[n], delta[n], new_m[n], new_v[n] = g_, d_, m_, v_
    for n in big_names:
        results = _adamw(w_loc[n], m_loc[n], v_loc[n], [mine[(n, l)] for l in range(DEPTH)],
                         [theirs[(n, l)] for l in range(DEPTH)], 'adamw_' + n)
        grad[n], delta[n], new_m[n], new_v[n] = [seen(n, t) for t in results]
    return (loss, grad_x, *[grad[n] for n in WEIGHTS], *[delta[n] for n in WEIGHTS],
            *[new_m[n] for n in WEIGHTS], *[new_v[n] for n in WEIGHTS])
```

```python
import functools
import math

import numpy as np
import jax
import jax.numpy as jnp
from jax import lax
from jax.experimental import pallas as pl
from jax.experimental.pallas import tpu as pltpu

f32 = jnp.float32
bf16 = jnp.bfloat16
HI = lax.Precision.HIGHEST
MESH = pl.DeviceIdType.MESH

VMEM_LIMIT_BYTES = 48 * 1024 * 1024
MM_VMEM_BUDGET_BYTES = 32 * 1024 * 1024
LANES = 128

D_MODEL = 1024
DEPTH = 2
GROUP_W = 256
HEAD_DIM = 64
EPS = 1e-6
NEG_INF = -1e30
N_HEADS = 4
MLA_NOPE, MLA_ROPE = 64, 32
MLA_DQK = MLA_NOPE + MLA_ROPE
ROPE_THETA = 10000.0
Q_BLOCK = 128
S5_G, S5_CG, S5_P = 16, 16, 64
DIL_PAIRS = ((128, 1), (512, 4), (2048, 16))
T5_BUCKETS, T5_MAX_DIST = 32, 2048
DN_CHUNK = 64
FFN_HIDDEN = 2816
IN_SPLITS = (256, 128, 32, 256, 768, 768, 4, 4, 256)
IN_COLS = sum(IN_SPLITS)

ADAM_LR, ADAM_B1, ADAM_B2, ADAM_EPS, ADAM_WD, ADAM_STEP = 0.001, 0.9, 0.999, 1e-08, 0.01, 10

WEIGHTS = ['attn_norm', 'w_in', 'w_out', 'mla_q_norm', 'mla_kv_norm', 'mla_w_uq', 'mla_w_ukv', 'mla_qk_q', 'mla_qk_k',
           's5_lambda_re', 's5_lambda_im', 's5_log_dt', 's5_b_re', 's5_b_im', 's5_c_re', 's5_c_im', 's5_d', 's5_w_glu',
           'dil_q_norm', 'dil_k_norm', 't5_bias', 'dn_conv', 'dn_a_log', 'dn_dt_bias', 'dn_o_norm', 'ffn_norm',
           'ffn_w1', 'ffn_w3', 'ffn_w2']
BIG = {'w_in': 2, 'w_out': 1, 'mla_w_uq': 2, 'mla_w_ukv': 2, 's5_w_glu': 2, 'dn_conv': 2, 'ffn_w1': 1, 'ffn_w3': 1,
       'ffn_w2': 1}
TRANSPOSED = ('ffn_w1', 'ffn_w3')
SMALL = [n for n in WEIGHTS if n not in BIG]
GATHER_FIRST = ['w_in', 'mla_w_uq', 'mla_w_ukv', 's5_w_glu', 'dn_conv', 'w_out']
GATHER_FFN = ['ffn_w1', 'ffn_w3', 'ffn_w2']
N_SHARDS = 4
PACK_COLS = 1024


def _cparams(sem=None, big=False):
    kw = {}
    if sem is not None:
        kw['dimension_semantics'] = sem
    if big:
        kw['vmem_limit_bytes'] = VMEM_LIMIT_BYTES
    return pltpu.CompilerParams(**kw)


def _pick(n, prefs):
    for p in prefs:
        if p <= n and n % p == 0:
            return p
    return n


def _lane_tile(n, cap):
    for t in range(cap - cap % LANES, 0, -LANES):
        if n % t == 0:
            return t
    return n


def _mm(a, b, mode, name, add=None, out_dtype=f32):
    if mode == 'nn':
        (M, K), (K2, N) = a.shape, b.shape
    elif mode == 'nt':
        (M, K), (N, K2) = a.shape, b.shape
    else:
        (K, M), (K2, N) = a.shape, b.shape
    assert K == K2, (name, a.shape, b.shape)
    tk = K if K <= 2816 else _pick(K, (2816, 2048, 1408, 1024, 512))
    cap_m, cap_n = (1408 if mode == 'tn' else 512), 1408

    def need(tm_, tn_):
        per_step = tm_ * tk * a.dtype.itemsize + tk * tn_ * b.dtype.itemsize + tm_ * tn_ * jnp.dtype(out_dtype).itemsize
        if add is not None:
            per_step += tm_ * tn_ * add.dtype.itemsize
        return 2 * per_step + tm_ * tn_ * 4

    tm, tn = _lane_tile(M, cap_m), _lane_tile(N, cap_n)
    while need(tm, tn) > MM_VMEM_BUDGET_BYTES and cap_m > LANES:
        cap_m //= 2
        tm = _lane_tile(M, cap_m)
    nk = K // tk
    dims = {'nn': (((1,), (0,)), ((), ())), 'nt': (((1,), (1,)), ((), ())), 'tn': (((0,), (0,)), ((), ()))}[mode]
    has_add = add is not None

    def body(*refs):
        a_ref, b_ref = refs[0], refs[1]
        add_ref = refs[2] if has_add else None
        o_ref = refs[3] if has_add else refs[2]
        part = lax.dot_general(a_ref[...].astype(bf16), b_ref[...].astype(bf16), dims, preferred_element_type=f32)
        if nk == 1:
            if has_add:
                part = part + add_ref[...].astype(f32)
            o_ref[...] = part.astype(out_dtype)
        else:
            acc_ref = refs[-1]
            k = pl.program_id(2)

            @pl.when(k == 0)
            def _():
                acc_ref[...] = part

            @pl.when(k > 0)
            def _():
                acc_ref[...] += part

            @pl.when(k == nk - 1)
            def _():
                r = acc_ref[...]
                if has_add:
                    r = r + add_ref[...].astype(f32)
                o_ref[...] = r.astype(out_dtype)

    if mode == 'nn':
        a_spec = pl.BlockSpec((tm, tk), lambda i, j, k: (i, k))
        b_spec = pl.BlockSpec((tk, tn), lambda i, j, k: (k, j))
    elif mode == 'nt':
        a_spec = pl.BlockSpec((tm, tk), lambda i, j, k: (i, k))
        b_spec = pl.BlockSpec((tn, tk), lambda i, j, k: (j, k))
    else:
        a_spec = pl.BlockSpec((tk, tm), lambda i, j, k: (k, i))
        b_spec = pl.BlockSpec((tk, tn), lambda i, j, k: (k, j))
    in_specs = [a_spec, b_spec]
    args = [a, b]
    if has_add:
        in_specs.append(pl.BlockSpec((tm, tn), lambda i, j, k: (i, j)))
        args.append(add)
    return pl.pallas_call(
        body, name=name, grid=(M // tm, N // tn, nk), in_specs=in_specs,
        out_specs=pl.BlockSpec((tm, tn), lambda i, j, k: (i, j)),
        out_shape=jax.ShapeDtypeStruct((M, N), out_dtype),
        scratch_shapes=[pltpu.VMEM((tm, tn), f32)] if nk > 1 else [],
        compiler_params=_cparams(('parallel', 'parallel', 'arbitrary'), big=True),
    )(*args)


def _full_spec(p):
    nd = p.ndim
    return pl.BlockSpec(p.shape, lambda i, _nd=nd: (0,) * _nd)


def _tile_fwd(f, tiled, params, outs, tm, name):
    S = tiled[0].shape[0]
    nt, npar = len(tiled), len(params)

    def body(*refs):
        vals = [r[...].astype(f32) for r in refs[:nt + npar]]
        res = f(*vals)
        for r, o in zip(res, refs[nt + npar:]):
            o[...] = r.astype(o.dtype)

    return pl.pallas_call(
        body, name=name, grid=(S // tm,),
        in_specs=[pl.BlockSpec((tm, t.shape[1]), lambda i: (i, 0)) for t in tiled] + [_full_spec(p) for p in params],
        out_specs=[pl.BlockSpec((tm, c), lambda i: (i, 0)) for c, _ in outs],
        out_shape=[jax.ShapeDtypeStruct((S, c), dt) for c, dt in outs],
        compiler_params=_cparams(('parallel',), big=True),
    )(*tiled, *params)


def _tile_bwd(f, tiled, params, cts, diff_t, diff_p, tm, name, dt_dtypes=None):
    S = tiled[0].shape[0]
    nt, npar, nc = len(tiled), len(params), len(cts)
    it = [i for i in range(nt) if diff_t[i]]
    ip = [i for i in range(npar) if diff_p[i]]
    if dt_dtypes is None:
        dt_dtypes = [f32] * len(it)

    def body(*refs):
        vals = [r[...].astype(f32) for r in refs[:nt + npar]]
        ct_vals = tuple(r[...].astype(f32) for r in refs[nt + npar:nt + npar + nc])
        out_refs = refs[nt + npar + nc:]

        def g(*dv):
            full = list(vals)
            for k, i in enumerate(it):
                full[i] = dv[k]
            for k, i in enumerate(ip):
                full[nt + i] = dv[len(it) + k]
            return tuple(f(*full))

        _, vjp = jax.vjp(g, *[vals[i] for i in it], *[vals[nt + i] for i in ip])
        grads = vjp(ct_vals)
        for k in range(len(it)):
            out_refs[k][...] = grads[k].astype(out_refs[k].dtype)
        step = pl.program_id(0)
        for k in range(len(ip)):
            o = out_refs[len(it) + k]
            gk = grads[len(it) + k]

            @pl.when(step == 0)
            def _(o=o, gk=gk):
                o[...] = gk

            @pl.when(step > 0)
            def _(o=o, gk=gk):
                o[...] += gk

    out_specs = [pl.BlockSpec((tm, tiled[i].shape[1]), lambda i_: (i_, 0)) for i in it] + [_full_spec(params[i]) for i in ip]
    out_shape = [jax.ShapeDtypeStruct(tiled[i].shape, dt_dtypes[k]) for k, i in enumerate(it)] + \
                [jax.ShapeDtypeStruct(params[i].shape, f32) for i in ip]
    res = pl.pallas_call(
        body, name=name, grid=(S // tm,),
        in_specs=[pl.BlockSpec((tm, t.shape[1]), lambda i: (i, 0)) for t in tiled] + [_full_spec(p) for p in params] +
                 [pl.BlockSpec((tm, c.shape[1]), lambda i: (i, 0)) for c in cts],
        out_specs=out_specs, out_shape=out_shape,
        compiler_params=_cparams(('arbitrary',), big=True),
    )(*tiled, *params, *cts)
    return list(res[:len(it)]), list(res[len(it):])


def _rms(x, g):
    return x * lax.rsqrt(jnp.mean(x * x, axis=-1, keepdims=True) + EPS) * g


def _f_rms(x, g):
    return (_rms(x, g),)


def _f_swiglu(u, v):
    return (u * jax.nn.sigmoid(u) * v,)


def _loss_head(y, target):
    S, D = y.shape
    tm = _pick(S, (256, 128))

    def body(y_ref, t_ref, part_ref, dy_ref):
        e = y_ref[...] - t_ref[...]
        dy_ref[...] = e * (1.0 / D)
        s = 0.5 * jnp.sum(jnp.sum(e * e, axis=1, keepdims=True), axis=0, keepdims=True) * (1.0 / D)
        r = lax.broadcasted_iota(jnp.int32, (8, LANES), 0)
        c = lax.broadcasted_iota(jnp.int32, (8, LANES), 1)
        part_ref[0] = jnp.where((r == 0) & (c == 0), s, 0.0)

    return pl.pallas_call(
        body, name='loss_head', grid=(S // tm,),
        in_specs=[pl.BlockSpec((tm, D), lambda i: (i, 0))] * 2,
        out_specs=[pl.BlockSpec((1, 8, LANES), lambda i: (i, 0, 0)), pl.BlockSpec((tm, D), lambda i: (i, 0))],
        out_shape=[jax.ShapeDtypeStruct((S // tm, 8, LANES), f32), jax.ShapeDtypeStruct((S, D), f32)],
        compiler_params=_cparams(('parallel',)),
    )(y, target)


def _pack_rows_of(shape):
    rows = -(-math.prod(shape) // PACK_COLS)
    return -(-rows // 8) * 8


def _pack(arrs):
    parts = []
    for a in arrs:
        rows = _pack_rows_of(a.shape)
        flat = a.astype(f32).reshape(-1)
        parts.append(jnp.pad(flat, (0, rows * PACK_COLS - flat.shape[0])).reshape(rows, PACK_COLS))
    return jnp.concatenate(parts, axis=0)


def _unpack(pack, shapes):
    out, row = [], 0
    for s in shapes:
        rows = _pack_rows_of(s)
        out.append(pack[row:row + rows].reshape(-1)[:math.prod(s)].reshape(s))
        row += rows
    return out


ANY = pl.BlockSpec(memory_space=pl.ANY)


def _place():
    return lax.axis_index('x'), lax.axis_index('y'), lax.axis_index('c')


def _where():
    return jnp.stack([lax.axis_index('c'), 2 * lax.axis_index('x') + lax.axis_index('y')]).astype(jnp.int32)


def _remote(src, dst, send_sems, recv_sems, k, to):
    return pltpu.make_async_remote_copy(src_ref=src, dst_ref=dst, send_sem=send_sems.at[k], recv_sem=recv_sems.at[k],
                                        device_id=to, device_id_type=MESH)


def _swap_with_sibling(gs, small):
    n = len(gs)

    def body(*refs):
        g_refs, s_ref = refs[:n], refs[n]
        r_refs, rs_ref = refs[n + 1:2 * n + 1], refs[2 * n + 1]
        send_sems, recv_sems = refs[2 * n + 2:]
        x, y, c = _place()
        sib = (x, y, 1 - c)
        cps = [_remote(g_refs[t].at[:, 1 - c], r_refs[t], send_sems, recv_sems, t, sib) for t in range(n)]
        cps.append(_remote(s_ref, rs_ref, send_sems, recv_sems, n, sib))
        for cp in cps:
            cp.start()
        for cp in cps:
            cp.wait()

    res = pl.pallas_call(
        body, name='swap_with_sibling', in_specs=[ANY] * (n + 1), out_specs=[ANY] * (n + 1),
        out_shape=[jax.ShapeDtypeStruct((N_SHARDS,) + g.shape[2:], g.dtype) for g in gs] +
                  [jax.ShapeDtypeStruct(small.shape, small.dtype)],
        scratch_shapes=[pltpu.SemaphoreType.DMA((n + 1,)), pltpu.SemaphoreType.DMA((n + 1,))],
    )(*gs, small)
    return list(res[:n]), res[n]


def _exchange_between_chips(cs, small):
    n = len(cs)

    def body(*refs):
        c_refs, s_ref = refs[:n], refs[n]
        r_refs, rs_ref = refs[n + 1:2 * n + 1], refs[2 * n + 1]
        send_sems, recv_sems = refs[2 * n + 2:]
        x, y, c = _place()
        chips = [(1 - x, y), (x, 1 - y), (1 - x, 1 - y)]
        cps = []
        for j, (px, py) in enumerate(chips):
            for t in range(n):
                cps.append(_remote(c_refs[t].at[2 * px + py], r_refs[t].at[j], send_sems, recv_sems, 3 * t + j, (px, py, c)))
            cps.append(_remote(s_ref, rs_ref.at[j], send_sems, recv_sems, 3 * n + j, (px, py, c)))
        for cp in cps:
            cp.start()
        for cp in cps:
            cp.wait()

    res = pl.pallas_call(
        body, name='exchange_between_chips', in_specs=[ANY] * (n + 1), out_specs=[ANY] * (n + 1),
        out_shape=[jax.ShapeDtypeStruct((3,) + c.shape[1:], c.dtype) for c in cs] +
                  [jax.ShapeDtypeStruct((3,) + small.shape, small.dtype)],
        scratch_shapes=[pltpu.SemaphoreType.DMA((3 * n + 3,)), pltpu.SemaphoreType.DMA((3 * n + 3,))],
    )(*cs, small)
    return list(res[:n]), res[n]


def _swap_partials(ts):
    n = len(ts)

    def body(*refs):
        t_refs, o_refs = refs[:n], refs[n:2 * n]
        send_sems, recv_sems = refs[2 * n:]
        x, y, c = _place()
        cps = [_remote(t_refs[t], o_refs[t], send_sems, recv_sems, t, (x, y, 1 - c)) for t in range(n)]
        for cp in cps:
            cp.start()
        for cp in cps:
            cp.wait()

    return pl.pallas_call(
        body, name='swap_partials', in_specs=[ANY] * n, out_specs=[ANY] * n,
        out_shape=[jax.ShapeDtypeStruct(t.shape, t.dtype) for t in ts],
        scratch_shapes=[pltpu.SemaphoreType.DMA((n,)), pltpu.SemaphoreType.DMA((n,))],
    )(*ts)


HBM = pl.BlockSpec(memory_space=pltpu.HBM)
SEM = pl.BlockSpec(memory_space=pltpu.SEMAPHORE)
DATAFLOW = pltpu.SideEffectType.DATAFLOW_SIDE_EFFECTING


def _in_hbm(t):
    return pltpu.with_memory_space_constraint(t, pltpu.HBM)


def _other_chips():
    x, y, c = _place()
    return [(1 - x, y, c), (x, 1 - y, c), (1 - x, 1 - y, c)]


def _to_chips_copies(src_refs, land_refs, send_sems, recv_sems, per_peer):
    x, y, _ = _place()
    cps = []
    for t, (src, land) in enumerate(zip(src_refs, land_refs)):
        for j, (px, py, pc) in enumerate(_other_chips()):
            s = src.at[2 * px + py] if per_peer else src
            d = land.at[j] if per_peer else land.at[2 * x + y]
            cps.append(_remote(s, d, send_sems, recv_sems, 3 * t + j, (px, py, pc)))
    return cps


def _to_chips_start(srcs, lands, per_peer, order, name):
    n = len(srcs)

    def body(*refs):
        src_refs, land_refs = refs[:n], refs[n:2 * n]
        send_sems, recv_sems = refs[2 * n + 1], refs[2 * n + 2]
        token = refs[-1]
        for cp in _to_chips_copies(src_refs, land_refs, send_sems, recv_sems, per_peer):
            cp.start()
        token[...] = jnp.zeros_like(token)

    res = pl.pallas_call(
        body, name=name, in_specs=[HBM] * (2 * n) + [ANY],
        out_specs=[SEM, SEM] + [HBM] * (2 * n) + [pl.BlockSpec(memory_space=pltpu.VMEM)],
        out_shape=[pltpu.SemaphoreType.DMA((3 * n,)), pltpu.SemaphoreType.DMA((3 * n,))] +
                  [pltpu.HBM(t.shape, t.dtype) for t in srcs] + [pltpu.HBM(t.shape, t.dtype) for t in lands] +
                  [jax.ShapeDtypeStruct((8, LANES), f32)],
        input_output_aliases={i: 2 + i for i in range(2 * n)},
        compiler_params=pltpu.CompilerParams(has_side_effects=DATAFLOW),
    )(*[_in_hbm(t) for t in srcs], *[_in_hbm(t) for t in lands], order)
    return res[0], res[1], list(res[2:2 + n]), list(res[2 + n:2 + 2 * n]), res[-1]


def _to_chips_wait(send_sems, recv_sems, srcs, lands, per_peer, after, name):
    n = len(srcs)

    def body(*refs):
        src_refs, land_refs = refs[:n], refs[n:2 * n]
        send_ref, recv_ref = refs[2 * n], refs[2 * n + 1]
        for cp in _to_chips_copies(src_refs, land_refs, send_ref, recv_ref, per_peer):
            cp.wait_send()
            cp.wait_recv()

    res = pl.pallas_call(
        body, name=name, in_specs=[HBM] * (2 * n) + [SEM, SEM, ANY],
        out_specs=[HBM] * (2 * n),
        out_shape=[pltpu.HBM(t.shape, t.dtype) for t in srcs] + [pltpu.HBM(t.shape, t.dtype) for t in lands],
        input_output_aliases={i: i for i in range(2 * n)},
        compiler_params=pltpu.CompilerParams(has_side_effects=DATAFLOW),
    )(*srcs, *lands, send_sems, recv_sems, after)
    return list(res[:n]), list(res[n:])


def _row_tile(a):
    return _pick(a, (512, 256, 128, 64, 32, 16, 8))


def _partial_sum(g, land, name):
    _, a, b = g.shape
    tr = _row_tile(a)

    def body(w_ref, g_ref, r_ref, o_ref):
        t = g_ref[0].astype(f32) + r_ref[0].astype(f32)
        t = t + r_ref[1].astype(f32)
        t = t + r_ref[2].astype(f32)
        o_ref[...] = t.astype(o_ref.dtype)

    return pl.pallas_call(
        body, name=name,
        grid_spec=pltpu.PrefetchScalarGridSpec(
            num_scalar_prefetch=1, grid=(a // tr,),
            in_specs=[pl.BlockSpec((1, tr, b), lambda i, w: (w[1], i, 0)), pl.BlockSpec((3, tr, b), lambda i, w: (0, i, 0))],
            out_specs=pl.BlockSpec((tr, b), lambda i, w: (i, 0))),
        out_shape=jax.ShapeDtypeStruct((a, b), bf16),
        compiler_params=_cparams(('parallel',)),
    )(_where(), g, land)


def _by_shard(name, t):
    r, c = t.shape
    if BIG[name] == 2:
        return t.reshape(r, N_SHARDS, c // N_SHARDS).transpose(1, 0, 2)
    return t.reshape(N_SHARDS, r // N_SHARDS, c)


def _from_shards(name, g):
    s, a, b = g.shape
    if BIG[name] == 2:
        return g.transpose(1, 0, 2).reshape(a, s * b)
    return g.reshape(s * a, b)


def _adam_math(w, g, m, v):
    m = ADAM_B1 * m + (1.0 - ADAM_B1) * g
    v = ADAM_B2 * v + (1.0 - ADAM_B2) * (g * g)
    m_hat = m / (1.0 - ADAM_B1 ** ADAM_STEP)
    v_hat = v / (1.0 - ADAM_B2 ** ADAM_STEP)
    delta = -ADAM_LR * (m_hat / (jnp.sqrt(v_hat) + ADAM_EPS) + ADAM_WD * w)
    return delta, m, v


def _small_update(own, sib, chips, w, m, v):
    def body(o_ref, s_ref, c_ref, w_ref, m_ref, v_ref, g_out, d_out, m_out, v_out):
        chip = o_ref[...] + s_ref[...]
        g = (chip + c_ref[0]) + (c_ref[1] + c_ref[2])
        d, mn, vn = _adam_math(w_ref[...], g, m_ref[...], v_ref[...])
        g_out[...] = g
        d_out[...] = d
        m_out[...] = mn
        v_out[...] = vn

    return pl.pallas_call(body, name='small_update', out_shape=[jax.ShapeDtypeStruct(own.shape, f32)] * 4)(
        own, sib, chips, w, m, v)


def _small_chip_sum(own, sib):
    def body(o_ref, s_ref, out):
        out[...] = o_ref[...] + s_ref[...]
    return pl.pallas_call(body, name='small_chip_sum', out_shape=jax.ShapeDtypeStruct(own.shape, f32))(own, sib)


def _adamw(w, m, v, mine, theirs, name):
    layers, a, b = w.shape
    tr = _row_tile(a)

    def body(w_ref, m_ref, v_ref, p0, p1, q0, q1, g_out, d_out, m_out, v_out):
        first = pl.program_id(0) == 0
        g = jnp.where(first, p0[...].astype(f32) + q0[...].astype(f32), p1[...].astype(f32) + q1[...].astype(f32))
        d, mn, vn = _adam_math(w_ref[0], g, m_ref[0], v_ref[0])
        g_out[0] = g
        d_out[0] = d
        m_out[0] = mn
        v_out[0] = vn

    full = pl.BlockSpec((1, tr, b), lambda l, i: (l, i, 0))
    part = pl.BlockSpec((tr, b), lambda l, i: (i, 0))
    return pl.pallas_call(body, name=name, grid=(layers, a // tr), in_specs=[full] * 3 + [part] * 4, out_specs=[full] * 4,
                          out_shape=[jax.ShapeDtypeStruct(w.shape, f32)] * 4,
                          compiler_params=_cparams(('parallel', 'parallel')))(w, m, v, *mine, *theirs)


def _dg(a, b, ca, cb):
    return lax.dot_general(a.astype(bf16), b.astype(bf16), (((ca,), (cb,)), ((), ())), preferred_element_type=f32)


@jax.custom_vjp
def _bmm(a, b):
    return _dg(a, b, 1, 0)


_bmm.defvjp(lambda a, b: (_dg(a, b, 1, 0), (a, b)), lambda r, g: (_dg(g, r[1], 1, 1), _dg(r[0], g, 0, 0)))


@jax.custom_vjp
def _bmm_nt(a, b):
    return _dg(a, b, 1, 1)


_bmm_nt.defvjp(lambda a, b: (_dg(a, b, 1, 1), (a, b)), lambda r, g: (_dg(g, r[1], 1, 0), _dg(g, r[0], 0, 0)))


@jax.custom_vjp
def _bmm_tn(a, b):
    return _dg(a, b, 0, 0)


_bmm_tn.defvjp(lambda a, b: (_dg(a, b, 0, 0), (a, b)), lambda r, g: (_dg(r[1], g, 1, 1), _dg(r[0], g, 1, 0)))


def _hdot(a, b):
    return jnp.dot(a, b, precision=HI, preferred_element_type=f32)


def _hdot_nt(a, b):
    return lax.dot_general(a, b, (((1,), (1,)), ((), ())), precision=HI, preferred_element_type=f32)


def _hdot_tn(a, b):
    return lax.dot_general(a, b, (((0,), (0,)), ((), ())), precision=HI, preferred_element_type=f32)


def _head_mask(h, width=GROUP_W):
    lane = lax.broadcasted_iota(jnp.int32, (1, width), 1)
    return ((lane >= h * HEAD_DIM) & (lane < (h + 1) * HEAD_DIM)).astype(f32)


def _rope_perm():
    p = np.zeros((LANES, LANES), np.float32)
    half = MLA_ROPE // 2
    for i in range(half):
        p[MLA_NOPE + half + i, MLA_NOPE + i] = -1.0
        p[MLA_NOPE + i, MLA_NOPE + half + i] = 1.0
    return jnp.asarray(p)


def _rope_tables(S):
    half = MLA_ROPE // 2
    freqs = ROPE_THETA ** (-jnp.arange(half, dtype=f32) / half)
    ang = jnp.arange(S, dtype=f32)[:, None] * freqs[None, :]
    cos, sin = jnp.cos(ang), jnp.sin(ang)
    ones, zeros = jnp.ones((S, MLA_NOPE), f32), jnp.zeros((S, LANES - MLA_DQK), f32)
    c_tab = jnp.concatenate([ones, cos, cos, zeros], axis=1)
    s_tab = jnp.concatenate([jnp.zeros((S, MLA_NOPE), f32), sin, sin, zeros], axis=1)
    return c_tab, s_tab


def _f_mla_pre(c_q, c_kv, krope, c_tab, s_tab, q_norm, kv_norm, wq0, wq1, wq2, wq3, wk0, wk1, wk2, wk3, wv, gq, gk, perm):
    wq, wk = (wq0, wq1, wq2, wq3), (wk0, wk1, wk2, wk3)
    nq = _rms(c_q, q_norm)
    nkv = _rms(c_kv, kv_norm)

    def norm_rope(t, g):
        t = t * lax.rsqrt(jnp.sum(t * t, axis=-1, keepdims=True) * (1.0 / MLA_DQK) + EPS) * g
        return t * c_tab + _hdot(t, perm) * s_tab

    qs = [norm_rope(_bmm(nq, wq[h]), gq) * (MLA_DQK ** -0.5) for h in range(N_HEADS)]
    ks = [norm_rope(_bmm(nkv, wk[h]) + krope, gk) for h in range(N_HEADS)]
    return (*qs, *ks, _bmm(nkv, wv))


def _f_attn(qs, ks, v, q0):
    tq, S = qs[0].shape[0], ks[0].shape[0]
    qpos = q0 + lax.broadcasted_iota(jnp.int32, (tq, S), 0)
    kpos = lax.broadcasted_iota(jnp.int32, (tq, S), 1)
    keep = kpos <= qpos
    logits = [jnp.where(keep, _bmm_nt(qs[h], ks[h]), NEG_INF) for h in range(N_HEADS)]
    ps = [jnp.exp(lg - jnp.max(lg, axis=-1, keepdims=True)) for lg in logits]
    ps = [p / jnp.sum(p, axis=-1, keepdims=True) for p in ps]
    return sum(_bmm(p, v) * _head_mask(h) for h, p in enumerate(ps))


ATTN_PARTS = 4


def _mla_attn_fwd(qs, ks, v, name):
    S = v.shape[0]
    tq = Q_BLOCK
    parts = ATTN_PARTS if S % (ATTN_PARTS * tq) == 0 else 1
    per = S // parts
    outs = []
    for p in range(parts):
        n_keys = (p + 1) * per
        first_block = p * (per // tq)

        def body(*refs, first_block=first_block):
            q_vals = [r[...] for r in refs[:4]]
            k_vals = [r[...] for r in refs[4:8]]
            refs[9][...] = _f_attn(q_vals, k_vals, refs[8][...], (first_block + pl.program_id(0)) * tq)

        qspec = pl.BlockSpec((tq, LANES), lambda i, fb=first_block: (fb + i, 0))
        outs.append(pl.pallas_call(
            body, name=f'{name}_{p}', grid=(per // tq,),
            in_specs=[qspec] * 4 + [pl.BlockSpec((n_keys, LANES), lambda i: (0, 0))] * 4 +
                     [pl.BlockSpec((n_keys, GROUP_W), lambda i: (0, 0))],
            out_specs=pl.BlockSpec((tq, GROUP_W), lambda i: (i, 0)),
            out_shape=jax.ShapeDtypeStruct((per, GROUP_W), f32),
            compiler_params=_cparams(('parallel',), big=True),
        )(*qs, *ks, v))
    return jnp.concatenate(outs, axis=0)


def _mla_attn_bwd(qs, ks, v, do, name):
    S = v.shape[0]
    tq = Q_BLOCK
    parts = ATTN_PARTS if S % (ATTN_PARTS * tq) == 0 else 1
    per = S // parts
    dq_parts, dkv_sum = [], None
    for p in range(parts):
        n_keys = (p + 1) * per
        first_block = p * (per // tq)

        def body(*refs, first_block=first_block):
            q_vals = [r[...].astype(f32) for r in refs[:4]]
            k_vals = [r[...].astype(f32) for r in refs[4:8]]
            v_val = refs[8][...].astype(f32)
            q0 = (first_block + pl.program_id(0)) * tq
            _, vjp = jax.vjp(lambda a, b, c: _f_attn(a, b, c, q0), q_vals, k_vals, v_val)
            dqs, dks, dv = vjp(refs[9][...])
            outs = refs[10:]
            for h in range(N_HEADS):
                outs[h][...] = dqs[h]
            first = pl.program_id(0) == 0
            for o, g in zip(outs[4:], (*dks, dv)):
                @pl.when(first)
                def _(o=o, g=g):
                    o[...] = g

                @pl.when(jnp.logical_not(first))
                def _(o=o, g=g):
                    o[...] += g

        qspec = pl.BlockSpec((tq, LANES), lambda i, fb=first_block: (fb + i, 0))
        kspec = pl.BlockSpec((n_keys, LANES), lambda i: (0, 0))
        vspec = pl.BlockSpec((n_keys, GROUP_W), lambda i: (0, 0))
        res = pl.pallas_call(
            body, name=f'{name}_{p}', grid=(per // tq,),
            in_specs=[qspec] * 4 + [kspec] * 4 + [vspec, pl.BlockSpec((tq, GROUP_W), lambda i, fb=first_block: (fb + i, 0))],
            out_specs=[pl.BlockSpec((tq, LANES), lambda i: (i, 0))] * 4 + [kspec] * 4 + [vspec],
            out_shape=[jax.ShapeDtypeStruct((per, LANES), f32)] * 4 + [jax.ShapeDtypeStruct((n_keys, LANES), f32)] * 4 +
                      [jax.ShapeDtypeStruct((n_keys, GROUP_W), f32)],
            compiler_params=_cparams(('arbitrary',), big=True),
        )(*qs, *ks, v, do)
        dq_parts.append(res[:4])
        dkv = [jnp.pad(t, ((0, S - n_keys), (0, 0))) for t in res[4:]]
        dkv_sum = dkv if dkv_sum is None else [a_ + b_ for a_, b_ in zip(dkv_sum, dkv)]
    dqs = [jnp.concatenate([dq_parts[p][h] for p in range(parts)], axis=0) for h in range(N_HEADS)]
    return dqs, dkv_sum[:4], dkv_sum[4]


def _mla_params(mp):
    pad = LANES - MLA_DQK
    wq = jnp.pad(mp['mla_w_uq'].reshape(GROUP_W, N_HEADS, MLA_DQK).transpose(1, 0, 2), ((0, 0), (0, 0), (0, pad)))
    wkv = mp['mla_w_ukv'].reshape(LANES, N_HEADS, MLA_NOPE + HEAD_DIM)
    wk = jnp.pad(wkv[:, :, :MLA_NOPE].transpose(1, 0, 2), ((0, 0), (0, 0), (0, LANES - MLA_NOPE)))
    wv = wkv[:, :, MLA_NOPE:].reshape(LANES, GROUP_W)
    gq = jnp.pad(mp['mla_qk_q'], (0, pad))[None]
    gk = jnp.pad(mp['mla_qk_k'], (0, pad))[None]
    return [mp['mla_q_norm'][None], mp['mla_kv_norm'][None], *[wq[h] for h in range(N_HEADS)],
            *[wk[h] for h in range(N_HEADS)], wv, gq, gk, _rope_perm()]


def _mla_fwd(c_q, c_kv, k_rope, mp, l):
    S = c_q.shape[0]
    tm = _pick(S, (256, 128))
    krope = jnp.pad(k_rope, ((0, 0), (MLA_NOPE, LANES - MLA_DQK)))
    c_tab, s_tab = _rope_tables(S)
    tiled = [c_q, c_kv, krope, c_tab, s_tab]
    params = _mla_params(mp)
    res = _tile_fwd(_f_mla_pre, tiled, params, [(LANES, bf16)] * 8 + [(GROUP_W, bf16)], tm, f'mla_pre_fwd_{l}')
    qs, ks, v = res[:4], res[4:8], res[8]
    y = _mla_attn_fwd(qs, ks, v, f'mla_attn_fwd_{l}')
    return y, (tiled, params, qs, ks, v)


def _mla_bwd(dy, saved, l):
    tiled, params, qs, ks, v = saved
    S = dy.shape[0]
    tm = _pick(S, (256, 128))
    dqs, dks, dv = _mla_attn_bwd(qs, ks, v, dy, f'mla_attn_bwd_{l}')
    (dc_q, dc_kv, dkrope), dpar = _tile_bwd(_f_mla_pre, tiled, params, [*dqs, *dks, dv], [True, True, True, False, False],
                                            [True] * 13 + [False], tm, f'mla_pre_bwd_{l}')
    dqn, dkvn = dpar[0], dpar[1]
    dwq, dwk = jnp.stack(dpar[2:6]), jnp.stack(dpar[6:10])
    dwv, dgq, dgk = dpar[10:13]
    dw_uq = dwq[:, :, :MLA_DQK].transpose(1, 0, 2).reshape(GROUP_W, N_HEADS * MLA_DQK)
    dw_ukv = jnp.concatenate([dwk[:, :, :MLA_NOPE].transpose(1, 0, 2), dwv.reshape(LANES, N_HEADS, HEAD_DIM)],
                             axis=2).reshape(LANES, N_HEADS * (MLA_NOPE + HEAD_DIM))
    grads = {'mla_q_norm': dqn[0], 'mla_kv_norm': dkvn[0], 'mla_w_uq': dw_uq, 'mla_w_ukv': dw_ukv,
             'mla_qk_q': dgq[0, :MLA_DQK], 'mla_qk_k': dgk[0, :MLA_DQK]}
    return dc_q, dc_kv, dkrope[:, MLA_NOPE:MLA_DQK], grads


SPAN = 128


def _head_mean_matrix():
    h = np.arange(GROUP_W) // HEAD_DIM
    return jnp.asarray((h[:, None] == h[None, :]).astype(np.float32) / HEAD_DIM)


def _f_dil_pre(q, k, gq, gk, hm):
    qn = q * lax.rsqrt(_hdot(q * q, hm) + EPS) * gq * (HEAD_DIM ** -0.5)
    kn = k * lax.rsqrt(_hdot(k * k, hm) + EPS) * gk
    return qn, kn


def _f_dil_branch(qb, kp, kc, vp, vc, b0, b1, b2, b3, first):
    kcat = jnp.concatenate([kp, kc], axis=0)
    vcat = jnp.concatenate([vp, vc], axis=0)
    qi = lax.broadcasted_iota(jnp.int32, (SPAN, 2 * SPAN), 0) + SPAN
    kj = lax.broadcasted_iota(jnp.int32, (SPAN, 2 * SPAN), 1)
    delta = qi - kj
    valid = (delta >= 0) & (delta <= SPAN) & jnp.logical_not(first & (kj < SPAN))
    masks = [_head_mask(h) for h in range(N_HEADS)]
    raw = [_bmm_nt(qb * hm, kcat) for hm in masks]
    logits = [jnp.where(valid, r + bias, NEG_INF) for r, bias in zip(raw, (b0, b1, b2, b3))]
    ms = [jnp.max(lg, axis=-1, keepdims=True) for lg in logits]
    ps = [jnp.exp(lg - m) for lg, m in zip(logits, ms)]
    pvs = [_bmm(p, vcat) for p in ps]
    o = sum(pv * hm for pv, hm in zip(pvs, masks))
    m_full = sum(m * hm for m, hm in zip(ms, masks))
    l_full = sum(jnp.sum(p, axis=-1, keepdims=True) * hm for p, hm in zip(ps, masks))
    return o, m_full, l_full


def _dil_branch_specs(d, nb):
    cur = pl.BlockSpec((SPAN, GROUP_W), lambda r, n: (n, r))
    prev = pl.BlockSpec((SPAN, GROUP_W), lambda r, n: (jnp.maximum(n - 1, 0), r))
    bias = pl.BlockSpec((1, SPAN, 2 * SPAN), lambda r, n: (0, 0, 0))
    return cur, prev, bias


def _head_table_specs():
    return [pl.BlockSpec((1, SPAN, 2 * SPAN), lambda r, n, h=h: (h, 0, 0)) for h in range(N_HEADS)]


def _dil_branch_fwd(q, k, v, table, name):
    L, d = q.shape[0], q.shape[1] // GROUP_W
    nb = L // SPAN
    cur, prev, bias = _dil_branch_specs(d, nb)

    def body(q_ref, kp_ref, kc_ref, vp_ref, vc_ref, b0, b1, b2, b3, o_ref, m_ref, l_ref):
        o, m, l = _f_dil_branch(*[r[...].astype(f32) for r in (q_ref, kp_ref, kc_ref, vp_ref, vc_ref)], b0[0], b1[0], b2[0], b3[0],
                                pl.program_id(1) == 0)
        o_ref[...] = o
        m_ref[...] = m
        l_ref[...] = l

    return pl.pallas_call(
        body, name=name, grid=(d, nb), in_specs=[cur, prev, cur, prev, cur] + _head_table_specs(),
        out_specs=[cur] * 3, out_shape=[jax.ShapeDtypeStruct(q.shape, f32)] * 3,
        compiler_params=_cparams(('parallel', 'parallel')),
    )(q, k, k, v, v, *[table] * N_HEADS)


def _dil_branch_bwd(q, k, v, table, do, dm, dl, name):
    L, d = q.shape[0], q.shape[1] // GROUP_W
    nb = L // SPAN
    cur, prev, bias = _dil_branch_specs(d, nb)
    whole = pl.BlockSpec((L, GROUP_W), lambda r, n: (0, r))

    def body(q_ref, kp_ref, kc_ref, vp_ref, vc_ref, b0, b1, b2, b3, do_ref, dm_ref, dl_ref,
             dq_ref, dk_ref, dv_ref, db0, db1, db2, db3):
        r, n = pl.program_id(0), pl.program_id(1)
        first = n == 0
        _, vjp = jax.vjp(lambda *a: _f_dil_branch(*a, first), *[r[...].astype(f32) for r in (q_ref, kp_ref, kc_ref, vp_ref, vc_ref)],
                         b0[0], b1[0], b2[0], b3[0])
        dq, dkp, dkc, dvp, dvc, g0, g1, g2, g3 = vjp((do_ref[...], dm_ref[...], dl_ref[...]))
        dq_ref[...] = dq

        @pl.when(first)
        def _():
            dk_ref[...] = jnp.zeros_like(dk_ref)
            dv_ref[...] = jnp.zeros_like(dv_ref)

        rows = pl.ds(pl.multiple_of(n * SPAN, SPAN), SPAN)
        dk_ref[rows, :] += dkc
        dv_ref[rows, :] += dvc

        @pl.when(n > 0)
        def _():
            before = pl.ds(pl.multiple_of((n - 1) * SPAN, SPAN), SPAN)
            dk_ref[before, :] += dkp
            dv_ref[before, :] += dvp

        start = first & (r == 0)
        for o, g in zip((db0, db1, db2, db3), (g0, g1, g2, g3)):
            @pl.when(start)
            def _(o=o, g=g):
                o[0] = g

            @pl.when(jnp.logical_not(start))
            def _(o=o, g=g):
                o[0] += g

    res = pl.pallas_call(
        body, name=name, grid=(d, nb), in_specs=[cur, prev, cur, prev, cur] + _head_table_specs() + [cur] * 3,
        out_specs=[cur, whole, whole] + [bias] * 4,
        out_shape=[jax.ShapeDtypeStruct(q.shape, f32)] * 3 + [jax.ShapeDtypeStruct((1, SPAN, 2 * SPAN), f32)] * 4,
        compiler_params=_cparams(('arbitrary', 'arbitrary')),
    )(q, k, k, v, v, *[table] * N_HEADS, do, dm, dl)
    return res[0], res[1], res[2], res[3:]


def _f_dil_merge(o1, m1, l1, o2, m2, l2, o3, m3, l3):
    mx = jnp.maximum(jnp.maximum(m1, m2), m3)
    w1, w2, w3 = jnp.exp(m1 - mx), jnp.exp(m2 - mx), jnp.exp(m3 - mx)
    return ((w1 * o1 + w2 * o2 + w3 * o3) / (w1 * l1 + w2 * l2 + w3 * l3),)


def _bias_onehot(dilation):
    qi = jnp.arange(SPAN, dtype=jnp.int32)[:, None] + SPAN
    kj = jnp.arange(2 * SPAN, dtype=jnp.int32)[None, :]
    bucket = _t5_bucket(jnp.clip(qi - kj, 0, SPAN) * dilation).reshape(-1)
    return (bucket[None, :] == jnp.arange(T5_BUCKETS, dtype=jnp.int32)[:, None]).astype(f32)


def _bias_tables(t5_t, onehot, name):
    N = onehot.shape[1]
    tn = _pick(N, (4096, 2048, 1024))

    def body(t_ref, oh_ref, o_ref):
        o_ref[...] = _hdot(t_ref[...], oh_ref[...])

    return pl.pallas_call(
        body, name=name, grid=(N // tn,),
        in_specs=[pl.BlockSpec((8, T5_BUCKETS), lambda i: (0, 0)), pl.BlockSpec((T5_BUCKETS, tn), lambda i: (0, i))],
        out_specs=pl.BlockSpec((8, tn), lambda i: (0, i)), out_shape=jax.ShapeDtypeStruct((8, N), f32),
        compiler_params=_cparams(('parallel',)),
    )(t5_t, onehot)


def _bias_tables_bwd(d_tab, onehot, name):
    N = onehot.shape[1]
    tn = _pick(N, (4096, 2048, 1024))

    def body(g_ref, oh_ref, o_ref):
        part = _hdot_nt(g_ref[...], oh_ref[...])

        @pl.when(pl.program_id(0) == 0)
        def _():
            o_ref[...] = part

        @pl.when(pl.program_id(0) > 0)
        def _():
            o_ref[...] += part

    return pl.pallas_call(
        body, name=name, grid=(N // tn,),
        in_specs=[pl.BlockSpec((8, tn), lambda i: (0, i)), pl.BlockSpec((T5_BUCKETS, tn), lambda i: (0, i))],
        out_specs=pl.BlockSpec((8, T5_BUCKETS), lambda i: (0, 0)), out_shape=jax.ShapeDtypeStruct((8, T5_BUCKETS), f32),
        compiler_params=_cparams(('arbitrary',)),
    )(d_tab, onehot)


def _by_residue(t, d):
    S, C = t.shape
    return t.reshape(S // d, d * C)


def _from_residue(t):
    return t.reshape(-1, GROUP_W)


def _dil_tables(t5_bias):
    t5_t = jnp.pad(t5_bias.T, ((0, 8 - N_HEADS), (0, 0)))
    return [_bias_tables(t5_t, _bias_onehot(d), f'dil_bias_fwd_{bi}').reshape(8, SPAN, 2 * SPAN)
            for bi, (_, d) in enumerate(DIL_PAIRS)]


def _t5_grad(d_tables):
    total = None
    for bi, (_, d) in enumerate(DIL_PAIRS):
        g = _bias_tables_bwd(d_tables[bi], _bias_onehot(d), f'dil_bias_bwd_{bi}')
        total = g if total is None else total + g
    return total[:N_HEADS].T


def _dil_fwd(qkv, mp, l):
    S = qkv.shape[0]
    tm = _pick(S, (256, 128))
    q, k, v = qkv[:, :GROUP_W], qkv[:, GROUP_W:2 * GROUP_W], qkv[:, 2 * GROUP_W:]
    pre_params = [jnp.tile(mp['dil_q_norm'], N_HEADS)[None], jnp.tile(mp['dil_k_norm'], N_HEADS)[None], _head_mean_matrix()]
    qn, kn = _tile_fwd(_f_dil_pre, [q, k], pre_params, [(GROUP_W, bf16)] * 2, tm, f'dil_pre_fwd_{l}')
    v = v.astype(bf16)
    tables = mp['dil_tables'] if 'dil_tables' in mp else _dil_tables(mp['t5_bias'])
    branches, outs = [], []
    for bi, (_, d) in enumerate(DIL_PAIRS):
        tab = tables[bi]
        qd, kd, vd = _by_residue(qn, d), _by_residue(kn, d), _by_residue(v, d)
        o, m, lsum = _dil_branch_fwd(qd, kd, vd, tab, f'dil_branch_fwd_{l}_{bi}')
        branches.append((qd, kd, vd, tab))
        outs += [_from_residue(o), _from_residue(m), _from_residue(lsum)]
    (y,) = _tile_fwd(_f_dil_merge, outs, [], [(GROUP_W, f32)], tm, f'dil_merge_fwd_{l}')
    return y, (q, k, pre_params, branches, outs)


def _dil_bwd(dy, saved, l):
    q, k, pre_params, branches, outs = saved
    S = dy.shape[0]
    tm = _pick(S, (256, 128))
    douts, _ = _tile_bwd(_f_dil_merge, outs, [], [dy], [True] * 9, [], tm, f'dil_merge_bwd_{l}')
    dqn = dkn = dv = None
    d_tabs = []
    for bi, (_, d) in enumerate(DIL_PAIRS):
        qd, kd, vd, tab = branches[bi]
        do, dm, dl = [_by_residue(t, d) for t in douts[3 * bi:3 * bi + 3]]
        dq_b, dk_b, dv_b, dbias = _dil_branch_bwd(qd, kd, vd, tab, do, dm, dl, f'dil_branch_bwd_{l}_{bi}')
        d_tabs.append(jnp.concatenate([*dbias, jnp.zeros((8 - N_HEADS, SPAN, 2 * SPAN), f32)], axis=0).reshape(8, -1))
        dq_b, dk_b, dv_b = _from_residue(dq_b), _from_residue(dk_b), _from_residue(dv_b)
        dqn = dq_b if dqn is None else dqn + dq_b
        dkn = dk_b if dkn is None else dkn + dk_b
        dv = dv_b if dv is None else dv + dv_b
    (dq, dk), (dgq, dgk) = _tile_bwd(_f_dil_pre, [q, k], pre_params, [dqn, dkn], [True, True], [True, True, False], tm,
                                     f'dil_pre_bwd_{l}')
    grads = {'dil_q_norm': dgq.reshape(N_HEADS, HEAD_DIM).sum(0), 'dil_k_norm': dgk.reshape(N_HEADS, HEAD_DIM).sum(0),
             't5_tables': d_tabs}
    return jnp.concatenate([dq, dk, dv], axis=1), grads


S5_LANES = S5_G * S5_P
SCAN_SEGMENTS = 8
SCAN_W = 256


def _f_s5_prep(bre, bim, lr, li, logdt_col, expand):
    dt = jnp.sum(jnp.exp(logdt_col) * expand, axis=0, keepdims=True)
    mag = jnp.exp(lr * dt)
    ar, ai = mag * jnp.cos(li * dt), mag * jnp.sin(li * dt)
    den = lr * lr + li * li
    nr, ni = ar - 1.0, ai
    zr = (nr * lr + ni * li) / den
    zi = (ni * lr - nr * li) / den
    bb = jnp.concatenate([zr * bre - zi * bim, zr * bim + zi * bre], axis=1)
    a_rows = jnp.broadcast_to(jnp.concatenate([ar, ai], axis=1), bb.shape)
    return bb, a_rows


def _s5_scan(x, a_rows, name, reverse=False, h=None):
    S = x.shape[0]
    NL = x.shape[1] // 2
    T = S // SCAN_SEGMENTS
    nblk = NL // SCAN_W
    n_in = 4 if reverse else 2

    def body(*refs):
        if reverse:
            (x_hbm, pr_hbm, pi_hbm, ar_ref, ai_ref, hr_hbm, hi_hbm, dar_ref, dai_ref,
             xr_s, xi_s, pr_s, pi_s, hr_s, hi_s, in_sems, out_sems) = refs
        else:
            x_hbm, ar_ref, ai_ref, hr_hbm, hi_hbm, xr_s, xi_s, hr_s, hi_s, in_sems, out_sems = refs
        col = pl.multiple_of(pl.program_id(0) * SCAN_W, SCAN_W)
        loads = []
        for k in range(SCAN_SEGMENTS):
            rows = pl.ds(k * T, T)
            sources = [(x_hbm, col, xr_s), (x_hbm, NL + col, xi_s)]
            if reverse:
                sources += [(pr_hbm, col, pr_s), (pi_hbm, col, pi_s)]
            for i, (src, c0, dst) in enumerate(sources):
                loads.append(pltpu.make_async_copy(src.at[rows, pl.ds(c0, SCAN_W)], dst.at[:, k, :],
                                                   in_sems.at[i * SCAN_SEGMENTS + k]))
        for cp in loads:
            cp.start()
        for cp in loads:
            cp.wait()
        ar = ar_ref[...]
        ai = -ai_ref[...] if reverse else ai_ref[...]
        zero = jnp.zeros((SCAN_SEGMENTS, SCAN_W), f32)

        def at(s):
            return T - 1 - s if reverse else s

        def local(s, c):
            hr, hi, pr, pi = c
            j = at(s)
            nhr = ar * hr - ai * hi + xr_s[j]
            nhi = ar * hi + ai * hr + xi_s[j]
            hr_s[j] = nhr
            hi_s[j] = nhi
            return nhr, nhi, ar * pr - ai * pi, ar * pi + ai * pr

        er, ei, pr, pi = lax.fori_loop(0, T, local, (zero, zero, zero + 1.0, zero), unroll=2)
        row = lax.broadcasted_iota(jnp.int32, (SCAN_SEGMENTS, SCAN_W), 0)
        cr, ci = zero, zero
        order = range(SCAN_SEGMENTS - 2, -1, -1) if reverse else range(1, SCAN_SEGMENTS)
        for k in order:
            src = k + 1 if reverse else k - 1
            tr = er + pr * cr - pi * ci
            ti = ei + pr * ci + pi * cr
            cr = jnp.where(row == k, jnp.sum(jnp.where(row == src, tr, 0.0), axis=0, keepdims=True), cr)
            ci = jnp.where(row == k, jnp.sum(jnp.where(row == src, ti, 0.0), axis=0, keepdims=True), ci)

        def fix_at(j, c, before):
            pr, pi, sr, si = c
            pr, pi = ar * pr - ai * pi, ar * pi + ai * pr
            hr = hr_s[j] + pr * cr - pi * ci
            hi = hi_s[j] + pr * ci + pi * cr
            hr_s[j] = hr
            hi_s[j] = hi
            if reverse:
                qr, qi = before
                sr = sr + hr * qr + hi * qi
                si = si + hi * qr - hr * qi
            return pr, pi, sr, si

        start = (zero + 1.0, zero, zero, zero)
        if reverse:
            def fix(s, c):
                j = T - 1 - s
                return fix_at(j, c, (pr_s[j - 1], pi_s[j - 1]))

            c = lax.fori_loop(0, T - 1, fix, start, unroll=2)
            last_r = jnp.where(row == 0, 0.0, pltpu.roll(pr_s[T - 1], 1, 0))
            last_i = jnp.where(row == 0, 0.0, pltpu.roll(pi_s[T - 1], 1, 0))
            _, _, sr, si = fix_at(0, c, (last_r, last_i))
            dar_ref[...] = sr
            dai_ref[...] = si
        else:
            lax.fori_loop(0, T, lambda s, c: fix_at(s, c, None), start, unroll=2)
        stores = []
        for k in range(SCAN_SEGMENTS):
            rows = pl.ds(k * T, T)
            stores.append(pltpu.make_async_copy(hr_s.at[:, k, :], hr_hbm.at[rows, pl.ds(col, SCAN_W)], out_sems.at[k]))
            stores.append(pltpu.make_async_copy(hi_s.at[:, k, :], hi_hbm.at[rows, pl.ds(col, SCAN_W)],
                                                out_sems.at[SCAN_SEGMENTS + k]))
        for cp in stores:
            cp.start()
        for cp in stores:
            cp.wait()

    a_re = pl.BlockSpec((SCAN_SEGMENTS, SCAN_W), lambda b: (0, b))
    a_im = pl.BlockSpec((SCAN_SEGMENTS, SCAN_W), lambda b: (0, nblk + b))
    seq = pltpu.VMEM((T, SCAN_SEGMENTS, SCAN_W), f32)
    if reverse:
        in_specs, args = [ANY, ANY, ANY, a_re, a_im], [x, h[0], h[1], a_rows, a_rows]
        out_specs = [ANY, ANY, a_re, a_re]
        out_shape = [jax.ShapeDtypeStruct((S, NL), f32)] * 2 + [jax.ShapeDtypeStruct((SCAN_SEGMENTS, NL), f32)] * 2
    else:
        in_specs, args = [ANY, a_re, a_im], [x, a_rows, a_rows]
        out_specs = [ANY, ANY]
        out_shape = [jax.ShapeDtypeStruct((S, NL), f32)] * 2
    scratch = [seq] * (n_in + 2) + [pltpu.SemaphoreType.DMA((n_in * SCAN_SEGMENTS,)),
                                    pltpu.SemaphoreType.DMA((2 * SCAN_SEGMENTS,))]
    return pl.pallas_call(body, name=name, grid=(nblk,), in_specs=in_specs, out_specs=out_specs, out_shape=out_shape,
                          scratch_shapes=scratch, compiler_params=_cparams(('arbitrary',), big=True))(*args)


def _f_s5_post(y, u, d, w_glu):
    z = _bmm(y + d * u, w_glu)
    return (z[:, :GROUP_W] * jax.nn.sigmoid(z[:, GROUP_W:]),)


def _block_diag(t):
    G, a, b = t.shape
    eye = jnp.eye(G, dtype=t.dtype)
    return (t[:, :, None, :] * eye[:, None, :, None]).reshape(G * a, G * b)


def _diag_blocks(m, a, b):
    G = m.shape[0] // a
    return jnp.moveaxis(jnp.diagonal(m.reshape(G, a, G, b), axis1=0, axis2=2), -1, 0)


def _s5_fwd(u, mp, l):
    S = u.shape[0]
    tm = _pick(S, (256, 128))
    bre = _block_diag(mp['s5_b_re'].transpose(0, 2, 1))
    bim = _block_diag(mp['s5_b_im'].transpose(0, 2, 1))
    expand = jnp.repeat(jnp.eye(S5_G, dtype=f32), S5_P, axis=1)
    prep_params = [mp['s5_lambda_re'].reshape(1, S5_LANES), mp['s5_lambda_im'].reshape(1, S5_LANES),
                   mp['s5_log_dt'].reshape(S5_G, 1), expand]
    bb, a_rows = _tile_fwd(_f_s5_prep, [bre, bim], prep_params, [(2 * S5_LANES, f32)] * 2, GROUP_W, f's5_prep_fwd_{l}')
    x = _mm(u, bb, 'nn', f's5_in_fwd_{l}')
    hr, hi = _s5_scan(x, a_rows, f's5_scan_fwd_{l}')
    c_re, c_im = _block_diag(mp['s5_c_re'].transpose(0, 2, 1)), -_block_diag(mp['s5_c_im'].transpose(0, 2, 1))
    y = _mm(hi, c_im, 'nn', f's5_out_im_fwd_{l}', add=_mm(hr, c_re, 'nn', f's5_out_re_fwd_{l}'))
    post_params = [mp['s5_d'][None], mp['s5_w_glu']]
    (out,) = _tile_fwd(_f_s5_post, [y, u], post_params, [(GROUP_W, f32)], tm, f's5_post_fwd_{l}')
    return out, (u, bre, bim, prep_params, bb, a_rows, hr, hi, c_re, c_im, y, post_params)


def _s5_bwd(dout, saved, l):
    u, bre, bim, prep_params, bb, a_rows, hr, hi, c_re, c_im, y, post_params = saved
    S = u.shape[0]
    tm = _pick(S, (256, 128))
    (dy, du1), (dd, dwglu) = _tile_bwd(_f_s5_post, [y, u], post_params, [dout], [True, True], [True, True], tm,
                                       f's5_post_bwd_{l}')
    ccat = jnp.concatenate([c_re, c_im], axis=0)
    dh = _mm(dy, ccat, 'nt', f's5_out_dx_{l}')
    dccat = jnp.concatenate([_mm(hr, dy, 'tn', f's5_out_re_dw_{l}'), _mm(hi, dy, 'tn', f's5_out_im_dw_{l}')], axis=0)
    lr_, li_, dar, dai = _s5_scan(dh, a_rows, f's5_scan_bwd_{l}', reverse=True, h=(hr, hi))
    du2 = _mm(li_, bb[:, S5_LANES:], 'nt', f's5_in_im_dx_{l}', add=_mm(lr_, bb[:, :S5_LANES], 'nt', f's5_in_re_dx_{l}'))
    dbb = jnp.concatenate([_mm(u, lr_, 'tn', f's5_in_re_dw_{l}'), _mm(u, li_, 'tn', f's5_in_im_dw_{l}')], axis=1)
    da_rows = jnp.pad(jnp.concatenate([dar, dai], axis=1), ((0, GROUP_W - SCAN_SEGMENTS), (0, 0)))
    (dbre, dbim), (dlr, dli, dlogdt) = _tile_bwd(_f_s5_prep, [bre, bim], prep_params, [dbb, da_rows], [True, True],
                                                 [True, True, True, False], GROUP_W, f's5_prep_bwd_{l}')
    grads = {
        's5_lambda_re': dlr.reshape(S5_G, S5_P), 's5_lambda_im': dli.reshape(S5_G, S5_P), 's5_log_dt': dlogdt[:, 0],
        's5_b_re': _diag_blocks(dbre, S5_CG, S5_P).transpose(0, 2, 1),
        's5_b_im': _diag_blocks(dbim, S5_CG, S5_P).transpose(0, 2, 1),
        's5_c_re': _diag_blocks(dccat[:S5_LANES], S5_P, S5_CG).transpose(0, 2, 1),
        's5_c_im': -_diag_blocks(dccat[S5_LANES:], S5_P, S5_CG).transpose(0, 2, 1),
        's5_d': dd[0], 's5_w_glu': dwglu}
    return du1 + du2, grads


DN_CONV = 4


def _head_sum_matrix():
    h = np.arange(GROUP_W) // HEAD_DIM
    return jnp.asarray((h[:, None] == h[None, :]).astype(np.float32))


def _f_dn_pre(x0, x1, x2, x3, ab, w0, w1, w2, w3, alog, dtb, ea, eb, hs):
    c = w0 * x0 + w1 * x1 + w2 * x2 + w3 * x3
    s = c * jax.nn.sigmoid(c)
    q, k, v = s[:, :GROUP_W], s[:, GROUP_W:2 * GROUP_W], s[:, 2 * GROUP_W:]
    q = q * lax.rsqrt(_hdot(q * q, hs) + EPS) * (HEAD_DIM ** -0.5)
    k = k * lax.rsqrt(_hdot(k * k, hs) + EPS)
    beta = jax.nn.sigmoid(_hdot(ab, eb))
    g = -jnp.exp(alog) * jax.nn.softplus(_hdot(ab, ea) + dtb)
    return q, k, v, g, beta


DN_CHUNKS_PER_STEP = 4


def _f_dn_chunks(q, k, v, g, beta):
    C = DN_CHUNK
    n_chunks = q.shape[0] // C
    r = lax.broadcasted_iota(jnp.int32, (C, C), 0)
    c = lax.broadcasted_iota(jnp.int32, (C, C), 1)
    causal, strict = r >= c, r > c
    eye = (r == c).astype(f32)
    tril = causal.astype(f32)
    ones = jnp.ones((C, GROUP_W), f32)
    masks = [_head_mask(h) for h in range(N_HEADS)]
    rows = [tuple(t[i * C:(i + 1) * C] for t in (q, k, v, g, beta)) for i in range(n_chunks)]
    gcs = [_hdot(tril, gi) for (_, _, _, gi, _) in rows]
    items = [(i, h) for i in range(n_chunks) for h in range(N_HEADS)]
    grows = [_hdot_nt(ones * (masks[h] * (1.0 / HEAD_DIM)), gcs[i]) for i, h in items]
    decs = []
    for (i, h), grow in zip(items, grows):
        gcol = jnp.sum(gcs[i] * masks[h], axis=1, keepdims=True) * (1.0 / HEAD_DIM)
        decs.append(jnp.exp(jnp.where(causal, gcol - grow, NEG_INF)))
    kbs = [ki * bi for (_, ki, _, _, bi) in rows]
    kks = [_bmm_nt(kbs[i] * masks[h], rows[i][1]) for i, h in items]
    qks = [_bmm_nt(rows[i][0] * masks[h], rows[i][1]) for i, h in items]
    lmats = [jnp.where(strict, kk * dec, 0.0) for kk, dec in zip(kks, decs)]
    a_qk = [jnp.where(causal, qk * dec, 0.0) for qk, dec in zip(qks, decs)]
    ts = [eye - lm for lm in lmats]
    ps = lmats
    for _ in range(5):
        ps = [_bmm(p, p) for p in ps]
        ts = [t + _bmm(t, p) for t, p in zip(ts, ps)]
    egs = [jnp.exp(gc) for gc in gcs]
    tw = [_bmm(t, kbs[i] * egs[i]) for (i, h), t in zip(items, ts)]
    tu = [_bmm(t, rows[i][2] * rows[i][4]) for (i, h), t in zip(items, ts)]
    outs = []
    for i in range(n_chunks):
        qi, ki, _, gi, _ = rows[i]
        glast = jnp.sum(gi, axis=0, keepdims=True)
        w = sum(tw[i * N_HEADS + h] * masks[h] for h in range(N_HEADS))
        u = sum(tu[i * N_HEADS + h] * masks[h] for h in range(N_HEADS))
        outs.append((w, u, qi * egs[i], ki * jnp.exp(glast - gcs[i]), *a_qk[i * N_HEADS:(i + 1) * N_HEADS],
                     jnp.broadcast_to(jnp.exp(glast), (C, GROUP_W))))
    return tuple(jnp.concatenate(parts, axis=0) for parts in zip(*outs))


def _f_dn_step(w, u, qd, kdec, a0, a1, a2, a3, dfull, state, bd):
    row0 = (lax.broadcasted_iota(jnp.int32, dfull.shape, 0) == 0).astype(f32)
    dvec = jnp.sum(dfull * row0, axis=0, keepdims=True)
    ws, qs = _bmm(w, state), _bmm(qd, state)
    vnew = u - ws
    avs = [_bmm(a, vnew) for a in (a0, a1, a2, a3)]
    kv = _bmm_tn(kdec, vnew)
    o = qs + sum(av * _head_mask(h) for h, av in enumerate(avs))
    return o, state * dvec + bd * kv


def _dn_scan_fwd(ins, name):
    S = ins[0].shape[0]
    N = S // DN_CHUNK
    bd = _head_sum_matrix()

    def body(*refs):
        o_ref, s_ref, state = refs[10], refs[11], refs[12]

        @pl.when(pl.program_id(0) == 0)
        def _():
            state[...] = jnp.zeros_like(state)

        s_in = state[...]
        s_ref[0] = s_in
        o, s_out = _f_dn_step(*[r[...] for r in refs[:9]], s_in, refs[9][...])
        o_ref[...] = o
        state[...] = s_out

    return pl.pallas_call(
        body, name=name, grid=(N,),
        in_specs=[pl.BlockSpec((DN_CHUNK, t.shape[1]), lambda n: (n, 0)) for t in ins] + [_full_spec(bd)],
        out_specs=[pl.BlockSpec((DN_CHUNK, GROUP_W), lambda n: (n, 0)), pl.BlockSpec((1, GROUP_W, GROUP_W), lambda n: (n, 0, 0))],
        out_shape=[jax.ShapeDtypeStruct((S, GROUP_W), f32), jax.ShapeDtypeStruct((N, GROUP_W, GROUP_W), f32)],
        scratch_shapes=[pltpu.VMEM((GROUP_W, GROUP_W), f32)],
        compiler_params=_cparams(('arbitrary',)),
    )(*ins, bd)


def _dn_scan_bwd(ins, states, do, name):
    S = ins[0].shape[0]
    N = S // DN_CHUNK
    bd = _head_sum_matrix()

    def body(*refs):
        s_ref, do_ref = refs[9], refs[10]
        bd_ref = refs[11]
        outs = refs[12:21]
        dstate = refs[21]

        @pl.when(pl.program_id(0) == 0)
        def _():
            dstate[...] = jnp.zeros_like(dstate)

        bd_val = bd_ref[...]
        _, vjp = jax.vjp(lambda *a: _f_dn_step(*a, bd_val), *[r[...] for r in refs[:9]], s_ref[0])
        grads = vjp((do_ref[...], dstate[...]))
        for o, g in zip(outs, grads[:9]):
            o[...] = g
        dstate[...] = grads[9]

    def rev(n):
        return (N - 1 - n, 0)

    res = pl.pallas_call(
        body, name=name, grid=(N,),
        in_specs=[pl.BlockSpec((DN_CHUNK, t.shape[1]), rev) for t in ins] +
                 [pl.BlockSpec((1, GROUP_W, GROUP_W), lambda n: (N - 1 - n, 0, 0)), pl.BlockSpec((DN_CHUNK, GROUP_W), rev),
                  _full_spec(bd)],
        out_specs=[pl.BlockSpec((DN_CHUNK, t.shape[1]), rev) for t in ins],
        out_shape=[jax.ShapeDtypeStruct(t.shape, f32) for t in ins],
        scratch_shapes=[pltpu.VMEM((GROUP_W, GROUP_W), f32)],
        compiler_params=_cparams(('arbitrary',)),
    )(*ins, states, do, bd)
    return list(res)


def _f_dn_post(o, gate, gain, hmean):
    return (o * lax.rsqrt(_hdot(o * o, hmean) + EPS) * gain * (gate * jax.nn.sigmoid(gate)),)


def _dn_delays(x, name):
    S, C = x.shape
    tm = _pick(S, (256, 128))

    def body(prev_ref, cur_ref, *outs):
        before = jnp.where(pl.program_id(0) > 0, prev_ref[...], 0.0)
        both = jnp.concatenate([before, cur_ref[...]], axis=0)
        for o, k in zip(outs, range(DN_CONV - 1, 0, -1)):
            o[...] = pltpu.roll(both, k, 0)[tm:]

    spec = pl.BlockSpec((tm, C), lambda i: (i, 0))
    return pl.pallas_call(
        body, name=name, grid=(S // tm,),
        in_specs=[pl.BlockSpec((tm, C), lambda i: (jnp.maximum(i - 1, 0), 0)), spec],
        out_specs=[spec] * (DN_CONV - 1), out_shape=[jax.ShapeDtypeStruct((S, C), x.dtype)] * (DN_CONV - 1),
        compiler_params=_cparams(('parallel',), big=True),
    )(x, x)


def _dn_undelay_sum(ds, name):
    S, C = ds[0].shape
    tm = _pick(S, (256, 128))
    n = S // tm

    def body(*refs):
        o = refs[-1]
        total = refs[2 * (DN_CONV - 1)][...]
        for j in range(DN_CONV - 1):
            k = DN_CONV - 1 - j
            after = jnp.where(pl.program_id(0) < n - 1, refs[2 * j + 1][...], 0.0)
            both = jnp.concatenate([refs[2 * j][...], after], axis=0)
            total = total + pltpu.roll(both, 2 * tm - k, 0)[:tm]
        o[...] = total

    spec = pl.BlockSpec((tm, C), lambda i: (i, 0))
    nxt = pl.BlockSpec((tm, C), lambda i: (jnp.minimum(i + 1, n - 1), 0))
    args, in_specs = [], []
    for j in range(DN_CONV - 1):
        args += [ds[j], ds[j]]
        in_specs += [spec, nxt]
    return pl.pallas_call(
        body, name=name, grid=(n,), in_specs=in_specs + [spec], out_specs=spec,
        out_shape=jax.ShapeDtypeStruct((S, C), f32), compiler_params=_cparams(('parallel',), big=True),
    )(*args, ds[DN_CONV - 1])


def _dn_fwd(qkv, a, b, gate, mp, l):
    S = qkv.shape[0]
    tm = _pick(S, (256, 128))
    xs = [*_dn_delays(qkv, f'dn_delay_{l}'), qkv]
    ab = jnp.pad(jnp.concatenate([a, b], axis=1), ((0, 0), (0, LANES - 2 * N_HEADS)))
    sel = np.zeros((2, LANES, GROUP_W), np.float32)
    for h in range(N_HEADS):
        sel[0, h, h * HEAD_DIM:(h + 1) * HEAD_DIM] = 1.0
        sel[1, N_HEADS + h, h * HEAD_DIM:(h + 1) * HEAD_DIM] = 1.0
    pre_params = [*[mp['dn_conv'][j][None] for j in range(DN_CONV)], jnp.repeat(mp['dn_a_log'], HEAD_DIM)[None],
                  jnp.repeat(mp['dn_dt_bias'], HEAD_DIM)[None], jnp.asarray(sel[0]), jnp.asarray(sel[1]), _head_sum_matrix()]
    pre = _tile_fwd(_f_dn_pre, [*xs, ab], pre_params, [(GROUP_W, f32)] * 5, tm, f'dn_pre_fwd_{l}')
    chunk_outs = [(GROUP_W, f32)] * 4 + [(HEAD_DIM, f32)] * 4 + [(GROUP_W, f32)]
    parts = _tile_fwd(_f_dn_chunks, pre, [], chunk_outs, DN_CHUNK * DN_CHUNKS_PER_STEP, f'dn_chunk_fwd_{l}')
    o, states = _dn_scan_fwd(parts, f'dn_scan_fwd_{l}')
    post_params = [jnp.tile(mp['dn_o_norm'], N_HEADS)[None], _head_mean_matrix()]
    (y,) = _tile_fwd(_f_dn_post, [o, gate], post_params, [(GROUP_W, f32)], tm, f'dn_post_fwd_{l}')
    return y, (xs, ab, pre_params, pre, parts, states, o, gate, post_params)


def _dn_bwd(dy, saved, l):
    xs, ab, pre_params, pre, parts, states, o, gate, post_params = saved
    S = dy.shape[0]
    tm = _pick(S, (256, 128))
    (do, dgate), (dgain,) = _tile_bwd(_f_dn_post, [o, gate], post_params, [dy], [True, True], [True, False], tm,
                                      f'dn_post_bwd_{l}')
    dparts = _dn_scan_bwd(parts, states, do, f'dn_scan_bwd_{l}')
    dpre, _ = _tile_bwd(_f_dn_chunks, pre, [], dparts, [True] * 5, [], DN_CHUNK * DN_CHUNKS_PER_STEP, f'dn_chunk_bwd_{l}')
    dins, dpar = _tile_bwd(_f_dn_pre, [*xs, ab], pre_params, dpre, [True] * 5, [True] * 6 + [False] * 3, tm,
                           f'dn_pre_bwd_{l}')
    dqkv = _dn_undelay_sum(dins[:DN_CONV], f'dn_undelay_{l}')
    dab = dins[DN_CONV]
    grads = {'dn_conv': jnp.concatenate(dpar[:DN_CONV], axis=0),
             'dn_a_log': dpar[4].reshape(N_HEADS, HEAD_DIM).sum(1), 'dn_dt_bias': dpar[5].reshape(N_HEADS, HEAD_DIM).sum(1),
             'dn_o_norm': dgain.reshape(N_HEADS, HEAD_DIM).sum(0)}
    return dqkv, dab[:, :N_HEADS], dab[:, N_HEADS:2 * N_HEADS], dgate, grads


def _t5_bucket(dist):
    exact = T5_BUCKETS // 2
    df = jnp.maximum(dist, 1).astype(f32)
    large = exact + (jnp.log(df / exact) / math.log(T5_MAX_DIST / exact) * (T5_BUCKETS - exact)).astype(jnp.int32)
    large = jnp.minimum(large, T5_BUCKETS - 1)
    return jnp.where(dist < exact, dist, large)


def _split_cols(t, sizes):
    out, start = [], 0
    for s in sizes:
        out.append(t[..., start:start + s])
        start += s
    return out


def _mixers_fwd(proj, mp, l):
    c_q, c_kv, k_rope, u_s5, qkv_dil, qkv_dn, a_dn, b_dn, gate_dn = _split_cols(proj, IN_SPLITS)
    y_mla, s_mla = _mla_fwd(c_q, c_kv, k_rope, mp, l)
    y_s5, s_s5 = _s5_fwd(u_s5, mp, l)
    y_dil, s_dil = _dil_fwd(qkv_dil, mp, l)
    y_dn, s_dn = _dn_fwd(qkv_dn, a_dn, b_dn, gate_dn, mp, l)
    return jnp.concatenate([y_mla, y_s5, y_dil, y_dn], axis=-1), (s_mla, s_s5, s_dil, s_dn)


def _mixers_bwd(dmixed, saved, l):
    s_mla, s_s5, s_dil, s_dn = saved
    d_mla, d_s5, d_dil, d_dn = _split_cols(dmixed, (GROUP_W,) * 4)
    dc_q, dc_kv, dk_rope, g_mla = _mla_bwd(d_mla, s_mla, l)
    du, g_s5 = _s5_bwd(d_s5, s_s5, l)
    dqkv_dil, g_dil = _dil_bwd(d_dil, s_dil, l)
    dqkv_dn, da, db, dgate, g_dn = _dn_bwd(d_dn, s_dn, l)
    parts = [dc_q, dc_kv, dk_rope, du, dqkv_dil, dqkv_dn, da, db, dgate]
    dproj = jnp.concatenate([p.astype(bf16) for p in parts], axis=-1)
    return dproj, {**g_mla, **g_s5, **g_dil, **g_dn}


MIXER_PARAMS = ['mla_q_norm', 'mla_kv_norm', 'mla_w_uq', 'mla_w_ukv', 'mla_qk_q', 'mla_qk_k', 's5_lambda_re',
                's5_lambda_im', 's5_log_dt', 's5_b_re', 's5_b_im', 's5_c_re', 's5_c_im', 's5_d', 's5_w_glu',
                'dil_q_norm', 'dil_k_norm', 't5_bias', 'dn_conv', 'dn_a_log', 'dn_dt_bias', 'dn_o_norm']


def _layer_fwd_mix(h, W, l):
    S = h.shape[0]
    tm = _pick(S, (256, 128))
    g1 = W['attn_norm'][l][None]
    (n1,) = _tile_fwd(_f_rms, [h], [g1], [(D_MODEL, bf16)], tm, f'rms1_fwd_{l}')
    proj = _mm(n1, W['w_in'][l], 'nn', f'proj_fwd_{l}')
    mp = {k: (W[k] if k == 't5_bias' else W[k][l]).astype(f32) for k in MIXER_PARAMS}
    if 'dil_tables' in W:
        mp['dil_tables'] = W['dil_tables']
    mixed, mix_saved = _mixers_fwd(proj, mp, l)
    mixed_b = mixed.astype(bf16)
    h2 = _mm(mixed_b, W['w_out'][l], 'nn', f'out_fwd_{l}', add=h)
    return h2, dict(h=h, n1=n1, mix=mix_saved, mixed=mixed_b, h2=h2)


def _layer_fwd_ffn(h2, W, l, saved):
    S = h2.shape[0]
    tm = _pick(S, (256, 128))
    g2 = W['ffn_norm'][l][None]
    (n2,) = _tile_fwd(_f_rms, [h2], [g2], [(D_MODEL, bf16)], tm, f'rms2_fwd_{l}')
    u = _mm(n2, W['ffn_w1'][l], 'nt', f'ffn1_fwd_{l}', out_dtype=bf16)
    v = _mm(n2, W['ffn_w3'][l], 'nt', f'ffn3_fwd_{l}', out_dtype=bf16)
    (act,) = _tile_fwd(_f_swiglu, [u, v], [], [(FFN_HIDDEN, bf16)], tm, f'swiglu_fwd_{l}')
    h3 = _mm(act, W['ffn_w2'][l], 'nn', f'ffn2_fwd_{l}', add=h2)
    saved.update(n2=n2, u=u, v=v, act=act)
    return h3


def _layer_bwd_ffn(dh3, saved, W, l):
    S = dh3.shape[0]
    tm = _pick(S, (256, 128))
    g2 = W['ffn_norm'][l][None]
    grads = {}
    dact = _mm(dh3, W['ffn_w2'][l], 'nt', f'ffn2_dx_{l}', out_dtype=bf16)
    grads['ffn_w2'] = _mm(saved['act'], dh3, 'tn', f'ffn2_dw_{l}', out_dtype=bf16)
    (du, dv), _ = _tile_bwd(_f_swiglu, [saved['u'], saved['v']], [], [dact], [True, True], [], tm, f'swiglu_bwd_{l}',
                            dt_dtypes=[bf16, bf16])
    dn2 = _mm(dv, W['ffn_w3'][l], 'nn', f'ffn3_dx_{l}', add=_mm(du, W['ffn_w1'][l], 'nn', f'ffn1_dx_{l}'))
    grads['ffn_w1'] = _mm(du, saved['n2'], 'tn', f'ffn1_dw_{l}', out_dtype=bf16)
    grads['ffn_w3'] = _mm(dv, saved['n2'], 'tn', f'ffn3_dw_{l}', out_dtype=bf16)
    (dh2n,), (dg2,) = _tile_bwd(_f_rms, [saved['h2']], [g2], [dn2], [True], [True], tm, f'rms2_bwd_{l}')
    grads['ffn_norm'] = dg2[0]
    return (dh3, dh2n), grads


def _layer_bwd_mix(dh2, saved, W, l):
    S = dh2.shape[0]
    tm = _pick(S, (256, 128))
    g1 = W['attn_norm'][l][None]
    grads = {}
    dmixed = _mm(dh2, W['w_out'][l], 'nt', f'out_dx_{l}')
    grads['w_out'] = _mm(saved['mixed'], dh2, 'tn', f'out_dw_{l}', out_dtype=bf16)
    dproj, dmp = _mixers_bwd(dmixed, saved['mix'], l)
    grads.update(dmp)
    dn1 = _mm(dproj, W['w_in'][l], 'nt', f'proj_dx_{l}')
    grads['w_in'] = _mm(saved['n1'], dproj, 'tn', f'proj_dw_{l}', out_dtype=bf16)
    (dh1n,), (dg1,) = _tile_bwd(_f_rms, [saved['h']], [g1], [dn1], [True], [True], tm, f'rms1_bwd_{l}')
    grads['attn_norm'] = dg1[0]
    return (dh2, dh1n), grads


def kernel(x, attn_norm, w_in, w_out, mla_q_norm, mla_kv_norm, mla_w_uq, mla_w_ukv, mla_qk_q, mla_qk_k, s5_lambda_re, s5_lambda_im, s5_log_dt, s5_b_re, s5_b_im, s5_c_re, s5_c_im, s5_d, s5_w_glu, dil_q_norm, dil_k_norm, t5_bias, dn_conv, dn_a_log, dn_dt_bias, dn_o_norm, ffn_norm, ffn_w1, ffn_w3, ffn_w2, loss_target, m_attn_norm, m_w_in, m_w_out, m_mla_q_norm, m_mla_kv_norm, m_mla_w_uq, m_mla_w_ukv, m_mla_qk_q, m_mla_qk_k, m_s5_lambda_re, m_s5_lambda_im, m_s5_log_dt, m_s5_b_re, m_s5_b_im, m_s5_c_re, m_s5_c_im, m_s5_d, m_s5_w_glu, m_dil_q_norm, m_dil_k_norm, m_t5_bias, m_dn_conv, m_dn_a_log, m_dn_dt_bias, m_dn_o_norm, m_ffn_norm, m_ffn_w1, m_ffn_w3, m_ffn_w2, v_attn_norm, v_w_in, v_w_out, v_mla_q_norm, v_mla_kv_norm, v_mla_w_uq, v_mla_w_ukv, v_mla_qk_q, v_mla_qk_k, v_s5_lambda_re, v_s5_lambda_im, v_s5_log_dt, v_s5_b_re, v_s5_b_im, v_s5_c_re, v_s5_c_im, v_s5_d, v_s5_w_glu, v_dil_q_norm, v_dil_k_norm, v_t5_bias, v_dn_conv, v_dn_a_log, v_dn_dt_bias, v_dn_o_norm, v_ffn_norm, v_ffn_w1, v_ffn_w3, v_ffn_w2):
    given = dict(locals())
    def seen(n, t):
        return jnp.swapaxes(t, 1, 2) if n in TRANSPOSED else t

    w_loc = {n: seen(n, given[n]) for n in WEIGHTS}
    m_loc = {n: seen(n, given['m_' + n]) for n in WEIGHTS}
    v_loc = {n: seen(n, given['v_' + n]) for n in WEIGHTS}
    big_names = list(BIG)

    own = 2 * lax.axis_index('x') + lax.axis_index('y')
    groups = [[(n, 0) for n in GATHER_FIRST], [(n, 0) for n in GATHER_FFN], [(n, 1) for n in big_names]]
    started, order = [], jnp.zeros((8, LANES), f32)
    for gi, group in enumerate(groups):
        blocks = [w_loc[n][l].astype(bf16) for n, l in group]
        lands = [lax.empty((N_SHARDS,) + b.shape, bf16) for b in blocks]
        send_sems, recv_sems, blocks, lands, order = _to_chips_start(blocks, lands, False, order, f'gather_start_{gi}')
        started.append((send_sems, recv_sems, blocks, lands))
    W = {n: [None] * DEPTH for n in big_names}
    for n in SMALL:
        W[n] = w_loc[n]
    W['dil_tables'] = _dil_tables(w_loc['t5_bias'])

    def arrive(gi, after):
        send_sems, recv_sems, blocks, lands = started[gi]
        blocks, lands = _to_chips_wait(send_sems, recv_sems, blocks, lands, False, after, f'gather_wait_{gi}')
        for (n, l), block, land in zip(groups[gi], blocks, lands):
            W[n][l] = _from_shards(n, lax.dynamic_update_slice(land, block[None], (own, 0, 0)))

    arrive(0, order)
    h = x[0]
    saved = []
    for l in range(DEPTH):
        h2, sv = _layer_fwd_mix(h, W, l)
        if l == 0:
            arrive(1, h2)
        h = _layer_fwd_ffn(h2, W, l, sv)
        if l == 0:
            arrive(2, h)
        saved.append(sv)
    parts_loss, dh = _loss_head(h, loss_target[0])
    local_loss = jnp.sum(parts_loss)

    layer_grads = [dict() for _ in range(DEPTH)]
    sent = []

    def send(group, tag):
        srcs = [_by_shard(n, layer_grads[l][n]).astype(bf16) for n, l in group]
        lands = [lax.empty((3,) + s.shape[1:], bf16) for s in srcs]
        send_sems, recv_sems, srcs, lands, token = _to_chips_start(srcs, lands, True, jnp.zeros((8, LANES), f32),
                                                                   f'reduce_start_{tag}')
        sent.append((group, tag, send_sems, recv_sems, srcs, lands))
        return token[0, 0]

    for l in reversed(range(DEPTH)):
        (dh3, dh2n), g_ffn = _layer_bwd_ffn(dh, saved[l], W, l)
        layer_grads[l].update(g_ffn)
        dh2 = dh3 + dh2n
        if l == 0:
            dh2 = dh2 + send([(n, 0) for n in GATHER_FFN], 'ffn0')
        (dh2, dh1n), g_mix = _layer_bwd_mix(dh2, saved[l], W, l)
        layer_grads[l].update(g_mix)
        dh = dh2 + dh1n
        if l == 1:
            dh = dh + send([(n, 1) for n in big_names], 'layer1')
    last = send([(n, 0) for n in GATHER_FIRST], 'first0')
    grad_x = dh[None]
    small_full = []
    for n in SMALL:
        if n == 't5_bias':
            small_full.append(_t5_grad([a_ + b_ for a_, b_ in zip(layer_grads[0]['t5_tables'], layer_grads[1]['t5_tables'])]))
        else:
            small_full.append(jnp.stack([layer_grads[l][n] for l in range(DEPTH)]))

    small_shapes = [w_loc[n].shape for n in SMALL] + [(1,)]
    nothing = [jnp.zeros((1,), f32)]
    small_pack = _pack(small_full + [local_loss.reshape(1)]) + last
    _, recv_small = _swap_with_sibling([], small_pack)
    chip_small = _small_chip_sum(small_pack, recv_small)
    _, from_chips_small = _exchange_between_chips([], chip_small)

    mine = {}
    for group, tag, send_sems, recv_sems, srcs, lands in sent:
        srcs, lands = _to_chips_wait(send_sems, recv_sems, srcs, lands, True, from_chips_small, f'reduce_wait_{tag}')
        for (n, l), src, land in zip(group, srcs, lands):
            mine[(n, l)] = _partial_sum(src, land, f'partial_{n}_{l}')
    keys = [(n, l) for n in big_names for l in range(DEPTH)]
    theirs = dict(zip(keys, _swap_partials([mine[k] for k in keys])))

    g_small_p, d_small_p, m_small_p, v_small_p = _small_update(
        small_pack, recv_small, from_chips_small, _pack([w_loc[n] for n in SMALL] + nothing),
        _pack([m_loc[n] for n in SMALL] + nothing), _pack([v_loc[n] for n in SMALL] + nothing))
    loss = _unpack(g_small_p, small_shapes)[-1][0]
    grad, delta, new_m, new_v = {}, {}, {}, {}
    for n, g_, d_, m_, v_ in zip(SMALL, _unpack(g_small_p, small_shapes), _unpack(d_small_p, small_shapes),
                                 _unpack(m_small_p, small_shapes), _unpack(v_small_p, small_shapes)):
        grad[n], delta[n], new_m[n], new_v[n] = g_, d_, m_, v_
    for n in big_names:
        results = _adamw(w_loc[n], m_loc[n], v_loc[n], [mine[(n, l)] for l in range(DEPTH)],
                         [theirs[(n, l)] for l in range(DEPTH)], 'adamw_' + n)
        grad[n], delta[n], new_m[n], new_v[n] = [seen(n, t) for t in results]
    return (loss, grad_x, *[grad[n] for n in WEIGHTS], *[delta[n] for n in WEIGHTS],
            *[new_m[n] for n in WEIGHTS], *[new_v[n] for n in WEIGHTS])
```

```python
import functools
import math

import numpy as np
import jax
import jax.numpy as jnp
from jax import lax
from jax.experimental import pallas as pl
from jax.experimental.pallas import tpu as pltpu

f32 = jnp.float32
bf16 = jnp.bfloat16
HI = lax.Precision.HIGHEST
MESH = pl.DeviceIdType.MESH

VMEM_LIMIT_BYTES = 48 * 1024 * 1024
MM_VMEM_BUDGET_BYTES = 32 * 1024 * 1024
LANES = 128

D_MODEL = 1024
DEPTH = 2
GROUP_W = 256
HEAD_DIM = 64
EPS = 1e-6
NEG_INF = -1e30
N_HEADS = 4
MLA_NOPE, MLA_ROPE = 64, 32
MLA_DQK = MLA_NOPE + MLA_ROPE
ROPE_THETA = 10000.0
Q_BLOCK = 128
S5_G, S5_CG, S5_P = 16, 16, 64
DIL_PAIRS = ((128, 1), (512, 4), (2048, 16))
T5_BUCKETS, T5_MAX_DIST = 32, 2048
DN_CHUNK = 64
FFN_HIDDEN = 2816
IN_SPLITS = (256, 128, 32, 256, 768, 768, 4, 4, 256)
IN_COLS = sum(IN_SPLITS)

ADAM_LR, ADAM_B1, ADAM_B2, ADAM_EPS, ADAM_WD, ADAM_STEP = 0.001, 0.9, 0.999, 1e-08, 0.01, 10

WEIGHTS = ['attn_norm', 'w_in', 'w_out', 'mla_q_norm', 'mla_kv_norm', 'mla_w_uq', 'mla_w_ukv', 'mla_qk_q', 'mla_qk_k',
           's5_lambda_re', 's5_lambda_im', 's5_log_dt', 's5_b_re', 's5_b_im', 's5_c_re', 's5_c_im', 's5_d', 's5_w_glu',
           'dil_q_norm', 'dil_k_norm', 't5_bias', 'dn_conv', 'dn_a_log', 'dn_dt_bias', 'dn_o_norm', 'ffn_norm',
           'ffn_w1', 'ffn_w3', 'ffn_w2']
BIG = {'w_in': 1, 'w_out': 1, 'mla_w_uq': 2, 'mla_w_ukv': 2, 's5_w_glu': 2, 'dn_conv': 2, 'ffn_w1': 1, 'ffn_w3': 1,
       'ffn_w2': 1}
TRANSPOSED = ('ffn_w1', 'ffn_w3')
COLUMNS_FIRST = ('w_in',)
SMALL = [n for n in WEIGHTS if n not in BIG]
GATHER_FIRST = ['w_in', 'mla_w_uq', 'mla_w_ukv', 's5_w_glu', 'dn_conv', 'w_out']
GATHER_FFN = ['ffn_w1', 'ffn_w3', 'ffn_w2']
N_SHARDS = 4
PACK_COLS = 1024


def _cparams(sem=None, big=False):
    kw = {}
    if sem is not None:
        kw['dimension_semantics'] = sem
    if big:
        kw['vmem_limit_bytes'] = VMEM_LIMIT_BYTES
    return pltpu.CompilerParams(**kw)


def _pick(n, prefs):
    for p in prefs:
        if p <= n and n % p == 0:
            return p
    return n


def _lane_tile(n, cap):
    for t in range(cap - cap % LANES, 0, -LANES):
        if n % t == 0:
            return t
    return n


def _mm(a, b, mode, name, add=None, out_dtype=f32):
    if mode == 'nn':
        (M, K), (K2, N) = a.shape, b.shape
    elif mode == 'nt':
        (M, K), (N, K2) = a.shape, b.shape
    else:
        (K, M), (K2, N) = a.shape, b.shape
    assert K == K2, (name, a.shape, b.shape)
    tk = K if K <= 2816 else _pick(K, (2816, 2048, 1408, 1024, 512))
    cap_m, cap_n = (1408 if mode == 'tn' else 512), 1408

    def need(tm_, tn_):
        per_step = tm_ * tk * a.dtype.itemsize + tk * tn_ * b.dtype.itemsize + tm_ * tn_ * jnp.dtype(out_dtype).itemsize
        if add is not None:
            per_step += tm_ * tn_ * add.dtype.itemsize
        return 2 * per_step + tm_ * tn_ * 4

    tm, tn = _lane_tile(M, cap_m), _lane_tile(N, cap_n)
    while need(tm, tn) > MM_VMEM_BUDGET_BYTES and cap_m > LANES:
        cap_m //= 2
        tm = _lane_tile(M, cap_m)
    while need(tm, tn) > MM_VMEM_BUDGET_BYTES and cap_n > LANES:
        cap_n //= 2
        tn = _lane_tile(N, cap_n)
    nk = K // tk
    dims = {'nn': (((1,), (0,)), ((), ())), 'nt': (((1,), (1,)), ((), ())), 'tn': (((0,), (0,)), ((), ()))}[mode]
    has_add = add is not None

    def body(*refs):
        a_ref, b_ref = refs[0], refs[1]
        add_ref = refs[2] if has_add else None
        o_ref = refs[3] if has_add else refs[2]
        part = lax.dot_general(a_ref[...].astype(bf16), b_ref[...].astype(bf16), dims, preferred_element_type=f32)
        if nk == 1:
            if has_add:
                part = part + add_ref[...].astype(f32)
            o_ref[...] = part.astype(out_dtype)
        else:
            acc_ref = refs[-1]
            k = pl.program_id(2)

            @pl.when(k == 0)
            def _():
                acc_ref[...] = part

            @pl.when(k > 0)
            def _():
                acc_ref[...] += part

            @pl.when(k == nk - 1)
            def _():
                r = acc_ref[...]
                if has_add:
                    r = r + add_ref[...].astype(f32)
                o_ref[...] = r.astype(out_dtype)

    if mode == 'nn':
        a_spec = pl.BlockSpec((tm, tk), lambda i, j, k: (i, k))
        b_spec = pl.BlockSpec((tk, tn), lambda i, j, k: (k, j))
    elif mode == 'nt':
        a_spec = pl.BlockSpec((tm, tk), lambda i, j, k: (i, k))
        b_spec = pl.BlockSpec((tn, tk), lambda i, j, k: (j, k))
    else:
        a_spec = pl.BlockSpec((tk, tm), lambda i, j, k: (k, i))
        b_spec = pl.BlockSpec((tk, tn), lambda i, j, k: (k, j))
    in_specs = [a_spec, b_spec]
    args = [a, b]
    if has_add:
        in_specs.append(pl.BlockSpec((tm, tn), lambda i, j, k: (i, j)))
        args.append(add)
    return pl.pallas_call(
        body, name=name, grid=(M // tm, N // tn, nk), in_specs=in_specs,
        out_specs=pl.BlockSpec((tm, tn), lambda i, j, k: (i, j)),
        out_shape=jax.ShapeDtypeStruct((M, N), out_dtype),
        scratch_shapes=[pltpu.VMEM((tm, tn), f32)] if nk > 1 else [],
        compiler_params=_cparams(('parallel', 'parallel', 'arbitrary'), big=True),
    )(*args)


def _full_spec(p):
    nd = p.ndim
    return pl.BlockSpec(p.shape, lambda i, _nd=nd: (0,) * _nd)


def _tile_fwd(f, tiled, params, outs, tm, name):
    S = tiled[0].shape[0]
    nt, npar = len(tiled), len(params)

    def body(*refs):
        vals = [r[...].astype(f32) for r in refs[:nt + npar]]
        res = f(*vals)
        for r, o in zip(res, refs[nt + npar:]):
            o[...] = r.astype(o.dtype)

    return pl.pallas_call(
        body, name=name, grid=(S // tm,),
        in_specs=[pl.BlockSpec((tm, t.shape[1]), lambda i: (i, 0)) for t in tiled] + [_full_spec(p) for p in params],
        out_specs=[pl.BlockSpec((tm, c), lambda i: (i, 0)) for c, _ in outs],
        out_shape=[jax.ShapeDtypeStruct((S, c), dt) for c, dt in outs],
        compiler_params=_cparams(('parallel',), big=True),
    )(*tiled, *params)


def _tile_bwd(f, tiled, params, cts, diff_t, diff_p, tm, name, dt_dtypes=None):
    S = tiled[0].shape[0]
    nt, npar, nc = len(tiled), len(params), len(cts)
    it = [i for i in range(nt) if diff_t[i]]
    ip = [i for i in range(npar) if diff_p[i]]
    if dt_dtypes is None:
        dt_dtypes = [f32] * len(it)

    def body(*refs):
        vals = [r[...].astype(f32) for r in refs[:nt + npar]]
        ct_vals = tuple(r[...].astype(f32) for r in refs[nt + npar:nt + npar + nc])
        out_refs = refs[nt + npar + nc:]

        def g(*dv):
            full = list(vals)
            for k, i in enumerate(it):
                full[i] = dv[k]
            for k, i in enumerate(ip):
                full[nt + i] = dv[len(it) + k]
            return tuple(f(*full))

        _, vjp = jax.vjp(g, *[vals[i] for i in it], *[vals[nt + i] for i in ip])
        grads = vjp(ct_vals)
        for k in range(len(it)):
            out_refs[k][...] = grads[k].astype(out_refs[k].dtype)
        step = pl.program_id(0)
        for k in range(len(ip)):
            o = out_refs[len(it) + k]
            gk = grads[len(it) + k]

            @pl.when(step == 0)
            def _(o=o, gk=gk):
                o[...] = gk

            @pl.when(step > 0)
            def _(o=o, gk=gk):
                o[...] += gk

    out_specs = [pl.BlockSpec((tm, tiled[i].shape[1]), lambda i_: (i_, 0)) for i in it] + [_full_spec(params[i]) for i in ip]
    out_shape = [jax.ShapeDtypeStruct(tiled[i].shape, dt_dtypes[k]) for k, i in enumerate(it)] + \
                [jax.ShapeDtypeStruct(params[i].shape, f32) for i in ip]
    res = pl.pallas_call(
        body, name=name, grid=(S // tm,),
        in_specs=[pl.BlockSpec((tm, t.shape[1]), lambda i: (i, 0)) for t in tiled] + [_full_spec(p) for p in params] +
                 [pl.BlockSpec((tm, c.shape[1]), lambda i: (i, 0)) for c in cts],
        out_specs=out_specs, out_shape=out_shape,
        compiler_params=_cparams(('arbitrary',), big=True),
    )(*tiled, *params, *cts)
    return list(res[:len(it)]), list(res[len(it):])


def _rms(x, g):
    return x * lax.rsqrt(jnp.mean(x * x, axis=-1, keepdims=True) + EPS) * g


def _f_rms(x, g):
    return (_rms(x, g),)


def _f_swiglu(u, v):
    return (u * jax.nn.sigmoid(u) * v,)


def _loss_head(y, target):
    S, D = y.shape
    tm = _pick(S, (256, 128))

    def body(y_ref, t_ref, part_ref, dy_ref):
        e = y_ref[...] - t_ref[...]
        dy_ref[...] = e * (1.0 / D)
        s = 0.5 * jnp.sum(jnp.sum(e * e, axis=1, keepdims=True), axis=0, keepdims=True) * (1.0 / D)
        r = lax.broadcasted_iota(jnp.int32, (8, LANES), 0)
        c = lax.broadcasted_iota(jnp.int32, (8, LANES), 1)
        part_ref[0] = jnp.where((r == 0) & (c == 0), s, 0.0)

    return pl.pallas_call(
        body, name='loss_head', grid=(S // tm,),
        in_specs=[pl.BlockSpec((tm, D), lambda i: (i, 0))] * 2,
        out_specs=[pl.BlockSpec((1, 8, LANES), lambda i: (i, 0, 0)), pl.BlockSpec((tm, D), lambda i: (i, 0))],
        out_shape=[jax.ShapeDtypeStruct((S // tm, 8, LANES), f32), jax.ShapeDtypeStruct((S, D), f32)],
        compiler_params=_cparams(('parallel',)),
    )(y, target)


def _pack_rows_of(shape):
    rows = -(-math.prod(shape) // PACK_COLS)
    return -(-rows // 8) * 8


def _pack(arrs):
    parts = []
    for a in arrs:
        rows = _pack_rows_of(a.shape)
        flat = a.astype(f32).reshape(-1)
        parts.append(jnp.pad(flat, (0, rows * PACK_COLS - flat.shape[0])).reshape(rows, PACK_COLS))
    return jnp.concatenate(parts, axis=0)


def _unpack(pack, shapes):
    out, row = [], 0
    for s in shapes:
        rows = _pack_rows_of(s)
        out.append(pack[row:row + rows].reshape(-1)[:math.prod(s)].reshape(s))
        row += rows
    return out


ANY = pl.BlockSpec(memory_space=pl.ANY)


def _place():
    return lax.axis_index('x'), lax.axis_index('y'), lax.axis_index('c')


def _where():
    return jnp.stack([lax.axis_index('c'), 2 * lax.axis_index('x') + lax.axis_index('y')]).astype(jnp.int32)


def _remote(src, dst, send_sems, recv_sems, k, to):
    return pltpu.make_async_remote_copy(src_ref=src, dst_ref=dst, send_sem=send_sems.at[k], recv_sem=recv_sems.at[k],
                                        device_id=to, device_id_type=MESH)


def _swap_with_sibling(gs, small):
    n = len(gs)

    def body(*refs):
        g_refs, s_ref = refs[:n], refs[n]
        r_refs, rs_ref = refs[n + 1:2 * n + 1], refs[2 * n + 1]
        send_sems, recv_sems = refs[2 * n + 2:]
        x, y, c = _place()
        sib = (x, y, 1 - c)
        cps = [_remote(g_refs[t].at[:, 1 - c], r_refs[t], send_sems, recv_sems, t, sib) for t in range(n)]
        cps.append(_remote(s_ref, rs_ref, send_sems, recv_sems, n, sib))
        for cp in cps:
            cp.start()
        for cp in cps:
            cp.wait()

    res = pl.pallas_call(
        body, name='swap_with_sibling', in_specs=[ANY] * (n + 1), out_specs=[ANY] * (n + 1),
        out_shape=[jax.ShapeDtypeStruct((N_SHARDS,) + g.shape[2:], g.dtype) for g in gs] +
                  [jax.ShapeDtypeStruct(small.shape, small.dtype)],
        scratch_shapes=[pltpu.SemaphoreType.DMA((n + 1,)), pltpu.SemaphoreType.DMA((n + 1,))],
    )(*gs, small)
    return list(res[:n]), res[n]


def _exchange_between_chips(cs, small):
    n = len(cs)

    def body(*refs):
        c_refs, s_ref = refs[:n], refs[n]
        r_refs, rs_ref = refs[n + 1:2 * n + 1], refs[2 * n + 1]
        send_sems, recv_sems = refs[2 * n + 2:]
        x, y, c = _place()
        chips = [(1 - x, y), (x, 1 - y), (1 - x, 1 - y)]
        cps = []
        for j, (px, py) in enumerate(chips):
            for t in range(n):
                cps.append(_remote(c_refs[t].at[2 * px + py], r_refs[t].at[j], send_sems, recv_sems, 3 * t + j, (px, py, c)))
            cps.append(_remote(s_ref, rs_ref.at[j], send_sems, recv_sems, 3 * n + j, (px, py, c)))
        for cp in cps:
            cp.start()
        for cp in cps:
            cp.wait()

    res = pl.pallas_call(
        body, name='exchange_between_chips', in_specs=[ANY] * (n + 1), out_specs=[ANY] * (n + 1),
        out_shape=[jax.ShapeDtypeStruct((3,) + c.shape[1:], c.dtype) for c in cs] +
                  [jax.ShapeDtypeStruct((3,) + small.shape, small.dtype)],
        scratch_shapes=[pltpu.SemaphoreType.DMA((3 * n + 3,)), pltpu.SemaphoreType.DMA((3 * n + 3,))],
    )(*cs, small)
    return list(res[:n]), res[n]


def _swap_partials(ts):
    n = len(ts)

    def body(*refs):
        t_refs, o_refs = refs[:n], refs[n:2 * n]
        send_sems, recv_sems = refs[2 * n:]
        x, y, c = _place()
        cps = [_remote(t_refs[t], o_refs[t], send_sems, recv_sems, t, (x, y, 1 - c)) for t in range(n)]
        for cp in cps:
            cp.start()
        for cp in cps:
            cp.wait()

    return pl.pallas_call(
        body, name='swap_partials', in_specs=[ANY] * n, out_specs=[ANY] * n,
        out_shape=[jax.ShapeDtypeStruct(t.shape, t.dtype) for t in ts],
        scratch_shapes=[pltpu.SemaphoreType.DMA((n,)), pltpu.SemaphoreType.DMA((n,))],
    )(*ts)


HBM = pl.BlockSpec(memory_space=pltpu.HBM)
SEM = pl.BlockSpec(memory_space=pltpu.SEMAPHORE)
DATAFLOW = pltpu.SideEffectType.DATAFLOW_SIDE_EFFECTING


def _in_hbm(t):
    return pltpu.with_memory_space_constraint(t, pltpu.HBM)


def _other_chips():
    x, y, c = _place()
    return [(1 - x, y, c), (x, 1 - y, c), (1 - x, 1 - y, c)]


def _to_chips_copies(src_refs, land_refs, send_sems, recv_sems, per_peer):
    x, y, _ = _place()
    cps = []
    for t, (src, land) in enumerate(zip(src_refs, land_refs)):
        for j, (px, py, pc) in enumerate(_other_chips()):
            s = src.at[2 * px + py] if per_peer else src
            d = land.at[j] if per_peer else land.at[2 * x + y]
            cps.append(_remote(s, d, send_sems, recv_sems, 3 * t + j, (px, py, pc)))
    return cps


def _to_chips_start(srcs, lands, per_peer, order, name):
    n = len(srcs)

    def body(*refs):
        src_refs, land_refs = refs[:n], refs[n:2 * n]
        send_sems, recv_sems = refs[2 * n + 1], refs[2 * n + 2]
        token = refs[-1]
        for cp in _to_chips_copies(src_refs, land_refs, send_sems, recv_sems, per_peer):
            cp.start()
        token[...] = jnp.zeros_like(token)

    res = pl.pallas_call(
        body, name=name, in_specs=[HBM] * (2 * n) + [ANY],
        out_specs=[SEM, SEM] + [HBM] * (2 * n) + [pl.BlockSpec(memory_space=pltpu.VMEM)],
        out_shape=[pltpu.SemaphoreType.DMA((3 * n,)), pltpu.SemaphoreType.DMA((3 * n,))] +
                  [pltpu.HBM(t.shape, t.dtype) for t in srcs] + [pltpu.HBM(t.shape, t.dtype) for t in lands] +
                  [jax.ShapeDtypeStruct((8, LANES), f32)],
        input_output_aliases={i: 2 + i for i in range(2 * n)},
        compiler_params=pltpu.CompilerParams(has_side_effects=DATAFLOW),
    )(*[_in_hbm(t) for t in srcs], *[_in_hbm(t) for t in lands], order)
    return res[0], res[1], list(res[2:2 + n]), list(res[2 + n:2 + 2 * n]), res[-1]


def _to_chips_wait(send_sems, recv_sems, srcs, lands, per_peer, after, name):
    n = len(srcs)

    def body(*refs):
        src_refs, land_refs = refs[:n], refs[n:2 * n]
        send_ref, recv_ref = refs[2 * n], refs[2 * n + 1]
        for cp in _to_chips_copies(src_refs, land_refs, send_ref, recv_ref, per_peer):
            cp.wait_send()
            cp.wait_recv()

    res = pl.pallas_call(
        body, name=name, in_specs=[HBM] * (2 * n) + [SEM, SEM, ANY],
        out_specs=[HBM] * (2 * n),
        out_shape=[pltpu.HBM(t.shape, t.dtype) for t in srcs] + [pltpu.HBM(t.shape, t.dtype) for t in lands],
        input_output_aliases={i: i for i in range(2 * n)},
        compiler_params=pltpu.CompilerParams(has_side_effects=DATAFLOW),
    )(*srcs, *lands, send_sems, recv_sems, after)
    return list(res[:n]), list(res[n:])


def _row_tile(a):
    return _pick(a, (512, 256, 128, 64, 32, 16, 8))


def _partial_sum(g, land, name):
    _, a, b = g.shape
    tr = _row_tile(a)

    def body(w_ref, g_ref, r_ref, o_ref):
        t = g_ref[0].astype(f32) + r_ref[0].astype(f32)
        t = t + r_ref[1].astype(f32)
        t = t + r_ref[2].astype(f32)
        o_ref[...] = t.astype(o_ref.dtype)

    return pl.pallas_call(
        body, name=name,
        grid_spec=pltpu.PrefetchScalarGridSpec(
            num_scalar_prefetch=1, grid=(a // tr,),
            in_specs=[pl.BlockSpec((1, tr, b), lambda i, w: (w[1], i, 0)), pl.BlockSpec((3, tr, b), lambda i, w: (0, i, 0))],
            out_specs=pl.BlockSpec((tr, b), lambda i, w: (i, 0))),
        out_shape=jax.ShapeDtypeStruct((a, b), bf16),
        compiler_params=_cparams(('parallel',)),
    )(_where(), g, land)


def _by_shard(name, t):
    r, c = t.shape
    if BIG[name] == 2:
        return t.reshape(r, N_SHARDS, c // N_SHARDS).transpose(1, 0, 2)
    return t.reshape(N_SHARDS, r // N_SHARDS, c)


def _from_shards(name, g):
    s, a, b = g.shape
    if BIG[name] == 2:
        return g.transpose(1, 0, 2).reshape(a, s * b)
    return g.reshape(s * a, b)


def _adam_math(w, g, m, v):
    m = ADAM_B1 * m + (1.0 - ADAM_B1) * g
    v = ADAM_B2 * v + (1.0 - ADAM_B2) * (g * g)
    m_hat = m / (1.0 - ADAM_B1 ** ADAM_STEP)
    v_hat = v / (1.0 - ADAM_B2 ** ADAM_STEP)
    delta = -ADAM_LR * (m_hat / (jnp.sqrt(v_hat) + ADAM_EPS) + ADAM_WD * w)
    return delta, m, v


def _small_update(own, sib, chips, w, m, v):
    def body(o_ref, s_ref, c_ref, w_ref, m_ref, v_ref, g_out, d_out, m_out, v_out):
        chip = o_ref[...] + s_ref[...]
        g = (chip + c_ref[0]) + (c_ref[1] + c_ref[2])
        d, mn, vn = _adam_math(w_ref[...], g, m_ref[...], v_ref[...])
        g_out[...] = g
        d_out[...] = d
        m_out[...] = mn
        v_out[...] = vn

    return pl.pallas_call(body, name='small_update', out_shape=[jax.ShapeDtypeStruct(own.shape, f32)] * 4)(
        own, sib, chips, w, m, v)


def _small_chip_sum(own, sib):
    def body(o_ref, s_ref, out):
        out[...] = o_ref[...] + s_ref[...]
    return pl.pallas_call(body, name='small_chip_sum', out_shape=jax.ShapeDtypeStruct(own.shape, f32))(own, sib)


def _adamw(w, m, v, mine, theirs, name):
    layers, a, b = w.shape
    tr = _row_tile(a)

    def body(w_ref, m_ref, v_ref, p0, p1, q0, q1, g_out, d_out, m_out, v_out):
        first = pl.program_id(0) == 0
        g = jnp.where(first, p0[...].astype(f32) + q0[...].astype(f32), p1[...].astype(f32) + q1[...].astype(f32))
        d, mn, vn = _adam_math(w_ref[0], g, m_ref[0], v_ref[0])
        g_out[0] = g
        d_out[0] = d
        m_out[0] = mn
        v_out[0] = vn

    full = pl.BlockSpec((1, tr, b), lambda l, i: (l, i, 0))
    part = pl.BlockSpec((tr, b), lambda l, i: (i, 0))
    return pl.pallas_call(body, name=name, grid=(layers, a // tr), in_specs=[full] * 3 + [part] * 4, out_specs=[full] * 4,
                          out_shape=[jax.ShapeDtypeStruct(w.shape, f32)] * 4,
                          compiler_params=_cparams(('parallel', 'parallel')))(w, m, v, *mine, *theirs)


def _adamw_layer_in_the_middle(w, m, v, mine, theirs, name):
    a, layers, b = w.shape
    assert layers == 2 and b % LANES == 0

    def body(w_ref, m_ref, v_ref, p0, p1, q0, q1, g_out, d_out, m_out, v_out):
        g = jnp.stack([p0[...].astype(f32) + q0[...].astype(f32), p1[...].astype(f32) + q1[...].astype(f32)], axis=1)
        d, mn, vn = _adam_math(w_ref[...], g, m_ref[...], v_ref[...])
        g_out[...] = g
        d_out[...] = d
        m_out[...] = mn
        v_out[...] = vn

    full = pl.BlockSpec((a, layers, LANES), lambda i: (0, 0, i))
    part = pl.BlockSpec((a, LANES), lambda i: (0, i))
    return pl.pallas_call(body, name=name, grid=(b // LANES,), in_specs=[full] * 3 + [part] * 4, out_specs=[full] * 4,
                          out_shape=[jax.ShapeDtypeStruct(w.shape, f32)] * 4,
                          compiler_params=_cparams(('parallel',), big=True))(w, m, v, *mine, *theirs)


def _dg(a, b, ca, cb):
    return lax.dot_general(a.astype(bf16), b.astype(bf16), (((ca,), (cb,)), ((), ())), preferred_element_type=f32)


@jax.custom_vjp
def _bmm(a, b):
    return _dg(a, b, 1, 0)


_bmm.defvjp(lambda a, b: (_dg(a, b, 1, 0), (a, b)), lambda r, g: (_dg(g, r[1], 1, 1), _dg(r[0], g, 0, 0)))


@jax.custom_vjp
def _bmm_nt(a, b):
    return _dg(a, b, 1, 1)


_bmm_nt.defvjp(lambda a, b: (_dg(a, b, 1, 1), (a, b)), lambda r, g: (_dg(g, r[1], 1, 0), _dg(g, r[0], 0, 0)))


@jax.custom_vjp
def _bmm_tn(a, b):
    return _dg(a, b, 0, 0)


_bmm_tn.defvjp(lambda a, b: (_dg(a, b, 0, 0), (a, b)), lambda r, g: (_dg(r[1], g, 1, 1), _dg(r[0], g, 1, 0)))


def _hdot(a, b):
    return jnp.dot(a, b, precision=HI, preferred_element_type=f32)


def _hdot_nt(a, b):
    return lax.dot_general(a, b, (((1,), (1,)), ((), ())), precision=HI, preferred_element_type=f32)


def _hdot_tn(a, b):
    return lax.dot_general(a, b, (((0,), (0,)), ((), ())), precision=HI, preferred_element_type=f32)


def _head_mask(h, width=GROUP_W):
    lane = lax.broadcasted_iota(jnp.int32, (1, width), 1)
    return ((lane >= h * HEAD_DIM) & (lane < (h + 1) * HEAD_DIM)).astype(f32)


def _rope_perm():
    p = np.zeros((LANES, LANES), np.float32)
    half = MLA_ROPE // 2
    for i in range(half):
        p[MLA_NOPE + half + i, MLA_NOPE + i] = -1.0
        p[MLA_NOPE + i, MLA_NOPE + half + i] = 1.0
    return jnp.asarray(p)


def _rope_tables(S):
    half = MLA_ROPE // 2
    freqs = ROPE_THETA ** (-jnp.arange(half, dtype=f32) / half)
    ang = jnp.arange(S, dtype=f32)[:, None] * freqs[None, :]
    cos, sin = jnp.cos(ang), jnp.sin(ang)
    ones, zeros = jnp.ones((S, MLA_NOPE), f32), jnp.zeros((S, LANES - MLA_DQK), f32)
    c_tab = jnp.concatenate([ones, cos, cos, zeros], axis=1)
    s_tab = jnp.concatenate([jnp.zeros((S, MLA_NOPE), f32), sin, sin, zeros], axis=1)
    return c_tab, s_tab


def _f_mla_pre(c_q, c_kv, krope, c_tab, s_tab, q_norm, kv_norm, wq0, wq1, wq2, wq3, wk0, wk1, wk2, wk3, wv, gq, gk, perm):
    wq, wk = (wq0, wq1, wq2, wq3), (wk0, wk1, wk2, wk3)
    nq = _rms(c_q, q_norm)
    nkv = _rms(c_kv, kv_norm)

    def norm_rope(t, g):
        t = t * lax.rsqrt(jnp.sum(t * t, axis=-1, keepdims=True) * (1.0 / MLA_DQK) + EPS) * g
        return t * c_tab + _hdot(t, perm) * s_tab

    qs = [norm_rope(_bmm(nq, wq[h]), gq) * (MLA_DQK ** -0.5) for h in range(N_HEADS)]
    ks = [norm_rope(_bmm(nkv, wk[h]) + krope, gk) for h in range(N_HEADS)]
    return (*qs, *ks, _bmm(nkv, wv))


def _f_attn(qs, ks, v, q0):
    tq, S = qs[0].shape[0], ks[0].shape[0]
    qpos = q0 + lax.broadcasted_iota(jnp.int32, (tq, S), 0)
    kpos = lax.broadcasted_iota(jnp.int32, (tq, S), 1)
    keep = kpos <= qpos
    logits = [jnp.where(keep, _bmm_nt(qs[h], ks[h]), NEG_INF) for h in range(N_HEADS)]
    ps = [jnp.exp(lg - jnp.max(lg, axis=-1, keepdims=True)) for lg in logits]
    ps = [p / jnp.sum(p, axis=-1, keepdims=True) for p in ps]
    return sum(_bmm(p, v) * _head_mask(h) for h, p in enumerate(ps))


ATTN_PARTS = 4


def _mla_attn_fwd(qs, ks, v, name):
    S = v.shape[0]
    tq = Q_BLOCK
    parts = ATTN_PARTS if S % (ATTN_PARTS * tq) == 0 else 1
    per = S // parts
    outs = []
    for p in range(parts):
        n_keys = (p + 1) * per
        first_block = p * (per // tq)

        def body(*refs, first_block=first_block):
            q_vals = [r[...] for r in refs[:4]]
            k_vals = [r[...] for r in refs[4:8]]
            refs[9][...] = _f_attn(q_vals, k_vals, refs[8][...], (first_block + pl.program_id(0)) * tq)

        qspec = pl.BlockSpec((tq, LANES), lambda i, fb=first_block: (fb + i, 0))
        outs.append(pl.pallas_call(
            body, name=f'{name}_{p}', grid=(per // tq,),
            in_specs=[qspec] * 4 + [pl.BlockSpec((n_keys, LANES), lambda i: (0, 0))] * 4 +
                     [pl.BlockSpec((n_keys, GROUP_W), lambda i: (0, 0))],
            out_specs=pl.BlockSpec((tq, GROUP_W), lambda i: (i, 0)),
            out_shape=jax.ShapeDtypeStruct((per, GROUP_W), f32),
            compiler_params=_cparams(('parallel',), big=True),
        )(*qs, *ks, v))
    return jnp.concatenate(outs, axis=0)


def _mla_attn_bwd(qs, ks, v, do, name):
    S = v.shape[0]
    tq = Q_BLOCK
    parts = ATTN_PARTS if S % (ATTN_PARTS * tq) == 0 else 1
    per = S // parts
    dq_parts, dkv_sum = [], None
    for p in range(parts):
        n_keys = (p + 1) * per
        first_block = p * (per // tq)

        def body(*refs, first_block=first_block):
            q_vals = [r[...].astype(f32) for r in refs[:4]]
            k_vals = [r[...].astype(f32) for r in refs[4:8]]
            v_val = refs[8][...].astype(f32)
            q0 = (first_block + pl.program_id(0)) * tq
            _, vjp = jax.vjp(lambda a, b, c: _f_attn(a, b, c, q0), q_vals, k_vals, v_val)
            dqs, dks, dv = vjp(refs[9][...])
            outs = refs[10:]
            for h in range(N_HEADS):
                outs[h][...] = dqs[h]
            first = pl.program_id(0) == 0
            for o, g in zip(outs[4:], (*dks, dv)):
                @pl.when(first)
                def _(o=o, g=g):
                    o[...] = g

                @pl.when(jnp.logical_not(first))
                def _(o=o, g=g):
                    o[...] += g

        qspec = pl.BlockSpec((tq, LANES), lambda i, fb=first_block: (fb + i, 0))
        kspec = pl.BlockSpec((n_keys, LANES), lambda i: (0, 0))
        vspec = pl.BlockSpec((n_keys, GROUP_W), lambda i: (0, 0))
        res = pl.pallas_call(
            body, name=f'{name}_{p}', grid=(per // tq,),
            in_specs=[qspec] * 4 + [kspec] * 4 + [vspec, pl.BlockSpec((tq, GROUP_W), lambda i, fb=first_block: (fb + i, 0))],
            out_specs=[pl.BlockSpec((tq, LANES), lambda i: (i, 0))] * 4 + [kspec] * 4 + [vspec],
            out_shape=[jax.ShapeDtypeStruct((per, LANES), f32)] * 4 + [jax.ShapeDtypeStruct((n_keys, LANES), f32)] * 4 +
                      [jax.ShapeDtypeStruct((n_keys, GROUP_W), f32)],
            compiler_params=_cparams(('arbitrary',), big=True),
        )(*qs, *ks, v, do)
        dq_parts.append(res[:4])
        dkv = [jnp.pad(t, ((0, S - n_keys), (0, 0))) for t in res[4:]]
        dkv_sum = dkv if dkv_sum is None else [a_ + b_ for a_, b_ in zip(dkv_sum, dkv)]
    dqs = [jnp.concatenate([dq_parts[p][h] for p in range(parts)], axis=0) for h in range(N_HEADS)]
    return dqs, dkv_sum[:4], dkv_sum[4]


def _mla_params(mp):
    pad = LANES - MLA_DQK
    wq = jnp.pad(mp['mla_w_uq'].reshape(GROUP_W, N_HEADS, MLA_DQK).transpose(1, 0, 2), ((0, 0), (0, 0), (0, pad)))
    wkv = mp['mla_w_ukv'].reshape(LANES, N_HEADS, MLA_NOPE + HEAD_DIM)
    wk = jnp.pad(wkv[:, :, :MLA_NOPE].transpose(1, 0, 2), ((0, 0), (0, 0), (0, LANES - MLA_NOPE)))
    wv = wkv[:, :, MLA_NOPE:].reshape(LANES, GROUP_W)
    gq = jnp.pad(mp['mla_qk_q'], (0, pad))[None]
    gk = jnp.pad(mp['mla_qk_k'], (0, pad))[None]
    return [mp['mla_q_norm'][None], mp['mla_kv_norm'][None], *[wq[h] for h in range(N_HEADS)],
            *[wk[h] for h in range(N_HEADS)], wv, gq, gk, _rope_perm()]


def _mla_fwd(c_q, c_kv, k_rope, mp, l):
    S = c_q.shape[0]
    tm = _pick(S, (256, 128))
    krope = jnp.pad(k_rope, ((0, 0), (MLA_NOPE, LANES - MLA_DQK)))
    c_tab, s_tab = _rope_tables(S)
    tiled = [c_q, c_kv, krope, c_tab, s_tab]
    params = _mla_params(mp)
    res = _tile_fwd(_f_mla_pre, tiled, params, [(LANES, bf16)] * 8 + [(GROUP_W, bf16)], tm, f'mla_pre_fwd_{l}')
    qs, ks, v = res[:4], res[4:8], res[8]
    y = _mla_attn_fwd(qs, ks, v, f'mla_attn_fwd_{l}')
    return y, (tiled, params, qs, ks, v)


def _mla_bwd(dy, saved, l):
    tiled, params, qs, ks, v = saved
    S = dy.shape[0]
    tm = _pick(S, (256, 128))
    dqs, dks, dv = _mla_attn_bwd(qs, ks, v, dy, f'mla_attn_bwd_{l}')
    (dc_q, dc_kv, dkrope), dpar = _tile_bwd(_f_mla_pre, tiled, params, [*dqs, *dks, dv], [True, True, True, False, False],
                                            [True] * 13 + [False], tm, f'mla_pre_bwd_{l}')
    dqn, dkvn = dpar[0], dpar[1]
    dwq, dwk = jnp.stack(dpar[2:6]), jnp.stack(dpar[6:10])
    dwv, dgq, dgk = dpar[10:13]
    dw_uq = dwq[:, :, :MLA_DQK].transpose(1, 0, 2).reshape(GROUP_W, N_HEADS * MLA_DQK)
    dw_ukv = jnp.concatenate([dwk[:, :, :MLA_NOPE].transpose(1, 0, 2), dwv.reshape(LANES, N_HEADS, HEAD_DIM)],
                             axis=2).reshape(LANES, N_HEADS * (MLA_NOPE + HEAD_DIM))
    grads = {'mla_q_norm': dqn[0], 'mla_kv_norm': dkvn[0], 'mla_w_uq': dw_uq, 'mla_w_ukv': dw_ukv,
             'mla_qk_q': dgq[0, :MLA_DQK], 'mla_qk_k': dgk[0, :MLA_DQK]}
    return dc_q, dc_kv, dkrope[:, MLA_NOPE:MLA_DQK], grads


SPAN = 128


def _head_mean_matrix():
    h = np.arange(GROUP_W) // HEAD_DIM
    return jnp.asarray((h[:, None] == h[None, :]).astype(np.float32) / HEAD_DIM)


def _f_dil_pre(q, k, gq, gk, hm):
    qn = q * lax.rsqrt(_hdot(q * q, hm) + EPS) * gq * (HEAD_DIM ** -0.5)
    kn = k * lax.rsqrt(_hdot(k * k, hm) + EPS) * gk
    return qn, kn


def _f_dil_branch(qb, kp, kc, vp, vc, b0, b1, b2, b3, first):
    kcat = jnp.concatenate([kp, kc], axis=0)
    vcat = jnp.concatenate([vp, vc], axis=0)
    qi = lax.broadcasted_iota(jnp.int32, (SPAN, 2 * SPAN), 0) + SPAN
    kj = lax.broadcasted_iota(jnp.int32, (SPAN, 2 * SPAN), 1)
    delta = qi - kj
    valid = (delta >= 0) & (delta <= SPAN) & jnp.logical_not(first & (kj < SPAN))
    masks = [_head_mask(h) for h in range(N_HEADS)]
    raw = [_bmm_nt(qb * hm, kcat) for hm in masks]
    logits = [jnp.where(valid, r + bias, NEG_INF) for r, bias in zip(raw, (b0, b1, b2, b3))]
    ms = [jnp.max(lg, axis=-1, keepdims=True) for lg in logits]
    ps = [jnp.exp(lg - m) for lg, m in zip(logits, ms)]
    pvs = [_bmm(p, vcat) for p in ps]
    o = sum(pv * hm for pv, hm in zip(pvs, masks))
    m_full = sum(m * hm for m, hm in zip(ms, masks))
    l_full = sum(jnp.sum(p, axis=-1, keepdims=True) * hm for p, hm in zip(ps, masks))
    return o, m_full, l_full


def _dil_branch_specs(d, nb):
    cur = pl.BlockSpec((SPAN, GROUP_W), lambda r, n: (n, r))
    prev = pl.BlockSpec((SPAN, GROUP_W), lambda r, n: (jnp.maximum(n - 1, 0), r))
    bias = pl.BlockSpec((1, SPAN, 2 * SPAN), lambda r, n: (0, 0, 0))
    return cur, prev, bias


def _head_table_specs():
    return [pl.BlockSpec((1, SPAN, 2 * SPAN), lambda r, n, h=h: (h, 0, 0)) for h in range(N_HEADS)]


def _dil_branch_fwd(q, k, v, table, name):
    L, d = q.shape[0], q.shape[1] // GROUP_W
    nb = L // SPAN
    cur, prev, bias = _dil_branch_specs(d, nb)

    def body(q_ref, kp_ref, kc_ref, vp_ref, vc_ref, b0, b1, b2, b3, o_ref, m_ref, l_ref):
        o, m, l = _f_dil_branch(*[r[...].astype(f32) for r in (q_ref, kp_ref, kc_ref, vp_ref, vc_ref)], b0[0], b1[0], b2[0], b3[0],
                                pl.program_id(1) == 0)
        o_ref[...] = o
        m_ref[...] = m
        l_ref[...] = l

    return pl.pallas_call(
        body, name=name, grid=(d, nb), in_specs=[cur, prev, cur, prev, cur] + _head_table_specs(),
        out_specs=[cur] * 3, out_shape=[jax.ShapeDtypeStruct(q.shape, f32)] * 3,
        compiler_params=_cparams(('parallel', 'parallel')),
    )(q, k, k, v, v, *[table] * N_HEADS)


def _dil_branch_bwd(q, k, v, table, do, dm, dl, name):
    L, d = q.shape[0], q.shape[1] // GROUP_W
    nb = L // SPAN
    cur, prev, bias = _dil_branch_specs(d, nb)
    whole = pl.BlockSpec((L, GROUP_W), lambda r, n: (0, r))

    def body(q_ref, kp_ref, kc_ref, vp_ref, vc_ref, b0, b1, b2, b3, do_ref, dm_ref, dl_ref,
             dq_ref, dk_ref, dv_ref, db0, db1, db2, db3):
        r, n = pl.program_id(0), pl.program_id(1)
        first = n == 0
        _, vjp = jax.vjp(lambda *a: _f_dil_branch(*a, first), *[r[...].astype(f32) for r in (q_ref, kp_ref, kc_ref, vp_ref, vc_ref)],
                         b0[0], b1[0], b2[0], b3[0])
        dq, dkp, dkc, dvp, dvc, g0, g1, g2, g3 = vjp((do_ref[...], dm_ref[...], dl_ref[...]))
        dq_ref[...] = dq

        @pl.when(first)
        def _():
            dk_ref[...] = jnp.zeros_like(dk_ref)
            dv_ref[...] = jnp.zeros_like(dv_ref)

        rows = pl.ds(pl.multiple_of(n * SPAN, SPAN), SPAN)
        dk_ref[rows, :] += dkc
        dv_ref[rows, :] += dvc

        @pl.when(n > 0)
        def _():
            before = pl.ds(pl.multiple_of((n - 1) * SPAN, SPAN), SPAN)
            dk_ref[before, :] += dkp
            dv_ref[before, :] += dvp

        start = first & (r == 0)
        for o, g in zip((db0, db1, db2, db3), (g0, g1, g2, g3)):
            @pl.when(start)
            def _(o=o, g=g):
                o[0] = g

            @pl.when(jnp.logical_not(start))
            def _(o=o, g=g):
                o[0] += g

    res = pl.pallas_call(
        body, name=name, grid=(d, nb), in_specs=[cur, prev, cur, prev, cur] + _head_table_specs() + [cur] * 3,
        out_specs=[cur, whole, whole] + [bias] * 4,
        out_shape=[jax.ShapeDtypeStruct(q.shape, f32)] * 3 + [jax.ShapeDtypeStruct((1, SPAN, 2 * SPAN), f32)] * 4,
        compiler_params=_cparams(('arbitrary', 'arbitrary')),
    )(q, k, k, v, v, *[table] * N_HEADS, do, dm, dl)
    return res[0], res[1], res[2], res[3:]


def _f_dil_merge(o1, m1, l1, o2, m2, l2, o3, m3, l3):
    mx = jnp.maximum(jnp.maximum(m1, m2), m3)
    w1, w2, w3 = jnp.exp(m1 - mx), jnp.exp(m2 - mx), jnp.exp(m3 - mx)
    return ((w1 * o1 + w2 * o2 + w3 * o3) / (w1 * l1 + w2 * l2 + w3 * l3),)


def _bias_onehot(dilation):
    qi = jnp.arange(SPAN, dtype=jnp.int32)[:, None] + SPAN
    kj = jnp.arange(2 * SPAN, dtype=jnp.int32)[None, :]
    bucket = _t5_bucket(jnp.clip(qi - kj, 0, SPAN) * dilation).reshape(-1)
    return (bucket[None, :] == jnp.arange(T5_BUCKETS, dtype=jnp.int32)[:, None]).astype(f32)


def _bias_tables(t5_t, onehot, name):
    N = onehot.shape[1]
    tn = _pick(N, (4096, 2048, 1024))

    def body(t_ref, oh_ref, o_ref):
        o_ref[...] = _hdot(t_ref[...], oh_ref[...])

    return pl.pallas_call(
        body, name=name, grid=(N // tn,),
        in_specs=[pl.BlockSpec((8, T5_BUCKETS), lambda i: (0, 0)), pl.BlockSpec((T5_BUCKETS, tn), lambda i: (0, i))],
        out_specs=pl.BlockSpec((8, tn), lambda i: (0, i)), out_shape=jax.ShapeDtypeStruct((8, N), f32),
        compiler_params=_cparams(('parallel',)),
    )(t5_t, onehot)


def _bias_tables_bwd(d_tab, onehot, name):
    N = onehot.shape[1]
    tn = _pick(N, (4096, 2048, 1024))

    def body(g_ref, oh_ref, o_ref):
        part = _hdot_nt(g_ref[...], oh_ref[...])

        @pl.when(pl.program_id(0) == 0)
        def _():
            o_ref[...] = part

        @pl.when(pl.program_id(0) > 0)
        def _():
            o_ref[...] += part

    return pl.pallas_call(
        body, name=name, grid=(N // tn,),
        in_specs=[pl.BlockSpec((8, tn), lambda i: (0, i)), pl.BlockSpec((T5_BUCKETS, tn), lambda i: (0, i))],
        out_specs=pl.BlockSpec((8, T5_BUCKETS), lambda i: (0, 0)), out_shape=jax.ShapeDtypeStruct((8, T5_BUCKETS), f32),
        compiler_params=_cparams(('arbitrary',)),
    )(d_tab, onehot)


def _by_residue(t, d):
    S, C = t.shape
    return t.reshape(S // d, d * C)


def _from_residue(t):
    return t.reshape(-1, GROUP_W)


def _dil_tables(t5_bias):
    t5_t = jnp.pad(t5_bias.T, ((0, 8 - N_HEADS), (0, 0)))
    return [_bias_tables(t5_t, _bias_onehot(d), f'dil_bias_fwd_{bi}').reshape(8, SPAN, 2 * SPAN)
            for bi, (_, d) in enumerate(DIL_PAIRS)]


def _t5_grad(d_tables):
    total = None
    for bi, (_, d) in enumerate(DIL_PAIRS):
        g = _bias_tables_bwd(d_tables[bi], _bias_onehot(d), f'dil_bias_bwd_{bi}')
        total = g if total is None else total + g
    return total[:N_HEADS].T


def _dil_fwd(qkv, mp, l):
    S = qkv.shape[0]
    tm = _pick(S, (256, 128))
    q, k, v = qkv[:, :GROUP_W], qkv[:, GROUP_W:2 * GROUP_W], qkv[:, 2 * GROUP_W:]
    pre_params = [jnp.tile(mp['dil_q_norm'], N_HEADS)[None], jnp.tile(mp['dil_k_norm'], N_HEADS)[None], _head_mean_matrix()]
    qn, kn = _tile_fwd(_f_dil_pre, [q, k], pre_params, [(GROUP_W, bf16)] * 2, tm, f'dil_pre_fwd_{l}')
    v = v.astype(bf16)
    tables = mp['dil_tables'] if 'dil_tables' in mp else _dil_tables(mp['t5_bias'])
    branches, outs = [], []
    for bi, (_, d) in enumerate(DIL_PAIRS):
        tab = tables[bi]
        qd, kd, vd = _by_residue(qn, d), _by_residue(kn, d), _by_residue(v, d)
        o, m, lsum = _dil_branch_fwd(qd, kd, vd, tab, f'dil_branch_fwd_{l}_{bi}')
        branches.append((qd, kd, vd, tab))
        outs += [_from_residue(o), _from_residue(m), _from_residue(lsum)]
    (y,) = _tile_fwd(_f_dil_merge, outs, [], [(GROUP_W, f32)], tm, f'dil_merge_fwd_{l}')
    return y, (q, k, pre_params, branches, outs)


def _dil_bwd(dy, saved, l):
    q, k, pre_params, branches, outs = saved
    S = dy.shape[0]
    tm = _pick(S, (256, 128))
    douts, _ = _tile_bwd(_f_dil_merge, outs, [], [dy], [True] * 9, [], tm, f'dil_merge_bwd_{l}')
    dqn = dkn = dv = None
    d_tabs = []
    for bi, (_, d) in enumerate(DIL_PAIRS):
        qd, kd, vd, tab = branches[bi]
        do, dm, dl = [_by_residue(t, d) for t in douts[3 * bi:3 * bi + 3]]
        dq_b, dk_b, dv_b, dbias = _dil_branch_bwd(qd, kd, vd, tab, do, dm, dl, f'dil_branch_bwd_{l}_{bi}')
        d_tabs.append(jnp.concatenate([*dbias, jnp.zeros((8 - N_HEADS, SPAN, 2 * SPAN), f32)], axis=0).reshape(8, -1))
        dq_b, dk_b, dv_b = _from_residue(dq_b), _from_residue(dk_b), _from_residue(dv_b)
        dqn = dq_b if dqn is None else dqn + dq_b
        dkn = dk_b if dkn is None else dkn + dk_b
        dv = dv_b if dv is None else dv + dv_b
    (dq, dk), (dgq, dgk) = _tile_bwd(_f_dil_pre, [q, k], pre_params, [dqn, dkn], [True, True], [True, True, False], tm,
                                     f'dil_pre_bwd_{l}')
    grads = {'dil_q_norm': dgq.reshape(N_HEADS, HEAD_DIM).sum(0), 'dil_k_norm': dgk.reshape(N_HEADS, HEAD_DIM).sum(0),
             't5_tables': d_tabs}
    return jnp.concatenate([dq, dk, dv], axis=1), grads


S5_LANES = S5_G * S5_P
SCAN_SEGMENTS = 8
SCAN_W = 256


def _f_s5_prep(bre, bim, lr, li, logdt_col, expand):
    dt = jnp.sum(jnp.exp(logdt_col) * expand, axis=0, keepdims=True)
    mag = jnp.exp(lr * dt)
    ar, ai = mag * jnp.cos(li * dt), mag * jnp.sin(li * dt)
    den = lr * lr + li * li
    nr, ni = ar - 1.0, ai
    zr = (nr * lr + ni * li) / den
    zi = (ni * lr - nr * li) / den
    bb = jnp.concatenate([zr * bre - zi * bim, zr * bim + zi * bre], axis=1)
    a_rows = jnp.broadcast_to(jnp.concatenate([ar, ai], axis=1), bb.shape)
    return bb, a_rows


def _s5_scan(x, a_rows, name, reverse=False, h=None):
    S = x.shape[0]
    NL = x.shape[1] // 2
    T = S // SCAN_SEGMENTS
    nblk = NL // SCAN_W
    n_in = 4 if reverse else 2

    def body(*refs):
        if reverse:
            (x_hbm, pr_hbm, pi_hbm, ar_ref, ai_ref, hr_hbm, hi_hbm, dar_ref, dai_ref,
             xr_s, xi_s, pr_s, pi_s, hr_s, hi_s, in_sems, out_sems) = refs
        else:
            x_hbm, ar_ref, ai_ref, hr_hbm, hi_hbm, xr_s, xi_s, hr_s, hi_s, in_sems, out_sems = refs
        col = pl.multiple_of(pl.program_id(0) * SCAN_W, SCAN_W)
        loads = []
        for k in range(SCAN_SEGMENTS):
            rows = pl.ds(k * T, T)
            sources = [(x_hbm, col, xr_s), (x_hbm, NL + col, xi_s)]
            if reverse:
                sources += [(pr_hbm, col, pr_s), (pi_hbm, col, pi_s)]
            for i, (src, c0, dst) in enumerate(sources):
                loads.append(pltpu.make_async_copy(src.at[rows, pl.ds(c0, SCAN_W)], dst.at[:, k, :],
                                                   in_sems.at[i * SCAN_SEGMENTS + k]))
        for cp in loads:
            cp.start()
        for cp in loads:
            cp.wait()
        ar = ar_ref[...]
        ai = -ai_ref[...] if reverse else ai_ref[...]
        zero = jnp.zeros((SCAN_SEGMENTS, SCAN_W), f32)

        def at(s):
            return T - 1 - s if reverse else s

        def local(s, c):
            hr, hi, pr, pi = c
            j = at(s)
            nhr = ar * hr - ai * hi + xr_s[j]
            nhi = ar * hi + ai * hr + xi_s[j]
            hr_s[j] = nhr
            hi_s[j] = nhi
            return nhr, nhi, ar * pr - ai * pi, ar * pi + ai * pr

        er, ei, pr, pi = lax.fori_loop(0, T, local, (zero, zero, zero + 1.0, zero), unroll=2)
        row = lax.broadcasted_iota(jnp.int32, (SCAN_SEGMENTS, SCAN_W), 0)
        cr, ci = zero, zero
        order = range(SCAN_SEGMENTS - 2, -1, -1) if reverse else range(1, SCAN_SEGMENTS)
        for k in order:
            src = k + 1 if reverse else k - 1
            tr = er + pr * cr - pi * ci
            ti = ei + pr * ci + pi * cr
            cr = jnp.where(row == k, jnp.sum(jnp.where(row == src, tr, 0.0), axis=0, keepdims=True), cr)
            ci = jnp.where(row == k, jnp.sum(jnp.where(row == src, ti, 0.0), axis=0, keepdims=True), ci)

        def fix_at(j, c, before):
            pr, pi, sr, si = c
            pr, pi = ar * pr - ai * pi, ar * pi + ai * pr
            hr = hr_s[j] + pr * cr - pi * ci
            hi = hi_s[j] + pr * ci + pi * cr
            hr_s[j] = hr
            hi_s[j] = hi
            if reverse:
                qr, qi = before
                sr = sr + hr * qr + hi * qi
                si = si + hi * qr - hr * qi
            return pr, pi, sr, si

        start = (zero + 1.0, zero, zero, zero)
        if reverse:
            def fix(s, c):
                j = T - 1 - s
                return fix_at(j, c, (pr_s[j - 1], pi_s[j - 1]))

            c = lax.fori_loop(0, T - 1, fix, start, unroll=2)
            last_r = jnp.where(row == 0, 0.0, pltpu.roll(pr_s[T - 1], 1, 0))
            last_i = jnp.where(row == 0, 0.0, pltpu.roll(pi_s[T - 1], 1, 0))
            _, _, sr, si = fix_at(0, c, (last_r, last_i))
            dar_ref[...] = sr
            dai_ref[...] = si
        else:
            lax.fori_loop(0, T, lambda s, c: fix_at(s, c, None), start, unroll=2)
        stores = []
        for k in range(SCAN_SEGMENTS):
            rows = pl.ds(k * T, T)
            stores.append(pltpu.make_async_copy(hr_s.at[:, k, :], hr_hbm.at[rows, pl.ds(col, SCAN_W)], out_sems.at[k]))
            stores.append(pltpu.make_async_copy(hi_s.at[:, k, :], hi_hbm.at[rows, pl.ds(col, SCAN_W)],
                                                out_sems.at[SCAN_SEGMENTS + k]))
        for cp in stores:
            cp.start()
        for cp in stores:
            cp.wait()

    a_re = pl.BlockSpec((SCAN_SEGMENTS, SCAN_W), lambda b: (0, b))
    a_im = pl.BlockSpec((SCAN_SEGMENTS, SCAN_W), lambda b: (0, nblk + b))
    seq = pltpu.VMEM((T, SCAN_SEGMENTS, SCAN_W), f32)
    if reverse:
        in_specs, args = [ANY, ANY, ANY, a_re, a_im], [x, h[0], h[1], a_rows, a_rows]
        out_specs = [ANY, ANY, a_re, a_re]
        out_shape = [jax.ShapeDtypeStruct((S, NL), f32)] * 2 + [jax.ShapeDtypeStruct((SCAN_SEGMENTS, NL), f32)] * 2
    else:
        in_specs, args = [ANY, a_re, a_im], [x, a_rows, a_rows]
        out_specs = [ANY, ANY]
        out_shape = [jax.ShapeDtypeStruct((S, NL), f32)] * 2
    scratch = [seq] * (n_in + 2) + [pltpu.SemaphoreType.DMA((n_in * SCAN_SEGMENTS,)),
                                    pltpu.SemaphoreType.DMA((2 * SCAN_SEGMENTS,))]
    return pl.pallas_call(body, name=name, grid=(nblk,), in_specs=in_specs, out_specs=out_specs, out_shape=out_shape,
                          scratch_shapes=scratch, compiler_params=_cparams(('arbitrary',), big=True))(*args)


def _f_s5_post(y, u, d, w_glu):
    z = _bmm(y + d * u, w_glu)
    return (z[:, :GROUP_W] * jax.nn.sigmoid(z[:, GROUP_W:]),)


def _block_diag(t):
    G, a, b = t.shape
    eye = jnp.eye(G, dtype=t.dtype)
    return (t[:, :, None, :] * eye[:, None, :, None]).reshape(G * a, G * b)


def _diag_blocks(m, a, b):
    G = m.shape[0] // a
    return jnp.moveaxis(jnp.diagonal(m.reshape(G, a, G, b), axis1=0, axis2=2), -1, 0)


def _s5_fwd(u, mp, l):
    S = u.shape[0]
    tm = _pick(S, (256, 128))
    bre = _block_diag(mp['s5_b_re'].transpose(0, 2, 1))
    bim = _block_diag(mp['s5_b_im'].transpose(0, 2, 1))
    expand = jnp.repeat(jnp.eye(S5_G, dtype=f32), S5_P, axis=1)
    prep_params = [mp['s5_lambda_re'].reshape(1, S5_LANES), mp['s5_lambda_im'].reshape(1, S5_LANES),
                   mp['s5_log_dt'].reshape(S5_G, 1), expand]
    bb, a_rows = _tile_fwd(_f_s5_prep, [bre, bim], prep_params, [(2 * S5_LANES, f32)] * 2, GROUP_W, f's5_prep_fwd_{l}')
    x = _mm(u, bb, 'nn', f's5_in_fwd_{l}')
    hr, hi = _s5_scan(x, a_rows, f's5_scan_fwd_{l}')
    c_re, c_im = _block_diag(mp['s5_c_re'].transpose(0, 2, 1)), -_block_diag(mp['s5_c_im'].transpose(0, 2, 1))
    y = _mm(hi, c_im, 'nn', f's5_out_im_fwd_{l}', add=_mm(hr, c_re, 'nn', f's5_out_re_fwd_{l}'))
    post_params = [mp['s5_d'][None], mp['s5_w_glu']]
    (out,) = _tile_fwd(_f_s5_post, [y, u], post_params, [(GROUP_W, f32)], tm, f's5_post_fwd_{l}')
    return out, (u, bre, bim, prep_params, bb, a_rows, hr, hi, c_re, c_im, y, post_params)


def _s5_bwd(dout, saved, l):
    u, bre, bim, prep_params, bb, a_rows, hr, hi, c_re, c_im, y, post_params = saved
    S = u.shape[0]
    tm = _pick(S, (256, 128))
    (dy, du1), (dd, dwglu) = _tile_bwd(_f_s5_post, [y, u], post_params, [dout], [True, True], [True, True], tm,
                                       f's5_post_bwd_{l}')
    ccat = jnp.concatenate([c_re, c_im], axis=0)
    dh = _mm(dy, ccat, 'nt', f's5_out_dx_{l}')
    dccat = jnp.concatenate([_mm(hr, dy, 'tn', f's5_out_re_dw_{l}'), _mm(hi, dy, 'tn', f's5_out_im_dw_{l}')], axis=0)
    lr_, li_, dar, dai = _s5_scan(dh, a_rows, f's5_scan_bwd_{l}', reverse=True, h=(hr, hi))
    du2 = _mm(li_, bb[:, S5_LANES:], 'nt', f's5_in_im_dx_{l}', add=_mm(lr_, bb[:, :S5_LANES], 'nt', f's5_in_re_dx_{l}'))
    dbb = jnp.concatenate([_mm(u, lr_, 'tn', f's5_in_re_dw_{l}'), _mm(u, li_, 'tn', f's5_in_im_dw_{l}')], axis=1)
    da_rows = jnp.pad(jnp.concatenate([dar, dai], axis=1), ((0, GROUP_W - SCAN_SEGMENTS), (0, 0)))
    (dbre, dbim), (dlr, dli, dlogdt) = _tile_bwd(_f_s5_prep, [bre, bim], prep_params, [dbb, da_rows], [True, True],
                                                 [True, True, True, False], GROUP_W, f's5_prep_bwd_{l}')
    grads = {
        's5_lambda_re': dlr.reshape(S5_G, S5_P), 's5_lambda_im': dli.reshape(S5_G, S5_P), 's5_log_dt': dlogdt[:, 0],
        's5_b_re': _diag_blocks(dbre, S5_CG, S5_P).transpose(0, 2, 1),
        's5_b_im': _diag_blocks(dbim, S5_CG, S5_P).transpose(0, 2, 1),
        's5_c_re': _diag_blocks(dccat[:S5_LANES], S5_P, S5_CG).transpose(0, 2, 1),
        's5_c_im': -_diag_blocks(dccat[S5_LANES:], S5_P, S5_CG).transpose(0, 2, 1),
        's5_d': dd[0], 's5_w_glu': dwglu}
    return du1 + du2, grads


DN_CONV = 4


def _head_sum_matrix():
    h = np.arange(GROUP_W) // HEAD_DIM
    return jnp.asarray((h[:, None] == h[None, :]).astype(np.float32))


def _f_dn_pre(x0, x1, x2, x3, ab, w0, w1, w2, w3, alog, dtb, ea, eb, hs):
    c = w0 * x0 + w1 * x1 + w2 * x2 + w3 * x3
    s = c * jax.nn.sigmoid(c)
    q, k, v = s[:, :GROUP_W], s[:, GROUP_W:2 * GROUP_W], s[:, 2 * GROUP_W:]
    q = q * lax.rsqrt(_hdot(q * q, hs) + EPS) * (HEAD_DIM ** -0.5)
    k = k * lax.rsqrt(_hdot(k * k, hs) + EPS)
    beta = jax.nn.sigmoid(_hdot(ab, eb))
    g = -jnp.exp(alog) * jax.nn.softplus(_hdot(ab, ea) + dtb)
    return q, k, v, g, beta


DN_CHUNKS_PER_STEP = 4


def _f_dn_chunks(q, k, v, g, beta):
    C = DN_CHUNK
    n_chunks = q.shape[0] // C
    r = lax.broadcasted_iota(jnp.int32, (C, C), 0)
    c = lax.broadcasted_iota(jnp.int32, (C, C), 1)
    causal, strict = r >= c, r > c
    eye = (r == c).astype(f32)
    tril = causal.astype(f32)
    ones = jnp.ones((C, GROUP_W), f32)
    masks = [_head_mask(h) for h in range(N_HEADS)]
    rows = [tuple(t[i * C:(i + 1) * C] for t in (q, k, v, g, beta)) for i in range(n_chunks)]
    gcs = [_hdot(tril, gi) for (_, _, _, gi, _) in rows]
    items = [(i, h) for i in range(n_chunks) for h in range(N_HEADS)]
    grows = [_hdot_nt(ones * (masks[h] * (1.0 / HEAD_DIM)), gcs[i]) for i, h in items]
    decs = []
    for (i, h), grow in zip(items, grows):
        gcol = jnp.sum(gcs[i] * masks[h], axis=1, keepdims=True) * (1.0 / HEAD_DIM)
        decs.append(jnp.exp(jnp.where(causal, gcol - grow, NEG_INF)))
    kbs = [ki * bi for (_, ki, _, _, bi) in rows]
    kks = [_bmm_nt(kbs[i] * masks[h], rows[i][1]) for i, h in items]
    qks = [_bmm_nt(rows[i][0] * masks[h], rows[i][1]) for i, h in items]
    lmats = [jnp.where(strict, kk * dec, 0.0) for kk, dec in zip(kks, decs)]
    a_qk = [jnp.where(causal, qk * dec, 0.0) for qk, dec in zip(qks, decs)]
    ts = [eye - lm for lm in lmats]
    ps = lmats
    for _ in range(5):
        ps = [_bmm(p, p) for p in ps]
        ts = [t + _bmm(t, p) for t, p in zip(ts, ps)]
    egs = [jnp.exp(gc) for gc in gcs]
    tw = [_bmm(t, kbs[i] * egs[i]) for (i, h), t in zip(items, ts)]
    tu = [_bmm(t, rows[i][2] * rows[i][4]) for (i, h), t in zip(items, ts)]
    outs = []
    for i in range(n_chunks):
        qi, ki, _, gi, _ = rows[i]
        glast = jnp.sum(gi, axis=0, keepdims=True)
        w = sum(tw[i * N_HEADS + h] * masks[h] for h in range(N_HEADS))
        u = sum(tu[i * N_HEADS + h] * masks[h] for h in range(N_HEADS))
        outs.append((w, u, qi * egs[i], ki * jnp.exp(glast - gcs[i]), *a_qk[i * N_HEADS:(i + 1) * N_HEADS],
                     jnp.broadcast_to(jnp.exp(glast), (C, GROUP_W))))
    return tuple(jnp.concatenate(parts, axis=0) for parts in zip(*outs))


def _f_dn_step(w, u, qd, kdec, a0, a1, a2, a3, dfull, state, bd):
    row0 = (lax.broadcasted_iota(jnp.int32, dfull.shape, 0) == 0).astype(f32)
    dvec = jnp.sum(dfull * row0, axis=0, keepdims=True)
    ws, qs = _bmm(w, state), _bmm(qd, state)
    vnew = u - ws
    avs = [_bmm(a, vnew) for a in (a0, a1, a2, a3)]
    kv = _bmm_tn(kdec, vnew)
    o = qs + sum(av * _head_mask(h) for h, av in enumerate(avs))
    return o, state * dvec + bd * kv


def _dn_scan_fwd(ins, name):
    S = ins[0].shape[0]
    N = S // DN_CHUNK
    bd = _head_sum_matrix()

    def body(*refs):
        o_ref, s_ref, state = refs[10], refs[11], refs[12]

        @pl.when(pl.program_id(0) == 0)
        def _():
            state[...] = jnp.zeros_like(state)

        s_in = state[...]
        s_ref[0] = s_in
        o, s_out = _f_dn_step(*[r[...] for r in refs[:9]], s_in, refs[9][...])
        o_ref[...] = o
        state[...] = s_out

    return pl.pallas_call(
        body, name=name, grid=(N,),
        in_specs=[pl.BlockSpec((DN_CHUNK, t.shape[1]), lambda n: (n, 0)) for t in ins] + [_full_spec(bd)],
        out_specs=[pl.BlockSpec((DN_CHUNK, GROUP_W), lambda n: (n, 0)), pl.BlockSpec((1, GROUP_W, GROUP_W), lambda n: (n, 0, 0))],
        out_shape=[jax.ShapeDtypeStruct((S, GROUP_W), f32), jax.ShapeDtypeStruct((N, GROUP_W, GROUP_W), f32)],
        scratch_shapes=[pltpu.VMEM((GROUP_W, GROUP_W), f32)],
        compiler_params=_cparams(('arbitrary',)),
    )(*ins, bd)


def _dn_scan_bwd(ins, states, do, name):
    S = ins[0].shape[0]
    N = S // DN_CHUNK
    bd = _head_sum_matrix()

    def body(*refs):
        s_ref, do_ref = refs[9], refs[10]
        bd_ref = refs[11]
        outs = refs[12:21]
        dstate = refs[21]

        @pl.when(pl.program_id(0) == 0)
        def _():
            dstate[...] = jnp.zeros_like(dstate)

        bd_val = bd_ref[...]
        _, vjp = jax.vjp(lambda *a: _f_dn_step(*a, bd_val), *[r[...] for r in refs[:9]], s_ref[0])
        grads = vjp((do_ref[...], dstate[...]))
        for o, g in zip(outs, grads[:9]):
            o[...] = g
        dstate[...] = grads[9]

    def rev(n):
        return (N - 1 - n, 0)

    res = pl.pallas_call(
        body, name=name, grid=(N,),
        in_specs=[pl.BlockSpec((DN_CHUNK, t.shape[1]), rev) for t in ins] +
                 [pl.BlockSpec((1, GROUP_W, GROUP_W), lambda n: (N - 1 - n, 0, 0)), pl.BlockSpec((DN_CHUNK, GROUP_W), rev),
                  _full_spec(bd)],
        out_specs=[pl.BlockSpec((DN_CHUNK, t.shape[1]), rev) for t in ins],
        out_shape=[jax.ShapeDtypeStruct(t.shape, f32) for t in ins],
        scratch_shapes=[pltpu.VMEM((GROUP_W, GROUP_W), f32)],
        compiler_params=_cparams(('arbitrary',)),
    )(*ins, states, do, bd)
    return list(res)


def _f_dn_post(o, gate, gain, hmean):
    return (o * lax.rsqrt(_hdot(o * o, hmean) + EPS) * gain * (gate * jax.nn.sigmoid(gate)),)


def _dn_delays(x, name):
    S, C = x.shape
    tm = _pick(S, (256, 128))

    def body(prev_ref, cur_ref, *outs):
        before = jnp.where(pl.program_id(0) > 0, prev_ref[...], 0.0)
        both = jnp.concatenate([before, cur_ref[...]], axis=0)
        for o, k in zip(outs, range(DN_CONV - 1, 0, -1)):
            o[...] = pltpu.roll(both, k, 0)[tm:]

    spec = pl.BlockSpec((tm, C), lambda i: (i, 0))
    return pl.pallas_call(
        body, name=name, grid=(S // tm,),
        in_specs=[pl.BlockSpec((tm, C), lambda i: (jnp.maximum(i - 1, 0), 0)), spec],
        out_specs=[spec] * (DN_CONV - 1), out_shape=[jax.ShapeDtypeStruct((S, C), x.dtype)] * (DN_CONV - 1),
        compiler_params=_cparams(('parallel',), big=True),
    )(x, x)


def _dn_undelay_sum(ds, name):
    S, C = ds[0].shape
    tm = _pick(S, (256, 128))
    n = S // tm

    def body(*refs):
        o = refs[-1]
        total = refs[2 * (DN_CONV - 1)][...]
        for j in range(DN_CONV - 1):
            k = DN_CONV - 1 - j
            after = jnp.where(pl.program_id(0) < n - 1, refs[2 * j + 1][...], 0.0)
            both = jnp.concatenate([refs[2 * j][...], after], axis=0)
            total = total + pltpu.roll(both, 2 * tm - k, 0)[:tm]
        o[...] = total

    spec = pl.BlockSpec((tm, C), lambda i: (i, 0))
    nxt = pl.BlockSpec((tm, C), lambda i: (jnp.minimum(i + 1, n - 1), 0))
    args, in_specs = [], []
    for j in range(DN_CONV - 1):
        args += [ds[j], ds[j]]
        in_specs += [spec, nxt]
    return pl.pallas_call(
        body, name=name, grid=(n,), in_specs=in_specs + [spec], out_specs=spec,
        out_shape=jax.ShapeDtypeStruct((S, C), f32), compiler_params=_cparams(('parallel',), big=True),
    )(*args, ds[DN_CONV - 1])


def _dn_fwd(qkv, a, b, gate, mp, l):
    S = qkv.shape[0]
    tm = _pick(S, (256, 128))
    xs = [*_dn_delays(qkv, f'dn_delay_{l}'), qkv]
    ab = jnp.pad(jnp.concatenate([a, b], axis=1), ((0, 0), (0, LANES - 2 * N_HEADS)))
    sel = np.zeros((2, LANES, GROUP_W), np.float32)
    for h in range(N_HEADS):
        sel[0, h, h * HEAD_DIM:(h + 1) * HEAD_DIM] = 1.0
        sel[1, N_HEADS + h, h * HEAD_DIM:(h + 1) * HEAD_DIM] = 1.0
    pre_params = [*[mp['dn_conv'][j][None] for j in range(DN_CONV)], jnp.repeat(mp['dn_a_log'], HEAD_DIM)[None],
                  jnp.repeat(mp['dn_dt_bias'], HEAD_DIM)[None], jnp.asarray(sel[0]), jnp.asarray(sel[1]), _head_sum_matrix()]
    pre = _tile_fwd(_f_dn_pre, [*xs, ab], pre_params, [(GROUP_W, f32)] * 5, tm, f'dn_pre_fwd_{l}')
    chunk_outs = [(GROUP_W, f32)] * 4 + [(HEAD_DIM, f32)] * 4 + [(GROUP_W, f32)]
    parts = _tile_fwd(_f_dn_chunks, pre, [], chunk_outs, DN_CHUNK * DN_CHUNKS_PER_STEP, f'dn_chunk_fwd_{l}')
    o, states = _dn_scan_fwd(parts, f'dn_scan_fwd_{l}')
    post_params = [jnp.tile(mp['dn_o_norm'], N_HEADS)[None], _head_mean_matrix()]
    (y,) = _tile_fwd(_f_dn_post, [o, gate], post_params, [(GROUP_W, f32)], tm, f'dn_post_fwd_{l}')
    return y, (xs, ab, pre_params, pre, parts, states, o, gate, post_params)


def _dn_bwd(dy, saved, l):
    xs, ab, pre_params, pre, parts, states, o, gate, post_params = saved
    S = dy.shape[0]
    tm = _pick(S, (256, 128))
    (do, dgate), (dgain,) = _tile_bwd(_f_dn_post, [o, gate], post_params, [dy], [True, True], [True, False], tm,
                                      f'dn_post_bwd_{l}')
    dparts = _dn_scan_bwd(parts, states, do, f'dn_scan_bwd_{l}')
    dpre, _ = _tile_bwd(_f_dn_chunks, pre, [], dparts, [True] * 5, [], DN_CHUNK * DN_CHUNKS_PER_STEP, f'dn_chunk_bwd_{l}')
    dins, dpar = _tile_bwd(_f_dn_pre, [*xs, ab], pre_params, dpre, [True] * 5, [True] * 6 + [False] * 3, tm,
                           f'dn_pre_bwd_{l}')
    dqkv = _dn_undelay_sum(dins[:DN_CONV], f'dn_undelay_{l}')
    dab = dins[DN_CONV]
    grads = {'dn_conv': jnp.concatenate(dpar[:DN_CONV], axis=0),
             'dn_a_log': dpar[4].reshape(N_HEADS, HEAD_DIM).sum(1), 'dn_dt_bias': dpar[5].reshape(N_HEADS, HEAD_DIM).sum(1),
             'dn_o_norm': dgain.reshape(N_HEADS, HEAD_DIM).sum(0)}
    return dqkv, dab[:, :N_HEADS], dab[:, N_HEADS:2 * N_HEADS], dgate, grads


def _t5_bucket(dist):
    exact = T5_BUCKETS // 2
    df = jnp.maximum(dist, 1).astype(f32)
    large = exact + (jnp.log(df / exact) / math.log(T5_MAX_DIST / exact) * (T5_BUCKETS - exact)).astype(jnp.int32)
    large = jnp.minimum(large, T5_BUCKETS - 1)
    return jnp.where(dist < exact, dist, large)


def _split_cols(t, sizes):
    out, start = [], 0
    for s in sizes:
        out.append(t[..., start:start + s])
        start += s
    return out


def _mixers_fwd(proj, mp, l):
    c_q, c_kv, k_rope, u_s5, qkv_dil, qkv_dn, a_dn, b_dn, gate_dn = _split_cols(proj, IN_SPLITS)
    y_mla, s_mla = _mla_fwd(c_q, c_kv, k_rope, mp, l)
    y_s5, s_s5 = _s5_fwd(u_s5, mp, l)
    y_dil, s_dil = _dil_fwd(qkv_dil, mp, l)
    y_dn, s_dn = _dn_fwd(qkv_dn, a_dn, b_dn, gate_dn, mp, l)
    return jnp.concatenate([y_mla, y_s5, y_dil, y_dn], axis=-1), (s_mla, s_s5, s_dil, s_dn)


def _mixers_bwd(dmixed, saved, l):
    s_mla, s_s5, s_dil, s_dn = saved
    d_mla, d_s5, d_dil, d_dn = _split_cols(dmixed, (GROUP_W,) * 4)
    dc_q, dc_kv, dk_rope, g_mla = _mla_bwd(d_mla, s_mla, l)
    du, g_s5 = _s5_bwd(d_s5, s_s5, l)
    dqkv_dil, g_dil = _dil_bwd(d_dil, s_dil, l)
    dqkv_dn, da, db, dgate, g_dn = _dn_bwd(d_dn, s_dn, l)
    parts = [dc_q, dc_kv, dk_rope, du, dqkv_dil, dqkv_dn, da, db, dgate]
    dproj = jnp.concatenate([p.astype(bf16) for p in parts], axis=-1)
    return dproj, {**g_mla, **g_s5, **g_dil, **g_dn}


MIXER_PARAMS = ['mla_q_norm', 'mla_kv_norm', 'mla_w_uq', 'mla_w_ukv', 'mla_qk_q', 'mla_qk_k', 's5_lambda_re',
                's5_lambda_im', 's5_log_dt', 's5_b_re', 's5_b_im', 's5_c_re', 's5_c_im', 's5_d', 's5_w_glu',
                'dil_q_norm', 'dil_k_norm', 't5_bias', 'dn_conv', 'dn_a_log', 'dn_dt_bias', 'dn_o_norm']


def _layer_fwd_mix(h, W, l):
    S = h.shape[0]
    tm = _pick(S, (256, 128))
    g1 = W['attn_norm'][l][None]
    (n1,) = _tile_fwd(_f_rms, [h], [g1], [(D_MODEL, bf16)], tm, f'rms1_fwd_{l}')
    proj = _mm(n1, W['w_in'][l], 'nt', f'proj_fwd_{l}')
    mp = {k: (W[k] if k == 't5_bias' else W[k][l]).astype(f32) for k in MIXER_PARAMS}
    if 'dil_tables' in W:
        mp['dil_tables'] = W['dil_tables']
    mixed, mix_saved = _mixers_fwd(proj, mp, l)
    mixed_b = mixed.astype(bf16)
    h2 = _mm(mixed_b, W['w_out'][l], 'nn', f'out_fwd_{l}', add=h)
    return h2, dict(h=h, n1=n1, mix=mix_saved, mixed=mixed_b, h2=h2)


def _layer_fwd_ffn(h2, W, l, saved):
    S = h2.shape[0]
    tm = _pick(S, (256, 128))
    g2 = W['ffn_norm'][l][None]
    (n2,) = _tile_fwd(_f_rms, [h2], [g2], [(D_MODEL, bf16)], tm, f'rms2_fwd_{l}')
    u = _mm(n2, W['ffn_w1'][l], 'nt', f'ffn1_fwd_{l}', out_dtype=bf16)
    v = _mm(n2, W['ffn_w3'][l], 'nt', f'ffn3_fwd_{l}', out_dtype=bf16)
    (act,) = _tile_fwd(_f_swiglu, [u, v], [], [(FFN_HIDDEN, bf16)], tm, f'swiglu_fwd_{l}')
    h3 = _mm(act, W['ffn_w2'][l], 'nn', f'ffn2_fwd_{l}', add=h2)
    saved.update(n2=n2, u=u, v=v, act=act)
    return h3


def _layer_bwd_ffn(dh3, saved, W, l):
    S = dh3.shape[0]
    tm = _pick(S, (256, 128))
    g2 = W['ffn_norm'][l][None]
    grads = {}
    dact = _mm(dh3, W['ffn_w2'][l], 'nt', f'ffn2_dx_{l}', out_dtype=bf16)
    grads['ffn_w2'] = _mm(saved['act'], dh3, 'tn', f'ffn2_dw_{l}', out_dtype=bf16)
    (du, dv), _ = _tile_bwd(_f_swiglu, [saved['u'], saved['v']], [], [dact], [True, True], [], tm, f'swiglu_bwd_{l}',
                            dt_dtypes=[bf16, bf16])
    dn2 = _mm(dv, W['ffn_w3'][l], 'nn', f'ffn3_dx_{l}', add=_mm(du, W['ffn_w1'][l], 'nn', f'ffn1_dx_{l}'))
    grads['ffn_w1'] = _mm(du, saved['n2'], 'tn', f'ffn1_dw_{l}', out_dtype=bf16)
    grads['ffn_w3'] = _mm(dv, saved['n2'], 'tn', f'ffn3_dw_{l}', out_dtype=bf16)
    (dh2n,), (dg2,) = _tile_bwd(_f_rms, [saved['h2']], [g2], [dn2], [True], [True], tm, f'rms2_bwd_{l}')
    grads['ffn_norm'] = dg2[0]
    return (dh3, dh2n), grads


def _layer_bwd_mix(dh2, saved, W, l):
    S = dh2.shape[0]
    tm = _pick(S, (256, 128))
    g1 = W['attn_norm'][l][None]
    grads = {}
    dmixed = _mm(dh2, W['w_out'][l], 'nt', f'out_dx_{l}')
    grads['w_out'] = _mm(saved['mixed'], dh2, 'tn', f'out_dw_{l}', out_dtype=bf16)
    dproj, dmp = _mixers_bwd(dmixed, saved['mix'], l)
    grads.update(dmp)
    dn1 = _mm(dproj, W['w_in'][l], 'nn', f'proj_dx_{l}')
    grads['w_in'] = _mm(dproj, saved['n1'], 'tn', f'proj_dw_{l}', out_dtype=bf16)
    (dh1n,), (dg1,) = _tile_bwd(_f_rms, [saved['h']], [g1], [dn1], [True], [True], tm, f'rms1_bwd_{l}')
    grads['attn_norm'] = dg1[0]
    return (dh2, dh1n), grads


def kernel(x, attn_norm, w_in, w_out, mla_q_norm, mla_kv_norm, mla_w_uq, mla_w_ukv, mla_qk_q, mla_qk_k, s5_lambda_re, s5_lambda_im, s5_log_dt, s5_b_re, s5_b_im, s5_c_re, s5_c_im, s5_d, s5_w_glu, dil_q_norm, dil_k_norm, t5_bias, dn_conv, dn_a_log, dn_dt_bias, dn_o_norm, ffn_norm, ffn_w1, ffn_w3, ffn_w2, loss_target, m_attn_norm, m_w_in, m_w_out, m_mla_q_norm, m_mla_kv_norm, m_mla_w_uq, m_mla_w_ukv, m_mla_qk_q, m_mla_qk_k, m_s5_lambda_re, m_s5_lambda_im, m_s5_log_dt, m_s5_b_re, m_s5_b_im, m_s5_c_re, m_s5_c_im, m_s5_d, m_s5_w_glu, m_dil_q_norm, m_dil_k_norm, m_t5_bias, m_dn_conv, m_dn_a_log, m_dn_dt_bias, m_dn_o_norm, m_ffn_norm, m_ffn_w1, m_ffn_w3, m_ffn_w2, v_attn_norm, v_w_in, v_w_out, v_mla_q_norm, v_mla_kv_norm, v_mla_w_uq, v_mla_w_ukv, v_mla_qk_q, v_mla_qk_k, v_s5_lambda_re, v_s5_lambda_im, v_s5_log_dt, v_s5_b_re, v_s5_b_im, v_s5_c_re, v_s5_c_im, v_s5_d, v_s5_w_glu, v_dil_q_norm, v_dil_k_norm, v_t5_bias, v_dn_conv, v_dn_a_log, v_dn_dt_bias, v_dn_o_norm, v_ffn_norm, v_ffn_w1, v_ffn_w3, v_ffn_w2):
    given = dict(locals())
    def seen(n, t):
        if n in COLUMNS_FIRST:
            return jnp.transpose(t, (2, 0, 1))
        return jnp.swapaxes(t, 1, 2) if n in TRANSPOSED else t

    def given_back(n, t):
        return jnp.transpose(t, (1, 2, 0)) if n in COLUMNS_FIRST else seen(n, t)

    def layer_of(n, t, l):
        return t[:, l] if n in COLUMNS_FIRST else t[l]

    w_loc = {n: seen(n, given[n]) for n in WEIGHTS}
    m_loc = {n: seen(n, given['m_' + n]) for n in WEIGHTS}
    v_loc = {n: seen(n, given['v_' + n]) for n in WEIGHTS}
    big_names = list(BIG)

    own = 2 * lax.axis_index('x') + lax.axis_index('y')
    groups = [[(n, 0) for n in GATHER_FIRST], [(n, 0) for n in GATHER_FFN], [(n, 1) for n in big_names]]
    started, order = [], jnp.zeros((8, LANES), f32)
    for gi, group in enumerate(groups):
        blocks = [layer_of(n, w_loc[n], l).astype(bf16) for n, l in group]
        lands = [lax.empty((N_SHARDS,) + b.shape, bf16) for b in blocks]
        send_sems, recv_sems, blocks, lands, order = _to_chips_start(blocks, lands, False, order, f'gather_start_{gi}')
        started.append((send_sems, recv_sems, blocks, lands))
    W = {n: [None] * DEPTH for n in big_names}
    for n in SMALL:
        W[n] = w_loc[n]
    W['dil_tables'] = _dil_tables(w_loc['t5_bias'])

    def arrive(gi, after):
        send_sems, recv_sems, blocks, lands = started[gi]
        blocks, lands = _to_chips_wait(send_sems, recv_sems, blocks, lands, False, after, f'gather_wait_{gi}')
        for (n, l), block, land in zip(groups[gi], blocks, lands):
            W[n][l] = _from_shards(n, lax.dynamic_update_slice(land, block[None], (own, 0, 0)))

    arrive(0, order)
    h = x[0]
    saved = []
    for l in range(DEPTH):
        h2, sv = _layer_fwd_mix(h, W, l)
        if l == 0:
            arrive(1, h2)
        h = _layer_fwd_ffn(h2, W, l, sv)
        if l == 0:
            arrive(2, h)
        saved.append(sv)
    parts_loss, dh = _loss_head(h, loss_target[0])
    local_loss = jnp.sum(parts_loss)

    layer_grads = [dict() for _ in range(DEPTH)]
    sent = []

    def send(group, tag):
        srcs = [_by_shard(n, layer_grads[l][n]).astype(bf16) for n, l in group]
        lands = [lax.empty((3,) + s.shape[1:], bf16) for s in srcs]
        send_sems, recv_sems, srcs, lands, token = _to_chips_start(srcs, lands, True, jnp.zeros((8, LANES), f32),
                                                                   f'reduce_start_{tag}')
        sent.append((group, tag, send_sems, recv_sems, srcs, lands))
        return token[0, 0]

    for l in reversed(range(DEPTH)):
        (dh3, dh2n), g_ffn = _layer_bwd_ffn(dh, saved[l], W, l)
        layer_grads[l].update(g_ffn)
        dh2 = dh3 + dh2n
        if l == 0:
            dh2 = dh2 + send([(n, 0) for n in GATHER_FFN], 'ffn0')
        (dh2, dh1n), g_mix = _layer_bwd_mix(dh2, saved[l], W, l)
        layer_grads[l].update(g_mix)
        dh = dh2 + dh1n
        if l == 1:
            dh = dh + send([(n, 1) for n in big_names], 'layer1')
    last = send([(n, 0) for n in GATHER_FIRST], 'first0')
    grad_x = dh[None]
    small_full = []
    for n in SMALL:
        if n == 't5_bias':
            small_full.append(_t5_grad([a_ + b_ for a_, b_ in zip(layer_grads[0]['t5_tables'], layer_grads[1]['t5_tables'])]))
        else:
            small_full.append(jnp.stack([layer_grads[l][n] for l in range(DEPTH)]))

    small_shapes = [w_loc[n].shape for n in SMALL] + [(1,)]
    nothing = [jnp.zeros((1,), f32)]
    small_pack = _pack(small_full + [local_loss.reshape(1)]) + last
    _, recv_small = _swap_with_sibling([], small_pack)
    chip_small = _small_chip_sum(small_pack, recv_small)
    _, from_chips_small = _exchange_between_chips([], chip_small)

    mine = {}
    for group, tag, send_sems, recv_sems, srcs, lands in sent:
        srcs, lands = _to_chips_wait(send_sems, recv_sems, srcs, lands, True, from_chips_small, f'reduce_wait_{tag}')
        for (n, l), src, land in zip(group, srcs, lands):
            mine[(n, l)] = _partial_sum(src, land, f'partial_{n}_{l}')
    keys = [(n, l) for n in big_names for l in range(DEPTH)]
    theirs = dict(zip(keys, _swap_partials([mine[k] for k in keys])))

    g_small_p, d_small_p, m_small_p, v_small_p = _small_update(
        small_pack, recv_small, from_chips_small, _pack([w_loc[n] for n in SMALL] + nothing),
        _pack([m_loc[n] for n in SMALL] + nothing), _pack([v_loc[n] for n in SMALL] + nothing))
    loss = _unpack(g_small_p, small_shapes)[-1][0]
    grad, delta, new_m, new_v = {}, {}, {}, {}
    for n, g_, d_, m_, v_ in zip(SMALL, _unpack(g_small_p, small_shapes), _unpack(d_small_p, small_shapes),
                                 _unpack(m_small_p, small_shapes), _unpack(v_small_p, small_shapes)):
        grad[n], delta[n], new_m[n], new_v[n] = g_, d_, m_, v_
    for n in big_names:
        update = _adamw_layer_in_the_middle if n in COLUMNS_FIRST else _adamw
        results = update(w_loc[n], m_loc[n], v_loc[n], [mine[(n, l)] for l in range(DEPTH)],
                         [theirs[(n, l)] for l in range(DEPTH)], 'adamw_' + n)
        grad[n], delta[n], new_m[n], new_v[n] = [given_back(n, t) for t in results]
    return (loss, grad_x, *[grad[n] for n in WEIGHTS], *[delta[n] for n in WEIGHTS],
            *[new_m[n] for n in WEIGHTS], *[new_v[n] for n in WEIGHTS])
```

```python
import functools
import math

import numpy as np
import jax
import jax.numpy as jnp
from jax import lax
from jax.experimental import pallas as pl
from jax.experimental.pallas import tpu as pltpu

f32 = jnp.float32
bf16 = jnp.bfloat16
HI = lax.Precision.HIGHEST
MESH = pl.DeviceIdType.MESH

VMEM_LIMIT_BYTES = 48 * 1024 * 1024
MM_VMEM_BUDGET_BYTES = 32 * 1024 * 1024
LANES = 128

D_MODEL = 1024
DEPTH = 2
GROUP_W = 256
HEAD_DIM = 64
EPS = 1e-6
NEG_INF = -1e30
N_HEADS = 4
MLA_NOPE, MLA_ROPE = 64, 32
MLA_DQK = MLA_NOPE + MLA_ROPE
ROPE_THETA = 10000.0
Q_BLOCK = 128
S5_G, S5_CG, S5_P = 16, 16, 64
DIL_PAIRS = ((128, 1), (512, 4), (2048, 16))
T5_BUCKETS, T5_MAX_DIST = 32, 2048
DN_CHUNK = 64
FFN_HIDDEN = 2816
IN_SPLITS = (256, 128, 32, 256, 768, 768, 4, 4, 256)
IN_COLS = sum(IN_SPLITS)

ADAM_LR, ADAM_B1, ADAM_B2, ADAM_EPS, ADAM_WD, ADAM_STEP = 0.001, 0.9, 0.999, 1e-08, 0.01, 10

WEIGHTS = ['attn_norm', 'w_in', 'w_out', 'mla_q_norm', 'mla_kv_norm', 'mla_w_uq', 'mla_w_ukv', 'mla_qk_q', 'mla_qk_k',
           's5_lambda_re', 's5_lambda_im', 's5_log_dt', 's5_b_re', 's5_b_im', 's5_c_re', 's5_c_im', 's5_d', 's5_w_glu',
           'dil_q_norm', 'dil_k_norm', 't5_bias', 'dn_conv', 'dn_a_log', 'dn_dt_bias', 'dn_o_norm', 'ffn_norm',
           'ffn_w1', 'ffn_w3', 'ffn_w2']
BIG = {'w_in': 1, 'w_out': 1, 'mla_w_uq': 2, 'mla_w_ukv': 2, 's5_w_glu': 2, 'dn_conv': 2, 'ffn_w1': 1, 'ffn_w3': 1,
       'ffn_w2': 1}
TRANSPOSED = ('ffn_w1', 'ffn_w3')
COLUMNS_FIRST = ('w_in',)
SMALL = [n for n in WEIGHTS if n not in BIG]
GATHER_FIRST = ['w_in', 'mla_w_uq', 'mla_w_ukv', 's5_w_glu', 'dn_conv', 'w_out']
GATHER_FFN = ['ffn_w1', 'ffn_w3', 'ffn_w2']
N_SHARDS = 4
PACK_COLS = 1024


def _cparams(sem=None, big=False):
    kw = {}
    if sem is not None:
        kw['dimension_semantics'] = sem
    if big:
        kw['vmem_limit_bytes'] = VMEM_LIMIT_BYTES
    return pltpu.CompilerParams(**kw)


def _pick(n, prefs):
    for p in prefs:
        if p <= n and n % p == 0:
            return p
    return n


def _lane_tile(n, cap):
    for t in range(cap - cap % LANES, 0, -LANES):
        if n % t == 0:
            return t
    return n


def _mm(a, b, mode, name, add=None, out_dtype=f32):
    if mode == 'nn':
        (M, K), (K2, N) = a.shape, b.shape
    elif mode == 'nt':
        (M, K), (N, K2) = a.shape, b.shape
    else:
        (K, M), (K2, N) = a.shape, b.shape
    assert K == K2, (name, a.shape, b.shape)
    tk = K if K <= 2816 else _pick(K, (2816, 2048, 1408, 1024, 512))
    cap_m, cap_n = (1408 if mode == 'tn' else 512), 1408

    def need(tm_, tn_):
        per_step = tm_ * tk * a.dtype.itemsize + tk * tn_ * b.dtype.itemsize + tm_ * tn_ * jnp.dtype(out_dtype).itemsize
        if add is not None:
            per_step += tm_ * tn_ * add.dtype.itemsize
        return 2 * per_step + tm_ * tn_ * 4

    tm, tn = _lane_tile(M, cap_m), _lane_tile(N, cap_n)
    while need(tm, tn) > MM_VMEM_BUDGET_BYTES and cap_m > LANES:
        cap_m //= 2
        tm = _lane_tile(M, cap_m)
    while need(tm, tn) > MM_VMEM_BUDGET_BYTES and cap_n > LANES:
        cap_n //= 2
        tn = _lane_tile(N, cap_n)
    nk = K // tk
    dims = {'nn': (((1,), (0,)), ((), ())), 'nt': (((1,), (1,)), ((), ())), 'tn': (((0,), (0,)), ((), ()))}[mode]
    has_add = add is not None

    def body(*refs):
        a_ref, b_ref = refs[0], refs[1]
        add_ref = refs[2] if has_add else None
        o_ref = refs[3] if has_add else refs[2]
        part = lax.dot_general(a_ref[...].astype(bf16), b_ref[...].astype(bf16), dims, preferred_element_type=f32)
        if nk == 1:
            if has_add:
                part = part + add_ref[...].astype(f32)
            o_ref[...] = part.astype(out_dtype)
        else:
            acc_ref = refs[-1]
            k = pl.program_id(2)

            @pl.when(k == 0)
            def _():
                acc_ref[...] = part

            @pl.when(k > 0)
            def _():
                acc_ref[...] += part

            @pl.when(k == nk - 1)
            def _():
                r = acc_ref[...]
                if has_add:
                    r = r + add_ref[...].astype(f32)
                o_ref[...] = r.astype(out_dtype)

    if mode == 'nn':
        a_spec = pl.BlockSpec((tm, tk), lambda i, j, k: (i, k))
        b_spec = pl.BlockSpec((tk, tn), lambda i, j, k: (k, j))
    elif mode == 'nt':
        a_spec = pl.BlockSpec((tm, tk), lambda i, j, k: (i, k))
        b_spec = pl.BlockSpec((tn, tk), lambda i, j, k: (j, k))
    else:
        a_spec = pl.BlockSpec((tk, tm), lambda i, j, k: (k, i))
        b_spec = pl.BlockSpec((tk, tn), lambda i, j, k: (k, j))
    in_specs = [a_spec, b_spec]
    args = [a, b]
    if has_add:
        in_specs.append(pl.BlockSpec((tm, tn), lambda i, j, k: (i, j)))
        args.append(add)
    return pl.pallas_call(
        body, name=name, grid=(M // tm, N // tn, nk), in_specs=in_specs,
        out_specs=pl.BlockSpec((tm, tn), lambda i, j, k: (i, j)),
        out_shape=jax.ShapeDtypeStruct((M, N), out_dtype),
        scratch_shapes=[pltpu.VMEM((tm, tn), f32)] if nk > 1 else [],
        compiler_params=_cparams(('parallel', 'parallel', 'arbitrary'), big=True),
    )(*args)


def _full_spec(p):
    nd = p.ndim
    return pl.BlockSpec(p.shape, lambda i, _nd=nd: (0,) * _nd)


def _tile_fwd(f, tiled, params, outs, tm, name):
    S = tiled[0].shape[0]
    nt, npar = len(tiled), len(params)

    def body(*refs):
        vals = [r[...].astype(f32) for r in refs[:nt + npar]]
        res = f(*vals)
        for r, o in zip(res, refs[nt + npar:]):
            o[...] = r.astype(o.dtype)

    return pl.pallas_call(
        body, name=name, grid=(S // tm,),
        in_specs=[pl.BlockSpec((tm, t.shape[1]), lambda i: (i, 0)) for t in tiled] + [_full_spec(p) for p in params],
        out_specs=[pl.BlockSpec((tm, c), lambda i: (i, 0)) for c, _ in outs],
        out_shape=[jax.ShapeDtypeStruct((S, c), dt) for c, dt in outs],
        compiler_params=_cparams(('parallel',), big=True),
    )(*tiled, *params)


def _tile_bwd(f, tiled, params, cts, diff_t, diff_p, tm, name, dt_dtypes=None):
    S = tiled[0].shape[0]
    nt, npar, nc = len(tiled), len(params), len(cts)
    it = [i for i in range(nt) if diff_t[i]]
    ip = [i for i in range(npar) if diff_p[i]]
    if dt_dtypes is None:
        dt_dtypes = [f32] * len(it)

    def body(*refs):
        vals = [r[...].astype(f32) for r in refs[:nt + npar]]
        ct_vals = tuple(r[...].astype(f32) for r in refs[nt + npar:nt + npar + nc])
        out_refs = refs[nt + npar + nc:]

        def g(*dv):
            full = list(vals)
            for k, i in enumerate(it):
                full[i] = dv[k]
            for k, i in enumerate(ip):
                full[nt + i] = dv[len(it) + k]
            return tuple(f(*full))

        _, vjp = jax.vjp(g, *[vals[i] for i in it], *[vals[nt + i] for i in ip])
        grads = vjp(ct_vals)
        for k in range(len(it)):
            out_refs[k][...] = grads[k].astype(out_refs[k].dtype)
        step = pl.program_id(0)
        for k in range(len(ip)):
            o = out_refs[len(it) + k]
            gk = grads[len(it) + k]

            @pl.when(step == 0)
            def _(o=o, gk=gk):
                o[...] = gk

            @pl.when(step > 0)
            def _(o=o, gk=gk):
                o[...] += gk

    out_specs = [pl.BlockSpec((tm, tiled[i].shape[1]), lambda i_: (i_, 0)) for i in it] + [_full_spec(params[i]) for i in ip]
    out_shape = [jax.ShapeDtypeStruct(tiled[i].shape, dt_dtypes[k]) for k, i in enumerate(it)] + \
                [jax.ShapeDtypeStruct(params[i].shape, f32) for i in ip]
    res = pl.pallas_call(
        body, name=name, grid=(S // tm,),
        in_specs=[pl.BlockSpec((tm, t.shape[1]), lambda i: (i, 0)) for t in tiled] + [_full_spec(p) for p in params] +
                 [pl.BlockSpec((tm, c.shape[1]), lambda i: (i, 0)) for c in cts],
        out_specs=out_specs, out_shape=out_shape,
        compiler_params=_cparams(('arbitrary',), big=True),
    )(*tiled, *params, *cts)
    return list(res[:len(it)]), list(res[len(it):])


def _rms(x, g):
    return x * lax.rsqrt(jnp.mean(x * x, axis=-1, keepdims=True) + EPS) * g


def _f_rms(x, g):
    return (_rms(x, g),)


def _f_swiglu(u, v):
    return (u * jax.nn.sigmoid(u) * v,)


def _loss_head(y, target):
    S, D = y.shape
    tm = _pick(S, (256, 128))

    def body(y_ref, t_ref, part_ref, dy_ref):
        e = y_ref[...] - t_ref[...]
        dy_ref[...] = e * (1.0 / D)
        s = 0.5 * jnp.sum(jnp.sum(e * e, axis=1, keepdims=True), axis=0, keepdims=True) * (1.0 / D)
        r = lax.broadcasted_iota(jnp.int32, (8, LANES), 0)
        c = lax.broadcasted_iota(jnp.int32, (8, LANES), 1)
        part_ref[0] = jnp.where((r == 0) & (c == 0), s, 0.0)

    return pl.pallas_call(
        body, name='loss_head', grid=(S // tm,),
        in_specs=[pl.BlockSpec((tm, D), lambda i: (i, 0))] * 2,
        out_specs=[pl.BlockSpec((1, 8, LANES), lambda i: (i, 0, 0)), pl.BlockSpec((tm, D), lambda i: (i, 0))],
        out_shape=[jax.ShapeDtypeStruct((S // tm, 8, LANES), f32), jax.ShapeDtypeStruct((S, D), f32)],
        compiler_params=_cparams(('parallel',)),
    )(y, target)


def _pack_rows_of(shape):
    rows = -(-math.prod(shape) // PACK_COLS)
    return -(-rows // 8) * 8


def _pack(arrs):
    parts = []
    for a in arrs:
        rows = _pack_rows_of(a.shape)
        flat = a.astype(f32).reshape(-1)
        parts.append(jnp.pad(flat, (0, rows * PACK_COLS - flat.shape[0])).reshape(rows, PACK_COLS))
    return jnp.concatenate(parts, axis=0)


def _unpack(pack, shapes):
    out, row = [], 0
    for s in shapes:
        rows = _pack_rows_of(s)
        out.append(pack[row:row + rows].reshape(-1)[:math.prod(s)].reshape(s))
        row += rows
    return out


ANY = pl.BlockSpec(memory_space=pl.ANY)


def _place():
    return lax.axis_index('x'), lax.axis_index('y'), lax.axis_index('c')


def _where():
    return jnp.stack([lax.axis_index('c'), 2 * lax.axis_index('x') + lax.axis_index('y')]).astype(jnp.int32)


def _remote(src, dst, send_sems, recv_sems, k, to):
    return pltpu.make_async_remote_copy(src_ref=src, dst_ref=dst, send_sem=send_sems.at[k], recv_sem=recv_sems.at[k],
                                        device_id=to, device_id_type=MESH)


def _swap_with_sibling(gs, small):
    n = len(gs)

    def body(*refs):
        g_refs, s_ref = refs[:n], refs[n]
        r_refs, rs_ref = refs[n + 1:2 * n + 1], refs[2 * n + 1]
        send_sems, recv_sems = refs[2 * n + 2:]
        x, y, c = _place()
        sib = (x, y, 1 - c)
        cps = [_remote(g_refs[t].at[:, 1 - c], r_refs[t], send_sems, recv_sems, t, sib) for t in range(n)]
        cps.append(_remote(s_ref, rs_ref, send_sems, recv_sems, n, sib))
        for cp in cps:
            cp.start()
        for cp in cps:
            cp.wait()

    res = pl.pallas_call(
        body, name='swap_with_sibling', in_specs=[ANY] * (n + 1), out_specs=[ANY] * (n + 1),
        out_shape=[jax.ShapeDtypeStruct((N_SHARDS,) + g.shape[2:], g.dtype) for g in gs] +
                  [jax.ShapeDtypeStruct(small.shape, small.dtype)],
        scratch_shapes=[pltpu.SemaphoreType.DMA((n + 1,)), pltpu.SemaphoreType.DMA((n + 1,))],
    )(*gs, small)
    return list(res[:n]), res[n]


def _swap_partials(ts):
    n = len(ts)

    def body(*refs):
        t_refs, o_refs = refs[:n], refs[n:2 * n]
        send_sems, recv_sems = refs[2 * n:]
        x, y, c = _place()
        cps = [_remote(t_refs[t], o_refs[t], send_sems, recv_sems, t, (x, y, 1 - c)) for t in range(n)]
        for cp in cps:
            cp.start()
        for cp in cps:
            cp.wait()

    return pl.pallas_call(
        body, name='swap_partials', in_specs=[ANY] * n, out_specs=[ANY] * n,
        out_shape=[jax.ShapeDtypeStruct(t.shape, t.dtype) for t in ts],
        scratch_shapes=[pltpu.SemaphoreType.DMA((n,)), pltpu.SemaphoreType.DMA((n,))],
    )(*ts)


HBM = pl.BlockSpec(memory_space=pltpu.HBM)
SEM = pl.BlockSpec(memory_space=pltpu.SEMAPHORE)
DATAFLOW = pltpu.SideEffectType.DATAFLOW_SIDE_EFFECTING


def _in_hbm(t):
    return pltpu.with_memory_space_constraint(t, pltpu.HBM)


def _other_chips():
    x, y, c = _place()
    return [(1 - x, y, c), (x, 1 - y, c), (1 - x, 1 - y, c)]


def _to_chips_copies(src_refs, land_refs, send_sems, recv_sems, per_peer):
    x, y, _ = _place()
    modes = per_peer if isinstance(per_peer, (list, tuple)) else [per_peer] * len(src_refs)
    cps = []
    for t, (src, land, mode) in enumerate(zip(src_refs, land_refs, modes)):
        for j, (px, py, pc) in enumerate(_other_chips()):
            s = src.at[2 * px + py] if mode is True else src
            d = land.at[2 * x + y] if mode is False else land.at[j]
            cps.append(_remote(s, d, send_sems, recv_sems, 3 * t + j, (px, py, pc)))
    return cps


def _to_chips_start(srcs, lands, per_peer, order, name):
    n = len(srcs)

    def body(*refs):
        src_refs, land_refs = refs[:n], refs[n:2 * n]
        send_sems, recv_sems = refs[2 * n + 1], refs[2 * n + 2]
        token = refs[-1]
        for cp in _to_chips_copies(src_refs, land_refs, send_sems, recv_sems, per_peer):
            cp.start()
        token[...] = jnp.zeros_like(token)

    res = pl.pallas_call(
        body, name=name, in_specs=[HBM] * (2 * n) + [ANY],
        out_specs=[SEM, SEM] + [HBM] * (2 * n) + [pl.BlockSpec(memory_space=pltpu.VMEM)],
        out_shape=[pltpu.SemaphoreType.DMA((3 * n,)), pltpu.SemaphoreType.DMA((3 * n,))] +
                  [pltpu.HBM(t.shape, t.dtype) for t in srcs] + [pltpu.HBM(t.shape, t.dtype) for t in lands] +
                  [jax.ShapeDtypeStruct((8, LANES), f32)],
        input_output_aliases={i: 2 + i for i in range(2 * n)},
        compiler_params=pltpu.CompilerParams(has_side_effects=DATAFLOW),
    )(*[_in_hbm(t) for t in srcs], *[_in_hbm(t) for t in lands], order)
    return res[0], res[1], list(res[2:2 + n]), list(res[2 + n:2 + 2 * n]), res[-1]


def _to_chips_wait(send_sems, recv_sems, srcs, lands, per_peer, after, name):
    n = len(srcs)

    def body(*refs):
        src_refs, land_refs = refs[:n], refs[n:2 * n]
        send_ref, recv_ref = refs[2 * n], refs[2 * n + 1]
        for cp in _to_chips_copies(src_refs, land_refs, send_ref, recv_ref, per_peer):
            cp.wait_send()
            cp.wait_recv()

    res = pl.pallas_call(
        body, name=name, in_specs=[HBM] * (2 * n) + [SEM, SEM, ANY],
        out_specs=[HBM] * (2 * n),
        out_shape=[pltpu.HBM(t.shape, t.dtype) for t in srcs] + [pltpu.HBM(t.shape, t.dtype) for t in lands],
        input_output_aliases={i: i for i in range(2 * n)},
        compiler_params=pltpu.CompilerParams(has_side_effects=DATAFLOW),
    )(*srcs, *lands, send_sems, recv_sems, after)
    return list(res[:n]), list(res[n:])


def _row_tile(a):
    return _pick(a, (512, 256, 128, 64, 32, 16, 8))


def _partial_sum(g, land, name):
    _, a, b = g.shape
    tr = _row_tile(a)

    def body(w_ref, g_ref, r_ref, o_ref):
        t = g_ref[0].astype(f32) + r_ref[0].astype(f32)
        t = t + r_ref[1].astype(f32)
        t = t + r_ref[2].astype(f32)
        o_ref[...] = t.astype(o_ref.dtype)

    return pl.pallas_call(
        body, name=name,
        grid_spec=pltpu.PrefetchScalarGridSpec(
            num_scalar_prefetch=1, grid=(a // tr,),
            in_specs=[pl.BlockSpec((1, tr, b), lambda i, w: (w[1], i, 0)), pl.BlockSpec((3, tr, b), lambda i, w: (0, i, 0))],
            out_specs=pl.BlockSpec((tr, b), lambda i, w: (i, 0))),
        out_shape=jax.ShapeDtypeStruct((a, b), bf16),
        compiler_params=_cparams(('parallel',)),
    )(_where(), g, land)


def _by_shard(name, t):
    r, c = t.shape
    if BIG[name] == 2:
        return t.reshape(r, N_SHARDS, c // N_SHARDS).transpose(1, 0, 2)
    return t.reshape(N_SHARDS, r // N_SHARDS, c)


def _from_shards(name, g):
    s, a, b = g.shape
    if BIG[name] == 2:
        return g.transpose(1, 0, 2).reshape(a, s * b)
    return g.reshape(s * a, b)


def _adam_math(w, g, m, v):
    m = ADAM_B1 * m + (1.0 - ADAM_B1) * g
    v = ADAM_B2 * v + (1.0 - ADAM_B2) * (g * g)
    m_hat = m / (1.0 - ADAM_B1 ** ADAM_STEP)
    v_hat = v / (1.0 - ADAM_B2 ** ADAM_STEP)
    delta = -ADAM_LR * (m_hat / (jnp.sqrt(v_hat) + ADAM_EPS) + ADAM_WD * w)
    return delta, m, v


def _small_update(own, sib, chips, w, m, v):
    def body(o_ref, s_ref, c_ref, w_ref, m_ref, v_ref, g_out, d_out, m_out, v_out):
        chip = o_ref[...] + s_ref[...]
        g = (chip + c_ref[0]) + (c_ref[1] + c_ref[2])
        d, mn, vn = _adam_math(w_ref[...], g, m_ref[...], v_ref[...])
        g_out[...] = g
        d_out[...] = d
        m_out[...] = mn
        v_out[...] = vn

    return pl.pallas_call(body, name='small_update', out_shape=[jax.ShapeDtypeStruct(own.shape, f32)] * 4)(
        own, sib, chips, w, m, v)


def _small_chip_sum(own, sib):
    def body(o_ref, s_ref, out):
        out[...] = o_ref[...] + s_ref[...]
    return pl.pallas_call(body, name='small_chip_sum', out_shape=jax.ShapeDtypeStruct(own.shape, f32))(own, sib)


def _adamw(w, m, v, mine, theirs, name):
    layers, a, b = w.shape
    tr = _row_tile(a)

    def body(w_ref, m_ref, v_ref, p0, p1, q0, q1, g_out, d_out, m_out, v_out):
        first = pl.program_id(0) == 0
        g = jnp.where(first, p0[...].astype(f32) + q0[...].astype(f32), p1[...].astype(f32) + q1[...].astype(f32))
        d, mn, vn = _adam_math(w_ref[0], g, m_ref[0], v_ref[0])
        g_out[0] = g
        d_out[0] = d
        m_out[0] = mn
        v_out[0] = vn

    full = pl.BlockSpec((1, tr, b), lambda l, i: (l, i, 0))
    part = pl.BlockSpec((tr, b), lambda l, i: (i, 0))
    return pl.pallas_call(body, name=name, grid=(layers, a // tr), in_specs=[full] * 3 + [part] * 4, out_specs=[full] * 4,
                          out_shape=[jax.ShapeDtypeStruct(w.shape, f32)] * 4,
                          compiler_params=_cparams(('parallel', 'parallel')))(w, m, v, *mine, *theirs)


def _adamw_layer_in_the_middle(w, m, v, mine, theirs, name):
    a, layers, b = w.shape
    assert layers == 2 and b % LANES == 0

    def body(w_ref, m_ref, v_ref, p0, p1, q0, q1, g_out, d_out, m_out, v_out):
        g = jnp.stack([p0[...].astype(f32) + q0[...].astype(f32), p1[...].astype(f32) + q1[...].astype(f32)], axis=1)
        d, mn, vn = _adam_math(w_ref[...], g, m_ref[...], v_ref[...])
        g_out[...] = g
        d_out[...] = d
        m_out[...] = mn
        v_out[...] = vn

    full = pl.BlockSpec((a, layers, LANES), lambda i: (0, 0, i))
    part = pl.BlockSpec((a, LANES), lambda i: (0, i))
    return pl.pallas_call(body, name=name, grid=(b // LANES,), in_specs=[full] * 3 + [part] * 4, out_specs=[full] * 4,
                          out_shape=[jax.ShapeDtypeStruct(w.shape, f32)] * 4,
                          compiler_params=_cparams(('parallel',), big=True))(w, m, v, *mine, *theirs)


def _dg(a, b, ca, cb):
    return lax.dot_general(a.astype(bf16), b.astype(bf16), (((ca,), (cb,)), ((), ())), preferred_element_type=f32)


@jax.custom_vjp
def _bmm(a, b):
    return _dg(a, b, 1, 0)


_bmm.defvjp(lambda a, b: (_dg(a, b, 1, 0), (a, b)), lambda r, g: (_dg(g, r[1], 1, 1), _dg(r[0], g, 0, 0)))


@jax.custom_vjp
def _bmm_nt(a, b):
    return _dg(a, b, 1, 1)


_bmm_nt.defvjp(lambda a, b: (_dg(a, b, 1, 1), (a, b)), lambda r, g: (_dg(g, r[1], 1, 0), _dg(g, r[0], 0, 0)))


@jax.custom_vjp
def _bmm_tn(a, b):
    return _dg(a, b, 0, 0)


_bmm_tn.defvjp(lambda a, b: (_dg(a, b, 0, 0), (a, b)), lambda r, g: (_dg(r[1], g, 1, 1), _dg(r[0], g, 1, 0)))


def _hdot(a, b):
    return jnp.dot(a, b, precision=HI, preferred_element_type=f32)


def _hdot_nt(a, b):
    return lax.dot_general(a, b, (((1,), (1,)), ((), ())), precision=HI, preferred_element_type=f32)


def _hdot_tn(a, b):
    return lax.dot_general(a, b, (((0,), (0,)), ((), ())), precision=HI, preferred_element_type=f32)


def _head_mask(h, width=GROUP_W):
    lane = lax.broadcasted_iota(jnp.int32, (1, width), 1)
    return ((lane >= h * HEAD_DIM) & (lane < (h + 1) * HEAD_DIM)).astype(f32)


def _rope_perm():
    p = np.zeros((LANES, LANES), np.float32)
    half = MLA_ROPE // 2
    for i in range(half):
        p[MLA_NOPE + half + i, MLA_NOPE + i] = -1.0
        p[MLA_NOPE + i, MLA_NOPE + half + i] = 1.0
    return jnp.asarray(p)


def _rope_tables(S):
    half = MLA_ROPE // 2
    freqs = ROPE_THETA ** (-jnp.arange(half, dtype=f32) / half)
    ang = jnp.arange(S, dtype=f32)[:, None] * freqs[None, :]
    cos, sin = jnp.cos(ang), jnp.sin(ang)
    ones, zeros = jnp.ones((S, MLA_NOPE), f32), jnp.zeros((S, LANES - MLA_DQK), f32)
    c_tab = jnp.concatenate([ones, cos, cos, zeros], axis=1)
    s_tab = jnp.concatenate([jnp.zeros((S, MLA_NOPE), f32), sin, sin, zeros], axis=1)
    return c_tab, s_tab


def _f_mla_pre(c_q, c_kv, krope, c_tab, s_tab, q_norm, kv_norm, wq0, wq1, wq2, wq3, wk0, wk1, wk2, wk3, wv, gq, gk, perm):
    wq, wk = (wq0, wq1, wq2, wq3), (wk0, wk1, wk2, wk3)
    nq = _rms(c_q, q_norm)
    nkv = _rms(c_kv, kv_norm)

    def norm_rope(t, g):
        t = t * lax.rsqrt(jnp.sum(t * t, axis=-1, keepdims=True) * (1.0 / MLA_DQK) + EPS) * g
        return t * c_tab + _hdot(t, perm) * s_tab

    qs = [norm_rope(_bmm(nq, wq[h]), gq) * (MLA_DQK ** -0.5) for h in range(N_HEADS)]
    ks = [norm_rope(_bmm(nkv, wk[h]) + krope, gk) for h in range(N_HEADS)]
    return (*qs, *ks, _bmm(nkv, wv))


def _f_attn(qs, ks, v, q0):
    tq, S = qs[0].shape[0], ks[0].shape[0]
    qpos = q0 + lax.broadcasted_iota(jnp.int32, (tq, S), 0)
    kpos = lax.broadcasted_iota(jnp.int32, (tq, S), 1)
    keep = kpos <= qpos
    logits = [jnp.where(keep, _bmm_nt(qs[h], ks[h]), NEG_INF) for h in range(N_HEADS)]
    ps = [jnp.exp(lg - jnp.max(lg, axis=-1, keepdims=True)) for lg in logits]
    ps = [p / jnp.sum(p, axis=-1, keepdims=True) for p in ps]
    return sum(_bmm(p, v) * _head_mask(h) for h, p in enumerate(ps))


ATTN_PARTS = 4


def _mla_attn_fwd(qs, ks, v, name):
    S = v.shape[0]
    tq = Q_BLOCK
    parts = ATTN_PARTS if S % (ATTN_PARTS * tq) == 0 else 1
    per = S // parts
    outs = []
    for p in range(parts):
        n_keys = (p + 1) * per
        first_block = p * (per // tq)

        def body(*refs, first_block=first_block):
            q_vals = [r[...] for r in refs[:4]]
            k_vals = [r[...] for r in refs[4:8]]
            refs[9][...] = _f_attn(q_vals, k_vals, refs[8][...], (first_block + pl.program_id(0)) * tq)

        qspec = pl.BlockSpec((tq, LANES), lambda i, fb=first_block: (fb + i, 0))
        outs.append(pl.pallas_call(
            body, name=f'{name}_{p}', grid=(per // tq,),
            in_specs=[qspec] * 4 + [pl.BlockSpec((n_keys, LANES), lambda i: (0, 0))] * 4 +
                     [pl.BlockSpec((n_keys, GROUP_W), lambda i: (0, 0))],
            out_specs=pl.BlockSpec((tq, GROUP_W), lambda i: (i, 0)),
            out_shape=jax.ShapeDtypeStruct((per, GROUP_W), f32),
            compiler_params=_cparams(('parallel',), big=True),
        )(*qs, *ks, v))
    return jnp.concatenate(outs, axis=0)


def _mla_attn_bwd(qs, ks, v, do, name):
    S = v.shape[0]
    tq = Q_BLOCK
    parts = ATTN_PARTS if S % (ATTN_PARTS * tq) == 0 else 1
    per = S // parts
    dq_parts, dkv_sum = [], None
    for p in range(parts):
        n_keys = (p + 1) * per
        first_block = p * (per // tq)

        def body(*refs, first_block=first_block):
            q_vals = [r[...].astype(f32) for r in refs[:4]]
            k_vals = [r[...].astype(f32) for r in refs[4:8]]
            v_val = refs[8][...].astype(f32)
            q0 = (first_block + pl.program_id(0)) * tq
            _, vjp = jax.vjp(lambda a, b, c: _f_attn(a, b, c, q0), q_vals, k_vals, v_val)
            dqs, dks, dv = vjp(refs[9][...])
            outs = refs[10:]
            for h in range(N_HEADS):
                outs[h][...] = dqs[h]
            first = pl.program_id(0) == 0
            for o, g in zip(outs[4:], (*dks, dv)):
                @pl.when(first)
                def _(o=o, g=g):
                    o[...] = g

                @pl.when(jnp.logical_not(first))
                def _(o=o, g=g):
                    o[...] += g

        qspec = pl.BlockSpec((tq, LANES), lambda i, fb=first_block: (fb + i, 0))
        kspec = pl.BlockSpec((n_keys, LANES), lambda i: (0, 0))
        vspec = pl.BlockSpec((n_keys, GROUP_W), lambda i: (0, 0))
        res = pl.pallas_call(
            body, name=f'{name}_{p}', grid=(per // tq,),
            in_specs=[qspec] * 4 + [kspec] * 4 + [vspec, pl.BlockSpec((tq, GROUP_W), lambda i, fb=first_block: (fb + i, 0))],
            out_specs=[pl.BlockSpec((tq, LANES), lambda i: (i, 0))] * 4 + [kspec] * 4 + [vspec],
            out_shape=[jax.ShapeDtypeStruct((per, LANES), f32)] * 4 + [jax.ShapeDtypeStruct((n_keys, LANES), f32)] * 4 +
                      [jax.ShapeDtypeStruct((n_keys, GROUP_W), f32)],
            compiler_params=_cparams(('arbitrary',), big=True),
        )(*qs, *ks, v, do)
        dq_parts.append(res[:4])
        dkv = [jnp.pad(t, ((0, S - n_keys), (0, 0))) for t in res[4:]]
        dkv_sum = dkv if dkv_sum is None else [a_ + b_ for a_, b_ in zip(dkv_sum, dkv)]
    dqs = [jnp.concatenate([dq_parts[p][h] for p in range(parts)], axis=0) for h in range(N_HEADS)]
    return dqs, dkv_sum[:4], dkv_sum[4]


def _mla_params(mp):
    pad = LANES - MLA_DQK
    wq = jnp.pad(mp['mla_w_uq'].reshape(GROUP_W, N_HEADS, MLA_DQK).transpose(1, 0, 2), ((0, 0), (0, 0), (0, pad)))
    wkv = mp['mla_w_ukv'].reshape(LANES, N_HEADS, MLA_NOPE + HEAD_DIM)
    wk = jnp.pad(wkv[:, :, :MLA_NOPE].transpose(1, 0, 2), ((0, 0), (0, 0), (0, LANES - MLA_NOPE)))
    wv = wkv[:, :, MLA_NOPE:].reshape(LANES, GROUP_W)
    gq = jnp.pad(mp['mla_qk_q'], (0, pad))[None]
    gk = jnp.pad(mp['mla_qk_k'], (0, pad))[None]
    return [mp['mla_q_norm'][None], mp['mla_kv_norm'][None], *[wq[h] for h in range(N_HEADS)],
            *[wk[h] for h in range(N_HEADS)], wv, gq, gk, _rope_perm()]


def _mla_fwd(c_q, c_kv, k_rope, mp, l):
    S = c_q.shape[0]
    tm = _pick(S, (256, 128))
    krope = jnp.pad(k_rope, ((0, 0), (MLA_NOPE, LANES - MLA_DQK)))
    c_tab, s_tab = _rope_tables(S)
    tiled = [c_q, c_kv, krope, c_tab, s_tab]
    params = _mla_params(mp)
    res = _tile_fwd(_f_mla_pre, tiled, params, [(LANES, bf16)] * 8 + [(GROUP_W, bf16)], tm, f'mla_pre_fwd_{l}')
    qs, ks, v = res[:4], res[4:8], res[8]
    y = _mla_attn_fwd(qs, ks, v, f'mla_attn_fwd_{l}')
    return y, (tiled, params, qs, ks, v)


def _mla_bwd(dy, saved, l):
    tiled, params, qs, ks, v = saved
    S = dy.shape[0]
    tm = _pick(S, (256, 128))
    dqs, dks, dv = _mla_attn_bwd(qs, ks, v, dy, f'mla_attn_bwd_{l}')
    (dc_q, dc_kv, dkrope), dpar = _tile_bwd(_f_mla_pre, tiled, params, [*dqs, *dks, dv], [True, True, True, False, False],
                                            [True] * 13 + [False], tm, f'mla_pre_bwd_{l}')
    dqn, dkvn = dpar[0], dpar[1]
    dwq, dwk = jnp.stack(dpar[2:6]), jnp.stack(dpar[6:10])
    dwv, dgq, dgk = dpar[10:13]
    dw_uq = dwq[:, :, :MLA_DQK].transpose(1, 0, 2).reshape(GROUP_W, N_HEADS * MLA_DQK)
    dw_ukv = jnp.concatenate([dwk[:, :, :MLA_NOPE].transpose(1, 0, 2), dwv.reshape(LANES, N_HEADS, HEAD_DIM)],
                             axis=2).reshape(LANES, N_HEADS * (MLA_NOPE + HEAD_DIM))
    grads = {'mla_q_norm': dqn[0], 'mla_kv_norm': dkvn[0], 'mla_w_uq': dw_uq, 'mla_w_ukv': dw_ukv,
             'mla_qk_q': dgq[0, :MLA_DQK], 'mla_qk_k': dgk[0, :MLA_DQK]}
    return dc_q, dc_kv, dkrope[:, MLA_NOPE:MLA_DQK], grads


SPAN = 128


def _head_mean_matrix():
    h = np.arange(GROUP_W) // HEAD_DIM
    return jnp.asarray((h[:, None] == h[None, :]).astype(np.float32) / HEAD_DIM)


def _f_dil_pre(q, k, gq, gk, hm):
    qn = q * lax.rsqrt(_hdot(q * q, hm) + EPS) * gq * (HEAD_DIM ** -0.5)
    kn = k * lax.rsqrt(_hdot(k * k, hm) + EPS) * gk
    return qn, kn


def _f_dil_branch(qb, kp, kc, vp, vc, b0, b1, b2, b3, first):
    kcat = jnp.concatenate([kp, kc], axis=0)
    vcat = jnp.concatenate([vp, vc], axis=0)
    qi = lax.broadcasted_iota(jnp.int32, (SPAN, 2 * SPAN), 0) + SPAN
    kj = lax.broadcasted_iota(jnp.int32, (SPAN, 2 * SPAN), 1)
    delta = qi - kj
    valid = (delta >= 0) & (delta <= SPAN) & jnp.logical_not(first & (kj < SPAN))
    masks = [_head_mask(h) for h in range(N_HEADS)]
    raw = [_bmm_nt(qb * hm, kcat) for hm in masks]
    logits = [jnp.where(valid, r + bias, NEG_INF) for r, bias in zip(raw, (b0, b1, b2, b3))]
    ms = [jnp.max(lg, axis=-1, keepdims=True) for lg in logits]
    ps = [jnp.exp(lg - m) for lg, m in zip(logits, ms)]
    pvs = [_bmm(p, vcat) for p in ps]
    o = sum(pv * hm for pv, hm in zip(pvs, masks))
    m_full = sum(m * hm for m, hm in zip(ms, masks))
    l_full = sum(jnp.sum(p, axis=-1, keepdims=True) * hm for p, hm in zip(ps, masks))
    return o, m_full, l_full


def _dil_branch_specs(d, nb):
    cur = pl.BlockSpec((SPAN, GROUP_W), lambda r, n: (n, r))
    prev = pl.BlockSpec((SPAN, GROUP_W), lambda r, n: (jnp.maximum(n - 1, 0), r))
    bias = pl.BlockSpec((1, SPAN, 2 * SPAN), lambda r, n: (0, 0, 0))
    return cur, prev, bias


def _head_table_specs():
    return [pl.BlockSpec((1, SPAN, 2 * SPAN), lambda r, n, h=h: (h, 0, 0)) for h in range(N_HEADS)]


def _dil_branch_fwd(q, k, v, table, name):
    L, d = q.shape[0], q.shape[1] // GROUP_W
    nb = L // SPAN
    cur, prev, bias = _dil_branch_specs(d, nb)

    def body(q_ref, kp_ref, kc_ref, vp_ref, vc_ref, b0, b1, b2, b3, o_ref, m_ref, l_ref):
        o, m, l = _f_dil_branch(*[r[...].astype(f32) for r in (q_ref, kp_ref, kc_ref, vp_ref, vc_ref)], b0[0], b1[0], b2[0], b3[0],
                                pl.program_id(1) == 0)
        o_ref[...] = o
        m_ref[...] = m
        l_ref[...] = l

    return pl.pallas_call(
        body, name=name, grid=(d, nb), in_specs=[cur, prev, cur, prev, cur] + _head_table_specs(),
        out_specs=[cur] * 3, out_shape=[jax.ShapeDtypeStruct(q.shape, f32)] * 3,
        compiler_params=_cparams(('parallel', 'parallel')),
    )(q, k, k, v, v, *[table] * N_HEADS)


def _dil_branch_bwd(q, k, v, table, do, dm, dl, name):
    L, d = q.shape[0], q.shape[1] // GROUP_W
    nb = L // SPAN
    cur, prev, bias = _dil_branch_specs(d, nb)
    whole = pl.BlockSpec((L, GROUP_W), lambda r, n: (0, r))

    def body(q_ref, kp_ref, kc_ref, vp_ref, vc_ref, b0, b1, b2, b3, do_ref, dm_ref, dl_ref,
             dq_ref, dk_ref, dv_ref, db0, db1, db2, db3):
        r, n = pl.program_id(0), pl.program_id(1)
        first = n == 0
        _, vjp = jax.vjp(lambda *a: _f_dil_branch(*a, first), *[r[...].astype(f32) for r in (q_ref, kp_ref, kc_ref, vp_ref, vc_ref)],
                         b0[0], b1[0], b2[0], b3[0])
        dq, dkp, dkc, dvp, dvc, g0, g1, g2, g3 = vjp((do_ref[...], dm_ref[...], dl_ref[...]))
        dq_ref[...] = dq

        @pl.when(first)
        def _():
            dk_ref[...] = jnp.zeros_like(dk_ref)
            dv_ref[...] = jnp.zeros_like(dv_ref)

        rows = pl.ds(pl.multiple_of(n * SPAN, SPAN), SPAN)
        dk_ref[rows, :] += dkc
        dv_ref[rows, :] += dvc

        @pl.when(n > 0)
        def _():
            before = pl.ds(pl.multiple_of((n - 1) * SPAN, SPAN), SPAN)
            dk_ref[before, :] += dkp
            dv_ref[before, :] += dvp

        start = first & (r == 0)
        for o, g in zip((db0, db1, db2, db3), (g0, g1, g2, g3)):
            @pl.when(start)
            def _(o=o, g=g):
                o[0] = g

            @pl.when(jnp.logical_not(start))
            def _(o=o, g=g):
                o[0] += g

    res = pl.pallas_call(
        body, name=name, grid=(d, nb), in_specs=[cur, prev, cur, prev, cur] + _head_table_specs() + [cur] * 3,
        out_specs=[cur, whole, whole] + [bias] * 4,
        out_shape=[jax.ShapeDtypeStruct(q.shape, f32)] * 3 + [jax.ShapeDtypeStruct((1, SPAN, 2 * SPAN), f32)] * 4,
        compiler_params=_cparams(('arbitrary', 'arbitrary')),
    )(q, k, k, v, v, *[table] * N_HEADS, do, dm, dl)
    return res[0], res[1], res[2], res[3:]


def _f_dil_merge(o1, m1, l1, o2, m2, l2, o3, m3, l3):
    mx = jnp.maximum(jnp.maximum(m1, m2), m3)
    w1, w2, w3 = jnp.exp(m1 - mx), jnp.exp(m2 - mx), jnp.exp(m3 - mx)
    return ((w1 * o1 + w2 * o2 + w3 * o3) / (w1 * l1 + w2 * l2 + w3 * l3),)


def _bias_onehot(dilation):
    qi = jnp.arange(SPAN, dtype=jnp.int32)[:, None] + SPAN
    kj = jnp.arange(2 * SPAN, dtype=jnp.int32)[None, :]
    bucket = _t5_bucket(jnp.clip(qi - kj, 0, SPAN) * dilation).reshape(-1)
    return (bucket[None, :] == jnp.arange(T5_BUCKETS, dtype=jnp.int32)[:, None]).astype(f32)


def _bias_tables(t5_t, onehot, name):
    N = onehot.shape[1]
    tn = _pick(N, (4096, 2048, 1024))

    def body(t_ref, oh_ref, o_ref):
        o_ref[...] = _hdot(t_ref[...], oh_ref[...])

    return pl.pallas_call(
        body, name=name, grid=(N // tn,),
        in_specs=[pl.BlockSpec((8, T5_BUCKETS), lambda i: (0, 0)), pl.BlockSpec((T5_BUCKETS, tn), lambda i: (0, i))],
        out_specs=pl.BlockSpec((8, tn), lambda i: (0, i)), out_shape=jax.ShapeDtypeStruct((8, N), f32),
        compiler_params=_cparams(('parallel',)),
    )(t5_t, onehot)


def _bias_tables_bwd(d_tab, onehot, name):
    N = onehot.shape[1]
    tn = _pick(N, (4096, 2048, 1024))

    def body(g_ref, oh_ref, o_ref):
        part = _hdot_nt(g_ref[...], oh_ref[...])

        @pl.when(pl.program_id(0) == 0)
        def _():
            o_ref[...] = part

        @pl.when(pl.program_id(0) > 0)
        def _():
            o_ref[...] += part

    return pl.pallas_call(
        body, name=name, grid=(N // tn,),
        in_specs=[pl.BlockSpec((8, tn), lambda i: (0, i)), pl.BlockSpec((T5_BUCKETS, tn), lambda i: (0, i))],
        out_specs=pl.BlockSpec((8, T5_BUCKETS), lambda i: (0, 0)), out_shape=jax.ShapeDtypeStruct((8, T5_BUCKETS), f32),
        compiler_params=_cparams(('arbitrary',)),
    )(d_tab, onehot)


def _by_residue(t, d):
    S, C = t.shape
    return t.reshape(S // d, d * C)


def _from_residue(t):
    return t.reshape(-1, GROUP_W)


def _dil_tables(t5_bias):
    t5_t = jnp.pad(t5_bias.T, ((0, 8 - N_HEADS), (0, 0)))
    return [_bias_tables(t5_t, _bias_onehot(d), f'dil_bias_fwd_{bi}').reshape(8, SPAN, 2 * SPAN)
            for bi, (_, d) in enumerate(DIL_PAIRS)]


def _t5_grad(d_tables):
    total = None
    for bi, (_, d) in enumerate(DIL_PAIRS):
        g = _bias_tables_bwd(d_tables[bi], _bias_onehot(d), f'dil_bias_bwd_{bi}')
        total = g if total is None else total + g
    return total[:N_HEADS].T


def _dil_fwd(qkv, mp, l):
    S = qkv.shape[0]
    tm = _pick(S, (256, 128))
    q, k, v = qkv[:, :GROUP_W], qkv[:, GROUP_W:2 * GROUP_W], qkv[:, 2 * GROUP_W:]
    pre_params = [jnp.tile(mp['dil_q_norm'], N_HEADS)[None], jnp.tile(mp['dil_k_norm'], N_HEADS)[None], _head_mean_matrix()]
    qn, kn = _tile_fwd(_f_dil_pre, [q, k], pre_params, [(GROUP_W, bf16)] * 2, tm, f'dil_pre_fwd_{l}')
    v = v.astype(bf16)
    tables = mp['dil_tables'] if 'dil_tables' in mp else _dil_tables(mp['t5_bias'])
    branches, outs = [], []
    for bi, (_, d) in enumerate(DIL_PAIRS):
        tab = tables[bi]
        qd, kd, vd = _by_residue(qn, d), _by_residue(kn, d), _by_residue(v, d)
        o, m, lsum = _dil_branch_fwd(qd, kd, vd, tab, f'dil_branch_fwd_{l}_{bi}')
        branches.append((qd, kd, vd, tab))
        outs += [_from_residue(o), _from_residue(m), _from_residue(lsum)]
    (y,) = _tile_fwd(_f_dil_merge, outs, [], [(GROUP_W, f32)], tm, f'dil_merge_fwd_{l}')
    return y, (q, k, pre_params, branches, outs)


def _dil_bwd(dy, saved, l):
    q, k, pre_params, branches, outs = saved
    S = dy.shape[0]
    tm = _pick(S, (256, 128))
    douts, _ = _tile_bwd(_f_dil_merge, outs, [], [dy], [True] * 9, [], tm, f'dil_merge_bwd_{l}')
    dqn = dkn = dv = None
    d_tabs = []
    for bi, (_, d) in enumerate(DIL_PAIRS):
        qd, kd, vd, tab = branches[bi]
        do, dm, dl = [_by_residue(t, d) for t in douts[3 * bi:3 * bi + 3]]
        dq_b, dk_b, dv_b, dbias = _dil_branch_bwd(qd, kd, vd, tab, do, dm, dl, f'dil_branch_bwd_{l}_{bi}')
        d_tabs.append(jnp.concatenate([*dbias, jnp.zeros((8 - N_HEADS, SPAN, 2 * SPAN), f32)], axis=0).reshape(8, -1))
        dq_b, dk_b, dv_b = _from_residue(dq_b), _from_residue(dk_b), _from_residue(dv_b)
        dqn = dq_b if dqn is None else dqn + dq_b
        dkn = dk_b if dkn is None else dkn + dk_b
        dv = dv_b if dv is None else dv + dv_b
    (dq, dk), (dgq, dgk) = _tile_bwd(_f_dil_pre, [q, k], pre_params, [dqn, dkn], [True, True], [True, True, False], tm,
                                     f'dil_pre_bwd_{l}')
    grads = {'dil_q_norm': dgq.reshape(N_HEADS, HEAD_DIM).sum(0), 'dil_k_norm': dgk.reshape(N_HEADS, HEAD_DIM).sum(0),
             't5_tables': d_tabs}
    return jnp.concatenate([dq, dk, dv], axis=1), grads


S5_LANES = S5_G * S5_P
SCAN_SEGMENTS = 8
SCAN_W = 256


def _f_s5_prep(bre, bim, lr, li, logdt_col, expand):
    dt = jnp.sum(jnp.exp(logdt_col) * expand, axis=0, keepdims=True)
    mag = jnp.exp(lr * dt)
    ar, ai = mag * jnp.cos(li * dt), mag * jnp.sin(li * dt)
    den = lr * lr + li * li
    nr, ni = ar - 1.0, ai
    zr = (nr * lr + ni * li) / den
    zi = (ni * lr - nr * li) / den
    bb = jnp.concatenate([zr * bre - zi * bim, zr * bim + zi * bre], axis=1)
    a_rows = jnp.broadcast_to(jnp.concatenate([ar, ai], axis=1), bb.shape)
    return bb, a_rows


def _s5_scan(x, a_rows, name, reverse=False, h=None):
    S = x.shape[0]
    NL = x.shape[1] // 2
    T = S // SCAN_SEGMENTS
    nblk = NL // SCAN_W
    n_in = 4 if reverse else 2

    def body(*refs):
        if reverse:
            (x_hbm, pr_hbm, pi_hbm, ar_ref, ai_ref, hr_hbm, hi_hbm, dar_ref, dai_ref,
             xr_s, xi_s, pr_s, pi_s, hr_s, hi_s, in_sems, out_sems) = refs
        else:
            x_hbm, ar_ref, ai_ref, hr_hbm, hi_hbm, xr_s, xi_s, hr_s, hi_s, in_sems, out_sems = refs
        col = pl.multiple_of(pl.program_id(0) * SCAN_W, SCAN_W)
        loads = []
        for k in range(SCAN_SEGMENTS):
            rows = pl.ds(k * T, T)
            sources = [(x_hbm, col, xr_s), (x_hbm, NL + col, xi_s)]
            if reverse:
                sources += [(pr_hbm, col, pr_s), (pi_hbm, col, pi_s)]
            for i, (src, c0, dst) in enumerate(sources):
                loads.append(pltpu.make_async_copy(src.at[rows, pl.ds(c0, SCAN_W)], dst.at[:, k, :],
                                                   in_sems.at[i * SCAN_SEGMENTS + k]))
        for cp in loads:
            cp.start()
        for cp in loads:
            cp.wait()
        ar = ar_ref[...]
        ai = -ai_ref[...] if reverse else ai_ref[...]
        zero = jnp.zeros((SCAN_SEGMENTS, SCAN_W), f32)

        def at(s):
            return T - 1 - s if reverse else s

        def local(s, c):
            hr, hi, pr, pi = c
            j = at(s)
            nhr = ar * hr - ai * hi + xr_s[j]
            nhi = ar * hi + ai * hr + xi_s[j]
            hr_s[j] = nhr
            hi_s[j] = nhi
            return nhr, nhi, ar * pr - ai * pi, ar * pi + ai * pr

        er, ei, pr, pi = lax.fori_loop(0, T, local, (zero, zero, zero + 1.0, zero), unroll=2)
        row = lax.broadcasted_iota(jnp.int32, (SCAN_SEGMENTS, SCAN_W), 0)
        cr, ci = zero, zero
        order = range(SCAN_SEGMENTS - 2, -1, -1) if reverse else range(1, SCAN_SEGMENTS)
        for k in order:
            src = k + 1 if reverse else k - 1
            tr = er + pr * cr - pi * ci
            ti = ei + pr * ci + pi * cr
            cr = jnp.where(row == k, jnp.sum(jnp.where(row == src, tr, 0.0), axis=0, keepdims=True), cr)
            ci = jnp.where(row == k, jnp.sum(jnp.where(row == src, ti, 0.0), axis=0, keepdims=True), ci)

        def fix_at(j, c, before):
            pr, pi, sr, si = c
            pr, pi = ar * pr - ai * pi, ar * pi + ai * pr
            hr = hr_s[j] + pr * cr - pi * ci
            hi = hi_s[j] + pr * ci + pi * cr
            hr_s[j] = hr
            hi_s[j] = hi
            if reverse:
                qr, qi = before
                sr = sr + hr * qr + hi * qi
                si = si + hi * qr - hr * qi
            return pr, pi, sr, si

        start = (zero + 1.0, zero, zero, zero)
        if reverse:
            def fix(s, c):
                j = T - 1 - s
                return fix_at(j, c, (pr_s[j - 1], pi_s[j - 1]))

            c = lax.fori_loop(0, T - 1, fix, start, unroll=2)
            last_r = jnp.where(row == 0, 0.0, pltpu.roll(pr_s[T - 1], 1, 0))
            last_i = jnp.where(row == 0, 0.0, pltpu.roll(pi_s[T - 1], 1, 0))
            _, _, sr, si = fix_at(0, c, (last_r, last_i))
            dar_ref[...] = sr
            dai_ref[...] = si
        else:
            lax.fori_loop(0, T, lambda s, c: fix_at(s, c, None), start, unroll=2)
        stores = []
        for k in range(SCAN_SEGMENTS):
            rows = pl.ds(k * T, T)
            stores.append(pltpu.make_async_copy(hr_s.at[:, k, :], hr_hbm.at[rows, pl.ds(col, SCAN_W)], out_sems.at[k]))
            stores.append(pltpu.make_async_copy(hi_s.at[:, k, :], hi_hbm.at[rows, pl.ds(col, SCAN_W)],
                                                out_sems.at[SCAN_SEGMENTS + k]))
        for cp in stores:
            cp.start()
        for cp in stores:
            cp.wait()

    a_re = pl.BlockSpec((SCAN_SEGMENTS, SCAN_W), lambda b: (0, b))
    a_im = pl.BlockSpec((SCAN_SEGMENTS, SCAN_W), lambda b: (0, nblk + b))
    seq = pltpu.VMEM((T, SCAN_SEGMENTS, SCAN_W), f32)
    if reverse:
        in_specs, args = [ANY, ANY, ANY, a_re, a_im], [x, h[0], h[1], a_rows, a_rows]
        out_specs = [ANY, ANY, a_re, a_re]
        out_shape = [jax.ShapeDtypeStruct((S, NL), f32)] * 2 + [jax.ShapeDtypeStruct((SCAN_SEGMENTS, NL), f32)] * 2
    else:
        in_specs, args = [ANY, a_re, a_im], [x, a_rows, a_rows]
        out_specs = [ANY, ANY]
        out_shape = [jax.ShapeDtypeStruct((S, NL), f32)] * 2
    scratch = [seq] * (n_in + 2) + [pltpu.SemaphoreType.DMA((n_in * SCAN_SEGMENTS,)),
                                    pltpu.SemaphoreType.DMA((2 * SCAN_SEGMENTS,))]
    return pl.pallas_call(body, name=name, grid=(nblk,), in_specs=in_specs, out_specs=out_specs, out_shape=out_shape,
                          scratch_shapes=scratch, compiler_params=_cparams(('arbitrary',), big=True))(*args)


def _f_s5_post(y, u, d, w_glu):
    z = _bmm(y + d * u, w_glu)
    return (z[:, :GROUP_W] * jax.nn.sigmoid(z[:, GROUP_W:]),)


def _block_diag(t):
    G, a, b = t.shape
    eye = jnp.eye(G, dtype=t.dtype)
    return (t[:, :, None, :] * eye[:, None, :, None]).reshape(G * a, G * b)


def _diag_blocks(m, a, b):
    G = m.shape[0] // a
    return jnp.moveaxis(jnp.diagonal(m.reshape(G, a, G, b), axis1=0, axis2=2), -1, 0)


def _s5_fwd(u, mp, l):
    S = u.shape[0]
    tm = _pick(S, (256, 128))
    bre = _block_diag(mp['s5_b_re'].transpose(0, 2, 1))
    bim = _block_diag(mp['s5_b_im'].transpose(0, 2, 1))
    expand = jnp.repeat(jnp.eye(S5_G, dtype=f32), S5_P, axis=1)
    prep_params = [mp['s5_lambda_re'].reshape(1, S5_LANES), mp['s5_lambda_im'].reshape(1, S5_LANES),
                   mp['s5_log_dt'].reshape(S5_G, 1), expand]
    bb, a_rows = _tile_fwd(_f_s5_prep, [bre, bim], prep_params, [(2 * S5_LANES, f32)] * 2, GROUP_W, f's5_prep_fwd_{l}')
    x = _mm(u, bb, 'nn', f's5_in_fwd_{l}')
    hr, hi = _s5_scan(x, a_rows, f's5_scan_fwd_{l}')
    c_re, c_im = _block_diag(mp['s5_c_re'].transpose(0, 2, 1)), -_block_diag(mp['s5_c_im'].transpose(0, 2, 1))
    y = _mm(hi, c_im, 'nn', f's5_out_im_fwd_{l}', add=_mm(hr, c_re, 'nn', f's5_out_re_fwd_{l}'))
    post_params = [mp['s5_d'][None], mp['s5_w_glu']]
    (out,) = _tile_fwd(_f_s5_post, [y, u], post_params, [(GROUP_W, f32)], tm, f's5_post_fwd_{l}')
    return out, (u, bre, bim, prep_params, bb, a_rows, hr, hi, c_re, c_im, y, post_params)


def _s5_bwd(dout, saved, l):
    u, bre, bim, prep_params, bb, a_rows, hr, hi, c_re, c_im, y, post_params = saved
    S = u.shape[0]
    tm = _pick(S, (256, 128))
    (dy, du1), (dd, dwglu) = _tile_bwd(_f_s5_post, [y, u], post_params, [dout], [True, True], [True, True], tm,
                                       f's5_post_bwd_{l}')
    ccat = jnp.concatenate([c_re, c_im], axis=0)
    dh = _mm(dy, ccat, 'nt', f's5_out_dx_{l}')
    dccat = jnp.concatenate([_mm(hr, dy, 'tn', f's5_out_re_dw_{l}'), _mm(hi, dy, 'tn', f's5_out_im_dw_{l}')], axis=0)
    lr_, li_, dar, dai = _s5_scan(dh, a_rows, f's5_scan_bwd_{l}', reverse=True, h=(hr, hi))
    du2 = _mm(li_, bb[:, S5_LANES:], 'nt', f's5_in_im_dx_{l}', add=_mm(lr_, bb[:, :S5_LANES], 'nt', f's5_in_re_dx_{l}'))
    dbb = jnp.concatenate([_mm(u, lr_, 'tn', f's5_in_re_dw_{l}'), _mm(u, li_, 'tn', f's5_in_im_dw_{l}')], axis=1)
    da_rows = jnp.pad(jnp.concatenate([dar, dai], axis=1), ((0, GROUP_W - SCAN_SEGMENTS), (0, 0)))
    (dbre, dbim), (dlr, dli, dlogdt) = _tile_bwd(_f_s5_prep, [bre, bim], prep_params, [dbb, da_rows], [True, True],
                                                 [True, True, True, False], GROUP_W, f's5_prep_bwd_{l}')
    grads = {
        's5_lambda_re': dlr.reshape(S5_G, S5_P), 's5_lambda_im': dli.reshape(S5_G, S5_P), 's5_log_dt': dlogdt[:, 0],
        's5_b_re': _diag_blocks(dbre, S5_CG, S5_P).transpose(0, 2, 1),
        's5_b_im': _diag_blocks(dbim, S5_CG, S5_P).transpose(0, 2, 1),
        's5_c_re': _diag_blocks(dccat[:S5_LANES], S5_P, S5_CG).transpose(0, 2, 1),
        's5_c_im': -_diag_blocks(dccat[S5_LANES:], S5_P, S5_CG).transpose(0, 2, 1),
        's5_d': dd[0], 's5_w_glu': dwglu}
    return du1 + du2, grads


DN_CONV = 4


def _head_sum_matrix():
    h = np.arange(GROUP_W) // HEAD_DIM
    return jnp.asarray((h[:, None] == h[None, :]).astype(np.float32))


def _f_dn_pre(x0, x1, x2, x3, ab, w0, w1, w2, w3, alog, dtb, ea, eb, hs):
    c = w0 * x0 + w1 * x1 + w2 * x2 + w3 * x3
    s = c * jax.nn.sigmoid(c)
    q, k, v = s[:, :GROUP_W], s[:, GROUP_W:2 * GROUP_W], s[:, 2 * GROUP_W:]
    q = q * lax.rsqrt(_hdot(q * q, hs) + EPS) * (HEAD_DIM ** -0.5)
    k = k * lax.rsqrt(_hdot(k * k, hs) + EPS)
    beta = jax.nn.sigmoid(_hdot(ab, eb))
    g = -jnp.exp(alog) * jax.nn.softplus(_hdot(ab, ea) + dtb)
    return q, k, v, g, beta


DN_CHUNKS_PER_STEP = 4


def _f_dn_chunks(q, k, v, g, beta):
    C = DN_CHUNK
    n_chunks = q.shape[0] // C
    r = lax.broadcasted_iota(jnp.int32, (C, C), 0)
    c = lax.broadcasted_iota(jnp.int32, (C, C), 1)
    causal, strict = r >= c, r > c
    eye = (r == c).astype(f32)
    tril = causal.astype(f32)
    ones = jnp.ones((C, GROUP_W), f32)
    masks = [_head_mask(h) for h in range(N_HEADS)]
    rows = [tuple(t[i * C:(i + 1) * C] for t in (q, k, v, g, beta)) for i in range(n_chunks)]
    gcs = [_hdot(tril, gi) for (_, _, _, gi, _) in rows]
    items = [(i, h) for i in range(n_chunks) for h in range(N_HEADS)]
    grows = [_hdot_nt(ones * (masks[h] * (1.0 / HEAD_DIM)), gcs[i]) for i, h in items]
    decs = []
    for (i, h), grow in zip(items, grows):
        gcol = jnp.sum(gcs[i] * masks[h], axis=1, keepdims=True) * (1.0 / HEAD_DIM)
        decs.append(jnp.exp(jnp.where(causal, gcol - grow, NEG_INF)))
    kbs = [ki * bi for (_, ki, _, _, bi) in rows]
    kks = [_bmm_nt(kbs[i] * masks[h], rows[i][1]) for i, h in items]
    qks = [_bmm_nt(rows[i][0] * masks[h], rows[i][1]) for i, h in items]
    lmats = [jnp.where(strict, kk * dec, 0.0) for kk, dec in zip(kks, decs)]
    a_qk = [jnp.where(causal, qk * dec, 0.0) for qk, dec in zip(qks, decs)]
    ts = [eye - lm for lm in lmats]
    ps = lmats
    for _ in range(5):
        ps = [_bmm(p, p) for p in ps]
        ts = [t + _bmm(t, p) for t, p in zip(ts, ps)]
    egs = [jnp.exp(gc) for gc in gcs]
    tw = [_bmm(t, kbs[i] * egs[i]) for (i, h), t in zip(items, ts)]
    tu = [_bmm(t, rows[i][2] * rows[i][4]) for (i, h), t in zip(items, ts)]
    outs = []
    for i in range(n_chunks):
        qi, ki, _, gi, _ = rows[i]
        glast = jnp.sum(gi, axis=0, keepdims=True)
        w = sum(tw[i * N_HEADS + h] * masks[h] for h in range(N_HEADS))
        u = sum(tu[i * N_HEADS + h] * masks[h] for h in range(N_HEADS))
        outs.append((w, u, qi * egs[i], ki * jnp.exp(glast - gcs[i]), *a_qk[i * N_HEADS:(i + 1) * N_HEADS],
                     jnp.broadcast_to(jnp.exp(glast), (C, GROUP_W))))
    return tuple(jnp.concatenate(parts, axis=0) for parts in zip(*outs))


def _f_dn_step(w, u, qd, kdec, a0, a1, a2, a3, dfull, state, bd):
    row0 = (lax.broadcasted_iota(jnp.int32, dfull.shape, 0) == 0).astype(f32)
    dvec = jnp.sum(dfull * row0, axis=0, keepdims=True)
    ws, qs = _bmm(w, state), _bmm(qd, state)
    vnew = u - ws
    avs = [_bmm(a, vnew) for a in (a0, a1, a2, a3)]
    kv = _bmm_tn(kdec, vnew)
    o = qs + sum(av * _head_mask(h) for h, av in enumerate(avs))
    return o, state * dvec + bd * kv


def _dn_scan_fwd(ins, name):
    S = ins[0].shape[0]
    N = S // DN_CHUNK
    bd = _head_sum_matrix()

    def body(*refs):
        o_ref, s_ref, state = refs[10], refs[11], refs[12]

        @pl.when(pl.program_id(0) == 0)
        def _():
            state[...] = jnp.zeros_like(state)

        s_in = state[...]
        s_ref[0] = s_in
        o, s_out = _f_dn_step(*[r[...] for r in refs[:9]], s_in, refs[9][...])
        o_ref[...] = o
        state[...] = s_out

    return pl.pallas_call(
        body, name=name, grid=(N,),
        in_specs=[pl.BlockSpec((DN_CHUNK, t.shape[1]), lambda n: (n, 0)) for t in ins] + [_full_spec(bd)],
        out_specs=[pl.BlockSpec((DN_CHUNK, GROUP_W), lambda n: (n, 0)), pl.BlockSpec((1, GROUP_W, GROUP_W), lambda n: (n, 0, 0))],
        out_shape=[jax.ShapeDtypeStruct((S, GROUP_W), f32), jax.ShapeDtypeStruct((N, GROUP_W, GROUP_W), f32)],
        scratch_shapes=[pltpu.VMEM((GROUP_W, GROUP_W), f32)],
        compiler_params=_cparams(('arbitrary',)),
    )(*ins, bd)


def _dn_scan_bwd(ins, states, do, name):
    S = ins[0].shape[0]
    N = S // DN_CHUNK
    bd = _head_sum_matrix()

    def body(*refs):
        s_ref, do_ref = refs[9], refs[10]
        bd_ref = refs[11]
        outs = refs[12:21]
        dstate = refs[21]

        @pl.when(pl.program_id(0) == 0)
        def _():
            dstate[...] = jnp.zeros_like(dstate)

        bd_val = bd_ref[...]
        _, vjp = jax.vjp(lambda *a: _f_dn_step(*a, bd_val), *[r[...] for r in refs[:9]], s_ref[0])
        grads = vjp((do_ref[...], dstate[...]))
        for o, g in zip(outs, grads[:9]):
            o[...] = g
        dstate[...] = grads[9]

    def rev(n):
        return (N - 1 - n, 0)

    res = pl.pallas_call(
        body, name=name, grid=(N,),
        in_specs=[pl.BlockSpec((DN_CHUNK, t.shape[1]), rev) for t in ins] +
                 [pl.BlockSpec((1, GROUP_W, GROUP_W), lambda n: (N - 1 - n, 0, 0)), pl.BlockSpec((DN_CHUNK, GROUP_W), rev),
                  _full_spec(bd)],
        out_specs=[pl.BlockSpec((DN_CHUNK, t.shape[1]), rev) for t in ins],
        out_shape=[jax.ShapeDtypeStruct(t.shape, f32) for t in ins],
        scratch_shapes=[pltpu.VMEM((GROUP_W, GROUP_W), f32)],
        compiler_params=_cparams(('arbitrary',)),
    )(*ins, states, do, bd)
    return list(res)


def _f_dn_post(o, gate, gain, hmean):
    return (o * lax.rsqrt(_hdot(o * o, hmean) + EPS) * gain * (gate * jax.nn.sigmoid(gate)),)


def _dn_delays(x, name):
    S, C = x.shape
    tm = _pick(S, (256, 128))

    def body(prev_ref, cur_ref, *outs):
        before = jnp.where(pl.program_id(0) > 0, prev_ref[...], 0.0)
        both = jnp.concatenate([before, cur_ref[...]], axis=0)
        for o, k in zip(outs, range(DN_CONV - 1, 0, -1)):
            o[...] = pltpu.roll(both, k, 0)[tm:]

    spec = pl.BlockSpec((tm, C), lambda i: (i, 0))
    return pl.pallas_call(
        body, name=name, grid=(S // tm,),
        in_specs=[pl.BlockSpec((tm, C), lambda i: (jnp.maximum(i - 1, 0), 0)), spec],
        out_specs=[spec] * (DN_CONV - 1), out_shape=[jax.ShapeDtypeStruct((S, C), x.dtype)] * (DN_CONV - 1),
        compiler_params=_cparams(('parallel',), big=True),
    )(x, x)


def _dn_undelay_sum(ds, name):
    S, C = ds[0].shape
    tm = _pick(S, (256, 128))
    n = S // tm

    def body(*refs):
        o = refs[-1]
        total = refs[2 * (DN_CONV - 1)][...]
        for j in range(DN_CONV - 1):
            k = DN_CONV - 1 - j
            after = jnp.where(pl.program_id(0) < n - 1, refs[2 * j + 1][...], 0.0)
            both = jnp.concatenate([refs[2 * j][...], after], axis=0)
            total = total + pltpu.roll(both, 2 * tm - k, 0)[:tm]
        o[...] = total

    spec = pl.BlockSpec((tm, C), lambda i: (i, 0))
    nxt = pl.BlockSpec((tm, C), lambda i: (jnp.minimum(i + 1, n - 1), 0))
    args, in_specs = [], []
    for j in range(DN_CONV - 1):
        args += [ds[j], ds[j]]
        in_specs += [spec, nxt]
    return pl.pallas_call(
        body, name=name, grid=(n,), in_specs=in_specs + [spec], out_specs=spec,
        out_shape=jax.ShapeDtypeStruct((S, C), f32), compiler_params=_cparams(('parallel',), big=True),
    )(*args, ds[DN_CONV - 1])


def _dn_fwd(qkv, a, b, gate, mp, l):
    S = qkv.shape[0]
    tm = _pick(S, (256, 128))
    xs = [*_dn_delays(qkv, f'dn_delay_{l}'), qkv]
    ab = jnp.pad(jnp.concatenate([a, b], axis=1), ((0, 0), (0, LANES - 2 * N_HEADS)))
    sel = np.zeros((2, LANES, GROUP_W), np.float32)
    for h in range(N_HEADS):
        sel[0, h, h * HEAD_DIM:(h + 1) * HEAD_DIM] = 1.0
        sel[1, N_HEADS + h, h * HEAD_DIM:(h + 1) * HEAD_DIM] = 1.0
    pre_params = [*[mp['dn_conv'][j][None] for j in range(DN_CONV)], jnp.repeat(mp['dn_a_log'], HEAD_DIM)[None],
                  jnp.repeat(mp['dn_dt_bias'], HEAD_DIM)[None], jnp.asarray(sel[0]), jnp.asarray(sel[1]), _head_sum_matrix()]
    pre = _tile_fwd(_f_dn_pre, [*xs, ab], pre_params, [(GROUP_W, f32)] * 5, tm, f'dn_pre_fwd_{l}')
    chunk_outs = [(GROUP_W, f32)] * 4 + [(HEAD_DIM, f32)] * 4 + [(GROUP_W, f32)]
    parts = _tile_fwd(_f_dn_chunks, pre, [], chunk_outs, DN_CHUNK * DN_CHUNKS_PER_STEP, f'dn_chunk_fwd_{l}')
    o, states = _dn_scan_fwd(parts, f'dn_scan_fwd_{l}')
    post_params = [jnp.tile(mp['dn_o_norm'], N_HEADS)[None], _head_mean_matrix()]
    (y,) = _tile_fwd(_f_dn_post, [o, gate], post_params, [(GROUP_W, f32)], tm, f'dn_post_fwd_{l}')
    return y, (xs, ab, pre_params, pre, parts, states, o, gate, post_params)


def _dn_bwd(dy, saved, l):
    xs, ab, pre_params, pre, parts, states, o, gate, post_params = saved
    S = dy.shape[0]
    tm = _pick(S, (256, 128))
    (do, dgate), (dgain,) = _tile_bwd(_f_dn_post, [o, gate], post_params, [dy], [True, True], [True, False], tm,
                                      f'dn_post_bwd_{l}')
    dparts = _dn_scan_bwd(parts, states, do, f'dn_scan_bwd_{l}')
    dpre, _ = _tile_bwd(_f_dn_chunks, pre, [], dparts, [True] * 5, [], DN_CHUNK * DN_CHUNKS_PER_STEP, f'dn_chunk_bwd_{l}')
    dins, dpar = _tile_bwd(_f_dn_pre, [*xs, ab], pre_params, dpre, [True] * 5, [True] * 6 + [False] * 3, tm,
                           f'dn_pre_bwd_{l}')
    dqkv = _dn_undelay_sum(dins[:DN_CONV], f'dn_undelay_{l}')
    dab = dins[DN_CONV]
    grads = {'dn_conv': jnp.concatenate(dpar[:DN_CONV], axis=0),
             'dn_a_log': dpar[4].reshape(N_HEADS, HEAD_DIM).sum(1), 'dn_dt_bias': dpar[5].reshape(N_HEADS, HEAD_DIM).sum(1),
             'dn_o_norm': dgain.reshape(N_HEADS, HEAD_DIM).sum(0)}
    return dqkv, dab[:, :N_HEADS], dab[:, N_HEADS:2 * N_HEADS], dgate, grads


def _t5_bucket(dist):
    exact = T5_BUCKETS // 2
    df = jnp.maximum(dist, 1).astype(f32)
    large = exact + (jnp.log(df / exact) / math.log(T5_MAX_DIST / exact) * (T5_BUCKETS - exact)).astype(jnp.int32)
    large = jnp.minimum(large, T5_BUCKETS - 1)
    return jnp.where(dist < exact, dist, large)


def _split_cols(t, sizes):
    out, start = [], 0
    for s in sizes:
        out.append(t[..., start:start + s])
        start += s
    return out


def _mixers_fwd(proj, mp, l):
    c_q, c_kv, k_rope, u_s5, qkv_dil, qkv_dn, a_dn, b_dn, gate_dn = _split_cols(proj, IN_SPLITS)
    y_mla, s_mla = _mla_fwd(c_q, c_kv, k_rope, mp, l)
    y_s5, s_s5 = _s5_fwd(u_s5, mp, l)
    y_dil, s_dil = _dil_fwd(qkv_dil, mp, l)
    y_dn, s_dn = _dn_fwd(qkv_dn, a_dn, b_dn, gate_dn, mp, l)
    return jnp.concatenate([y_mla, y_s5, y_dil, y_dn], axis=-1), (s_mla, s_s5, s_dil, s_dn)


def _mixers_bwd(dmixed, saved, l):
    s_mla, s_s5, s_dil, s_dn = saved
    d_mla, d_s5, d_dil, d_dn = _split_cols(dmixed, (GROUP_W,) * 4)
    dc_q, dc_kv, dk_rope, g_mla = _mla_bwd(d_mla, s_mla, l)
    du, g_s5 = _s5_bwd(d_s5, s_s5, l)
    dqkv_dil, g_dil = _dil_bwd(d_dil, s_dil, l)
    dqkv_dn, da, db, dgate, g_dn = _dn_bwd(d_dn, s_dn, l)
    parts = [dc_q, dc_kv, dk_rope, du, dqkv_dil, dqkv_dn, da, db, dgate]
    dproj = jnp.concatenate([p.astype(bf16) for p in parts], axis=-1)
    return dproj, {**g_mla, **g_s5, **g_dil, **g_dn}


MIXER_PARAMS = ['mla_q_norm', 'mla_kv_norm', 'mla_w_uq', 'mla_w_ukv', 'mla_qk_q', 'mla_qk_k', 's5_lambda_re',
                's5_lambda_im', 's5_log_dt', 's5_b_re', 's5_b_im', 's5_c_re', 's5_c_im', 's5_d', 's5_w_glu',
                'dil_q_norm', 'dil_k_norm', 't5_bias', 'dn_conv', 'dn_a_log', 'dn_dt_bias', 'dn_o_norm']


def _layer_fwd_mix(h, W, l):
    S = h.shape[0]
    tm = _pick(S, (256, 128))
    g1 = W['attn_norm'][l][None]
    (n1,) = _tile_fwd(_f_rms, [h], [g1], [(D_MODEL, bf16)], tm, f'rms1_fwd_{l}')
    proj = _mm(n1, W['w_in'][l], 'nt', f'proj_fwd_{l}')
    mp = {k: (W[k] if k == 't5_bias' else W[k][l]).astype(f32) for k in MIXER_PARAMS}
    if 'dil_tables' in W:
        mp['dil_tables'] = W['dil_tables']
    mixed, mix_saved = _mixers_fwd(proj, mp, l)
    mixed_b = mixed.astype(bf16)
    h2 = _mm(mixed_b, W['w_out'][l], 'nn', f'out_fwd_{l}', add=h)
    return h2, dict(h=h, n1=n1, mix=mix_saved, mixed=mixed_b, h2=h2)


def _layer_fwd_ffn(h2, W, l, saved):
    S = h2.shape[0]
    tm = _pick(S, (256, 128))
    g2 = W['ffn_norm'][l][None]
    (n2,) = _tile_fwd(_f_rms, [h2], [g2], [(D_MODEL, bf16)], tm, f'rms2_fwd_{l}')
    u = _mm(n2, W['ffn_w1'][l], 'nt', f'ffn1_fwd_{l}', out_dtype=bf16)
    v = _mm(n2, W['ffn_w3'][l], 'nt', f'ffn3_fwd_{l}', out_dtype=bf16)
    (act,) = _tile_fwd(_f_swiglu, [u, v], [], [(FFN_HIDDEN, bf16)], tm, f'swiglu_fwd_{l}')
    h3 = _mm(act, W['ffn_w2'][l], 'nn', f'ffn2_fwd_{l}', add=h2)
    saved.update(n2=n2, u=u, v=v, act=act)
    return h3


def _layer_bwd_ffn(dh3, saved, W, l):
    S = dh3.shape[0]
    tm = _pick(S, (256, 128))
    g2 = W['ffn_norm'][l][None]
    grads = {}
    dact = _mm(dh3, W['ffn_w2'][l], 'nt', f'ffn2_dx_{l}', out_dtype=bf16)
    grads['ffn_w2'] = _mm(saved['act'], dh3, 'tn', f'ffn2_dw_{l}', out_dtype=bf16)
    (du, dv), _ = _tile_bwd(_f_swiglu, [saved['u'], saved['v']], [], [dact], [True, True], [], tm, f'swiglu_bwd_{l}',
                            dt_dtypes=[bf16, bf16])
    dn2 = _mm(dv, W['ffn_w3'][l], 'nn', f'ffn3_dx_{l}', add=_mm(du, W['ffn_w1'][l], 'nn', f'ffn1_dx_{l}'))
    grads['ffn_w1'] = _mm(du, saved['n2'], 'tn', f'ffn1_dw_{l}', out_dtype=bf16)
    grads['ffn_w3'] = _mm(dv, saved['n2'], 'tn', f'ffn3_dw_{l}', out_dtype=bf16)
    (dh2n,), (dg2,) = _tile_bwd(_f_rms, [saved['h2']], [g2], [dn2], [True], [True], tm, f'rms2_bwd_{l}')
    grads['ffn_norm'] = dg2[0]
    return (dh3, dh2n), grads


def _layer_bwd_mix(dh2, saved, W, l):
    S = dh2.shape[0]
    tm = _pick(S, (256, 128))
    g1 = W['attn_norm'][l][None]
    grads = {}
    dmixed = _mm(dh2, W['w_out'][l], 'nt', f'out_dx_{l}')
    grads['w_out'] = _mm(saved['mixed'], dh2, 'tn', f'out_dw_{l}', out_dtype=bf16)
    dproj, dmp = _mixers_bwd(dmixed, saved['mix'], l)
    grads.update(dmp)
    dn1 = _mm(dproj, W['w_in'][l], 'nn', f'proj_dx_{l}')
    grads['w_in'] = _mm(dproj, saved['n1'], 'tn', f'proj_dw_{l}', out_dtype=bf16)
    (dh1n,), (dg1,) = _tile_bwd(_f_rms, [saved['h']], [g1], [dn1], [True], [True], tm, f'rms1_bwd_{l}')
    grads['attn_norm'] = dg1[0]
    return (dh2, dh1n), grads


def kernel(x, attn_norm, w_in, w_out, mla_q_norm, mla_kv_norm, mla_w_uq, mla_w_ukv, mla_qk_q, mla_qk_k, s5_lambda_re, s5_lambda_im, s5_log_dt, s5_b_re, s5_b_im, s5_c_re, s5_c_im, s5_d, s5_w_glu, dil_q_norm, dil_k_norm, t5_bias, dn_conv, dn_a_log, dn_dt_bias, dn_o_norm, ffn_norm, ffn_w1, ffn_w3, ffn_w2, loss_target, m_attn_norm, m_w_in, m_w_out, m_mla_q_norm, m_mla_kv_norm, m_mla_w_uq, m_mla_w_ukv, m_mla_qk_q, m_mla_qk_k, m_s5_lambda_re, m_s5_lambda_im, m_s5_log_dt, m_s5_b_re, m_s5_b_im, m_s5_c_re, m_s5_c_im, m_s5_d, m_s5_w_glu, m_dil_q_norm, m_dil_k_norm, m_t5_bias, m_dn_conv, m_dn_a_log, m_dn_dt_bias, m_dn_o_norm, m_ffn_norm, m_ffn_w1, m_ffn_w3, m_ffn_w2, v_attn_norm, v_w_in, v_w_out, v_mla_q_norm, v_mla_kv_norm, v_mla_w_uq, v_mla_w_ukv, v_mla_qk_q, v_mla_qk_k, v_s5_lambda_re, v_s5_lambda_im, v_s5_log_dt, v_s5_b_re, v_s5_b_im, v_s5_c_re, v_s5_c_im, v_s5_d, v_s5_w_glu, v_dil_q_norm, v_dil_k_norm, v_t5_bias, v_dn_conv, v_dn_a_log, v_dn_dt_bias, v_dn_o_norm, v_ffn_norm, v_ffn_w1, v_ffn_w3, v_ffn_w2):
    given = dict(locals())
    def seen(n, t):
        if n in COLUMNS_FIRST:
            return jnp.transpose(t, (2, 0, 1))
        return jnp.swapaxes(t, 1, 2) if n in TRANSPOSED else t

    def given_back(n, t):
        return jnp.transpose(t, (1, 2, 0)) if n in COLUMNS_FIRST else seen(n, t)

    def layer_of(n, t, l):
        return t[:, l] if n in COLUMNS_FIRST else t[l]

    w_loc = {n: seen(n, given[n]) for n in WEIGHTS}
    m_loc = {n: seen(n, given['m_' + n]) for n in WEIGHTS}
    v_loc = {n: seen(n, given['v_' + n]) for n in WEIGHTS}
    big_names = list(BIG)

    own = 2 * lax.axis_index('x') + lax.axis_index('y')
    groups = [[(n, 0) for n in GATHER_FIRST], [(n, 0) for n in GATHER_FFN], [(n, 1) for n in big_names]]
    started, order = [], jnp.zeros((8, LANES), f32)
    for gi, group in enumerate(groups):
        blocks = [layer_of(n, w_loc[n], l).astype(bf16) for n, l in group]
        lands = [lax.empty((N_SHARDS,) + b.shape, bf16) for b in blocks]
        send_sems, recv_sems, blocks, lands, order = _to_chips_start(blocks, lands, False, order, f'gather_start_{gi}')
        started.append((send_sems, recv_sems, blocks, lands))
    W = {n: [None] * DEPTH for n in big_names}
    for n in SMALL:
        W[n] = w_loc[n]
    W['dil_tables'] = _dil_tables(w_loc['t5_bias'])

    def arrive(gi, after):
        send_sems, recv_sems, blocks, lands = started[gi]
        blocks, lands = _to_chips_wait(send_sems, recv_sems, blocks, lands, False, after, f'gather_wait_{gi}')
        for (n, l), block, land in zip(groups[gi], blocks, lands):
            W[n][l] = _from_shards(n, lax.dynamic_update_slice(land, block[None], (own, 0, 0)))

    arrive(0, order)
    h = x[0]
    saved = []
    for l in range(DEPTH):
        h2, sv = _layer_fwd_mix(h, W, l)
        if l == 0:
            arrive(1, h2)
        h = _layer_fwd_ffn(h2, W, l, sv)
        if l == 0:
            arrive(2, h)
        saved.append(sv)
    parts_loss, dh = _loss_head(h, loss_target[0])
    local_loss = jnp.sum(parts_loss)

    layer_grads = [dict() for _ in range(DEPTH)]
    sent = []

    def send(group, tag, also=None):
        srcs = [_by_shard(n, layer_grads[l][n]).astype(bf16) for n, l in group]
        lands = [lax.empty((3,) + s.shape[1:], bf16) for s in srcs]
        modes = [True] * len(srcs)
        if also is not None:
            srcs, lands, modes = srcs + [also], lands + [lax.empty((3,) + also.shape, also.dtype)], modes + ['same']
        send_sems, recv_sems, srcs, lands, token = _to_chips_start(srcs, lands, modes, jnp.zeros((8, LANES), f32),
                                                                   f'reduce_start_{tag}')
        sent.append((group, tag, send_sems, recv_sems, srcs, lands, modes))
        return token

    for l in reversed(range(DEPTH)):
        (dh3, dh2n), g_ffn = _layer_bwd_ffn(dh, saved[l], W, l)
        layer_grads[l].update(g_ffn)
        dh2 = dh3 + dh2n
        if l == 0:
            dh2 = dh2 + send([(n, 0) for n in GATHER_FFN], 'ffn0')[0, 0]
        (dh2, dh1n), g_mix = _layer_bwd_mix(dh2, saved[l], W, l)
        layer_grads[l].update(g_mix)
        dh = dh2 + dh1n
        if l == 1:
            dh = dh + send([(n, 1) for n in big_names], 'layer1')[0, 0]
    grad_x = dh[None]
    small_full = []
    for n in SMALL:
        if n == 't5_bias':
            small_full.append(_t5_grad([a_ + b_ for a_, b_ in zip(layer_grads[0]['t5_tables'], layer_grads[1]['t5_tables'])]))
        else:
            small_full.append(jnp.stack([layer_grads[l][n] for l in range(DEPTH)]))

    small_shapes = [w_loc[n].shape for n in SMALL] + [(1,)]
    nothing = [jnp.zeros((1,), f32)]
    small_pack = _pack(small_full + [local_loss.reshape(1)])
    _, recv_small = _swap_with_sibling([], small_pack)
    chip_small = _small_chip_sum(small_pack, recv_small)
    last = send([(n, 0) for n in GATHER_FIRST], 'first0', also=chip_small)

    mine = {}
    for group, tag, send_sems, recv_sems, srcs, lands, modes in sent:
        srcs, lands = _to_chips_wait(send_sems, recv_sems, srcs, lands, modes, last, f'reduce_wait_{tag}')
        if len(lands) > len(group):
            from_chips_small = lands[-1]
        for (n, l), src, land in zip(group, srcs, lands):
            mine[(n, l)] = _partial_sum(src, land, f'partial_{n}_{l}')
    keys = [(n, l) for n in big_names for l in range(DEPTH)]
    theirs = dict(zip(keys, _swap_partials([mine[k] for k in keys])))

    g_small_p, d_small_p, m_small_p, v_small_p = _small_update(
        small_pack, recv_small, from_chips_small, _pack([w_loc[n] for n in SMALL] + nothing),
        _pack([m_loc[n] for n in SMALL] + nothing), _pack([v_loc[n] for n in SMALL] + nothing))
    loss = _unpack(g_small_p, small_shapes)[-1][0]
    grad, delta, new_m, new_v = {}, {}, {}, {}
    for n, g_, d_, m_, v_ in zip(SMALL, _unpack(g_small_p, small_shapes), _unpack(d_small_p, small_shapes),
                                 _unpack(m_small_p, small_shapes), _unpack(v_small_p, small_shapes)):
        grad[n], delta[n], new_m[n], new_v[n] = g_, d_, m_, v_
    for n in big_names:
        update = _adamw_layer_in_the_middle if n in COLUMNS_FIRST else _adamw
        results = update(w_loc[n], m_loc[n], v_loc[n], [mine[(n, l)] for l in range(DEPTH)],
                         [theirs[(n, l)] for l in range(DEPTH)], 'adamw_' + n)
        grad[n], delta[n], new_m[n], new_v[n] = [given_back(n, t) for t in results]
    return (loss, grad_x, *[grad[n] for n in WEIGHTS], *[delta[n] for n in WEIGHTS],
            *[new_m[n] for n in WEIGHTS], *[new_v[n] for n in WEIGHTS])
```

```python
import functools
import math

import numpy as np
import jax
import jax.numpy as jnp
from jax import lax
from jax.experimental import pallas as pl
from jax.experimental.pallas import tpu as pltpu

f32 = jnp.float32
bf16 = jnp.bfloat16
HI = lax.Precision.HIGHEST
MESH = pl.DeviceIdType.MESH

VMEM_LIMIT_BYTES = 48 * 1024 * 1024
MM_VMEM_BUDGET_BYTES = 32 * 1024 * 1024
LANES = 128

D_MODEL = 1024
DEPTH = 2
GROUP_W = 256
HEAD_DIM = 64
EPS = 1e-6
NEG_INF = -1e30
N_HEADS = 4
MLA_NOPE, MLA_ROPE = 64, 32
MLA_DQK = MLA_NOPE + MLA_ROPE
ROPE_THETA = 10000.0
Q_BLOCK = 128
S5_G, S5_CG, S5_P = 16, 16, 64
DIL_PAIRS = ((128, 1), (512, 4), (2048, 16))
T5_BUCKETS, T5_MAX_DIST = 32, 2048
DN_CHUNK = 64
FFN_HIDDEN = 2816
IN_SPLITS = (256, 128, 32, 256, 768, 768, 4, 4, 256)
IN_COLS = sum(IN_SPLITS)

ADAM_LR, ADAM_B1, ADAM_B2, ADAM_EPS, ADAM_WD, ADAM_STEP = 0.001, 0.9, 0.999, 1e-08, 0.01, 10

WEIGHTS = ['attn_norm', 'w_in', 'w_out', 'mla_q_norm', 'mla_kv_norm', 'mla_w_uq', 'mla_w_ukv', 'mla_qk_q', 'mla_qk_k',
           's5_lambda_re', 's5_lambda_im', 's5_log_dt', 's5_b_re', 's5_b_im', 's5_c_re', 's5_c_im', 's5_d', 's5_w_glu',
           'dil_q_norm', 'dil_k_norm', 't5_bias', 'dn_conv', 'dn_a_log', 'dn_dt_bias', 'dn_o_norm', 'ffn_norm',
           'ffn_w1', 'ffn_w3', 'ffn_w2']
BIG = {'w_in': 1, 'w_out': 1, 'mla_w_uq': 2, 'mla_w_ukv': 2, 's5_w_glu': 2, 'dn_conv': 2, 'ffn_w1': 1, 'ffn_w3': 1,
       'ffn_w2': 1}
TRANSPOSED = ('ffn_w1', 'ffn_w3')
COLUMNS_FIRST = ('w_in',)
SMALL = [n for n in WEIGHTS if n not in BIG]
GATHER_FIRST = ['w_in', 'mla_w_uq', 'mla_w_ukv', 's5_w_glu', 'dn_conv', 'w_out']
GATHER_FFN = ['ffn_w1', 'ffn_w3', 'ffn_w2']
N_SHARDS = 4
PACK_COLS = 1024


def _cparams(sem=None, big=False):
    kw = {}
    if sem is not None:
        kw['dimension_semantics'] = sem
    if big:
        kw['vmem_limit_bytes'] = VMEM_LIMIT_BYTES
    return pltpu.CompilerParams(**kw)


def _pick(n, prefs):
    for p in prefs:
        if p <= n and n % p == 0:
            return p
    return n


def _lane_tile(n, cap):
    for t in range(cap - cap % LANES, 0, -LANES):
        if n % t == 0:
            return t
    return n


def _mm(a, b, mode, name, add=None, out_dtype=f32):
    if mode == 'nn':
        (M, K), (K2, N) = a.shape, b.shape
    elif mode == 'nt':
        (M, K), (N, K2) = a.shape, b.shape
    else:
        (K, M), (K2, N) = a.shape, b.shape
    assert K == K2, (name, a.shape, b.shape)
    tk = K if K <= 2816 else _pick(K, (2816, 2048, 1408, 1024, 512))
    cap_m, cap_n = (1408 if mode == 'tn' else 512), 1408

    def need(tm_, tn_):
        per_step = tm_ * tk * a.dtype.itemsize + tk * tn_ * b.dtype.itemsize + tm_ * tn_ * jnp.dtype(out_dtype).itemsize
        if add is not None:
            per_step += tm_ * tn_ * add.dtype.itemsize
        return 2 * per_step + tm_ * tn_ * 4

    tm, tn = _lane_tile(M, cap_m), _lane_tile(N, cap_n)
    while need(tm, tn) > MM_VMEM_BUDGET_BYTES and cap_m > LANES:
        cap_m //= 2
        tm = _lane_tile(M, cap_m)
    while need(tm, tn) > MM_VMEM_BUDGET_BYTES and cap_n > LANES:
        cap_n //= 2
        tn = _lane_tile(N, cap_n)
    nk = K // tk
    dims = {'nn': (((1,), (0,)), ((), ())), 'nt': (((1,), (1,)), ((), ())), 'tn': (((0,), (0,)), ((), ()))}[mode]
    has_add = add is not None

    def body(*refs):
        a_ref, b_ref = refs[0], refs[1]
        add_ref = refs[2] if has_add else None
        o_ref = refs[3] if has_add else refs[2]
        part = lax.dot_general(a_ref[...].astype(bf16), b_ref[...].astype(bf16), dims, preferred_element_type=f32)
        if nk == 1:
            if has_add:
                part = part + add_ref[...].astype(f32)
            o_ref[...] = part.astype(out_dtype)
        else:
            acc_ref = refs[-1]
            k = pl.program_id(2)

            @pl.when(k == 0)
            def _():
                acc_ref[...] = part

            @pl.when(k > 0)
            def _():
                acc_ref[...] += part

            @pl.when(k == nk - 1)
            def _():
                r = acc_ref[...]
                if has_add:
                    r = r + add_ref[...].astype(f32)
                o_ref[...] = r.astype(out_dtype)

    if mode == 'nn':
        a_spec = pl.BlockSpec((tm, tk), lambda i, j, k: (i, k))
        b_spec = pl.BlockSpec((tk, tn), lambda i, j, k: (k, j))
    elif mode == 'nt':
        a_spec = pl.BlockSpec((tm, tk), lambda i, j, k: (i, k))
        b_spec = pl.BlockSpec((tn, tk), lambda i, j, k: (j, k))
    else:
        a_spec = pl.BlockSpec((tk, tm), lambda i, j, k: (k, i))
        b_spec = pl.BlockSpec((tk, tn), lambda i, j, k: (k, j))
    in_specs = [a_spec, b_spec]
    args = [a, b]
    if has_add:
        in_specs.append(pl.BlockSpec((tm, tn), lambda i, j, k: (i, j)))
        args.append(add)
    return pl.pallas_call(
        body, name=name, grid=(M // tm, N // tn, nk), in_specs=in_specs,
        out_specs=pl.BlockSpec((tm, tn), lambda i, j, k: (i, j)),
        out_shape=jax.ShapeDtypeStruct((M, N), out_dtype),
        scratch_shapes=[pltpu.VMEM((tm, tn), f32)] if nk > 1 else [],
        compiler_params=_cparams(('parallel', 'parallel', 'arbitrary'), big=True),
    )(*args)


def _full_spec(p):
    nd = p.ndim
    return pl.BlockSpec(p.shape, lambda i, _nd=nd: (0,) * _nd)


def _tile_fwd(f, tiled, params, outs, tm, name):
    S = tiled[0].shape[0]
    nt, npar = len(tiled), len(params)

    def body(*refs):
        vals = [r[...].astype(f32) for r in refs[:nt + npar]]
        res = f(*vals)
        for r, o in zip(res, refs[nt + npar:]):
            o[...] = r.astype(o.dtype)

    return pl.pallas_call(
        body, name=name, grid=(S // tm,),
        in_specs=[pl.BlockSpec((tm, t.shape[1]), lambda i: (i, 0)) for t in tiled] + [_full_spec(p) for p in params],
        out_specs=[pl.BlockSpec((tm, c), lambda i: (i, 0)) for c, _ in outs],
        out_shape=[jax.ShapeDtypeStruct((S, c), dt) for c, dt in outs],
        compiler_params=_cparams(('parallel',), big=True),
    )(*tiled, *params)


def _tile_bwd(f, tiled, params, cts, diff_t, diff_p, tm, name, dt_dtypes=None):
    S = tiled[0].shape[0]
    nt, npar, nc = len(tiled), len(params), len(cts)
    it = [i for i in range(nt) if diff_t[i]]
    ip = [i for i in range(npar) if diff_p[i]]
    if dt_dtypes is None:
        dt_dtypes = [f32] * len(it)

    def body(*refs):
        vals = [r[...].astype(f32) for r in refs[:nt + npar]]
        ct_vals = tuple(r[...].astype(f32) for r in refs[nt + npar:nt + npar + nc])
        out_refs = refs[nt + npar + nc:]

        def g(*dv):
            full = list(vals)
            for k, i in enumerate(it):
                full[i] = dv[k]
            for k, i in enumerate(ip):
                full[nt + i] = dv[len(it) + k]
            return tuple(f(*full))

        _, vjp = jax.vjp(g, *[vals[i] for i in it], *[vals[nt + i] for i in ip])
        grads = vjp(ct_vals)
        for k in range(len(it)):
            out_refs[k][...] = grads[k].astype(out_refs[k].dtype)
        step = pl.program_id(0)
        for k in range(len(ip)):
            o = out_refs[len(it) + k]
            gk = grads[len(it) + k]

            @pl.when(step == 0)
            def _(o=o, gk=gk):
                o[...] = gk

            @pl.when(step > 0)
            def _(o=o, gk=gk):
                o[...] += gk

    out_specs = [pl.BlockSpec((tm, tiled[i].shape[1]), lambda i_: (i_, 0)) for i in it] + [_full_spec(params[i]) for i in ip]
    out_shape = [jax.ShapeDtypeStruct(tiled[i].shape, dt_dtypes[k]) for k, i in enumerate(it)] + \
                [jax.ShapeDtypeStruct(params[i].shape, f32) for i in ip]
    res = pl.pallas_call(
        body, name=name, grid=(S // tm,),
        in_specs=[pl.BlockSpec((tm, t.shape[1]), lambda i: (i, 0)) for t in tiled] + [_full_spec(p) for p in params] +
                 [pl.BlockSpec((tm, c.shape[1]), lambda i: (i, 0)) for c in cts],
        out_specs=out_specs, out_shape=out_shape,
        compiler_params=_cparams(('arbitrary',), big=True),
    )(*tiled, *params, *cts)
    return list(res[:len(it)]), list(res[len(it):])


def _rms(x, g):
    return x * lax.rsqrt(jnp.mean(x * x, axis=-1, keepdims=True) + EPS) * g


def _f_rms(x, g):
    return (_rms(x, g),)


def _f_swiglu(u, v):
    return (u * jax.nn.sigmoid(u) * v,)


def _loss_head(y, target):
    S, D = y.shape
    tm = _pick(S, (256, 128))

    def body(y_ref, t_ref, part_ref, dy_ref):
        e = y_ref[...] - t_ref[...]
        dy_ref[...] = e * (1.0 / D)
        s = 0.5 * jnp.sum(jnp.sum(e * e, axis=1, keepdims=True), axis=0, keepdims=True) * (1.0 / D)
        r = lax.broadcasted_iota(jnp.int32, (8, LANES), 0)
        c = lax.broadcasted_iota(jnp.int32, (8, LANES), 1)
        part_ref[0] = jnp.where((r == 0) & (c == 0), s, 0.0)

    return pl.pallas_call(
        body, name='loss_head', grid=(S // tm,),
        in_specs=[pl.BlockSpec((tm, D), lambda i: (i, 0))] * 2,
        out_specs=[pl.BlockSpec((1, 8, LANES), lambda i: (i, 0, 0)), pl.BlockSpec((tm, D), lambda i: (i, 0))],
        out_shape=[jax.ShapeDtypeStruct((S // tm, 8, LANES), f32), jax.ShapeDtypeStruct((S, D), f32)],
        compiler_params=_cparams(('parallel',)),
    )(y, target)


def _pack_rows_of(shape):
    rows = -(-math.prod(shape) // PACK_COLS)
    return -(-rows // 8) * 8


def _pack(arrs):
    parts = []
    for a in arrs:
        rows = _pack_rows_of(a.shape)
        flat = a.astype(f32).reshape(-1)
        parts.append(jnp.pad(flat, (0, rows * PACK_COLS - flat.shape[0])).reshape(rows, PACK_COLS))
    return jnp.concatenate(parts, axis=0)


def _unpack(pack, shapes):
    out, row = [], 0
    for s in shapes:
        rows = _pack_rows_of(s)
        out.append(pack[row:row + rows].reshape(-1)[:math.prod(s)].reshape(s))
        row += rows
    return out


ANY = pl.BlockSpec(memory_space=pl.ANY)


def _place():
    return lax.axis_index('x'), lax.axis_index('y'), lax.axis_index('c')


def _where():
    return jnp.stack([lax.axis_index('c'), 2 * lax.axis_index('x') + lax.axis_index('y')]).astype(jnp.int32)


def _remote(src, dst, send_sems, recv_sems, k, to):
    return pltpu.make_async_remote_copy(src_ref=src, dst_ref=dst, send_sem=send_sems.at[k], recv_sem=recv_sems.at[k],
                                        device_id=to, device_id_type=MESH)


def _swap_with_sibling(gs, small):
    n = len(gs)

    def body(*refs):
        g_refs, s_ref = refs[:n], refs[n]
        r_refs, rs_ref = refs[n + 1:2 * n + 1], refs[2 * n + 1]
        send_sems, recv_sems = refs[2 * n + 2:]
        x, y, c = _place()
        sib = (x, y, 1 - c)
        cps = [_remote(g_refs[t].at[:, 1 - c], r_refs[t], send_sems, recv_sems, t, sib) for t in range(n)]
        cps.append(_remote(s_ref, rs_ref, send_sems, recv_sems, n, sib))
        for cp in cps:
            cp.start()
        for cp in cps:
            cp.wait()

    res = pl.pallas_call(
        body, name='swap_with_sibling', in_specs=[ANY] * (n + 1), out_specs=[ANY] * (n + 1),
        out_shape=[jax.ShapeDtypeStruct((N_SHARDS,) + g.shape[2:], g.dtype) for g in gs] +
                  [jax.ShapeDtypeStruct(small.shape, small.dtype)],
        scratch_shapes=[pltpu.SemaphoreType.DMA((n + 1,)), pltpu.SemaphoreType.DMA((n + 1,))],
    )(*gs, small)
    return list(res[:n]), res[n]


def _exchange_between_chips(cs, small):
    n = len(cs)

    def body(*refs):
        c_refs, s_ref = refs[:n], refs[n]
        r_refs, rs_ref = refs[n + 1:2 * n + 1], refs[2 * n + 1]
        send_sems, recv_sems = refs[2 * n + 2:]
        x, y, c = _place()
        chips = [(1 - x, y), (x, 1 - y), (1 - x, 1 - y)]
        cps = []
        for j, (px, py) in enumerate(chips):
            for t in range(n):
                cps.append(_remote(c_refs[t].at[2 * px + py], r_refs[t].at[j], send_sems, recv_sems, 3 * t + j, (px, py, c)))
            cps.append(_remote(s_ref, rs_ref.at[j], send_sems, recv_sems, 3 * n + j, (px, py, c)))
        for cp in cps:
            cp.start()
        for cp in cps:
            cp.wait()

    res = pl.pallas_call(
        body, name='exchange_between_chips', in_specs=[ANY] * (n + 1), out_specs=[ANY] * (n + 1),
        out_shape=[jax.ShapeDtypeStruct((3,) + c.shape[1:], c.dtype) for c in cs] +
                  [jax.ShapeDtypeStruct((3,) + small.shape, small.dtype)],
        scratch_shapes=[pltpu.SemaphoreType.DMA((3 * n + 3,)), pltpu.SemaphoreType.DMA((3 * n + 3,))],
    )(*cs, small)
    return list(res[:n]), res[n]


def _swap_partials(ts):
    n = len(ts)

    def body(*refs):
        t_refs, o_refs = refs[:n], refs[n:2 * n]
        send_sems, recv_sems = refs[2 * n:]
        x, y, c = _place()
        cps = [_remote(t_refs[t], o_refs[t], send_sems, recv_sems, t, (x, y, 1 - c)) for t in range(n)]
        for cp in cps:
            cp.start()
        for cp in cps:
            cp.wait()

    return pl.pallas_call(
        body, name='swap_partials', in_specs=[ANY] * n, out_specs=[ANY] * n,
        out_shape=[jax.ShapeDtypeStruct(t.shape, t.dtype) for t in ts],
        scratch_shapes=[pltpu.SemaphoreType.DMA((n,)), pltpu.SemaphoreType.DMA((n,))],
    )(*ts)


HBM = pl.BlockSpec(memory_space=pltpu.HBM)
SEM = pl.BlockSpec(memory_space=pltpu.SEMAPHORE)
DATAFLOW = pltpu.SideEffectType.DATAFLOW_SIDE_EFFECTING


def _in_hbm(t):
    return pltpu.with_memory_space_constraint(t, pltpu.HBM)


def _other_chips():
    x, y, c = _place()
    return [(1 - x, y, c), (x, 1 - y, c), (1 - x, 1 - y, c)]


def _to_chips_copies(src_refs, land_refs, send_sems, recv_sems, per_peer):
    x, y, _ = _place()
    cps = []
    for t, (src, land) in enumerate(zip(src_refs, land_refs)):
        for j, (px, py, pc) in enumerate(_other_chips()):
            s = src.at[2 * px + py] if per_peer else src
            d = land.at[j] if per_peer else land.at[2 * x + y]
            cps.append(_remote(s, d, send_sems, recv_sems, 3 * t + j, (px, py, pc)))
    return cps


def _to_chips_start(srcs, lands, per_peer, order, name):
    n = len(srcs)

    def body(*refs):
        src_refs, land_refs = refs[:n], refs[n:2 * n]
        send_sems, recv_sems = refs[2 * n + 1], refs[2 * n + 2]
        token = refs[-1]
        for cp in _to_chips_copies(src_refs, land_refs, send_sems, recv_sems, per_peer):
            cp.start()
        token[...] = jnp.zeros_like(token)

    res = pl.pallas_call(
        body, name=name, in_specs=[HBM] * (2 * n) + [ANY],
        out_specs=[SEM, SEM] + [HBM] * (2 * n) + [pl.BlockSpec(memory_space=pltpu.VMEM)],
        out_shape=[pltpu.SemaphoreType.DMA((3 * n,)), pltpu.SemaphoreType.DMA((3 * n,))] +
                  [pltpu.HBM(t.shape, t.dtype) for t in srcs] + [pltpu.HBM(t.shape, t.dtype) for t in lands] +
                  [jax.ShapeDtypeStruct((8, LANES), f32)],
        input_output_aliases={i: 2 + i for i in range(2 * n)},
        compiler_params=pltpu.CompilerParams(has_side_effects=DATAFLOW),
    )(*[_in_hbm(t) for t in srcs], *[_in_hbm(t) for t in lands], order)
    return res[0], res[1], list(res[2:2 + n]), list(res[2 + n:2 + 2 * n]), res[-1]


def _to_chips_wait(send_sems, recv_sems, srcs, lands, per_peer, after, name):
    n = len(srcs)

    def body(*refs):
        src_refs, land_refs = refs[:n], refs[n:2 * n]
        send_ref, recv_ref = refs[2 * n], refs[2 * n + 1]
        for cp in _to_chips_copies(src_refs, land_refs, send_ref, recv_ref, per_peer):
            cp.wait_send()
            cp.wait_recv()

    res = pl.pallas_call(
        body, name=name, in_specs=[HBM] * (2 * n) + [SEM, SEM, ANY],
        out_specs=[HBM] * (2 * n),
        out_shape=[pltpu.HBM(t.shape, t.dtype) for t in srcs] + [pltpu.HBM(t.shape, t.dtype) for t in lands],
        input_output_aliases={i: i for i in range(2 * n)},
        compiler_params=pltpu.CompilerParams(has_side_effects=DATAFLOW),
    )(*srcs, *lands, send_sems, recv_sems, after)
    return list(res[:n]), list(res[n:])


def _row_tile(a):
    return _pick(a, (512, 256, 128, 64, 32, 16, 8))


def _partial_sum(g, land, name):
    _, a, b = g.shape
    tr = _row_tile(a)

    def body(w_ref, g_ref, r_ref, o_ref):
        t = g_ref[0].astype(f32) + r_ref[0].astype(f32)
        t = t + r_ref[1].astype(f32)
        t = t + r_ref[2].astype(f32)
        o_ref[...] = t.astype(o_ref.dtype)

    return pl.pallas_call(
        body, name=name,
        grid_spec=pltpu.PrefetchScalarGridSpec(
            num_scalar_prefetch=1, grid=(a // tr,),
            in_specs=[pl.BlockSpec((1, tr, b), lambda i, w: (w[1], i, 0)), pl.BlockSpec((3, tr, b), lambda i, w: (0, i, 0))],
            out_specs=pl.BlockSpec((tr, b), lambda i, w: (i, 0))),
        out_shape=jax.ShapeDtypeStruct((a, b), bf16),
        compiler_params=_cparams(('parallel',)),
    )(_where(), g, land)


def _by_shard(name, t):
    r, c = t.shape
    if BIG[name] == 2:
        return t.reshape(r, N_SHARDS, c // N_SHARDS).transpose(1, 0, 2)
    return t.reshape(N_SHARDS, r // N_SHARDS, c)


def _from_shards(name, g):
    s, a, b = g.shape
    if BIG[name] == 2:
        return g.transpose(1, 0, 2).reshape(a, s * b)
    return g.reshape(s * a, b)


def _adam_math(w, g, m, v):
    m = ADAM_B1 * m + (1.0 - ADAM_B1) * g
    v = ADAM_B2 * v + (1.0 - ADAM_B2) * (g * g)
    m_hat = m / (1.0 - ADAM_B1 ** ADAM_STEP)
    v_hat = v / (1.0 - ADAM_B2 ** ADAM_STEP)
    delta = -ADAM_LR * (m_hat / (jnp.sqrt(v_hat) + ADAM_EPS) + ADAM_WD * w)
    return delta, m, v


def _small_update(own, sib, chips, w, m, v):
    def body(o_ref, s_ref, c_ref, w_ref, m_ref, v_ref, g_out, d_out, m_out, v_out):
        chip = o_ref[...] + s_ref[...]
        g = (chip + c_ref[0]) + (c_ref[1] + c_ref[2])
        d, mn, vn = _adam_math(w_ref[...], g, m_ref[...], v_ref[...])
        g_out[...] = g
        d_out[...] = d
        m_out[...] = mn
        v_out[...] = vn

    return pl.pallas_call(body, name='small_update', out_shape=[jax.ShapeDtypeStruct(own.shape, f32)] * 4)(
        own, sib, chips, w, m, v)


def _small_chip_sum(own, sib):
    def body(o_ref, s_ref, out):
        out[...] = o_ref[...] + s_ref[...]
    return pl.pallas_call(body, name='small_chip_sum', out_shape=jax.ShapeDtypeStruct(own.shape, f32))(own, sib)


def _adamw(w, m, v, mine, theirs, name):
    layers, a, b = w.shape
    tr = _row_tile(a)

    def body(w_ref, m_ref, v_ref, p0, p1, q0, q1, g_out, d_out, m_out, v_out):
        first = pl.program_id(0) == 0
        g = jnp.where(first, p0[...].astype(f32) + q0[...].astype(f32), p1[...].astype(f32) + q1[...].astype(f32))
        d, mn, vn = _adam_math(w_ref[0], g, m_ref[0], v_ref[0])
        g_out[0] = g
        d_out[0] = d
        m_out[0] = mn
        v_out[0] = vn

    full = pl.BlockSpec((1, tr, b), lambda l, i: (l, i, 0))
    part = pl.BlockSpec((tr, b), lambda l, i: (i, 0))
    return pl.pallas_call(body, name=name, grid=(layers, a // tr), in_specs=[full] * 3 + [part] * 4, out_specs=[full] * 4,
                          out_shape=[jax.ShapeDtypeStruct(w.shape, f32)] * 4,
                          compiler_params=_cparams(('parallel', 'parallel')))(w, m, v, *mine, *theirs)


def _adamw_layer_in_the_middle(w, m, v, mine, theirs, name):
    a, layers, b = w.shape
    assert layers == 2 and b % LANES == 0

    def body(w_ref, m_ref, v_ref, p0, p1, q0, q1, g_out, d_out, m_out, v_out):
        g = jnp.stack([p0[...].astype(f32) + q0[...].astype(f32), p1[...].astype(f32) + q1[...].astype(f32)], axis=1)
        d, mn, vn = _adam_math(w_ref[...], g, m_ref[...], v_ref[...])
        g_out[...] = g
        d_out[...] = d
        m_out[...] = mn
        v_out[...] = vn

    full = pl.BlockSpec((a, layers, LANES), lambda i: (0, 0, i))
    part = pl.BlockSpec((a, LANES), lambda i: (0, i))
    return pl.pallas_call(body, name=name, grid=(b // LANES,), in_specs=[full] * 3 + [part] * 4, out_specs=[full] * 4,
                          out_shape=[jax.ShapeDtypeStruct(w.shape, f32)] * 4,
                          compiler_params=_cparams(('parallel',), big=True))(w, m, v, *mine, *theirs)


def _dg(a, b, ca, cb):
    return lax.dot_general(a.astype(bf16), b.astype(bf16), (((ca,), (cb,)), ((), ())), preferred_element_type=f32)


@jax.custom_vjp
def _bmm(a, b):
    return _dg(a, b, 1, 0)


_bmm.defvjp(lambda a, b: (_dg(a, b, 1, 0), (a, b)), lambda r, g: (_dg(g, r[1], 1, 1), _dg(r[0], g, 0, 0)))


@jax.custom_vjp
def _bmm_nt(a, b):
    return _dg(a, b, 1, 1)


_bmm_nt.defvjp(lambda a, b: (_dg(a, b, 1, 1), (a, b)), lambda r, g: (_dg(g, r[1], 1, 0), _dg(g, r[0], 0, 0)))


@jax.custom_vjp
def _bmm_tn(a, b):
    return _dg(a, b, 0, 0)


_bmm_tn.defvjp(lambda a, b: (_dg(a, b, 0, 0), (a, b)), lambda r, g: (_dg(r[1], g, 1, 1), _dg(r[0], g, 1, 0)))


def _hdot(a, b):
    return jnp.dot(a, b, precision=HI, preferred_element_type=f32)


def _hdot_nt(a, b):
    return lax.dot_general(a, b, (((1,), (1,)), ((), ())), precision=HI, preferred_element_type=f32)


def _hdot_tn(a, b):
    return lax.dot_general(a, b, (((0,), (0,)), ((), ())), precision=HI, preferred_element_type=f32)


def _head_mask(h, width=GROUP_W):
    lane = lax.broadcasted_iota(jnp.int32, (1, width), 1)
    return ((lane >= h * HEAD_DIM) & (lane < (h + 1) * HEAD_DIM)).astype(f32)


def _rope_perm():
    p = np.zeros((LANES, LANES), np.float32)
    half = MLA_ROPE // 2
    for i in range(half):
        p[MLA_NOPE + half + i, MLA_NOPE + i] = -1.0
        p[MLA_NOPE + i, MLA_NOPE + half + i] = 1.0
    return jnp.asarray(p)


def _rope_tables(S):
    half = MLA_ROPE // 2
    freqs = ROPE_THETA ** (-jnp.arange(half, dtype=f32) / half)
    ang = jnp.arange(S, dtype=f32)[:, None] * freqs[None, :]
    cos, sin = jnp.cos(ang), jnp.sin(ang)
    ones, zeros = jnp.ones((S, MLA_NOPE), f32), jnp.zeros((S, LANES - MLA_DQK), f32)
    c_tab = jnp.concatenate([ones, cos, cos, zeros], axis=1)
    s_tab = jnp.concatenate([jnp.zeros((S, MLA_NOPE), f32), sin, sin, zeros], axis=1)
    return c_tab, s_tab


def _f_mla_pre(c_q, c_kv, krope, c_tab, s_tab, q_norm, kv_norm, wq0, wq1, wq2, wq3, wk0, wk1, wk2, wk3, wv, gq, gk, perm):
    wq, wk = (wq0, wq1, wq2, wq3), (wk0, wk1, wk2, wk3)
    nq = _rms(c_q, q_norm)
    nkv = _rms(c_kv, kv_norm)

    def norm_rope(t, g):
        t = t * lax.rsqrt(jnp.sum(t * t, axis=-1, keepdims=True) * (1.0 / MLA_DQK) + EPS) * g
        return t * c_tab + _hdot(t, perm) * s_tab

    qs = [norm_rope(_bmm(nq, wq[h]), gq) * (MLA_DQK ** -0.5) for h in range(N_HEADS)]
    ks = [norm_rope(_bmm(nkv, wk[h]) + krope, gk) for h in range(N_HEADS)]
    return (*qs, *ks, _bmm(nkv, wv))


def _f_attn(qs, ks, v, q0):
    tq, S = qs[0].shape[0], ks[0].shape[0]
    qpos = q0 + lax.broadcasted_iota(jnp.int32, (tq, S), 0)
    kpos = lax.broadcasted_iota(jnp.int32, (tq, S), 1)
    keep = kpos <= qpos
    logits = [jnp.where(keep, _bmm_nt(qs[h], ks[h]), NEG_INF) for h in range(N_HEADS)]
    ps = [jnp.exp(lg - jnp.max(lg, axis=-1, keepdims=True)) for lg in logits]
    ps = [p / jnp.sum(p, axis=-1, keepdims=True) for p in ps]
    return sum(_bmm(p, v) * _head_mask(h) for h, p in enumerate(ps))


ATTN_PARTS = 4


def _mla_attn_fwd(qs, ks, v, name):
    S = v.shape[0]
    tq = Q_BLOCK
    parts = ATTN_PARTS if S % (ATTN_PARTS * tq) == 0 else 1
    per = S // parts
    outs = []
    for p in range(parts):
        n_keys = (p + 1) * per
        first_block = p * (per // tq)

        def body(*refs, first_block=first_block):
            q_vals = [r[...] for r in refs[:4]]
            k_vals = [r[...] for r in refs[4:8]]
            refs[9][...] = _f_attn(q_vals, k_vals, refs[8][...], (first_block + pl.program_id(0)) * tq)

        qspec = pl.BlockSpec((tq, LANES), lambda i, fb=first_block: (fb + i, 0))
        outs.append(pl.pallas_call(
            body, name=f'{name}_{p}', grid=(per // tq,),
            in_specs=[qspec] * 4 + [pl.BlockSpec((n_keys, LANES), lambda i: (0, 0))] * 4 +
                     [pl.BlockSpec((n_keys, GROUP_W), lambda i: (0, 0))],
            out_specs=pl.BlockSpec((tq, GROUP_W), lambda i: (i, 0)),
            out_shape=jax.ShapeDtypeStruct((per, GROUP_W), f32),
            compiler_params=_cparams(('parallel',), big=True),
        )(*qs, *ks, v))
    return jnp.concatenate(outs, axis=0)


def _mla_attn_bwd(qs, ks, v, do, name):
    S = v.shape[0]
    tq = Q_BLOCK
    parts = ATTN_PARTS if S % (ATTN_PARTS * tq) == 0 else 1
    per = S // parts
    dq_parts, dkv_sum = [], None
    for p in range(parts):
        n_keys = (p + 1) * per
        first_block = p * (per // tq)

        def body(*refs, first_block=first_block):
            q_vals = [r[...].astype(f32) for r in refs[:4]]
            k_vals = [r[...].astype(f32) for r in refs[4:8]]
            v_val = refs[8][...].astype(f32)
            q0 = (first_block + pl.program_id(0)) * tq
            _, vjp = jax.vjp(lambda a, b, c: _f_attn(a, b, c, q0), q_vals, k_vals, v_val)
            dqs, dks, dv = vjp(refs[9][...])
            outs = refs[10:]
            for h in range(N_HEADS):
                outs[h][...] = dqs[h]
            first = pl.program_id(0) == 0
            for o, g in zip(outs[4:], (*dks, dv)):
                @pl.when(first)
                def _(o=o, g=g):
                    o[...] = g

                @pl.when(jnp.logical_not(first))
                def _(o=o, g=g):
                    o[...] += g

        qspec = pl.BlockSpec((tq, LANES), lambda i, fb=first_block: (fb + i, 0))
        kspec = pl.BlockSpec((n_keys, LANES), lambda i: (0, 0))
        vspec = pl.BlockSpec((n_keys, GROUP_W), lambda i: (0, 0))
        res = pl.pallas_call(
            body, name=f'{name}_{p}', grid=(per // tq,),
            in_specs=[qspec] * 4 + [kspec] * 4 + [vspec, pl.BlockSpec((tq, GROUP_W), lambda i, fb=first_block: (fb + i, 0))],
            out_specs=[pl.BlockSpec((tq, LANES), lambda i: (i, 0))] * 4 + [kspec] * 4 + [vspec],
            out_shape=[jax.ShapeDtypeStruct((per, LANES), f32)] * 4 + [jax.ShapeDtypeStruct((n_keys, LANES), f32)] * 4 +
                      [jax.ShapeDtypeStruct((n_keys, GROUP_W), f32)],
            compiler_params=_cparams(('arbitrary',), big=True),
        )(*qs, *ks, v, do)
        dq_parts.append(res[:4])
        dkv = [jnp.pad(t, ((0, S - n_keys), (0, 0))) for t in res[4:]]
        dkv_sum = dkv if dkv_sum is None else [a_ + b_ for a_, b_ in zip(dkv_sum, dkv)]
    dqs = [jnp.concatenate([dq_parts[p][h] for p in range(parts)], axis=0) for h in range(N_HEADS)]
    return dqs, dkv_sum[:4], dkv_sum[4]


def _mla_params(mp):
    pad = LANES - MLA_DQK
    wq = jnp.pad(mp['mla_w_uq'].reshape(GROUP_W, N_HEADS, MLA_DQK).transpose(1, 0, 2), ((0, 0), (0, 0), (0, pad)))
    wkv = mp['mla_w_ukv'].reshape(LANES, N_HEADS, MLA_NOPE + HEAD_DIM)
    wk = jnp.pad(wkv[:, :, :MLA_NOPE].transpose(1, 0, 2), ((0, 0), (0, 0), (0, LANES - MLA_NOPE)))
    wv = wkv[:, :, MLA_NOPE:].reshape(LANES, GROUP_W)
    gq = jnp.pad(mp['mla_qk_q'], (0, pad))[None]
    gk = jnp.pad(mp['mla_qk_k'], (0, pad))[None]
    return [mp['mla_q_norm'][None], mp['mla_kv_norm'][None], *[wq[h] for h in range(N_HEADS)],
            *[wk[h] for h in range(N_HEADS)], wv, gq, gk, _rope_perm()]


def _mla_fwd(c_q, c_kv, k_rope, mp, l):
    S = c_q.shape[0]
    tm = _pick(S, (256, 128))
    krope = jnp.pad(k_rope, ((0, 0), (MLA_NOPE, LANES - MLA_DQK)))
    c_tab, s_tab = _rope_tables(S)
    tiled = [c_q, c_kv, krope, c_tab, s_tab]
    params = _mla_params(mp)
    res = _tile_fwd(_f_mla_pre, tiled, params, [(LANES, bf16)] * 8 + [(GROUP_W, bf16)], tm, f'mla_pre_fwd_{l}')
    qs, ks, v = res[:4], res[4:8], res[8]
    y = _mla_attn_fwd(qs, ks, v, f'mla_attn_fwd_{l}')
    return y, (tiled, params, qs, ks, v)


def _mla_bwd(dy, saved, l):
    tiled, params, qs, ks, v = saved
    S = dy.shape[0]
    tm = _pick(S, (256, 128))
    dqs, dks, dv = _mla_attn_bwd(qs, ks, v, dy, f'mla_attn_bwd_{l}')
    (dc_q, dc_kv, dkrope), dpar = _tile_bwd(_f_mla_pre, tiled, params, [*dqs, *dks, dv], [True, True, True, False, False],
                                            [True] * 13 + [False], tm, f'mla_pre_bwd_{l}')
    dqn, dkvn = dpar[0], dpar[1]
    dwq, dwk = jnp.stack(dpar[2:6]), jnp.stack(dpar[6:10])
    dwv, dgq, dgk = dpar[10:13]
    dw_uq = dwq[:, :, :MLA_DQK].transpose(1, 0, 2).reshape(GROUP_W, N_HEADS * MLA_DQK)
    dw_ukv = jnp.concatenate([dwk[:, :, :MLA_NOPE].transpose(1, 0, 2), dwv.reshape(LANES, N_HEADS, HEAD_DIM)],
                             axis=2).reshape(LANES, N_HEADS * (MLA_NOPE + HEAD_DIM))
    grads = {'mla_q_norm': dqn[0], 'mla_kv_norm': dkvn[0], 'mla_w_uq': dw_uq, 'mla_w_ukv': dw_ukv,
             'mla_qk_q': dgq[0, :MLA_DQK], 'mla_qk_k': dgk[0, :MLA_DQK]}
    return dc_q, dc_kv, dkrope[:, MLA_NOPE:MLA_DQK], grads


SPAN = 128


def _head_mean_matrix():
    h = np.arange(GROUP_W) // HEAD_DIM
    return jnp.asarray((h[:, None] == h[None, :]).astype(np.float32) / HEAD_DIM)


def _f_dil_pre(q, k, gq, gk, hm):
    qn = q * lax.rsqrt(_hdot(q * q, hm) + EPS) * gq * (HEAD_DIM ** -0.5)
    kn = k * lax.rsqrt(_hdot(k * k, hm) + EPS) * gk
    return qn, kn


def _f_dil_branch(qb, kp, kc, vp, vc, b0, b1, b2, b3, first):
    kcat = jnp.concatenate([kp, kc], axis=0)
    vcat = jnp.concatenate([vp, vc], axis=0)
    qi = lax.broadcasted_iota(jnp.int32, (SPAN, 2 * SPAN), 0) + SPAN
    kj = lax.broadcasted_iota(jnp.int32, (SPAN, 2 * SPAN), 1)
    delta = qi - kj
    valid = (delta >= 0) & (delta <= SPAN) & jnp.logical_not(first & (kj < SPAN))
    masks = [_head_mask(h) for h in range(N_HEADS)]
    raw = [_bmm_nt(qb * hm, kcat) for hm in masks]
    logits = [jnp.where(valid, r + bias, NEG_INF) for r, bias in zip(raw, (b0, b1, b2, b3))]
    ms = [jnp.max(lg, axis=-1, keepdims=True) for lg in logits]
    ps = [jnp.exp(lg - m) for lg, m in zip(logits, ms)]
    pvs = [_bmm(p, vcat) for p in ps]
    o = sum(pv * hm for pv, hm in zip(pvs, masks))
    m_full = sum(m * hm for m, hm in zip(ms, masks))
    l_full = sum(jnp.sum(p, axis=-1, keepdims=True) * hm for p, hm in zip(ps, masks))
    return o, m_full, l_full


def _dil_branch_specs(d, nb):
    cur = pl.BlockSpec((SPAN, GROUP_W), lambda r, n: (n, r))
    prev = pl.BlockSpec((SPAN, GROUP_W), lambda r, n: (jnp.maximum(n - 1, 0), r))
    bias = pl.BlockSpec((1, SPAN, 2 * SPAN), lambda r, n: (0, 0, 0))
    return cur, prev, bias


def _head_table_specs():
    return [pl.BlockSpec((1, SPAN, 2 * SPAN), lambda r, n, h=h: (h, 0, 0)) for h in range(N_HEADS)]


def _dil_branch_fwd(q, k, v, table, name):
    L, d = q.shape[0], q.shape[1] // GROUP_W
    nb = L // SPAN
    cur, prev, bias = _dil_branch_specs(d, nb)

    def body(q_ref, kp_ref, kc_ref, vp_ref, vc_ref, b0, b1, b2, b3, o_ref, m_ref, l_ref):
        o, m, l = _f_dil_branch(*[r[...].astype(f32) for r in (q_ref, kp_ref, kc_ref, vp_ref, vc_ref)], b0[0], b1[0], b2[0], b3[0],
                                pl.program_id(1) == 0)
        o_ref[...] = o
        m_ref[...] = m
        l_ref[...] = l

    return pl.pallas_call(
        body, name=name, grid=(d, nb), in_specs=[cur, prev, cur, prev, cur] + _head_table_specs(),
        out_specs=[cur] * 3, out_shape=[jax.ShapeDtypeStruct(q.shape, f32)] * 3,
        compiler_params=_cparams(('parallel', 'parallel')),
    )(q, k, k, v, v, *[table] * N_HEADS)


def _dil_branch_bwd(q, k, v, table, do, dm, dl, name):
    L, d = q.shape[0], q.shape[1] // GROUP_W
    nb = L // SPAN
    cur, prev, bias = _dil_branch_specs(d, nb)
    whole = pl.BlockSpec((L, GROUP_W), lambda r, n: (0, r))

    def body(q_ref, kp_ref, kc_ref, vp_ref, vc_ref, b0, b1, b2, b3, do_ref, dm_ref, dl_ref,
             dq_ref, dk_ref, dv_ref, db0, db1, db2, db3):
        r, n = pl.program_id(0), pl.program_id(1)
        first = n == 0
        _, vjp = jax.vjp(lambda *a: _f_dil_branch(*a, first), *[r[...].astype(f32) for r in (q_ref, kp_ref, kc_ref, vp_ref, vc_ref)],
                         b0[0], b1[0], b2[0], b3[0])
        dq, dkp, dkc, dvp, dvc, g0, g1, g2, g3 = vjp((do_ref[...], dm_ref[...], dl_ref[...]))
        dq_ref[...] = dq

        @pl.when(first)
        def _():
            dk_ref[...] = jnp.zeros_like(dk_ref)
            dv_ref[...] = jnp.zeros_like(dv_ref)

        rows = pl.ds(pl.multiple_of(n * SPAN, SPAN), SPAN)
        dk_ref[rows, :] += dkc
        dv_ref[rows, :] += dvc

        @pl.when(n > 0)
        def _():
            before = pl.ds(pl.multiple_of((n - 1) * SPAN, SPAN), SPAN)
            dk_ref[before, :] += dkp
            dv_ref[before, :] += dvp

        start = first & (r == 0)
        for o, g in zip((db0, db1, db2, db3), (g0, g1, g2, g3)):
            @pl.when(start)
            def _(o=o, g=g):
                o[0] = g

            @pl.when(jnp.logical_not(start))
            def _(o=o, g=g):
                o[0] += g

    res = pl.pallas_call(
        body, name=name, grid=(d, nb), in_specs=[cur, prev, cur, prev, cur] + _head_table_specs() + [cur] * 3,
        out_specs=[cur, whole, whole] + [bias] * 4,
        out_shape=[jax.ShapeDtypeStruct(q.shape, f32)] * 3 + [jax.ShapeDtypeStruct((1, SPAN, 2 * SPAN), f32)] * 4,
        compiler_params=_cparams(('arbitrary', 'arbitrary')),
    )(q, k, k, v, v, *[table] * N_HEADS, do, dm, dl)
    return res[0], res[1], res[2], res[3:]


def _f_dil_merge(o1, m1, l1, o2, m2, l2, o3, m3, l3):
    mx = jnp.maximum(jnp.maximum(m1, m2), m3)
    w1, w2, w3 = jnp.exp(m1 - mx), jnp.exp(m2 - mx), jnp.exp(m3 - mx)
    return ((w1 * o1 + w2 * o2 + w3 * o3) / (w1 * l1 + w2 * l2 + w3 * l3),)


def _bias_onehot(dilation):
    qi = jnp.arange(SPAN, dtype=jnp.int32)[:, None] + SPAN
    kj = jnp.arange(2 * SPAN, dtype=jnp.int32)[None, :]
    bucket = _t5_bucket(jnp.clip(qi - kj, 0, SPAN) * dilation).reshape(-1)
    return (bucket[None, :] == jnp.arange(T5_BUCKETS, dtype=jnp.int32)[:, None]).astype(f32)


def _bias_tables(t5_t, onehot, name):
    N = onehot.shape[1]
    tn = _pick(N, (4096, 2048, 1024))

    def body(t_ref, oh_ref, o_ref):
        o_ref[...] = _hdot(t_ref[...], oh_ref[...])

    return pl.pallas_call(
        body, name=name, grid=(N // tn,),
        in_specs=[pl.BlockSpec((8, T5_BUCKETS), lambda i: (0, 0)), pl.BlockSpec((T5_BUCKETS, tn), lambda i: (0, i))],
        out_specs=pl.BlockSpec((8, tn), lambda i: (0, i)), out_shape=jax.ShapeDtypeStruct((8, N), f32),
        compiler_params=_cparams(('parallel',)),
    )(t5_t, onehot)


def _bias_tables_bwd(d_tab, onehot, name):
    N = onehot.shape[1]
    tn = _pick(N, (4096, 2048, 1024))

    def body(g_ref, oh_ref, o_ref):
        part = _hdot_nt(g_ref[...], oh_ref[...])

        @pl.when(pl.program_id(0) == 0)
        def _():
            o_ref[...] = part

        @pl.when(pl.program_id(0) > 0)
        def _():
            o_ref[...] += part

    return pl.pallas_call(
        body, name=name, grid=(N // tn,),
        in_specs=[pl.BlockSpec((8, tn), lambda i: (0, i)), pl.BlockSpec((T5_BUCKETS, tn), lambda i: (0, i))],
        out_specs=pl.BlockSpec((8, T5_BUCKETS), lambda i: (0, 0)), out_shape=jax.ShapeDtypeStruct((8, T5_BUCKETS), f32),
        compiler_params=_cparams(('arbitrary',)),
    )(d_tab, onehot)


def _by_residue(t, d):
    S, C = t.shape
    return t.reshape(S // d, d * C)


def _from_residue(t):
    return t.reshape(-1, GROUP_W)


def _dil_tables(t5_bias):
    t5_t = jnp.pad(t5_bias.T, ((0, 8 - N_HEADS), (0, 0)))
    return [_bias_tables(t5_t, _bias_onehot(d), f'dil_bias_fwd_{bi}').reshape(8, SPAN, 2 * SPAN)
            for bi, (_, d) in enumerate(DIL_PAIRS)]


def _t5_grad(d_tables):
    total = None
    for bi, (_, d) in enumerate(DIL_PAIRS):
        g = _bias_tables_bwd(d_tables[bi], _bias_onehot(d), f'dil_bias_bwd_{bi}')
        total = g if total is None else total + g
    return total[:N_HEADS].T


def _dil_fwd(qkv, mp, l):
    S = qkv.shape[0]
    tm = _pick(S, (256, 128))
    q, k, v = qkv[:, :GROUP_W], qkv[:, GROUP_W:2 * GROUP_W], qkv[:, 2 * GROUP_W:]
    pre_params = [jnp.tile(mp['dil_q_norm'], N_HEADS)[None], jnp.tile(mp['dil_k_norm'], N_HEADS)[None], _head_mean_matrix()]
    qn, kn = _tile_fwd(_f_dil_pre, [q, k], pre_params, [(GROUP_W, bf16)] * 2, tm, f'dil_pre_fwd_{l}')
    v = v.astype(bf16)
    tables = mp['dil_tables'] if 'dil_tables' in mp else _dil_tables(mp['t5_bias'])
    branches, outs = [], []
    for bi, (_, d) in enumerate(DIL_PAIRS):
        tab = tables[bi]
        qd, kd, vd = _by_residue(qn, d), _by_residue(kn, d), _by_residue(v, d)
        o, m, lsum = _dil_branch_fwd(qd, kd, vd, tab, f'dil_branch_fwd_{l}_{bi}')
        branches.append((qd, kd, vd, tab))
        outs += [_from_residue(o), _from_residue(m), _from_residue(lsum)]
    (y,) = _tile_fwd(_f_dil_merge, outs, [], [(GROUP_W, f32)], tm, f'dil_merge_fwd_{l}')
    return y, (q, k, pre_params, branches, outs)


def _dil_bwd(dy, saved, l):
    q, k, pre_params, branches, outs = saved
    S = dy.shape[0]
    tm = _pick(S, (256, 128))
    douts, _ = _tile_bwd(_f_dil_merge, outs, [], [dy], [True] * 9, [], tm, f'dil_merge_bwd_{l}')
    dqn = dkn = dv = None
    d_tabs = []
    for bi, (_, d) in enumerate(DIL_PAIRS):
        qd, kd, vd, tab = branches[bi]
        do, dm, dl = [_by_residue(t, d) for t in douts[3 * bi:3 * bi + 3]]
        dq_b, dk_b, dv_b, dbias = _dil_branch_bwd(qd, kd, vd, tab, do, dm, dl, f'dil_branch_bwd_{l}_{bi}')
        d_tabs.append(jnp.concatenate([*dbias, jnp.zeros((8 - N_HEADS, SPAN, 2 * SPAN), f32)], axis=0).reshape(8, -1))
        dq_b, dk_b, dv_b = _from_residue(dq_b), _from_residue(dk_b), _from_residue(dv_b)
        dqn = dq_b if dqn is None else dqn + dq_b
        dkn = dk_b if dkn is None else dkn + dk_b
        dv = dv_b if dv is None else dv + dv_b
    (dq, dk), (dgq, dgk) = _tile_bwd(_f_dil_pre, [q, k], pre_params, [dqn, dkn], [True, True], [True, True, False], tm,
                                     f'dil_pre_bwd_{l}')
    grads = {'dil_q_norm': dgq.reshape(N_HEADS, HEAD_DIM).sum(0), 'dil_k_norm': dgk.reshape(N_HEADS, HEAD_DIM).sum(0),
             't5_tables': d_tabs}
    return jnp.concatenate([dq, dk, dv], axis=1), grads


S5_LANES = S5_G * S5_P
SCAN_SEGMENTS = 8
SCAN_W = 512


def _f_s5_prep(bre, bim, lr, li, logdt_col, expand):
    dt = jnp.sum(jnp.exp(logdt_col) * expand, axis=0, keepdims=True)
    mag = jnp.exp(lr * dt)
    ar, ai = mag * jnp.cos(li * dt), mag * jnp.sin(li * dt)
    den = lr * lr + li * li
    nr, ni = ar - 1.0, ai
    zr = (nr * lr + ni * li) / den
    zi = (ni * lr - nr * li) / den
    bb = jnp.concatenate([zr * bre - zi * bim, zr * bim + zi * bre], axis=1)
    a_rows = jnp.broadcast_to(jnp.concatenate([ar, ai], axis=1), bb.shape)
    return bb, a_rows


def _s5_scan(x, a_rows, name, reverse=False, h=None):
    S = x.shape[0]
    NL = x.shape[1] // 2
    T = S // SCAN_SEGMENTS
    nblk = NL // SCAN_W
    n_in = 4 if reverse else 2

    def body(*refs):
        if reverse:
            (x_hbm, pr_hbm, pi_hbm, ar_ref, ai_ref, hr_hbm, hi_hbm, dar_ref, dai_ref,
             xr_s, xi_s, pr_s, pi_s, hr_s, hi_s, in_sems, out_sems) = refs
        else:
            x_hbm, ar_ref, ai_ref, hr_hbm, hi_hbm, xr_s, xi_s, hr_s, hi_s, in_sems, out_sems = refs
        col = pl.multiple_of(pl.program_id(0) * SCAN_W, SCAN_W)
        loads = []
        for k in range(SCAN_SEGMENTS):
            rows = pl.ds(k * T, T)
            sources = [(x_hbm, col, xr_s), (x_hbm, NL + col, xi_s)]
            if reverse:
                sources += [(pr_hbm, col, pr_s), (pi_hbm, col, pi_s)]
            for i, (src, c0, dst) in enumerate(sources):
                loads.append(pltpu.make_async_copy(src.at[rows, pl.ds(c0, SCAN_W)], dst.at[:, k, :],
                                                   in_sems.at[i * SCAN_SEGMENTS + k]))
        for cp in loads:
            cp.start()
        for cp in loads:
            cp.wait()
        ar = ar_ref[...]
        ai = -ai_ref[...] if reverse else ai_ref[...]
        zero = jnp.zeros((SCAN_SEGMENTS, SCAN_W), f32)

        def at(s):
            return T - 1 - s if reverse else s

        def local(s, c):
            hr, hi, pr, pi = c
            j = at(s)
            nhr = ar * hr - ai * hi + xr_s[j]
            nhi = ar * hi + ai * hr + xi_s[j]
            hr_s[j] = nhr
            hi_s[j] = nhi
            return nhr, nhi, ar * pr - ai * pi, ar * pi + ai * pr

        er, ei, pr, pi = lax.fori_loop(0, T, local, (zero, zero, zero + 1.0, zero), unroll=2)
        row = lax.broadcasted_iota(jnp.int32, (SCAN_SEGMENTS, SCAN_W), 0)
        cr, ci = zero, zero
        order = range(SCAN_SEGMENTS - 2, -1, -1) if reverse else range(1, SCAN_SEGMENTS)
        for k in order:
            src = k + 1 if reverse else k - 1
            tr = er + pr * cr - pi * ci
            ti = ei + pr * ci + pi * cr
            cr = jnp.where(row == k, jnp.sum(jnp.where(row == src, tr, 0.0), axis=0, keepdims=True), cr)
            ci = jnp.where(row == k, jnp.sum(jnp.where(row == src, ti, 0.0), axis=0, keepdims=True), ci)

        def fix_at(j, c, before):
            pr, pi, sr, si = c
            pr, pi = ar * pr - ai * pi, ar * pi + ai * pr
            hr = hr_s[j] + pr * cr - pi * ci
            hi = hi_s[j] + pr * ci + pi * cr
            hr_s[j] = hr
            hi_s[j] = hi
            if reverse:
                qr, qi = before
                sr = sr + hr * qr + hi * qi
                si = si + hi * qr - hr * qi
            return pr, pi, sr, si

        start = (zero + 1.0, zero, zero, zero)
        if reverse:
            def fix(s, c):
                j = T - 1 - s
                return fix_at(j, c, (pr_s[j - 1], pi_s[j - 1]))

            c = lax.fori_loop(0, T - 1, fix, start, unroll=2)
            last_r = jnp.where(row == 0, 0.0, pltpu.roll(pr_s[T - 1], 1, 0))
            last_i = jnp.where(row == 0, 0.0, pltpu.roll(pi_s[T - 1], 1, 0))
            _, _, sr, si = fix_at(0, c, (last_r, last_i))
            dar_ref[...] = sr
            dai_ref[...] = si
        else:
            lax.fori_loop(0, T, lambda s, c: fix_at(s, c, None), start, unroll=2)
        stores = []
        for k in range(SCAN_SEGMENTS):
            rows = pl.ds(k * T, T)
            stores.append(pltpu.make_async_copy(hr_s.at[:, k, :], hr_hbm.at[rows, pl.ds(col, SCAN_W)], out_sems.at[k]))
            stores.append(pltpu.make_async_copy(hi_s.at[:, k, :], hi_hbm.at[rows, pl.ds(col, SCAN_W)],
                                                out_sems.at[SCAN_SEGMENTS + k]))
        for cp in stores:
            cp.start()
        for cp in stores:
            cp.wait()

    a_re = pl.BlockSpec((SCAN_SEGMENTS, SCAN_W), lambda b: (0, b))
    a_im = pl.BlockSpec((SCAN_SEGMENTS, SCAN_W), lambda b: (0, nblk + b))
    seq = pltpu.VMEM((T, SCAN_SEGMENTS, SCAN_W), f32)
    if reverse:
        in_specs, args = [ANY, ANY, ANY, a_re, a_im], [x, h[0], h[1], a_rows, a_rows]
        out_specs = [ANY, ANY, a_re, a_re]
        out_shape = [jax.ShapeDtypeStruct((S, NL), f32)] * 2 + [jax.ShapeDtypeStruct((SCAN_SEGMENTS, NL), f32)] * 2
    else:
        in_specs, args = [ANY, a_re, a_im], [x, a_rows, a_rows]
        out_specs = [ANY, ANY]
        out_shape = [jax.ShapeDtypeStruct((S, NL), f32)] * 2
    scratch = [seq] * (n_in + 2) + [pltpu.SemaphoreType.DMA((n_in * SCAN_SEGMENTS,)),
                                    pltpu.SemaphoreType.DMA((2 * SCAN_SEGMENTS,))]
    return pl.pallas_call(body, name=name, grid=(nblk,), in_specs=in_specs, out_specs=out_specs, out_shape=out_shape,
                          scratch_shapes=scratch, compiler_params=_cparams(('arbitrary',), big=True))(*args)


def _f_s5_post(y, u, d, w_glu):
    z = _bmm(y + d * u, w_glu)
    return (z[:, :GROUP_W] * jax.nn.sigmoid(z[:, GROUP_W:]),)


def _block_diag(t):
    G, a, b = t.shape
    eye = jnp.eye(G, dtype=t.dtype)
    return (t[:, :, None, :] * eye[:, None, :, None]).reshape(G * a, G * b)


def _diag_blocks(m, a, b):
    G = m.shape[0] // a
    return jnp.moveaxis(jnp.diagonal(m.reshape(G, a, G, b), axis1=0, axis2=2), -1, 0)


def _s5_fwd(u, mp, l):
    S = u.shape[0]
    tm = _pick(S, (256, 128))
    bre = _block_diag(mp['s5_b_re'].transpose(0, 2, 1))
    bim = _block_diag(mp['s5_b_im'].transpose(0, 2, 1))
    expand = jnp.repeat(jnp.eye(S5_G, dtype=f32), S5_P, axis=1)
    prep_params = [mp['s5_lambda_re'].reshape(1, S5_LANES), mp['s5_lambda_im'].reshape(1, S5_LANES),
                   mp['s5_log_dt'].reshape(S5_G, 1), expand]
    bb, a_rows = _tile_fwd(_f_s5_prep, [bre, bim], prep_params, [(2 * S5_LANES, f32)] * 2, GROUP_W, f's5_prep_fwd_{l}')
    x = _mm(u, bb, 'nn', f's5_in_fwd_{l}')
    hr, hi = _s5_scan(x, a_rows, f's5_scan_fwd_{l}')
    c_re, c_im = _block_diag(mp['s5_c_re'].transpose(0, 2, 1)), -_block_diag(mp['s5_c_im'].transpose(0, 2, 1))
    y = _mm(hi, c_im, 'nn', f's5_out_im_fwd_{l}', add=_mm(hr, c_re, 'nn', f's5_out_re_fwd_{l}'))
    post_params = [mp['s5_d'][None], mp['s5_w_glu']]
    (out,) = _tile_fwd(_f_s5_post, [y, u], post_params, [(GROUP_W, f32)], tm, f's5_post_fwd_{l}')
    return out, (u, bre, bim, prep_params, bb, a_rows, hr, hi, c_re, c_im, y, post_params)


def _s5_bwd(dout, saved, l):
    u, bre, bim, prep_params, bb, a_rows, hr, hi, c_re, c_im, y, post_params = saved
    S = u.shape[0]
    tm = _pick(S, (256, 128))
    (dy, du1), (dd, dwglu) = _tile_bwd(_f_s5_post, [y, u], post_params, [dout], [True, True], [True, True], tm,
                                       f's5_post_bwd_{l}')
    ccat = jnp.concatenate([c_re, c_im], axis=0)
    dh = _mm(dy, ccat, 'nt', f's5_out_dx_{l}')
    dccat = jnp.concatenate([_mm(hr, dy, 'tn', f's5_out_re_dw_{l}'), _mm(hi, dy, 'tn', f's5_out_im_dw_{l}')], axis=0)
    lr_, li_, dar, dai = _s5_scan(dh, a_rows, f's5_scan_bwd_{l}', reverse=True, h=(hr, hi))
    du2 = _mm(li_, bb[:, S5_LANES:], 'nt', f's5_in_im_dx_{l}', add=_mm(lr_, bb[:, :S5_LANES], 'nt', f's5_in_re_dx_{l}'))
    dbb = jnp.concatenate([_mm(u, lr_, 'tn', f's5_in_re_dw_{l}'), _mm(u, li_, 'tn', f's5_in_im_dw_{l}')], axis=1)
    da_rows = jnp.pad(jnp.concatenate([dar, dai], axis=1), ((0, GROUP_W - SCAN_SEGMENTS), (0, 0)))
    (dbre, dbim), (dlr, dli, dlogdt) = _tile_bwd(_f_s5_prep, [bre, bim], prep_params, [dbb, da_rows], [True, True],
                                                 [True, True, True, False], GROUP_W, f's5_prep_bwd_{l}')
    grads = {
        's5_lambda_re': dlr.reshape(S5_G, S5_P), 's5_lambda_im': dli.reshape(S5_G, S5_P), 's5_log_dt': dlogdt[:, 0],
        's5_b_re': _diag_blocks(dbre, S5_CG, S5_P).transpose(0, 2, 1),
        's5_b_im': _diag_blocks(dbim, S5_CG, S5_P).transpose(0, 2, 1),
        's5_c_re': _diag_blocks(dccat[:S5_LANES], S5_P, S5_CG).transpose(0, 2, 1),
        's5_c_im': -_diag_blocks(dccat[S5_LANES:], S5_P, S5_CG).transpose(0, 2, 1),
        's5_d': dd[0], 's5_w_glu': dwglu}
    return du1 + du2, grads


DN_CONV = 4


def _head_sum_matrix():
    h = np.arange(GROUP_W) // HEAD_DIM
    return jnp.asarray((h[:, None] == h[None, :]).astype(np.float32))


def _f_dn_pre(x0, x1, x2, x3, ab, w0, w1, w2, w3, alog, dtb, ea, eb, hs):
    c = w0 * x0 + w1 * x1 + w2 * x2 + w3 * x3
    s = c * jax.nn.sigmoid(c)
    q, k, v = s[:, :GROUP_W], s[:, GROUP_W:2 * GROUP_W], s[:, 2 * GROUP_W:]
    q = q * lax.rsqrt(_hdot(q * q, hs) + EPS) * (HEAD_DIM ** -0.5)
    k = k * lax.rsqrt(_hdot(k * k, hs) + EPS)
    beta = jax.nn.sigmoid(_hdot(ab, eb))
    g = -jnp.exp(alog) * jax.nn.softplus(_hdot(ab, ea) + dtb)
    return q, k, v, g, beta


DN_CHUNKS_PER_STEP = 8


def _f_dn_chunks(q, k, v, g, beta):
    C = DN_CHUNK
    n_chunks = q.shape[0] // C
    r = lax.broadcasted_iota(jnp.int32, (C, C), 0)
    c = lax.broadcasted_iota(jnp.int32, (C, C), 1)
    causal, strict = r >= c, r > c
    eye = (r == c).astype(f32)
    tril = causal.astype(f32)
    ones = jnp.ones((C, GROUP_W), f32)
    masks = [_head_mask(h) for h in range(N_HEADS)]
    rows = [tuple(t[i * C:(i + 1) * C] for t in (q, k, v, g, beta)) for i in range(n_chunks)]
    gcs = [_hdot(tril, gi) for (_, _, _, gi, _) in rows]
    items = [(i, h) for i in range(n_chunks) for h in range(N_HEADS)]
    grows = [_hdot_nt(ones * (masks[h] * (1.0 / HEAD_DIM)), gcs[i]) for i, h in items]
    decs = []
    for (i, h), grow in zip(items, grows):
        gcol = jnp.sum(gcs[i] * masks[h], axis=1, keepdims=True) * (1.0 / HEAD_DIM)
        decs.append(jnp.exp(jnp.where(causal, gcol - grow, NEG_INF)))
    kbs = [ki * bi for (_, ki, _, _, bi) in rows]
    kks = [_bmm_nt(kbs[i] * masks[h], rows[i][1]) for i, h in items]
    qks = [_bmm_nt(rows[i][0] * masks[h], rows[i][1]) for i, h in items]
    lmats = [jnp.where(strict, kk * dec, 0.0) for kk, dec in zip(kks, decs)]
    a_qk = [jnp.where(causal, qk * dec, 0.0) for qk, dec in zip(qks, decs)]
    ts = [eye - lm for lm in lmats]
    ps = lmats
    for _ in range(5):
        ps = [_bmm(p, p) for p in ps]
        ts = [t + _bmm(t, p) for t, p in zip(ts, ps)]
    egs = [jnp.exp(gc) for gc in gcs]
    tw = [_bmm(t, kbs[i] * egs[i]) for (i, h), t in zip(items, ts)]
    tu = [_bmm(t, rows[i][2] * rows[i][4]) for (i, h), t in zip(items, ts)]
    outs = []
    for i in range(n_chunks):
        qi, ki, _, gi, _ = rows[i]
        glast = jnp.sum(gi, axis=0, keepdims=True)
        w = sum(tw[i * N_HEADS + h] * masks[h] for h in range(N_HEADS))
        u = sum(tu[i * N_HEADS + h] * masks[h] for h in range(N_HEADS))
        outs.append((w, u, qi * egs[i], ki * jnp.exp(glast - gcs[i]), *a_qk[i * N_HEADS:(i + 1) * N_HEADS],
                     jnp.broadcast_to(jnp.exp(glast), (C, GROUP_W))))
    return tuple(jnp.concatenate(parts, axis=0) for parts in zip(*outs))


def _f_dn_step(w, u, qd, kdec, a0, a1, a2, a3, dfull, state, bd):
    row0 = (lax.broadcasted_iota(jnp.int32, dfull.shape, 0) == 0).astype(f32)
    dvec = jnp.sum(dfull * row0, axis=0, keepdims=True)
    ws, qs = _bmm(w, state), _bmm(qd, state)
    vnew = u - ws
    avs = [_bmm(a, vnew) for a in (a0, a1, a2, a3)]
    kv = _bmm_tn(kdec, vnew)
    o = qs + sum(av * _head_mask(h) for h, av in enumerate(avs))
    return o, state * dvec + bd * kv


def _dn_scan_fwd(ins, name):
    S = ins[0].shape[0]
    N = S // DN_CHUNK
    bd = _head_sum_matrix()

    def body(*refs):
        o_ref, s_ref, state = refs[10], refs[11], refs[12]

        @pl.when(pl.program_id(0) == 0)
        def _():
            state[...] = jnp.zeros_like(state)

        s_in = state[...]
        s_ref[0] = s_in
        o, s_out = _f_dn_step(*[r[...] for r in refs[:9]], s_in, refs[9][...])
        o_ref[...] = o
        state[...] = s_out

    return pl.pallas_call(
        body, name=name, grid=(N,),
        in_specs=[pl.BlockSpec((DN_CHUNK, t.shape[1]), lambda n: (n, 0)) for t in ins] + [_full_spec(bd)],
        out_specs=[pl.BlockSpec((DN_CHUNK, GROUP_W), lambda n: (n, 0)), pl.BlockSpec((1, GROUP_W, GROUP_W), lambda n: (n, 0, 0))],
        out_shape=[jax.ShapeDtypeStruct((S, GROUP_W), f32), jax.ShapeDtypeStruct((N, GROUP_W, GROUP_W), f32)],
        scratch_shapes=[pltpu.VMEM((GROUP_W, GROUP_W), f32)],
        compiler_params=_cparams(('arbitrary',)),
    )(*ins, bd)


def _dn_scan_bwd(ins, states, do, name):
    S = ins[0].shape[0]
    N = S // DN_CHUNK
    bd = _head_sum_matrix()

    def body(*refs):
        s_ref, do_ref = refs[9], refs[10]
        bd_ref = refs[11]
        outs = refs[12:21]
        dstate = refs[21]

        @pl.when(pl.program_id(0) == 0)
        def _():
            dstate[...] = jnp.zeros_like(dstate)

        bd_val = bd_ref[...]
        _, vjp = jax.vjp(lambda *a: _f_dn_step(*a, bd_val), *[r[...] for r in refs[:9]], s_ref[0])
        grads = vjp((do_ref[...], dstate[...]))
        for o, g in zip(outs, grads[:9]):
            o[...] = g
        dstate[...] = grads[9]

    def rev(n):
        return (N - 1 - n, 0)

    res = pl.pallas_call(
        body, name=name, grid=(N,),
        in_specs=[pl.BlockSpec((DN_CHUNK, t.shape[1]), rev) for t in ins] +
                 [pl.BlockSpec((1, GROUP_W, GROUP_W), lambda n: (N - 1 - n, 0, 0)), pl.BlockSpec((DN_CHUNK, GROUP_W), rev),
                  _full_spec(bd)],
        out_specs=[pl.BlockSpec((DN_CHUNK, t.shape[1]), rev) for t in ins],
        out_shape=[jax.ShapeDtypeStruct(t.shape, f32) for t in ins],
        scratch_shapes=[pltpu.VMEM((GROUP_W, GROUP_W), f32)],
        compiler_params=_cparams(('arbitrary',)),
    )(*ins, states, do, bd)
    return list(res)


def _f_dn_post(o, gate, gain, hmean):
    return (o * lax.rsqrt(_hdot(o * o, hmean) + EPS) * gain * (gate * jax.nn.sigmoid(gate)),)


def _dn_delays(x, name):
    S, C = x.shape
    tm = _pick(S, (256, 128))

    def body(prev_ref, cur_ref, *outs):
        before = jnp.where(pl.program_id(0) > 0, prev_ref[...], 0.0)
        both = jnp.concatenate([before, cur_ref[...]], axis=0)
        for o, k in zip(outs, range(DN_CONV - 1, 0, -1)):
            o[...] = pltpu.roll(both, k, 0)[tm:]

    spec = pl.BlockSpec((tm, C), lambda i: (i, 0))
    return pl.pallas_call(
        body, name=name, grid=(S // tm,),
        in_specs=[pl.BlockSpec((tm, C), lambda i: (jnp.maximum(i - 1, 0), 0)), spec],
        out_specs=[spec] * (DN_CONV - 1), out_shape=[jax.ShapeDtypeStruct((S, C), x.dtype)] * (DN_CONV - 1),
        compiler_params=_cparams(('parallel',), big=True),
    )(x, x)


def _dn_undelay_sum(ds, name):
    S, C = ds[0].shape
    tm = _pick(S, (256, 128))
    n = S // tm

    def body(*refs):
        o = refs[-1]
        total = refs[2 * (DN_CONV - 1)][...]
        for j in range(DN_CONV - 1):
            k = DN_CONV - 1 - j
            after = jnp.where(pl.program_id(0) < n - 1, refs[2 * j + 1][...], 0.0)
            both = jnp.concatenate([refs[2 * j][...], after], axis=0)
            total = total + pltpu.roll(both, 2 * tm - k, 0)[:tm]
        o[...] = total

    spec = pl.BlockSpec((tm, C), lambda i: (i, 0))
    nxt = pl.BlockSpec((tm, C), lambda i: (jnp.minimum(i + 1, n - 1), 0))
    args, in_specs = [], []
    for j in range(DN_CONV - 1):
        args += [ds[j], ds[j]]
        in_specs += [spec, nxt]
    return pl.pallas_call(
        body, name=name, grid=(n,), in_specs=in_specs + [spec], out_specs=spec,
        out_shape=jax.ShapeDtypeStruct((S, C), f32), compiler_params=_cparams(('parallel',), big=True),
    )(*args, ds[DN_CONV - 1])


def _dn_fwd(qkv, a, b, gate, mp, l):
    S = qkv.shape[0]
    tm = _pick(S, (256, 128))
    xs = [*_dn_delays(qkv, f'dn_delay_{l}'), qkv]
    ab = jnp.pad(jnp.concatenate([a, b], axis=1), ((0, 0), (0, LANES - 2 * N_HEADS)))
    sel = np.zeros((2, LANES, GROUP_W), np.float32)
    for h in range(N_HEADS):
        sel[0, h, h * HEAD_DIM:(h + 1) * HEAD_DIM] = 1.0
        sel[1, N_HEADS + h, h * HEAD_DIM:(h + 1) * HEAD_DIM] = 1.0
    pre_params = [*[mp['dn_conv'][j][None] for j in range(DN_CONV)], jnp.repeat(mp['dn_a_log'], HEAD_DIM)[None],
                  jnp.repeat(mp['dn_dt_bias'], HEAD_DIM)[None], jnp.asarray(sel[0]), jnp.asarray(sel[1]), _head_sum_matrix()]
    pre = _tile_fwd(_f_dn_pre, [*xs, ab], pre_params, [(GROUP_W, f32)] * 5, tm, f'dn_pre_fwd_{l}')
    chunk_outs = [(GROUP_W, f32)] * 4 + [(HEAD_DIM, f32)] * 4 + [(GROUP_W, f32)]
    parts = _tile_fwd(_f_dn_chunks, pre, [], chunk_outs, DN_CHUNK * DN_CHUNKS_PER_STEP, f'dn_chunk_fwd_{l}')
    o, states = _dn_scan_fwd(parts, f'dn_scan_fwd_{l}')
    post_params = [jnp.tile(mp['dn_o_norm'], N_HEADS)[None], _head_mean_matrix()]
    (y,) = _tile_fwd(_f_dn_post, [o, gate], post_params, [(GROUP_W, f32)], tm, f'dn_post_fwd_{l}')
    return y, (xs, ab, pre_params, pre, parts, states, o, gate, post_params)


def _dn_bwd(dy, saved, l):
    xs, ab, pre_params, pre, parts, states, o, gate, post_params = saved
    S = dy.shape[0]
    tm = _pick(S, (256, 128))
    (do, dgate), (dgain,) = _tile_bwd(_f_dn_post, [o, gate], post_params, [dy], [True, True], [True, False], tm,
                                      f'dn_post_bwd_{l}')
    dparts = _dn_scan_bwd(parts, states, do, f'dn_scan_bwd_{l}')
    dpre, _ = _tile_bwd(_f_dn_chunks, pre, [], dparts, [True] * 5, [], DN_CHUNK * DN_CHUNKS_PER_STEP, f'dn_chunk_bwd_{l}')
    dins, dpar = _tile_bwd(_f_dn_pre, [*xs, ab], pre_params, dpre, [True] * 5, [True] * 6 + [False] * 3, tm,
                           f'dn_pre_bwd_{l}')
    dqkv = _dn_undelay_sum(dins[:DN_CONV], f'dn_undelay_{l}')
    dab = dins[DN_CONV]
    grads = {'dn_conv': jnp.concatenate(dpar[:DN_CONV], axis=0),
             'dn_a_log': dpar[4].reshape(N_HEADS, HEAD_DIM).sum(1), 'dn_dt_bias': dpar[5].reshape(N_HEADS, HEAD_DIM).sum(1),
             'dn_o_norm': dgain.reshape(N_HEADS, HEAD_DIM).sum(0)}
    return dqkv, dab[:, :N_HEADS], dab[:, N_HEADS:2 * N_HEADS], dgate, grads


def _t5_bucket(dist):
    exact = T5_BUCKETS // 2
    df = jnp.maximum(dist, 1).astype(f32)
    large = exact + (jnp.log(df / exact) / math.log(T5_MAX_DIST / exact) * (T5_BUCKETS - exact)).astype(jnp.int32)
    large = jnp.minimum(large, T5_BUCKETS - 1)
    return jnp.where(dist < exact, dist, large)


def _split_cols(t, sizes):
    out, start = [], 0
    for s in sizes:
        out.append(t[..., start:start + s])
        start += s
    return out


def _mixers_fwd(proj, mp, l):
    c_q, c_kv, k_rope, u_s5, qkv_dil, qkv_dn, a_dn, b_dn, gate_dn = _split_cols(proj, IN_SPLITS)
    y_mla, s_mla = _mla_fwd(c_q, c_kv, k_rope, mp, l)
    y_s5, s_s5 = _s5_fwd(u_s5, mp, l)
    y_dil, s_dil = _dil_fwd(qkv_dil, mp, l)
    y_dn, s_dn = _dn_fwd(qkv_dn, a_dn, b_dn, gate_dn, mp, l)
    return jnp.concatenate([y_mla, y_s5, y_dil, y_dn], axis=-1), (s_mla, s_s5, s_dil, s_dn)


def _mixers_bwd(dmixed, saved, l):
    s_mla, s_s5, s_dil, s_dn = saved
    d_mla, d_s5, d_dil, d_dn = _split_cols(dmixed, (GROUP_W,) * 4)
    dc_q, dc_kv, dk_rope, g_mla = _mla_bwd(d_mla, s_mla, l)
    du, g_s5 = _s5_bwd(d_s5, s_s5, l)
    dqkv_dil, g_dil = _dil_bwd(d_dil, s_dil, l)
    dqkv_dn, da, db, dgate, g_dn = _dn_bwd(d_dn, s_dn, l)
    parts = [dc_q, dc_kv, dk_rope, du, dqkv_dil, dqkv_dn, da, db, dgate]
    dproj = jnp.concatenate([p.astype(bf16) for p in parts], axis=-1)
    return dproj, {**g_mla, **g_s5, **g_dil, **g_dn}


MIXER_PARAMS = ['mla_q_norm', 'mla_kv_norm', 'mla_w_uq', 'mla_w_ukv', 'mla_qk_q', 'mla_qk_k', 's5_lambda_re',
                's5_lambda_im', 's5_log_dt', 's5_b_re', 's5_b_im', 's5_c_re', 's5_c_im', 's5_d', 's5_w_glu',
                'dil_q_norm', 'dil_k_norm', 't5_bias', 'dn_conv', 'dn_a_log', 'dn_dt_bias', 'dn_o_norm']


def _layer_fwd_mix(h, W, l):
    S = h.shape[0]
    tm = _pick(S, (256, 128))
    g1 = W['attn_norm'][l][None]
    (n1,) = _tile_fwd(_f_rms, [h], [g1], [(D_MODEL, bf16)], tm, f'rms1_fwd_{l}')
    proj = _mm(n1, W['w_in'][l], 'nt', f'proj_fwd_{l}')
    mp = {k: (W[k] if k == 't5_bias' else W[k][l]).astype(f32) for k in MIXER_PARAMS}
    if 'dil_tables' in W:
        mp['dil_tables'] = W['dil_tables']
    mixed, mix_saved = _mixers_fwd(proj, mp, l)
    mixed_b = mixed.astype(bf16)
    h2 = _mm(mixed_b, W['w_out'][l], 'nn', f'out_fwd_{l}', add=h)
    return h2, dict(h=h, n1=n1, mix=mix_saved, mixed=mixed_b, h2=h2)


def _layer_fwd_ffn(h2, W, l, saved):
    S = h2.shape[0]
    tm = _pick(S, (256, 128))
    g2 = W['ffn_norm'][l][None]
    (n2,) = _tile_fwd(_f_rms, [h2], [g2], [(D_MODEL, bf16)], tm, f'rms2_fwd_{l}')
    u = _mm(n2, W['ffn_w1'][l], 'nt', f'ffn1_fwd_{l}', out_dtype=bf16)
    v = _mm(n2, W['ffn_w3'][l], 'nt', f'ffn3_fwd_{l}', out_dtype=bf16)
    (act,) = _tile_fwd(_f_swiglu, [u, v], [], [(FFN_HIDDEN, bf16)], tm, f'swiglu_fwd_{l}')
    h3 = _mm(act, W['ffn_w2'][l], 'nn', f'ffn2_fwd_{l}', add=h2)
    saved.update(n2=n2, u=u, v=v, act=act)
    return h3


def _layer_bwd_ffn(dh3, saved, W, l):
    S = dh3.shape[0]
    tm = _pick(S, (256, 128))
    g2 = W['ffn_norm'][l][None]
    grads = {}
    dact = _mm(dh3, W['ffn_w2'][l], 'nt', f'ffn2_dx_{l}', out_dtype=bf16)
    grads['ffn_w2'] = _mm(saved['act'], dh3, 'tn', f'ffn2_dw_{l}', out_dtype=bf16)
    (du, dv), _ = _tile_bwd(_f_swiglu, [saved['u'], saved['v']], [], [dact], [True, True], [], tm, f'swiglu_bwd_{l}',
                            dt_dtypes=[bf16, bf16])
    dn2 = _mm(dv, W['ffn_w3'][l], 'nn', f'ffn3_dx_{l}', add=_mm(du, W['ffn_w1'][l], 'nn', f'ffn1_dx_{l}'))
    grads['ffn_w1'] = _mm(du, saved['n2'], 'tn', f'ffn1_dw_{l}', out_dtype=bf16)
    grads['ffn_w3'] = _mm(dv, saved['n2'], 'tn', f'ffn3_dw_{l}', out_dtype=bf16)
    (dh2n,), (dg2,) = _tile_bwd(_f_rms, [saved['h2']], [g2], [dn2], [True], [True], tm, f'rms2_bwd_{l}')
    grads['ffn_norm'] = dg2[0]
    return (dh3, dh2n), grads


def _layer_bwd_mix(dh2, saved, W, l):
    S = dh2.shape[0]
    tm = _pick(S, (256, 128))
    g1 = W['attn_norm'][l][None]
    grads = {}
    dmixed = _mm(dh2, W['w_out'][l], 'nt', f'out_dx_{l}')
    grads['w_out'] = _mm(saved['mixed'], dh2, 'tn', f'out_dw_{l}', out_dtype=bf16)
    dproj, dmp = _mixers_bwd(dmixed, saved['mix'], l)
    grads.update(dmp)
    dn1 = _mm(dproj, W['w_in'][l], 'nn', f'proj_dx_{l}')
    grads['w_in'] = _mm(dproj, saved['n1'], 'tn', f'proj_dw_{l}', out_dtype=bf16)
    (dh1n,), (dg1,) = _tile_bwd(_f_rms, [saved['h']], [g1], [dn1], [True], [True], tm, f'rms1_bwd_{l}')
    grads['attn_norm'] = dg1[0]
    return (dh2, dh1n), grads


def kernel(x, attn_norm, w_in, w_out, mla_q_norm, mla_kv_norm, mla_w_uq, mla_w_ukv, mla_qk_q, mla_qk_k, s5_lambda_re, s5_lambda_im, s5_log_dt, s5_b_re, s5_b_im, s5_c_re, s5_c_im, s5_d, s5_w_glu, dil_q_norm, dil_k_norm, t5_bias, dn_conv, dn_a_log, dn_dt_bias, dn_o_norm, ffn_norm, ffn_w1, ffn_w3, ffn_w2, loss_target, m_attn_norm, m_w_in, m_w_out, m_mla_q_norm, m_mla_kv_norm, m_mla_w_uq, m_mla_w_ukv, m_mla_qk_q, m_mla_qk_k, m_s5_lambda_re, m_s5_lambda_im, m_s5_log_dt, m_s5_b_re, m_s5_b_im, m_s5_c_re, m_s5_c_im, m_s5_d, m_s5_w_glu, m_dil_q_norm, m_dil_k_norm, m_t5_bias, m_dn_conv, m_dn_a_log, m_dn_dt_bias, m_dn_o_norm, m_ffn_norm, m_ffn_w1, m_ffn_w3, m_ffn_w2, v_attn_norm, v_w_in, v_w_out, v_mla_q_norm, v_mla_kv_norm, v_mla_w_uq, v_mla_w_ukv, v_mla_qk_q, v_mla_qk_k, v_s5_lambda_re, v_s5_lambda_im, v_s5_log_dt, v_s5_b_re, v_s5_b_im, v_s5_c_re, v_s5_c_im, v_s5_d, v_s5_w_glu, v_dil_q_norm, v_dil_k_norm, v_t5_bias, v_dn_conv, v_dn_a_log, v_dn_dt_bias, v_dn_o_norm, v_ffn_norm, v_ffn_w1, v_ffn_w3, v_ffn_w2):
    given = dict(locals())
    def seen(n, t):
        if n in COLUMNS_FIRST:
            return jnp.transpose(t, (2, 0, 1))
        return jnp.swapaxes(t, 1, 2) if n in TRANSPOSED else t

    def given_back(n, t):
        return jnp.transpose(t, (1, 2, 0)) if n in COLUMNS_FIRST else seen(n, t)

    def layer_of(n, t, l):
        return t[:, l] if n in COLUMNS_FIRST else t[l]

    w_loc = {n: seen(n, given[n]) for n in WEIGHTS}
    m_loc = {n: seen(n, given['m_' + n]) for n in WEIGHTS}
    v_loc = {n: seen(n, given['v_' + n]) for n in WEIGHTS}
    big_names = list(BIG)

    own = 2 * lax.axis_index('x') + lax.axis_index('y')
    groups = [[(n, 0) for n in GATHER_FIRST], [(n, 0) for n in GATHER_FFN], [(n, 1) for n in big_names]]
    started, order = [], jnp.zeros((8, LANES), f32)
    for gi, group in enumerate(groups):
        blocks = [layer_of(n, w_loc[n], l).astype(bf16) for n, l in group]
        lands = [lax.empty((N_SHARDS,) + b.shape, bf16) for b in blocks]
        send_sems, recv_sems, blocks, lands, order = _to_chips_start(blocks, lands, False, order, f'gather_start_{gi}')
        started.append((send_sems, recv_sems, blocks, lands))
    W = {n: [None] * DEPTH for n in big_names}
    for n in SMALL:
        W[n] = w_loc[n]
    W['dil_tables'] = _dil_tables(w_loc['t5_bias'])

    def arrive(gi, after):
        send_sems, recv_sems, blocks, lands = started[gi]
        blocks, lands = _to_chips_wait(send_sems, recv_sems, blocks, lands, False, after, f'gather_wait_{gi}')
        for (n, l), block, land in zip(groups[gi], blocks, lands):
            W[n][l] = _from_shards(n, lax.dynamic_update_slice(land, block[None], (own, 0, 0)))

    arrive(0, order)
    h = x[0]
    saved = []
    for l in range(DEPTH):
        h2, sv = _layer_fwd_mix(h, W, l)
        if l == 0:
            arrive(1, h2)
        h = _layer_fwd_ffn(h2, W, l, sv)
        if l == 0:
            arrive(2, h)
        saved.append(sv)
    parts_loss, dh = _loss_head(h, loss_target[0])
    local_loss = jnp.sum(parts_loss)

    layer_grads = [dict() for _ in range(DEPTH)]
    sent = []

    def send(group, tag):
        srcs = [_by_shard(n, layer_grads[l][n]).astype(bf16) for n, l in group]
        lands = [lax.empty((3,) + s.shape[1:], bf16) for s in srcs]
        send_sems, recv_sems, srcs, lands, token = _to_chips_start(srcs, lands, True, jnp.zeros((8, LANES), f32),
                                                                   f'reduce_start_{tag}')
        sent.append((group, tag, send_sems, recv_sems, srcs, lands))
        return token[0, 0]

    for l in reversed(range(DEPTH)):
        (dh3, dh2n), g_ffn = _layer_bwd_ffn(dh, saved[l], W, l)
        layer_grads[l].update(g_ffn)
        dh2 = dh3 + dh2n
        if l == 0:
            dh2 = dh2 + send([(n, 0) for n in GATHER_FFN], 'ffn0')
        (dh2, dh1n), g_mix = _layer_bwd_mix(dh2, saved[l], W, l)
        layer_grads[l].update(g_mix)
        dh = dh2 + dh1n
        if l == 1:
            dh = dh + send([(n, 1) for n in big_names], 'layer1')
    last = send([(n, 0) for n in GATHER_FIRST], 'first0')
    grad_x = dh[None]
    small_full = []
    for n in SMALL:
        if n == 't5_bias':
            small_full.append(_t5_grad([a_ + b_ for a_, b_ in zip(layer_grads[0]['t5_tables'], layer_grads[1]['t5_tables'])]))
        else:
            small_full.append(jnp.stack([layer_grads[l][n] for l in range(DEPTH)]))

    small_shapes = [w_loc[n].shape for n in SMALL] + [(1,)]
    nothing = [jnp.zeros((1,), f32)]
    small_pack = _pack(small_full + [local_loss.reshape(1)]) + last
    _, recv_small = _swap_with_sibling([], small_pack)
    chip_small = _small_chip_sum(small_pack, recv_small)
    _, from_chips_small = _exchange_between_chips([], chip_small)

    mine = {}
    for group, tag, send_sems, recv_sems, srcs, lands in sent:
        srcs, lands = _to_chips_wait(send_sems, recv_sems, srcs, lands, True, from_chips_small, f'reduce_wait_{tag}')
        for (n, l), src, land in zip(group, srcs, lands):
            mine[(n, l)] = _partial_sum(src, land, f'partial_{n}_{l}')
    keys = [(n, l) for n in big_names for l in range(DEPTH)]
    theirs = dict(zip(keys, _swap_partials([mine[k] for k in keys])))

    g_small_p, d_small_p, m_small_p, v_small_p = _small_update(
        small_pack, recv_small, from_chips_small, _pack([w_loc[n] for n in SMALL] + nothing),
        _pack([m_loc[n] for n in SMALL] + nothing), _pack([v_loc[n] for n in SMALL] + nothing))
    loss = _unpack(g_small_p, small_shapes)[-1][0]
    grad, delta, new_m, new_v = {}, {}, {}, {}
    for n, g_, d_, m_, v_ in zip(SMALL, _unpack(g_small_p, small_shapes), _unpack(d_small_p, small_shapes),
                                 _unpack(m_small_p, small_shapes), _unpack(v_small_p, small_shapes)):
        grad[n], delta[n], new_m[n], new_v[n] = g_, d_, m_, v_
    for n in big_names:
        update = _adamw_layer_in_the_middle if n in COLUMNS_FIRST else _adamw
        results = update(w_loc[n], m_loc[n], v_loc[n], [mine[(n, l)] for l in range(DEPTH)],
                         [theirs[(n, l)] for l in range(DEPTH)], 'adamw_' + n)
        grad[n], delta[n], new_m[n], new_v[n] = [given_back(n, t) for t in results]
    return (loss, grad_x, *[grad[n] for n in WEIGHTS], *[delta[n] for n in WEIGHTS],
            *[new_m[n] for n in WEIGHTS], *[new_v[n] for n in WEIGHTS])
```

```python
import functools
import math

import numpy as np
import jax
import jax.numpy as jnp
from jax import lax
from jax.experimental import pallas as pl
from jax.experimental.pallas import tpu as pltpu

f32 = jnp.float32
bf16 = jnp.bfloat16
HI = lax.Precision.HIGHEST
MESH = pl.DeviceIdType.MESH

VMEM_LIMIT_BYTES = 48 * 1024 * 1024
MM_VMEM_BUDGET_BYTES = 32 * 1024 * 1024
LANES = 128

D_MODEL = 1024
DEPTH = 2
GROUP_W = 256
HEAD_DIM = 64
EPS = 1e-6
NEG_INF = -1e30
N_HEADS = 4
MLA_NOPE, MLA_ROPE = 64, 32
MLA_DQK = MLA_NOPE + MLA_ROPE
ROPE_THETA = 10000.0
Q_BLOCK = 128
S5_G, S5_CG, S5_P = 16, 16, 64
DIL_PAIRS = ((128, 1), (512, 4), (2048, 16))
T5_BUCKETS, T5_MAX_DIST = 32, 2048
DN_CHUNK = 64
FFN_HIDDEN = 2816
IN_SPLITS = (256, 128, 32, 256, 768, 768, 4, 4, 256)
IN_COLS = sum(IN_SPLITS)

ADAM_LR, ADAM_B1, ADAM_B2, ADAM_EPS, ADAM_WD, ADAM_STEP = 0.001, 0.9, 0.999, 1e-08, 0.01, 10

WEIGHTS = ['attn_norm', 'w_in', 'w_out', 'mla_q_norm', 'mla_kv_norm', 'mla_w_uq', 'mla_w_ukv', 'mla_qk_q', 'mla_qk_k',
           's5_lambda_re', 's5_lambda_im', 's5_log_dt', 's5_b_re', 's5_b_im', 's5_c_re', 's5_c_im', 's5_d', 's5_w_glu',
           'dil_q_norm', 'dil_k_norm', 't5_bias', 'dn_conv', 'dn_a_log', 'dn_dt_bias', 'dn_o_norm', 'ffn_norm',
           'ffn_w1', 'ffn_w3', 'ffn_w2']
BIG = {'w_in': 1, 'w_out': 1, 'mla_w_uq': 2, 'mla_w_ukv': 2, 's5_w_glu': 2, 'dn_conv': 2, 'ffn_w1': 1, 'ffn_w3': 1,
       'ffn_w2': 1}
TRANSPOSED = ('ffn_w1', 'ffn_w3')
COLUMNS_FIRST = ('w_in',)
SMALL = [n for n in WEIGHTS if n not in BIG]
GATHER_FIRST = ['w_in', 'mla_w_uq', 'mla_w_ukv', 's5_w_glu', 'dn_conv', 'w_out']
GATHER_FFN = ['ffn_w1', 'ffn_w3', 'ffn_w2']
N_SHARDS = 4
PACK_COLS = 1024


def _cparams(sem=None, big=False):
    kw = {}
    if sem is not None:
        kw['dimension_semantics'] = sem
    if big:
        kw['vmem_limit_bytes'] = VMEM_LIMIT_BYTES
    return pltpu.CompilerParams(**kw)


def _pick(n, prefs):
    for p in prefs:
        if p <= n and n % p == 0:
            return p
    return n


def _lane_tile(n, cap):
    for t in range(cap - cap % LANES, 0, -LANES):
        if n % t == 0:
            return t
    return n


def _mm(a, b, mode, name, add=None, out_dtype=f32):
    if mode == 'nn':
        (M, K), (K2, N) = a.shape, b.shape
    elif mode == 'nt':
        (M, K), (N, K2) = a.shape, b.shape
    else:
        (K, M), (K2, N) = a.shape, b.shape
    assert K == K2, (name, a.shape, b.shape)
    tk = K if K <= 2816 else _pick(K, (2816, 2048, 1408, 1024, 512))
    cap_m, cap_n = (1408 if mode == 'tn' else 512), 1408

    def need(tm_, tn_):
        per_step = tm_ * tk * a.dtype.itemsize + tk * tn_ * b.dtype.itemsize + tm_ * tn_ * jnp.dtype(out_dtype).itemsize
        if add is not None:
            per_step += tm_ * tn_ * add.dtype.itemsize
        return 2 * per_step + tm_ * tn_ * 4

    tm, tn = _lane_tile(M, cap_m), _lane_tile(N, cap_n)
    while need(tm, tn) > MM_VMEM_BUDGET_BYTES and cap_m > LANES:
        cap_m //= 2
        tm = _lane_tile(M, cap_m)
    while need(tm, tn) > MM_VMEM_BUDGET_BYTES and cap_n > LANES:
        cap_n //= 2
        tn = _lane_tile(N, cap_n)
    nk = K // tk
    dims = {'nn': (((1,), (0,)), ((), ())), 'nt': (((1,), (1,)), ((), ())), 'tn': (((0,), (0,)), ((), ()))}[mode]
    has_add = add is not None

    def body(*refs):
        a_ref, b_ref = refs[0], refs[1]
        add_ref = refs[2] if has_add else None
        o_ref = refs[3] if has_add else refs[2]
        part = lax.dot_general(a_ref[...].astype(bf16), b_ref[...].astype(bf16), dims, preferred_element_type=f32)
        if nk == 1:
            if has_add:
                part = part + add_ref[...].astype(f32)
            o_ref[...] = part.astype(out_dtype)
        else:
            acc_ref = refs[-1]
            k = pl.program_id(2)

            @pl.when(k == 0)
            def _():
                acc_ref[...] = part

            @pl.when(k > 0)
            def _():
                acc_ref[...] += part

            @pl.when(k == nk - 1)
            def _():
                r = acc_ref[...]
                if has_add:
                    r = r + add_ref[...].astype(f32)
                o_ref[...] = r.astype(out_dtype)

    if mode == 'nn':
        a_spec = pl.BlockSpec((tm, tk), lambda i, j, k: (i, k))
        b_spec = pl.BlockSpec((tk, tn), lambda i, j, k: (k, j))
    elif mode == 'nt':
        a_spec = pl.BlockSpec((tm, tk), lambda i, j, k: (i, k))
        b_spec = pl.BlockSpec((tn, tk), lambda i, j, k: (j, k))
    else:
        a_spec = pl.BlockSpec((tk, tm), lambda i, j, k: (k, i))
        b_spec = pl.BlockSpec((tk, tn), lambda i, j, k: (k, j))
    in_specs = [a_spec, b_spec]
    args = [a, b]
    if has_add:
        in_specs.append(pl.BlockSpec((tm, tn), lambda i, j, k: (i, j)))
        args.append(add)
    return pl.pallas_call(
        body, name=name, grid=(M // tm, N // tn, nk), in_specs=in_specs,
        out_specs=pl.BlockSpec((tm, tn), lambda i, j, k: (i, j)),
        out_shape=jax.ShapeDtypeStruct((M, N), out_dtype),
        scratch_shapes=[pltpu.VMEM((tm, tn), f32)] if nk > 1 else [],
        compiler_params=_cparams(('parallel', 'parallel', 'arbitrary'), big=True),
    )(*args)


def _full_spec(p):
    nd = p.ndim
    return pl.BlockSpec(p.shape, lambda i, _nd=nd: (0,) * _nd)


def _tile_fwd(f, tiled, params, outs, tm, name):
    S = tiled[0].shape[0]
    nt, npar = len(tiled), len(params)

    def body(*refs):
        vals = [r[...].astype(f32) for r in refs[:nt + npar]]
        res = f(*vals)
        for r, o in zip(res, refs[nt + npar:]):
            o[...] = r.astype(o.dtype)

    return pl.pallas_call(
        body, name=name, grid=(S // tm,),
        in_specs=[pl.BlockSpec((tm, t.shape[1]), lambda i: (i, 0)) for t in tiled] + [_full_spec(p) for p in params],
        out_specs=[pl.BlockSpec((tm, c), lambda i: (i, 0)) for c, _ in outs],
        out_shape=[jax.ShapeDtypeStruct((S, c), dt) for c, dt in outs],
        compiler_params=_cparams(('parallel',), big=True),
    )(*tiled, *params)


def _tile_bwd(f, tiled, params, cts, diff_t, diff_p, tm, name, dt_dtypes=None):
    S = tiled[0].shape[0]
    nt, npar, nc = len(tiled), len(params), len(cts)
    it = [i for i in range(nt) if diff_t[i]]
    ip = [i for i in range(npar) if diff_p[i]]
    if dt_dtypes is None:
        dt_dtypes = [f32] * len(it)

    def body(*refs):
        vals = [r[...].astype(f32) for r in refs[:nt + npar]]
        ct_vals = tuple(r[...].astype(f32) for r in refs[nt + npar:nt + npar + nc])
        out_refs = refs[nt + npar + nc:]

        def g(*dv):
            full = list(vals)
            for k, i in enumerate(it):
                full[i] = dv[k]
            for k, i in enumerate(ip):
                full[nt + i] = dv[len(it) + k]
            return tuple(f(*full))

        _, vjp = jax.vjp(g, *[vals[i] for i in it], *[vals[nt + i] for i in ip])
        grads = vjp(ct_vals)
        for k in range(len(it)):
            out_refs[k][...] = grads[k].astype(out_refs[k].dtype)
        step = pl.program_id(0)
        for k in range(len(ip)):
            o = out_refs[len(it) + k]
            gk = grads[len(it) + k]

            @pl.when(step == 0)
            def _(o=o, gk=gk):
                o[...] = gk

            @pl.when(step > 0)
            def _(o=o, gk=gk):
                o[...] += gk

    out_specs = [pl.BlockSpec((tm, tiled[i].shape[1]), lambda i_: (i_, 0)) for i in it] + [_full_spec(params[i]) for i in ip]
    out_shape = [jax.ShapeDtypeStruct(tiled[i].shape, dt_dtypes[k]) for k, i in enumerate(it)] + \
                [jax.ShapeDtypeStruct(params[i].shape, f32) for i in ip]
    res = pl.pallas_call(
        body, name=name, grid=(S // tm,),
        in_specs=[pl.BlockSpec((tm, t.shape[1]), lambda i: (i, 0)) for t in tiled] + [_full_spec(p) for p in params] +
                 [pl.BlockSpec((tm, c.shape[1]), lambda i: (i, 0)) for c in cts],
        out_specs=out_specs, out_shape=out_shape,
        compiler_params=_cparams(('arbitrary',), big=True),
    )(*tiled, *params, *cts)
    return list(res[:len(it)]), list(res[len(it):])


def _rms(x, g):
    return x * lax.rsqrt(jnp.mean(x * x, axis=-1, keepdims=True) + EPS) * g


def _f_rms(x, g):
    return (_rms(x, g),)


def _f_swiglu(u, v):
    return (u * jax.nn.sigmoid(u) * v,)


def _loss_head(y, target):
    S, D = y.shape
    tm = _pick(S, (256, 128))

    def body(y_ref, t_ref, part_ref, dy_ref):
        e = y_ref[...] - t_ref[...]
        dy_ref[...] = e * (1.0 / D)
        s = 0.5 * jnp.sum(jnp.sum(e * e, axis=1, keepdims=True), axis=0, keepdims=True) * (1.0 / D)
        r = lax.broadcasted_iota(jnp.int32, (8, LANES), 0)
        c = lax.broadcasted_iota(jnp.int32, (8, LANES), 1)
        part_ref[0] = jnp.where((r == 0) & (c == 0), s, 0.0)

    return pl.pallas_call(
        body, name='loss_head', grid=(S // tm,),
        in_specs=[pl.BlockSpec((tm, D), lambda i: (i, 0))] * 2,
        out_specs=[pl.BlockSpec((1, 8, LANES), lambda i: (i, 0, 0)), pl.BlockSpec((tm, D), lambda i: (i, 0))],
        out_shape=[jax.ShapeDtypeStruct((S // tm, 8, LANES), f32), jax.ShapeDtypeStruct((S, D), f32)],
        compiler_params=_cparams(('parallel',)),
    )(y, target)


def _pack_rows_of(shape):
    rows = -(-math.prod(shape) // PACK_COLS)
    return -(-rows // 8) * 8


def _pack(arrs):
    parts = []
    for a in arrs:
        rows = _pack_rows_of(a.shape)
        flat = a.astype(f32).reshape(-1)
        parts.append(jnp.pad(flat, (0, rows * PACK_COLS - flat.shape[0])).reshape(rows, PACK_COLS))
    return jnp.concatenate(parts, axis=0)


def _unpack(pack, shapes):
    out, row = [], 0
    for s in shapes:
        rows = _pack_rows_of(s)
        out.append(pack[row:row + rows].reshape(-1)[:math.prod(s)].reshape(s))
        row += rows
    return out


ANY = pl.BlockSpec(memory_space=pl.ANY)


def _place():
    return lax.axis_index('x'), lax.axis_index('y'), lax.axis_index('c')


def _where():
    return jnp.stack([lax.axis_index('c'), 2 * lax.axis_index('x') + lax.axis_index('y')]).astype(jnp.int32)


def _remote(src, dst, send_sems, recv_sems, k, to):
    return pltpu.make_async_remote_copy(src_ref=src, dst_ref=dst, send_sem=send_sems.at[k], recv_sem=recv_sems.at[k],
                                        device_id=to, device_id_type=MESH)


def _swap_with_sibling(gs, small):
    n = len(gs)

    def body(*refs):
        g_refs, s_ref = refs[:n], refs[n]
        r_refs, rs_ref = refs[n + 1:2 * n + 1], refs[2 * n + 1]
        send_sems, recv_sems = refs[2 * n + 2:]
        x, y, c = _place()
        sib = (x, y, 1 - c)
        cps = [_remote(g_refs[t].at[:, 1 - c], r_refs[t], send_sems, recv_sems, t, sib) for t in range(n)]
        cps.append(_remote(s_ref, rs_ref, send_sems, recv_sems, n, sib))
        for cp in cps:
            cp.start()
        for cp in cps:
            cp.wait()

    res = pl.pallas_call(
        body, name='swap_with_sibling', in_specs=[ANY] * (n + 1), out_specs=[ANY] * (n + 1),
        out_shape=[jax.ShapeDtypeStruct((N_SHARDS,) + g.shape[2:], g.dtype) for g in gs] +
                  [jax.ShapeDtypeStruct(small.shape, small.dtype)],
        scratch_shapes=[pltpu.SemaphoreType.DMA((n + 1,)), pltpu.SemaphoreType.DMA((n + 1,))],
    )(*gs, small)
    return list(res[:n]), res[n]


def _exchange_between_chips(cs, small):
    n = len(cs)

    def body(*refs):
        c_refs, s_ref = refs[:n], refs[n]
        r_refs, rs_ref = refs[n + 1:2 * n + 1], refs[2 * n + 1]
        send_sems, recv_sems = refs[2 * n + 2:]
        x, y, c = _place()
        chips = [(1 - x, y), (x, 1 - y), (1 - x, 1 - y)]
        cps = []
        for j, (px, py) in enumerate(chips):
            for t in range(n):
                cps.append(_remote(c_refs[t].at[2 * px + py], r_refs[t].at[j], send_sems, recv_sems, 3 * t + j, (px, py, c)))
            cps.append(_remote(s_ref, rs_ref.at[j], send_sems, recv_sems, 3 * n + j, (px, py, c)))
        for cp in cps:
            cp.start()
        for cp in cps:
            cp.wait()

    res = pl.pallas_call(
        body, name='exchange_between_chips', in_specs=[ANY] * (n + 1), out_specs=[ANY] * (n + 1),
        out_shape=[jax.ShapeDtypeStruct((3,) + c.shape[1:], c.dtype) for c in cs] +
                  [jax.ShapeDtypeStruct((3,) + small.shape, small.dtype)],
        scratch_shapes=[pltpu.SemaphoreType.DMA((3 * n + 3,)), pltpu.SemaphoreType.DMA((3 * n + 3,))],
    )(*cs, small)
    return list(res[:n]), res[n]


def _swap_partials(ts):
    n = len(ts)

    def body(*refs):
        t_refs, o_refs = refs[:n], refs[n:2 * n]
        send_sems, recv_sems = refs[2 * n:]
        x, y, c = _place()
        cps = [_remote(t_refs[t], o_refs[t], send_sems, recv_sems, t, (x, y, 1 - c)) for t in range(n)]
        for cp in cps:
            cp.start()
        for cp in cps:
            cp.wait()

    return pl.pallas_call(
        body, name='swap_partials', in_specs=[ANY] * n, out_specs=[ANY] * n,
        out_shape=[jax.ShapeDtypeStruct(t.shape, t.dtype) for t in ts],
        scratch_shapes=[pltpu.SemaphoreType.DMA((n,)), pltpu.SemaphoreType.DMA((n,))],
    )(*ts)


HBM = pl.BlockSpec(memory_space=pltpu.HBM)
SEM = pl.BlockSpec(memory_space=pltpu.SEMAPHORE)
DATAFLOW = pltpu.SideEffectType.DATAFLOW_SIDE_EFFECTING


def _in_hbm(t):
    return pltpu.with_memory_space_constraint(t, pltpu.HBM)


def _other_chips():
    x, y, c = _place()
    return [(1 - x, y, c), (x, 1 - y, c), (1 - x, 1 - y, c)]


def _to_chips_copies(src_refs, land_refs, send_sems, recv_sems, per_peer):
    x, y, _ = _place()
    cps = []
    for t, (src, land) in enumerate(zip(src_refs, land_refs)):
        for j, (px, py, pc) in enumerate(_other_chips()):
            s = src.at[2 * px + py] if per_peer else src
            d = land.at[j] if per_peer else land.at[2 * x + y]
            cps.append(_remote(s, d, send_sems, recv_sems, 3 * t + j, (px, py, pc)))
    return cps


def _to_chips_start(srcs, lands, per_peer, order, name):
    n = len(srcs)

    def body(*refs):
        src_refs, land_refs = refs[:n], refs[n:2 * n]
        send_sems, recv_sems = refs[2 * n + 1], refs[2 * n + 2]
        token = refs[-1]
        for cp in _to_chips_copies(src_refs, land_refs, send_sems, recv_sems, per_peer):
            cp.start()
        token[...] = jnp.zeros_like(token)

    res = pl.pallas_call(
        body, name=name, in_specs=[HBM] * (2 * n) + [ANY],
        out_specs=[SEM, SEM] + [HBM] * (2 * n) + [pl.BlockSpec(memory_space=pltpu.VMEM)],
        out_shape=[pltpu.SemaphoreType.DMA((3 * n,)), pltpu.SemaphoreType.DMA((3 * n,))] +
                  [pltpu.HBM(t.shape, t.dtype) for t in srcs] + [pltpu.HBM(t.shape, t.dtype) for t in lands] +
                  [jax.ShapeDtypeStruct((8, LANES), f32)],
        input_output_aliases={i: 2 + i for i in range(2 * n)},
        compiler_params=pltpu.CompilerParams(has_side_effects=DATAFLOW),
    )(*[_in_hbm(t) for t in srcs], *[_in_hbm(t) for t in lands], order)
    return res[0], res[1], list(res[2:2 + n]), list(res[2 + n:2 + 2 * n]), res[-1]


def _to_chips_wait(send_sems, recv_sems, srcs, lands, per_peer, after, name):
    n = len(srcs)

    def body(*refs):
        src_refs, land_refs = refs[:n], refs[n:2 * n]
        send_ref, recv_ref = refs[2 * n], refs[2 * n + 1]
        for cp in _to_chips_copies(src_refs, land_refs, send_ref, recv_ref, per_peer):
            cp.wait_send()
            cp.wait_recv()

    res = pl.pallas_call(
        body, name=name, in_specs=[HBM] * (2 * n) + [SEM, SEM, ANY],
        out_specs=[HBM] * (2 * n),
        out_shape=[pltpu.HBM(t.shape, t.dtype) for t in srcs] + [pltpu.HBM(t.shape, t.dtype) for t in lands],
        input_output_aliases={i: i for i in range(2 * n)},
        compiler_params=pltpu.CompilerParams(has_side_effects=DATAFLOW),
    )(*srcs, *lands, send_sems, recv_sems, after)
    return list(res[:n]), list(res[n:])


def _row_tile(a):
    return _pick(a, (512, 256, 128, 64, 32, 16, 8))


def _partial_sum(g, land, name):
    _, a, b = g.shape
    tr = _row_tile(a)

    def body(w_ref, g_ref, r_ref, o_ref):
        t = g_ref[0].astype(f32) + r_ref[0].astype(f32)
        t = t + r_ref[1].astype(f32)
        t = t + r_ref[2].astype(f32)
        o_ref[...] = t.astype(o_ref.dtype)

    return pl.pallas_call(
        body, name=name,
        grid_spec=pltpu.PrefetchScalarGridSpec(
            num_scalar_prefetch=1, grid=(a // tr,),
            in_specs=[pl.BlockSpec((1, tr, b), lambda i, w: (w[1], i, 0)), pl.BlockSpec((3, tr, b), lambda i, w: (0, i, 0))],
            out_specs=pl.BlockSpec((tr, b), lambda i, w: (i, 0))),
        out_shape=jax.ShapeDtypeStruct((a, b), bf16),
        compiler_params=_cparams(('parallel',)),
    )(_where(), g, land)


def _by_shard(name, t):
    r, c = t.shape
    if BIG[name] == 2:
        return t.reshape(r, N_SHARDS, c // N_SHARDS).transpose(1, 0, 2)
    return t.reshape(N_SHARDS, r // N_SHARDS, c)


def _from_shards(name, g):
    s, a, b = g.shape
    if BIG[name] == 2:
        return g.transpose(1, 0, 2).reshape(a, s * b)
    return g.reshape(s * a, b)


def _adam_math(w, g, m, v):
    m = ADAM_B1 * m + (1.0 - ADAM_B1) * g
    v = ADAM_B2 * v + (1.0 - ADAM_B2) * (g * g)
    m_hat = m / (1.0 - ADAM_B1 ** ADAM_STEP)
    v_hat = v / (1.0 - ADAM_B2 ** ADAM_STEP)
    delta = -ADAM_LR * (m_hat / (jnp.sqrt(v_hat) + ADAM_EPS) + ADAM_WD * w)
    return delta, m, v


def _small_update(own, sib, chips, w, m, v):
    def body(o_ref, s_ref, c_ref, w_ref, m_ref, v_ref, g_out, d_out, m_out, v_out):
        chip = o_ref[...] + s_ref[...]
        g = (chip + c_ref[0]) + (c_ref[1] + c_ref[2])
        d, mn, vn = _adam_math(w_ref[...], g, m_ref[...], v_ref[...])
        g_out[...] = g
        d_out[...] = d
        m_out[...] = mn
        v_out[...] = vn

    return pl.pallas_call(body, name='small_update', out_shape=[jax.ShapeDtypeStruct(own.shape, f32)] * 4)(
        own, sib, chips, w, m, v)


def _small_chip_sum(own, sib):
    def body(o_ref, s_ref, out):
        out[...] = o_ref[...] + s_ref[...]
    return pl.pallas_call(body, name='small_chip_sum', out_shape=jax.ShapeDtypeStruct(own.shape, f32))(own, sib)


def _adamw(w, m, v, mine, theirs, name):
    layers, a, b = w.shape
    tr = _row_tile(a)

    def body(w_ref, m_ref, v_ref, p0, p1, q0, q1, g_out, d_out, m_out, v_out):
        first = pl.program_id(0) == 0
        g = jnp.where(first, p0[...].astype(f32) + q0[...].astype(f32), p1[...].astype(f32) + q1[...].astype(f32))
        d, mn, vn = _adam_math(w_ref[0], g, m_ref[0], v_ref[0])
        g_out[0] = g
        d_out[0] = d
        m_out[0] = mn
        v_out[0] = vn

    full = pl.BlockSpec((1, tr, b), lambda l, i: (l, i, 0))
    part = pl.BlockSpec((tr, b), lambda l, i: (i, 0))
    return pl.pallas_call(body, name=name, grid=(layers, a // tr), in_specs=[full] * 3 + [part] * 4, out_specs=[full] * 4,
                          out_shape=[jax.ShapeDtypeStruct(w.shape, f32)] * 4,
                          compiler_params=_cparams(('parallel', 'parallel')))(w, m, v, *mine, *theirs)


def _adamw_layer_in_the_middle(w, m, v, mine, theirs, name):
    a, layers, b = w.shape
    assert layers == 2 and b % LANES == 0

    def body(w_ref, m_ref, v_ref, p0, p1, q0, q1, g_out, d_out, m_out, v_out):
        g = jnp.stack([p0[...].astype(f32) + q0[...].astype(f32), p1[...].astype(f32) + q1[...].astype(f32)], axis=1)
        d, mn, vn = _adam_math(w_ref[...], g, m_ref[...], v_ref[...])
        g_out[...] = g
        d_out[...] = d
        m_out[...] = mn
        v_out[...] = vn

    full = pl.BlockSpec((a, layers, LANES), lambda i: (0, 0, i))
    part = pl.BlockSpec((a, LANES), lambda i: (0, i))
    return pl.pallas_call(body, name=name, grid=(b // LANES,), in_specs=[full] * 3 + [part] * 4, out_specs=[full] * 4,
                          out_shape=[jax.ShapeDtypeStruct(w.shape, f32)] * 4,
                          compiler_params=_cparams(('parallel',), big=True))(w, m, v, *mine, *theirs)


def _dg(a, b, ca, cb):
    return lax.dot_general(a.astype(bf16), b.astype(bf16), (((ca,), (cb,)), ((), ())), preferred_element_type=f32)


@jax.custom_vjp
def _bmm(a, b):
    return _dg(a, b, 1, 0)


_bmm.defvjp(lambda a, b: (_dg(a, b, 1, 0), (a, b)), lambda r, g: (_dg(g, r[1], 1, 1), _dg(r[0], g, 0, 0)))


@jax.custom_vjp
def _bmm_nt(a, b):
    return _dg(a, b, 1, 1)


_bmm_nt.defvjp(lambda a, b: (_dg(a, b, 1, 1), (a, b)), lambda r, g: (_dg(g, r[1], 1, 0), _dg(g, r[0], 0, 0)))


@jax.custom_vjp
def _bmm_tn(a, b):
    return _dg(a, b, 0, 0)


_bmm_tn.defvjp(lambda a, b: (_dg(a, b, 0, 0), (a, b)), lambda r, g: (_dg(r[1], g, 1, 1), _dg(r[0], g, 1, 0)))


def _hdot(a, b):
    return jnp.dot(a, b, precision=HI, preferred_element_type=f32)


def _hdot_nt(a, b):
    return lax.dot_general(a, b, (((1,), (1,)), ((), ())), precision=HI, preferred_element_type=f32)


def _hdot_tn(a, b):
    return lax.dot_general(a, b, (((0,), (0,)), ((), ())), precision=HI, preferred_element_type=f32)


def _head_mask(h, width=GROUP_W):
    lane = lax.broadcasted_iota(jnp.int32, (1, width), 1)
    return ((lane >= h * HEAD_DIM) & (lane < (h + 1) * HEAD_DIM)).astype(f32)


def _rope_perm():
    p = np.zeros((LANES, LANES), np.float32)
    half = MLA_ROPE // 2
    for i in range(half):
        p[MLA_NOPE + half + i, MLA_NOPE + i] = -1.0
        p[MLA_NOPE + i, MLA_NOPE + half + i] = 1.0
    return jnp.asarray(p)


def _rope_tables(S):
    half = MLA_ROPE // 2
    freqs = ROPE_THETA ** (-jnp.arange(half, dtype=f32) / half)
    ang = jnp.arange(S, dtype=f32)[:, None] * freqs[None, :]
    cos, sin = jnp.cos(ang), jnp.sin(ang)
    ones, zeros = jnp.ones((S, MLA_NOPE), f32), jnp.zeros((S, LANES - MLA_DQK), f32)
    c_tab = jnp.concatenate([ones, cos, cos, zeros], axis=1)
    s_tab = jnp.concatenate([jnp.zeros((S, MLA_NOPE), f32), sin, sin, zeros], axis=1)
    return c_tab, s_tab


def _f_mla_pre(c_q, c_kv, krope, c_tab, s_tab, q_norm, kv_norm, wq0, wq1, wq2, wq3, wk0, wk1, wk2, wk3, wv, gq, gk, perm):
    wq, wk = (wq0, wq1, wq2, wq3), (wk0, wk1, wk2, wk3)
    nq = _rms(c_q, q_norm)
    nkv = _rms(c_kv, kv_norm)

    def norm_rope(t, g):
        t = t * lax.rsqrt(jnp.sum(t * t, axis=-1, keepdims=True) * (1.0 / MLA_DQK) + EPS) * g
        return t * c_tab + _hdot(t, perm) * s_tab

    qs = [norm_rope(_bmm(nq, wq[h]), gq) * (MLA_DQK ** -0.5) for h in range(N_HEADS)]
    ks = [norm_rope(_bmm(nkv, wk[h]) + krope, gk) for h in range(N_HEADS)]
    return (*qs, *ks, _bmm(nkv, wv))


def _f_attn(qs, ks, v, q0):
    tq, S = qs[0].shape[0], ks[0].shape[0]
    qpos = q0 + lax.broadcasted_iota(jnp.int32, (tq, S), 0)
    kpos = lax.broadcasted_iota(jnp.int32, (tq, S), 1)
    keep = kpos <= qpos
    logits = [jnp.where(keep, _bmm_nt(qs[h], ks[h]), NEG_INF) for h in range(N_HEADS)]
    ps = [jnp.exp(lg - jnp.max(lg, axis=-1, keepdims=True)) for lg in logits]
    ps = [p / jnp.sum(p, axis=-1, keepdims=True) for p in ps]
    return sum(_bmm(p, v) * _head_mask(h) for h, p in enumerate(ps))


ATTN_PARTS = 4


def _mla_attn_fwd(qs, ks, v, name):
    S = v.shape[0]
    tq = 2 * Q_BLOCK if S % (2 * ATTN_PARTS * Q_BLOCK) == 0 else Q_BLOCK
    parts = ATTN_PARTS if S % (ATTN_PARTS * tq) == 0 else 1
    per = S // parts
    outs = []
    for p in range(parts):
        n_keys = (p + 1) * per
        first_block = p * (per // tq)

        def body(*refs, first_block=first_block):
            q_vals = [r[...] for r in refs[:4]]
            k_vals = [r[...] for r in refs[4:8]]
            refs[9][...] = _f_attn(q_vals, k_vals, refs[8][...], (first_block + pl.program_id(0)) * tq)

        qspec = pl.BlockSpec((tq, LANES), lambda i, fb=first_block: (fb + i, 0))
        outs.append(pl.pallas_call(
            body, name=f'{name}_{p}', grid=(per // tq,),
            in_specs=[qspec] * 4 + [pl.BlockSpec((n_keys, LANES), lambda i: (0, 0))] * 4 +
                     [pl.BlockSpec((n_keys, GROUP_W), lambda i: (0, 0))],
            out_specs=pl.BlockSpec((tq, GROUP_W), lambda i: (i, 0)),
            out_shape=jax.ShapeDtypeStruct((per, GROUP_W), f32),
            compiler_params=_cparams(('parallel',), big=True),
        )(*qs, *ks, v))
    return jnp.concatenate(outs, axis=0)


def _mla_attn_bwd(qs, ks, v, do, name):
    S = v.shape[0]
    tq = Q_BLOCK
    parts = ATTN_PARTS if S % (ATTN_PARTS * tq) == 0 else 1
    per = S // parts
    dq_parts, dkv_sum = [], None
    for p in range(parts):
        n_keys = (p + 1) * per
        first_block = p * (per // tq)

        def body(*refs, first_block=first_block):
            q_vals = [r[...].astype(f32) for r in refs[:4]]
            k_vals = [r[...].astype(f32) for r in refs[4:8]]
            v_val = refs[8][...].astype(f32)
            q0 = (first_block + pl.program_id(0)) * tq
            _, vjp = jax.vjp(lambda a, b, c: _f_attn(a, b, c, q0), q_vals, k_vals, v_val)
            dqs, dks, dv = vjp(refs[9][...])
            outs = refs[10:]
            for h in range(N_HEADS):
                outs[h][...] = dqs[h]
            first = pl.program_id(0) == 0
            for o, g in zip(outs[4:], (*dks, dv)):
                @pl.when(first)
                def _(o=o, g=g):
                    o[...] = g

                @pl.when(jnp.logical_not(first))
                def _(o=o, g=g):
                    o[...] += g

        qspec = pl.BlockSpec((tq, LANES), lambda i, fb=first_block: (fb + i, 0))
        kspec = pl.BlockSpec((n_keys, LANES), lambda i: (0, 0))
        vspec = pl.BlockSpec((n_keys, GROUP_W), lambda i: (0, 0))
        res = pl.pallas_call(
            body, name=f'{name}_{p}', grid=(per // tq,),
            in_specs=[qspec] * 4 + [kspec] * 4 + [vspec, pl.BlockSpec((tq, GROUP_W), lambda i, fb=first_block: (fb + i, 0))],
            out_specs=[pl.BlockSpec((tq, LANES), lambda i: (i, 0))] * 4 + [kspec] * 4 + [vspec],
            out_shape=[jax.ShapeDtypeStruct((per, LANES), f32)] * 4 + [jax.ShapeDtypeStruct((n_keys, LANES), f32)] * 4 +
                      [jax.ShapeDtypeStruct((n_keys, GROUP_W), f32)],
            compiler_params=_cparams(('arbitrary',), big=True),
        )(*qs, *ks, v, do)
        dq_parts.append(res[:4])
        dkv = [jnp.pad(t, ((0, S - n_keys), (0, 0))) for t in res[4:]]
        dkv_sum = dkv if dkv_sum is None else [a_ + b_ for a_, b_ in zip(dkv_sum, dkv)]
    dqs = [jnp.concatenate([dq_parts[p][h] for p in range(parts)], axis=0) for h in range(N_HEADS)]
    return dqs, dkv_sum[:4], dkv_sum[4]


def _mla_params(mp):
    pad = LANES - MLA_DQK
    wq = jnp.pad(mp['mla_w_uq'].reshape(GROUP_W, N_HEADS, MLA_DQK).transpose(1, 0, 2), ((0, 0), (0, 0), (0, pad)))
    wkv = mp['mla_w_ukv'].reshape(LANES, N_HEADS, MLA_NOPE + HEAD_DIM)
    wk = jnp.pad(wkv[:, :, :MLA_NOPE].transpose(1, 0, 2), ((0, 0), (0, 0), (0, LANES - MLA_NOPE)))
    wv = wkv[:, :, MLA_NOPE:].reshape(LANES, GROUP_W)
    gq = jnp.pad(mp['mla_qk_q'], (0, pad))[None]
    gk = jnp.pad(mp['mla_qk_k'], (0, pad))[None]
    return [mp['mla_q_norm'][None], mp['mla_kv_norm'][None], *[wq[h] for h in range(N_HEADS)],
            *[wk[h] for h in range(N_HEADS)], wv, gq, gk, _rope_perm()]


def _mla_fwd(c_q, c_kv, k_rope, mp, l):
    S = c_q.shape[0]
    tm = _pick(S, (256, 128))
    krope = jnp.pad(k_rope, ((0, 0), (MLA_NOPE, LANES - MLA_DQK)))
    c_tab, s_tab = _rope_tables(S)
    tiled = [c_q, c_kv, krope, c_tab, s_tab]
    params = _mla_params(mp)
    res = _tile_fwd(_f_mla_pre, tiled, params, [(LANES, bf16)] * 8 + [(GROUP_W, bf16)], tm, f'mla_pre_fwd_{l}')
    qs, ks, v = res[:4], res[4:8], res[8]
    y = _mla_attn_fwd(qs, ks, v, f'mla_attn_fwd_{l}')
    return y, (tiled, params, qs, ks, v)


def _mla_bwd(dy, saved, l):
    tiled, params, qs, ks, v = saved
    S = dy.shape[0]
    tm = _pick(S, (256, 128))
    dqs, dks, dv = _mla_attn_bwd(qs, ks, v, dy, f'mla_attn_bwd_{l}')
    (dc_q, dc_kv, dkrope), dpar = _tile_bwd(_f_mla_pre, tiled, params, [*dqs, *dks, dv], [True, True, True, False, False],
                                            [True] * 13 + [False], tm, f'mla_pre_bwd_{l}')
    dqn, dkvn = dpar[0], dpar[1]
    dwq, dwk = jnp.stack(dpar[2:6]), jnp.stack(dpar[6:10])
    dwv, dgq, dgk = dpar[10:13]
    dw_uq = dwq[:, :, :MLA_DQK].transpose(1, 0, 2).reshape(GROUP_W, N_HEADS * MLA_DQK)
    dw_ukv = jnp.concatenate([dwk[:, :, :MLA_NOPE].transpose(1, 0, 2), dwv.reshape(LANES, N_HEADS, HEAD_DIM)],
                             axis=2).reshape(LANES, N_HEADS * (MLA_NOPE + HEAD_DIM))
    grads = {'mla_q_norm': dqn[0], 'mla_kv_norm': dkvn[0], 'mla_w_uq': dw_uq, 'mla_w_ukv': dw_ukv,
             'mla_qk_q': dgq[0, :MLA_DQK], 'mla_qk_k': dgk[0, :MLA_DQK]}
    return dc_q, dc_kv, dkrope[:, MLA_NOPE:MLA_DQK], grads


SPAN = 128


def _head_mean_matrix():
    h = np.arange(GROUP_W) // HEAD_DIM
    return jnp.asarray((h[:, None] == h[None, :]).astype(np.float32) / HEAD_DIM)


def _f_dil_pre(q, k, gq, gk, hm):
    qn = q * lax.rsqrt(_hdot(q * q, hm) + EPS) * gq * (HEAD_DIM ** -0.5)
    kn = k * lax.rsqrt(_hdot(k * k, hm) + EPS) * gk
    return qn, kn


def _f_dil_branch(qb, kp, kc, vp, vc, b0, b1, b2, b3, first):
    kcat = jnp.concatenate([kp, kc], axis=0)
    vcat = jnp.concatenate([vp, vc], axis=0)
    qi = lax.broadcasted_iota(jnp.int32, (SPAN, 2 * SPAN), 0) + SPAN
    kj = lax.broadcasted_iota(jnp.int32, (SPAN, 2 * SPAN), 1)
    delta = qi - kj
    valid = (delta >= 0) & (delta <= SPAN) & jnp.logical_not(first & (kj < SPAN))
    masks = [_head_mask(h) for h in range(N_HEADS)]
    raw = [_bmm_nt(qb * hm, kcat) for hm in masks]
    logits = [jnp.where(valid, r + bias, NEG_INF) for r, bias in zip(raw, (b0, b1, b2, b3))]
    ms = [jnp.max(lg, axis=-1, keepdims=True) for lg in logits]
    ps = [jnp.exp(lg - m) for lg, m in zip(logits, ms)]
    pvs = [_bmm(p, vcat) for p in ps]
    o = sum(pv * hm for pv, hm in zip(pvs, masks))
    m_full = sum(m * hm for m, hm in zip(ms, masks))
    l_full = sum(jnp.sum(p, axis=-1, keepdims=True) * hm for p, hm in zip(ps, masks))
    return o, m_full, l_full


def _dil_branch_specs(d, nb):
    cur = pl.BlockSpec((SPAN, GROUP_W), lambda r, n: (n, r))
    prev = pl.BlockSpec((SPAN, GROUP_W), lambda r, n: (jnp.maximum(n - 1, 0), r))
    bias = pl.BlockSpec((1, SPAN, 2 * SPAN), lambda r, n: (0, 0, 0))
    return cur, prev, bias


def _head_table_specs():
    return [pl.BlockSpec((1, SPAN, 2 * SPAN), lambda r, n, h=h: (h, 0, 0)) for h in range(N_HEADS)]


def _dil_branch_fwd(q, k, v, table, name):
    L, d = q.shape[0], q.shape[1] // GROUP_W
    nb = L // SPAN
    cur, prev, bias = _dil_branch_specs(d, nb)

    def body(q_ref, kp_ref, kc_ref, vp_ref, vc_ref, b0, b1, b2, b3, o_ref, m_ref, l_ref):
        o, m, l = _f_dil_branch(*[r[...].astype(f32) for r in (q_ref, kp_ref, kc_ref, vp_ref, vc_ref)], b0[0], b1[0], b2[0], b3[0],
                                pl.program_id(1) == 0)
        o_ref[...] = o
        m_ref[...] = m
        l_ref[...] = l

    return pl.pallas_call(
        body, name=name, grid=(d, nb), in_specs=[cur, prev, cur, prev, cur] + _head_table_specs(),
        out_specs=[cur] * 3, out_shape=[jax.ShapeDtypeStruct(q.shape, f32)] * 3,
        compiler_params=_cparams(('parallel', 'parallel')),
    )(q, k, k, v, v, *[table] * N_HEADS)


def _dil_branch_bwd(q, k, v, table, do, dm, dl, name):
    L, d = q.shape[0], q.shape[1] // GROUP_W
    nb = L // SPAN
    cur, prev, bias = _dil_branch_specs(d, nb)
    whole = pl.BlockSpec((L, GROUP_W), lambda r, n: (0, r))

    def body(q_ref, kp_ref, kc_ref, vp_ref, vc_ref, b0, b1, b2, b3, do_ref, dm_ref, dl_ref,
             dq_ref, dk_ref, dv_ref, db0, db1, db2, db3):
        r, n = pl.program_id(0), pl.program_id(1)
        first = n == 0
        _, vjp = jax.vjp(lambda *a: _f_dil_branch(*a, first), *[r[...].astype(f32) for r in (q_ref, kp_ref, kc_ref, vp_ref, vc_ref)],
                         b0[0], b1[0], b2[0], b3[0])
        dq, dkp, dkc, dvp, dvc, g0, g1, g2, g3 = vjp((do_ref[...], dm_ref[...], dl_ref[...]))
        dq_ref[...] = dq

        @pl.when(first)
        def _():
            dk_ref[...] = jnp.zeros_like(dk_ref)
            dv_ref[...] = jnp.zeros_like(dv_ref)

        rows = pl.ds(pl.multiple_of(n * SPAN, SPAN), SPAN)
        dk_ref[rows, :] += dkc
        dv_ref[rows, :] += dvc

        @pl.when(n > 0)
        def _():
            before = pl.ds(pl.multiple_of((n - 1) * SPAN, SPAN), SPAN)
            dk_ref[before, :] += dkp
            dv_ref[before, :] += dvp

        start = first & (r == 0)
        for o, g in zip((db0, db1, db2, db3), (g0, g1, g2, g3)):
            @pl.when(start)
            def _(o=o, g=g):
                o[0] = g

            @pl.when(jnp.logical_not(start))
            def _(o=o, g=g):
                o[0] += g

    res = pl.pallas_call(
        body, name=name, grid=(d, nb), in_specs=[cur, prev, cur, prev, cur] + _head_table_specs() + [cur] * 3,
        out_specs=[cur, whole, whole] + [bias] * 4,
        out_shape=[jax.ShapeDtypeStruct(q.shape, f32)] * 3 + [jax.ShapeDtypeStruct((1, SPAN, 2 * SPAN), f32)] * 4,
        compiler_params=_cparams(('arbitrary', 'arbitrary')),
    )(q, k, k, v, v, *[table] * N_HEADS, do, dm, dl)
    return res[0], res[1], res[2], res[3:]


def _f_dil_merge(o1, m1, l1, o2, m2, l2, o3, m3, l3):
    mx = jnp.maximum(jnp.maximum(m1, m2), m3)
    w1, w2, w3 = jnp.exp(m1 - mx), jnp.exp(m2 - mx), jnp.exp(m3 - mx)
    return ((w1 * o1 + w2 * o2 + w3 * o3) / (w1 * l1 + w2 * l2 + w3 * l3),)


def _bias_onehot(dilation):
    qi = jnp.arange(SPAN, dtype=jnp.int32)[:, None] + SPAN
    kj = jnp.arange(2 * SPAN, dtype=jnp.int32)[None, :]
    bucket = _t5_bucket(jnp.clip(qi - kj, 0, SPAN) * dilation).reshape(-1)
    return (bucket[None, :] == jnp.arange(T5_BUCKETS, dtype=jnp.int32)[:, None]).astype(f32)


def _bias_tables(t5_t, onehot, name):
    N = onehot.shape[1]
    tn = _pick(N, (4096, 2048, 1024))

    def body(t_ref, oh_ref, o_ref):
        o_ref[...] = _hdot(t_ref[...], oh_ref[...])

    return pl.pallas_call(
        body, name=name, grid=(N // tn,),
        in_specs=[pl.BlockSpec((8, T5_BUCKETS), lambda i: (0, 0)), pl.BlockSpec((T5_BUCKETS, tn), lambda i: (0, i))],
        out_specs=pl.BlockSpec((8, tn), lambda i: (0, i)), out_shape=jax.ShapeDtypeStruct((8, N), f32),
        compiler_params=_cparams(('parallel',)),
    )(t5_t, onehot)


def _bias_tables_bwd(d_tab, onehot, name):
    N = onehot.shape[1]
    tn = _pick(N, (4096, 2048, 1024))

    def body(g_ref, oh_ref, o_ref):
        part = _hdot_nt(g_ref[...], oh_ref[...])

        @pl.when(pl.program_id(0) == 0)
        def _():
            o_ref[...] = part

        @pl.when(pl.program_id(0) > 0)
        def _():
            o_ref[...] += part

    return pl.pallas_call(
        body, name=name, grid=(N // tn,),
        in_specs=[pl.BlockSpec((8, tn), lambda i: (0, i)), pl.BlockSpec((T5_BUCKETS, tn), lambda i: (0, i))],
        out_specs=pl.BlockSpec((8, T5_BUCKETS), lambda i: (0, 0)), out_shape=jax.ShapeDtypeStruct((8, T5_BUCKETS), f32),
        compiler_params=_cparams(('arbitrary',)),
    )(d_tab, onehot)


def _by_residue(t, d):
    S, C = t.shape
    return t.reshape(S // d, d * C)


def _from_residue(t):
    return t.reshape(-1, GROUP_W)


def _dil_tables(t5_bias):
    t5_t = jnp.pad(t5_bias.T, ((0, 8 - N_HEADS), (0, 0)))
    return [_bias_tables(t5_t, _bias_onehot(d), f'dil_bias_fwd_{bi}').reshape(8, SPAN, 2 * SPAN)
            for bi, (_, d) in enumerate(DIL_PAIRS)]


def _t5_grad(d_tables):
    total = None
    for bi, (_, d) in enumerate(DIL_PAIRS):
        g = _bias_tables_bwd(d_tables[bi], _bias_onehot(d), f'dil_bias_bwd_{bi}')
        total = g if total is None else total + g
    return total[:N_HEADS].T


def _dil_fwd(qkv, mp, l):
    S = qkv.shape[0]
    tm = _pick(S, (256, 128))
    q, k, v = qkv[:, :GROUP_W], qkv[:, GROUP_W:2 * GROUP_W], qkv[:, 2 * GROUP_W:]
    pre_params = [jnp.tile(mp['dil_q_norm'], N_HEADS)[None], jnp.tile(mp['dil_k_norm'], N_HEADS)[None], _head_mean_matrix()]
    qn, kn = _tile_fwd(_f_dil_pre, [q, k], pre_params, [(GROUP_W, bf16)] * 2, tm, f'dil_pre_fwd_{l}')
    v = v.astype(bf16)
    tables = mp['dil_tables'] if 'dil_tables' in mp else _dil_tables(mp['t5_bias'])
    branches, outs = [], []
    for bi, (_, d) in enumerate(DIL_PAIRS):
        tab = tables[bi]
        qd, kd, vd = _by_residue(qn, d), _by_residue(kn, d), _by_residue(v, d)
        o, m, lsum = _dil_branch_fwd(qd, kd, vd, tab, f'dil_branch_fwd_{l}_{bi}')
        branches.append((qd, kd, vd, tab))
        outs += [_from_residue(o), _from_residue(m), _from_residue(lsum)]
    (y,) = _tile_fwd(_f_dil_merge, outs, [], [(GROUP_W, f32)], tm, f'dil_merge_fwd_{l}')
    return y, (q, k, pre_params, branches, outs)


def _dil_bwd(dy, saved, l):
    q, k, pre_params, branches, outs = saved
    S = dy.shape[0]
    tm = _pick(S, (256, 128))
    douts, _ = _tile_bwd(_f_dil_merge, outs, [], [dy], [True] * 9, [], tm, f'dil_merge_bwd_{l}')
    dqn = dkn = dv = None
    d_tabs = []
    for bi, (_, d) in enumerate(DIL_PAIRS):
        qd, kd, vd, tab = branches[bi]
        do, dm, dl = [_by_residue(t, d) for t in douts[3 * bi:3 * bi + 3]]
        dq_b, dk_b, dv_b, dbias = _dil_branch_bwd(qd, kd, vd, tab, do, dm, dl, f'dil_branch_bwd_{l}_{bi}')
        d_tabs.append(jnp.concatenate([*dbias, jnp.zeros((8 - N_HEADS, SPAN, 2 * SPAN), f32)], axis=0).reshape(8, -1))
        dq_b, dk_b, dv_b = _from_residue(dq_b), _from_residue(dk_b), _from_residue(dv_b)
        dqn = dq_b if dqn is None else dqn + dq_b
        dkn = dk_b if dkn is None else dkn + dk_b
        dv = dv_b if dv is None else dv + dv_b
    (dq, dk), (dgq, dgk) = _tile_bwd(_f_dil_pre, [q, k], pre_params, [dqn, dkn], [True, True], [True, True, False], tm,
                                     f'dil_pre_bwd_{l}')
    grads = {'dil_q_norm': dgq.reshape(N_HEADS, HEAD_DIM).sum(0), 'dil_k_norm': dgk.reshape(N_HEADS, HEAD_DIM).sum(0),
             't5_tables': d_tabs}
    return jnp.concatenate([dq, dk, dv], axis=1), grads


S5_LANES = S5_G * S5_P
SCAN_SEGMENTS = 8
SCAN_W = 512


def _f_s5_prep(bre, bim, lr, li, logdt_col, expand):
    dt = jnp.sum(jnp.exp(logdt_col) * expand, axis=0, keepdims=True)
    mag = jnp.exp(lr * dt)
    ar, ai = mag * jnp.cos(li * dt), mag * jnp.sin(li * dt)
    den = lr * lr + li * li
    nr, ni = ar - 1.0, ai
    zr = (nr * lr + ni * li) / den
    zi = (ni * lr - nr * li) / den
    bb = jnp.concatenate([zr * bre - zi * bim, zr * bim + zi * bre], axis=1)
    a_rows = jnp.broadcast_to(jnp.concatenate([ar, ai], axis=1), bb.shape)
    return bb, a_rows


def _s5_scan(x, a_rows, name, reverse=False, h=None):
    S = x.shape[0]
    NL = x.shape[1] // 2
    T = S // SCAN_SEGMENTS
    nblk = NL // SCAN_W
    n_in = 4 if reverse else 2

    def body(*refs):
        if reverse:
            (x_hbm, pr_hbm, pi_hbm, ar_ref, ai_ref, hr_hbm, hi_hbm, dar_ref, dai_ref,
             xr_s, xi_s, pr_s, pi_s, hr_s, hi_s, in_sems, out_sems) = refs
        else:
            x_hbm, ar_ref, ai_ref, hr_hbm, hi_hbm, xr_s, xi_s, hr_s, hi_s, in_sems, out_sems = refs
        col = pl.multiple_of(pl.program_id(0) * SCAN_W, SCAN_W)
        loads = []
        for k in range(SCAN_SEGMENTS):
            rows = pl.ds(k * T, T)
            sources = [(x_hbm, col, xr_s), (x_hbm, NL + col, xi_s)]
            if reverse:
                sources += [(pr_hbm, col, pr_s), (pi_hbm, col, pi_s)]
            for i, (src, c0, dst) in enumerate(sources):
                loads.append(pltpu.make_async_copy(src.at[rows, pl.ds(c0, SCAN_W)], dst.at[:, k, :],
                                                   in_sems.at[i * SCAN_SEGMENTS + k]))
        for cp in loads:
            cp.start()
        for cp in loads:
            cp.wait()
        ar = ar_ref[...]
        ai = -ai_ref[...] if reverse else ai_ref[...]
        zero = jnp.zeros((SCAN_SEGMENTS, SCAN_W), f32)

        def at(s):
            return T - 1 - s if reverse else s

        def local(s, c):
            hr, hi, pr, pi = c
            j = at(s)
            nhr = ar * hr - ai * hi + xr_s[j]
            nhi = ar * hi + ai * hr + xi_s[j]
            hr_s[j] = nhr
            hi_s[j] = nhi
            return nhr, nhi, ar * pr - ai * pi, ar * pi + ai * pr

        er, ei, pr, pi = lax.fori_loop(0, T, local, (zero, zero, zero + 1.0, zero), unroll=2)
        row = lax.broadcasted_iota(jnp.int32, (SCAN_SEGMENTS, SCAN_W), 0)
        cr, ci = zero, zero
        order = range(SCAN_SEGMENTS - 2, -1, -1) if reverse else range(1, SCAN_SEGMENTS)
        for k in order:
            src = k + 1 if reverse else k - 1
            tr = er + pr * cr - pi * ci
            ti = ei + pr * ci + pi * cr
            cr = jnp.where(row == k, jnp.sum(jnp.where(row == src, tr, 0.0), axis=0, keepdims=True), cr)
            ci = jnp.where(row == k, jnp.sum(jnp.where(row == src, ti, 0.0), axis=0, keepdims=True), ci)

        def fix_at(j, c, before):
            pr, pi, sr, si = c
            pr, pi = ar * pr - ai * pi, ar * pi + ai * pr
            hr = hr_s[j] + pr * cr - pi * ci
            hi = hi_s[j] + pr * ci + pi * cr
            hr_s[j] = hr
            hi_s[j] = hi
            if reverse:
                qr, qi = before
                sr = sr + hr * qr + hi * qi
                si = si + hi * qr - hr * qi
            return pr, pi, sr, si

        start = (zero + 1.0, zero, zero, zero)
        if reverse:
            def fix(s, c):
                j = T - 1 - s
                return fix_at(j, c, (pr_s[j - 1], pi_s[j - 1]))

            c = lax.fori_loop(0, T - 1, fix, start, unroll=2)
            last_r = jnp.where(row == 0, 0.0, pltpu.roll(pr_s[T - 1], 1, 0))
            last_i = jnp.where(row == 0, 0.0, pltpu.roll(pi_s[T - 1], 1, 0))
            _, _, sr, si = fix_at(0, c, (last_r, last_i))
            dar_ref[...] = sr
            dai_ref[...] = si
        else:
            lax.fori_loop(0, T, lambda s, c: fix_at(s, c, None), start, unroll=2)
        stores = []
        for k in range(SCAN_SEGMENTS):
            rows = pl.ds(k * T, T)
            stores.append(pltpu.make_async_copy(hr_s.at[:, k, :], hr_hbm.at[rows, pl.ds(col, SCAN_W)], out_sems.at[k]))
            stores.append(pltpu.make_async_copy(hi_s.at[:, k, :], hi_hbm.at[rows, pl.ds(col, SCAN_W)],
                                                out_sems.at[SCAN_SEGMENTS + k]))
        for cp in stores:
            cp.start()
        for cp in stores:
            cp.wait()

    a_re = pl.BlockSpec((SCAN_SEGMENTS, SCAN_W), lambda b: (0, b))
    a_im = pl.BlockSpec((SCAN_SEGMENTS, SCAN_W), lambda b: (0, nblk + b))
    seq = pltpu.VMEM((T, SCAN_SEGMENTS, SCAN_W), f32)
    if reverse:
        in_specs, args = [ANY, ANY, ANY, a_re, a_im], [x, h[0], h[1], a_rows, a_rows]
        out_specs = [ANY, ANY, a_re, a_re]
        out_shape = [jax.ShapeDtypeStruct((S, NL), f32)] * 2 + [jax.ShapeDtypeStruct((SCAN_SEGMENTS, NL), f32)] * 2
    else:
        in_specs, args = [ANY, a_re, a_im], [x, a_rows, a_rows]
        out_specs = [ANY, ANY]
        out_shape = [jax.ShapeDtypeStruct((S, NL), f32)] * 2
    scratch = [seq] * (n_in + 2) + [pltpu.SemaphoreType.DMA((n_in * SCAN_SEGMENTS,)),
                                    pltpu.SemaphoreType.DMA((2 * SCAN_SEGMENTS,))]
    return pl.pallas_call(body, name=name, grid=(nblk,), in_specs=in_specs, out_specs=out_specs, out_shape=out_shape,
                          scratch_shapes=scratch, compiler_params=_cparams(('arbitrary',), big=True))(*args)


def _f_s5_post(y, u, d, w_glu):
    z = _bmm(y + d * u, w_glu)
    return (z[:, :GROUP_W] * jax.nn.sigmoid(z[:, GROUP_W:]),)


def _block_diag(t):
    G, a, b = t.shape
    eye = jnp.eye(G, dtype=t.dtype)
    return (t[:, :, None, :] * eye[:, None, :, None]).reshape(G * a, G * b)


def _diag_blocks(m, a, b):
    G = m.shape[0] // a
    return jnp.moveaxis(jnp.diagonal(m.reshape(G, a, G, b), axis1=0, axis2=2), -1, 0)


def _s5_fwd(u, mp, l):
    S = u.shape[0]
    tm = _pick(S, (256, 128))
    bre = _block_diag(mp['s5_b_re'].transpose(0, 2, 1))
    bim = _block_diag(mp['s5_b_im'].transpose(0, 2, 1))
    expand = jnp.repeat(jnp.eye(S5_G, dtype=f32), S5_P, axis=1)
    prep_params = [mp['s5_lambda_re'].reshape(1, S5_LANES), mp['s5_lambda_im'].reshape(1, S5_LANES),
                   mp['s5_log_dt'].reshape(S5_G, 1), expand]
    bb, a_rows = _tile_fwd(_f_s5_prep, [bre, bim], prep_params, [(2 * S5_LANES, f32)] * 2, GROUP_W, f's5_prep_fwd_{l}')
    x = _mm(u, bb, 'nn', f's5_in_fwd_{l}')
    hr, hi = _s5_scan(x, a_rows, f's5_scan_fwd_{l}')
    c_re, c_im = _block_diag(mp['s5_c_re'].transpose(0, 2, 1)), -_block_diag(mp['s5_c_im'].transpose(0, 2, 1))
    y = _mm(hi, c_im, 'nn', f's5_out_im_fwd_{l}', add=_mm(hr, c_re, 'nn', f's5_out_re_fwd_{l}'))
    post_params = [mp['s5_d'][None], mp['s5_w_glu']]
    (out,) = _tile_fwd(_f_s5_post, [y, u], post_params, [(GROUP_W, f32)], tm, f's5_post_fwd_{l}')
    return out, (u, bre, bim, prep_params, bb, a_rows, hr, hi, c_re, c_im, y, post_params)


def _s5_bwd(dout, saved, l):
    u, bre, bim, prep_params, bb, a_rows, hr, hi, c_re, c_im, y, post_params = saved
    S = u.shape[0]
    tm = _pick(S, (256, 128))
    (dy, du1), (dd, dwglu) = _tile_bwd(_f_s5_post, [y, u], post_params, [dout], [True, True], [True, True], tm,
                                       f's5_post_bwd_{l}')
    ccat = jnp.concatenate([c_re, c_im], axis=0)
    dh = _mm(dy, ccat, 'nt', f's5_out_dx_{l}')
    dccat = jnp.concatenate([_mm(hr, dy, 'tn', f's5_out_re_dw_{l}'), _mm(hi, dy, 'tn', f's5_out_im_dw_{l}')], axis=0)
    lr_, li_, dar, dai = _s5_scan(dh, a_rows, f's5_scan_bwd_{l}', reverse=True, h=(hr, hi))
    du2 = _mm(li_, bb[:, S5_LANES:], 'nt', f's5_in_im_dx_{l}', add=_mm(lr_, bb[:, :S5_LANES], 'nt', f's5_in_re_dx_{l}'))
    dbb = jnp.concatenate([_mm(u, lr_, 'tn', f's5_in_re_dw_{l}'), _mm(u, li_, 'tn', f's5_in_im_dw_{l}')], axis=1)
    da_rows = jnp.pad(jnp.concatenate([dar, dai], axis=1), ((0, GROUP_W - SCAN_SEGMENTS), (0, 0)))
    (dbre, dbim), (dlr, dli, dlogdt) = _tile_bwd(_f_s5_prep, [bre, bim], prep_params, [dbb, da_rows], [True, True],
                                                 [True, True, True, False], GROUP_W, f's5_prep_bwd_{l}')
    grads = {
        's5_lambda_re': dlr.reshape(S5_G, S5_P), 's5_lambda_im': dli.reshape(S5_G, S5_P), 's5_log_dt': dlogdt[:, 0],
        's5_b_re': _diag_blocks(dbre, S5_CG, S5_P).transpose(0, 2, 1),
        's5_b_im': _diag_blocks(dbim, S5_CG, S5_P).transpose(0, 2, 1),
        's5_c_re': _diag_blocks(dccat[:S5_LANES], S5_P, S5_CG).transpose(0, 2, 1),
        's5_c_im': -_diag_blocks(dccat[S5_LANES:], S5_P, S5_CG).transpose(0, 2, 1),
        's5_d': dd[0], 's5_w_glu': dwglu}
    return du1 + du2, grads


DN_CONV = 4


def _head_sum_matrix():
    h = np.arange(GROUP_W) // HEAD_DIM
    return jnp.asarray((h[:, None] == h[None, :]).astype(np.float32))


def _f_dn_pre(x0, x1, x2, x3, ab, w0, w1, w2, w3, alog, dtb, ea, eb, hs):
    c = w0 * x0 + w1 * x1 + w2 * x2 + w3 * x3
    s = c * jax.nn.sigmoid(c)
    q, k, v = s[:, :GROUP_W], s[:, GROUP_W:2 * GROUP_W], s[:, 2 * GROUP_W:]
    q = q * lax.rsqrt(_hdot(q * q, hs) + EPS) * (HEAD_DIM ** -0.5)
    k = k * lax.rsqrt(_hdot(k * k, hs) + EPS)
    beta = jax.nn.sigmoid(_hdot(ab, eb))
    g = -jnp.exp(alog) * jax.nn.softplus(_hdot(ab, ea) + dtb)
    return q, k, v, g, beta


DN_CHUNKS_PER_STEP = 8


def _f_dn_chunks(q, k, v, g, beta):
    C = DN_CHUNK
    n_chunks = q.shape[0] // C
    r = lax.broadcasted_iota(jnp.int32, (C, C), 0)
    c = lax.broadcasted_iota(jnp.int32, (C, C), 1)
    causal, strict = r >= c, r > c
    eye = (r == c).astype(f32)
    tril = causal.astype(f32)
    ones = jnp.ones((C, GROUP_W), f32)
    masks = [_head_mask(h) for h in range(N_HEADS)]
    rows = [tuple(t[i * C:(i + 1) * C] for t in (q, k, v, g, beta)) for i in range(n_chunks)]
    gcs = [_hdot(tril, gi) for (_, _, _, gi, _) in rows]
    items = [(i, h) for i in range(n_chunks) for h in range(N_HEADS)]
    grows = [_hdot_nt(ones * (masks[h] * (1.0 / HEAD_DIM)), gcs[i]) for i, h in items]
    decs = []
    for (i, h), grow in zip(items, grows):
        gcol = jnp.sum(gcs[i] * masks[h], axis=1, keepdims=True) * (1.0 / HEAD_DIM)
        decs.append(jnp.exp(jnp.where(causal, gcol - grow, NEG_INF)))
    kbs = [ki * bi for (_, ki, _, _, bi) in rows]
    kks = [_bmm_nt(kbs[i] * masks[h], rows[i][1]) for i, h in items]
    qks = [_bmm_nt(rows[i][0] * masks[h], rows[i][1]) for i, h in items]
    lmats = [jnp.where(strict, kk * dec, 0.0) for kk, dec in zip(kks, decs)]
    a_qk = [jnp.where(causal, qk * dec, 0.0) for qk, dec in zip(qks, decs)]
    ts = [eye - lm for lm in lmats]
    ps = lmats
    for _ in range(5):
        ps = [_bmm(p, p) for p in ps]
        ts = [t + _bmm(t, p) for t, p in zip(ts, ps)]
    egs = [jnp.exp(gc) for gc in gcs]
    tw = [_bmm(t, kbs[i] * egs[i]) for (i, h), t in zip(items, ts)]
    tu = [_bmm(t, rows[i][2] * rows[i][4]) for (i, h), t in zip(items, ts)]
    outs = []
    for i in range(n_chunks):
        qi, ki, _, gi, _ = rows[i]
        glast = jnp.sum(gi, axis=0, keepdims=True)
        w = sum(tw[i * N_HEADS + h] * masks[h] for h in range(N_HEADS))
        u = sum(tu[i * N_HEADS + h] * masks[h] for h in range(N_HEADS))
        outs.append((w, u, qi * egs[i], ki * jnp.exp(glast - gcs[i]), *a_qk[i * N_HEADS:(i + 1) * N_HEADS],
                     jnp.broadcast_to(jnp.exp(glast), (C, GROUP_W))))
    return tuple(jnp.concatenate(parts, axis=0) for parts in zip(*outs))


def _f_dn_step(w, u, qd, kdec, a0, a1, a2, a3, dfull, state, bd):
    row0 = (lax.broadcasted_iota(jnp.int32, dfull.shape, 0) == 0).astype(f32)
    dvec = jnp.sum(dfull * row0, axis=0, keepdims=True)
    ws, qs = _bmm(w, state), _bmm(qd, state)
    vnew = u - ws
    avs = [_bmm(a, vnew) for a in (a0, a1, a2, a3)]
    kv = _bmm_tn(kdec, vnew)
    o = qs + sum(av * _head_mask(h) for h, av in enumerate(avs))
    return o, state * dvec + bd * kv


def _dn_scan_fwd(ins, name):
    S = ins[0].shape[0]
    N = S // DN_CHUNK
    bd = _head_sum_matrix()

    def body(*refs):
        o_ref, s_ref, state = refs[10], refs[11], refs[12]

        @pl.when(pl.program_id(0) == 0)
        def _():
            state[...] = jnp.zeros_like(state)

        s_in = state[...]
        s_ref[0] = s_in
        o, s_out = _f_dn_step(*[r[...] for r in refs[:9]], s_in, refs[9][...])
        o_ref[...] = o
        state[...] = s_out

    return pl.pallas_call(
        body, name=name, grid=(N,),
        in_specs=[pl.BlockSpec((DN_CHUNK, t.shape[1]), lambda n: (n, 0)) for t in ins] + [_full_spec(bd)],
        out_specs=[pl.BlockSpec((DN_CHUNK, GROUP_W), lambda n: (n, 0)), pl.BlockSpec((1, GROUP_W, GROUP_W), lambda n: (n, 0, 0))],
        out_shape=[jax.ShapeDtypeStruct((S, GROUP_W), f32), jax.ShapeDtypeStruct((N, GROUP_W, GROUP_W), f32)],
        scratch_shapes=[pltpu.VMEM((GROUP_W, GROUP_W), f32)],
        compiler_params=_cparams(('arbitrary',)),
    )(*ins, bd)


def _dn_scan_bwd(ins, states, do, name):
    S = ins[0].shape[0]
    N = S // DN_CHUNK
    bd = _head_sum_matrix()

    def body(*refs):
        s_ref, do_ref = refs[9], refs[10]
        bd_ref = refs[11]
        outs = refs[12:21]
        dstate = refs[21]

        @pl.when(pl.program_id(0) == 0)
        def _():
            dstate[...] = jnp.zeros_like(dstate)

        bd_val = bd_ref[...]
        _, vjp = jax.vjp(lambda *a: _f_dn_step(*a, bd_val), *[r[...] for r in refs[:9]], s_ref[0])
        grads = vjp((do_ref[...], dstate[...]))
        for o, g in zip(outs, grads[:9]):
            o[...] = g
        dstate[...] = grads[9]

    def rev(n):
        return (N - 1 - n, 0)

    res = pl.pallas_call(
        body, name=name, grid=(N,),
        in_specs=[pl.BlockSpec((DN_CHUNK, t.shape[1]), rev) for t in ins] +
                 [pl.BlockSpec((1, GROUP_W, GROUP_W), lambda n: (N - 1 - n, 0, 0)), pl.BlockSpec((DN_CHUNK, GROUP_W), rev),
                  _full_spec(bd)],
        out_specs=[pl.BlockSpec((DN_CHUNK, t.shape[1]), rev) for t in ins],
        out_shape=[jax.ShapeDtypeStruct(t.shape, f32) for t in ins],
        scratch_shapes=[pltpu.VMEM((GROUP_W, GROUP_W), f32)],
        compiler_params=_cparams(('arbitrary',)),
    )(*ins, states, do, bd)
    return list(res)


def _f_dn_post(o, gate, gain, hmean):
    return (o * lax.rsqrt(_hdot(o * o, hmean) + EPS) * gain * (gate * jax.nn.sigmoid(gate)),)


def _dn_delays(x, name):
    S, C = x.shape
    tm = _pick(S, (256, 128))

    def body(prev_ref, cur_ref, *outs):
        before = jnp.where(pl.program_id(0) > 0, prev_ref[...], 0.0)
        both = jnp.concatenate([before, cur_ref[...]], axis=0)
        for o, k in zip(outs, range(DN_CONV - 1, 0, -1)):
            o[...] = pltpu.roll(both, k, 0)[tm:]

    spec = pl.BlockSpec((tm, C), lambda i: (i, 0))
    return pl.pallas_call(
        body, name=name, grid=(S // tm,),
        in_specs=[pl.BlockSpec((tm, C), lambda i: (jnp.maximum(i - 1, 0), 0)), spec],
        out_specs=[spec] * (DN_CONV - 1), out_shape=[jax.ShapeDtypeStruct((S, C), x.dtype)] * (DN_CONV - 1),
        compiler_params=_cparams(('parallel',), big=True),
    )(x, x)


def _dn_undelay_sum(ds, name):
    S, C = ds[0].shape
    tm = _pick(S, (256, 128))
    n = S // tm

    def body(*refs):
        o = refs[-1]
        total = refs[2 * (DN_CONV - 1)][...]
        for j in range(DN_CONV - 1):
            k = DN_CONV - 1 - j
            after = jnp.where(pl.program_id(0) < n - 1, refs[2 * j + 1][...], 0.0)
            both = jnp.concatenate([refs[2 * j][...], after], axis=0)
            total = total + pltpu.roll(both, 2 * tm - k, 0)[:tm]
        o[...] = total

    spec = pl.BlockSpec((tm, C), lambda i: (i, 0))
    nxt = pl.BlockSpec((tm, C), lambda i: (jnp.minimum(i + 1, n - 1), 0))
    args, in_specs = [], []
    for j in range(DN_CONV - 1):
        args += [ds[j], ds[j]]
        in_specs += [spec, nxt]
    return pl.pallas_call(
        body, name=name, grid=(n,), in_specs=in_specs + [spec], out_specs=spec,
        out_shape=jax.ShapeDtypeStruct((S, C), f32), compiler_params=_cparams(('parallel',), big=True),
    )(*args, ds[DN_CONV - 1])


def _dn_fwd(qkv, a, b, gate, mp, l):
    S = qkv.shape[0]
    tm = _pick(S, (256, 128))
    xs = [*_dn_delays(qkv, f'dn_delay_{l}'), qkv]
    ab = jnp.pad(jnp.concatenate([a, b], axis=1), ((0, 0), (0, LANES - 2 * N_HEADS)))
    sel = np.zeros((2, LANES, GROUP_W), np.float32)
    for h in range(N_HEADS):
        sel[0, h, h * HEAD_DIM:(h + 1) * HEAD_DIM] = 1.0
        sel[1, N_HEADS + h, h * HEAD_DIM:(h + 1) * HEAD_DIM] = 1.0
    pre_params = [*[mp['dn_conv'][j][None] for j in range(DN_CONV)], jnp.repeat(mp['dn_a_log'], HEAD_DIM)[None],
                  jnp.repeat(mp['dn_dt_bias'], HEAD_DIM)[None], jnp.asarray(sel[0]), jnp.asarray(sel[1]), _head_sum_matrix()]
    pre = _tile_fwd(_f_dn_pre, [*xs, ab], pre_params, [(GROUP_W, f32)] * 5, tm, f'dn_pre_fwd_{l}')
    chunk_outs = [(GROUP_W, f32)] * 4 + [(HEAD_DIM, f32)] * 4 + [(GROUP_W, f32)]
    parts = _tile_fwd(_f_dn_chunks, pre, [], chunk_outs, DN_CHUNK * DN_CHUNKS_PER_STEP, f'dn_chunk_fwd_{l}')
    o, states = _dn_scan_fwd(parts, f'dn_scan_fwd_{l}')
    post_params = [jnp.tile(mp['dn_o_norm'], N_HEADS)[None], _head_mean_matrix()]
    (y,) = _tile_fwd(_f_dn_post, [o, gate], post_params, [(GROUP_W, f32)], tm, f'dn_post_fwd_{l}')
    return y, (xs, ab, pre_params, pre, parts, states, o, gate, post_params)


def _dn_bwd(dy, saved, l):
    xs, ab, pre_params, pre, parts, states, o, gate, post_params = saved
    S = dy.shape[0]
    tm = _pick(S, (256, 128))
    (do, dgate), (dgain,) = _tile_bwd(_f_dn_post, [o, gate], post_params, [dy], [True, True], [True, False], tm,
                                      f'dn_post_bwd_{l}')
    dparts = _dn_scan_bwd(parts, states, do, f'dn_scan_bwd_{l}')
    dpre, _ = _tile_bwd(_f_dn_chunks, pre, [], dparts, [True] * 5, [], DN_CHUNK * DN_CHUNKS_PER_STEP, f'dn_chunk_bwd_{l}')
    dins, dpar = _tile_bwd(_f_dn_pre, [*xs, ab], pre_params, dpre, [True] * 5, [True] * 6 + [False] * 3, tm,
                           f'dn_pre_bwd_{l}')
    dqkv = _dn_undelay_sum(dins[:DN_CONV], f'dn_undelay_{l}')
    dab = dins[DN_CONV]
    grads = {'dn_conv': jnp.concatenate(dpar[:DN_CONV], axis=0),
             'dn_a_log': dpar[4].reshape(N_HEADS, HEAD_DIM).sum(1), 'dn_dt_bias': dpar[5].reshape(N_HEADS, HEAD_DIM).sum(1),
             'dn_o_norm': dgain.reshape(N_HEADS, HEAD_DIM).sum(0)}
    return dqkv, dab[:, :N_HEADS], dab[:, N_HEADS:2 * N_HEADS], dgate, grads


def _t5_bucket(dist):
    exact = T5_BUCKETS // 2
    df = jnp.maximum(dist, 1).astype(f32)
    large = exact + (jnp.log(df / exact) / math.log(T5_MAX_DIST / exact) * (T5_BUCKETS - exact)).astype(jnp.int32)
    large = jnp.minimum(large, T5_BUCKETS - 1)
    return jnp.where(dist < exact, dist, large)


def _split_cols(t, sizes):
    out, start = [], 0
    for s in sizes:
        out.append(t[..., start:start + s])
        start += s
    return out


def _mixers_fwd(proj, mp, l):
    c_q, c_kv, k_rope, u_s5, qkv_dil, qkv_dn, a_dn, b_dn, gate_dn = _split_cols(proj, IN_SPLITS)
    y_mla, s_mla = _mla_fwd(c_q, c_kv, k_rope, mp, l)
    y_s5, s_s5 = _s5_fwd(u_s5, mp, l)
    y_dil, s_dil = _dil_fwd(qkv_dil, mp, l)
    y_dn, s_dn = _dn_fwd(qkv_dn, a_dn, b_dn, gate_dn, mp, l)
    return jnp.concatenate([y_mla, y_s5, y_dil, y_dn], axis=-1), (s_mla, s_s5, s_dil, s_dn)


def _mixers_bwd(dmixed, saved, l):
    s_mla, s_s5, s_dil, s_dn = saved
    d_mla, d_s5, d_dil, d_dn = _split_cols(dmixed, (GROUP_W,) * 4)
    dc_q, dc_kv, dk_rope, g_mla = _mla_bwd(d_mla, s_mla, l)
    du, g_s5 = _s5_bwd(d_s5, s_s5, l)
    dqkv_dil, g_dil = _dil_bwd(d_dil, s_dil, l)
    dqkv_dn, da, db, dgate, g_dn = _dn_bwd(d_dn, s_dn, l)
    parts = [dc_q, dc_kv, dk_rope, du, dqkv_dil, dqkv_dn, da, db, dgate]
    dproj = jnp.concatenate([p.astype(bf16) for p in parts], axis=-1)
    return dproj, {**g_mla, **g_s5, **g_dil, **g_dn}


MIXER_PARAMS = ['mla_q_norm', 'mla_kv_norm', 'mla_w_uq', 'mla_w_ukv', 'mla_qk_q', 'mla_qk_k', 's5_lambda_re',
                's5_lambda_im', 's5_log_dt', 's5_b_re', 's5_b_im', 's5_c_re', 's5_c_im', 's5_d', 's5_w_glu',
                'dil_q_norm', 'dil_k_norm', 't5_bias', 'dn_conv', 'dn_a_log', 'dn_dt_bias', 'dn_o_norm']


def _layer_fwd_mix(h, W, l):
    S = h.shape[0]
    tm = _pick(S, (256, 128))
    g1 = W['attn_norm'][l][None]
    (n1,) = _tile_fwd(_f_rms, [h], [g1], [(D_MODEL, bf16)], tm, f'rms1_fwd_{l}')
    proj = _mm(n1, W['w_in'][l], 'nt', f'proj_fwd_{l}')
    mp = {k: (W[k] if k == 't5_bias' else W[k][l]).astype(f32) for k in MIXER_PARAMS}
    if 'dil_tables' in W:
        mp['dil_tables'] = W['dil_tables']
    mixed, mix_saved = _mixers_fwd(proj, mp, l)
    mixed_b = mixed.astype(bf16)
    h2 = _mm(mixed_b, W['w_out'][l], 'nn', f'out_fwd_{l}', add=h)
    return h2, dict(h=h, n1=n1, mix=mix_saved, mixed=mixed_b, h2=h2)


def _layer_fwd_ffn(h2, W, l, saved):
    S = h2.shape[0]
    tm = _pick(S, (256, 128))
    g2 = W['ffn_norm'][l][None]
    (n2,) = _tile_fwd(_f_rms, [h2], [g2], [(D_MODEL, bf16)], tm, f'rms2_fwd_{l}')
    u = _mm(n2, W['ffn_w1'][l], 'nt', f'ffn1_fwd_{l}', out_dtype=bf16)
    v = _mm(n2, W['ffn_w3'][l], 'nt', f'ffn3_fwd_{l}', out_dtype=bf16)
    (act,) = _tile_fwd(_f_swiglu, [u, v], [], [(FFN_HIDDEN, bf16)], tm, f'swiglu_fwd_{l}')
    h3 = _mm(act, W['ffn_w2'][l], 'nn', f'ffn2_fwd_{l}', add=h2)
    saved.update(n2=n2, u=u, v=v, act=act)
    return h3


def _layer_bwd_ffn(dh3, saved, W, l):
    S = dh3.shape[0]
    tm = _pick(S, (256, 128))
    g2 = W['ffn_norm'][l][None]
    grads = {}
    dact = _mm(dh3, W['ffn_w2'][l], 'nt', f'ffn2_dx_{l}', out_dtype=bf16)
    grads['ffn_w2'] = _mm(saved['act'], dh3, 'tn', f'ffn2_dw_{l}', out_dtype=bf16)
    (du, dv), _ = _tile_bwd(_f_swiglu, [saved['u'], saved['v']], [], [dact], [True, True], [], tm, f'swiglu_bwd_{l}',
                            dt_dtypes=[bf16, bf16])
    dn2 = _mm(dv, W['ffn_w3'][l], 'nn', f'ffn3_dx_{l}', add=_mm(du, W['ffn_w1'][l], 'nn', f'ffn1_dx_{l}'))
    grads['ffn_w1'] = _mm(du, saved['n2'], 'tn', f'ffn1_dw_{l}', out_dtype=bf16)
    grads['ffn_w3'] = _mm(dv, saved['n2'], 'tn', f'ffn3_dw_{l}', out_dtype=bf16)
    (dh2n,), (dg2,) = _tile_bwd(_f_rms, [saved['h2']], [g2], [dn2], [True], [True], tm, f'rms2_bwd_{l}')
    grads['ffn_norm'] = dg2[0]
    return (dh3, dh2n), grads


def _layer_bwd_mix(dh2, saved, W, l):
    S = dh2.shape[0]
    tm = _pick(S, (256, 128))
    g1 = W['attn_norm'][l][None]
    grads = {}
    dmixed = _mm(dh2, W['w_out'][l], 'nt', f'out_dx_{l}')
    grads['w_out'] = _mm(saved['mixed'], dh2, 'tn', f'out_dw_{l}', out_dtype=bf16)
    dproj, dmp = _mixers_bwd(dmixed, saved['mix'], l)
    grads.update(dmp)
    dn1 = _mm(dproj, W['w_in'][l], 'nn', f'proj_dx_{l}')
    grads['w_in'] = _mm(dproj, saved['n1'], 'tn', f'proj_dw_{l}', out_dtype=bf16)
    (dh1n,), (dg1,) = _tile_bwd(_f_rms, [saved['h']], [g1], [dn1], [True], [True], tm, f'rms1_bwd_{l}')
    grads['attn_norm'] = dg1[0]
    return (dh2, dh1n), grads


def kernel(x, attn_norm, w_in, w_out, mla_q_norm, mla_kv_norm, mla_w_uq, mla_w_ukv, mla_qk_q, mla_qk_k, s5_lambda_re, s5_lambda_im, s5_log_dt, s5_b_re, s5_b_im, s5_c_re, s5_c_im, s5_d, s5_w_glu, dil_q_norm, dil_k_norm, t5_bias, dn_conv, dn_a_log, dn_dt_bias, dn_o_norm, ffn_norm, ffn_w1, ffn_w3, ffn_w2, loss_target, m_attn_norm, m_w_in, m_w_out, m_mla_q_norm, m_mla_kv_norm, m_mla_w_uq, m_mla_w_ukv, m_mla_qk_q, m_mla_qk_k, m_s5_lambda_re, m_s5_lambda_im, m_s5_log_dt, m_s5_b_re, m_s5_b_im, m_s5_c_re, m_s5_c_im, m_s5_d, m_s5_w_glu, m_dil_q_norm, m_dil_k_norm, m_t5_bias, m_dn_conv, m_dn_a_log, m_dn_dt_bias, m_dn_o_norm, m_ffn_norm, m_ffn_w1, m_ffn_w3, m_ffn_w2, v_attn_norm, v_w_in, v_w_out, v_mla_q_norm, v_mla_kv_norm, v_mla_w_uq, v_mla_w_ukv, v_mla_qk_q, v_mla_qk_k, v_s5_lambda_re, v_s5_lambda_im, v_s5_log_dt, v_s5_b_re, v_s5_b_im, v_s5_c_re, v_s5_c_im, v_s5_d, v_s5_w_glu, v_dil_q_norm, v_dil_k_norm, v_t5_bias, v_dn_conv, v_dn_a_log, v_dn_dt_bias, v_dn_o_norm, v_ffn_norm, v_ffn_w1, v_ffn_w3, v_ffn_w2):
    given = dict(locals())
    def seen(n, t):
        if n in COLUMNS_FIRST:
            return jnp.transpose(t, (2, 0, 1))
        return jnp.swapaxes(t, 1, 2) if n in TRANSPOSED else t

    def given_back(n, t):
        return jnp.transpose(t, (1, 2, 0)) if n in COLUMNS_FIRST else seen(n, t)

    def layer_of(n, t, l):
        return t[:, l] if n in COLUMNS_FIRST else t[l]

    w_loc = {n: seen(n, given[n]) for n in WEIGHTS}
    m_loc = {n: seen(n, given['m_' + n]) for n in WEIGHTS}
    v_loc = {n: seen(n, given['v_' + n]) for n in WEIGHTS}
    big_names = list(BIG)

    own = 2 * lax.axis_index('x') + lax.axis_index('y')
    groups = [[(n, 0) for n in GATHER_FIRST], [(n, 0) for n in GATHER_FFN], [(n, 1) for n in big_names]]
    started, order = [], jnp.zeros((8, LANES), f32)
    for gi, group in enumerate(groups):
        blocks = [layer_of(n, w_loc[n], l).astype(bf16) for n, l in group]
        lands = [lax.empty((N_SHARDS,) + b.shape, bf16) for b in blocks]
        send_sems, recv_sems, blocks, lands, order = _to_chips_start(blocks, lands, False, order, f'gather_start_{gi}')
        started.append((send_sems, recv_sems, blocks, lands))
    W = {n: [None] * DEPTH for n in big_names}
    for n in SMALL:
        W[n] = w_loc[n]
    W['dil_tables'] = _dil_tables(w_loc['t5_bias'])

    def arrive(gi, after):
        send_sems, recv_sems, blocks, lands = started[gi]
        blocks, lands = _to_chips_wait(send_sems, recv_sems, blocks, lands, False, after, f'gather_wait_{gi}')
        for (n, l), block, land in zip(groups[gi], blocks, lands):
            W[n][l] = _from_shards(n, lax.dynamic_update_slice(land, block[None], (own, 0, 0)))

    arrive(0, order)
    h = x[0]
    saved = []
    for l in range(DEPTH):
        h2, sv = _layer_fwd_mix(h, W, l)
        if l == 0:
            arrive(1, h2)
        h = _layer_fwd_ffn(h2, W, l, sv)
        if l == 0:
            arrive(2, h)
        saved.append(sv)
    parts_loss, dh = _loss_head(h, loss_target[0])
    local_loss = jnp.sum(parts_loss)

    layer_grads = [dict() for _ in range(DEPTH)]
    sent = []

    def send(group, tag):
        srcs = [_by_shard(n, layer_grads[l][n]).astype(bf16) for n, l in group]
        lands = [lax.empty((3,) + s.shape[1:], bf16) for s in srcs]
        send_sems, recv_sems, srcs, lands, token = _to_chips_start(srcs, lands, True, jnp.zeros((8, LANES), f32),
                                                                   f'reduce_start_{tag}')
        sent.append((group, tag, send_sems, recv_sems, srcs, lands))
        return token[0, 0]

    for l in reversed(range(DEPTH)):
        (dh3, dh2n), g_ffn = _layer_bwd_ffn(dh, saved[l], W, l)
        layer_grads[l].update(g_ffn)
        dh2 = dh3 + dh2n
        if l == 0:
            dh2 = dh2 + send([(n, 0) for n in GATHER_FFN], 'ffn0')
        (dh2, dh1n), g_mix = _layer_bwd_mix(dh2, saved[l], W, l)
        layer_grads[l].update(g_mix)
        dh = dh2 + dh1n
        if l == 1:
            dh = dh + send([(n, 1) for n in big_names], 'layer1')
    last = send([(n, 0) for n in GATHER_FIRST], 'first0')
    grad_x = dh[None]
    small_full = []
    for n in SMALL:
        if n == 't5_bias':
            small_full.append(_t5_grad([a_ + b_ for a_, b_ in zip(layer_grads[0]['t5_tables'], layer_grads[1]['t5_tables'])]))
        else:
            small_full.append(jnp.stack([layer_grads[l][n] for l in range(DEPTH)]))

    small_shapes = [w_loc[n].shape for n in SMALL] + [(1,)]
    nothing = [jnp.zeros((1,), f32)]
    small_pack = _pack(small_full + [local_loss.reshape(1)]) + last
    _, recv_small = _swap_with_sibling([], small_pack)
    chip_small = _small_chip_sum(small_pack, recv_small)
    _, from_chips_small = _exchange_between_chips([], chip_small)

    mine = {}
    for group, tag, send_sems, recv_sems, srcs, lands in sent:
        srcs, lands = _to_chips_wait(send_sems, recv_sems, srcs, lands, True, from_chips_small, f'reduce_wait_{tag}')
        for (n, l), src, land in zip(group, srcs, lands):
            mine[(n, l)] = _partial_sum(src, land, f'partial_{n}_{l}')
    keys = [(n, l) for n in big_names for l in range(DEPTH)]
    theirs = dict(zip(keys, _swap_partials([mine[k] for k in keys])))

    g_small_p, d_small_p, m_small_p, v_small_p = _small_update(
        small_pack, recv_small, from_chips_small, _pack([w_loc[n] for n in SMALL] + nothing),
        _pack([m_loc[n] for n in SMALL] + nothing), _pack([v_loc[n] for n in SMALL] + nothing))
    loss = _unpack(g_small_p, small_shapes)[-1][0]
    grad, delta, new_m, new_v = {}, {}, {}, {}
    for n, g_, d_, m_, v_ in zip(SMALL, _unpack(g_small_p, small_shapes), _unpack(d_small_p, small_shapes),
                                 _unpack(m_small_p, small_shapes), _unpack(v_small_p, small_shapes)):
        grad[n], delta[n], new_m[n], new_v[n] = g_, d_, m_, v_
    for n in big_names:
        update = _adamw_layer_in_the_middle if n in COLUMNS_FIRST else _adamw
        results = update(w_loc[n], m_loc[n], v_loc[n], [mine[(n, l)] for l in range(DEPTH)],
                         [theirs[(n, l)] for l in range(DEPTH)], 'adamw_' + n)
        grad[n], delta[n], new_m[n], new_v[n] = [given_back(n, t) for t in results]
    return (loss, grad_x, *[grad[n] for n in WEIGHTS], *[delta[n] for n in WEIGHTS],
            *[new_m[n] for n in WEIGHTS], *[new_v[n] for n in WEIGHTS])
```

```python
import functools
import math

import numpy as np
import jax
import jax.numpy as jnp
from jax import lax
from jax.experimental import pallas as pl
from jax.experimental.pallas import tpu as pltpu

f32 = jnp.float32
bf16 = jnp.bfloat16
HI = lax.Precision.HIGHEST
MESH = pl.DeviceIdType.MESH

VMEM_LIMIT_BYTES = 48 * 1024 * 1024
MM_VMEM_BUDGET_BYTES = 32 * 1024 * 1024
LANES = 128

D_MODEL = 1024
DEPTH = 2
GROUP_W = 256
HEAD_DIM = 64
EPS = 1e-6
NEG_INF = -1e30
N_HEADS = 4
MLA_NOPE, MLA_ROPE = 64, 32
MLA_DQK = MLA_NOPE + MLA_ROPE
ROPE_THETA = 10000.0
Q_BLOCK = 128
S5_G, S5_CG, S5_P = 16, 16, 64
DIL_PAIRS = ((128, 1), (512, 4), (2048, 16))
T5_BUCKETS, T5_MAX_DIST = 32, 2048
DN_CHUNK = 64
FFN_HIDDEN = 2816
IN_SPLITS = (256, 128, 32, 256, 768, 768, 4, 4, 256)
IN_COLS = sum(IN_SPLITS)

ADAM_LR, ADAM_B1, ADAM_B2, ADAM_EPS, ADAM_WD, ADAM_STEP = 0.001, 0.9, 0.999, 1e-08, 0.01, 10

WEIGHTS = ['attn_norm', 'w_in', 'w_out', 'mla_q_norm', 'mla_kv_norm', 'mla_w_uq', 'mla_w_ukv', 'mla_qk_q', 'mla_qk_k',
           's5_lambda_re', 's5_lambda_im', 's5_log_dt', 's5_b_re', 's5_b_im', 's5_c_re', 's5_c_im', 's5_d', 's5_w_glu',
           'dil_q_norm', 'dil_k_norm', 't5_bias', 'dn_conv', 'dn_a_log', 'dn_dt_bias', 'dn_o_norm', 'ffn_norm',
           'ffn_w1', 'ffn_w3', 'ffn_w2']
BIG = {'w_in': 1, 'w_out': 1, 'mla_w_uq': 2, 'mla_w_ukv': 2, 's5_w_glu': 2, 'dn_conv': 2, 'ffn_w1': 1, 'ffn_w3': 1,
       'ffn_w2': 1}
TRANSPOSED = ('ffn_w1', 'ffn_w3')
COLUMNS_FIRST = ('w_in',)
SMALL = [n for n in WEIGHTS if n not in BIG]
GATHER_FIRST = ['w_in', 'mla_w_uq', 'mla_w_ukv', 's5_w_glu', 'dn_conv', 'w_out']
GATHER_FFN = ['ffn_w1', 'ffn_w3', 'ffn_w2']
N_SHARDS = 4
PACK_COLS = 1024


def _cparams(sem=None, big=False):
    kw = {}
    if sem is not None:
        kw['dimension_semantics'] = sem
    if big:
        kw['vmem_limit_bytes'] = VMEM_LIMIT_BYTES
    return pltpu.CompilerParams(**kw)


def _pick(n, prefs):
    for p in prefs:
        if p <= n and n % p == 0:
            return p
    return n


def _lane_tile(n, cap):
    for t in range(cap - cap % LANES, 0, -LANES):
        if n % t == 0:
            return t
    return n


def _mm(a, b, mode, name, add=None, out_dtype=f32):
    if mode == 'nn':
        (M, K), (K2, N) = a.shape, b.shape
    elif mode == 'nt':
        (M, K), (N, K2) = a.shape, b.shape
    else:
        (K, M), (K2, N) = a.shape, b.shape
    assert K == K2, (name, a.shape, b.shape)
    tk = K if K <= 2816 else _pick(K, (2816, 2048, 1408, 1024, 512))
    cap_m, cap_n = (1408 if mode == 'tn' else 512), 1408

    def need(tm_, tn_):
        per_step = tm_ * tk * a.dtype.itemsize + tk * tn_ * b.dtype.itemsize + tm_ * tn_ * jnp.dtype(out_dtype).itemsize
        if add is not None:
            per_step += tm_ * tn_ * add.dtype.itemsize
        return 2 * per_step + tm_ * tn_ * 4

    tm, tn = _lane_tile(M, cap_m), _lane_tile(N, cap_n)
    while need(tm, tn) > MM_VMEM_BUDGET_BYTES and cap_m > LANES:
        cap_m //= 2
        tm = _lane_tile(M, cap_m)
    while need(tm, tn) > MM_VMEM_BUDGET_BYTES and cap_n > LANES:
        cap_n //= 2
        tn = _lane_tile(N, cap_n)
    nk = K // tk
    dims = {'nn': (((1,), (0,)), ((), ())), 'nt': (((1,), (1,)), ((), ())), 'tn': (((0,), (0,)), ((), ()))}[mode]
    has_add = add is not None

    def body(*refs):
        a_ref, b_ref = refs[0], refs[1]
        add_ref = refs[2] if has_add else None
        o_ref = refs[3] if has_add else refs[2]
        part = lax.dot_general(a_ref[...].astype(bf16), b_ref[...].astype(bf16), dims, preferred_element_type=f32)
        if nk == 1:
            if has_add:
                part = part + add_ref[...].astype(f32)
            o_ref[...] = part.astype(out_dtype)
        else:
            acc_ref = refs[-1]
            k = pl.program_id(2)

            @pl.when(k == 0)
            def _():
                acc_ref[...] = part

            @pl.when(k > 0)
            def _():
                acc_ref[...] += part

            @pl.when(k == nk - 1)
            def _():
                r = acc_ref[...]
                if has_add:
                    r = r + add_ref[...].astype(f32)
                o_ref[...] = r.astype(out_dtype)

    if mode == 'nn':
        a_spec = pl.BlockSpec((tm, tk), lambda i, j, k: (i, k))
        b_spec = pl.BlockSpec((tk, tn), lambda i, j, k: (k, j))
    elif mode == 'nt':
        a_spec = pl.BlockSpec((tm, tk), lambda i, j, k: (i, k))
        b_spec = pl.BlockSpec((tn, tk), lambda i, j, k: (j, k))
    else:
        a_spec = pl.BlockSpec((tk, tm), lambda i, j, k: (k, i))
        b_spec = pl.BlockSpec((tk, tn), lambda i, j, k: (k, j))
    in_specs = [a_spec, b_spec]
    args = [a, b]
    if has_add:
        in_specs.append(pl.BlockSpec((tm, tn), lambda i, j, k: (i, j)))
        args.append(add)
    return pl.pallas_call(
        body, name=name, grid=(M // tm, N // tn, nk), in_specs=in_specs,
        out_specs=pl.BlockSpec((tm, tn), lambda i, j, k: (i, j)),
        out_shape=jax.ShapeDtypeStruct((M, N), out_dtype),
        scratch_shapes=[pltpu.VMEM((tm, tn), f32)] if nk > 1 else [],
        compiler_params=_cparams(('parallel', 'parallel', 'arbitrary'), big=True),
    )(*args)


def _full_spec(p):
    nd = p.ndim
    return pl.BlockSpec(p.shape, lambda i, _nd=nd: (0,) * _nd)


def _tile_fwd(f, tiled, params, outs, tm, name):
    S = tiled[0].shape[0]
    nt, npar = len(tiled), len(params)

    def body(*refs):
        vals = [r[...].astype(f32) for r in refs[:nt + npar]]
        res = f(*vals)
        for r, o in zip(res, refs[nt + npar:]):
            o[...] = r.astype(o.dtype)

    return pl.pallas_call(
        body, name=name, grid=(S // tm,),
        in_specs=[pl.BlockSpec((tm, t.shape[1]), lambda i: (i, 0)) for t in tiled] + [_full_spec(p) for p in params],
        out_specs=[pl.BlockSpec((tm, c), lambda i: (i, 0)) for c, _ in outs],
        out_shape=[jax.ShapeDtypeStruct((S, c), dt) for c, dt in outs],
        compiler_params=_cparams(('parallel',), big=True),
    )(*tiled, *params)


def _tile_bwd(f, tiled, params, cts, diff_t, diff_p, tm, name, dt_dtypes=None):
    S = tiled[0].shape[0]
    nt, npar, nc = len(tiled), len(params), len(cts)
    it = [i for i in range(nt) if diff_t[i]]
    ip = [i for i in range(npar) if diff_p[i]]
    if dt_dtypes is None:
        dt_dtypes = [f32] * len(it)

    def body(*refs):
        vals = [r[...].astype(f32) for r in refs[:nt + npar]]
        ct_vals = tuple(r[...].astype(f32) for r in refs[nt + npar:nt + npar + nc])
        out_refs = refs[nt + npar + nc:]

        def g(*dv):
            full = list(vals)
            for k, i in enumerate(it):
                full[i] = dv[k]
            for k, i in enumerate(ip):
                full[nt + i] = dv[len(it) + k]
            return tuple(f(*full))

        _, vjp = jax.vjp(g, *[vals[i] for i in it], *[vals[nt + i] for i in ip])
        grads = vjp(ct_vals)
        for k in range(len(it)):
            out_refs[k][...] = grads[k].astype(out_refs[k].dtype)
        step = pl.program_id(0)
        for k in range(len(ip)):
            o = out_refs[len(it) + k]
            gk = grads[len(it) + k]

            @pl.when(step == 0)
            def _(o=o, gk=gk):
                o[...] = gk

            @pl.when(step > 0)
            def _(o=o, gk=gk):
                o[...] += gk

    out_specs = [pl.BlockSpec((tm, tiled[i].shape[1]), lambda i_: (i_, 0)) for i in it] + [_full_spec(params[i]) for i in ip]
    out_shape = [jax.ShapeDtypeStruct(tiled[i].shape, dt_dtypes[k]) for k, i in enumerate(it)] + \
                [jax.ShapeDtypeStruct(params[i].shape, f32) for i in ip]
    res = pl.pallas_call(
        body, name=name, grid=(S // tm,),
        in_specs=[pl.BlockSpec((tm, t.shape[1]), lambda i: (i, 0)) for t in tiled] + [_full_spec(p) for p in params] +
                 [pl.BlockSpec((tm, c.shape[1]), lambda i: (i, 0)) for c in cts],
        out_specs=out_specs, out_shape=out_shape,
        compiler_params=_cparams(('arbitrary',), big=True),
    )(*tiled, *params, *cts)
    return list(res[:len(it)]), list(res[len(it):])


def _rms(x, g):
    return x * lax.rsqrt(jnp.mean(x * x, axis=-1, keepdims=True) + EPS) * g


def _f_rms(x, g):
    return (_rms(x, g),)


def _f_swiglu(u, v):
    return (u * jax.nn.sigmoid(u) * v,)


def _loss_head(y, target):
    S, D = y.shape
    tm = _pick(S, (256, 128))

    def body(y_ref, t_ref, part_ref, dy_ref):
        e = y_ref[...] - t_ref[...]
        dy_ref[...] = e * (1.0 / D)
        s = 0.5 * jnp.sum(jnp.sum(e * e, axis=1, keepdims=True), axis=0, keepdims=True) * (1.0 / D)
        r = lax.broadcasted_iota(jnp.int32, (8, LANES), 0)
        c = lax.broadcasted_iota(jnp.int32, (8, LANES), 1)
        part_ref[0] = jnp.where((r == 0) & (c == 0), s, 0.0)

    return pl.pallas_call(
        body, name='loss_head', grid=(S // tm,),
        in_specs=[pl.BlockSpec((tm, D), lambda i: (i, 0))] * 2,
        out_specs=[pl.BlockSpec((1, 8, LANES), lambda i: (i, 0, 0)), pl.BlockSpec((tm, D), lambda i: (i, 0))],
        out_shape=[jax.ShapeDtypeStruct((S // tm, 8, LANES), f32), jax.ShapeDtypeStruct((S, D), f32)],
        compiler_params=_cparams(('parallel',)),
    )(y, target)


def _pack_rows_of(shape):
    rows = -(-math.prod(shape) // PACK_COLS)
    return -(-rows // 8) * 8


def _pack(arrs):
    parts = []
    for a in arrs:
        rows = _pack_rows_of(a.shape)
        flat = a.astype(f32).reshape(-1)
        parts.append(jnp.pad(flat, (0, rows * PACK_COLS - flat.shape[0])).reshape(rows, PACK_COLS))
    return jnp.concatenate(parts, axis=0)


def _unpack(pack, shapes):
    out, row = [], 0
    for s in shapes:
        rows = _pack_rows_of(s)
        out.append(pack[row:row + rows].reshape(-1)[:math.prod(s)].reshape(s))
        row += rows
    return out


ANY = pl.BlockSpec(memory_space=pl.ANY)


def _place():
    return lax.axis_index('x'), lax.axis_index('y'), lax.axis_index('c')


def _where():
    return jnp.stack([lax.axis_index('c'), 2 * lax.axis_index('x') + lax.axis_index('y')]).astype(jnp.int32)


def _remote(src, dst, send_sems, recv_sems, k, to):
    return pltpu.make_async_remote_copy(src_ref=src, dst_ref=dst, send_sem=send_sems.at[k], recv_sem=recv_sems.at[k],
                                        device_id=to, device_id_type=MESH)


def _swap_with_sibling(gs, small):
    n = len(gs)

    def body(*refs):
        g_refs, s_ref = refs[:n], refs[n]
        r_refs, rs_ref = refs[n + 1:2 * n + 1], refs[2 * n + 1]
        send_sems, recv_sems = refs[2 * n + 2:]
        x, y, c = _place()
        sib = (x, y, 1 - c)
        cps = [_remote(g_refs[t].at[:, 1 - c], r_refs[t], send_sems, recv_sems, t, sib) for t in range(n)]
        cps.append(_remote(s_ref, rs_ref, send_sems, recv_sems, n, sib))
        for cp in cps:
            cp.start()
        for cp in cps:
            cp.wait()

    res = pl.pallas_call(
        body, name='swap_with_sibling', in_specs=[ANY] * (n + 1), out_specs=[ANY] * (n + 1),
        out_shape=[jax.ShapeDtypeStruct((N_SHARDS,) + g.shape[2:], g.dtype) for g in gs] +
                  [jax.ShapeDtypeStruct(small.shape, small.dtype)],
        scratch_shapes=[pltpu.SemaphoreType.DMA((n + 1,)), pltpu.SemaphoreType.DMA((n + 1,))],
    )(*gs, small)
    return list(res[:n]), res[n]


def _exchange_between_chips(cs, small):
    n = len(cs)

    def body(*refs):
        c_refs, s_ref = refs[:n], refs[n]
        r_refs, rs_ref = refs[n + 1:2 * n + 1], refs[2 * n + 1]
        send_sems, recv_sems = refs[2 * n + 2:]
        x, y, c = _place()
        chips = [(1 - x, y), (x, 1 - y), (1 - x, 1 - y)]
        cps = []
        for j, (px, py) in enumerate(chips):
            for t in range(n):
                cps.append(_remote(c_refs[t].at[2 * px + py], r_refs[t].at[j], send_sems, recv_sems, 3 * t + j, (px, py, c)))
            cps.append(_remote(s_ref, rs_ref.at[j], send_sems, recv_sems, 3 * n + j, (px, py, c)))
        for cp in cps:
            cp.start()
        for cp in cps:
            cp.wait()

    res = pl.pallas_call(
        body, name='exchange_between_chips', in_specs=[ANY] * (n + 1), out_specs=[ANY] * (n + 1),
        out_shape=[jax.ShapeDtypeStruct((3,) + c.shape[1:], c.dtype) for c in cs] +
                  [jax.ShapeDtypeStruct((3,) + small.shape, small.dtype)],
        scratch_shapes=[pltpu.SemaphoreType.DMA((3 * n + 3,)), pltpu.SemaphoreType.DMA((3 * n + 3,))],
    )(*cs, small)
    return list(res[:n]), res[n]


def _swap_partials(ts):
    n = len(ts)

    def body(*refs):
        t_refs, o_refs = refs[:n], refs[n:2 * n]
        send_sems, recv_sems = refs[2 * n:]
        x, y, c = _place()
        cps = [_remote(t_refs[t], o_refs[t], send_sems, recv_sems, t, (x, y, 1 - c)) for t in range(n)]
        for cp in cps:
            cp.start()
        for cp in cps:
            cp.wait()

    return pl.pallas_call(
        body, name='swap_partials', in_specs=[ANY] * n, out_specs=[ANY] * n,
        out_shape=[jax.ShapeDtypeStruct(t.shape, t.dtype) for t in ts],
        scratch_shapes=[pltpu.SemaphoreType.DMA((n,)), pltpu.SemaphoreType.DMA((n,))],
    )(*ts)


HBM = pl.BlockSpec(memory_space=pltpu.HBM)
SEM = pl.BlockSpec(memory_space=pltpu.SEMAPHORE)
DATAFLOW = pltpu.SideEffectType.DATAFLOW_SIDE_EFFECTING


def _in_hbm(t):
    return pltpu.with_memory_space_constraint(t, pltpu.HBM)


def _other_chips():
    x, y, c = _place()
    return [(1 - x, y, c), (x, 1 - y, c), (1 - x, 1 - y, c)]


def _to_chips_copies(src_refs, land_refs, send_sems, recv_sems, per_peer):
    x, y, _ = _place()
    cps = []
    for t, (src, land) in enumerate(zip(src_refs, land_refs)):
        for j, (px, py, pc) in enumerate(_other_chips()):
            s = src.at[2 * px + py] if per_peer else src
            d = land.at[j] if per_peer else land.at[2 * x + y]
            cps.append(_remote(s, d, send_sems, recv_sems, 3 * t + j, (px, py, pc)))
    return cps


def _to_chips_start(srcs, lands, per_peer, order, name):
    n = len(srcs)

    def body(*refs):
        src_refs, land_refs = refs[:n], refs[n:2 * n]
        send_sems, recv_sems = refs[2 * n + 1], refs[2 * n + 2]
        token = refs[-1]
        for cp in _to_chips_copies(src_refs, land_refs, send_sems, recv_sems, per_peer):
            cp.start()
        token[...] = jnp.zeros_like(token)

    res = pl.pallas_call(
        body, name=name, in_specs=[HBM] * (2 * n) + [ANY],
        out_specs=[SEM, SEM] + [HBM] * (2 * n) + [pl.BlockSpec(memory_space=pltpu.VMEM)],
        out_shape=[pltpu.SemaphoreType.DMA((3 * n,)), pltpu.SemaphoreType.DMA((3 * n,))] +
                  [pltpu.HBM(t.shape, t.dtype) for t in srcs] + [pltpu.HBM(t.shape, t.dtype) for t in lands] +
                  [jax.ShapeDtypeStruct((8, LANES), f32)],
        input_output_aliases={i: 2 + i for i in range(2 * n)},
        compiler_params=pltpu.CompilerParams(has_side_effects=DATAFLOW),
    )(*[_in_hbm(t) for t in srcs], *[_in_hbm(t) for t in lands], order)
    return res[0], res[1], list(res[2:2 + n]), list(res[2 + n:2 + 2 * n]), res[-1]


def _to_chips_wait(send_sems, recv_sems, srcs, lands, per_peer, after, name):
    n = len(srcs)

    def body(*refs):
        src_refs, land_refs = refs[:n], refs[n:2 * n]
        send_ref, recv_ref = refs[2 * n], refs[2 * n + 1]
        for cp in _to_chips_copies(src_refs, land_refs, send_ref, recv_ref, per_peer):
            cp.wait_send()
            cp.wait_recv()

    res = pl.pallas_call(
        body, name=name, in_specs=[HBM] * (2 * n) + [SEM, SEM, ANY],
        out_specs=[HBM] * (2 * n),
        out_shape=[pltpu.HBM(t.shape, t.dtype) for t in srcs] + [pltpu.HBM(t.shape, t.dtype) for t in lands],
        input_output_aliases={i: i for i in range(2 * n)},
        compiler_params=pltpu.CompilerParams(has_side_effects=DATAFLOW),
    )(*srcs, *lands, send_sems, recv_sems, after)
    return list(res[:n]), list(res[n:])


def _row_tile(a):
    return _pick(a, (512, 256, 128, 64, 32, 16, 8))


def _partial_sum(g, land, name):
    _, a, b = g.shape
    tr = _row_tile(a)

    def body(w_ref, g_ref, r_ref, o_ref):
        t = g_ref[0].astype(f32) + r_ref[0].astype(f32)
        t = t + r_ref[1].astype(f32)
        t = t + r_ref[2].astype(f32)
        o_ref[...] = t.astype(o_ref.dtype)

    return pl.pallas_call(
        body, name=name,
        grid_spec=pltpu.PrefetchScalarGridSpec(
            num_scalar_prefetch=1, grid=(a // tr,),
            in_specs=[pl.BlockSpec((1, tr, b), lambda i, w: (w[1], i, 0)), pl.BlockSpec((3, tr, b), lambda i, w: (0, i, 0))],
            out_specs=pl.BlockSpec((tr, b), lambda i, w: (i, 0))),
        out_shape=jax.ShapeDtypeStruct((a, b), bf16),
        compiler_params=_cparams(('parallel',)),
    )(_where(), g, land)


def _by_shard(name, t):
    r, c = t.shape
    if BIG[name] == 2:
        return t.reshape(r, N_SHARDS, c // N_SHARDS).transpose(1, 0, 2)
    return t.reshape(N_SHARDS, r // N_SHARDS, c)


def _from_shards(name, g):
    s, a, b = g.shape
    if BIG[name] == 2:
        return g.transpose(1, 0, 2).reshape(a, s * b)
    return g.reshape(s * a, b)


def _adam_math(w, g, m, v):
    m = ADAM_B1 * m + (1.0 - ADAM_B1) * g
    v = ADAM_B2 * v + (1.0 - ADAM_B2) * (g * g)
    m_hat = m / (1.0 - ADAM_B1 ** ADAM_STEP)
    v_hat = v / (1.0 - ADAM_B2 ** ADAM_STEP)
    delta = -ADAM_LR * (m_hat / (jnp.sqrt(v_hat) + ADAM_EPS) + ADAM_WD * w)
    return delta, m, v


def _small_update(own, sib, chips, w, m, v):
    def body(o_ref, s_ref, c_ref, w_ref, m_ref, v_ref, g_out, d_out, m_out, v_out):
        chip = o_ref[...] + s_ref[...]
        g = (chip + c_ref[0]) + (c_ref[1] + c_ref[2])
        d, mn, vn = _adam_math(w_ref[...], g, m_ref[...], v_ref[...])
        g_out[...] = g
        d_out[...] = d
        m_out[...] = mn
        v_out[...] = vn

    return pl.pallas_call(body, name='small_update', out_shape=[jax.ShapeDtypeStruct(own.shape, f32)] * 4)(
        own, sib, chips, w, m, v)


def _small_chip_sum(own, sib):
    def body(o_ref, s_ref, out):
        out[...] = o_ref[...] + s_ref[...]
    return pl.pallas_call(body, name='small_chip_sum', out_shape=jax.ShapeDtypeStruct(own.shape, f32))(own, sib)


def _adamw(w, m, v, mine, theirs, name):
    layers, a, b = w.shape
    tr = _row_tile(a)

    def body(w_ref, m_ref, v_ref, p0, p1, q0, q1, g_out, d_out, m_out, v_out):
        first = pl.program_id(0) == 0
        g = jnp.where(first, p0[...].astype(f32) + q0[...].astype(f32), p1[...].astype(f32) + q1[...].astype(f32))
        d, mn, vn = _adam_math(w_ref[0], g, m_ref[0], v_ref[0])
        g_out[0] = g
        d_out[0] = d
        m_out[0] = mn
        v_out[0] = vn

    full = pl.BlockSpec((1, tr, b), lambda l, i: (l, i, 0))
    part = pl.BlockSpec((tr, b), lambda l, i: (i, 0))
    return pl.pallas_call(body, name=name, grid=(layers, a // tr), in_specs=[full] * 3 + [part] * 4, out_specs=[full] * 4,
                          out_shape=[jax.ShapeDtypeStruct(w.shape, f32)] * 4,
                          compiler_params=_cparams(('parallel', 'parallel')))(w, m, v, *mine, *theirs)


def _adamw_layer_in_the_middle(w, m, v, mine, theirs, name):
    a, layers, b = w.shape
    assert layers == 2 and b % LANES == 0

    def body(w_ref, m_ref, v_ref, p0, p1, q0, q1, g_out, d_out, m_out, v_out):
        g = jnp.stack([p0[...].astype(f32) + q0[...].astype(f32), p1[...].astype(f32) + q1[...].astype(f32)], axis=1)
        d, mn, vn = _adam_math(w_ref[...], g, m_ref[...], v_ref[...])
        g_out[...] = g
        d_out[...] = d
        m_out[...] = mn
        v_out[...] = vn

    full = pl.BlockSpec((a, layers, LANES), lambda i: (0, 0, i))
    part = pl.BlockSpec((a, LANES), lambda i: (0, i))
    return pl.pallas_call(body, name=name, grid=(b // LANES,), in_specs=[full] * 3 + [part] * 4, out_specs=[full] * 4,
                          out_shape=[jax.ShapeDtypeStruct(w.shape, f32)] * 4,
                          compiler_params=_cparams(('parallel',), big=True))(w, m, v, *mine, *theirs)


def _dg(a, b, ca, cb):
    return lax.dot_general(a.astype(bf16), b.astype(bf16), (((ca,), (cb,)), ((), ())), preferred_element_type=f32)


@jax.custom_vjp
def _bmm(a, b):
    return _dg(a, b, 1, 0)


_bmm.defvjp(lambda a, b: (_dg(a, b, 1, 0), (a, b)), lambda r, g: (_dg(g, r[1], 1, 1), _dg(r[0], g, 0, 0)))


@jax.custom_vjp
def _bmm_nt(a, b):
    return _dg(a, b, 1, 1)


_bmm_nt.defvjp(lambda a, b: (_dg(a, b, 1, 1), (a, b)), lambda r, g: (_dg(g, r[1], 1, 0), _dg(g, r[0], 0, 0)))


@jax.custom_vjp
def _bmm_tn(a, b):
    return _dg(a, b, 0, 0)


_bmm_tn.defvjp(lambda a, b: (_dg(a, b, 0, 0), (a, b)), lambda r, g: (_dg(r[1], g, 1, 1), _dg(r[0], g, 1, 0)))


def _hdot(a, b):
    return jnp.dot(a, b, precision=HI, preferred_element_type=f32)


def _hdot_nt(a, b):
    return lax.dot_general(a, b, (((1,), (1,)), ((), ())), precision=HI, preferred_element_type=f32)


def _hdot_tn(a, b):
    return lax.dot_general(a, b, (((0,), (0,)), ((), ())), precision=HI, preferred_element_type=f32)


def _head_mask(h, width=GROUP_W):
    lane = lax.broadcasted_iota(jnp.int32, (1, width), 1)
    return ((lane >= h * HEAD_DIM) & (lane < (h + 1) * HEAD_DIM)).astype(f32)


def _rope_perm():
    p = np.zeros((LANES, LANES), np.float32)
    half = MLA_ROPE // 2
    for i in range(half):
        p[MLA_NOPE + half + i, MLA_NOPE + i] = -1.0
        p[MLA_NOPE + i, MLA_NOPE + half + i] = 1.0
    return jnp.asarray(p)


def _rope_tables(S):
    half = MLA_ROPE // 2
    freqs = ROPE_THETA ** (-jnp.arange(half, dtype=f32) / half)
    ang = jnp.arange(S, dtype=f32)[:, None] * freqs[None, :]
    cos, sin = jnp.cos(ang), jnp.sin(ang)
    ones, zeros = jnp.ones((S, MLA_NOPE), f32), jnp.zeros((S, LANES - MLA_DQK), f32)
    c_tab = jnp.concatenate([ones, cos, cos, zeros], axis=1)
    s_tab = jnp.concatenate([jnp.zeros((S, MLA_NOPE), f32), sin, sin, zeros], axis=1)
    return c_tab, s_tab


def _f_mla_pre(c_q, c_kv, krope, c_tab, s_tab, q_norm, kv_norm, wq0, wq1, wq2, wq3, wk0, wk1, wk2, wk3, wv, gq, gk, perm):
    wq, wk = (wq0, wq1, wq2, wq3), (wk0, wk1, wk2, wk3)
    nq = _rms(c_q, q_norm)
    nkv = _rms(c_kv, kv_norm)

    def norm_rope(t, g):
        t = t * lax.rsqrt(jnp.sum(t * t, axis=-1, keepdims=True) * (1.0 / MLA_DQK) + EPS) * g
        return t * c_tab + _hdot(t, perm) * s_tab

    qs = [norm_rope(_bmm(nq, wq[h]), gq) * (MLA_DQK ** -0.5) for h in range(N_HEADS)]
    ks = [norm_rope(_bmm(nkv, wk[h]) + krope, gk) for h in range(N_HEADS)]
    return (*qs, *ks, _bmm(nkv, wv))


def _f_attn(qs, ks, v, q0):
    tq, S = qs[0].shape[0], ks[0].shape[0]
    qpos = q0 + lax.broadcasted_iota(jnp.int32, (tq, S), 0)
    kpos = lax.broadcasted_iota(jnp.int32, (tq, S), 1)
    keep = kpos <= qpos
    logits = [jnp.where(keep, _bmm_nt(qs[h], ks[h]), NEG_INF) for h in range(N_HEADS)]
    ps = [jnp.exp(lg - jnp.max(lg, axis=-1, keepdims=True)) for lg in logits]
    ps = [p / jnp.sum(p, axis=-1, keepdims=True) for p in ps]
    return sum(_bmm(p, v) * _head_mask(h) for h, p in enumerate(ps))


ATTN_PARTS = 4


def _mla_attn_fwd(qs, ks, v, name):
    S = v.shape[0]
    tq = 2 * Q_BLOCK if S % (2 * ATTN_PARTS * Q_BLOCK) == 0 else Q_BLOCK
    parts = ATTN_PARTS if S % (ATTN_PARTS * tq) == 0 else 1
    per = S // parts
    outs = []
    for p in range(parts):
        n_keys = (p + 1) * per
        first_block = p * (per // tq)

        def body(*refs, first_block=first_block):
            q_vals = [r[...] for r in refs[:4]]
            k_vals = [r[...] for r in refs[4:8]]
            refs[9][...] = _f_attn(q_vals, k_vals, refs[8][...], (first_block + pl.program_id(0)) * tq)

        qspec = pl.BlockSpec((tq, LANES), lambda i, fb=first_block: (fb + i, 0))
        outs.append(pl.pallas_call(
            body, name=f'{name}_{p}', grid=(per // tq,),
            in_specs=[qspec] * 4 + [pl.BlockSpec((n_keys, LANES), lambda i: (0, 0))] * 4 +
                     [pl.BlockSpec((n_keys, GROUP_W), lambda i: (0, 0))],
            out_specs=pl.BlockSpec((tq, GROUP_W), lambda i: (i, 0)),
            out_shape=jax.ShapeDtypeStruct((per, GROUP_W), f32),
            compiler_params=_cparams(('parallel',), big=True),
        )(*qs, *ks, v))
    return jnp.concatenate(outs, axis=0)


def _mla_attn_bwd(qs, ks, v, do, name):
    S = v.shape[0]
    tq = Q_BLOCK
    parts = ATTN_PARTS if S % (ATTN_PARTS * tq) == 0 else 1
    per = S // parts
    dq_parts, dkv_sum = [], None
    for p in range(parts):
        n_keys = (p + 1) * per
        first_block = p * (per // tq)

        def body(*refs, first_block=first_block):
            q_vals = [r[...].astype(f32) for r in refs[:4]]
            k_vals = [r[...].astype(f32) for r in refs[4:8]]
            v_val = refs[8][...].astype(f32)
            q0 = (first_block + pl.program_id(0)) * tq
            _, vjp = jax.vjp(lambda a, b, c: _f_attn(a, b, c, q0), q_vals, k_vals, v_val)
            dqs, dks, dv = vjp(refs[9][...])
            outs = refs[10:]
            for h in range(N_HEADS):
                outs[h][...] = dqs[h]
            first = pl.program_id(0) == 0
            for o, g in zip(outs[4:], (*dks, dv)):
                @pl.when(first)
                def _(o=o, g=g):
                    o[...] = g

                @pl.when(jnp.logical_not(first))
                def _(o=o, g=g):
                    o[...] += g

        qspec = pl.BlockSpec((tq, LANES), lambda i, fb=first_block: (fb + i, 0))
        kspec = pl.BlockSpec((n_keys, LANES), lambda i: (0, 0))
        vspec = pl.BlockSpec((n_keys, GROUP_W), lambda i: (0, 0))
        res = pl.pallas_call(
            body, name=f'{name}_{p}', grid=(per // tq,),
            in_specs=[qspec] * 4 + [kspec] * 4 + [vspec, pl.BlockSpec((tq, GROUP_W), lambda i, fb=first_block: (fb + i, 0))],
            out_specs=[pl.BlockSpec((tq, LANES), lambda i: (i, 0))] * 4 + [kspec] * 4 + [vspec],
            out_shape=[jax.ShapeDtypeStruct((per, LANES), f32)] * 4 + [jax.ShapeDtypeStruct((n_keys, LANES), f32)] * 4 +
                      [jax.ShapeDtypeStruct((n_keys, GROUP_W), f32)],
            compiler_params=_cparams(('arbitrary',), big=True),
        )(*qs, *ks, v, do)
        dq_parts.append(res[:4])
        dkv = [jnp.pad(t, ((0, S - n_keys), (0, 0))) for t in res[4:]]
        dkv_sum = dkv if dkv_sum is None else [a_ + b_ for a_, b_ in zip(dkv_sum, dkv)]
    dqs = [jnp.concatenate([dq_parts[p][h] for p in range(parts)], axis=0) for h in range(N_HEADS)]
    return dqs, dkv_sum[:4], dkv_sum[4]


def _mla_params(mp):
    pad = LANES - MLA_DQK
    wq = jnp.pad(mp['mla_w_uq'].reshape(GROUP_W, N_HEADS, MLA_DQK).transpose(1, 0, 2), ((0, 0), (0, 0), (0, pad)))
    wkv = mp['mla_w_ukv'].reshape(LANES, N_HEADS, MLA_NOPE + HEAD_DIM)
    wk = jnp.pad(wkv[:, :, :MLA_NOPE].transpose(1, 0, 2), ((0, 0), (0, 0), (0, LANES - MLA_NOPE)))
    wv = wkv[:, :, MLA_NOPE:].reshape(LANES, GROUP_W)
    gq = jnp.pad(mp['mla_qk_q'], (0, pad))[None]
    gk = jnp.pad(mp['mla_qk_k'], (0, pad))[None]
    return [mp['mla_q_norm'][None], mp['mla_kv_norm'][None], *[wq[h] for h in range(N_HEADS)],
            *[wk[h] for h in range(N_HEADS)], wv, gq, gk, _rope_perm()]


def _mla_fwd(c_q, c_kv, k_rope, mp, l):
    S = c_q.shape[0]
    tm = _pick(S, (256, 128))
    krope = jnp.pad(k_rope, ((0, 0), (MLA_NOPE, LANES - MLA_DQK)))
    c_tab, s_tab = _rope_tables(S)
    tiled = [c_q, c_kv, krope, c_tab, s_tab]
    params = _mla_params(mp)
    res = _tile_fwd(_f_mla_pre, tiled, params, [(LANES, bf16)] * 8 + [(GROUP_W, bf16)], tm, f'mla_pre_fwd_{l}')
    qs, ks, v = res[:4], res[4:8], res[8]
    y = _mla_attn_fwd(qs, ks, v, f'mla_attn_fwd_{l}')
    return y, (tiled, params, qs, ks, v)


def _mla_bwd(dy, saved, l):
    tiled, params, qs, ks, v = saved
    S = dy.shape[0]
    tm = _pick(S, (256, 128))
    dqs, dks, dv = _mla_attn_bwd(qs, ks, v, dy, f'mla_attn_bwd_{l}')
    (dc_q, dc_kv, dkrope), dpar = _tile_bwd(_f_mla_pre, tiled, params, [*dqs, *dks, dv], [True, True, True, False, False],
                                            [True] * 13 + [False], tm, f'mla_pre_bwd_{l}')
    dqn, dkvn = dpar[0], dpar[1]
    dwq, dwk = jnp.stack(dpar[2:6]), jnp.stack(dpar[6:10])
    dwv, dgq, dgk = dpar[10:13]
    dw_uq = dwq[:, :, :MLA_DQK].transpose(1, 0, 2).reshape(GROUP_W, N_HEADS * MLA_DQK)
    dw_ukv = jnp.concatenate([dwk[:, :, :MLA_NOPE].transpose(1, 0, 2), dwv.reshape(LANES, N_HEADS, HEAD_DIM)],
                             axis=2).reshape(LANES, N_HEADS * (MLA_NOPE + HEAD_DIM))
    grads = {'mla_q_norm': dqn[0], 'mla_kv_norm': dkvn[0], 'mla_w_uq': dw_uq, 'mla_w_ukv': dw_ukv,
             'mla_qk_q': dgq[0, :MLA_DQK], 'mla_qk_k': dgk[0, :MLA_DQK]}
    return dc_q, dc_kv, dkrope[:, MLA_NOPE:MLA_DQK], grads


SPAN = 128


def _head_mean_matrix():
    h = np.arange(GROUP_W) // HEAD_DIM
    return jnp.asarray((h[:, None] == h[None, :]).astype(np.float32) / HEAD_DIM)


def _f_dil_pre(q, k, gq, gk, hm):
    qn = q * lax.rsqrt(_hdot(q * q, hm) + EPS) * gq * (HEAD_DIM ** -0.5)
    kn = k * lax.rsqrt(_hdot(k * k, hm) + EPS) * gk
    return qn, kn


def _f_dil_branch(qb, kp, kc, vp, vc, b0, b1, b2, b3, first):
    kcat = jnp.concatenate([kp, kc], axis=0)
    vcat = jnp.concatenate([vp, vc], axis=0)
    qi = lax.broadcasted_iota(jnp.int32, (SPAN, 2 * SPAN), 0) + SPAN
    kj = lax.broadcasted_iota(jnp.int32, (SPAN, 2 * SPAN), 1)
    delta = qi - kj
    valid = (delta >= 0) & (delta <= SPAN) & jnp.logical_not(first & (kj < SPAN))
    masks = [_head_mask(h) for h in range(N_HEADS)]
    raw = [_bmm_nt(qb * hm, kcat) for hm in masks]
    logits = [jnp.where(valid, r + bias, NEG_INF) for r, bias in zip(raw, (b0, b1, b2, b3))]
    ms = [jnp.max(lg, axis=-1, keepdims=True) for lg in logits]
    ps = [jnp.exp(lg - m) for lg, m in zip(logits, ms)]
    pvs = [_bmm(p, vcat) for p in ps]
    o = sum(pv * hm for pv, hm in zip(pvs, masks))
    m_full = sum(m * hm for m, hm in zip(ms, masks))
    l_full = sum(jnp.sum(p, axis=-1, keepdims=True) * hm for p, hm in zip(ps, masks))
    return o, m_full, l_full


def _dil_branch_specs(d, nb):
    cur = pl.BlockSpec((SPAN, GROUP_W), lambda r, n: (n, r))
    prev = pl.BlockSpec((SPAN, GROUP_W), lambda r, n: (jnp.maximum(n - 1, 0), r))
    bias = pl.BlockSpec((1, SPAN, 2 * SPAN), lambda r, n: (0, 0, 0))
    return cur, prev, bias


def _head_table_specs():
    return [pl.BlockSpec((1, SPAN, 2 * SPAN), lambda r, n, h=h: (h, 0, 0)) for h in range(N_HEADS)]


def _dil_branch_fwd(q, k, v, table, name):
    L, d = q.shape[0], q.shape[1] // GROUP_W
    nb = L // SPAN
    cur, prev, bias = _dil_branch_specs(d, nb)

    def body(q_ref, kp_ref, kc_ref, vp_ref, vc_ref, b0, b1, b2, b3, o_ref, m_ref, l_ref):
        o, m, l = _f_dil_branch(*[r[...].astype(f32) for r in (q_ref, kp_ref, kc_ref, vp_ref, vc_ref)], b0[0], b1[0], b2[0], b3[0],
                                pl.program_id(1) == 0)
        o_ref[...] = o
        m_ref[...] = m
        l_ref[...] = l

    return pl.pallas_call(
        body, name=name, grid=(d, nb), in_specs=[cur, prev, cur, prev, cur] + _head_table_specs(),
        out_specs=[cur] * 3, out_shape=[jax.ShapeDtypeStruct(q.shape, f32)] * 3,
        compiler_params=_cparams(('parallel', 'parallel')),
    )(q, k, k, v, v, *[table] * N_HEADS)


def _dil_branch_bwd(q, k, v, table, do, dm, dl, name):
    L, d = q.shape[0], q.shape[1] // GROUP_W
    nb = L // SPAN
    cur, prev, bias = _dil_branch_specs(d, nb)
    whole = pl.BlockSpec((L, GROUP_W), lambda r, n: (0, r))

    def body(q_ref, kp_ref, kc_ref, vp_ref, vc_ref, b0, b1, b2, b3, do_ref, dm_ref, dl_ref,
             dq_ref, dk_ref, dv_ref, db0, db1, db2, db3):
        r, n = pl.program_id(0), pl.program_id(1)
        first = n == 0
        _, vjp = jax.vjp(lambda *a: _f_dil_branch(*a, first), *[r[...].astype(f32) for r in (q_ref, kp_ref, kc_ref, vp_ref, vc_ref)],
                         b0[0], b1[0], b2[0], b3[0])
        dq, dkp, dkc, dvp, dvc, g0, g1, g2, g3 = vjp((do_ref[...], dm_ref[...], dl_ref[...]))
        dq_ref[...] = dq

        @pl.when(first)
        def _():
            dk_ref[...] = jnp.zeros_like(dk_ref)
            dv_ref[...] = jnp.zeros_like(dv_ref)

        rows = pl.ds(pl.multiple_of(n * SPAN, SPAN), SPAN)
        dk_ref[rows, :] += dkc
        dv_ref[rows, :] += dvc

        @pl.when(n > 0)
        def _():
            before = pl.ds(pl.multiple_of((n - 1) * SPAN, SPAN), SPAN)
            dk_ref[before, :] += dkp
            dv_ref[before, :] += dvp

        start = first & (r == 0)
        for o, g in zip((db0, db1, db2, db3), (g0, g1, g2, g3)):
            @pl.when(start)
            def _(o=o, g=g):
                o[0] = g

            @pl.when(jnp.logical_not(start))
            def _(o=o, g=g):
                o[0] += g

    res = pl.pallas_call(
        body, name=name, grid=(d, nb), in_specs=[cur, prev, cur, prev, cur] + _head_table_specs() + [cur] * 3,
        out_specs=[cur, whole, whole] + [bias] * 4,
        out_shape=[jax.ShapeDtypeStruct(q.shape, f32)] * 3 + [jax.ShapeDtypeStruct((1, SPAN, 2 * SPAN), f32)] * 4,
        compiler_params=_cparams(('arbitrary', 'arbitrary')),
    )(q, k, k, v, v, *[table] * N_HEADS, do, dm, dl)
    return res[0], res[1], res[2], res[3:]


def _f_dil_merge(o1, m1, l1, o2, m2, l2, o3, m3, l3):
    mx = jnp.maximum(jnp.maximum(m1, m2), m3)
    w1, w2, w3 = jnp.exp(m1 - mx), jnp.exp(m2 - mx), jnp.exp(m3 - mx)
    return ((w1 * o1 + w2 * o2 + w3 * o3) / (w1 * l1 + w2 * l2 + w3 * l3),)


def _bias_onehot(dilation):
    qi = jnp.arange(SPAN, dtype=jnp.int32)[:, None] + SPAN
    kj = jnp.arange(2 * SPAN, dtype=jnp.int32)[None, :]
    bucket = _t5_bucket(jnp.clip(qi - kj, 0, SPAN) * dilation).reshape(-1)
    return (bucket[None, :] == jnp.arange(T5_BUCKETS, dtype=jnp.int32)[:, None]).astype(f32)


def _bias_tables(t5_t, onehot, name):
    N = onehot.shape[1]
    tn = _pick(N, (4096, 2048, 1024))

    def body(t_ref, oh_ref, o_ref):
        o_ref[...] = _hdot(t_ref[...], oh_ref[...])

    return pl.pallas_call(
        body, name=name, grid=(N // tn,),
        in_specs=[pl.BlockSpec((8, T5_BUCKETS), lambda i: (0, 0)), pl.BlockSpec((T5_BUCKETS, tn), lambda i: (0, i))],
        out_specs=pl.BlockSpec((8, tn), lambda i: (0, i)), out_shape=jax.ShapeDtypeStruct((8, N), f32),
        compiler_params=_cparams(('parallel',)),
    )(t5_t, onehot)


def _bias_tables_bwd(d_tab, onehot, name):
    N = onehot.shape[1]
    tn = _pick(N, (4096, 2048, 1024))

    def body(g_ref, oh_ref, o_ref):
        part = _hdot_nt(g_ref[...], oh_ref[...])

        @pl.when(pl.program_id(0) == 0)
        def _():
            o_ref[...] = part

        @pl.when(pl.program_id(0) > 0)
        def _():
            o_ref[...] += part

    return pl.pallas_call(
        body, name=name, grid=(N // tn,),
        in_specs=[pl.BlockSpec((8, tn), lambda i: (0, i)), pl.BlockSpec((T5_BUCKETS, tn), lambda i: (0, i))],
        out_specs=pl.BlockSpec((8, T5_BUCKETS), lambda i: (0, 0)), out_shape=jax.ShapeDtypeStruct((8, T5_BUCKETS), f32),
        compiler_params=_cparams(('arbitrary',)),
    )(d_tab, onehot)


def _by_residue(t, d):
    S, C = t.shape
    return t.reshape(S // d, d * C)


def _from_residue(t):
    return t.reshape(-1, GROUP_W)


def _dil_tables(t5_bias):
    t5_t = jnp.pad(t5_bias.T, ((0, 8 - N_HEADS), (0, 0)))
    return [_bias_tables(t5_t, _bias_onehot(d), f'dil_bias_fwd_{bi}').reshape(8, SPAN, 2 * SPAN)
            for bi, (_, d) in enumerate(DIL_PAIRS)]


def _t5_grad(d_tables):
    total = None
    for bi, (_, d) in enumerate(DIL_PAIRS):
        g = _bias_tables_bwd(d_tables[bi], _bias_onehot(d), f'dil_bias_bwd_{bi}')
        total = g if total is None else total + g
    return total[:N_HEADS].T


def _dil_fwd(qkv, mp, l):
    S = qkv.shape[0]
    tm = _pick(S, (256, 128))
    q, k, v = qkv[:, :GROUP_W], qkv[:, GROUP_W:2 * GROUP_W], qkv[:, 2 * GROUP_W:]
    pre_params = [jnp.tile(mp['dil_q_norm'], N_HEADS)[None], jnp.tile(mp['dil_k_norm'], N_HEADS)[None], _head_mean_matrix()]
    qn, kn = _tile_fwd(_f_dil_pre, [q, k], pre_params, [(GROUP_W, bf16)] * 2, tm, f'dil_pre_fwd_{l}')
    v = v.astype(bf16)
    tables = mp['dil_tables'] if 'dil_tables' in mp else _dil_tables(mp['t5_bias'])
    branches, outs = [], []
    for bi, (_, d) in enumerate(DIL_PAIRS):
        tab = tables[bi]
        qd, kd, vd = _by_residue(qn, d), _by_residue(kn, d), _by_residue(v, d)
        o, m, lsum = _dil_branch_fwd(qd, kd, vd, tab, f'dil_branch_fwd_{l}_{bi}')
        branches.append((qd, kd, vd, tab))
        outs += [_from_residue(o), _from_residue(m), _from_residue(lsum)]
    (y,) = _tile_fwd(_f_dil_merge, outs, [], [(GROUP_W, f32)], tm, f'dil_merge_fwd_{l}')
    return y, (q, k, pre_params, branches, outs)


def _dil_bwd(dy, saved, l):
    q, k, pre_params, branches, outs = saved
    S = dy.shape[0]
    tm = _pick(S, (256, 128))
    douts, _ = _tile_bwd(_f_dil_merge, outs, [], [dy], [True] * 9, [], tm, f'dil_merge_bwd_{l}')
    dqn = dkn = dv = None
    d_tabs = []
    for bi, (_, d) in enumerate(DIL_PAIRS):
        qd, kd, vd, tab = branches[bi]
        do, dm, dl = [_by_residue(t, d) for t in douts[3 * bi:3 * bi + 3]]
        dq_b, dk_b, dv_b, dbias = _dil_branch_bwd(qd, kd, vd, tab, do, dm, dl, f'dil_branch_bwd_{l}_{bi}')
        d_tabs.append(jnp.concatenate([*dbias, jnp.zeros((8 - N_HEADS, SPAN, 2 * SPAN), f32)], axis=0).reshape(8, -1))
        dq_b, dk_b, dv_b = _from_residue(dq_b), _from_residue(dk_b), _from_residue(dv_b)
        dqn = dq_b if dqn is None else dqn + dq_b
        dkn = dk_b if dkn is None else dkn + dk_b
        dv = dv_b if dv is None else dv + dv_b
    (dq, dk), (dgq, dgk) = _tile_bwd(_f_dil_pre, [q, k], pre_params, [dqn, dkn], [True, True], [True, True, False], tm,
                                     f'dil_pre_bwd_{l}')
    grads = {'dil_q_norm': dgq.reshape(N_HEADS, HEAD_DIM).sum(0), 'dil_k_norm': dgk.reshape(N_HEADS, HEAD_DIM).sum(0),
             't5_tables': d_tabs}
    return jnp.concatenate([dq, dk, dv], axis=1), grads


S5_LANES = S5_G * S5_P
SCAN_SEGMENTS = 8
SCAN_W = 512


def _f_s5_prep(bre, bim, lr, li, logdt_col, expand):
    dt = jnp.sum(jnp.exp(logdt_col) * expand, axis=0, keepdims=True)
    mag = jnp.exp(lr * dt)
    ar, ai = mag * jnp.cos(li * dt), mag * jnp.sin(li * dt)
    den = lr * lr + li * li
    nr, ni = ar - 1.0, ai
    zr = (nr * lr + ni * li) / den
    zi = (ni * lr - nr * li) / den
    bb = jnp.concatenate([zr * bre - zi * bim, zr * bim + zi * bre], axis=1)
    a_rows = jnp.broadcast_to(jnp.concatenate([ar, ai], axis=1), bb.shape)
    return bb, a_rows


def _s5_scan(x, a_rows, name, reverse=False, h=None):
    S = x.shape[0]
    NL = x.shape[1] // 2
    T = S // SCAN_SEGMENTS
    nblk = NL // SCAN_W
    n_in = 4 if reverse else 2

    def body(*refs):
        if reverse:
            (x_hbm, pr_hbm, pi_hbm, ar_ref, ai_ref, hr_hbm, hi_hbm, dar_ref, dai_ref,
             xr_s, xi_s, pr_s, pi_s, hr_s, hi_s, in_sems, out_sems) = refs
        else:
            x_hbm, ar_ref, ai_ref, hr_hbm, hi_hbm, xr_s, xi_s, hr_s, hi_s, in_sems, out_sems = refs
        col = pl.multiple_of(pl.program_id(0) * SCAN_W, SCAN_W)
        loads = []
        for k in range(SCAN_SEGMENTS):
            rows = pl.ds(k * T, T)
            sources = [(x_hbm, col, xr_s), (x_hbm, NL + col, xi_s)]
            if reverse:
                sources += [(pr_hbm, col, pr_s), (pi_hbm, col, pi_s)]
            for i, (src, c0, dst) in enumerate(sources):
                loads.append(pltpu.make_async_copy(src.at[rows, pl.ds(c0, SCAN_W)], dst.at[:, k, :],
                                                   in_sems.at[i * SCAN_SEGMENTS + k]))
        for cp in loads:
            cp.start()
        for cp in loads:
            cp.wait()
        ar = ar_ref[...]
        ai = -ai_ref[...] if reverse else ai_ref[...]
        zero = jnp.zeros((SCAN_SEGMENTS, SCAN_W), f32)

        def at(s):
            return T - 1 - s if reverse else s

        def local(s, c):
            hr, hi, pr, pi = c
            j = at(s)
            nhr = ar * hr - ai * hi + xr_s[j]
            nhi = ar * hi + ai * hr + xi_s[j]
            hr_s[j] = nhr
            hi_s[j] = nhi
            return nhr, nhi, ar * pr - ai * pi, ar * pi + ai * pr

        er, ei, pr, pi = lax.fori_loop(0, T, local, (zero, zero, zero + 1.0, zero), unroll=4)
        row = lax.broadcasted_iota(jnp.int32, (SCAN_SEGMENTS, SCAN_W), 0)
        cr, ci = zero, zero
        order = range(SCAN_SEGMENTS - 2, -1, -1) if reverse else range(1, SCAN_SEGMENTS)
        for k in order:
            src = k + 1 if reverse else k - 1
            tr = er + pr * cr - pi * ci
            ti = ei + pr * ci + pi * cr
            cr = jnp.where(row == k, jnp.sum(jnp.where(row == src, tr, 0.0), axis=0, keepdims=True), cr)
            ci = jnp.where(row == k, jnp.sum(jnp.where(row == src, ti, 0.0), axis=0, keepdims=True), ci)

        def fix_at(j, c, before):
            pr, pi, sr, si = c
            pr, pi = ar * pr - ai * pi, ar * pi + ai * pr
            hr = hr_s[j] + pr * cr - pi * ci
            hi = hi_s[j] + pr * ci + pi * cr
            hr_s[j] = hr
            hi_s[j] = hi
            if reverse:
                qr, qi = before
                sr = sr + hr * qr + hi * qi
                si = si + hi * qr - hr * qi
            return pr, pi, sr, si

        start = (zero + 1.0, zero, zero, zero)
        if reverse:
            def fix(s, c):
                j = T - 1 - s
                return fix_at(j, c, (pr_s[j - 1], pi_s[j - 1]))

            c = lax.fori_loop(0, T - 1, fix, start, unroll=4)
            last_r = jnp.where(row == 0, 0.0, pltpu.roll(pr_s[T - 1], 1, 0))
            last_i = jnp.where(row == 0, 0.0, pltpu.roll(pi_s[T - 1], 1, 0))
            _, _, sr, si = fix_at(0, c, (last_r, last_i))
            dar_ref[...] = sr
            dai_ref[...] = si
        else:
            lax.fori_loop(0, T, lambda s, c: fix_at(s, c, None), start, unroll=4)
        stores = []
        for k in range(SCAN_SEGMENTS):
            rows = pl.ds(k * T, T)
            stores.append(pltpu.make_async_copy(hr_s.at[:, k, :], hr_hbm.at[rows, pl.ds(col, SCAN_W)], out_sems.at[k]))
            stores.append(pltpu.make_async_copy(hi_s.at[:, k, :], hi_hbm.at[rows, pl.ds(col, SCAN_W)],
                                                out_sems.at[SCAN_SEGMENTS + k]))
        for cp in stores:
            cp.start()
        for cp in stores:
            cp.wait()

    a_re = pl.BlockSpec((SCAN_SEGMENTS, SCAN_W), lambda b: (0, b))
    a_im = pl.BlockSpec((SCAN_SEGMENTS, SCAN_W), lambda b: (0, nblk + b))
    seq = pltpu.VMEM((T, SCAN_SEGMENTS, SCAN_W), f32)
    if reverse:
        in_specs, args = [ANY, ANY, ANY, a_re, a_im], [x, h[0], h[1], a_rows, a_rows]
        out_specs = [ANY, ANY, a_re, a_re]
        out_shape = [jax.ShapeDtypeStruct((S, NL), f32)] * 2 + [jax.ShapeDtypeStruct((SCAN_SEGMENTS, NL), f32)] * 2
    else:
        in_specs, args = [ANY, a_re, a_im], [x, a_rows, a_rows]
        out_specs = [ANY, ANY]
        out_shape = [jax.ShapeDtypeStruct((S, NL), f32)] * 2
    scratch = [seq] * (n_in + 2) + [pltpu.SemaphoreType.DMA((n_in * SCAN_SEGMENTS,)),
                                    pltpu.SemaphoreType.DMA((2 * SCAN_SEGMENTS,))]
    return pl.pallas_call(body, name=name, grid=(nblk,), in_specs=in_specs, out_specs=out_specs, out_shape=out_shape,
                          scratch_shapes=scratch, compiler_params=_cparams(('arbitrary',), big=True))(*args)


def _f_s5_post(y, u, d, w_glu):
    z = _bmm(y + d * u, w_glu)
    return (z[:, :GROUP_W] * jax.nn.sigmoid(z[:, GROUP_W:]),)


def _block_diag(t):
    G, a, b = t.shape
    eye = jnp.eye(G, dtype=t.dtype)
    return (t[:, :, None, :] * eye[:, None, :, None]).reshape(G * a, G * b)


def _diag_blocks(m, a, b):
    G = m.shape[0] // a
    return jnp.moveaxis(jnp.diagonal(m.reshape(G, a, G, b), axis1=0, axis2=2), -1, 0)


def _s5_fwd(u, mp, l):
    S = u.shape[0]
    tm = _pick(S, (256, 128))
    bre = _block_diag(mp['s5_b_re'].transpose(0, 2, 1))
    bim = _block_diag(mp['s5_b_im'].transpose(0, 2, 1))
    expand = jnp.repeat(jnp.eye(S5_G, dtype=f32), S5_P, axis=1)
    prep_params = [mp['s5_lambda_re'].reshape(1, S5_LANES), mp['s5_lambda_im'].reshape(1, S5_LANES),
                   mp['s5_log_dt'].reshape(S5_G, 1), expand]
    bb, a_rows = _tile_fwd(_f_s5_prep, [bre, bim], prep_params, [(2 * S5_LANES, f32)] * 2, GROUP_W, f's5_prep_fwd_{l}')
    x = _mm(u, bb, 'nn', f's5_in_fwd_{l}')
    hr, hi = _s5_scan(x, a_rows, f's5_scan_fwd_{l}')
    c_re, c_im = _block_diag(mp['s5_c_re'].transpose(0, 2, 1)), -_block_diag(mp['s5_c_im'].transpose(0, 2, 1))
    y = _mm(hi, c_im, 'nn', f's5_out_im_fwd_{l}', add=_mm(hr, c_re, 'nn', f's5_out_re_fwd_{l}'))
    post_params = [mp['s5_d'][None], mp['s5_w_glu']]
    (out,) = _tile_fwd(_f_s5_post, [y, u], post_params, [(GROUP_W, f32)], tm, f's5_post_fwd_{l}')
    return out, (u, bre, bim, prep_params, bb, a_rows, hr, hi, c_re, c_im, y, post_params)


def _s5_bwd(dout, saved, l):
    u, bre, bim, prep_params, bb, a_rows, hr, hi, c_re, c_im, y, post_params = saved
    S = u.shape[0]
    tm = _pick(S, (256, 128))
    (dy, du1), (dd, dwglu) = _tile_bwd(_f_s5_post, [y, u], post_params, [dout], [True, True], [True, True], tm,
                                       f's5_post_bwd_{l}')
    ccat = jnp.concatenate([c_re, c_im], axis=0)
    dh = _mm(dy, ccat, 'nt', f's5_out_dx_{l}')
    dccat = jnp.concatenate([_mm(hr, dy, 'tn', f's5_out_re_dw_{l}'), _mm(hi, dy, 'tn', f's5_out_im_dw_{l}')], axis=0)
    lr_, li_, dar, dai = _s5_scan(dh, a_rows, f's5_scan_bwd_{l}', reverse=True, h=(hr, hi))
    du2 = _mm(li_, bb[:, S5_LANES:], 'nt', f's5_in_im_dx_{l}', add=_mm(lr_, bb[:, :S5_LANES], 'nt', f's5_in_re_dx_{l}'))
    dbb = jnp.concatenate([_mm(u, lr_, 'tn', f's5_in_re_dw_{l}'), _mm(u, li_, 'tn', f's5_in_im_dw_{l}')], axis=1)
    da_rows = jnp.pad(jnp.concatenate([dar, dai], axis=1), ((0, GROUP_W - SCAN_SEGMENTS), (0, 0)))
    (dbre, dbim), (dlr, dli, dlogdt) = _tile_bwd(_f_s5_prep, [bre, bim], prep_params, [dbb, da_rows], [True, True],
                                                 [True, True, True, False], GROUP_W, f's5_prep_bwd_{l}')
    grads = {
        's5_lambda_re': dlr.reshape(S5_G, S5_P), 's5_lambda_im': dli.reshape(S5_G, S5_P), 's5_log_dt': dlogdt[:, 0],
        's5_b_re': _diag_blocks(dbre, S5_CG, S5_P).transpose(0, 2, 1),
        's5_b_im': _diag_blocks(dbim, S5_CG, S5_P).transpose(0, 2, 1),
        's5_c_re': _diag_blocks(dccat[:S5_LANES], S5_P, S5_CG).transpose(0, 2, 1),
        's5_c_im': -_diag_blocks(dccat[S5_LANES:], S5_P, S5_CG).transpose(0, 2, 1),
        's5_d': dd[0], 's5_w_glu': dwglu}
    return du1 + du2, grads


DN_CONV = 4


def _head_sum_matrix():
    h = np.arange(GROUP_W) // HEAD_DIM
    return jnp.asarray((h[:, None] == h[None, :]).astype(np.float32))


def _f_dn_pre(x0, x1, x2, x3, ab, w0, w1, w2, w3, alog, dtb, ea, eb, hs):
    c = w0 * x0 + w1 * x1 + w2 * x2 + w3 * x3
    s = c * jax.nn.sigmoid(c)
    q, k, v = s[:, :GROUP_W], s[:, GROUP_W:2 * GROUP_W], s[:, 2 * GROUP_W:]
    q = q * lax.rsqrt(_hdot(q * q, hs) + EPS) * (HEAD_DIM ** -0.5)
    k = k * lax.rsqrt(_hdot(k * k, hs) + EPS)
    beta = jax.nn.sigmoid(_hdot(ab, eb))
    g = -jnp.exp(alog) * jax.nn.softplus(_hdot(ab, ea) + dtb)
    return q, k, v, g, beta


DN_CHUNKS_PER_STEP = 8


def _f_dn_chunks(q, k, v, g, beta):
    C = DN_CHUNK
    n_chunks = q.shape[0] // C
    r = lax.broadcasted_iota(jnp.int32, (C, C), 0)
    c = lax.broadcasted_iota(jnp.int32, (C, C), 1)
    causal, strict = r >= c, r > c
    eye = (r == c).astype(f32)
    tril = causal.astype(f32)
    ones = jnp.ones((C, GROUP_W), f32)
    masks = [_head_mask(h) for h in range(N_HEADS)]
    rows = [tuple(t[i * C:(i + 1) * C] for t in (q, k, v, g, beta)) for i in range(n_chunks)]
    gcs = [_hdot(tril, gi) for (_, _, _, gi, _) in rows]
    items = [(i, h) for i in range(n_chunks) for h in range(N_HEADS)]
    grows = [_hdot_nt(ones * (masks[h] * (1.0 / HEAD_DIM)), gcs[i]) for i, h in items]
    decs = []
    for (i, h), grow in zip(items, grows):
        gcol = jnp.sum(gcs[i] * masks[h], axis=1, keepdims=True) * (1.0 / HEAD_DIM)
        decs.append(jnp.exp(jnp.where(causal, gcol - grow, NEG_INF)))
    kbs = [ki * bi for (_, ki, _, _, bi) in rows]
    kks = [_bmm_nt(kbs[i] * masks[h], rows[i][1]) for i, h in items]
    qks = [_bmm_nt(rows[i][0] * masks[h], rows[i][1]) for i, h in items]
    lmats = [jnp.where(strict, kk * dec, 0.0) for kk, dec in zip(kks, decs)]
    a_qk = [jnp.where(causal, qk * dec, 0.0) for qk, dec in zip(qks, decs)]
    ts = [eye - lm for lm in lmats]
    ps = lmats
    for _ in range(5):
        ps = [_bmm(p, p) for p in ps]
        ts = [t + _bmm(t, p) for t, p in zip(ts, ps)]
    egs = [jnp.exp(gc) for gc in gcs]
    tw = [_bmm(t, kbs[i] * egs[i]) for (i, h), t in zip(items, ts)]
    tu = [_bmm(t, rows[i][2] * rows[i][4]) for (i, h), t in zip(items, ts)]
    outs = []
    for i in range(n_chunks):
        qi, ki, _, gi, _ = rows[i]
        glast = jnp.sum(gi, axis=0, keepdims=True)
        w = sum(tw[i * N_HEADS + h] * masks[h] for h in range(N_HEADS))
        u = sum(tu[i * N_HEADS + h] * masks[h] for h in range(N_HEADS))
        outs.append((w, u, qi * egs[i], ki * jnp.exp(glast - gcs[i]), *a_qk[i * N_HEADS:(i + 1) * N_HEADS],
                     jnp.broadcast_to(jnp.exp(glast), (C, GROUP_W))))
    return tuple(jnp.concatenate(parts, axis=0) for parts in zip(*outs))


def _f_dn_step(w, u, qd, kdec, a0, a1, a2, a3, dfull, state, bd):
    row0 = (lax.broadcasted_iota(jnp.int32, dfull.shape, 0) == 0).astype(f32)
    dvec = jnp.sum(dfull * row0, axis=0, keepdims=True)
    ws, qs = _bmm(w, state), _bmm(qd, state)
    vnew = u - ws
    avs = [_bmm(a, vnew) for a in (a0, a1, a2, a3)]
    kv = _bmm_tn(kdec, vnew)
    o = qs + sum(av * _head_mask(h) for h, av in enumerate(avs))
    return o, state * dvec + bd * kv


def _dn_scan_fwd(ins, name):
    S = ins[0].shape[0]
    N = S // DN_CHUNK
    bd = _head_sum_matrix()

    def body(*refs):
        o_ref, s_ref, state = refs[10], refs[11], refs[12]

        @pl.when(pl.program_id(0) == 0)
        def _():
            state[...] = jnp.zeros_like(state)

        s_in = state[...]
        s_ref[0] = s_in
        o, s_out = _f_dn_step(*[r[...] for r in refs[:9]], s_in, refs[9][...])
        o_ref[...] = o
        state[...] = s_out

    return pl.pallas_call(
        body, name=name, grid=(N,),
        in_specs=[pl.BlockSpec((DN_CHUNK, t.shape[1]), lambda n: (n, 0)) for t in ins] + [_full_spec(bd)],
        out_specs=[pl.BlockSpec((DN_CHUNK, GROUP_W), lambda n: (n, 0)), pl.BlockSpec((1, GROUP_W, GROUP_W), lambda n: (n, 0, 0))],
        out_shape=[jax.ShapeDtypeStruct((S, GROUP_W), f32), jax.ShapeDtypeStruct((N, GROUP_W, GROUP_W), f32)],
        scratch_shapes=[pltpu.VMEM((GROUP_W, GROUP_W), f32)],
        compiler_params=_cparams(('arbitrary',)),
    )(*ins, bd)


def _dn_scan_bwd(ins, states, do, name):
    S = ins[0].shape[0]
    N = S // DN_CHUNK
    bd = _head_sum_matrix()

    def body(*refs):
        s_ref, do_ref = refs[9], refs[10]
        bd_ref = refs[11]
        outs = refs[12:21]
        dstate = refs[21]

        @pl.when(pl.program_id(0) == 0)
        def _():
            dstate[...] = jnp.zeros_like(dstate)

        bd_val = bd_ref[...]
        _, vjp = jax.vjp(lambda *a: _f_dn_step(*a, bd_val), *[r[...] for r in refs[:9]], s_ref[0])
        grads = vjp((do_ref[...], dstate[...]))
        for o, g in zip(outs, grads[:9]):
            o[...] = g
        dstate[...] = grads[9]

    def rev(n):
        return (N - 1 - n, 0)

    res = pl.pallas_call(
        body, name=name, grid=(N,),
        in_specs=[pl.BlockSpec((DN_CHUNK, t.shape[1]), rev) for t in ins] +
                 [pl.BlockSpec((1, GROUP_W, GROUP_W), lambda n: (N - 1 - n, 0, 0)), pl.BlockSpec((DN_CHUNK, GROUP_W), rev),
                  _full_spec(bd)],
        out_specs=[pl.BlockSpec((DN_CHUNK, t.shape[1]), rev) for t in ins],
        out_shape=[jax.ShapeDtypeStruct(t.shape, f32) for t in ins],
        scratch_shapes=[pltpu.VMEM((GROUP_W, GROUP_W), f32)],
        compiler_params=_cparams(('arbitrary',)),
    )(*ins, states, do, bd)
    return list(res)


def _f_dn_post(o, gate, gain, hmean):
    return (o * lax.rsqrt(_hdot(o * o, hmean) + EPS) * gain * (gate * jax.nn.sigmoid(gate)),)


def _dn_delays(x, name):
    S, C = x.shape
    tm = _pick(S, (256, 128))

    def body(prev_ref, cur_ref, *outs):
        before = jnp.where(pl.program_id(0) > 0, prev_ref[...], 0.0)
        both = jnp.concatenate([before, cur_ref[...]], axis=0)
        for o, k in zip(outs, range(DN_CONV - 1, 0, -1)):
            o[...] = pltpu.roll(both, k, 0)[tm:]

    spec = pl.BlockSpec((tm, C), lambda i: (i, 0))
    return pl.pallas_call(
        body, name=name, grid=(S // tm,),
        in_specs=[pl.BlockSpec((tm, C), lambda i: (jnp.maximum(i - 1, 0), 0)), spec],
        out_specs=[spec] * (DN_CONV - 1), out_shape=[jax.ShapeDtypeStruct((S, C), x.dtype)] * (DN_CONV - 1),
        compiler_params=_cparams(('parallel',), big=True),
    )(x, x)


def _dn_undelay_sum(ds, name):
    S, C = ds[0].shape
    tm = _pick(S, (256, 128))
    n = S // tm

    def body(*refs):
        o = refs[-1]
        total = refs[2 * (DN_CONV - 1)][...]
        for j in range(DN_CONV - 1):
            k = DN_CONV - 1 - j
            after = jnp.where(pl.program_id(0) < n - 1, refs[2 * j + 1][...], 0.0)
            both = jnp.concatenate([refs[2 * j][...], after], axis=0)
            total = total + pltpu.roll(both, 2 * tm - k, 0)[:tm]
        o[...] = total

    spec = pl.BlockSpec((tm, C), lambda i: (i, 0))
    nxt = pl.BlockSpec((tm, C), lambda i: (jnp.minimum(i + 1, n - 1), 0))
    args, in_specs = [], []
    for j in range(DN_CONV - 1):
        args += [ds[j], ds[j]]
        in_specs += [spec, nxt]
    return pl.pallas_call(
        body, name=name, grid=(n,), in_specs=in_specs + [spec], out_specs=spec,
        out_shape=jax.ShapeDtypeStruct((S, C), f32), compiler_params=_cparams(('parallel',), big=True),
    )(*args, ds[DN_CONV - 1])


def _dn_fwd(qkv, a, b, gate, mp, l):
    S = qkv.shape[0]
    tm = _pick(S, (256, 128))
    xs = [*_dn_delays(qkv, f'dn_delay_{l}'), qkv]
    ab = jnp.pad(jnp.concatenate([a, b], axis=1), ((0, 0), (0, LANES - 2 * N_HEADS)))
    sel = np.zeros((2, LANES, GROUP_W), np.float32)
    for h in range(N_HEADS):
        sel[0, h, h * HEAD_DIM:(h + 1) * HEAD_DIM] = 1.0
        sel[1, N_HEADS + h, h * HEAD_DIM:(h + 1) * HEAD_DIM] = 1.0
    pre_params = [*[mp['dn_conv'][j][None] for j in range(DN_CONV)], jnp.repeat(mp['dn_a_log'], HEAD_DIM)[None],
                  jnp.repeat(mp['dn_dt_bias'], HEAD_DIM)[None], jnp.asarray(sel[0]), jnp.asarray(sel[1]), _head_sum_matrix()]
    pre = _tile_fwd(_f_dn_pre, [*xs, ab], pre_params, [(GROUP_W, f32)] * 5, tm, f'dn_pre_fwd_{l}')
    chunk_outs = [(GROUP_W, f32)] * 4 + [(HEAD_DIM, f32)] * 4 + [(GROUP_W, f32)]
    parts = _tile_fwd(_f_dn_chunks, pre, [], chunk_outs, DN_CHUNK * DN_CHUNKS_PER_STEP, f'dn_chunk_fwd_{l}')
    o, states = _dn_scan_fwd(parts, f'dn_scan_fwd_{l}')
    post_params = [jnp.tile(mp['dn_o_norm'], N_HEADS)[None], _head_mean_matrix()]
    (y,) = _tile_fwd(_f_dn_post, [o, gate], post_params, [(GROUP_W, f32)], tm, f'dn_post_fwd_{l}')
    return y, (xs, ab, pre_params, pre, parts, states, o, gate, post_params)


def _dn_bwd(dy, saved, l):
    xs, ab, pre_params, pre, parts, states, o, gate, post_params = saved
    S = dy.shape[0]
    tm = _pick(S, (256, 128))
    (do, dgate), (dgain,) = _tile_bwd(_f_dn_post, [o, gate], post_params, [dy], [True, True], [True, False], tm,
                                      f'dn_post_bwd_{l}')
    dparts = _dn_scan_bwd(parts, states, do, f'dn_scan_bwd_{l}')
    dpre, _ = _tile_bwd(_f_dn_chunks, pre, [], dparts, [True] * 5, [], DN_CHUNK * DN_CHUNKS_PER_STEP, f'dn_chunk_bwd_{l}')
    dins, dpar = _tile_bwd(_f_dn_pre, [*xs, ab], pre_params, dpre, [True] * 5, [True] * 6 + [False] * 3, tm,
                           f'dn_pre_bwd_{l}')
    dqkv = _dn_undelay_sum(dins[:DN_CONV], f'dn_undelay_{l}')
    dab = dins[DN_CONV]
    grads = {'dn_conv': jnp.concatenate(dpar[:DN_CONV], axis=0),
             'dn_a_log': dpar[4].reshape(N_HEADS, HEAD_DIM).sum(1), 'dn_dt_bias': dpar[5].reshape(N_HEADS, HEAD_DIM).sum(1),
             'dn_o_norm': dgain.reshape(N_HEADS, HEAD_DIM).sum(0)}
    return dqkv, dab[:, :N_HEADS], dab[:, N_HEADS:2 * N_HEADS], dgate, grads


def _t5_bucket(dist):
    exact = T5_BUCKETS // 2
    df = jnp.maximum(dist, 1).astype(f32)
    large = exact + (jnp.log(df / exact) / math.log(T5_MAX_DIST / exact) * (T5_BUCKETS - exact)).astype(jnp.int32)
    large = jnp.minimum(large, T5_BUCKETS - 1)
    return jnp.where(dist < exact, dist, large)


def _split_cols(t, sizes):
    out, start = [], 0
    for s in sizes:
        out.append(t[..., start:start + s])
        start += s
    return out


def _mixers_fwd(proj, mp, l):
    c_q, c_kv, k_rope, u_s5, qkv_dil, qkv_dn, a_dn, b_dn, gate_dn = _split_cols(proj, IN_SPLITS)
    y_mla, s_mla = _mla_fwd(c_q, c_kv, k_rope, mp, l)
    y_s5, s_s5 = _s5_fwd(u_s5, mp, l)
    y_dil, s_dil = _dil_fwd(qkv_dil, mp, l)
    y_dn, s_dn = _dn_fwd(qkv_dn, a_dn, b_dn, gate_dn, mp, l)
    return jnp.concatenate([y_mla, y_s5, y_dil, y_dn], axis=-1), (s_mla, s_s5, s_dil, s_dn)


def _mixers_bwd(dmixed, saved, l):
    s_mla, s_s5, s_dil, s_dn = saved
    d_mla, d_s5, d_dil, d_dn = _split_cols(dmixed, (GROUP_W,) * 4)
    dc_q, dc_kv, dk_rope, g_mla = _mla_bwd(d_mla, s_mla, l)
    du, g_s5 = _s5_bwd(d_s5, s_s5, l)
    dqkv_dil, g_dil = _dil_bwd(d_dil, s_dil, l)
    dqkv_dn, da, db, dgate, g_dn = _dn_bwd(d_dn, s_dn, l)
    parts = [dc_q, dc_kv, dk_rope, du, dqkv_dil, dqkv_dn, da, db, dgate]
    dproj = jnp.concatenate([p.astype(bf16) for p in parts], axis=-1)
    return dproj, {**g_mla, **g_s5, **g_dil, **g_dn}


MIXER_PARAMS = ['mla_q_norm', 'mla_kv_norm', 'mla_w_uq', 'mla_w_ukv', 'mla_qk_q', 'mla_qk_k', 's5_lambda_re',
                's5_lambda_im', 's5_log_dt', 's5_b_re', 's5_b_im', 's5_c_re', 's5_c_im', 's5_d', 's5_w_glu',
                'dil_q_norm', 'dil_k_norm', 't5_bias', 'dn_conv', 'dn_a_log', 'dn_dt_bias', 'dn_o_norm']


def _layer_fwd_mix(h, W, l):
    S = h.shape[0]
    tm = _pick(S, (256, 128))
    g1 = W['attn_norm'][l][None]
    (n1,) = _tile_fwd(_f_rms, [h], [g1], [(D_MODEL, bf16)], tm, f'rms1_fwd_{l}')
    proj = _mm(n1, W['w_in'][l], 'nt', f'proj_fwd_{l}')
    mp = {k: (W[k] if k == 't5_bias' else W[k][l]).astype(f32) for k in MIXER_PARAMS}
    if 'dil_tables' in W:
        mp['dil_tables'] = W['dil_tables']
    mixed, mix_saved = _mixers_fwd(proj, mp, l)
    mixed_b = mixed.astype(bf16)
    h2 = _mm(mixed_b, W['w_out'][l], 'nn', f'out_fwd_{l}', add=h)
    return h2, dict(h=h, n1=n1, mix=mix_saved, mixed=mixed_b, h2=h2)


def _layer_fwd_ffn(h2, W, l, saved):
    S = h2.shape[0]
    tm = _pick(S, (256, 128))
    g2 = W['ffn_norm'][l][None]
    (n2,) = _tile_fwd(_f_rms, [h2], [g2], [(D_MODEL, bf16)], tm, f'rms2_fwd_{l}')
    u = _mm(n2, W['ffn_w1'][l], 'nt', f'ffn1_fwd_{l}', out_dtype=bf16)
    v = _mm(n2, W['ffn_w3'][l], 'nt', f'ffn3_fwd_{l}', out_dtype=bf16)
    (act,) = _tile_fwd(_f_swiglu, [u, v], [], [(FFN_HIDDEN, bf16)], tm, f'swiglu_fwd_{l}')
    h3 = _mm(act, W['ffn_w2'][l], 'nn', f'ffn2_fwd_{l}', add=h2)
    saved.update(n2=n2, u=u, v=v, act=act)
    return h3


def _layer_bwd_ffn(dh3, saved, W, l):
    S = dh3.shape[0]
    tm = _pick(S, (256, 128))
    g2 = W['ffn_norm'][l][None]
    grads = {}
    dact = _mm(dh3, W['ffn_w2'][l], 'nt', f'ffn2_dx_{l}', out_dtype=bf16)
    grads['ffn_w2'] = _mm(saved['act'], dh3, 'tn', f'ffn2_dw_{l}', out_dtype=bf16)
    (du, dv), _ = _tile_bwd(_f_swiglu, [saved['u'], saved['v']], [], [dact], [True, True], [], tm, f'swiglu_bwd_{l}',
                            dt_dtypes=[bf16, bf16])
    dn2 = _mm(dv, W['ffn_w3'][l], 'nn', f'ffn3_dx_{l}', add=_mm(du, W['ffn_w1'][l], 'nn', f'ffn1_dx_{l}'))
    grads['ffn_w1'] = _mm(du, saved['n2'], 'tn', f'ffn1_dw_{l}', out_dtype=bf16)
    grads['ffn_w3'] = _mm(dv, saved['n2'], 'tn', f'ffn3_dw_{l}', out_dtype=bf16)
    (dh2n,), (dg2,) = _tile_bwd(_f_rms, [saved['h2']], [g2], [dn2], [True], [True], tm, f'rms2_bwd_{l}')
    grads['ffn_norm'] = dg2[0]
    return (dh3, dh2n), grads


def _layer_bwd_mix(dh2, saved, W, l):
    S = dh2.shape[0]
    tm = _pick(S, (256, 128))
    g1 = W['attn_norm'][l][None]
    grads = {}
    dmixed = _mm(dh2, W['w_out'][l], 'nt', f'out_dx_{l}')
    grads['w_out'] = _mm(saved['mixed'], dh2, 'tn', f'out_dw_{l}', out_dtype=bf16)
    dproj, dmp = _mixers_bwd(dmixed, saved['mix'], l)
    grads.update(dmp)
    dn1 = _mm(dproj, W['w_in'][l], 'nn', f'proj_dx_{l}')
    grads['w_in'] = _mm(dproj, saved['n1'], 'tn', f'proj_dw_{l}', out_dtype=bf16)
    (dh1n,), (dg1,) = _tile_bwd(_f_rms, [saved['h']], [g1], [dn1], [True], [True], tm, f'rms1_bwd_{l}')
    grads['attn_norm'] = dg1[0]
    return (dh2, dh1n), grads


def kernel(x, attn_norm, w_in, w_out, mla_q_norm, mla_kv_norm, mla_w_uq, mla_w_ukv, mla_qk_q, mla_qk_k, s5_lambda_re, s5_lambda_im, s5_log_dt, s5_b_re, s5_b_im, s5_c_re, s5_c_im, s5_d, s5_w_glu, dil_q_norm, dil_k_norm, t5_bias, dn_conv, dn_a_log, dn_dt_bias, dn_o_norm, ffn_norm, ffn_w1, ffn_w3, ffn_w2, loss_target, m_attn_norm, m_w_in, m_w_out, m_mla_q_norm, m_mla_kv_norm, m_mla_w_uq, m_mla_w_ukv, m_mla_qk_q, m_mla_qk_k, m_s5_lambda_re, m_s5_lambda_im, m_s5_log_dt, m_s5_b_re, m_s5_b_im, m_s5_c_re, m_s5_c_im, m_s5_d, m_s5_w_glu, m_dil_q_norm, m_dil_k_norm, m_t5_bias, m_dn_conv, m_dn_a_log, m_dn_dt_bias, m_dn_o_norm, m_ffn_norm, m_ffn_w1, m_ffn_w3, m_ffn_w2, v_attn_norm, v_w_in, v_w_out, v_mla_q_norm, v_mla_kv_norm, v_mla_w_uq, v_mla_w_ukv, v_mla_qk_q, v_mla_qk_k, v_s5_lambda_re, v_s5_lambda_im, v_s5_log_dt, v_s5_b_re, v_s5_b_im, v_s5_c_re, v_s5_c_im, v_s5_d, v_s5_w_glu, v_dil_q_norm, v_dil_k_norm, v_t5_bias, v_dn_conv, v_dn_a_log, v_dn_dt_bias, v_dn_o_norm, v_ffn_norm, v_ffn_w1, v_ffn_w3, v_ffn_w2):
    given = dict(locals())
    def seen(n, t):
        if n in COLUMNS_FIRST:
            return jnp.transpose(t, (2, 0, 1))
        return jnp.swapaxes(t, 1, 2) if n in TRANSPOSED else t

    def given_back(n, t):
        return jnp.transpose(t, (1, 2, 0)) if n in COLUMNS_FIRST else seen(n, t)

    def layer_of(n, t, l):
        return t[:, l] if n in COLUMNS_FIRST else t[l]

    w_loc = {n: seen(n, given[n]) for n in WEIGHTS}
    m_loc = {n: seen(n, given['m_' + n]) for n in WEIGHTS}
    v_loc = {n: seen(n, given['v_' + n]) for n in WEIGHTS}
    big_names = list(BIG)

    own = 2 * lax.axis_index('x') + lax.axis_index('y')
    groups = [[(n, 0) for n in GATHER_FIRST], [(n, 0) for n in GATHER_FFN], [(n, 1) for n in big_names]]
    started, order = [], jnp.zeros((8, LANES), f32)
    for gi, group in enumerate(groups):
        blocks = [layer_of(n, w_loc[n], l).astype(bf16) for n, l in group]
        lands = [lax.empty((N_SHARDS,) + b.shape, bf16) for b in blocks]
        send_sems, recv_sems, blocks, lands, order = _to_chips_start(blocks, lands, False, order, f'gather_start_{gi}')
        started.append((send_sems, recv_sems, blocks, lands))
    W = {n: [None] * DEPTH for n in big_names}
    for n in SMALL:
        W[n] = w_loc[n]
    W['dil_tables'] = _dil_tables(w_loc['t5_bias'])

    def arrive(gi, after):
        send_sems, recv_sems, blocks, lands = started[gi]
        blocks, lands = _to_chips_wait(send_sems, recv_sems, blocks, lands, False, after, f'gather_wait_{gi}')
        for (n, l), block, land in zip(groups[gi], blocks, lands):
            W[n][l] = _from_shards(n, lax.dynamic_update_slice(land, block[None], (own, 0, 0)))

    arrive(0, order)
    h = x[0]
    saved = []
    for l in range(DEPTH):
        h2, sv = _layer_fwd_mix(h, W, l)
        if l == 0:
            arrive(1, h2)
        h = _layer_fwd_ffn(h2, W, l, sv)
        if l == 0:
            arrive(2, h)
        saved.append(sv)
    parts_loss, dh = _loss_head(h, loss_target[0])
    local_loss = jnp.sum(parts_loss)

    layer_grads = [dict() for _ in range(DEPTH)]
    sent = []

    def send(group, tag):
        srcs = [_by_shard(n, layer_grads[l][n]).astype(bf16) for n, l in group]
        lands = [lax.empty((3,) + s.shape[1:], bf16) for s in srcs]
        send_sems, recv_sems, srcs, lands, token = _to_chips_start(srcs, lands, True, jnp.zeros((8, LANES), f32),
                                                                   f'reduce_start_{tag}')
        sent.append((group, tag, send_sems, recv_sems, srcs, lands))
        return token[0, 0]

    for l in reversed(range(DEPTH)):
        (dh3, dh2n), g_ffn = _layer_bwd_ffn(dh, saved[l], W, l)
        layer_grads[l].update(g_ffn)
        dh2 = dh3 + dh2n
        if l == 0:
            dh2 = dh2 + send([(n, 0) for n in GATHER_FFN], 'ffn0')
        (dh2, dh1n), g_mix = _layer_bwd_mix(dh2, saved[l], W, l)
        layer_grads[l].update(g_mix)
        dh = dh2 + dh1n
        if l == 1:
            dh = dh + send([(n, 1) for n in big_names], 'layer1')
    last = send([(n, 0) for n in GATHER_FIRST], 'first0')
    grad_x = dh[None]
    small_full = []
    for n in SMALL:
        if n == 't5_bias':
            small_full.append(_t5_grad([a_ + b_ for a_, b_ in zip(layer_grads[0]['t5_tables'], layer_grads[1]['t5_tables'])]))
        else:
            small_full.append(jnp.stack([layer_grads[l][n] for l in range(DEPTH)]))

    small_shapes = [w_loc[n].shape for n in SMALL] + [(1,)]
    nothing = [jnp.zeros((1,), f32)]
    small_pack = _pack(small_full + [local_loss.reshape(1)]) + last
    _, recv_small = _swap_with_sibling([], small_pack)
    chip_small = _small_chip_sum(small_pack, recv_small)
    _, from_chips_small = _exchange_between_chips([], chip_small)

    mine = {}
    for group, tag, send_sems, recv_sems, srcs, lands in sent:
        srcs, lands = _to_chips_wait(send_sems, recv_sems, srcs, lands, True, from_chips_small, f'reduce_wait_{tag}')
        for (n, l), src, land in zip(group, srcs, lands):
            mine[(n, l)] = _partial_sum(src, land, f'partial_{n}_{l}')
    keys = [(n, l) for n in big_names for l in range(DEPTH)]
    theirs = dict(zip(keys, _swap_partials([mine[k] for k in keys])))

    g_small_p, d_small_p, m_small_p, v_small_p = _small_update(
        small_pack, recv_small, from_chips_small, _pack([w_loc[n] for n in SMALL] + nothing),
        _pack([m_loc[n] for n in SMALL] + nothing), _pack([v_loc[n] for n in SMALL] + nothing))
    loss = _unpack(g_small_p, small_shapes)[-1][0]
    grad, delta, new_m, new_v = {}, {}, {}, {}
    for n, g_, d_, m_, v_ in zip(SMALL, _unpack(g_small_p, small_shapes), _unpack(d_small_p, small_shapes),
                                 _unpack(m_small_p, small_shapes), _unpack(v_small_p, small_shapes)):
        grad[n], delta[n], new_m[n], new_v[n] = g_, d_, m_, v_
    for n in big_names:
        update = _adamw_layer_in_the_middle if n in COLUMNS_FIRST else _adamw
        results = update(w_loc[n], m_loc[n], v_loc[n], [mine[(n, l)] for l in range(DEPTH)],
                         [theirs[(n, l)] for l in range(DEPTH)], 'adamw_' + n)
        grad[n], delta[n], new_m[n], new_v[n] = [given_back(n, t) for t in results]
    return (loss, grad_x, *[grad[n] for n in WEIGHTS], *[delta[n] for n in WEIGHTS],
            *[new_m[n] for n in WEIGHTS], *[new_v[n] for n in WEIGHTS])
```

```python
import functools
import math

import numpy as np
import jax
import jax.numpy as jnp
from jax import lax
from jax.experimental import pallas as pl
from jax.experimental.pallas import tpu as pltpu

f32 = jnp.float32
bf16 = jnp.bfloat16
HI = lax.Precision.HIGHEST
MESH = pl.DeviceIdType.MESH

VMEM_LIMIT_BYTES = 48 * 1024 * 1024
MM_VMEM_BUDGET_BYTES = 32 * 1024 * 1024
LANES = 128

D_MODEL = 1024
DEPTH = 2
GROUP_W = 256
HEAD_DIM = 64
EPS = 1e-6
NEG_INF = -1e30
N_HEADS = 4
MLA_NOPE, MLA_ROPE = 64, 32
MLA_DQK = MLA_NOPE + MLA_ROPE
ROPE_THETA = 10000.0
Q_BLOCK = 128
S5_G, S5_CG, S5_P = 16, 16, 64
DIL_PAIRS = ((128, 1), (512, 4), (2048, 16))
T5_BUCKETS, T5_MAX_DIST = 32, 2048
DN_CHUNK = 64
FFN_HIDDEN = 2816
IN_SPLITS = (256, 128, 32, 256, 768, 768, 4, 4, 256)
IN_COLS = sum(IN_SPLITS)

ADAM_LR, ADAM_B1, ADAM_B2, ADAM_EPS, ADAM_WD, ADAM_STEP = 0.001, 0.9, 0.999, 1e-08, 0.01, 10

WEIGHTS = ['attn_norm', 'w_in', 'w_out', 'mla_q_norm', 'mla_kv_norm', 'mla_w_uq', 'mla_w_ukv', 'mla_qk_q', 'mla_qk_k',
           's5_lambda_re', 's5_lambda_im', 's5_log_dt', 's5_b_re', 's5_b_im', 's5_c_re', 's5_c_im', 's5_d', 's5_w_glu',
           'dil_q_norm', 'dil_k_norm', 't5_bias', 'dn_conv', 'dn_a_log', 'dn_dt_bias', 'dn_o_norm', 'ffn_norm',
           'ffn_w1', 'ffn_w3', 'ffn_w2']
BIG = {'w_in': 1, 'w_out': 1, 'mla_w_uq': 2, 'mla_w_ukv': 2, 's5_w_glu': 2, 'dn_conv': 2, 'ffn_w1': 1, 'ffn_w3': 1,
       'ffn_w2': 1}
TRANSPOSED = ('ffn_w1', 'ffn_w3')
COLUMNS_FIRST = ('w_in',)
SMALL = [n for n in WEIGHTS if n not in BIG]
GATHER_FIRST = ['w_in', 'mla_w_uq', 'mla_w_ukv', 's5_w_glu', 'dn_conv', 'w_out']
GATHER_FFN = ['ffn_w1', 'ffn_w3', 'ffn_w2']
N_SHARDS = 4
PACK_COLS = 1024


def _cparams(sem=None, big=False):
    kw = {}
    if sem is not None:
        kw['dimension_semantics'] = sem
    if big:
        kw['vmem_limit_bytes'] = VMEM_LIMIT_BYTES
    return pltpu.CompilerParams(**kw)


def _pick(n, prefs):
    for p in prefs:
        if p <= n and n % p == 0:
            return p
    return n


def _lane_tile(n, cap):
    for t in range(cap - cap % LANES, 0, -LANES):
        if n % t == 0:
            return t
    return n


def _mm(a, b, mode, name, add=None, out_dtype=f32):
    if mode == 'nn':
        (M, K), (K2, N) = a.shape, b.shape
    elif mode == 'nt':
        (M, K), (N, K2) = a.shape, b.shape
    else:
        (K, M), (K2, N) = a.shape, b.shape
    assert K == K2, (name, a.shape, b.shape)
    tk = K if K <= 2816 else _pick(K, (2816, 2048, 1408, 1024, 512))
    cap_m, cap_n = (1408 if mode == 'tn' else 512), 1408

    def need(tm_, tn_):
        per_step = tm_ * tk * a.dtype.itemsize + tk * tn_ * b.dtype.itemsize + tm_ * tn_ * jnp.dtype(out_dtype).itemsize
        if add is not None:
            per_step += tm_ * tn_ * add.dtype.itemsize
        return 2 * per_step + tm_ * tn_ * 4

    tm, tn = _lane_tile(M, cap_m), _lane_tile(N, cap_n)
    while need(tm, tn) > MM_VMEM_BUDGET_BYTES and cap_m > LANES:
        cap_m //= 2
        tm = _lane_tile(M, cap_m)
    while need(tm, tn) > MM_VMEM_BUDGET_BYTES and cap_n > LANES:
        cap_n //= 2
        tn = _lane_tile(N, cap_n)
    nk = K // tk
    dims = {'nn': (((1,), (0,)), ((), ())), 'nt': (((1,), (1,)), ((), ())), 'tn': (((0,), (0,)), ((), ()))}[mode]
    has_add = add is not None

    def body(*refs):
        a_ref, b_ref = refs[0], refs[1]
        add_ref = refs[2] if has_add else None
        o_ref = refs[3] if has_add else refs[2]
        part = lax.dot_general(a_ref[...].astype(bf16), b_ref[...].astype(bf16), dims, preferred_element_type=f32)
        if nk == 1:
            if has_add:
                part = part + add_ref[...].astype(f32)
            o_ref[...] = part.astype(out_dtype)
        else:
            acc_ref = refs[-1]
            k = pl.program_id(2)

            @pl.when(k == 0)
            def _():
                acc_ref[...] = part

            @pl.when(k > 0)
            def _():
                acc_ref[...] += part

            @pl.when(k == nk - 1)
            def _():
                r = acc_ref[...]
                if has_add:
                    r = r + add_ref[...].astype(f32)
                o_ref[...] = r.astype(out_dtype)

    if mode == 'nn':
        a_spec = pl.BlockSpec((tm, tk), lambda i, j, k: (i, k))
        b_spec = pl.BlockSpec((tk, tn), lambda i, j, k: (k, j))
    elif mode == 'nt':
        a_spec = pl.BlockSpec((tm, tk), lambda i, j, k: (i, k))
        b_spec = pl.BlockSpec((tn, tk), lambda i, j, k: (j, k))
    else:
        a_spec = pl.BlockSpec((tk, tm), lambda i, j, k: (k, i))
        b_spec = pl.BlockSpec((tk, tn), lambda i, j, k: (k, j))
    in_specs = [a_spec, b_spec]
    args = [a, b]
    if has_add:
        in_specs.append(pl.BlockSpec((tm, tn), lambda i, j, k: (i, j)))
        args.append(add)
    return pl.pallas_call(
        body, name=name, grid=(M // tm, N // tn, nk), in_specs=in_specs,
        out_specs=pl.BlockSpec((tm, tn), lambda i, j, k: (i, j)),
        out_shape=jax.ShapeDtypeStruct((M, N), out_dtype),
        scratch_shapes=[pltpu.VMEM((tm, tn), f32)] if nk > 1 else [],
        compiler_params=_cparams(('parallel', 'parallel', 'arbitrary'), big=True),
    )(*args)


def _full_spec(p):
    nd = p.ndim
    return pl.BlockSpec(p.shape, lambda i, _nd=nd: (0,) * _nd)


def _tile_fwd(f, tiled, params, outs, tm, name):
    S = tiled[0].shape[0]
    nt, npar = len(tiled), len(params)

    def body(*refs):
        vals = [r[...].astype(f32) for r in refs[:nt + npar]]
        res = f(*vals)
        for r, o in zip(res, refs[nt + npar:]):
            o[...] = r.astype(o.dtype)

    return pl.pallas_call(
        body, name=name, grid=(S // tm,),
        in_specs=[pl.BlockSpec((tm, t.shape[1]), lambda i: (i, 0)) for t in tiled] + [_full_spec(p) for p in params],
        out_specs=[pl.BlockSpec((tm, c), lambda i: (i, 0)) for c, _ in outs],
        out_shape=[jax.ShapeDtypeStruct((S, c), dt) for c, dt in outs],
        compiler_params=_cparams(('parallel',), big=True),
    )(*tiled, *params)


def _tile_bwd(f, tiled, params, cts, diff_t, diff_p, tm, name, dt_dtypes=None):
    S = tiled[0].shape[0]
    nt, npar, nc = len(tiled), len(params), len(cts)
    it = [i for i in range(nt) if diff_t[i]]
    ip = [i for i in range(npar) if diff_p[i]]
    if dt_dtypes is None:
        dt_dtypes = [f32] * len(it)

    def body(*refs):
        vals = [r[...].astype(f32) for r in refs[:nt + npar]]
        ct_vals = tuple(r[...].astype(f32) for r in refs[nt + npar:nt + npar + nc])
        out_refs = refs[nt + npar + nc:]

        def g(*dv):
            full = list(vals)
            for k, i in enumerate(it):
                full[i] = dv[k]
            for k, i in enumerate(ip):
                full[nt + i] = dv[len(it) + k]
            return tuple(f(*full))

        _, vjp = jax.vjp(g, *[vals[i] for i in it], *[vals[nt + i] for i in ip])
        grads = vjp(ct_vals)
        for k in range(len(it)):
            out_refs[k][...] = grads[k].astype(out_refs[k].dtype)
        step = pl.program_id(0)
        for k in range(len(ip)):
            o = out_refs[len(it) + k]
            gk = grads[len(it) + k]

            @pl.when(step == 0)
            def _(o=o, gk=gk):
                o[...] = gk

            @pl.when(step > 0)
            def _(o=o, gk=gk):
                o[...] += gk

    out_specs = [pl.BlockSpec((tm, tiled[i].shape[1]), lambda i_: (i_, 0)) for i in it] + [_full_spec(params[i]) for i in ip]
    out_shape = [jax.ShapeDtypeStruct(tiled[i].shape, dt_dtypes[k]) for k, i in enumerate(it)] + \
                [jax.ShapeDtypeStruct(params[i].shape, f32) for i in ip]
    res = pl.pallas_call(
        body, name=name, grid=(S // tm,),
        in_specs=[pl.BlockSpec((tm, t.shape[1]), lambda i: (i, 0)) for t in tiled] + [_full_spec(p) for p in params] +
                 [pl.BlockSpec((tm, c.shape[1]), lambda i: (i, 0)) for c in cts],
        out_specs=out_specs, out_shape=out_shape,
        compiler_params=_cparams(('arbitrary',), big=True),
    )(*tiled, *params, *cts)
    return list(res[:len(it)]), list(res[len(it):])


def _rms(x, g):
    return x * lax.rsqrt(jnp.mean(x * x, axis=-1, keepdims=True) + EPS) * g


def _f_rms(x, g):
    return (_rms(x, g),)


def _f_swiglu(u, v):
    return (u * jax.nn.sigmoid(u) * v,)


def _loss_head(y, target):
    S, D = y.shape
    tm = _pick(S, (256, 128))

    def body(y_ref, t_ref, part_ref, dy_ref):
        e = y_ref[...] - t_ref[...]
        dy_ref[...] = e * (1.0 / D)
        s = 0.5 * jnp.sum(jnp.sum(e * e, axis=1, keepdims=True), axis=0, keepdims=True) * (1.0 / D)
        r = lax.broadcasted_iota(jnp.int32, (8, LANES), 0)
        c = lax.broadcasted_iota(jnp.int32, (8, LANES), 1)
        part_ref[0] = jnp.where((r == 0) & (c == 0), s, 0.0)

    return pl.pallas_call(
        body, name='loss_head', grid=(S // tm,),
        in_specs=[pl.BlockSpec((tm, D), lambda i: (i, 0))] * 2,
        out_specs=[pl.BlockSpec((1, 8, LANES), lambda i: (i, 0, 0)), pl.BlockSpec((tm, D), lambda i: (i, 0))],
        out_shape=[jax.ShapeDtypeStruct((S // tm, 8, LANES), f32), jax.ShapeDtypeStruct((S, D), f32)],
        compiler_params=_cparams(('parallel',)),
    )(y, target)


def _pack_rows_of(shape):
    rows = -(-math.prod(shape) // PACK_COLS)
    return -(-rows // 8) * 8


def _pack(arrs):
    parts = []
    for a in arrs:
        rows = _pack_rows_of(a.shape)
        flat = a.astype(f32).reshape(-1)
        parts.append(jnp.pad(flat, (0, rows * PACK_COLS - flat.shape[0])).reshape(rows, PACK_COLS))
    return jnp.concatenate(parts, axis=0)


def _unpack(pack, shapes):
    out, row = [], 0
    for s in shapes:
        rows = _pack_rows_of(s)
        out.append(pack[row:row + rows].reshape(-1)[:math.prod(s)].reshape(s))
        row += rows
    return out


ANY = pl.BlockSpec(memory_space=pl.ANY)


def _place():
    return lax.axis_index('x'), lax.axis_index('y'), lax.axis_index('c')


def _where():
    return jnp.stack([lax.axis_index('c'), 2 * lax.axis_index('x') + lax.axis_index('y')]).astype(jnp.int32)


def _remote(src, dst, send_sems, recv_sems, k, to):
    return pltpu.make_async_remote_copy(src_ref=src, dst_ref=dst, send_sem=send_sems.at[k], recv_sem=recv_sems.at[k],
                                        device_id=to, device_id_type=MESH)


def _swap_with_sibling(gs, small):
    n = len(gs)

    def body(*refs):
        g_refs, s_ref = refs[:n], refs[n]
        r_refs, rs_ref = refs[n + 1:2 * n + 1], refs[2 * n + 1]
        send_sems, recv_sems = refs[2 * n + 2:]
        x, y, c = _place()
        sib = (x, y, 1 - c)
        cps = [_remote(g_refs[t].at[:, 1 - c], r_refs[t], send_sems, recv_sems, t, sib) for t in range(n)]
        cps.append(_remote(s_ref, rs_ref, send_sems, recv_sems, n, sib))
        for cp in cps:
            cp.start()
        for cp in cps:
            cp.wait()

    res = pl.pallas_call(
        body, name='swap_with_sibling', in_specs=[ANY] * (n + 1), out_specs=[ANY] * (n + 1),
        out_shape=[jax.ShapeDtypeStruct((N_SHARDS,) + g.shape[2:], g.dtype) for g in gs] +
                  [jax.ShapeDtypeStruct(small.shape, small.dtype)],
        scratch_shapes=[pltpu.SemaphoreType.DMA((n + 1,)), pltpu.SemaphoreType.DMA((n + 1,))],
    )(*gs, small)
    return list(res[:n]), res[n]


def _exchange_between_chips(cs, small):
    n = len(cs)

    def body(*refs):
        c_refs, s_ref = refs[:n], refs[n]
        r_refs, rs_ref = refs[n + 1:2 * n + 1], refs[2 * n + 1]
        send_sems, recv_sems = refs[2 * n + 2:]
        x, y, c = _place()
        chips = [(1 - x, y), (x, 1 - y), (1 - x, 1 - y)]
        cps = []
        for j, (px, py) in enumerate(chips):
            for t in range(n):
                cps.append(_remote(c_refs[t].at[2 * px + py], r_refs[t].at[j], send_sems, recv_sems, 3 * t + j, (px, py, c)))
            cps.append(_remote(s_ref, rs_ref.at[j], send_sems, recv_sems, 3 * n + j, (px, py, c)))
        for cp in cps:
            cp.start()
        for cp in cps:
            cp.wait()

    res = pl.pallas_call(
        body, name='exchange_between_chips', in_specs=[ANY] * (n + 1), out_specs=[ANY] * (n + 1),
        out_shape=[jax.ShapeDtypeStruct((3,) + c.shape[1:], c.dtype) for c in cs] +
                  [jax.ShapeDtypeStruct((3,) + small.shape, small.dtype)],
        scratch_shapes=[pltpu.SemaphoreType.DMA((3 * n + 3,)), pltpu.SemaphoreType.DMA((3 * n + 3,))],
    )(*cs, small)
    return list(res[:n]), res[n]


def _swap_partials(ts):
    n = len(ts)

    def body(*refs):
        t_refs, o_refs = refs[:n], refs[n:2 * n]
        send_sems, recv_sems = refs[2 * n:]
        x, y, c = _place()
        cps = [_remote(t_refs[t], o_refs[t], send_sems, recv_sems, t, (x, y, 1 - c)) for t in range(n)]
        for cp in cps:
            cp.start()
        for cp in cps:
            cp.wait()

    return pl.pallas_call(
        body, name='swap_partials', in_specs=[ANY] * n, out_specs=[ANY] * n,
        out_shape=[jax.ShapeDtypeStruct(t.shape, t.dtype) for t in ts],
        scratch_shapes=[pltpu.SemaphoreType.DMA((n,)), pltpu.SemaphoreType.DMA((n,))],
    )(*ts)


HBM = pl.BlockSpec(memory_space=pltpu.HBM)
SEM = pl.BlockSpec(memory_space=pltpu.SEMAPHORE)
DATAFLOW = pltpu.SideEffectType.DATAFLOW_SIDE_EFFECTING


def _in_hbm(t):
    return pltpu.with_memory_space_constraint(t, pltpu.HBM)


def _other_chips():
    x, y, c = _place()
    return [(1 - x, y, c), (x, 1 - y, c), (1 - x, 1 - y, c)]


def _to_chips_copies(src_refs, land_refs, send_sems, recv_sems, per_peer):
    x, y, _ = _place()
    cps = []
    for t, (src, land) in enumerate(zip(src_refs, land_refs)):
        for j, (px, py, pc) in enumerate(_other_chips()):
            s = src.at[2 * px + py] if per_peer else src
            d = land.at[j] if per_peer else land.at[2 * x + y]
            cps.append(_remote(s, d, send_sems, recv_sems, 3 * t + j, (px, py, pc)))
    return cps


def _to_chips_start(srcs, lands, per_peer, order, name):
    n = len(srcs)

    def body(*refs):
        src_refs, land_refs = refs[:n], refs[n:2 * n]
        send_sems, recv_sems = refs[2 * n + 1], refs[2 * n + 2]
        token = refs[-1]
        for cp in _to_chips_copies(src_refs, land_refs, send_sems, recv_sems, per_peer):
            cp.start()
        token[...] = jnp.zeros_like(token)

    res = pl.pallas_call(
        body, name=name, in_specs=[HBM] * (2 * n) + [ANY],
        out_specs=[SEM, SEM] + [HBM] * (2 * n) + [pl.BlockSpec(memory_space=pltpu.VMEM)],
        out_shape=[pltpu.SemaphoreType.DMA((3 * n,)), pltpu.SemaphoreType.DMA((3 * n,))] +
                  [pltpu.HBM(t.shape, t.dtype) for t in srcs] + [pltpu.HBM(t.shape, t.dtype) for t in lands] +
                  [jax.ShapeDtypeStruct((8, LANES), f32)],
        input_output_aliases={i: 2 + i for i in range(2 * n)},
        compiler_params=pltpu.CompilerParams(has_side_effects=DATAFLOW),
    )(*[_in_hbm(t) for t in srcs], *[_in_hbm(t) for t in lands], order)
    return res[0], res[1], list(res[2:2 + n]), list(res[2 + n:2 + 2 * n]), res[-1]


def _to_chips_wait(send_sems, recv_sems, srcs, lands, per_peer, after, name):
    n = len(srcs)

    def body(*refs):
        src_refs, land_refs = refs[:n], refs[n:2 * n]
        send_ref, recv_ref = refs[2 * n], refs[2 * n + 1]
        for cp in _to_chips_copies(src_refs, land_refs, send_ref, recv_ref, per_peer):
            cp.wait_send()
            cp.wait_recv()

    res = pl.pallas_call(
        body, name=name, in_specs=[HBM] * (2 * n) + [SEM, SEM, ANY],
        out_specs=[HBM] * (2 * n),
        out_shape=[pltpu.HBM(t.shape, t.dtype) for t in srcs] + [pltpu.HBM(t.shape, t.dtype) for t in lands],
        input_output_aliases={i: i for i in range(2 * n)},
        compiler_params=pltpu.CompilerParams(has_side_effects=DATAFLOW),
    )(*srcs, *lands, send_sems, recv_sems, after)
    return list(res[:n]), list(res[n:])


def _row_tile(a):
    return _pick(a, (512, 256, 128, 64, 32, 16, 8))


def _partial_sum(g, land, name):
    _, a, b = g.shape
    tr = _row_tile(a)

    def body(w_ref, g_ref, r_ref, o_ref):
        t = g_ref[0].astype(f32) + r_ref[0].astype(f32)
        t = t + r_ref[1].astype(f32)
        t = t + r_ref[2].astype(f32)
        o_ref[...] = t.astype(o_ref.dtype)

    return pl.pallas_call(
        body, name=name,
        grid_spec=pltpu.PrefetchScalarGridSpec(
            num_scalar_prefetch=1, grid=(a // tr,),
            in_specs=[pl.BlockSpec((1, tr, b), lambda i, w: (w[1], i, 0)), pl.BlockSpec((3, tr, b), lambda i, w: (0, i, 0))],
            out_specs=pl.BlockSpec((tr, b), lambda i, w: (i, 0))),
        out_shape=jax.ShapeDtypeStruct((a, b), bf16),
        compiler_params=_cparams(('parallel',)),
    )(_where(), g, land)


def _by_shard(name, t):
    r, c = t.shape
    if BIG[name] == 2:
        return t.reshape(r, N_SHARDS, c // N_SHARDS).transpose(1, 0, 2)
    return t.reshape(N_SHARDS, r // N_SHARDS, c)


def _from_shards(name, g):
    s, a, b = g.shape
    if BIG[name] == 2:
        return g.transpose(1, 0, 2).reshape(a, s * b)
    return g.reshape(s * a, b)


def _adam_math(w, g, m, v):
    m = ADAM_B1 * m + (1.0 - ADAM_B1) * g
    v = ADAM_B2 * v + (1.0 - ADAM_B2) * (g * g)
    m_hat = m / (1.0 - ADAM_B1 ** ADAM_STEP)
    v_hat = v / (1.0 - ADAM_B2 ** ADAM_STEP)
    delta = -ADAM_LR * (m_hat / (jnp.sqrt(v_hat) + ADAM_EPS) + ADAM_WD * w)
    return delta, m, v


def _small_update(own, sib, chips, w, m, v):
    def body(o_ref, s_ref, c_ref, w_ref, m_ref, v_ref, g_out, d_out, m_out, v_out):
        chip = o_ref[...] + s_ref[...]
        g = (chip + c_ref[0]) + (c_ref[1] + c_ref[2])
        d, mn, vn = _adam_math(w_ref[...], g, m_ref[...], v_ref[...])
        g_out[...] = g
        d_out[...] = d
        m_out[...] = mn
        v_out[...] = vn

    return pl.pallas_call(body, name='small_update', out_shape=[jax.ShapeDtypeStruct(own.shape, f32)] * 4)(
        own, sib, chips, w, m, v)


def _small_chip_sum(own, sib):
    def body(o_ref, s_ref, out):
        out[...] = o_ref[...] + s_ref[...]
    return pl.pallas_call(body, name='small_chip_sum', out_shape=jax.ShapeDtypeStruct(own.shape, f32))(own, sib)


def _adamw(w, m, v, mine, theirs, name):
    layers, a, b = w.shape
    tr = _row_tile(a)

    def body(w_ref, m_ref, v_ref, p0, p1, q0, q1, g_out, d_out, m_out, v_out):
        first = pl.program_id(0) == 0
        g = jnp.where(first, p0[...].astype(f32) + q0[...].astype(f32), p1[...].astype(f32) + q1[...].astype(f32))
        d, mn, vn = _adam_math(w_ref[0], g, m_ref[0], v_ref[0])
        g_out[0] = g
        d_out[0] = d
        m_out[0] = mn
        v_out[0] = vn

    full = pl.BlockSpec((1, tr, b), lambda l, i: (l, i, 0))
    part = pl.BlockSpec((tr, b), lambda l, i: (i, 0))
    return pl.pallas_call(body, name=name, grid=(layers, a // tr), in_specs=[full] * 3 + [part] * 4, out_specs=[full] * 4,
                          out_shape=[jax.ShapeDtypeStruct(w.shape, f32)] * 4,
                          compiler_params=_cparams(('parallel', 'parallel')))(w, m, v, *mine, *theirs)


def _adamw_layer_in_the_middle(w, m, v, mine, theirs, name):
    a, layers, b = w.shape
    assert layers == 2 and b % LANES == 0

    def body(w_ref, m_ref, v_ref, p0, p1, q0, q1, g_out, d_out, m_out, v_out):
        g = jnp.stack([p0[...].astype(f32) + q0[...].astype(f32), p1[...].astype(f32) + q1[...].astype(f32)], axis=1)
        d, mn, vn = _adam_math(w_ref[...], g, m_ref[...], v_ref[...])
        g_out[...] = g
        d_out[...] = d
        m_out[...] = mn
        v_out[...] = vn

    full = pl.BlockSpec((a, layers, LANES), lambda i: (0, 0, i))
    part = pl.BlockSpec((a, LANES), lambda i: (0, i))
    return pl.pallas_call(body, name=name, grid=(b // LANES,), in_specs=[full] * 3 + [part] * 4, out_specs=[full] * 4,
                          out_shape=[jax.ShapeDtypeStruct(w.shape, f32)] * 4,
                          compiler_params=_cparams(('parallel',), big=True))(w, m, v, *mine, *theirs)


def _dg(a, b, ca, cb):
    return lax.dot_general(a.astype(bf16), b.astype(bf16), (((ca,), (cb,)), ((), ())), preferred_element_type=f32)


@jax.custom_vjp
def _bmm(a, b):
    return _dg(a, b, 1, 0)


_bmm.defvjp(lambda a, b: (_dg(a, b, 1, 0), (a, b)), lambda r, g: (_dg(g, r[1], 1, 1), _dg(r[0], g, 0, 0)))


@jax.custom_vjp
def _bmm_nt(a, b):
    return _dg(a, b, 1, 1)


_bmm_nt.defvjp(lambda a, b: (_dg(a, b, 1, 1), (a, b)), lambda r, g: (_dg(g, r[1], 1, 0), _dg(g, r[0], 0, 0)))


@jax.custom_vjp
def _bmm_tn(a, b):
    return _dg(a, b, 0, 0)


_bmm_tn.defvjp(lambda a, b: (_dg(a, b, 0, 0), (a, b)), lambda r, g: (_dg(r[1], g, 1, 1), _dg(r[0], g, 1, 0)))


def _hdot(a, b):
    return jnp.dot(a, b, precision=HI, preferred_element_type=f32)


def _hdot_nt(a, b):
    return lax.dot_general(a, b, (((1,), (1,)), ((), ())), precision=HI, preferred_element_type=f32)


def _hdot_tn(a, b):
    return lax.dot_general(a, b, (((0,), (0,)), ((), ())), precision=HI, preferred_element_type=f32)


def _head_mask(h, width=GROUP_W):
    lane = lax.broadcasted_iota(jnp.int32, (1, width), 1)
    return ((lane >= h * HEAD_DIM) & (lane < (h + 1) * HEAD_DIM)).astype(f32)


def _rope_perm():
    p = np.zeros((LANES, LANES), np.float32)
    half = MLA_ROPE // 2
    for i in range(half):
        p[MLA_NOPE + half + i, MLA_NOPE + i] = -1.0
        p[MLA_NOPE + i, MLA_NOPE + half + i] = 1.0
    return jnp.asarray(p)


def _rope_tables(S):
    half = MLA_ROPE // 2
    freqs = ROPE_THETA ** (-jnp.arange(half, dtype=f32) / half)
    ang = jnp.arange(S, dtype=f32)[:, None] * freqs[None, :]
    cos, sin = jnp.cos(ang), jnp.sin(ang)
    ones, zeros = jnp.ones((S, MLA_NOPE), f32), jnp.zeros((S, LANES - MLA_DQK), f32)
    c_tab = jnp.concatenate([ones, cos, cos, zeros], axis=1)
    s_tab = jnp.concatenate([jnp.zeros((S, MLA_NOPE), f32), sin, sin, zeros], axis=1)
    return c_tab, s_tab


def _f_mla_pre(c_q, c_kv, krope, c_tab, s_tab, q_norm, kv_norm, wq0, wq1, wq2, wq3, wk0, wk1, wk2, wk3, wv, gq, gk, perm):
    wq, wk = (wq0, wq1, wq2, wq3), (wk0, wk1, wk2, wk3)
    nq = _rms(c_q, q_norm)
    nkv = _rms(c_kv, kv_norm)

    def norm_rope(t, g):
        t = t * lax.rsqrt(jnp.sum(t * t, axis=-1, keepdims=True) * (1.0 / MLA_DQK) + EPS) * g
        return t * c_tab + _hdot(t, perm) * s_tab

    qs = [norm_rope(_bmm(nq, wq[h]), gq) * (MLA_DQK ** -0.5) for h in range(N_HEADS)]
    ks = [norm_rope(_bmm(nkv, wk[h]) + krope, gk) for h in range(N_HEADS)]
    return (*qs, *ks, _bmm(nkv, wv))


def _f_attn(qs, ks, v, q0):
    tq, S = qs[0].shape[0], ks[0].shape[0]
    qpos = q0 + lax.broadcasted_iota(jnp.int32, (tq, S), 0)
    kpos = lax.broadcasted_iota(jnp.int32, (tq, S), 1)
    keep = kpos <= qpos
    logits = [jnp.where(keep, _bmm_nt(qs[h], ks[h]), NEG_INF) for h in range(N_HEADS)]
    ps = [jnp.exp(lg - jnp.max(lg, axis=-1, keepdims=True)) for lg in logits]
    ps = [p / jnp.sum(p, axis=-1, keepdims=True) for p in ps]
    return sum(_bmm(p, v) * _head_mask(h) for h, p in enumerate(ps))


ATTN_PARTS = 4


def _mla_attn_fwd(qs, ks, v, name):
    S = v.shape[0]
    tq = 2 * Q_BLOCK if S % (2 * ATTN_PARTS * Q_BLOCK) == 0 else Q_BLOCK
    parts = ATTN_PARTS if S % (ATTN_PARTS * tq) == 0 else 1
    per = S // parts
    outs = []
    for p in range(parts):
        n_keys = (p + 1) * per
        first_block = p * (per // tq)

        def body(*refs, first_block=first_block):
            q_vals = [r[...] for r in refs[:4]]
            k_vals = [r[...] for r in refs[4:8]]
            refs[9][...] = _f_attn(q_vals, k_vals, refs[8][...], (first_block + pl.program_id(0)) * tq)

        qspec = pl.BlockSpec((tq, LANES), lambda i, fb=first_block: (fb + i, 0))
        outs.append(pl.pallas_call(
            body, name=f'{name}_{p}', grid=(per // tq,),
            in_specs=[qspec] * 4 + [pl.BlockSpec((n_keys, LANES), lambda i: (0, 0))] * 4 +
                     [pl.BlockSpec((n_keys, GROUP_W), lambda i: (0, 0))],
            out_specs=pl.BlockSpec((tq, GROUP_W), lambda i: (i, 0)),
            out_shape=jax.ShapeDtypeStruct((per, GROUP_W), f32),
            compiler_params=_cparams(('parallel',), big=True),
        )(*qs, *ks, v))
    return jnp.concatenate(outs, axis=0)


def _mla_attn_bwd(qs, ks, v, do, name):
    S = v.shape[0]
    tq = 2 * Q_BLOCK if S % (2 * ATTN_PARTS * Q_BLOCK) == 0 else Q_BLOCK
    parts = ATTN_PARTS if S % (ATTN_PARTS * tq) == 0 else 1
    per = S // parts
    dq_parts, dkv_sum = [], None
    for p in range(parts):
        n_keys = (p + 1) * per
        first_block = p * (per // tq)

        def body(*refs, first_block=first_block):
            q_vals = [r[...].astype(f32) for r in refs[:4]]
            k_vals = [r[...].astype(f32) for r in refs[4:8]]
            v_val = refs[8][...].astype(f32)
            q0 = (first_block + pl.program_id(0)) * tq
            _, vjp = jax.vjp(lambda a, b, c: _f_attn(a, b, c, q0), q_vals, k_vals, v_val)
            dqs, dks, dv = vjp(refs[9][...])
            outs = refs[10:]
            for h in range(N_HEADS):
                outs[h][...] = dqs[h]
            first = pl.program_id(0) == 0
            for o, g in zip(outs[4:], (*dks, dv)):
                @pl.when(first)
                def _(o=o, g=g):
                    o[...] = g

                @pl.when(jnp.logical_not(first))
                def _(o=o, g=g):
                    o[...] += g

        qspec = pl.BlockSpec((tq, LANES), lambda i, fb=first_block: (fb + i, 0))
        kspec = pl.BlockSpec((n_keys, LANES), lambda i: (0, 0))
        vspec = pl.BlockSpec((n_keys, GROUP_W), lambda i: (0, 0))
        res = pl.pallas_call(
            body, name=f'{name}_{p}', grid=(per // tq,),
            in_specs=[qspec] * 4 + [kspec] * 4 + [vspec, pl.BlockSpec((tq, GROUP_W), lambda i, fb=first_block: (fb + i, 0))],
            out_specs=[pl.BlockSpec((tq, LANES), lambda i: (i, 0))] * 4 + [kspec] * 4 + [vspec],
            out_shape=[jax.ShapeDtypeStruct((per, LANES), f32)] * 4 + [jax.ShapeDtypeStruct((n_keys, LANES), f32)] * 4 +
                      [jax.ShapeDtypeStruct((n_keys, GROUP_W), f32)],
            compiler_params=_cparams(('arbitrary',), big=True),
        )(*qs, *ks, v, do)
        dq_parts.append(res[:4])
        dkv = [jnp.pad(t, ((0, S - n_keys), (0, 0))) for t in res[4:]]
        dkv_sum = dkv if dkv_sum is None else [a_ + b_ for a_, b_ in zip(dkv_sum, dkv)]
    dqs = [jnp.concatenate([dq_parts[p][h] for p in range(parts)], axis=0) for h in range(N_HEADS)]
    return dqs, dkv_sum[:4], dkv_sum[4]


def _mla_params(mp):
    pad = LANES - MLA_DQK
    wq = jnp.pad(mp['mla_w_uq'].reshape(GROUP_W, N_HEADS, MLA_DQK).transpose(1, 0, 2), ((0, 0), (0, 0), (0, pad)))
    wkv = mp['mla_w_ukv'].reshape(LANES, N_HEADS, MLA_NOPE + HEAD_DIM)
    wk = jnp.pad(wkv[:, :, :MLA_NOPE].transpose(1, 0, 2), ((0, 0), (0, 0), (0, LANES - MLA_NOPE)))
    wv = wkv[:, :, MLA_NOPE:].reshape(LANES, GROUP_W)
    gq = jnp.pad(mp['mla_qk_q'], (0, pad))[None]
    gk = jnp.pad(mp['mla_qk_k'], (0, pad))[None]
    return [mp['mla_q_norm'][None], mp['mla_kv_norm'][None], *[wq[h] for h in range(N_HEADS)],
            *[wk[h] for h in range(N_HEADS)], wv, gq, gk, _rope_perm()]


def _mla_fwd(c_q, c_kv, k_rope, mp, l):
    S = c_q.shape[0]
    tm = _pick(S, (256, 128))
    krope = jnp.pad(k_rope, ((0, 0), (MLA_NOPE, LANES - MLA_DQK)))
    c_tab, s_tab = _rope_tables(S)
    tiled = [c_q, c_kv, krope, c_tab, s_tab]
    params = _mla_params(mp)
    res = _tile_fwd(_f_mla_pre, tiled, params, [(LANES, bf16)] * 8 + [(GROUP_W, bf16)], tm, f'mla_pre_fwd_{l}')
    qs, ks, v = res[:4], res[4:8], res[8]
    y = _mla_attn_fwd(qs, ks, v, f'mla_attn_fwd_{l}')
    return y, (tiled, params, qs, ks, v)


def _mla_bwd(dy, saved, l):
    tiled, params, qs, ks, v = saved
    S = dy.shape[0]
    tm = _pick(S, (256, 128))
    dqs, dks, dv = _mla_attn_bwd(qs, ks, v, dy, f'mla_attn_bwd_{l}')
    (dc_q, dc_kv, dkrope), dpar = _tile_bwd(_f_mla_pre, tiled, params, [*dqs, *dks, dv], [True, True, True, False, False],
                                            [True] * 13 + [False], tm, f'mla_pre_bwd_{l}')
    dqn, dkvn = dpar[0], dpar[1]
    dwq, dwk = jnp.stack(dpar[2:6]), jnp.stack(dpar[6:10])
    dwv, dgq, dgk = dpar[10:13]
    dw_uq = dwq[:, :, :MLA_DQK].transpose(1, 0, 2).reshape(GROUP_W, N_HEADS * MLA_DQK)
    dw_ukv = jnp.concatenate([dwk[:, :, :MLA_NOPE].transpose(1, 0, 2), dwv.reshape(LANES, N_HEADS, HEAD_DIM)],
                             axis=2).reshape(LANES, N_HEADS * (MLA_NOPE + HEAD_DIM))
    grads = {'mla_q_norm': dqn[0], 'mla_kv_norm': dkvn[0], 'mla_w_uq': dw_uq, 'mla_w_ukv': dw_ukv,
             'mla_qk_q': dgq[0, :MLA_DQK], 'mla_qk_k': dgk[0, :MLA_DQK]}
    return dc_q, dc_kv, dkrope[:, MLA_NOPE:MLA_DQK], grads


SPAN = 128


def _head_mean_matrix():
    h = np.arange(GROUP_W) // HEAD_DIM
    return jnp.asarray((h[:, None] == h[None, :]).astype(np.float32) / HEAD_DIM)


def _f_dil_pre(q, k, gq, gk, hm):
    qn = q * lax.rsqrt(_hdot(q * q, hm) + EPS) * gq * (HEAD_DIM ** -0.5)
    kn = k * lax.rsqrt(_hdot(k * k, hm) + EPS) * gk
    return qn, kn


def _f_dil_branch(qb, kp, kc, vp, vc, b0, b1, b2, b3, first):
    kcat = jnp.concatenate([kp, kc], axis=0)
    vcat = jnp.concatenate([vp, vc], axis=0)
    qi = lax.broadcasted_iota(jnp.int32, (SPAN, 2 * SPAN), 0) + SPAN
    kj = lax.broadcasted_iota(jnp.int32, (SPAN, 2 * SPAN), 1)
    delta = qi - kj
    valid = (delta >= 0) & (delta <= SPAN) & jnp.logical_not(first & (kj < SPAN))
    masks = [_head_mask(h) for h in range(N_HEADS)]
    raw = [_bmm_nt(qb * hm, kcat) for hm in masks]
    logits = [jnp.where(valid, r + bias, NEG_INF) for r, bias in zip(raw, (b0, b1, b2, b3))]
    ms = [jnp.max(lg, axis=-1, keepdims=True) for lg in logits]
    ps = [jnp.exp(lg - m) for lg, m in zip(logits, ms)]
    pvs = [_bmm(p, vcat) for p in ps]
    o = sum(pv * hm for pv, hm in zip(pvs, masks))
    m_full = sum(m * hm for m, hm in zip(ms, masks))
    l_full = sum(jnp.sum(p, axis=-1, keepdims=True) * hm for p, hm in zip(ps, masks))
    return o, m_full, l_full


def _dil_branch_specs(d, nb):
    cur = pl.BlockSpec((SPAN, GROUP_W), lambda r, n: (n, r))
    prev = pl.BlockSpec((SPAN, GROUP_W), lambda r, n: (jnp.maximum(n - 1, 0), r))
    bias = pl.BlockSpec((1, SPAN, 2 * SPAN), lambda r, n: (0, 0, 0))
    return cur, prev, bias


def _head_table_specs():
    return [pl.BlockSpec((1, SPAN, 2 * SPAN), lambda r, n, h=h: (h, 0, 0)) for h in range(N_HEADS)]


def _dil_branch_fwd(q, k, v, table, name):
    L, d = q.shape[0], q.shape[1] // GROUP_W
    nb = L // SPAN
    cur, prev, bias = _dil_branch_specs(d, nb)

    def body(q_ref, kp_ref, kc_ref, vp_ref, vc_ref, b0, b1, b2, b3, o_ref, m_ref, l_ref):
        o, m, l = _f_dil_branch(*[r[...].astype(f32) for r in (q_ref, kp_ref, kc_ref, vp_ref, vc_ref)], b0[0], b1[0], b2[0], b3[0],
                                pl.program_id(1) == 0)
        o_ref[...] = o
        m_ref[...] = m
        l_ref[...] = l

    return pl.pallas_call(
        body, name=name, grid=(d, nb), in_specs=[cur, prev, cur, prev, cur] + _head_table_specs(),
        out_specs=[cur] * 3, out_shape=[jax.ShapeDtypeStruct(q.shape, f32)] * 3,
        compiler_params=_cparams(('parallel', 'parallel')),
    )(q, k, k, v, v, *[table] * N_HEADS)


def _dil_branch_bwd(q, k, v, table, do, dm, dl, name):
    L, d = q.shape[0], q.shape[1] // GROUP_W
    nb = L // SPAN
    cur, prev, bias = _dil_branch_specs(d, nb)
    whole = pl.BlockSpec((L, GROUP_W), lambda r, n: (0, r))

    def body(q_ref, kp_ref, kc_ref, vp_ref, vc_ref, b0, b1, b2, b3, do_ref, dm_ref, dl_ref,
             dq_ref, dk_ref, dv_ref, db0, db1, db2, db3):
        r, n = pl.program_id(0), pl.program_id(1)
        first = n == 0
        _, vjp = jax.vjp(lambda *a: _f_dil_branch(*a, first), *[r[...].astype(f32) for r in (q_ref, kp_ref, kc_ref, vp_ref, vc_ref)],
                         b0[0], b1[0], b2[0], b3[0])
        dq, dkp, dkc, dvp, dvc, g0, g1, g2, g3 = vjp((do_ref[...], dm_ref[...], dl_ref[...]))
        dq_ref[...] = dq

        @pl.when(first)
        def _():
            dk_ref[...] = jnp.zeros_like(dk_ref)
            dv_ref[...] = jnp.zeros_like(dv_ref)

        rows = pl.ds(pl.multiple_of(n * SPAN, SPAN), SPAN)
        dk_ref[rows, :] += dkc
        dv_ref[rows, :] += dvc

        @pl.when(n > 0)
        def _():
            before = pl.ds(pl.multiple_of((n - 1) * SPAN, SPAN), SPAN)
            dk_ref[before, :] += dkp
            dv_ref[before, :] += dvp

        start = first & (r == 0)
        for o, g in zip((db0, db1, db2, db3), (g0, g1, g2, g3)):
            @pl.when(start)
            def _(o=o, g=g):
                o[0] = g

            @pl.when(jnp.logical_not(start))
            def _(o=o, g=g):
                o[0] += g

    res = pl.pallas_call(
        body, name=name, grid=(d, nb), in_specs=[cur, prev, cur, prev, cur] + _head_table_specs() + [cur] * 3,
        out_specs=[cur, whole, whole] + [bias] * 4,
        out_shape=[jax.ShapeDtypeStruct(q.shape, f32)] * 3 + [jax.ShapeDtypeStruct((1, SPAN, 2 * SPAN), f32)] * 4,
        compiler_params=_cparams(('arbitrary', 'arbitrary')),
    )(q, k, k, v, v, *[table] * N_HEADS, do, dm, dl)
    return res[0], res[1], res[2], res[3:]


def _f_dil_merge(o1, m1, l1, o2, m2, l2, o3, m3, l3):
    mx = jnp.maximum(jnp.maximum(m1, m2), m3)
    w1, w2, w3 = jnp.exp(m1 - mx), jnp.exp(m2 - mx), jnp.exp(m3 - mx)
    return ((w1 * o1 + w2 * o2 + w3 * o3) / (w1 * l1 + w2 * l2 + w3 * l3),)


def _bias_onehot(dilation):
    qi = jnp.arange(SPAN, dtype=jnp.int32)[:, None] + SPAN
    kj = jnp.arange(2 * SPAN, dtype=jnp.int32)[None, :]
    bucket = _t5_bucket(jnp.clip(qi - kj, 0, SPAN) * dilation).reshape(-1)
    return (bucket[None, :] == jnp.arange(T5_BUCKETS, dtype=jnp.int32)[:, None]).astype(f32)


def _bias_tables(t5_t, onehot, name):
    N = onehot.shape[1]
    tn = _pick(N, (4096, 2048, 1024))

    def body(t_ref, oh_ref, o_ref):
        o_ref[...] = _hdot(t_ref[...], oh_ref[...])

    return pl.pallas_call(
        body, name=name, grid=(N // tn,),
        in_specs=[pl.BlockSpec((8, T5_BUCKETS), lambda i: (0, 0)), pl.BlockSpec((T5_BUCKETS, tn), lambda i: (0, i))],
        out_specs=pl.BlockSpec((8, tn), lambda i: (0, i)), out_shape=jax.ShapeDtypeStruct((8, N), f32),
        compiler_params=_cparams(('parallel',)),
    )(t5_t, onehot)


def _bias_tables_bwd(d_tab, onehot, name):
    N = onehot.shape[1]
    tn = _pick(N, (4096, 2048, 1024))

    def body(g_ref, oh_ref, o_ref):
        part = _hdot_nt(g_ref[...], oh_ref[...])

        @pl.when(pl.program_id(0) == 0)
        def _():
            o_ref[...] = part

        @pl.when(pl.program_id(0) > 0)
        def _():
            o_ref[...] += part

    return pl.pallas_call(
        body, name=name, grid=(N // tn,),
        in_specs=[pl.BlockSpec((8, tn), lambda i: (0, i)), pl.BlockSpec((T5_BUCKETS, tn), lambda i: (0, i))],
        out_specs=pl.BlockSpec((8, T5_BUCKETS), lambda i: (0, 0)), out_shape=jax.ShapeDtypeStruct((8, T5_BUCKETS), f32),
        compiler_params=_cparams(('arbitrary',)),
    )(d_tab, onehot)


def _by_residue(t, d):
    S, C = t.shape
    return t.reshape(S // d, d * C)


def _from_residue(t):
    return t.reshape(-1, GROUP_W)


def _dil_tables(t5_bias):
    t5_t = jnp.pad(t5_bias.T, ((0, 8 - N_HEADS), (0, 0)))
    return [_bias_tables(t5_t, _bias_onehot(d), f'dil_bias_fwd_{bi}').reshape(8, SPAN, 2 * SPAN)
            for bi, (_, d) in enumerate(DIL_PAIRS)]


def _t5_grad(d_tables):
    total = None
    for bi, (_, d) in enumerate(DIL_PAIRS):
        g = _bias_tables_bwd(d_tables[bi], _bias_onehot(d), f'dil_bias_bwd_{bi}')
        total = g if total is None else total + g
    return total[:N_HEADS].T


def _dil_fwd(qkv, mp, l):
    S = qkv.shape[0]
    tm = _pick(S, (256, 128))
    q, k, v = qkv[:, :GROUP_W], qkv[:, GROUP_W:2 * GROUP_W], qkv[:, 2 * GROUP_W:]
    pre_params = [jnp.tile(mp['dil_q_norm'], N_HEADS)[None], jnp.tile(mp['dil_k_norm'], N_HEADS)[None], _head_mean_matrix()]
    qn, kn = _tile_fwd(_f_dil_pre, [q, k], pre_params, [(GROUP_W, bf16)] * 2, tm, f'dil_pre_fwd_{l}')
    v = v.astype(bf16)
    tables = mp['dil_tables'] if 'dil_tables' in mp else _dil_tables(mp['t5_bias'])
    branches, outs = [], []
    for bi, (_, d) in enumerate(DIL_PAIRS):
        tab = tables[bi]
        qd, kd, vd = _by_residue(qn, d), _by_residue(kn, d), _by_residue(v, d)
        o, m, lsum = _dil_branch_fwd(qd, kd, vd, tab, f'dil_branch_fwd_{l}_{bi}')
        branches.append((qd, kd, vd, tab))
        outs += [_from_residue(o), _from_residue(m), _from_residue(lsum)]
    (y,) = _tile_fwd(_f_dil_merge, outs, [], [(GROUP_W, f32)], tm, f'dil_merge_fwd_{l}')
    return y, (q, k, pre_params, branches, outs)


def _dil_bwd(dy, saved, l):
    q, k, pre_params, branches, outs = saved
    S = dy.shape[0]
    tm = _pick(S, (256, 128))
    douts, _ = _tile_bwd(_f_dil_merge, outs, [], [dy], [True] * 9, [], tm, f'dil_merge_bwd_{l}')
    dqn = dkn = dv = None
    d_tabs = []
    for bi, (_, d) in enumerate(DIL_PAIRS):
        qd, kd, vd, tab = branches[bi]
        do, dm, dl = [_by_residue(t, d) for t in douts[3 * bi:3 * bi + 3]]
        dq_b, dk_b, dv_b, dbias = _dil_branch_bwd(qd, kd, vd, tab, do, dm, dl, f'dil_branch_bwd_{l}_{bi}')
        d_tabs.append(jnp.concatenate([*dbias, jnp.zeros((8 - N_HEADS, SPAN, 2 * SPAN), f32)], axis=0).reshape(8, -1))
        dq_b, dk_b, dv_b = _from_residue(dq_b), _from_residue(dk_b), _from_residue(dv_b)
        dqn = dq_b if dqn is None else dqn + dq_b
        dkn = dk_b if dkn is None else dkn + dk_b
        dv = dv_b if dv is None else dv + dv_b
    (dq, dk), (dgq, dgk) = _tile_bwd(_f_dil_pre, [q, k], pre_params, [dqn, dkn], [True, True], [True, True, False], tm,
                                     f'dil_pre_bwd_{l}')
    grads = {'dil_q_norm': dgq.reshape(N_HEADS, HEAD_DIM).sum(0), 'dil_k_norm': dgk.reshape(N_HEADS, HEAD_DIM).sum(0),
             't5_tables': d_tabs}
    return jnp.concatenate([dq, dk, dv], axis=1), grads


S5_LANES = S5_G * S5_P
SCAN_SEGMENTS = 8
SCAN_W = 512


def _f_s5_prep(bre, bim, lr, li, logdt_col, expand):
    dt = jnp.sum(jnp.exp(logdt_col) * expand, axis=0, keepdims=True)
    mag = jnp.exp(lr * dt)
    ar, ai = mag * jnp.cos(li * dt), mag * jnp.sin(li * dt)
    den = lr * lr + li * li
    nr, ni = ar - 1.0, ai
    zr = (nr * lr + ni * li) / den
    zi = (ni * lr - nr * li) / den
    bb = jnp.concatenate([zr * bre - zi * bim, zr * bim + zi * bre], axis=1)
    a_rows = jnp.broadcast_to(jnp.concatenate([ar, ai], axis=1), bb.shape)
    return bb, a_rows


def _s5_scan(x, a_rows, name, reverse=False, h=None):
    S = x.shape[0]
    NL = x.shape[1] // 2
    T = S // SCAN_SEGMENTS
    nblk = NL // SCAN_W
    n_in = 4 if reverse else 2

    def body(*refs):
        if reverse:
            (x_hbm, pr_hbm, pi_hbm, ar_ref, ai_ref, hr_hbm, hi_hbm, dar_ref, dai_ref,
             xr_s, xi_s, pr_s, pi_s, hr_s, hi_s, in_sems, out_sems) = refs
        else:
            x_hbm, ar_ref, ai_ref, hr_hbm, hi_hbm, xr_s, xi_s, hr_s, hi_s, in_sems, out_sems = refs
        col = pl.multiple_of(pl.program_id(0) * SCAN_W, SCAN_W)
        loads = []
        for k in range(SCAN_SEGMENTS):
            rows = pl.ds(k * T, T)
            sources = [(x_hbm, col, xr_s), (x_hbm, NL + col, xi_s)]
            if reverse:
                sources += [(pr_hbm, col, pr_s), (pi_hbm, col, pi_s)]
            for i, (src, c0, dst) in enumerate(sources):
                loads.append(pltpu.make_async_copy(src.at[rows, pl.ds(c0, SCAN_W)], dst.at[:, k, :],
                                                   in_sems.at[i * SCAN_SEGMENTS + k]))
        for cp in loads:
            cp.start()
        for cp in loads:
            cp.wait()
        ar = ar_ref[...]
        ai = -ai_ref[...] if reverse else ai_ref[...]
        zero = jnp.zeros((SCAN_SEGMENTS, SCAN_W), f32)

        def at(s):
            return T - 1 - s if reverse else s

        def local(s, c):
            hr, hi, pr, pi = c
            j = at(s)
            nhr = ar * hr - ai * hi + xr_s[j]
            nhi = ar * hi + ai * hr + xi_s[j]
            hr_s[j] = nhr
            hi_s[j] = nhi
            return nhr, nhi, ar * pr - ai * pi, ar * pi + ai * pr

        er, ei, pr, pi = lax.fori_loop(0, T, local, (zero, zero, zero + 1.0, zero), unroll=4)
        row = lax.broadcasted_iota(jnp.int32, (SCAN_SEGMENTS, SCAN_W), 0)
        cr, ci = zero, zero
        order = range(SCAN_SEGMENTS - 2, -1, -1) if reverse else range(1, SCAN_SEGMENTS)
        for k in order:
            src = k + 1 if reverse else k - 1
            tr = er + pr * cr - pi * ci
            ti = ei + pr * ci + pi * cr
            cr = jnp.where(row == k, jnp.sum(jnp.where(row == src, tr, 0.0), axis=0, keepdims=True), cr)
            ci = jnp.where(row == k, jnp.sum(jnp.where(row == src, ti, 0.0), axis=0, keepdims=True), ci)

        def fix_at(j, c, before):
            pr, pi, sr, si = c
            pr, pi = ar * pr - ai * pi, ar * pi + ai * pr
            hr = hr_s[j] + pr * cr - pi * ci
            hi = hi_s[j] + pr * ci + pi * cr
            hr_s[j] = hr
            hi_s[j] = hi
            if reverse:
                qr, qi = before
                sr = sr + hr * qr + hi * qi
                si = si + hi * qr - hr * qi
            return pr, pi, sr, si

        start = (zero + 1.0, zero, zero, zero)
        if reverse:
            def fix(s, c):
                j = T - 1 - s
                return fix_at(j, c, (pr_s[j - 1], pi_s[j - 1]))

            c = lax.fori_loop(0, T - 1, fix, start, unroll=4)
            last_r = jnp.where(row == 0, 0.0, pltpu.roll(pr_s[T - 1], 1, 0))
            last_i = jnp.where(row == 0, 0.0, pltpu.roll(pi_s[T - 1], 1, 0))
            _, _, sr, si = fix_at(0, c, (last_r, last_i))
            dar_ref[...] = sr
            dai_ref[...] = si
        else:
            lax.fori_loop(0, T, lambda s, c: fix_at(s, c, None), start, unroll=4)
        stores = []
        for k in range(SCAN_SEGMENTS):
            rows = pl.ds(k * T, T)
            stores.append(pltpu.make_async_copy(hr_s.at[:, k, :], hr_hbm.at[rows, pl.ds(col, SCAN_W)], out_sems.at[k]))
            stores.append(pltpu.make_async_copy(hi_s.at[:, k, :], hi_hbm.at[rows, pl.ds(col, SCAN_W)],
                                                out_sems.at[SCAN_SEGMENTS + k]))
        for cp in stores:
            cp.start()
        for cp in stores:
            cp.wait()

    a_re = pl.BlockSpec((SCAN_SEGMENTS, SCAN_W), lambda b: (0, b))
    a_im = pl.BlockSpec((SCAN_SEGMENTS, SCAN_W), lambda b: (0, nblk + b))
    seq = pltpu.VMEM((T, SCAN_SEGMENTS, SCAN_W), f32)
    if reverse:
        in_specs, args = [ANY, ANY, ANY, a_re, a_im], [x, h[0], h[1], a_rows, a_rows]
        out_specs = [ANY, ANY, a_re, a_re]
        out_shape = [jax.ShapeDtypeStruct((S, NL), f32)] * 2 + [jax.ShapeDtypeStruct((SCAN_SEGMENTS, NL), f32)] * 2
    else:
        in_specs, args = [ANY, a_re, a_im], [x, a_rows, a_rows]
        out_specs = [ANY, ANY]
        out_shape = [jax.ShapeDtypeStruct((S, NL), f32)] * 2
    scratch = [seq] * (n_in + 2) + [pltpu.SemaphoreType.DMA((n_in * SCAN_SEGMENTS,)),
                                    pltpu.SemaphoreType.DMA((2 * SCAN_SEGMENTS,))]
    return pl.pallas_call(body, name=name, grid=(nblk,), in_specs=in_specs, out_specs=out_specs, out_shape=out_shape,
                          scratch_shapes=scratch, compiler_params=_cparams(('arbitrary',), big=True))(*args)


def _f_s5_post(y, u, d, w_glu):
    z = _bmm(y + d * u, w_glu)
    return (z[:, :GROUP_W] * jax.nn.sigmoid(z[:, GROUP_W:]),)


def _block_diag(t):
    G, a, b = t.shape
    eye = jnp.eye(G, dtype=t.dtype)
    return (t[:, :, None, :] * eye[:, None, :, None]).reshape(G * a, G * b)


def _diag_blocks(m, a, b):
    G = m.shape[0] // a
    return jnp.moveaxis(jnp.diagonal(m.reshape(G, a, G, b), axis1=0, axis2=2), -1, 0)


def _s5_fwd(u, mp, l):
    S = u.shape[0]
    tm = _pick(S, (256, 128))
    bre = _block_diag(mp['s5_b_re'].transpose(0, 2, 1))
    bim = _block_diag(mp['s5_b_im'].transpose(0, 2, 1))
    expand = jnp.repeat(jnp.eye(S5_G, dtype=f32), S5_P, axis=1)
    prep_params = [mp['s5_lambda_re'].reshape(1, S5_LANES), mp['s5_lambda_im'].reshape(1, S5_LANES),
                   mp['s5_log_dt'].reshape(S5_G, 1), expand]
    bb, a_rows = _tile_fwd(_f_s5_prep, [bre, bim], prep_params, [(2 * S5_LANES, f32)] * 2, GROUP_W, f's5_prep_fwd_{l}')
    x = _mm(u, bb, 'nn', f's5_in_fwd_{l}')
    hr, hi = _s5_scan(x, a_rows, f's5_scan_fwd_{l}')
    c_re, c_im = _block_diag(mp['s5_c_re'].transpose(0, 2, 1)), -_block_diag(mp['s5_c_im'].transpose(0, 2, 1))
    y = _mm(hi, c_im, 'nn', f's5_out_im_fwd_{l}', add=_mm(hr, c_re, 'nn', f's5_out_re_fwd_{l}'))
    post_params = [mp['s5_d'][None], mp['s5_w_glu']]
    (out,) = _tile_fwd(_f_s5_post, [y, u], post_params, [(GROUP_W, f32)], tm, f's5_post_fwd_{l}')
    return out, (u, bre, bim, prep_params, bb, a_rows, hr, hi, c_re, c_im, y, post_params)


def _s5_bwd(dout, saved, l):
    u, bre, bim, prep_params, bb, a_rows, hr, hi, c_re, c_im, y, post_params = saved
    S = u.shape[0]
    tm = _pick(S, (256, 128))
    (dy, du1), (dd, dwglu) = _tile_bwd(_f_s5_post, [y, u], post_params, [dout], [True, True], [True, True], tm,
                                       f's5_post_bwd_{l}')
    ccat = jnp.concatenate([c_re, c_im], axis=0)
    dh = _mm(dy, ccat, 'nt', f's5_out_dx_{l}')
    dccat = jnp.concatenate([_mm(hr, dy, 'tn', f's5_out_re_dw_{l}'), _mm(hi, dy, 'tn', f's5_out_im_dw_{l}')], axis=0)
    lr_, li_, dar, dai = _s5_scan(dh, a_rows, f's5_scan_bwd_{l}', reverse=True, h=(hr, hi))
    du2 = _mm(li_, bb[:, S5_LANES:], 'nt', f's5_in_im_dx_{l}', add=_mm(lr_, bb[:, :S5_LANES], 'nt', f's5_in_re_dx_{l}'))
    dbb = jnp.concatenate([_mm(u, lr_, 'tn', f's5_in_re_dw_{l}'), _mm(u, li_, 'tn', f's5_in_im_dw_{l}')], axis=1)
    da_rows = jnp.pad(jnp.concatenate([dar, dai], axis=1), ((0, GROUP_W - SCAN_SEGMENTS), (0, 0)))
    (dbre, dbim), (dlr, dli, dlogdt) = _tile_bwd(_f_s5_prep, [bre, bim], prep_params, [dbb, da_rows], [True, True],
                                                 [True, True, True, False], GROUP_W, f's5_prep_bwd_{l}')
    grads = {
        's5_lambda_re': dlr.reshape(S5_G, S5_P), 's5_lambda_im': dli.reshape(S5_G, S5_P), 's5_log_dt': dlogdt[:, 0],
        's5_b_re': _diag_blocks(dbre, S5_CG, S5_P).transpose(0, 2, 1),
        's5_b_im': _diag_blocks(dbim, S5_CG, S5_P).transpose(0, 2, 1),
        's5_c_re': _diag_blocks(dccat[:S5_LANES], S5_P, S5_CG).transpose(0, 2, 1),
        's5_c_im': -_diag_blocks(dccat[S5_LANES:], S5_P, S5_CG).transpose(0, 2, 1),
        's5_d': dd[0], 's5_w_glu': dwglu}
    return du1 + du2, grads


DN_CONV = 4


def _head_sum_matrix():
    h = np.arange(GROUP_W) // HEAD_DIM
    return jnp.asarray((h[:, None] == h[None, :]).astype(np.float32))


def _f_dn_pre(x0, x1, x2, x3, ab, w0, w1, w2, w3, alog, dtb, ea, eb, hs):
    c = w0 * x0 + w1 * x1 + w2 * x2 + w3 * x3
    s = c * jax.nn.sigmoid(c)
    q, k, v = s[:, :GROUP_W], s[:, GROUP_W:2 * GROUP_W], s[:, 2 * GROUP_W:]
    q = q * lax.rsqrt(_hdot(q * q, hs) + EPS) * (HEAD_DIM ** -0.5)
    k = k * lax.rsqrt(_hdot(k * k, hs) + EPS)
    beta = jax.nn.sigmoid(_hdot(ab, eb))
    g = -jnp.exp(alog) * jax.nn.softplus(_hdot(ab, ea) + dtb)
    return q, k, v, g, beta


DN_CHUNKS_PER_STEP = 8


def _f_dn_chunks(q, k, v, g, beta):
    C = DN_CHUNK
    n_chunks = q.shape[0] // C
    r = lax.broadcasted_iota(jnp.int32, (C, C), 0)
    c = lax.broadcasted_iota(jnp.int32, (C, C), 1)
    causal, strict = r >= c, r > c
    eye = (r == c).astype(f32)
    tril = causal.astype(f32)
    ones = jnp.ones((C, GROUP_W), f32)
    masks = [_head_mask(h) for h in range(N_HEADS)]
    rows = [tuple(t[i * C:(i + 1) * C] for t in (q, k, v, g, beta)) for i in range(n_chunks)]
    gcs = [_hdot(tril, gi) for (_, _, _, gi, _) in rows]
    items = [(i, h) for i in range(n_chunks) for h in range(N_HEADS)]
    grows = [_hdot_nt(ones * (masks[h] * (1.0 / HEAD_DIM)), gcs[i]) for i, h in items]
    decs = []
    for (i, h), grow in zip(items, grows):
        gcol = jnp.sum(gcs[i] * masks[h], axis=1, keepdims=True) * (1.0 / HEAD_DIM)
        decs.append(jnp.exp(jnp.where(causal, gcol - grow, NEG_INF)))
    kbs = [ki * bi for (_, ki, _, _, bi) in rows]
    kks = [_bmm_nt(kbs[i] * masks[h], rows[i][1]) for i, h in items]
    qks = [_bmm_nt(rows[i][0] * masks[h], rows[i][1]) for i, h in items]
    lmats = [jnp.where(strict, kk * dec, 0.0) for kk, dec in zip(kks, decs)]
    a_qk = [jnp.where(causal, qk * dec, 0.0) for qk, dec in zip(qks, decs)]
    ts = [eye - lm for lm in lmats]
    ps = lmats
    for _ in range(5):
        ps = [_bmm(p, p) for p in ps]
        ts = [t + _bmm(t, p) for t, p in zip(ts, ps)]
    egs = [jnp.exp(gc) for gc in gcs]
    tw = [_bmm(t, kbs[i] * egs[i]) for (i, h), t in zip(items, ts)]
    tu = [_bmm(t, rows[i][2] * rows[i][4]) for (i, h), t in zip(items, ts)]
    outs = []
    for i in range(n_chunks):
        qi, ki, _, gi, _ = rows[i]
        glast = jnp.sum(gi, axis=0, keepdims=True)
        w = sum(tw[i * N_HEADS + h] * masks[h] for h in range(N_HEADS))
        u = sum(tu[i * N_HEADS + h] * masks[h] for h in range(N_HEADS))
        outs.append((w, u, qi * egs[i], ki * jnp.exp(glast - gcs[i]), *a_qk[i * N_HEADS:(i + 1) * N_HEADS],
                     jnp.broadcast_to(jnp.exp(glast), (C, GROUP_W))))
    return tuple(jnp.concatenate(parts, axis=0) for parts in zip(*outs))


def _f_dn_step(w, u, qd, kdec, a0, a1, a2, a3, dfull, state, bd):
    row0 = (lax.broadcasted_iota(jnp.int32, dfull.shape, 0) == 0).astype(f32)
    dvec = jnp.sum(dfull * row0, axis=0, keepdims=True)
    ws, qs = _bmm(w, state), _bmm(qd, state)
    vnew = u - ws
    avs = [_bmm(a, vnew) for a in (a0, a1, a2, a3)]
    kv = _bmm_tn(kdec, vnew)
    o = qs + sum(av * _head_mask(h) for h, av in enumerate(avs))
    return o, state * dvec + bd * kv


def _dn_scan_fwd(ins, name):
    S = ins[0].shape[0]
    N = S // DN_CHUNK
    bd = _head_sum_matrix()

    def body(*refs):
        o_ref, s_ref, state = refs[10], refs[11], refs[12]

        @pl.when(pl.program_id(0) == 0)
        def _():
            state[...] = jnp.zeros_like(state)

        s_in = state[...]
        s_ref[0] = s_in
        o, s_out = _f_dn_step(*[r[...] for r in refs[:9]], s_in, refs[9][...])
        o_ref[...] = o
        state[...] = s_out

    return pl.pallas_call(
        body, name=name, grid=(N,),
        in_specs=[pl.BlockSpec((DN_CHUNK, t.shape[1]), lambda n: (n, 0)) for t in ins] + [_full_spec(bd)],
        out_specs=[pl.BlockSpec((DN_CHUNK, GROUP_W), lambda n: (n, 0)), pl.BlockSpec((1, GROUP_W, GROUP_W), lambda n: (n, 0, 0))],
        out_shape=[jax.ShapeDtypeStruct((S, GROUP_W), f32), jax.ShapeDtypeStruct((N, GROUP_W, GROUP_W), f32)],
        scratch_shapes=[pltpu.VMEM((GROUP_W, GROUP_W), f32)],
        compiler_params=_cparams(('arbitrary',)),
    )(*ins, bd)


def _dn_scan_bwd(ins, states, do, name):
    S = ins[0].shape[0]
    N = S // DN_CHUNK
    bd = _head_sum_matrix()

    def body(*refs):
        s_ref, do_ref = refs[9], refs[10]
        bd_ref = refs[11]
        outs = refs[12:21]
        dstate = refs[21]

        @pl.when(pl.program_id(0) == 0)
        def _():
            dstate[...] = jnp.zeros_like(dstate)

        bd_val = bd_ref[...]
        _, vjp = jax.vjp(lambda *a: _f_dn_step(*a, bd_val), *[r[...] for r in refs[:9]], s_ref[0])
        grads = vjp((do_ref[...], dstate[...]))
        for o, g in zip(outs, grads[:9]):
            o[...] = g
        dstate[...] = grads[9]

    def rev(n):
        return (N - 1 - n, 0)

    res = pl.pallas_call(
        body, name=name, grid=(N,),
        in_specs=[pl.BlockSpec((DN_CHUNK, t.shape[1]), rev) for t in ins] +
                 [pl.BlockSpec((1, GROUP_W, GROUP_W), lambda n: (N - 1 - n, 0, 0)), pl.BlockSpec((DN_CHUNK, GROUP_W), rev),
                  _full_spec(bd)],
        out_specs=[pl.BlockSpec((DN_CHUNK, t.shape[1]), rev) for t in ins],
        out_shape=[jax.ShapeDtypeStruct(t.shape, f32) for t in ins],
        scratch_shapes=[pltpu.VMEM((GROUP_W, GROUP_W), f32)],
        compiler_params=_cparams(('arbitrary',)),
    )(*ins, states, do, bd)
    return list(res)


def _f_dn_post(o, gate, gain, hmean):
    return (o * lax.rsqrt(_hdot(o * o, hmean) + EPS) * gain * (gate * jax.nn.sigmoid(gate)),)


def _dn_delays(x, name):
    S, C = x.shape
    tm = _pick(S, (256, 128))

    def body(prev_ref, cur_ref, *outs):
        before = jnp.where(pl.program_id(0) > 0, prev_ref[...], 0.0)
        both = jnp.concatenate([before, cur_ref[...]], axis=0)
        for o, k in zip(outs, range(DN_CONV - 1, 0, -1)):
            o[...] = pltpu.roll(both, k, 0)[tm:]

    spec = pl.BlockSpec((tm, C), lambda i: (i, 0))
    return pl.pallas_call(
        body, name=name, grid=(S // tm,),
        in_specs=[pl.BlockSpec((tm, C), lambda i: (jnp.maximum(i - 1, 0), 0)), spec],
        out_specs=[spec] * (DN_CONV - 1), out_shape=[jax.ShapeDtypeStruct((S, C), x.dtype)] * (DN_CONV - 1),
        compiler_params=_cparams(('parallel',), big=True),
    )(x, x)


def _dn_undelay_sum(ds, name):
    S, C = ds[0].shape
    tm = _pick(S, (256, 128))
    n = S // tm

    def body(*refs):
        o = refs[-1]
        total = refs[2 * (DN_CONV - 1)][...]
        for j in range(DN_CONV - 1):
            k = DN_CONV - 1 - j
            after = jnp.where(pl.program_id(0) < n - 1, refs[2 * j + 1][...], 0.0)
            both = jnp.concatenate([refs[2 * j][...], after], axis=0)
            total = total + pltpu.roll(both, 2 * tm - k, 0)[:tm]
        o[...] = total

    spec = pl.BlockSpec((tm, C), lambda i: (i, 0))
    nxt = pl.BlockSpec((tm, C), lambda i: (jnp.minimum(i + 1, n - 1), 0))
    args, in_specs = [], []
    for j in range(DN_CONV - 1):
        args += [ds[j], ds[j]]
        in_specs += [spec, nxt]
    return pl.pallas_call(
        body, name=name, grid=(n,), in_specs=in_specs + [spec], out_specs=spec,
        out_shape=jax.ShapeDtypeStruct((S, C), f32), compiler_params=_cparams(('parallel',), big=True),
    )(*args, ds[DN_CONV - 1])


def _dn_fwd(qkv, a, b, gate, mp, l):
    S = qkv.shape[0]
    tm = _pick(S, (256, 128))
    xs = [*_dn_delays(qkv, f'dn_delay_{l}'), qkv]
    ab = jnp.pad(jnp.concatenate([a, b], axis=1), ((0, 0), (0, LANES - 2 * N_HEADS)))
    sel = np.zeros((2, LANES, GROUP_W), np.float32)
    for h in range(N_HEADS):
        sel[0, h, h * HEAD_DIM:(h + 1) * HEAD_DIM] = 1.0
        sel[1, N_HEADS + h, h * HEAD_DIM:(h + 1) * HEAD_DIM] = 1.0
    pre_params = [*[mp['dn_conv'][j][None] for j in range(DN_CONV)], jnp.repeat(mp['dn_a_log'], HEAD_DIM)[None],
                  jnp.repeat(mp['dn_dt_bias'], HEAD_DIM)[None], jnp.asarray(sel[0]), jnp.asarray(sel[1]), _head_sum_matrix()]
    pre = _tile_fwd(_f_dn_pre, [*xs, ab], pre_params, [(GROUP_W, f32)] * 5, tm, f'dn_pre_fwd_{l}')
    chunk_outs = [(GROUP_W, f32)] * 4 + [(HEAD_DIM, f32)] * 4 + [(GROUP_W, f32)]
    parts = _tile_fwd(_f_dn_chunks, pre, [], chunk_outs, DN_CHUNK * DN_CHUNKS_PER_STEP, f'dn_chunk_fwd_{l}')
    o, states = _dn_scan_fwd(parts, f'dn_scan_fwd_{l}')
    post_params = [jnp.tile(mp['dn_o_norm'], N_HEADS)[None], _head_mean_matrix()]
    (y,) = _tile_fwd(_f_dn_post, [o, gate], post_params, [(GROUP_W, f32)], tm, f'dn_post_fwd_{l}')
    return y, (xs, ab, pre_params, pre, parts, states, o, gate, post_params)


def _dn_bwd(dy, saved, l):
    xs, ab, pre_params, pre, parts, states, o, gate, post_params = saved
    S = dy.shape[0]
    tm = _pick(S, (256, 128))
    (do, dgate), (dgain,) = _tile_bwd(_f_dn_post, [o, gate], post_params, [dy], [True, True], [True, False], tm,
                                      f'dn_post_bwd_{l}')
    dparts = _dn_scan_bwd(parts, states, do, f'dn_scan_bwd_{l}')
    dpre, _ = _tile_bwd(_f_dn_chunks, pre, [], dparts, [True] * 5, [], DN_CHUNK * DN_CHUNKS_PER_STEP, f'dn_chunk_bwd_{l}')
    dins, dpar = _tile_bwd(_f_dn_pre, [*xs, ab], pre_params, dpre, [True] * 5, [True] * 6 + [False] * 3, tm,
                           f'dn_pre_bwd_{l}')
    dqkv = _dn_undelay_sum(dins[:DN_CONV], f'dn_undelay_{l}')
    dab = dins[DN_CONV]
    grads = {'dn_conv': jnp.concatenate(dpar[:DN_CONV], axis=0),
             'dn_a_log': dpar[4].reshape(N_HEADS, HEAD_DIM).sum(1), 'dn_dt_bias': dpar[5].reshape(N_HEADS, HEAD_DIM).sum(1),
             'dn_o_norm': dgain.reshape(N_HEADS, HEAD_DIM).sum(0)}
    return dqkv, dab[:, :N_HEADS], dab[:, N_HEADS:2 * N_HEADS], dgate, grads


def _t5_bucket(dist):
    exact = T5_BUCKETS // 2
    df = jnp.maximum(dist, 1).astype(f32)
    large = exact + (jnp.log(df / exact) / math.log(T5_MAX_DIST / exact) * (T5_BUCKETS - exact)).astype(jnp.int32)
    large = jnp.minimum(large, T5_BUCKETS - 1)
    return jnp.where(dist < exact, dist, large)


def _split_cols(t, sizes):
    out, start = [], 0
    for s in sizes:
        out.append(t[..., start:start + s])
        start += s
    return out


def _mixers_fwd(proj, mp, l):
    c_q, c_kv, k_rope, u_s5, qkv_dil, qkv_dn, a_dn, b_dn, gate_dn = _split_cols(proj, IN_SPLITS)
    y_mla, s_mla = _mla_fwd(c_q, c_kv, k_rope, mp, l)
    y_s5, s_s5 = _s5_fwd(u_s5, mp, l)
    y_dil, s_dil = _dil_fwd(qkv_dil, mp, l)
    y_dn, s_dn = _dn_fwd(qkv_dn, a_dn, b_dn, gate_dn, mp, l)
    return jnp.concatenate([y_mla, y_s5, y_dil, y_dn], axis=-1), (s_mla, s_s5, s_dil, s_dn)


def _mixers_bwd(dmixed, saved, l):
    s_mla, s_s5, s_dil, s_dn = saved
    d_mla, d_s5, d_dil, d_dn = _split_cols(dmixed, (GROUP_W,) * 4)
    dc_q, dc_kv, dk_rope, g_mla = _mla_bwd(d_mla, s_mla, l)
    du, g_s5 = _s5_bwd(d_s5, s_s5, l)
    dqkv_dil, g_dil = _dil_bwd(d_dil, s_dil, l)
    dqkv_dn, da, db, dgate, g_dn = _dn_bwd(d_dn, s_dn, l)
    parts = [dc_q, dc_kv, dk_rope, du, dqkv_dil, dqkv_dn, da, db, dgate]
    dproj = jnp.concatenate([p.astype(bf16) for p in parts], axis=-1)
    return dproj, {**g_mla, **g_s5, **g_dil, **g_dn}


MIXER_PARAMS = ['mla_q_norm', 'mla_kv_norm', 'mla_w_uq', 'mla_w_ukv', 'mla_qk_q', 'mla_qk_k', 's5_lambda_re',
                's5_lambda_im', 's5_log_dt', 's5_b_re', 's5_b_im', 's5_c_re', 's5_c_im', 's5_d', 's5_w_glu',
                'dil_q_norm', 'dil_k_norm', 't5_bias', 'dn_conv', 'dn_a_log', 'dn_dt_bias', 'dn_o_norm']


def _layer_fwd_mix(h, W, l):
    S = h.shape[0]
    tm = _pick(S, (256, 128))
    g1 = W['attn_norm'][l][None]
    (n1,) = _tile_fwd(_f_rms, [h], [g1], [(D_MODEL, bf16)], tm, f'rms1_fwd_{l}')
    proj = _mm(n1, W['w_in'][l], 'nt', f'proj_fwd_{l}')
    mp = {k: (W[k] if k == 't5_bias' else W[k][l]).astype(f32) for k in MIXER_PARAMS}
    if 'dil_tables' in W:
        mp['dil_tables'] = W['dil_tables']
    mixed, mix_saved = _mixers_fwd(proj, mp, l)
    mixed_b = mixed.astype(bf16)
    h2 = _mm(mixed_b, W['w_out'][l], 'nn', f'out_fwd_{l}', add=h)
    return h2, dict(h=h, n1=n1, mix=mix_saved, mixed=mixed_b, h2=h2)


def _layer_fwd_ffn(h2, W, l, saved):
    S = h2.shape[0]
    tm = _pick(S, (256, 128))
    g2 = W['ffn_norm'][l][None]
    (n2,) = _tile_fwd(_f_rms, [h2], [g2], [(D_MODEL, bf16)], tm, f'rms2_fwd_{l}')
    u = _mm(n2, W['ffn_w1'][l], 'nt', f'ffn1_fwd_{l}', out_dtype=bf16)
    v = _mm(n2, W['ffn_w3'][l], 'nt', f'ffn3_fwd_{l}', out_dtype=bf16)
    (act,) = _tile_fwd(_f_swiglu, [u, v], [], [(FFN_HIDDEN, bf16)], tm, f'swiglu_fwd_{l}')
    h3 = _mm(act, W['ffn_w2'][l], 'nn', f'ffn2_fwd_{l}', add=h2)
    saved.update(n2=n2, u=u, v=v, act=act)
    return h3


def _layer_bwd_ffn(dh3, saved, W, l):
    S = dh3.shape[0]
    tm = _pick(S, (256, 128))
    g2 = W['ffn_norm'][l][None]
    grads = {}
    dact = _mm(dh3, W['ffn_w2'][l], 'nt', f'ffn2_dx_{l}', out_dtype=bf16)
    grads['ffn_w2'] = _mm(saved['act'], dh3, 'tn', f'ffn2_dw_{l}', out_dtype=bf16)
    (du, dv), _ = _tile_bwd(_f_swiglu, [saved['u'], saved['v']], [], [dact], [True, True], [], tm, f'swiglu_bwd_{l}',
                            dt_dtypes=[bf16, bf16])
    dn2 = _mm(dv, W['ffn_w3'][l], 'nn', f'ffn3_dx_{l}', add=_mm(du, W['ffn_w1'][l], 'nn', f'ffn1_dx_{l}'))
    grads['ffn_w1'] = _mm(du, saved['n2'], 'tn', f'ffn1_dw_{l}', out_dtype=bf16)
    grads['ffn_w3'] = _mm(dv, saved['n2'], 'tn', f'ffn3_dw_{l}', out_dtype=bf16)
    (dh2n,), (dg2,) = _tile_bwd(_f_rms, [saved['h2']], [g2], [dn2], [True], [True], tm, f'rms2_bwd_{l}')
    grads['ffn_norm'] = dg2[0]
    return (dh3, dh2n), grads


def _layer_bwd_mix(dh2, saved, W, l):
    S = dh2.shape[0]
    tm = _pick(S, (256, 128))
    g1 = W['attn_norm'][l][None]
    grads = {}
    dmixed = _mm(dh2, W['w_out'][l], 'nt', f'out_dx_{l}')
    grads['w_out'] = _mm(saved['mixed'], dh2, 'tn', f'out_dw_{l}', out_dtype=bf16)
    dproj, dmp = _mixers_bwd(dmixed, saved['mix'], l)
    grads.update(dmp)
    dn1 = _mm(dproj, W['w_in'][l], 'nn', f'proj_dx_{l}')
    grads['w_in'] = _mm(dproj, saved['n1'], 'tn', f'proj_dw_{l}', out_dtype=bf16)
    (dh1n,), (dg1,) = _tile_bwd(_f_rms, [saved['h']], [g1], [dn1], [True], [True], tm, f'rms1_bwd_{l}')
    grads['attn_norm'] = dg1[0]
    return (dh2, dh1n), grads


def kernel(x, attn_norm, w_in, w_out, mla_q_norm, mla_kv_norm, mla_w_uq, mla_w_ukv, mla_qk_q, mla_qk_k, s5_lambda_re, s5_lambda_im, s5_log_dt, s5_b_re, s5_b_im, s5_c_re, s5_c_im, s5_d, s5_w_glu, dil_q_norm, dil_k_norm, t5_bias, dn_conv, dn_a_log, dn_dt_bias, dn_o_norm, ffn_norm, ffn_w1, ffn_w3, ffn_w2, loss_target, m_attn_norm, m_w_in, m_w_out, m_mla_q_norm, m_mla_kv_norm, m_mla_w_uq, m_mla_w_ukv, m_mla_qk_q, m_mla_qk_k, m_s5_lambda_re, m_s5_lambda_im, m_s5_log_dt, m_s5_b_re, m_s5_b_im, m_s5_c_re, m_s5_c_im, m_s5_d, m_s5_w_glu, m_dil_q_norm, m_dil_k_norm, m_t5_bias, m_dn_conv, m_dn_a_log, m_dn_dt_bias, m_dn_o_norm, m_ffn_norm, m_ffn_w1, m_ffn_w3, m_ffn_w2, v_attn_norm, v_w_in, v_w_out, v_mla_q_norm, v_mla_kv_norm, v_mla_w_uq, v_mla_w_ukv, v_mla_qk_q, v_mla_qk_k, v_s5_lambda_re, v_s5_lambda_im, v_s5_log_dt, v_s5_b_re, v_s5_b_im, v_s5_c_re, v_s5_c_im, v_s5_d, v_s5_w_glu, v_dil_q_norm, v_dil_k_norm, v_t5_bias, v_dn_conv, v_dn_a_log, v_dn_dt_bias, v_dn_o_norm, v_ffn_norm, v_ffn_w1, v_ffn_w3, v_ffn_w2):
    given = dict(locals())
    def seen(n, t):
        if n in COLUMNS_FIRST:
            return jnp.transpose(t, (2, 0, 1))
        return jnp.swapaxes(t, 1, 2) if n in TRANSPOSED else t

    def given_back(n, t):
        return jnp.transpose(t, (1, 2, 0)) if n in COLUMNS_FIRST else seen(n, t)

    def layer_of(n, t, l):
        return t[:, l] if n in COLUMNS_FIRST else t[l]

    w_loc = {n: seen(n, given[n]) for n in WEIGHTS}
    m_loc = {n: seen(n, given['m_' + n]) for n in WEIGHTS}
    v_loc = {n: seen(n, given['v_' + n]) for n in WEIGHTS}
    big_names = list(BIG)

    own = 2 * lax.axis_index('x') + lax.axis_index('y')
    groups = [[(n, 0) for n in GATHER_FIRST], [(n, 0) for n in GATHER_FFN], [(n, 1) for n in big_names]]
    started, order = [], jnp.zeros((8, LANES), f32)
    for gi, group in enumerate(groups):
        blocks = [layer_of(n, w_loc[n], l).astype(bf16) for n, l in group]
        lands = [lax.empty((N_SHARDS,) + b.shape, bf16) for b in blocks]
        send_sems, recv_sems, blocks, lands, order = _to_chips_start(blocks, lands, False, order, f'gather_start_{gi}')
        started.append((send_sems, recv_sems, blocks, lands))
    W = {n: [None] * DEPTH for n in big_names}
    for n in SMALL:
        W[n] = w_loc[n]
    W['dil_tables'] = _dil_tables(w_loc['t5_bias'])

    def arrive(gi, after):
        send_sems, recv_sems, blocks, lands = started[gi]
        blocks, lands = _to_chips_wait(send_sems, recv_sems, blocks, lands, False, after, f'gather_wait_{gi}')
        for (n, l), block, land in zip(groups[gi], blocks, lands):
            W[n][l] = _from_shards(n, lax.dynamic_update_slice(land, block[None], (own, 0, 0)))

    arrive(0, order)
    h = x[0]
    saved = []
    for l in range(DEPTH):
        h2, sv = _layer_fwd_mix(h, W, l)
        if l == 0:
            arrive(1, h2)
        h = _layer_fwd_ffn(h2, W, l, sv)
        if l == 0:
            arrive(2, h)
        saved.append(sv)
    parts_loss, dh = _loss_head(h, loss_target[0])
    local_loss = jnp.sum(parts_loss)

    layer_grads = [dict() for _ in range(DEPTH)]
    sent = []

    def send(group, tag):
        srcs = [_by_shard(n, layer_grads[l][n]).astype(bf16) for n, l in group]
        lands = [lax.empty((3,) + s.shape[1:], bf16) for s in srcs]
        send_sems, recv_sems, srcs, lands, token = _to_chips_start(srcs, lands, True, jnp.zeros((8, LANES), f32),
                                                                   f'reduce_start_{tag}')
        sent.append((group, tag, send_sems, recv_sems, srcs, lands))
        return token[0, 0]

    for l in reversed(range(DEPTH)):
        (dh3, dh2n), g_ffn = _layer_bwd_ffn(dh, saved[l], W, l)
        layer_grads[l].update(g_ffn)
        dh2 = dh3 + dh2n
        if l == 0:
            dh2 = dh2 + send([(n, 0) for n in GATHER_FFN], 'ffn0')
        (dh2, dh1n), g_mix = _layer_bwd_mix(dh2, saved[l], W, l)
        layer_grads[l].update(g_mix)
        dh = dh2 + dh1n
        if l == 1:
            dh = dh + send([(n, 1) for n in big_names], 'layer1')
    last = send([(n, 0) for n in GATHER_FIRST], 'first0')
    grad_x = dh[None]
    small_full = []
    for n in SMALL:
        if n == 't5_bias':
            small_full.append(_t5_grad([a_ + b_ for a_, b_ in zip(layer_grads[0]['t5_tables'], layer_grads[1]['t5_tables'])]))
        else:
            small_full.append(jnp.stack([layer_grads[l][n] for l in range(DEPTH)]))

    small_shapes = [w_loc[n].shape for n in SMALL] + [(1,)]
    nothing = [jnp.zeros((1,), f32)]
    small_pack = _pack(small_full + [local_loss.reshape(1)]) + last
    _, recv_small = _swap_with_sibling([], small_pack)
    chip_small = _small_chip_sum(small_pack, recv_small)
    _, from_chips_small = _exchange_between_chips([], chip_small)

    mine = {}
    for group, tag, send_sems, recv_sems, srcs, lands in sent:
        srcs, lands = _to_chips_wait(send_sems, recv_sems, srcs, lands, True, from_chips_small, f'reduce_wait_{tag}')
        for (n, l), src, land in zip(group, srcs, lands):
            mine[(n, l)] = _partial_sum(src, land, f'partial_{n}_{l}')
    keys = [(n, l) for n in big_names for l in range(DEPTH)]
    theirs = dict(zip(keys, _swap_partials([mine[k] for k in keys])))

    g_small_p, d_small_p, m_small_p, v_small_p = _small_update(
        small_pack, recv_small, from_chips_small, _pack([w_loc[n] for n in SMALL] + nothing),
        _pack([m_loc[n] for n in SMALL] + nothing), _pack([v_loc[n] for n in SMALL] + nothing))
    loss = _unpack(g_small_p, small_shapes)[-1][0]
    grad, delta, new_m, new_v = {}, {}, {}, {}
    for n, g_, d_, m_, v_ in zip(SMALL, _unpack(g_small_p, small_shapes), _unpack(d_small_p, small_shapes),
                                 _unpack(m_small_p, small_shapes), _unpack(v_small_p, small_shapes)):
        grad[n], delta[n], new_m[n], new_v[n] = g_, d_, m_, v_
    for n in big_names:
        update = _adamw_layer_in_the_middle if n in COLUMNS_FIRST else _adamw
        results = update(w_loc[n], m_loc[n], v_loc[n], [mine[(n, l)] for l in range(DEPTH)],
                         [theirs[(n, l)] for l in range(DEPTH)], 'adamw_' + n)
        grad[n], delta[n], new_m[n], new_v[n] = [given_back(n, t) for t in results]
    return (loss, grad_x, *[grad[n] for n in WEIGHTS], *[delta[n] for n in WEIGHTS],
            *[new_m[n] for n in WEIGHTS], *[new_v[n] for n in WEIGHTS])
```
